```python
import jax, jax.numpy as jnp
from jax import lax
import numpy as np

D_MODEL = 2048
BATCH = 4
SEQ = 2048
DEPTH = 2
DEC_BATCH = 32
DEC_SEQ = 8
PAST_LEN = 16384
PAGE_SIZE = 128

N_MEM = 256
MEM_HEADS = 4
MEM_WIDTH = D_MODEL // 4
MEM_HEAD_DIM = MEM_WIDTH // MEM_HEADS
TOKEN_WIDTH = D_MODEL - MEM_WIDTH
CONV_WIDTH = 3
WINDOW = 128
HEAD_DIM = 64
N_HEADS = TOKEN_WIDTH // HEAD_DIM
N_KV_HEADS = 4
GROUP = N_HEADS // N_KV_HEADS
KV_WIDTH = N_KV_HEADS * HEAD_DIM
D_FF = ((8 * D_MODEL + 3 * 256 - 1) // (3 * 256)) * 256
N_CONV_LAYERS = (DEPTH + 1) // 2
N_ATTN_LAYERS = DEPTH // 2
EPS = 1e-6

kernel_name = "hybrid_conv_swa_sink_memxattn_decoder_step"


def rmsnorm(x, g):
    xf = x.astype(jnp.float32)
    r = lax.rsqrt(jnp.mean(xf * xf, axis=-1, keepdims=True) + EPS)
    return (xf * r).astype(x.dtype) * g


def swiglu(h, w_gate, w_up, w_down):
    return (jax.nn.silu(h @ w_gate) * (h @ w_up)) @ w_down


def cross_attention(qm, mk, mv):
    n, t = qm.shape[:2]
    q = qm.reshape(n, t, MEM_HEADS, MEM_HEAD_DIM)
    s = jnp.einsum('nqhd,nmhd->nhqm', q, mk).astype(jnp.float32) * (MEM_HEAD_DIM ** -0.5)
    p = jax.nn.softmax(s, axis=-1).astype(mv.dtype)
    o = jnp.einsum('nhqm,nmhd->nqhd', p, mv)
    return o.reshape(n, t, MEM_WIDTH)


def band_attention(q, kk, vv, sinks, key_valid):
    n, nq = q.shape[:2]
    nk = kk.shape[1]
    qg = q.reshape(n, nq, N_KV_HEADS, GROUP, HEAD_DIM)
    s = jnp.einsum('nqkgd,njkd->nkgqj', qg, kk).astype(jnp.float32) * (HEAD_DIM ** -0.5)
    qi = jnp.arange(nq)[:, None]
    kj = jnp.arange(nk)[None, :]
    band = (kj > qi) & (kj <= qi + WINDOW)
    mask = band[None, None, None] & key_valid[:, None, None, None, :]
    s = jnp.where(mask, s, -jnp.inf)
    sink = sinks.astype(jnp.float32).reshape(N_KV_HEADS, GROUP, 1, 1)
    m = jnp.maximum(jnp.max(s, axis=-1, keepdims=True), sink)
    e = jnp.exp(s - m)
    p = e / (jnp.sum(e, axis=-1, keepdims=True) + jnp.exp(sink - m))
    o = jnp.einsum('nkgqj,njkd->nqkgd', p.astype(vv.dtype), vv)
    return o.reshape(n, nq, TOKEN_WIDTH)


def conv_mixer(h, conv_prev, mk, mv, w_in, w_conv, w_out):
    t = h.shape[1]
    z = h @ w_in
    b = z[..., :TOKEN_WIDTH]
    c = z[..., TOKEN_WIDTH:2 * TOKEN_WIDTH]
    u = z[..., 2 * TOKEN_WIDTH:3 * TOKEN_WIDTH]
    qm = z[..., 3 * TOKEN_WIDTH:]
    ext = jnp.concatenate([conv_prev, c * u], axis=1)
    conv = sum(w_conv[k] * ext[:, k:k + t] for k in range(CONV_WIDTH))
    tok = b * conv
    out = jnp.concatenate([tok, cross_attention(qm, mk, mv)], axis=-1) @ w_out
    return out, ext[:, -(CONV_WIDTH - 1):]


def attn_project(h, w_in):
    n, t = h.shape[:2]
    z = h @ w_in
    q = z[..., :TOKEN_WIDTH].reshape(n, t, N_HEADS, HEAD_DIM)
    k = z[..., TOKEN_WIDTH:TOKEN_WIDTH + KV_WIDTH].reshape(n, t, N_KV_HEADS, HEAD_DIM)
    v = z[..., TOKEN_WIDTH + KV_WIDTH:TOKEN_WIDTH + 2 * KV_WIDTH].reshape(n, t, N_KV_HEADS, HEAD_DIM)
    qm = z[..., TOKEN_WIDTH + 2 * KV_WIDTH:]
    return q, k, v, qm


def swa_prompt(q, k, v, sinks):
    b, s = q.shape[:2]
    nb = s // WINDOW
    kb = k.reshape(b, nb, WINDOW, N_KV_HEADS, HEAD_DIM)
    vb = v.reshape(b, nb, WINDOW, N_KV_HEADS, HEAD_DIM)
    kk = jnp.concatenate([jnp.concatenate([jnp.zeros_like(kb[:, :1]), kb[:, :-1]], axis=1), kb], axis=2)
    vv = jnp.concatenate([jnp.concatenate([jnp.zeros_like(vb[:, :1]), vb[:, :-1]], axis=1), vb], axis=2)
    valid = (jnp.arange(nb)[:, None] > 0) | (jnp.arange(2 * WINDOW)[None, :] >= WINDOW)
    valid = jnp.broadcast_to(valid, (b, nb, 2 * WINDOW)).reshape(b * nb, 2 * WINDOW)
    o = band_attention(q.reshape(b * nb, WINDOW, N_HEADS, HEAD_DIM),
                       kk.reshape(b * nb, 2 * WINDOW, N_KV_HEADS, HEAD_DIM),
                       vv.reshape(b * nb, 2 * WINDOW, N_KV_HEADS, HEAD_DIM), sinks, valid)
    return o.reshape(b, s, TOKEN_WIDTH)


def setup_inputs(seed: int = 0) -> dict:
    key = jax.random.key(seed)
    ks = jax.random.split(key, 24)
    f32 = jnp.float32
    nrm = lambda k, shape, scale: jax.random.normal(k, shape, f32) * scale
    gain = lambda k, shape: 1.0 + 0.01 * jax.random.normal(k, shape, f32)
    d = D_MODEL
    return {
        "x_prompt": nrm(ks[0], (BATCH, SEQ, d), 1.0),
        "x_sample": nrm(ks[1], (DEC_BATCH, DEC_SEQ, d), 1.0),
        "mem_prompt": nrm(ks[2], (BATCH, N_MEM, d), 1.0),
        "state_conv": nrm(ks[3], (N_CONV_LAYERS, DEC_BATCH, CONV_WIDTH - 1, TOKEN_WIDTH), 1.0),
        "cache_win_k": nrm(ks[4], (N_ATTN_LAYERS, DEC_BATCH, WINDOW, N_KV_HEADS, HEAD_DIM), 1.0),
        "cache_win_v": nrm(ks[5], (N_ATTN_LAYERS, DEC_BATCH, WINDOW, N_KV_HEADS, HEAD_DIM), 1.0),
        "cache_mem_k": nrm(ks[6], (DEPTH, DEC_BATCH, N_MEM, MEM_HEADS, MEM_HEAD_DIM), 1.0),
        "cache_mem_v": nrm(ks[7], (DEPTH, DEC_BATCH, N_MEM, MEM_HEADS, MEM_HEAD_DIM), 1.0),
        "norm_mix": gain(ks[8], (DEPTH, d)),
        "norm_mem": gain(ks[9], (DEPTH, d)),
        "w_mem_kv": nrm(ks[10], (DEPTH, d, 2 * MEM_WIDTH), d ** -0.5),
        "norm_ffn": gain(ks[11], (DEPTH, d)),
        "w_gate": nrm(ks[12], (DEPTH, d, D_FF), d ** -0.5),
        "w_up": nrm(ks[13], (DEPTH, d, D_FF), d ** -0.5),
        "w_down": nrm(ks[14], (DEPTH, D_FF, d), D_FF ** -0.5),
        "conv_w_in": nrm(ks[15], (N_CONV_LAYERS, d, 3 * TOKEN_WIDTH + MEM_WIDTH), d ** -0.5),
        "conv_w": nrm(ks[16], (N_CONV_LAYERS, CONV_WIDTH, TOKEN_WIDTH), CONV_WIDTH ** -0.5),
        "conv_w_out": nrm(ks[17], (N_CONV_LAYERS, TOKEN_WIDTH + MEM_WIDTH, d), (TOKEN_WIDTH + MEM_WIDTH) ** -0.5),
        "attn_w_in": nrm(ks[18], (N_ATTN_LAYERS, d, TOKEN_WIDTH + 2 * KV_WIDTH + MEM_WIDTH), d ** -0.5),
        "attn_sinks": nrm(ks[19], (N_ATTN_LAYERS, N_HEADS), 0.5),
        "attn_w_out": nrm(ks[20], (N_ATTN_LAYERS, TOKEN_WIDTH + MEM_WIDTH, d), (TOKEN_WIDTH + MEM_WIDTH) ** -0.5),
        "norm_final": gain(ks[21], (d,)),
    }


def reference(x_prompt, x_sample, mem_prompt, state_conv, cache_win_k, cache_win_v, cache_mem_k, cache_mem_v,
              norm_mix, norm_mem, w_mem_kv, norm_ffn, w_gate, w_up, w_down,
              conv_w_in, conv_w, conv_w_out, attn_w_in, attn_sinks, attn_w_out, norm_final):
    xp, xs = x_prompt, x_sample
    bp, dbs = xp.shape[0], xs.shape[0]
    conv_p, conv_s, wk_p, wv_p, wk_s, wv_s, mk_p, mv_p = [], [], [], [], [], [], [], []
    for i in range(DEPTH):
        mkv = rmsnorm(mem_prompt, norm_mem[i]) @ w_mem_kv[i]
        mk = mkv[..., :MEM_WIDTH].reshape(bp, N_MEM, MEM_HEADS, MEM_HEAD_DIM)
        mv = mkv[..., MEM_WIDTH:].reshape(bp, N_MEM, MEM_HEADS, MEM_HEAD_DIM)
        mk_p.append(mk)
        mv_p.append(mv)
        hp = rmsnorm(xp, norm_mix[i])
        hs = rmsnorm(xs, norm_mix[i])
        j = i // 2
        if i % 2 == 0:
            zero_prev = jnp.zeros((bp, CONV_WIDTH - 1, TOKEN_WIDTH), hp.dtype)
            op, st_p = conv_mixer(hp, zero_prev, mk, mv, conv_w_in[j], conv_w[j], conv_w_out[j])
            os_, st_s = conv_mixer(hs, state_conv[j], cache_mem_k[i], cache_mem_v[i],
                                   conv_w_in[j], conv_w[j], conv_w_out[j])
            conv_p.append(st_p)
            conv_s.append(st_s)
        else:
            q, k, v, qm = attn_project(hp, attn_w_in[j])
            a = swa_prompt(q, k, v, attn_sinks[j])
            op = jnp.concatenate([a, cross_attention(qm, mk, mv)], axis=-1) @ attn_w_out[j]
            wk_p.append(k[:, -WINDOW:])
            wv_p.append(v[:, -WINDOW:])
            q, k, v, qm = attn_project(hs, attn_w_in[j])
            kk = jnp.concatenate([cache_win_k[j], k], axis=1)
            vv = jnp.concatenate([cache_win_v[j], v], axis=1)
            valid = jnp.ones((dbs, kk.shape[1]), dtype=bool)
            a = band_attention(q, kk, vv, attn_sinks[j], valid)
            os_ = jnp.concatenate([a, cross_attention(qm, cache_mem_k[i], cache_mem_v[i])], axis=-1) @ attn_w_out[j]
            wk_s.append(kk[:, -WINDOW:])
            wv_s.append(vv[:, -WINDOW:])
        xp = xp + op
        xs = xs + os_
        xp = xp + swiglu(rmsnorm(xp, norm_ffn[i]), w_gate[i], w_up[i], w_down[i])
        xs = xs + swiglu(rmsnorm(xs, norm_ffn[i]), w_gate[i], w_up[i], w_down[i])
    y_prompt = rmsnorm(xp, norm_final)
    y_sample = rmsnorm(xs, norm_final)
    new_conv_prompt = jnp.stack(conv_p)
    new_conv_sample = jnp.stack(conv_s)
    new_win_k_prompt = jnp.stack(wk_p)
    new_win_v_prompt = jnp.stack(wv_p)
    new_win_k_sample = jnp.stack(wk_s)
    new_win_v_sample = jnp.stack(wv_s)
    new_mem_k_prompt = jnp.stack(mk_p)
    new_mem_v_prompt = jnp.stack(mv_p)
    return (y_prompt, y_sample, new_conv_prompt, new_conv_sample, new_win_k_prompt, new_win_v_prompt,
            new_win_k_sample, new_win_v_sample, new_mem_k_prompt, new_mem_v_prompt)
```

```python
import functools

import jax
import jax.numpy as jnp
from jax import lax
from jax.experimental import pallas as pl
from jax.experimental.pallas import tpu as pltpu

F32 = jnp.float32
BF16 = jnp.bfloat16

D_MODEL = 2048
N_MEM = 256
MEM_HEADS = 4
MEM_WIDTH = D_MODEL // 4
MEM_HEAD_DIM = MEM_WIDTH // MEM_HEADS
TOKEN_WIDTH = D_MODEL - MEM_WIDTH
CONV_WIDTH = 3
WINDOW = 128
HEAD_DIM = 64
N_HEADS = TOKEN_WIDTH // HEAD_DIM
N_KV_HEADS = 4
GROUP = N_HEADS // N_KV_HEADS
KV_WIDTH = N_KV_HEADS * HEAD_DIM
EPS = 1e-6

V7X_VMEM_BYTES = 64 * 1024 * 1024
V7X_SUBLANES = 8

ROW_TILE = 1056
COL_TILE = 512
CONV_ROWS = 512
SAMPLE_GROUP = 8
CARRY_ROWS = V7X_SUBLANES


def _vmem_limit(block_bytes, scratch_bytes):
    need = 2 * block_bytes + scratch_bytes
    return int(min(need + max(need // 2, 8 << 20), V7X_VMEM_BYTES - (6 << 20)))


def _nbytes(shape, dtype):
    n = 1
    for s in shape:
        n *= s
    return n * jnp.dtype(dtype).itemsize


def _rmsnorm(x, g):
    r = lax.rsqrt(jnp.mean(x * x, axis=-1, keepdims=True) + EPS)
    return (x * r) * g


def _softmax_rows(s):
    m = jnp.max(s, axis=-1, keepdims=True)
    e = jnp.exp(s - m)
    return e / jnp.sum(e, axis=-1, keepdims=True)


def _dot_nt(a, b):
    return lax.dot_general(a, b, (((1,), (1,)), ((), ())), preferred_element_type=F32)


def _norm_matmul_kernel(x_ref, g_ref, w_ref, o_ref, h_ref):
    @pl.when(pl.program_id(1) == 0)
    def _():
        h_ref[...] = _rmsnorm(x_ref[...], g_ref[...]).astype(BF16)

    o_ref[...] = jnp.dot(h_ref[...], w_ref[...], preferred_element_type=F32).astype(o_ref.dtype)


def _norm_matmul(x, g, w, out_dtype):
    rows, d = x.shape
    n = w.shape[1]
    tm, tn = ROW_TILE, COL_TILE
    blocks = (_nbytes((tm, d), F32) + _nbytes((d, tn), BF16) + _nbytes((tm, tn), out_dtype))
    return pl.pallas_call(
        _norm_matmul_kernel,
        grid=(rows // tm, n // tn),
        in_specs=[
            pl.BlockSpec((tm, d), lambda i, j: (i, 0)),
            pl.BlockSpec((1, d), lambda i, j: (0, 0)),
            pl.BlockSpec((d, tn), lambda i, j: (0, j)),
        ],
        out_specs=pl.BlockSpec((tm, tn), lambda i, j: (i, j)),
        out_shape=jax.ShapeDtypeStruct((rows, n), out_dtype),
        scratch_shapes=[pltpu.VMEM((tm, d), BF16)],
        compiler_params=pltpu.CompilerParams(
            dimension_semantics=("arbitrary", "arbitrary"),
            vmem_limit_bytes=_vmem_limit(blocks, _nbytes((tm, d), BF16))),
        name="norm_matmul",
    )(x, g.reshape(1, d), w)


def _mem_kv_kernel(x_ref, g_ref, w_ref, k_ref, v_ref):
    h = _rmsnorm(x_ref[...], g_ref[...]).astype(BF16)
    kv = jnp.dot(h, w_ref[...], preferred_element_type=F32)
    k_ref[...] = kv[:, :MEM_WIDTH]
    v_ref[...] = kv[:, MEM_WIDTH:]


def _mem_kv(mem, g, w):
    rows, d = mem.shape
    depth = w.shape[0]
    tm = 512
    out = jax.ShapeDtypeStruct((depth, rows, MEM_WIDTH), F32)
    blocks = (_nbytes((tm, d), F32) + _nbytes((d, 2 * MEM_WIDTH), BF16) + 2 * _nbytes((tm, MEM_WIDTH), F32))
    return pl.pallas_call(
        _mem_kv_kernel,
        grid=(depth, rows // tm),
        in_specs=[
            pl.BlockSpec((tm, d), lambda l, i: (i, 0)),
            pl.BlockSpec((None, 1, d), lambda l, i: (l, 0, 0)),
            pl.BlockSpec((None, d, 2 * MEM_WIDTH), lambda l, i: (l, 0, 0)),
        ],
        out_specs=[
            pl.BlockSpec((None, tm, MEM_WIDTH), lambda l, i: (l, i, 0)),
            pl.BlockSpec((None, tm, MEM_WIDTH), lambda l, i: (l, i, 0)),
        ],
        out_shape=[out, out],
        compiler_params=pltpu.CompilerParams(
            dimension_semantics=("arbitrary", "arbitrary"),
            vmem_limit_bytes=_vmem_limit(blocks, 0)),
        name="mem_kv",
    )(mem, g.reshape(depth, 1, d), w)


def _matmul_residual_kernel(a_ref, w_ref, x_ref, o_ref):
    o_ref[...] = x_ref[...] + jnp.dot(a_ref[...], w_ref[...], preferred_element_type=F32)


def _matmul_residual(a, w, x):
    rows, k = a.shape
    n = w.shape[1]
    tm, tn = ROW_TILE, COL_TILE
    blocks = (_nbytes((tm, k), BF16) + _nbytes((k, tn), BF16) + 2 * _nbytes((tm, tn), F32))
    return pl.pallas_call(
        _matmul_residual_kernel,
        grid=(rows // tm, n // tn),
        in_specs=[
            pl.BlockSpec((tm, k), lambda i, j: (i, 0)),
            pl.BlockSpec((k, tn), lambda i, j: (0, j)),
            pl.BlockSpec((tm, tn), lambda i, j: (i, j)),
        ],
        out_specs=pl.BlockSpec((tm, tn), lambda i, j: (i, j)),
        out_shape=jax.ShapeDtypeStruct((rows, n), F32),
        compiler_params=pltpu.CompilerParams(
            dimension_semantics=("arbitrary", "arbitrary"),
            vmem_limit_bytes=_vmem_limit(blocks, 0)),
        name="matmul_residual",
    )(a, w, x)


def _ffn_up_kernel(x_ref, g_ref, wg_ref, wu_ref, o_ref, h_ref):
    @pl.when(pl.program_id(1) == 0)
    def _():
        h_ref[...] = _rmsnorm(x_ref[...], g_ref[...]).astype(BF16)

    h = h_ref[...]
    gate = jnp.dot(h, wg_ref[...], preferred_element_type=F32)
    up = jnp.dot(h, wu_ref[...], preferred_element_type=F32)
    o_ref[...] = (gate * jax.nn.sigmoid(gate) * up).astype(o_ref.dtype)


def _ffn_up(x, g, wg, wu):
    rows, d = x.shape
    n = wg.shape[1]
    tm, tn = ROW_TILE, COL_TILE
    blocks = (_nbytes((tm, d), F32) + 2 * _nbytes((d, tn), BF16) + _nbytes((tm, tn), BF16))
    return pl.pallas_call(
        _ffn_up_kernel,
        grid=(rows // tm, n // tn),
        in_specs=[
            pl.BlockSpec((tm, d), lambda i, j: (i, 0)),
            pl.BlockSpec((1, d), lambda i, j: (0, 0)),
            pl.BlockSpec((d, tn), lambda i, j: (0, j)),
            pl.BlockSpec((d, tn), lambda i, j: (0, j)),
        ],
        out_specs=pl.BlockSpec((tm, tn), lambda i, j: (i, j)),
        out_shape=jax.ShapeDtypeStruct((rows, n), BF16),
        scratch_shapes=[pltpu.VMEM((tm, d), BF16)],
        compiler_params=pltpu.CompilerParams(
            dimension_semantics=("arbitrary", "arbitrary"),
            vmem_limit_bytes=_vmem_limit(blocks, _nbytes((tm, d), BF16))),
        name="ffn_up",
    )(x, g.reshape(1, d), wg, wu)


def _cross_attention_head(q, k, v):
    s = _dot_nt(q, k) * (MEM_HEAD_DIM ** -0.5)
    p = _softmax_rows(s).astype(BF16)
    return jnp.dot(p, v, preferred_element_type=F32)


def _conv_prompt_kernel(z_ref, mk_ref, mv_ref, cw_ref, ms_ref, mix_ref, st_ref, ext_ref, *,
                        tiles_per_seq, n_prompt_tiles, sample_rows):
    s = pl.program_id(0)
    tq = z_ref.shape[0]

    @pl.when(s < n_prompt_tiles)
    def _():
        @pl.when(s % tiles_per_seq == 0)
        def _():
            ext_ref[0:CARRY_ROWS, :] = jnp.zeros((CARRY_ROWS, TOKEN_WIDTH), F32)

        c = z_ref[:, TOKEN_WIDTH:2 * TOKEN_WIDTH].astype(F32)
        u = z_ref[:, 2 * TOKEN_WIDTH:3 * TOKEN_WIDTH].astype(F32)
        cu = c * u
        ext_ref[CARRY_ROWS:CARRY_ROWS + tq, :] = cu
        conv = (cw_ref[0:1, :] * ext_ref[CARRY_ROWS - 2:CARRY_ROWS - 2 + tq, :]
                + cw_ref[1:2, :] * ext_ref[CARRY_ROWS - 1:CARRY_ROWS - 1 + tq, :]
                + cw_ref[2:3, :] * cu)
        b = z_ref[:, 0:TOKEN_WIDTH].astype(F32)
        mix_ref[:, 0:TOKEN_WIDTH] = (b * conv).astype(BF16)
        st_ref[...] = ext_ref[CARRY_ROWS + tq - 2:CARRY_ROWS + tq, :]
        ext_ref[0:CARRY_ROWS, :] = ext_ref[tq:tq + CARRY_ROWS, :]

        for h in range(MEM_HEADS):
            lo, hi = h * MEM_HEAD_DIM, (h + 1) * MEM_HEAD_DIM
            q = z_ref[:, 3 * TOKEN_WIDTH + lo:3 * TOKEN_WIDTH + hi]
            o = _cross_attention_head(q, mk_ref[:, lo:hi].astype(BF16), mv_ref[:, lo:hi].astype(BF16))
            mix_ref[:, TOKEN_WIDTH + lo:TOKEN_WIDTH + hi] = o.astype(BF16)

    @pl.when(s >= n_prompt_tiles)
    def _():
        mix_ref[0:sample_rows, :] = ms_ref[...]


def _conv_prompt(z, mk, mv, conv_w, mix_sample, *, batch, seq):
    rows, zc = z.shape
    tq = CONV_ROWS
    tiles_per_seq = seq // tq
    n_prompt_tiles = batch * tiles_per_seq
    sample_rows = mix_sample.shape[0]
    last = n_prompt_tiles - 1
    kernel = functools.partial(_conv_prompt_kernel, tiles_per_seq=tiles_per_seq,
                               n_prompt_tiles=n_prompt_tiles, sample_rows=sample_rows)
    blocks = (_nbytes((tq, zc), BF16) + 2 * _nbytes((N_MEM, MEM_WIDTH), F32)
              + _nbytes((sample_rows, D_MODEL), BF16) + _nbytes((tq, D_MODEL), BF16))
    scratch = _nbytes((tq + CARRY_ROWS, TOKEN_WIDTH), F32)
    return pl.pallas_call(
        kernel,
        grid=(n_prompt_tiles + 1,),
        in_specs=[
            pl.BlockSpec((tq, zc), lambda s: (jnp.minimum(s, last), 0)),
            pl.BlockSpec((None, N_MEM, MEM_WIDTH), lambda s: (0, jnp.minimum(s, last) // tiles_per_seq, 0)),
            pl.BlockSpec((None, N_MEM, MEM_WIDTH), lambda s: (0, jnp.minimum(s, last) // tiles_per_seq, 0)),
            pl.BlockSpec((CONV_WIDTH, TOKEN_WIDTH), lambda s: (0, 0)),
            pl.BlockSpec((sample_rows, D_MODEL), lambda s: (0, 0)),
        ],
        out_specs=[
            pl.BlockSpec((tq, D_MODEL), lambda s: (s, 0)),
            pl.BlockSpec((None, CONV_WIDTH - 1, TOKEN_WIDTH), lambda s: (jnp.minimum(s, last) // tiles_per_seq, 0, 0)),
        ],
        out_shape=[
            jax.ShapeDtypeStruct((rows, D_MODEL), BF16),
            jax.ShapeDtypeStruct((batch, CONV_WIDTH - 1, TOKEN_WIDTH), F32),
        ],
        scratch_shapes=[pltpu.VMEM((tq + CARRY_ROWS, TOKEN_WIDTH), F32)],
        compiler_params=pltpu.CompilerParams(
            dimension_semantics=("arbitrary",),
            vmem_limit_bytes=_vmem_limit(blocks, scratch + 6 * _nbytes((tq, TOKEN_WIDTH), F32))),
        name="conv_prompt",
    )(z, mk, mv, conv_w, mix_sample)


def _conv_sample_kernel(z_ref, st_ref, mk_ref, mv_ref, cw_ref, mix_ref, nst_ref, ext_ref, *, dec_seq):
    t = dec_seq
    for n in range(SAMPLE_GROUP):
        r0, r1 = n * t, (n + 1) * t
        c = z_ref[r0:r1, TOKEN_WIDTH:2 * TOKEN_WIDTH].astype(F32)
        u = z_ref[r0:r1, 2 * TOKEN_WIDTH:3 * TOKEN_WIDTH].astype(F32)
        cu = c * u
        ext_ref[CARRY_ROWS - 2:CARRY_ROWS, :] = st_ref[n]
        ext_ref[CARRY_ROWS:CARRY_ROWS + t, :] = cu
        conv = (cw_ref[0:1, :] * ext_ref[CARRY_ROWS - 2:CARRY_ROWS - 2 + t, :]
                + cw_ref[1:2, :] * ext_ref[CARRY_ROWS - 1:CARRY_ROWS - 1 + t, :]
                + cw_ref[2:3, :] * cu)
        b = z_ref[r0:r1, 0:TOKEN_WIDTH].astype(F32)
        mix_ref[r0:r1, 0:TOKEN_WIDTH] = b * conv
        nst_ref[n] = ext_ref[CARRY_ROWS + t - 2:CARRY_ROWS + t, :]
        for h in range(MEM_HEADS):
            lo, hi = h * MEM_HEAD_DIM, (h + 1) * MEM_HEAD_DIM
            q = z_ref[r0:r1, 3 * TOKEN_WIDTH + lo:3 * TOKEN_WIDTH + hi]
            o = _cross_attention_head(q, mk_ref[n, :, lo:hi].astype(BF16), mv_ref[n, :, lo:hi].astype(BF16))
            mix_ref[r0:r1, TOKEN_WIDTH + lo:TOKEN_WIDTH + hi] = o


def _conv_sample(z, state, mem_k, mem_v, conv_w, *, layer, prompt_rows, dec_seq):
    zc = z.shape[1]
    dec_batch = state.shape[0]
    g = SAMPLE_GROUP
    gr = g * dec_seq
    first = prompt_rows // gr
    kernel = functools.partial(_conv_sample_kernel, dec_seq=dec_seq)
    blocks = (_nbytes((gr, zc), BF16) + 2 * _nbytes((g, CONV_WIDTH - 1, TOKEN_WIDTH), F32)
              + 2 * _nbytes((g, N_MEM, MEM_WIDTH), F32) + _nbytes((gr, D_MODEL), F32))
    return pl.pallas_call(
        kernel,
        grid=(dec_batch // g,),
        in_specs=[
            pl.BlockSpec((gr, zc), lambda i: (first + i, 0)),
            pl.BlockSpec((g, CONV_WIDTH - 1, TOKEN_WIDTH), lambda i: (i, 0, 0)),
            pl.BlockSpec((None, g, N_MEM, MEM_WIDTH), lambda i: (layer, i, 0, 0)),
            pl.BlockSpec((None, g, N_MEM, MEM_WIDTH), lambda i: (layer, i, 0, 0)),
            pl.BlockSpec((CONV_WIDTH, TOKEN_WIDTH), lambda i: (0, 0)),
        ],
        out_specs=[
            pl.BlockSpec((gr, D_MODEL), lambda i: (i, 0)),
            pl.BlockSpec((g, CONV_WIDTH - 1, TOKEN_WIDTH), lambda i: (i, 0, 0)),
        ],
        out_shape=[
            jax.ShapeDtypeStruct((dec_batch * dec_seq, D_MODEL), F32),
            jax.ShapeDtypeStruct((dec_batch, CONV_WIDTH - 1, TOKEN_WIDTH), F32),
        ],
        scratch_shapes=[pltpu.VMEM((2 * CARRY_ROWS, TOKEN_WIDTH), F32)],
        compiler_params=pltpu.CompilerParams(
            dimension_semantics=("arbitrary",),
            vmem_limit_bytes=_vmem_limit(blocks, 0)),
        name="conv_sample",
    )(z, state, mem_k, mem_v, conv_w)


def _band_attention_head(q, k, v, sink, mask):
    s = _dot_nt(q, k) * (HEAD_DIM ** -0.5)
    s = jnp.where(mask, s, -jnp.inf)
    m = jnp.maximum(jnp.max(s, axis=-1, keepdims=True), sink)
    e = jnp.exp(s - m)
    p = e / (jnp.sum(e, axis=-1, keepdims=True) + jnp.exp(sink - m))
    return jnp.dot(p.astype(BF16), v, preferred_element_type=F32)


def _swa_prompt_kernel(sink_ref, zq_ref, zp_ref, mk_ref, mv_ref, ms_ref, mix_ref, *,
                       blocks_per_seq, n_prompt_blocks):
    s = pl.program_id(0)
    k_off = TOKEN_WIDTH
    v_off = TOKEN_WIDTH + KV_WIDTH
    qm_off = TOKEN_WIDTH + 2 * KV_WIDTH

    @pl.when(s < n_prompt_blocks)
    def _():
        has_prev = (s % blocks_per_seq) > 0
        row = lax.broadcasted_iota(jnp.int32, (WINDOW, 2 * WINDOW), 0)
        col = lax.broadcasted_iota(jnp.int32, (WINDOW, 2 * WINDOW), 1)
        mask = jnp.logical_or(jnp.logical_and(jnp.logical_and(col < WINDOW, col > row), has_prev),
                              jnp.logical_and(col >= WINDOW, col - WINDOW <= row))
        for kh in range(N_KV_HEADS):
            lo, hi = kh * HEAD_DIM, (kh + 1) * HEAD_DIM
            k = jnp.concatenate([zp_ref[:, lo:hi], zq_ref[:, k_off + lo:k_off + hi]], axis=0)
            v = jnp.concatenate([zp_ref[:, KV_WIDTH + lo:KV_WIDTH + hi], zq_ref[:, v_off + lo:v_off + hi]], axis=0)
            for g in range(GROUP):
                h = kh * GROUP + g
                q = zq_ref[:, h * HEAD_DIM:(h + 1) * HEAD_DIM]
                o = _band_attention_head(q, k, v, sink_ref[h], mask)
                mix_ref[:, h * HEAD_DIM:(h + 1) * HEAD_DIM] = o.astype(BF16)
        for h in range(MEM_HEADS):
            lo, hi = h * MEM_HEAD_DIM, (h + 1) * MEM_HEAD_DIM
            q = zq_ref[:, qm_off + lo:qm_off + hi]
            o = _cross_attention_head(q, mk_ref[:, lo:hi].astype(BF16), mv_ref[:, lo:hi].astype(BF16))
            mix_ref[:, TOKEN_WIDTH + lo:TOKEN_WIDTH + hi] = o.astype(BF16)

    @pl.when(s >= n_prompt_blocks)
    def _():
        mix_ref[...] = ms_ref[...]


def _swa_prompt(z, mk, mv, sinks, mix_sample, *, layer, batch, seq):
    rows, zc = z.shape
    tq = WINDOW
    blocks_per_seq = seq // tq
    n_prompt_blocks = batch * blocks_per_seq
    n_sample_blocks = mix_sample.shape[0] // tq
    last = n_prompt_blocks - 1
    kv_col_block = TOKEN_WIDTH // (2 * KV_WIDTH)
    kernel = functools.partial(_swa_prompt_kernel, blocks_per_seq=blocks_per_seq,
                               n_prompt_blocks=n_prompt_blocks)
    blocks = (_nbytes((tq, zc), BF16) + _nbytes((tq, 2 * KV_WIDTH), BF16)
              + 2 * _nbytes((N_MEM, MEM_WIDTH), F32) + 2 * _nbytes((tq, D_MODEL), BF16))
    return pl.pallas_call(
        kernel,
        grid=(n_prompt_blocks + n_sample_blocks,),
        in_specs=[
            pl.BlockSpec(memory_space=pltpu.SMEM),
            pl.BlockSpec((tq, zc), lambda s: (jnp.minimum(s, last), 0)),
            pl.BlockSpec((tq, 2 * KV_WIDTH), lambda s: (jnp.maximum(jnp.minimum(s, last) - 1, 0), kv_col_block)),
            pl.BlockSpec((None, N_MEM, MEM_WIDTH), lambda s: (layer, jnp.minimum(s, last) // blocks_per_seq, 0)),
            pl.BlockSpec((None, N_MEM, MEM_WIDTH), lambda s: (layer, jnp.minimum(s, last) // blocks_per_seq, 0)),
            pl.BlockSpec((tq, D_MODEL), lambda s: (jnp.maximum(s - n_prompt_blocks, 0), 0)),
        ],
        out_specs=pl.BlockSpec((tq, D_MODEL), lambda s: (s, 0)),
        out_shape=jax.ShapeDtypeStruct((rows, D_MODEL), BF16),
        compiler_params=pltpu.CompilerParams(
            dimension_semantics=("arbitrary",),
            vmem_limit_bytes=_vmem_limit(blocks, 0)),
        name="swa_prompt",
    )(sinks, z, z, mk, mv, mix_sample)


def _swa_sample_kernel(sink_ref, z_ref, ck_ref, cv_ref, mk_ref, mv_ref,
                       mix_ref, nk_ref, nv_ref, knew_ref, vnew_ref, *, dec_seq):
    t = dec_seq
    k_off = TOKEN_WIDTH
    v_off = TOKEN_WIDTH + KV_WIDTH
    qm_off = TOKEN_WIDTH + 2 * KV_WIDTH
    rows = GROUP * t
    qi = lax.broadcasted_iota(jnp.int32, (rows, 2 * WINDOW), 0) % t
    col = lax.broadcasted_iota(jnp.int32, (rows, 2 * WINDOW), 1)
    mask = jnp.logical_or(jnp.logical_and(col < WINDOW, col > qi),
                          jnp.logical_and(col >= WINDOW, col - WINDOW <= qi))
    knew_ref[...] = jnp.zeros((WINDOW, KV_WIDTH), F32)
    vnew_ref[...] = jnp.zeros((WINDOW, KV_WIDTH), F32)
    for n in range(SAMPLE_GROUP):
        r0, r1 = n * t, (n + 1) * t
        k_new = z_ref[r0:r1, k_off:k_off + KV_WIDTH].astype(F32)
        v_new = z_ref[r0:r1, v_off:v_off + KV_WIDTH].astype(F32)
        knew_ref[0:t, :] = k_new
        vnew_ref[0:t, :] = v_new
        nk_ref[n, 0:WINDOW - t, :] = ck_ref[n, t:WINDOW, :]
        nv_ref[n, 0:WINDOW - t, :] = cv_ref[n, t:WINDOW, :]
        nk_ref[n, WINDOW - t:WINDOW, :] = k_new
        nv_ref[n, WINDOW - t:WINDOW, :] = v_new
        for kh in range(N_KV_HEADS):
            lo, hi = kh * HEAD_DIM, (kh + 1) * HEAD_DIM
            k = jnp.concatenate([ck_ref[n, :, lo:hi], knew_ref[:, lo:hi]], axis=0).astype(BF16)
            v = jnp.concatenate([cv_ref[n, :, lo:hi], vnew_ref[:, lo:hi]], axis=0).astype(BF16)
            q = jnp.concatenate(
                [z_ref[r0:r1, (kh * GROUP + g) * HEAD_DIM:(kh * GROUP + g + 1) * HEAD_DIM].astype(F32)
                 for g in range(GROUP)], axis=0).astype(BF16)
            sink = jnp.concatenate(
                [jnp.full((t, 1), sink_ref[kh * GROUP + g], F32) for g in range(GROUP)], axis=0)
            o = _band_attention_head(q, k, v, sink, mask)
            for g in range(GROUP):
                h = kh * GROUP + g
                mix_ref[r0:r1, h * HEAD_DIM:(h + 1) * HEAD_DIM] = o[g * t:(g + 1) * t, :]
        for h in range(MEM_HEADS):
            lo, hi = h * MEM_HEAD_DIM, (h + 1) * MEM_HEAD_DIM
            q = z_ref[r0:r1, qm_off + lo:qm_off + hi]
            o = _cross_attention_head(q, mk_ref[n, :, lo:hi].astype(BF16), mv_ref[n, :, lo:hi].astype(BF16))
            mix_ref[r0:r1, TOKEN_WIDTH + lo:TOKEN_WIDTH + hi] = o


def _swa_sample(z, cache_k, cache_v, mem_k, mem_v, sinks, *, layer, prompt_rows, dec_seq):
    zc = z.shape[1]
    dec_batch = cache_k.shape[0]
    g = SAMPLE_GROUP
    gr = g * dec_seq
    first = prompt_rows // gr
    kernel = functools.partial(_swa_sample_kernel, dec_seq=dec_seq)
    win = jax.ShapeDtypeStruct((dec_batch, WINDOW, KV_WIDTH), F32)
    blocks = (_nbytes((gr, zc), BF16) + 4 * _nbytes((g, WINDOW, KV_WIDTH), F32)
              + 2 * _nbytes((g, N_MEM, MEM_WIDTH), F32) + _nbytes((gr, D_MODEL), F32))
    return pl.pallas_call(
        kernel,
        grid=(dec_batch // g,),
        in_specs=[
            pl.BlockSpec(memory_space=pltpu.SMEM),
            pl.BlockSpec((gr, zc), lambda i: (first + i, 0)),
            pl.BlockSpec((g, WINDOW, KV_WIDTH), lambda i: (i, 0, 0)),
            pl.BlockSpec((g, WINDOW, KV_WIDTH), lambda i: (i, 0, 0)),
            pl.BlockSpec((None, g, N_MEM, MEM_WIDTH), lambda i: (layer, i, 0, 0)),
            pl.BlockSpec((None, g, N_MEM, MEM_WIDTH), lambda i: (layer, i, 0, 0)),
        ],
        out_specs=[
            pl.BlockSpec((gr, D_MODEL), lambda i: (i, 0)),
            pl.BlockSpec((g, WINDOW, KV_WIDTH), lambda i: (i, 0, 0)),
            pl.BlockSpec((g, WINDOW, KV_WIDTH), lambda i: (i, 0, 0)),
        ],
        out_shape=[jax.ShapeDtypeStruct((dec_batch * dec_seq, D_MODEL), F32), win, win],
        scratch_shapes=[pltpu.VMEM((WINDOW, KV_WIDTH), F32), pltpu.VMEM((WINDOW, KV_WIDTH), F32)],
        compiler_params=pltpu.CompilerParams(
            dimension_semantics=("arbitrary",),
            vmem_limit_bytes=_vmem_limit(blocks, 2 * _nbytes((WINDOW, KV_WIDTH), F32))),
        name="swa_sample",
    )(sinks, z, cache_k, cache_v, mem_k, mem_v)


def _final_norm_kernel(x_ref, g_ref, yp_ref, ys_ref, *, n_prompt_tiles):
    s = pl.program_id(0)
    y = _rmsnorm(x_ref[...], g_ref[...])

    @pl.when(s < n_prompt_tiles)
    def _():
        yp_ref[...] = y

    @pl.when(s >= n_prompt_tiles)
    def _():
        ys_ref[...] = y


def _final_norm(x, g, *, prompt_rows):
    rows, d = x.shape
    sample_rows = rows - prompt_rows
    tm = sample_rows
    n_prompt_tiles = prompt_rows // tm
    kernel = functools.partial(_final_norm_kernel, n_prompt_tiles=n_prompt_tiles)
    return pl.pallas_call(
        kernel,
        grid=(n_prompt_tiles + 1,),
        in_specs=[
            pl.BlockSpec((tm, d), lambda s: (s, 0)),
            pl.BlockSpec((1, d), lambda s: (0, 0)),
        ],
        out_specs=[
            pl.BlockSpec((tm, d), lambda s: (jnp.minimum(s, n_prompt_tiles - 1), 0)),
            pl.BlockSpec((tm, d), lambda s: (0, 0)),
        ],
        out_shape=[
            jax.ShapeDtypeStruct((prompt_rows, d), F32),
            jax.ShapeDtypeStruct((sample_rows, d), F32),
        ],
        compiler_params=pltpu.CompilerParams(
            dimension_semantics=("arbitrary",),
            vmem_limit_bytes=_vmem_limit(3 * _nbytes((tm, d), F32), 0)),
        name="final_norm",
    )(x, g.reshape(1, d))


def kernel(x_prompt, x_sample, mem_prompt, state_conv, cache_win_k, cache_win_v, cache_mem_k, cache_mem_v,
           norm_mix, norm_mem, w_mem_kv, norm_ffn, w_gate, w_up, w_down,
           conv_w_in, conv_w, conv_w_out, attn_w_in, attn_sinks, attn_w_out, norm_final):
    batch, seq, d = x_prompt.shape
    dec_batch, dec_seq, _ = x_sample.shape
    depth = norm_mix.shape[0]
    prompt_rows = batch * seq
    sample_rows = dec_batch * dec_seq
    assert d == D_MODEL and depth == 2 and seq % CONV_ROWS == 0 and seq % WINDOW == 0
    assert (prompt_rows + sample_rows) % ROW_TILE == 0 and dec_batch % SAMPLE_GROUP == 0
    assert dec_seq == V7X_SUBLANES and sample_rows % WINDOW == 0 and prompt_rows % sample_rows == 0

    bf = lambda w: w.astype(BF16)
    x = jnp.concatenate([x_prompt.reshape(prompt_rows, d), x_sample.reshape(sample_rows, d)], axis=0)
    mem = mem_prompt.reshape(batch * N_MEM, d)
    mem_k_s = cache_mem_k.reshape(depth, dec_batch, N_MEM, MEM_WIDTH)
    mem_v_s = cache_mem_v.reshape(depth, dec_batch, N_MEM, MEM_WIDTH)

    mk, mv = _mem_kv(mem, norm_mem, bf(w_mem_kv))

    z = _norm_matmul(x, norm_mix[0], bf(conv_w_in[0]), BF16)
    mix_s, conv_s = _conv_sample(z, state_conv[0], mem_k_s, mem_v_s, conv_w[0],
                                 layer=0, prompt_rows=prompt_rows, dec_seq=dec_seq)
    mix, conv_p = _conv_prompt(z, mk, mv, conv_w[0], mix_s.astype(BF16), batch=batch, seq=seq)
    x = _matmul_residual(mix, bf(conv_w_out[0]), x)
    a = _ffn_up(x, norm_ffn[0], bf(w_gate[0]), bf(w_up[0]))
    x = _matmul_residual(a, bf(w_down[0]), x)

    z = _norm_matmul(x, norm_mix[1], bf(attn_w_in[0]), BF16)
    sinks = attn_sinks[0]
    mix_s, win_k_s, win_v_s = _swa_sample(
        z, cache_win_k[0].reshape(dec_batch, WINDOW, KV_WIDTH), cache_win_v[0].reshape(dec_batch, WINDOW, KV_WIDTH),
        mem_k_s, mem_v_s, sinks, layer=1, prompt_rows=prompt_rows, dec_seq=dec_seq)
    mix = _swa_prompt(z, mk, mv, sinks, mix_s.astype(BF16), layer=1, batch=batch, seq=seq)
    x = _matmul_residual(mix, bf(attn_w_out[0]), x)
    a = _ffn_up(x, norm_ffn[1], bf(w_gate[1]), bf(w_up[1]))
    x = _matmul_residual(a, bf(w_down[1]), x)

    y_prompt, y_sample = _final_norm(x, norm_final, prompt_rows=prompt_rows)

    zp = z[:prompt_rows].reshape(batch, seq, -1)
    win_k_p = zp[:, seq - WINDOW:, TOKEN_WIDTH:TOKEN_WIDTH + KV_WIDTH].astype(F32)
    win_v_p = zp[:, seq - WINDOW:, TOKEN_WIDTH + KV_WIDTH:TOKEN_WIDTH + 2 * KV_WIDTH].astype(F32)
    kv_shape = (1, -1, WINDOW, N_KV_HEADS, HEAD_DIM)
    mem_shape = (depth, batch, N_MEM, MEM_HEADS, MEM_HEAD_DIM)
    return (y_prompt.reshape(batch, seq, d),
            y_sample.reshape(dec_batch, dec_seq, d),
            conv_p[None],
            conv_s[None],
            win_k_p.reshape(kv_shape),
            win_v_p.reshape(kv_shape),
            win_k_s.reshape(kv_shape),
            win_v_s.reshape(kv_shape),
            mk.reshape(mem_shape),
            mv.reshape(mem_shape))
```

```python
import functools

import jax
import jax.numpy as jnp
from jax import lax
from jax.experimental import pallas as pl
from jax.experimental.pallas import tpu as pltpu

F32 = jnp.float32
BF16 = jnp.bfloat16

D_MODEL = 2048
N_MEM = 256
MEM_HEADS = 4
MEM_WIDTH = D_MODEL // 4
MEM_HEAD_DIM = MEM_WIDTH // MEM_HEADS
TOKEN_WIDTH = D_MODEL - MEM_WIDTH
CONV_WIDTH = 3
WINDOW = 128
HEAD_DIM = 64
N_HEADS = TOKEN_WIDTH // HEAD_DIM
N_KV_HEADS = 4
GROUP = N_HEADS // N_KV_HEADS
KV_WIDTH = N_KV_HEADS * HEAD_DIM
EPS = 1e-6

V7X_VMEM_BYTES = 64 * 1024 * 1024
V7X_SUBLANES = 8

PROMPT_TILE = 1024
COL_TILE = 512
DOWN_COL_TILE = 256
CONV_ROWS = 512
NORM_ROWS = 512
SAMPLE_GROUP = 8
CARRY_ROWS = V7X_SUBLANES


def _nbytes(shape, dtype):
    n = 1
    for s in shape:
        n *= s
    return n * jnp.dtype(dtype).itemsize


def _vmem_limit(block_bytes, scratch_bytes):
    need = 2 * block_bytes + scratch_bytes
    return int(min(need + max(need // 4, 8 << 20), V7X_VMEM_BYTES - (6 << 20)))


def _rmsnorm(x, g):
    r = lax.rsqrt(jnp.mean(x * x, axis=-1, keepdims=True) + EPS)
    return (x * r) * g


def _softmax_rows(s):
    m = jnp.max(s, axis=-1, keepdims=True)
    e = jnp.exp(s - m)
    return e / jnp.sum(e, axis=-1, keepdims=True)


def _dot(a, b):
    return jnp.dot(a, b, preferred_element_type=F32)


def _dot_nt(a, b):
    return lax.dot_general(a, b, (((1,), (1,)), ((), ())), preferred_element_type=F32)


def _prompt_rows_map(n_pt):
    return lambda i, j: (jnp.minimum(i, n_pt - 1), 0)


def _prompt_tile_map(n_pt, n_j):
    return lambda i, j: (jnp.minimum(i, n_pt - 1), jnp.where(i < n_pt, j, n_j - 1))


def _sample_tile_map(n_pt):
    return lambda i, j: (0, jnp.where(i < n_pt, 0, j))


def _const_map(i, j):
    return (0, 0)


def _norm_matmul_kernel(xp_ref, xs_ref, g_ref, w_ref, zp_ref, zs_ref, hp_ref, hs_ref, wb_ref, *, n_pt):
    i, j = pl.program_id(0), pl.program_id(1)
    wb_ref[...] = w_ref[...].astype(BF16)

    @pl.when(i < n_pt)
    def _():
        @pl.when(j == 0)
        def _():
            hp_ref[...] = _rmsnorm(xp_ref[...], g_ref[...]).astype(BF16)

        zp_ref[...] = _dot(hp_ref[...], wb_ref[...]).astype(zp_ref.dtype)

    @pl.when(i == n_pt)
    def _():
        @pl.when(j == 0)
        def _():
            hs_ref[...] = _rmsnorm(xs_ref[...], g_ref[...]).astype(BF16)

        zs_ref[...] = _dot(hs_ref[...], wb_ref[...]).astype(zs_ref.dtype)


def _norm_matmul(xp, xs, g, w, *, layer, w_layer):
    rp, d = xp.shape
    rs = xs.shape[0]
    n = w.shape[2]
    tm, tn = PROMPT_TILE, COL_TILE
    n_pt, n_j = rp // tm, n // tn
    blocks = (_nbytes((tm, d), F32) + _nbytes((rs, d), F32) + _nbytes((d, tn), F32)
              + _nbytes((tm, tn), BF16) + _nbytes((rs, tn), BF16))
    scratch = _nbytes((tm, d), BF16) + _nbytes((rs, d), BF16) + _nbytes((d, tn), BF16)
    return pl.pallas_call(
        functools.partial(_norm_matmul_kernel, n_pt=n_pt),
        grid=(n_pt + 1, n_j),
        in_specs=[
            pl.BlockSpec((tm, d), _prompt_rows_map(n_pt)),
            pl.BlockSpec((rs, d), _const_map),
            pl.BlockSpec((None, 1, d), lambda i, j: (layer, 0, 0)),
            pl.BlockSpec((None, d, tn), lambda i, j: (w_layer, 0, j)),
        ],
        out_specs=[
            pl.BlockSpec((tm, tn), _prompt_tile_map(n_pt, n_j)),
            pl.BlockSpec((rs, tn), _sample_tile_map(n_pt)),
        ],
        out_shape=[jax.ShapeDtypeStruct((rp, n), BF16), jax.ShapeDtypeStruct((rs, n), BF16)],
        scratch_shapes=[pltpu.VMEM((tm, d), BF16), pltpu.VMEM((rs, d), BF16), pltpu.VMEM((d, tn), BF16)],
        compiler_params=pltpu.CompilerParams(
            dimension_semantics=("arbitrary", "arbitrary"),
            vmem_limit_bytes=_vmem_limit(blocks, scratch)),
        name="norm_matmul",
    )(xp, xs, g, w)


def _swiglu(h, wg, wu):
    gate = _dot(h, wg)
    up = _dot(h, wu)
    return (gate * jax.nn.sigmoid(gate) * up).astype(BF16)


def _ffn_up_kernel(xp_ref, xs_ref, g_ref, wg_ref, wu_ref, ap_ref, as_ref,
                   hp_ref, hs_ref, wgb_ref, wub_ref, *, n_pt):
    i, j = pl.program_id(0), pl.program_id(1)
    wgb_ref[...] = wg_ref[...].astype(BF16)
    wub_ref[...] = wu_ref[...].astype(BF16)

    @pl.when(i < n_pt)
    def _():
        @pl.when(j == 0)
        def _():
            hp_ref[...] = _rmsnorm(xp_ref[...], g_ref[...]).astype(BF16)

        ap_ref[...] = _swiglu(hp_ref[...], wgb_ref[...], wub_ref[...])

    @pl.when(i == n_pt)
    def _():
        @pl.when(j == 0)
        def _():
            hs_ref[...] = _rmsnorm(xs_ref[...], g_ref[...]).astype(BF16)

        as_ref[...] = _swiglu(hs_ref[...], wgb_ref[...], wub_ref[...])


def _ffn_up(xp, xs, g, wg, wu, *, layer):
    rp, d = xp.shape
    rs = xs.shape[0]
    n = wg.shape[2]
    tm, tn = PROMPT_TILE, COL_TILE
    n_pt, n_j = rp // tm, n // tn
    blocks = (_nbytes((tm, d), F32) + _nbytes((rs, d), F32) + 2 * _nbytes((d, tn), F32)
              + _nbytes((tm, tn), BF16) + _nbytes((rs, tn), BF16))
    scratch = _nbytes((tm, d), BF16) + _nbytes((rs, d), BF16) + 2 * _nbytes((d, tn), BF16)
    w_spec = pl.BlockSpec((None, d, tn), lambda i, j: (layer, 0, j))
    return pl.pallas_call(
        functools.partial(_ffn_up_kernel, n_pt=n_pt),
        grid=(n_pt + 1, n_j),
        in_specs=[
            pl.BlockSpec((tm, d), _prompt_rows_map(n_pt)),
            pl.BlockSpec((rs, d), _const_map),
            pl.BlockSpec((None, 1, d), lambda i, j: (layer, 0, 0)),
            w_spec,
            w_spec,
        ],
        out_specs=[
            pl.BlockSpec((tm, tn), _prompt_tile_map(n_pt, n_j)),
            pl.BlockSpec((rs, tn), _sample_tile_map(n_pt)),
        ],
        out_shape=[jax.ShapeDtypeStruct((rp, n), BF16), jax.ShapeDtypeStruct((rs, n), BF16)],
        scratch_shapes=[pltpu.VMEM((tm, d), BF16), pltpu.VMEM((rs, d), BF16),
                        pltpu.VMEM((d, tn), BF16), pltpu.VMEM((d, tn), BF16)],
        compiler_params=pltpu.CompilerParams(
            dimension_semantics=("arbitrary", "arbitrary"),
            vmem_limit_bytes=_vmem_limit(blocks, scratch)),
        name="ffn_up",
    )(xp, xs, g, wg, wu)


def _matmul_residual_kernel(ap_ref, as_ref, w_ref, xp_ref, xs_ref, op_ref, os_ref, wb_ref, *, n_pt):
    i = pl.program_id(0)
    wb_ref[...] = w_ref[...].astype(BF16)

    @pl.when(i < n_pt)
    def _():
        op_ref[...] = xp_ref[...] + _dot(ap_ref[...], wb_ref[...])

    @pl.when(i == n_pt)
    def _():
        os_ref[...] = xs_ref[...] + _dot(as_ref[...], wb_ref[...])


def _matmul_residual(ap, as_, w, xp, xs, *, w_layer, tn):
    rp, k = ap.shape
    rs = as_.shape[0]
    n = w.shape[2]
    tm = PROMPT_TILE
    n_pt, n_j = rp // tm, n // tn
    blocks = (_nbytes((tm, k), BF16) + _nbytes((rs, k), BF16) + _nbytes((k, tn), F32)
              + 2 * _nbytes((tm, tn), F32) + 2 * _nbytes((rs, tn), F32))
    scratch = _nbytes((k, tn), BF16)
    return pl.pallas_call(
        functools.partial(_matmul_residual_kernel, n_pt=n_pt),
        grid=(n_pt + 1, n_j),
        in_specs=[
            pl.BlockSpec((tm, k), _prompt_rows_map(n_pt)),
            pl.BlockSpec((rs, k), _const_map),
            pl.BlockSpec((None, k, tn), lambda i, j: (w_layer, 0, j)),
            pl.BlockSpec((tm, tn), _prompt_tile_map(n_pt, n_j)),
            pl.BlockSpec((rs, tn), _sample_tile_map(n_pt)),
        ],
        out_specs=[
            pl.BlockSpec((tm, tn), _prompt_tile_map(n_pt, n_j)),
            pl.BlockSpec((rs, tn), _sample_tile_map(n_pt)),
        ],
        out_shape=[jax.ShapeDtypeStruct((rp, n), F32), jax.ShapeDtypeStruct((rs, n), F32)],
        scratch_shapes=[pltpu.VMEM((k, tn), BF16)],
        compiler_params=pltpu.CompilerParams(
            dimension_semantics=("arbitrary", "arbitrary"),
            vmem_limit_bytes=_vmem_limit(blocks, scratch)),
        name="matmul_residual",
    )(ap, as_, w, xp, xs)


def _mem_kv_kernel(x_ref, g_ref, w_ref, k_ref, v_ref):
    h = _rmsnorm(x_ref[...], g_ref[...]).astype(BF16)
    kv = _dot(h, w_ref[...].astype(BF16))
    k_ref[...] = kv[:, :MEM_WIDTH]
    v_ref[...] = kv[:, MEM_WIDTH:]


def _mem_kv(mem, g, w):
    rows, d = mem.shape
    depth = w.shape[0]
    tm = 512
    out = jax.ShapeDtypeStruct((depth, rows, MEM_WIDTH), F32)
    blocks = (_nbytes((tm, d), F32) + _nbytes((d, 2 * MEM_WIDTH), F32) + 2 * _nbytes((tm, MEM_WIDTH), F32))
    return pl.pallas_call(
        _mem_kv_kernel,
        grid=(depth, rows // tm),
        in_specs=[
            pl.BlockSpec((tm, d), lambda l, i: (i, 0)),
            pl.BlockSpec((None, 1, d), lambda l, i: (l, 0, 0)),
            pl.BlockSpec((None, d, 2 * MEM_WIDTH), lambda l, i: (l, 0, 0)),
        ],
        out_specs=[
            pl.BlockSpec((None, tm, MEM_WIDTH), lambda l, i: (l, i, 0)),
            pl.BlockSpec((None, tm, MEM_WIDTH), lambda l, i: (l, i, 0)),
        ],
        out_shape=[out, out],
        compiler_params=pltpu.CompilerParams(
            dimension_semantics=("arbitrary", "arbitrary"),
            vmem_limit_bytes=_vmem_limit(blocks, _nbytes((d, 2 * MEM_WIDTH), BF16))),
        name="mem_kv",
    )(mem, g, w)


def _cross_attention_head(q, k, v):
    s = _dot_nt(q, k) * (MEM_HEAD_DIM ** -0.5)
    p = _softmax_rows(s).astype(BF16)
    return _dot(p, v)


def _conv_prompt_kernel(z_ref, mk_ref, mv_ref, cw_ref, mix_ref, st_ref, ext_ref, *, tiles_per_seq):
    s = pl.program_id(0)
    tq = z_ref.shape[0]

    @pl.when(s % tiles_per_seq == 0)
    def _():
        ext_ref[0:CARRY_ROWS, :] = jnp.zeros((CARRY_ROWS, TOKEN_WIDTH), F32)

    c = z_ref[:, TOKEN_WIDTH:2 * TOKEN_WIDTH].astype(F32)
    u = z_ref[:, 2 * TOKEN_WIDTH:3 * TOKEN_WIDTH].astype(F32)
    cu = c * u
    ext_ref[CARRY_ROWS:CARRY_ROWS + tq, :] = cu
    conv = (cw_ref[0:1, :] * ext_ref[CARRY_ROWS - 2:CARRY_ROWS - 2 + tq, :]
            + cw_ref[1:2, :] * ext_ref[CARRY_ROWS - 1:CARRY_ROWS - 1 + tq, :]
            + cw_ref[2:3, :] * cu)
    b = z_ref[:, 0:TOKEN_WIDTH].astype(F32)
    mix_ref[:, 0:TOKEN_WIDTH] = (b * conv).astype(BF16)
    st_ref[...] = ext_ref[CARRY_ROWS + tq - 2:CARRY_ROWS + tq, :]
    ext_ref[0:CARRY_ROWS, :] = ext_ref[tq:tq + CARRY_ROWS, :]

    for h in range(MEM_HEADS):
        lo, hi = h * MEM_HEAD_DIM, (h + 1) * MEM_HEAD_DIM
        q = z_ref[:, 3 * TOKEN_WIDTH + lo:3 * TOKEN_WIDTH + hi]
        o = _cross_attention_head(q, mk_ref[:, lo:hi].astype(BF16), mv_ref[:, lo:hi].astype(BF16))
        mix_ref[:, TOKEN_WIDTH + lo:TOKEN_WIDTH + hi] = o.astype(BF16)


def _conv_prompt(z, mk, mv, conv_w, *, layer, batch, seq):
    rows, zc = z.shape
    tq = CONV_ROWS
    tiles_per_seq = seq // tq
    blocks = (_nbytes((tq, zc), BF16) + 2 * _nbytes((N_MEM, MEM_WIDTH), F32) + _nbytes((tq, D_MODEL), BF16))
    scratch = _nbytes((tq + CARRY_ROWS, TOKEN_WIDTH), F32)
    return pl.pallas_call(
        functools.partial(_conv_prompt_kernel, tiles_per_seq=tiles_per_seq),
        grid=(batch * tiles_per_seq,),
        in_specs=[
            pl.BlockSpec((tq, zc), lambda s: (s, 0)),
            pl.BlockSpec((None, N_MEM, MEM_WIDTH), lambda s: (layer, s // tiles_per_seq, 0)),
            pl.BlockSpec((None, N_MEM, MEM_WIDTH), lambda s: (layer, s // tiles_per_seq, 0)),
            pl.BlockSpec((None, CONV_WIDTH, TOKEN_WIDTH), lambda s: (0, 0, 0)),
        ],
        out_specs=[
            pl.BlockSpec((tq, D_MODEL), lambda s: (s, 0)),
            pl.BlockSpec((None, CONV_WIDTH - 1, TOKEN_WIDTH), lambda s: (s // tiles_per_seq, 0, 0)),
        ],
        out_shape=[
            jax.ShapeDtypeStruct((rows, D_MODEL), BF16),
            jax.ShapeDtypeStruct((batch, CONV_WIDTH - 1, TOKEN_WIDTH), F32),
        ],
        scratch_shapes=[pltpu.VMEM((tq + CARRY_ROWS, TOKEN_WIDTH), F32)],
        compiler_params=pltpu.CompilerParams(
            dimension_semantics=("arbitrary",),
            vmem_limit_bytes=_vmem_limit(blocks, scratch + 6 * _nbytes((tq, TOKEN_WIDTH), F32))),
        name="conv_prompt",
    )(z, mk, mv, conv_w)


def _conv_sample_kernel(z_ref, st_ref, mk_ref, mv_ref, cw_ref, mix_ref, nst_ref, ext_ref, mixf_ref, *, dec_seq):
    t = dec_seq
    for n in range(SAMPLE_GROUP):
        r0, r1 = n * t, (n + 1) * t
        c = z_ref[r0:r1, TOKEN_WIDTH:2 * TOKEN_WIDTH].astype(F32)
        u = z_ref[r0:r1, 2 * TOKEN_WIDTH:3 * TOKEN_WIDTH].astype(F32)
        cu = c * u
        ext_ref[CARRY_ROWS - 2:CARRY_ROWS, :] = st_ref[n]
        ext_ref[CARRY_ROWS:CARRY_ROWS + t, :] = cu
        conv = (cw_ref[0:1, :] * ext_ref[CARRY_ROWS - 2:CARRY_ROWS - 2 + t, :]
                + cw_ref[1:2, :] * ext_ref[CARRY_ROWS - 1:CARRY_ROWS - 1 + t, :]
                + cw_ref[2:3, :] * cu)
        b = z_ref[r0:r1, 0:TOKEN_WIDTH].astype(F32)
        mixf_ref[r0:r1, 0:TOKEN_WIDTH] = b * conv
        nst_ref[n] = ext_ref[CARRY_ROWS + t - 2:CARRY_ROWS + t, :]
        for h in range(MEM_HEADS):
            lo, hi = h * MEM_HEAD_DIM, (h + 1) * MEM_HEAD_DIM
            q = z_ref[r0:r1, 3 * TOKEN_WIDTH + lo:3 * TOKEN_WIDTH + hi]
            o = _cross_attention_head(q, mk_ref[n, :, lo:hi].astype(BF16), mv_ref[n, :, lo:hi].astype(BF16))
            mixf_ref[r0:r1, TOKEN_WIDTH + lo:TOKEN_WIDTH + hi] = o
    mix_ref[...] = mixf_ref[...].astype(BF16)


def _conv_sample(z, state, mem_k, mem_v, conv_w, *, layer, dec_seq):
    rows, zc = z.shape
    dec_batch = state.shape[1]
    g = SAMPLE_GROUP
    gr = g * dec_seq
    blocks = (_nbytes((gr, zc), BF16) + 2 * _nbytes((g, CONV_WIDTH - 1, TOKEN_WIDTH), F32)
              + 2 * _nbytes((g, N_MEM, MEM_WIDTH), F32) + _nbytes((gr, D_MODEL), BF16))
    scratch = _nbytes((2 * CARRY_ROWS, TOKEN_WIDTH), F32) + _nbytes((gr, D_MODEL), F32)
    return pl.pallas_call(
        functools.partial(_conv_sample_kernel, dec_seq=dec_seq),
        grid=(dec_batch // g,),
        in_specs=[
            pl.BlockSpec((gr, zc), lambda i: (i, 0)),
            pl.BlockSpec((None, g, CONV_WIDTH - 1, TOKEN_WIDTH), lambda i: (0, i, 0, 0)),
            pl.BlockSpec((None, g, N_MEM, MEM_WIDTH), lambda i: (layer, i, 0, 0)),
            pl.BlockSpec((None, g, N_MEM, MEM_WIDTH), lambda i: (layer, i, 0, 0)),
            pl.BlockSpec((None, CONV_WIDTH, TOKEN_WIDTH), lambda i: (0, 0, 0)),
        ],
        out_specs=[
            pl.BlockSpec((gr, D_MODEL), lambda i: (i, 0)),
            pl.BlockSpec((g, CONV_WIDTH - 1, TOKEN_WIDTH), lambda i: (i, 0, 0)),
        ],
        out_shape=[
            jax.ShapeDtypeStruct((rows, D_MODEL), BF16),
            jax.ShapeDtypeStruct((dec_batch, CONV_WIDTH - 1, TOKEN_WIDTH), F32),
        ],
        scratch_shapes=[pltpu.VMEM((2 * CARRY_ROWS, TOKEN_WIDTH), F32), pltpu.VMEM((gr, D_MODEL), F32)],
        compiler_params=pltpu.CompilerParams(
            dimension_semantics=("arbitrary",),
            vmem_limit_bytes=_vmem_limit(blocks, scratch)),
        name="conv_sample",
    )(z, state, mem_k, mem_v, conv_w)


def _band_attention_head(q, k, v, sink, mask):
    s = _dot_nt(q, k) * (HEAD_DIM ** -0.5)
    s = jnp.where(mask, s, -jnp.inf)
    m = jnp.maximum(jnp.max(s, axis=-1, keepdims=True), sink)
    e = jnp.exp(s - m)
    p = e / (jnp.sum(e, axis=-1, keepdims=True) + jnp.exp(sink - m))
    return _dot(p.astype(BF16), v)


def _swa_prompt_kernel(sink_ref, zq_ref, zp_ref, mk_ref, mv_ref, mix_ref, *, blocks_per_seq):
    s = pl.program_id(0)
    k_off = TOKEN_WIDTH
    v_off = TOKEN_WIDTH + KV_WIDTH
    qm_off = TOKEN_WIDTH + 2 * KV_WIDTH
    has_prev = (s % blocks_per_seq) > 0
    row = lax.broadcasted_iota(jnp.int32, (WINDOW, 2 * WINDOW), 0)
    col = lax.broadcasted_iota(jnp.int32, (WINDOW, 2 * WINDOW), 1)
    mask = jnp.logical_or(jnp.logical_and(jnp.logical_and(col < WINDOW, col > row), has_prev),
                          jnp.logical_and(col >= WINDOW, col - WINDOW <= row))
    for kh in range(N_KV_HEADS):
        lo, hi = kh * HEAD_DIM, (kh + 1) * HEAD_DIM
        k = jnp.concatenate([zp_ref[:, lo:hi], zq_ref[:, k_off + lo:k_off + hi]], axis=0)
        v = jnp.concatenate([zp_ref[:, KV_WIDTH + lo:KV_WIDTH + hi], zq_ref[:, v_off + lo:v_off + hi]], axis=0)
        for g in range(GROUP):
            h = kh * GROUP + g
            q = zq_ref[:, h * HEAD_DIM:(h + 1) * HEAD_DIM]
            o = _band_attention_head(q, k, v, sink_ref[h], mask)
            mix_ref[:, h * HEAD_DIM:(h + 1) * HEAD_DIM] = o.astype(BF16)
    for h in range(MEM_HEADS):
        lo, hi = h * MEM_HEAD_DIM, (h + 1) * MEM_HEAD_DIM
        q = zq_ref[:, qm_off + lo:qm_off + hi]
        o = _cross_attention_head(q, mk_ref[:, lo:hi].astype(BF16), mv_ref[:, lo:hi].astype(BF16))
        mix_ref[:, TOKEN_WIDTH + lo:TOKEN_WIDTH + hi] = o.astype(BF16)


def _swa_prompt(z, mk, mv, sinks, *, layer, batch, seq):
    rows, zc = z.shape
    tq = WINDOW
    blocks_per_seq = seq // tq
    kv_col_block = TOKEN_WIDTH // (2 * KV_WIDTH)
    blocks = (_nbytes((tq, zc), BF16) + _nbytes((tq, 2 * KV_WIDTH), BF16)
              + 2 * _nbytes((N_MEM, MEM_WIDTH), F32) + _nbytes((tq, D_MODEL), BF16))
    return pl.pallas_call(
        functools.partial(_swa_prompt_kernel, blocks_per_seq=blocks_per_seq),
        grid=(batch * blocks_per_seq,),
        in_specs=[
            pl.BlockSpec(memory_space=pltpu.SMEM),
            pl.BlockSpec((tq, zc), lambda s: (s, 0)),
            pl.BlockSpec((tq, 2 * KV_WIDTH), lambda s: (jnp.maximum(s - 1, 0), kv_col_block)),
            pl.BlockSpec((None, N_MEM, MEM_WIDTH), lambda s: (layer, s // blocks_per_seq, 0)),
            pl.BlockSpec((None, N_MEM, MEM_WIDTH), lambda s: (layer, s // blocks_per_seq, 0)),
        ],
        out_specs=pl.BlockSpec((tq, D_MODEL), lambda s: (s, 0)),
        out_shape=jax.ShapeDtypeStruct((rows, D_MODEL), BF16),
        compiler_params=pltpu.CompilerParams(
            dimension_semantics=("arbitrary",),
            vmem_limit_bytes=_vmem_limit(blocks, 0)),
        name="swa_prompt",
    )(sinks, z, z, mk, mv)


def _swa_sample_kernel(sink_ref, z_ref, ck_ref, cv_ref, mk_ref, mv_ref,
                       mix_ref, nk_ref, nv_ref, knew_ref, vnew_ref, mixf_ref, *, dec_seq):
    t = dec_seq
    k_off = TOKEN_WIDTH
    v_off = TOKEN_WIDTH + KV_WIDTH
    qm_off = TOKEN_WIDTH + 2 * KV_WIDTH
    rows = GROUP * t
    qi = lax.broadcasted_iota(jnp.int32, (rows, 2 * WINDOW), 0) % t
    col = lax.broadcasted_iota(jnp.int32, (rows, 2 * WINDOW), 1)
    mask = jnp.logical_or(jnp.logical_and(col < WINDOW, col > qi),
                          jnp.logical_and(col >= WINDOW, col - WINDOW <= qi))
    knew_ref[...] = jnp.zeros((WINDOW, KV_WIDTH), F32)
    vnew_ref[...] = jnp.zeros((WINDOW, KV_WIDTH), F32)
    for n in range(SAMPLE_GROUP):
        r0, r1 = n * t, (n + 1) * t
        k_new = z_ref[r0:r1, k_off:k_off + KV_WIDTH].astype(F32)
        v_new = z_ref[r0:r1, v_off:v_off + KV_WIDTH].astype(F32)
        knew_ref[0:t, :] = k_new
        vnew_ref[0:t, :] = v_new
        nk_ref[n, 0:WINDOW - t, :] = ck_ref[n, t:WINDOW, :]
        nv_ref[n, 0:WINDOW - t, :] = cv_ref[n, t:WINDOW, :]
        nk_ref[n, WINDOW - t:WINDOW, :] = k_new
        nv_ref[n, WINDOW - t:WINDOW, :] = v_new
        for kh in range(N_KV_HEADS):
            lo, hi = kh * HEAD_DIM, (kh + 1) * HEAD_DIM
            k = jnp.concatenate([ck_ref[n, :, lo:hi], knew_ref[:, lo:hi]], axis=0).astype(BF16)
            v = jnp.concatenate([cv_ref[n, :, lo:hi], vnew_ref[:, lo:hi]], axis=0).astype(BF16)
            q = jnp.concatenate(
                [z_ref[r0:r1, (kh * GROUP + g) * HEAD_DIM:(kh * GROUP + g + 1) * HEAD_DIM].astype(F32)
                 for g in range(GROUP)], axis=0).astype(BF16)
            sink = jnp.concatenate(
                [jnp.full((t, 1), sink_ref[kh * GROUP + g], F32) for g in range(GROUP)], axis=0)
            o = _band_attention_head(q, k, v, sink, mask)
            for g in range(GROUP):
                h = kh * GROUP + g
                mixf_ref[r0:r1, h * HEAD_DIM:(h + 1) * HEAD_DIM] = o[g * t:(g + 1) * t, :]
        for h in range(MEM_HEADS):
            lo, hi = h * MEM_HEAD_DIM, (h + 1) * MEM_HEAD_DIM
            q = z_ref[r0:r1, qm_off + lo:qm_off + hi]
            o = _cross_attention_head(q, mk_ref[n, :, lo:hi].astype(BF16), mv_ref[n, :, lo:hi].astype(BF16))
            mixf_ref[r0:r1, TOKEN_WIDTH + lo:TOKEN_WIDTH + hi] = o
    mix_ref[...] = mixf_ref[...].astype(BF16)


def _swa_sample(z, cache_k, cache_v, mem_k, mem_v, sinks, *, layer, dec_seq):
    rows, zc = z.shape
    dec_batch = cache_k.shape[0]
    g = SAMPLE_GROUP
    gr = g * dec_seq
    win = jax.ShapeDtypeStruct((dec_batch, WINDOW, KV_WIDTH), F32)
    blocks = (_nbytes((gr, zc), BF16) + 4 * _nbytes((g, WINDOW, KV_WIDTH), F32)
              + 2 * _nbytes((g, N_MEM, MEM_WIDTH), F32) + _nbytes((gr, D_MODEL), BF16))
    scratch = 2 * _nbytes((WINDOW, KV_WIDTH), F32) + _nbytes((gr, D_MODEL), F32)
    return pl.pallas_call(
        functools.partial(_swa_sample_kernel, dec_seq=dec_seq),
        grid=(dec_batch // g,),
        in_specs=[
            pl.BlockSpec(memory_space=pltpu.SMEM),
            pl.BlockSpec((gr, zc), lambda i: (i, 0)),
            pl.BlockSpec((g, WINDOW, KV_WIDTH), lambda i: (i, 0, 0)),
            pl.BlockSpec((g, WINDOW, KV_WIDTH), lambda i: (i, 0, 0)),
            pl.BlockSpec((None, g, N_MEM, MEM_WIDTH), lambda i: (layer, i, 0, 0)),
            pl.BlockSpec((None, g, N_MEM, MEM_WIDTH), lambda i: (layer, i, 0, 0)),
        ],
        out_specs=[
            pl.BlockSpec((gr, D_MODEL), lambda i: (i, 0)),
            pl.BlockSpec((g, WINDOW, KV_WIDTH), lambda i: (i, 0, 0)),
            pl.BlockSpec((g, WINDOW, KV_WIDTH), lambda i: (i, 0, 0)),
        ],
        out_shape=[jax.ShapeDtypeStruct((rows, D_MODEL), BF16), win, win],
        scratch_shapes=[pltpu.VMEM((WINDOW, KV_WIDTH), F32), pltpu.VMEM((WINDOW, KV_WIDTH), F32),
                        pltpu.VMEM((gr, D_MODEL), F32)],
        compiler_params=pltpu.CompilerParams(
            dimension_semantics=("arbitrary",),
            vmem_limit_bytes=_vmem_limit(blocks, scratch)),
        name="swa_sample",
    )(sinks, z, cache_k, cache_v, mem_k, mem_v)


def _final_norm_kernel(xp_ref, xs_ref, g_ref, yp_ref, ys_ref, *, n_pt):
    s = pl.program_id(0)

    @pl.when(s < n_pt)
    def _():
        yp_ref[...] = _rmsnorm(xp_ref[...], g_ref[...])

    @pl.when(s == n_pt)
    def _():
        ys_ref[...] = _rmsnorm(xs_ref[...], g_ref[...])


def _final_norm(xp, xs, g):
    rp, d = xp.shape
    rs = xs.shape[0]
    tm = NORM_ROWS
    n_pt = rp // tm
    return pl.pallas_call(
        functools.partial(_final_norm_kernel, n_pt=n_pt),
        grid=(n_pt + 1,),
        in_specs=[
            pl.BlockSpec((tm, d), lambda s: (jnp.minimum(s, n_pt - 1), 0)),
            pl.BlockSpec((rs, d), lambda s: (0, 0)),
            pl.BlockSpec((1, d), lambda s: (0, 0)),
        ],
        out_specs=[
            pl.BlockSpec((tm, d), lambda s: (jnp.minimum(s, n_pt - 1), 0)),
            pl.BlockSpec((rs, d), lambda s: (0, 0)),
        ],
        out_shape=[jax.ShapeDtypeStruct((rp, d), F32), jax.ShapeDtypeStruct((rs, d), F32)],
        compiler_params=pltpu.CompilerParams(
            dimension_semantics=("arbitrary",),
            vmem_limit_bytes=_vmem_limit(2 * _nbytes((tm, d), F32) + 2 * _nbytes((rs, d), F32), 0)),
        name="final_norm",
    )(xp, xs, g.reshape(1, d))


def kernel(x_prompt, x_sample, mem_prompt, state_conv, cache_win_k, cache_win_v, cache_mem_k, cache_mem_v,
           norm_mix, norm_mem, w_mem_kv, norm_ffn, w_gate, w_up, w_down,
           conv_w_in, conv_w, conv_w_out, attn_w_in, attn_sinks, attn_w_out, norm_final):
    batch, seq, d = x_prompt.shape
    dec_batch, dec_seq, _ = x_sample.shape
    depth = norm_mix.shape[0]
    d_ff = w_gate.shape[2]
    prompt_rows = batch * seq
    sample_rows = dec_batch * dec_seq
    assert d == D_MODEL and depth == 2 and seq % CONV_ROWS == 0 and seq % WINDOW == 0
    assert prompt_rows % PROMPT_TILE == 0 and prompt_rows % NORM_ROWS == 0 and dec_batch % SAMPLE_GROUP == 0
    assert dec_seq == V7X_SUBLANES and d_ff % COL_TILE == 0 and d % COL_TILE == 0 and d % DOWN_COL_TILE == 0

    xp = x_prompt.reshape(prompt_rows, d)
    xs = x_sample.reshape(sample_rows, d)
    mem = mem_prompt.reshape(batch * N_MEM, d)
    mem_k_s = cache_mem_k.reshape(depth, dec_batch, N_MEM, MEM_WIDTH)
    mem_v_s = cache_mem_v.reshape(depth, dec_batch, N_MEM, MEM_WIDTH)
    g_mix = norm_mix.reshape(depth, 1, d)
    g_ffn = norm_ffn.reshape(depth, 1, d)

    mk, mv = _mem_kv(mem, norm_mem.reshape(depth, 1, d), w_mem_kv)

    zp, zs = _norm_matmul(xp, xs, g_mix, conv_w_in, layer=0, w_layer=0)
    mix_s, conv_s = _conv_sample(zs, state_conv, mem_k_s, mem_v_s, conv_w, layer=0, dec_seq=dec_seq)
    mix_p, conv_p = _conv_prompt(zp, mk, mv, conv_w, layer=0, batch=batch, seq=seq)
    xp, xs = _matmul_residual(mix_p, mix_s, conv_w_out, xp, xs, w_layer=0, tn=COL_TILE)
    ap, as_ = _ffn_up(xp, xs, g_ffn, w_gate, w_up, layer=0)
    xp, xs = _matmul_residual(ap, as_, w_down, xp, xs, w_layer=0, tn=DOWN_COL_TILE)

    zp, zs = _norm_matmul(xp, xs, g_mix, attn_w_in, layer=1, w_layer=0)
    sinks = attn_sinks[0]
    mix_s, win_k_s, win_v_s = _swa_sample(
        zs, cache_win_k[0].reshape(dec_batch, WINDOW, KV_WIDTH), cache_win_v[0].reshape(dec_batch, WINDOW, KV_WIDTH),
        mem_k_s, mem_v_s, sinks, layer=1, dec_seq=dec_seq)
    mix_p = _swa_prompt(zp, mk, mv, sinks, layer=1, batch=batch, seq=seq)
    xp, xs = _matmul_residual(mix_p, mix_s, attn_w_out, xp, xs, w_layer=0, tn=COL_TILE)
    ap, as_ = _ffn_up(xp, xs, g_ffn, w_gate, w_up, layer=1)
    xp, xs = _matmul_residual(ap, as_, w_down, xp, xs, w_layer=1, tn=DOWN_COL_TILE)

    y_prompt, y_sample = _final_norm(xp, xs, norm_final)

    win_p = zp.reshape(batch, seq, -1)[:, seq - WINDOW:, TOKEN_WIDTH:TOKEN_WIDTH + 2 * KV_WIDTH].astype(F32)
    kv_shape = (1, -1, WINDOW, N_KV_HEADS, HEAD_DIM)
    mem_shape = (depth, batch, N_MEM, MEM_HEADS, MEM_HEAD_DIM)
    return (y_prompt.reshape(batch, seq, d),
            y_sample.reshape(dec_batch, dec_seq, d),
            conv_p[None],
            conv_s[None],
            win_p[..., :KV_WIDTH].reshape(kv_shape),
            win_p[..., KV_WIDTH:].reshape(kv_shape),
            win_k_s.reshape(kv_shape),
            win_v_s.reshape(kv_shape),
            mk.reshape(mem_shape),
            mv.reshape(mem_shape))
```

```python
import functools

import jax
import jax.numpy as jnp
from jax import lax
from jax.experimental import pallas as pl
from jax.experimental.pallas import tpu as pltpu

F32 = jnp.float32
BF16 = jnp.bfloat16

D_MODEL = 2048
N_MEM = 256
MEM_HEADS = 4
MEM_WIDTH = D_MODEL // 4
MEM_HEAD_DIM = MEM_WIDTH // MEM_HEADS
TOKEN_WIDTH = D_MODEL - MEM_WIDTH
CONV_WIDTH = 3
WINDOW = 128
HEAD_DIM = 64
N_HEADS = TOKEN_WIDTH // HEAD_DIM
N_KV_HEADS = 4
GROUP = N_HEADS // N_KV_HEADS
KV_WIDTH = N_KV_HEADS * HEAD_DIM
EPS = 1e-6

V7X_VMEM_BYTES = 64 * 1024 * 1024
V7X_SUBLANES = 8

PROMPT_TILE = 1024
COL_TILE = 512
DOWN_COL_TILE = 256
CONV_ROWS = 512
NORM_ROWS = 512
SAMPLE_GROUP = 8
CARRY_ROWS = V7X_SUBLANES


def _nbytes(shape, dtype):
    n = 1
    for s in shape:
        n *= s
    return n * jnp.dtype(dtype).itemsize


def _vmem_limit(block_bytes, scratch_bytes):
    need = 2 * block_bytes + scratch_bytes
    return int(min(need + max(need // 4, 8 << 20), V7X_VMEM_BYTES - (6 << 20)))


def _rmsnorm(x, g):
    r = lax.rsqrt(jnp.mean(x * x, axis=-1, keepdims=True) + EPS)
    return (x * r) * g


def _softmax_rows(s):
    m = jnp.max(s, axis=-1, keepdims=True)
    e = jnp.exp(s - m)
    return e / jnp.sum(e, axis=-1, keepdims=True)


def _dot(a, b):
    return jnp.dot(a, b, preferred_element_type=F32)


def _dot_nt(a, b):
    return lax.dot_general(a, b, (((1,), (1,)), ((), ())), preferred_element_type=F32)


def _prompt_rows_map(n_pt):
    return lambda i, j: (jnp.minimum(i, n_pt - 1), 0)


def _prompt_tile_map(n_pt, n_j):
    return lambda i, j: (jnp.minimum(i, n_pt - 1), jnp.where(i < n_pt, j, n_j - 1))


def _sample_tile_map(n_pt):
    return lambda i, j: (0, jnp.where(i < n_pt, 0, j))


def _const_map(i, j):
    return (0, 0)


def _norm_matmul_kernel(xp_ref, xs_ref, g_ref, w_ref, zp_ref, zs_ref, hp_ref, hs_ref, *, n_pt):
    i, j = pl.program_id(0), pl.program_id(1)

    @pl.when(i < n_pt)
    def _():
        @pl.when(j == 0)
        def _():
            hp_ref[...] = _rmsnorm(xp_ref[...], g_ref[...]).astype(BF16)

        zp_ref[...] = _dot(hp_ref[...], w_ref[...].astype(BF16)).astype(zp_ref.dtype)

    @pl.when(i == n_pt)
    def _():
        @pl.when(j == 0)
        def _():
            hs_ref[...] = _rmsnorm(xs_ref[...], g_ref[...]).astype(BF16)

        zs_ref[...] = _dot(hs_ref[...], w_ref[...].astype(BF16)).astype(zs_ref.dtype)


def _norm_matmul(xp, xs, g, w, *, layer, w_layer):
    rp, d = xp.shape
    rs = xs.shape[0]
    n = w.shape[2]
    tm, tn = PROMPT_TILE, COL_TILE
    n_pt, n_j = rp // tm, n // tn
    blocks = (_nbytes((tm, d), F32) + _nbytes((rs, d), F32) + _nbytes((d, tn), F32)
              + _nbytes((tm, tn), BF16) + _nbytes((rs, tn), BF16))
    scratch = _nbytes((tm, d), BF16) + _nbytes((rs, d), BF16) + _nbytes((d, tn), BF16)
    return pl.pallas_call(
        functools.partial(_norm_matmul_kernel, n_pt=n_pt),
        grid=(n_pt + 1, n_j),
        in_specs=[
            pl.BlockSpec((tm, d), _prompt_rows_map(n_pt)),
            pl.BlockSpec((rs, d), _const_map),
            pl.BlockSpec((None, 1, d), lambda i, j: (layer, 0, 0)),
            pl.BlockSpec((None, d, tn), lambda i, j: (w_layer, 0, j)),
        ],
        out_specs=[
            pl.BlockSpec((tm, tn), _prompt_tile_map(n_pt, n_j)),
            pl.BlockSpec((rs, tn), _sample_tile_map(n_pt)),
        ],
        out_shape=[jax.ShapeDtypeStruct((rp, n), BF16), jax.ShapeDtypeStruct((rs, n), BF16)],
        scratch_shapes=[pltpu.VMEM((tm, d), BF16), pltpu.VMEM((rs, d), BF16)],
        compiler_params=pltpu.CompilerParams(
            dimension_semantics=("arbitrary", "arbitrary"),
            vmem_limit_bytes=_vmem_limit(blocks, scratch)),
        name="norm_matmul",
    )(xp, xs, g, w)


def _swiglu(h, wg, wu):
    gate = _dot(h, wg)
    up = _dot(h, wu)
    return (gate * jax.nn.sigmoid(gate) * up).astype(BF16)


def _ffn_up_kernel(xp_ref, xs_ref, g_ref, wg_ref, wu_ref, ap_ref, as_ref, hp_ref, hs_ref, *, n_pt):
    i, j = pl.program_id(0), pl.program_id(1)

    @pl.when(i < n_pt)
    def _():
        @pl.when(j == 0)
        def _():
            hp_ref[...] = _rmsnorm(xp_ref[...], g_ref[...]).astype(BF16)

        ap_ref[...] = _swiglu(hp_ref[...], wg_ref[...].astype(BF16), wu_ref[...].astype(BF16))

    @pl.when(i == n_pt)
    def _():
        @pl.when(j == 0)
        def _():
            hs_ref[...] = _rmsnorm(xs_ref[...], g_ref[...]).astype(BF16)

        as_ref[...] = _swiglu(hs_ref[...], wg_ref[...].astype(BF16), wu_ref[...].astype(BF16))


def _ffn_up(xp, xs, g, wg, wu, *, layer):
    rp, d = xp.shape
    rs = xs.shape[0]
    n = wg.shape[2]
    tm, tn = PROMPT_TILE, COL_TILE
    n_pt, n_j = rp // tm, n // tn
    blocks = (_nbytes((tm, d), F32) + _nbytes((rs, d), F32) + 2 * _nbytes((d, tn), F32)
              + _nbytes((tm, tn), BF16) + _nbytes((rs, tn), BF16))
    scratch = _nbytes((tm, d), BF16) + _nbytes((rs, d), BF16) + 2 * _nbytes((d, tn), BF16)
    w_spec = pl.BlockSpec((None, d, tn), lambda i, j: (layer, 0, j))
    return pl.pallas_call(
        functools.partial(_ffn_up_kernel, n_pt=n_pt),
        grid=(n_pt + 1, n_j),
        in_specs=[
            pl.BlockSpec((tm, d), _prompt_rows_map(n_pt)),
            pl.BlockSpec((rs, d), _const_map),
            pl.BlockSpec((None, 1, d), lambda i, j: (layer, 0, 0)),
            w_spec,
            w_spec,
        ],
        out_specs=[
            pl.BlockSpec((tm, tn), _prompt_tile_map(n_pt, n_j)),
            pl.BlockSpec((rs, tn), _sample_tile_map(n_pt)),
        ],
        out_shape=[jax.ShapeDtypeStruct((rp, n), BF16), jax.ShapeDtypeStruct((rs, n), BF16)],
        scratch_shapes=[pltpu.VMEM((tm, d), BF16), pltpu.VMEM((rs, d), BF16)],
        compiler_params=pltpu.CompilerParams(
            dimension_semantics=("arbitrary", "arbitrary"),
            vmem_limit_bytes=_vmem_limit(blocks, scratch)),
        name="ffn_up",
    )(xp, xs, g, wg, wu)


def _matmul_residual_kernel(ap_ref, as_ref, w_ref, xp_ref, xs_ref, op_ref, os_ref, *, n_pt):
    i = pl.program_id(0)

    @pl.when(i < n_pt)
    def _():
        op_ref[...] = xp_ref[...] + _dot(ap_ref[...], w_ref[...].astype(BF16))

    @pl.when(i == n_pt)
    def _():
        os_ref[...] = xs_ref[...] + _dot(as_ref[...], w_ref[...].astype(BF16))


def _matmul_residual(ap, as_, w, xp, xs, *, w_layer, tn):
    rp, k = ap.shape
    rs = as_.shape[0]
    n = w.shape[2]
    tm = PROMPT_TILE
    n_pt, n_j = rp // tm, n // tn
    blocks = (_nbytes((tm, k), BF16) + _nbytes((rs, k), BF16) + _nbytes((k, tn), F32)
              + 2 * _nbytes((tm, tn), F32) + 2 * _nbytes((rs, tn), F32))
    scratch = _nbytes((k, tn), BF16)
    return pl.pallas_call(
        functools.partial(_matmul_residual_kernel, n_pt=n_pt),
        grid=(n_pt + 1, n_j),
        in_specs=[
            pl.BlockSpec((tm, k), _prompt_rows_map(n_pt)),
            pl.BlockSpec((rs, k), _const_map),
            pl.BlockSpec((None, k, tn), lambda i, j: (w_layer, 0, j)),
            pl.BlockSpec((tm, tn), _prompt_tile_map(n_pt, n_j)),
            pl.BlockSpec((rs, tn), _sample_tile_map(n_pt)),
        ],
        out_specs=[
            pl.BlockSpec((tm, tn), _prompt_tile_map(n_pt, n_j)),
            pl.BlockSpec((rs, tn), _sample_tile_map(n_pt)),
        ],
        out_shape=[jax.ShapeDtypeStruct((rp, n), F32), jax.ShapeDtypeStruct((rs, n), F32)],
        compiler_params=pltpu.CompilerParams(
            dimension_semantics=("arbitrary", "arbitrary"),
            vmem_limit_bytes=_vmem_limit(blocks, scratch)),
        name="matmul_residual",
    )(ap, as_, w, xp, xs)


def _mem_kv_kernel(x_ref, g_ref, w_ref, k_ref, v_ref):
    h = _rmsnorm(x_ref[...], g_ref[...]).astype(BF16)
    kv = _dot(h, w_ref[...].astype(BF16))
    k_ref[...] = kv[:, :MEM_WIDTH]
    v_ref[...] = kv[:, MEM_WIDTH:]


def _mem_kv(mem, g, w):
    rows, d = mem.shape
    depth = w.shape[0]
    tm = 512
    out = jax.ShapeDtypeStruct((depth, rows, MEM_WIDTH), F32)
    blocks = (_nbytes((tm, d), F32) + _nbytes((d, 2 * MEM_WIDTH), F32) + 2 * _nbytes((tm, MEM_WIDTH), F32))
    return pl.pallas_call(
        _mem_kv_kernel,
        grid=(depth, rows // tm),
        in_specs=[
            pl.BlockSpec((tm, d), lambda l, i: (i, 0)),
            pl.BlockSpec((None, 1, d), lambda l, i: (l, 0, 0)),
            pl.BlockSpec((None, d, 2 * MEM_WIDTH), lambda l, i: (l, 0, 0)),
        ],
        out_specs=[
            pl.BlockSpec((None, tm, MEM_WIDTH), lambda l, i: (l, i, 0)),
            pl.BlockSpec((None, tm, MEM_WIDTH), lambda l, i: (l, i, 0)),
        ],
        out_shape=[out, out],
        compiler_params=pltpu.CompilerParams(
            dimension_semantics=("arbitrary", "arbitrary"),
            vmem_limit_bytes=_vmem_limit(blocks, _nbytes((d, 2 * MEM_WIDTH), BF16))),
        name="mem_kv",
    )(mem, g, w)


def _cross_attention_head(q, k, v):
    s = _dot_nt(q, k) * (MEM_HEAD_DIM ** -0.5)
    p = _softmax_rows(s).astype(BF16)
    return _dot(p, v)


def _conv_prompt_kernel(z_ref, mk_ref, mv_ref, cw_ref, mix_ref, st_ref, ext_ref, *, tiles_per_seq):
    s = pl.program_id(0)
    tq = z_ref.shape[0]

    @pl.when(s % tiles_per_seq == 0)
    def _():
        ext_ref[0:CARRY_ROWS, :] = jnp.zeros((CARRY_ROWS, TOKEN_WIDTH), F32)

    c = z_ref[:, TOKEN_WIDTH:2 * TOKEN_WIDTH].astype(F32)
    u = z_ref[:, 2 * TOKEN_WIDTH:3 * TOKEN_WIDTH].astype(F32)
    cu = c * u
    ext_ref[CARRY_ROWS:CARRY_ROWS + tq, :] = cu
    conv = (cw_ref[0:1, :] * ext_ref[CARRY_ROWS - 2:CARRY_ROWS - 2 + tq, :]
            + cw_ref[1:2, :] * ext_ref[CARRY_ROWS - 1:CARRY_ROWS - 1 + tq, :]
            + cw_ref[2:3, :] * cu)
    b = z_ref[:, 0:TOKEN_WIDTH].astype(F32)
    mix_ref[:, 0:TOKEN_WIDTH] = (b * conv).astype(BF16)
    st_ref[...] = ext_ref[CARRY_ROWS + tq - 2:CARRY_ROWS + tq, :]
    ext_ref[0:CARRY_ROWS, :] = ext_ref[tq:tq + CARRY_ROWS, :]

    for h in range(MEM_HEADS):
        lo, hi = h * MEM_HEAD_DIM, (h + 1) * MEM_HEAD_DIM
        q = z_ref[:, 3 * TOKEN_WIDTH + lo:3 * TOKEN_WIDTH + hi]
        o = _cross_attention_head(q, mk_ref[:, lo:hi].astype(BF16), mv_ref[:, lo:hi].astype(BF16))
        mix_ref[:, TOKEN_WIDTH + lo:TOKEN_WIDTH + hi] = o.astype(BF16)


def _conv_prompt(z, mk, mv, conv_w, *, layer, batch, seq):
    rows, zc = z.shape
    tq = CONV_ROWS
    tiles_per_seq = seq // tq
    blocks = (_nbytes((tq, zc), BF16) + 2 * _nbytes((N_MEM, MEM_WIDTH), F32) + _nbytes((tq, D_MODEL), BF16))
    scratch = _nbytes((tq + CARRY_ROWS, TOKEN_WIDTH), F32)
    return pl.pallas_call(
        functools.partial(_conv_prompt_kernel, tiles_per_seq=tiles_per_seq),
        grid=(batch * tiles_per_seq,),
        in_specs=[
            pl.BlockSpec((tq, zc), lambda s: (s, 0)),
            pl.BlockSpec((None, N_MEM, MEM_WIDTH), lambda s: (layer, s // tiles_per_seq, 0)),
            pl.BlockSpec((None, N_MEM, MEM_WIDTH), lambda s: (layer, s // tiles_per_seq, 0)),
            pl.BlockSpec((None, CONV_WIDTH, TOKEN_WIDTH), lambda s: (0, 0, 0)),
        ],
        out_specs=[
            pl.BlockSpec((tq, D_MODEL), lambda s: (s, 0)),
            pl.BlockSpec((None, CONV_WIDTH - 1, TOKEN_WIDTH), lambda s: (s // tiles_per_seq, 0, 0)),
        ],
        out_shape=[
            jax.ShapeDtypeStruct((rows, D_MODEL), BF16),
            jax.ShapeDtypeStruct((batch, CONV_WIDTH - 1, TOKEN_WIDTH), F32),
        ],
        scratch_shapes=[pltpu.VMEM((tq + CARRY_ROWS, TOKEN_WIDTH), F32)],
        compiler_params=pltpu.CompilerParams(
            dimension_semantics=("arbitrary",),
            vmem_limit_bytes=_vmem_limit(blocks, scratch + 6 * _nbytes((tq, TOKEN_WIDTH), F32))),
        name="conv_prompt",
    )(z, mk, mv, conv_w)


def _conv_sample_kernel(z_ref, st_ref, mk_ref, mv_ref, cw_ref, mix_ref, nst_ref, ext_ref, mixf_ref, *, dec_seq):
    t = dec_seq
    for n in range(SAMPLE_GROUP):
        r0, r1 = n * t, (n + 1) * t
        c = z_ref[r0:r1, TOKEN_WIDTH:2 * TOKEN_WIDTH].astype(F32)
        u = z_ref[r0:r1, 2 * TOKEN_WIDTH:3 * TOKEN_WIDTH].astype(F32)
        cu = c * u
        ext_ref[CARRY_ROWS - 2:CARRY_ROWS, :] = st_ref[n]
        ext_ref[CARRY_ROWS:CARRY_ROWS + t, :] = cu
        conv = (cw_ref[0:1, :] * ext_ref[CARRY_ROWS - 2:CARRY_ROWS - 2 + t, :]
                + cw_ref[1:2, :] * ext_ref[CARRY_ROWS - 1:CARRY_ROWS - 1 + t, :]
                + cw_ref[2:3, :] * cu)
        b = z_ref[r0:r1, 0:TOKEN_WIDTH].astype(F32)
        mixf_ref[r0:r1, 0:TOKEN_WIDTH] = b * conv
        nst_ref[n] = ext_ref[CARRY_ROWS + t - 2:CARRY_ROWS + t, :]
        for h in range(MEM_HEADS):
            lo, hi = h * MEM_HEAD_DIM, (h + 1) * MEM_HEAD_DIM
            q = z_ref[r0:r1, 3 * TOKEN_WIDTH + lo:3 * TOKEN_WIDTH + hi]
            o = _cross_attention_head(q, mk_ref[n, :, lo:hi].astype(BF16), mv_ref[n, :, lo:hi].astype(BF16))
            mixf_ref[r0:r1, TOKEN_WIDTH + lo:TOKEN_WIDTH + hi] = o
    mix_ref[...] = mixf_ref[...].astype(BF16)


def _conv_sample(z, state, mem_k, mem_v, conv_w, *, layer, dec_seq):
    rows, zc = z.shape
    dec_batch = state.shape[1]
    g = SAMPLE_GROUP
    gr = g * dec_seq
    blocks = (_nbytes((gr, zc), BF16) + 2 * _nbytes((g, CONV_WIDTH - 1, TOKEN_WIDTH), F32)
              + 2 * _nbytes((g, N_MEM, MEM_WIDTH), F32) + _nbytes((gr, D_MODEL), BF16))
    scratch = _nbytes((2 * CARRY_ROWS, TOKEN_WIDTH), F32) + _nbytes((gr, D_MODEL), F32)
    return pl.pallas_call(
        functools.partial(_conv_sample_kernel, dec_seq=dec_seq),
        grid=(dec_batch // g,),
        in_specs=[
            pl.BlockSpec((gr, zc), lambda i: (i, 0)),
            pl.BlockSpec((None, g, CONV_WIDTH - 1, TOKEN_WIDTH), lambda i: (0, i, 0, 0)),
            pl.BlockSpec((None, g, N_MEM, MEM_WIDTH), lambda i: (layer, i, 0, 0)),
            pl.BlockSpec((None, g, N_MEM, MEM_WIDTH), lambda i: (layer, i, 0, 0)),
            pl.BlockSpec((None, CONV_WIDTH, TOKEN_WIDTH), lambda i: (0, 0, 0)),
        ],
        out_specs=[
            pl.BlockSpec((gr, D_MODEL), lambda i: (i, 0)),
            pl.BlockSpec((g, CONV_WIDTH - 1, TOKEN_WIDTH), lambda i: (i, 0, 0)),
        ],
        out_shape=[
            jax.ShapeDtypeStruct((rows, D_MODEL), BF16),
            jax.ShapeDtypeStruct((dec_batch, CONV_WIDTH - 1, TOKEN_WIDTH), F32),
        ],
        scratch_shapes=[pltpu.VMEM((2 * CARRY_ROWS, TOKEN_WIDTH), F32), pltpu.VMEM((gr, D_MODEL), F32)],
        compiler_params=pltpu.CompilerParams(
            dimension_semantics=("arbitrary",),
            vmem_limit_bytes=_vmem_limit(blocks, scratch)),
        name="conv_sample",
    )(z, state, mem_k, mem_v, conv_w)


def _band_attention_head(q, k, v, sink, mask):
    s = _dot_nt(q, k) * (HEAD_DIM ** -0.5)
    s = jnp.where(mask, s, -jnp.inf)
    m = jnp.maximum(jnp.max(s, axis=-1, keepdims=True), sink)
    e = jnp.exp(s - m)
    p = e / (jnp.sum(e, axis=-1, keepdims=True) + jnp.exp(sink - m))
    return _dot(p.astype(BF16), v)


def _swa_prompt_kernel(sink_ref, zq_ref, zp_ref, mk_ref, mv_ref, mix_ref, *, blocks_per_seq):
    s = pl.program_id(0)
    k_off = TOKEN_WIDTH
    v_off = TOKEN_WIDTH + KV_WIDTH
    qm_off = TOKEN_WIDTH + 2 * KV_WIDTH
    has_prev = (s % blocks_per_seq) > 0
    row = lax.broadcasted_iota(jnp.int32, (WINDOW, 2 * WINDOW), 0)
    col = lax.broadcasted_iota(jnp.int32, (WINDOW, 2 * WINDOW), 1)
    mask = jnp.logical_or(jnp.logical_and(jnp.logical_and(col < WINDOW, col > row), has_prev),
                          jnp.logical_and(col >= WINDOW, col - WINDOW <= row))
    for kh in range(N_KV_HEADS):
        lo, hi = kh * HEAD_DIM, (kh + 1) * HEAD_DIM
        k = jnp.concatenate([zp_ref[:, lo:hi], zq_ref[:, k_off + lo:k_off + hi]], axis=0)
        v = jnp.concatenate([zp_ref[:, KV_WIDTH + lo:KV_WIDTH + hi], zq_ref[:, v_off + lo:v_off + hi]], axis=0)
        for g in range(GROUP):
            h = kh * GROUP + g
            q = zq_ref[:, h * HEAD_DIM:(h + 1) * HEAD_DIM]
            o = _band_attention_head(q, k, v, sink_ref[h], mask)
            mix_ref[:, h * HEAD_DIM:(h + 1) * HEAD_DIM] = o.astype(BF16)
    for h in range(MEM_HEADS):
        lo, hi = h * MEM_HEAD_DIM, (h + 1) * MEM_HEAD_DIM
        q = zq_ref[:, qm_off + lo:qm_off + hi]
        o = _cross_attention_head(q, mk_ref[:, lo:hi].astype(BF16), mv_ref[:, lo:hi].astype(BF16))
        mix_ref[:, TOKEN_WIDTH + lo:TOKEN_WIDTH + hi] = o.astype(BF16)


def _swa_prompt(z, mk, mv, sinks, *, layer, batch, seq):
    rows, zc = z.shape
    tq = WINDOW
    blocks_per_seq = seq // tq
    kv_col_block = TOKEN_WIDTH // (2 * KV_WIDTH)
    blocks = (_nbytes((tq, zc), BF16) + _nbytes((tq, 2 * KV_WIDTH), BF16)
              + 2 * _nbytes((N_MEM, MEM_WIDTH), F32) + _nbytes((tq, D_MODEL), BF16))
    return pl.pallas_call(
        functools.partial(_swa_prompt_kernel, blocks_per_seq=blocks_per_seq),
        grid=(batch * blocks_per_seq,),
        in_specs=[
            pl.BlockSpec(memory_space=pltpu.SMEM),
            pl.BlockSpec((tq, zc), lambda s: (s, 0)),
            pl.BlockSpec((tq, 2 * KV_WIDTH), lambda s: (jnp.maximum(s - 1, 0), kv_col_block)),
            pl.BlockSpec((None, N_MEM, MEM_WIDTH), lambda s: (layer, s // blocks_per_seq, 0)),
            pl.BlockSpec((None, N_MEM, MEM_WIDTH), lambda s: (layer, s // blocks_per_seq, 0)),
        ],
        out_specs=pl.BlockSpec((tq, D_MODEL), lambda s: (s, 0)),
        out_shape=jax.ShapeDtypeStruct((rows, D_MODEL), BF16),
        compiler_params=pltpu.CompilerParams(
            dimension_semantics=("arbitrary",),
            vmem_limit_bytes=_vmem_limit(blocks, 0)),
        name="swa_prompt",
    )(sinks, z, z, mk, mv)


def _swa_sample_kernel(sink_ref, z_ref, ck_ref, cv_ref, mk_ref, mv_ref,
                       mix_ref, nk_ref, nv_ref, knew_ref, vnew_ref, mixf_ref, *, dec_seq):
    t = dec_seq
    k_off = TOKEN_WIDTH
    v_off = TOKEN_WIDTH + KV_WIDTH
    qm_off = TOKEN_WIDTH + 2 * KV_WIDTH
    rows = GROUP * t
    qi = lax.broadcasted_iota(jnp.int32, (rows, 2 * WINDOW), 0) % t
    col = lax.broadcasted_iota(jnp.int32, (rows, 2 * WINDOW), 1)
    mask = jnp.logical_or(jnp.logical_and(col < WINDOW, col > qi),
                          jnp.logical_and(col >= WINDOW, col - WINDOW <= qi))
    knew_ref[...] = jnp.zeros((WINDOW, KV_WIDTH), F32)
    vnew_ref[...] = jnp.zeros((WINDOW, KV_WIDTH), F32)
    for n in range(SAMPLE_GROUP):
        r0, r1 = n * t, (n + 1) * t
        k_new = z_ref[r0:r1, k_off:k_off + KV_WIDTH].astype(F32)
        v_new = z_ref[r0:r1, v_off:v_off + KV_WIDTH].astype(F32)
        knew_ref[0:t, :] = k_new
        vnew_ref[0:t, :] = v_new
        nk_ref[n, 0:WINDOW - t, :] = ck_ref[n, t:WINDOW, :]
        nv_ref[n, 0:WINDOW - t, :] = cv_ref[n, t:WINDOW, :]
        nk_ref[n, WINDOW - t:WINDOW, :] = k_new
        nv_ref[n, WINDOW - t:WINDOW, :] = v_new
        for kh in range(N_KV_HEADS):
            lo, hi = kh * HEAD_DIM, (kh + 1) * HEAD_DIM
            k = jnp.concatenate([ck_ref[n, :, lo:hi], knew_ref[:, lo:hi]], axis=0).astype(BF16)
            v = jnp.concatenate([cv_ref[n, :, lo:hi], vnew_ref[:, lo:hi]], axis=0).astype(BF16)
            q = jnp.concatenate(
                [z_ref[r0:r1, (kh * GROUP + g) * HEAD_DIM:(kh * GROUP + g + 1) * HEAD_DIM].astype(F32)
                 for g in range(GROUP)], axis=0).astype(BF16)
            sink = jnp.concatenate(
                [jnp.full((t, 1), sink_ref[kh * GROUP + g], F32) for g in range(GROUP)], axis=0)
            o = _band_attention_head(q, k, v, sink, mask)
            for g in range(GROUP):
                h = kh * GROUP + g
                mixf_ref[r0:r1, h * HEAD_DIM:(h + 1) * HEAD_DIM] = o[g * t:(g + 1) * t, :]
        for h in range(MEM_HEADS):
            lo, hi = h * MEM_HEAD_DIM, (h + 1) * MEM_HEAD_DIM
            q = z_ref[r0:r1, qm_off + lo:qm_off + hi]
            o = _cross_attention_head(q, mk_ref[n, :, lo:hi].astype(BF16), mv_ref[n, :, lo:hi].astype(BF16))
            mixf_ref[r0:r1, TOKEN_WIDTH + lo:TOKEN_WIDTH + hi] = o
    mix_ref[...] = mixf_ref[...].astype(BF16)


def _swa_sample(z, cache_k, cache_v, mem_k, mem_v, sinks, *, layer, dec_seq):
    rows, zc = z.shape
    dec_batch = cache_k.shape[0]
    g = SAMPLE_GROUP
    gr = g * dec_seq
    win = jax.ShapeDtypeStruct((dec_batch, WINDOW, KV_WIDTH), F32)
    blocks = (_nbytes((gr, zc), BF16) + 4 * _nbytes((g, WINDOW, KV_WIDTH), F32)
              + 2 * _nbytes((g, N_MEM, MEM_WIDTH), F32) + _nbytes((gr, D_MODEL), BF16))
    scratch = 2 * _nbytes((WINDOW, KV_WIDTH), F32) + _nbytes((gr, D_MODEL), F32)
    return pl.pallas_call(
        functools.partial(_swa_sample_kernel, dec_seq=dec_seq),
        grid=(dec_batch // g,),
        in_specs=[
            pl.BlockSpec(memory_space=pltpu.SMEM),
            pl.BlockSpec((gr, zc), lambda i: (i, 0)),
            pl.BlockSpec((g, WINDOW, KV_WIDTH), lambda i: (i, 0, 0)),
            pl.BlockSpec((g, WINDOW, KV_WIDTH), lambda i: (i, 0, 0)),
            pl.BlockSpec((None, g, N_MEM, MEM_WIDTH), lambda i: (layer, i, 0, 0)),
            pl.BlockSpec((None, g, N_MEM, MEM_WIDTH), lambda i: (layer, i, 0, 0)),
        ],
        out_specs=[
            pl.BlockSpec((gr, D_MODEL), lambda i: (i, 0)),
            pl.BlockSpec((g, WINDOW, KV_WIDTH), lambda i: (i, 0, 0)),
            pl.BlockSpec((g, WINDOW, KV_WIDTH), lambda i: (i, 0, 0)),
        ],
        out_shape=[jax.ShapeDtypeStruct((rows, D_MODEL), BF16), win, win],
        scratch_shapes=[pltpu.VMEM((WINDOW, KV_WIDTH), F32), pltpu.VMEM((WINDOW, KV_WIDTH), F32),
                        pltpu.VMEM((gr, D_MODEL), F32)],
        compiler_params=pltpu.CompilerParams(
            dimension_semantics=("arbitrary",),
            vmem_limit_bytes=_vmem_limit(blocks, scratch)),
        name="swa_sample",
    )(sinks, z, cache_k, cache_v, mem_k, mem_v)


def _final_norm_kernel(xp_ref, xs_ref, g_ref, yp_ref, ys_ref, *, n_pt):
    s = pl.program_id(0)

    @pl.when(s < n_pt)
    def _():
        yp_ref[...] = _rmsnorm(xp_ref[...], g_ref[...])

    @pl.when(s == n_pt)
    def _():
        ys_ref[...] = _rmsnorm(xs_ref[...], g_ref[...])


def _final_norm(xp, xs, g):
    rp, d = xp.shape
    rs = xs.shape[0]
    tm = NORM_ROWS
    n_pt = rp // tm
    return pl.pallas_call(
        functools.partial(_final_norm_kernel, n_pt=n_pt),
        grid=(n_pt + 1,),
        in_specs=[
            pl.BlockSpec((tm, d), lambda s: (jnp.minimum(s, n_pt - 1), 0)),
            pl.BlockSpec((rs, d), lambda s: (0, 0)),
            pl.BlockSpec((1, d), lambda s: (0, 0)),
        ],
        out_specs=[
            pl.BlockSpec((tm, d), lambda s: (jnp.minimum(s, n_pt - 1), 0)),
            pl.BlockSpec((rs, d), lambda s: (0, 0)),
        ],
        out_shape=[jax.ShapeDtypeStruct((rp, d), F32), jax.ShapeDtypeStruct((rs, d), F32)],
        compiler_params=pltpu.CompilerParams(
            dimension_semantics=("arbitrary",),
            vmem_limit_bytes=_vmem_limit(2 * _nbytes((tm, d), F32) + 2 * _nbytes((rs, d), F32), 0)),
        name="final_norm",
    )(xp, xs, g.reshape(1, d))


def kernel(x_prompt, x_sample, mem_prompt, state_conv, cache_win_k, cache_win_v, cache_mem_k, cache_mem_v,
           norm_mix, norm_mem, w_mem_kv, norm_ffn, w_gate, w_up, w_down,
           conv_w_in, conv_w, conv_w_out, attn_w_in, attn_sinks, attn_w_out, norm_final):
    batch, seq, d = x_prompt.shape
    dec_batch, dec_seq, _ = x_sample.shape
    depth = norm_mix.shape[0]
    d_ff = w_gate.shape[2]
    prompt_rows = batch * seq
    sample_rows = dec_batch * dec_seq
    assert d == D_MODEL and depth == 2 and seq % CONV_ROWS == 0 and seq % WINDOW == 0
    assert prompt_rows % PROMPT_TILE == 0 and prompt_rows % NORM_ROWS == 0 and dec_batch % SAMPLE_GROUP == 0
    assert dec_seq == V7X_SUBLANES and d_ff % COL_TILE == 0 and d % COL_TILE == 0 and d % DOWN_COL_TILE == 0

    xp = x_prompt.reshape(prompt_rows, d)
    xs = x_sample.reshape(sample_rows, d)
    mem = mem_prompt.reshape(batch * N_MEM, d)
    mem_k_s = cache_mem_k.reshape(depth, dec_batch, N_MEM, MEM_WIDTH)
    mem_v_s = cache_mem_v.reshape(depth, dec_batch, N_MEM, MEM_WIDTH)
    g_mix = norm_mix.reshape(depth, 1, d)
    g_ffn = norm_ffn.reshape(depth, 1, d)

    mk, mv = _mem_kv(mem, norm_mem.reshape(depth, 1, d), w_mem_kv)

    zp, zs = _norm_matmul(xp, xs, g_mix, conv_w_in, layer=0, w_layer=0)
    mix_s, conv_s = _conv_sample(zs, state_conv, mem_k_s, mem_v_s, conv_w, layer=0, dec_seq=dec_seq)
    mix_p, conv_p = _conv_prompt(zp, mk, mv, conv_w, layer=0, batch=batch, seq=seq)
    xp, xs = _matmul_residual(mix_p, mix_s, conv_w_out, xp, xs, w_layer=0, tn=COL_TILE)
    ap, as_ = _ffn_up(xp, xs, g_ffn, w_gate, w_up, layer=0)
    xp, xs = _matmul_residual(ap, as_, w_down, xp, xs, w_layer=0, tn=DOWN_COL_TILE)

    zp, zs = _norm_matmul(xp, xs, g_mix, attn_w_in, layer=1, w_layer=0)
    sinks = attn_sinks[0]
    mix_s, win_k_s, win_v_s = _swa_sample(
        zs, cache_win_k[0].reshape(dec_batch, WINDOW, KV_WIDTH), cache_win_v[0].reshape(dec_batch, WINDOW, KV_WIDTH),
        mem_k_s, mem_v_s, sinks, layer=1, dec_seq=dec_seq)
    mix_p = _swa_prompt(zp, mk, mv, sinks, layer=1, batch=batch, seq=seq)
    xp, xs = _matmul_residual(mix_p, mix_s, attn_w_out, xp, xs, w_layer=0, tn=COL_TILE)
    ap, as_ = _ffn_up(xp, xs, g_ffn, w_gate, w_up, layer=1)
    xp, xs = _matmul_residual(ap, as_, w_down, xp, xs, w_layer=1, tn=DOWN_COL_TILE)

    y_prompt, y_sample = _final_norm(xp, xs, norm_final)

    win_p = zp.reshape(batch, seq, -1)[:, seq - WINDOW:, TOKEN_WIDTH:TOKEN_WIDTH + 2 * KV_WIDTH].astype(F32)
    kv_shape = (1, -1, WINDOW, N_KV_HEADS, HEAD_DIM)
    mem_shape = (depth, batch, N_MEM, MEM_HEADS, MEM_HEAD_DIM)
    return (y_prompt.reshape(batch, seq, d),
            y_sample.reshape(dec_batch, dec_seq, d),
            conv_p[None],
            conv_s[None],
            win_p[..., :KV_WIDTH].reshape(kv_shape),
            win_p[..., KV_WIDTH:].reshape(kv_shape),
            win_k_s.reshape(kv_shape),
            win_v_s.reshape(kv_shape),
            mk.reshape(mem_shape),
            mv.reshape(mem_shape))
```

```python
import functools

import jax
import jax.numpy as jnp
from jax import lax
from jax.experimental import pallas as pl
from jax.experimental.pallas import tpu as pltpu

F32 = jnp.float32
BF16 = jnp.bfloat16

D_MODEL = 2048
N_MEM = 256
MEM_HEADS = 4
MEM_WIDTH = D_MODEL // 4
MEM_HEAD_DIM = MEM_WIDTH // MEM_HEADS
TOKEN_WIDTH = D_MODEL - MEM_WIDTH
CONV_WIDTH = 3
WINDOW = 128
HEAD_DIM = 64
N_HEADS = TOKEN_WIDTH // HEAD_DIM
N_KV_HEADS = 4
GROUP = N_HEADS // N_KV_HEADS
KV_WIDTH = N_KV_HEADS * HEAD_DIM
EPS = 1e-6

V7X_VMEM_BYTES = 64 * 1024 * 1024
V7X_SUBLANES = 8

PROMPT_TILE = 1024
COL_TILE = 512
DOWN_COL_TILE = 256
FFN_ROW_TILE = 2048
FFN_COL_TILE = 256
CONV_ROWS = 512
NORM_ROWS = 512
SAMPLE_GROUP = 8
CARRY_ROWS = V7X_SUBLANES


def _nbytes(shape, dtype):
    n = 1
    for s in shape:
        n *= s
    return n * jnp.dtype(dtype).itemsize


def _vmem_limit(block_bytes, scratch_bytes):
    need = 2 * block_bytes + scratch_bytes
    return int(min(need + max(need // 4, 8 << 20), V7X_VMEM_BYTES - (6 << 20)))


def _rmsnorm(x, g):
    r = lax.rsqrt(jnp.mean(x * x, axis=-1, keepdims=True) + EPS)
    return (x * r) * g


def _softmax_rows(s):
    m = jnp.max(s, axis=-1, keepdims=True)
    e = jnp.exp(s - m)
    return e / jnp.sum(e, axis=-1, keepdims=True)


def _dot(a, b):
    return jnp.dot(a, b, preferred_element_type=F32)


def _dot_nt(a, b):
    return lax.dot_general(a, b, (((1,), (1,)), ((), ())), preferred_element_type=F32)


def _prompt_rows_map(n_pt):
    return lambda i, j: (jnp.minimum(i, n_pt - 1), 0)


def _prompt_tile_map(n_pt, n_j):
    return lambda i, j: (jnp.minimum(i, n_pt - 1), jnp.where(i < n_pt, j, n_j - 1))


def _sample_tile_map(n_pt):
    return lambda i, j: (0, jnp.where(i < n_pt, 0, j))


def _const_map(i, j):
    return (0, 0)


def _norm_matmul_kernel(xp_ref, xs_ref, g_ref, w_ref, zp_ref, zs_ref, hp_ref, hs_ref, *, n_pt):
    i, j = pl.program_id(0), pl.program_id(1)

    @pl.when(i < n_pt)
    def _():
        @pl.when(j == 0)
        def _():
            hp_ref[...] = _rmsnorm(xp_ref[...], g_ref[...]).astype(BF16)

        zp_ref[...] = _dot(hp_ref[...], w_ref[...].astype(BF16)).astype(zp_ref.dtype)

    @pl.when(i == n_pt)
    def _():
        @pl.when(j == 0)
        def _():
            hs_ref[...] = _rmsnorm(xs_ref[...], g_ref[...]).astype(BF16)

        zs_ref[...] = _dot(hs_ref[...], w_ref[...].astype(BF16)).astype(zs_ref.dtype)


def _norm_matmul(xp, xs, g, w, *, layer, w_layer):
    rp, d = xp.shape
    rs = xs.shape[0]
    n = w.shape[2]
    tm, tn = PROMPT_TILE, COL_TILE
    n_pt, n_j = rp // tm, n // tn
    blocks = (_nbytes((tm, d), F32) + _nbytes((rs, d), F32) + _nbytes((d, tn), F32)
              + _nbytes((tm, tn), BF16) + _nbytes((rs, tn), BF16))
    scratch = _nbytes((tm, d), BF16) + _nbytes((rs, d), BF16) + _nbytes((d, tn), BF16)
    return pl.pallas_call(
        functools.partial(_norm_matmul_kernel, n_pt=n_pt),
        grid=(n_pt + 1, n_j),
        in_specs=[
            pl.BlockSpec((tm, d), _prompt_rows_map(n_pt)),
            pl.BlockSpec((rs, d), _const_map),
            pl.BlockSpec((None, 1, d), lambda i, j: (layer, 0, 0)),
            pl.BlockSpec((None, d, tn), lambda i, j: (w_layer, 0, j)),
        ],
        out_specs=[
            pl.BlockSpec((tm, tn), _prompt_tile_map(n_pt, n_j)),
            pl.BlockSpec((rs, tn), _sample_tile_map(n_pt)),
        ],
        out_shape=[jax.ShapeDtypeStruct((rp, n), BF16), jax.ShapeDtypeStruct((rs, n), BF16)],
        scratch_shapes=[pltpu.VMEM((tm, d), BF16), pltpu.VMEM((rs, d), BF16)],
        compiler_params=pltpu.CompilerParams(
            dimension_semantics=("arbitrary", "arbitrary"),
            vmem_limit_bytes=_vmem_limit(blocks, scratch)),
        name="norm_matmul",
    )(xp, xs, g, w)


def _swiglu(h, wg, wu):
    gate = _dot(h, wg)
    up = _dot(h, wu)
    return (gate * jax.nn.sigmoid(gate) * up).astype(BF16)


def _ffn_up_kernel(hp_ref, hs_ref, wg_ref, wu_ref, ap_ref, as_ref, *, n_pt):
    i = pl.program_id(0)

    @pl.when(i < n_pt)
    def _():
        ap_ref[...] = _swiglu(hp_ref[...], wg_ref[...].astype(BF16), wu_ref[...].astype(BF16))

    @pl.when(i == n_pt)
    def _():
        as_ref[...] = _swiglu(hs_ref[...], wg_ref[...].astype(BF16), wu_ref[...].astype(BF16))


def _ffn_up(hp, hs, wg, wu, *, layer):
    rp, d = hp.shape
    rs = hs.shape[0]
    n = wg.shape[2]
    tm, tn = FFN_ROW_TILE, FFN_COL_TILE
    n_pt, n_j = rp // tm, n // tn
    blocks = (_nbytes((tm, d), BF16) + _nbytes((rs, d), BF16) + 2 * _nbytes((d, tn), F32)
              + _nbytes((tm, tn), BF16) + _nbytes((rs, tn), BF16))
    temps = 2 * _nbytes((d, tn), BF16) + 3 * _nbytes((tm, tn), F32)
    w_spec = pl.BlockSpec((None, d, tn), lambda i, j: (layer, 0, j))
    return pl.pallas_call(
        functools.partial(_ffn_up_kernel, n_pt=n_pt),
        grid=(n_pt + 1, n_j),
        in_specs=[
            pl.BlockSpec((tm, d), _prompt_rows_map(n_pt)),
            pl.BlockSpec((rs, d), _const_map),
            w_spec,
            w_spec,
        ],
        out_specs=[
            pl.BlockSpec((tm, tn), _prompt_tile_map(n_pt, n_j)),
            pl.BlockSpec((rs, tn), _sample_tile_map(n_pt)),
        ],
        out_shape=[jax.ShapeDtypeStruct((rp, n), BF16), jax.ShapeDtypeStruct((rs, n), BF16)],
        compiler_params=pltpu.CompilerParams(
            dimension_semantics=("arbitrary", "arbitrary"),
            vmem_limit_bytes=_vmem_limit(blocks, temps)),
        name="ffn_up",
    )(hp, hs, wg, wu)


def _out_proj_kernel(ap_ref, as_ref, w_ref, xp_ref, xs_ref, g_ref, op_ref, os_ref, hp_ref, hs_ref, wb_ref, *, n_pt):
    s = pl.program_id(0)

    @pl.when(s == 0)
    def _():
        wb_ref[...] = w_ref[...].astype(BF16)

    @pl.when(s < n_pt)
    def _():
        x = xp_ref[...] + _dot(ap_ref[...], wb_ref[...])
        op_ref[...] = x
        hp_ref[...] = _rmsnorm(x, g_ref[...]).astype(BF16)

    @pl.when(s == n_pt)
    def _():
        x = xs_ref[...] + _dot(as_ref[...], wb_ref[...])
        os_ref[...] = x
        hs_ref[...] = _rmsnorm(x, g_ref[...]).astype(BF16)


def _out_proj(ap, as_, w, xp, xs, g, *, layer):
    rp, k = ap.shape
    rs = as_.shape[0]
    n = w.shape[2]
    tm = rs
    n_pt = rp // tm
    rows_map = lambda s: (jnp.minimum(s, n_pt - 1), 0)
    once = dict(pipeline_mode=pl.Buffered(1))
    blocks = _nbytes((tm, k), BF16) + 2 * _nbytes((tm, n), F32) + _nbytes((tm, n), BF16)
    resident = _nbytes((k, n), F32) + _nbytes((k, n), BF16) + blocks
    return pl.pallas_call(
        functools.partial(_out_proj_kernel, n_pt=n_pt),
        grid=(n_pt + 1,),
        in_specs=[
            pl.BlockSpec((tm, k), rows_map),
            pl.BlockSpec((rs, k), lambda s: (0, 0), **once),
            pl.BlockSpec((None, k, n), lambda s: (0, 0, 0), **once),
            pl.BlockSpec((tm, n), rows_map),
            pl.BlockSpec((rs, n), lambda s: (0, 0), **once),
            pl.BlockSpec((None, 1, n), lambda s: (layer, 0, 0)),
        ],
        out_specs=[
            pl.BlockSpec((tm, n), rows_map),
            pl.BlockSpec((rs, n), lambda s: (0, 0)),
            pl.BlockSpec((tm, n), rows_map),
            pl.BlockSpec((rs, n), lambda s: (0, 0)),
        ],
        out_shape=[jax.ShapeDtypeStruct((rp, n), F32), jax.ShapeDtypeStruct((rs, n), F32),
                   jax.ShapeDtypeStruct((rp, n), BF16), jax.ShapeDtypeStruct((rs, n), BF16)],
        scratch_shapes=[pltpu.VMEM((k, n), BF16)],
        compiler_params=pltpu.CompilerParams(
            dimension_semantics=("arbitrary",),
            vmem_limit_bytes=_vmem_limit(blocks, resident + 2 * _nbytes((tm, n), F32))),
        name="out_proj",
    )(ap, as_, w, xp, xs, g)


def _matmul_residual_kernel(ap_ref, as_ref, w_ref, xp_ref, xs_ref, op_ref, os_ref, *, n_pt):
    i = pl.program_id(0)

    @pl.when(i < n_pt)
    def _():
        op_ref[...] = xp_ref[...] + _dot(ap_ref[...], w_ref[...].astype(BF16))

    @pl.when(i == n_pt)
    def _():
        os_ref[...] = xs_ref[...] + _dot(as_ref[...], w_ref[...].astype(BF16))


def _matmul_residual(ap, as_, w, xp, xs, *, w_layer, tn):
    rp, k = ap.shape
    rs = as_.shape[0]
    n = w.shape[2]
    tm = PROMPT_TILE
    n_pt, n_j = rp // tm, n // tn
    blocks = (_nbytes((tm, k), BF16) + _nbytes((rs, k), BF16) + _nbytes((k, tn), F32)
              + 2 * _nbytes((tm, tn), F32) + 2 * _nbytes((rs, tn), F32))
    scratch = _nbytes((k, tn), BF16)
    return pl.pallas_call(
        functools.partial(_matmul_residual_kernel, n_pt=n_pt),
        grid=(n_pt + 1, n_j),
        in_specs=[
            pl.BlockSpec((tm, k), _prompt_rows_map(n_pt)),
            pl.BlockSpec((rs, k), _const_map),
            pl.BlockSpec((None, k, tn), lambda i, j: (w_layer, 0, j)),
            pl.BlockSpec((tm, tn), _prompt_tile_map(n_pt, n_j)),
            pl.BlockSpec((rs, tn), _sample_tile_map(n_pt)),
        ],
        out_specs=[
            pl.BlockSpec((tm, tn), _prompt_tile_map(n_pt, n_j)),
            pl.BlockSpec((rs, tn), _sample_tile_map(n_pt)),
        ],
        out_shape=[jax.ShapeDtypeStruct((rp, n), F32), jax.ShapeDtypeStruct((rs, n), F32)],
        compiler_params=pltpu.CompilerParams(
            dimension_semantics=("arbitrary", "arbitrary"),
            vmem_limit_bytes=_vmem_limit(blocks, scratch)),
        name="matmul_residual",
    )(ap, as_, w, xp, xs)


def _mem_kv_kernel(x_ref, g_ref, w_ref, k_ref, v_ref):
    h = _rmsnorm(x_ref[...], g_ref[...]).astype(BF16)
    kv = _dot(h, w_ref[...].astype(BF16))
    k_ref[...] = kv[:, :MEM_WIDTH]
    v_ref[...] = kv[:, MEM_WIDTH:]


def _mem_kv(mem, g, w):
    rows, d = mem.shape
    depth = w.shape[0]
    tm = 512
    out = jax.ShapeDtypeStruct((depth, rows, MEM_WIDTH), F32)
    blocks = (_nbytes((tm, d), F32) + _nbytes((d, 2 * MEM_WIDTH), F32) + 2 * _nbytes((tm, MEM_WIDTH), F32))
    return pl.pallas_call(
        _mem_kv_kernel,
        grid=(depth, rows // tm),
        in_specs=[
            pl.BlockSpec((tm, d), lambda l, i: (i, 0)),
            pl.BlockSpec((None, 1, d), lambda l, i: (l, 0, 0)),
            pl.BlockSpec((None, d, 2 * MEM_WIDTH), lambda l, i: (l, 0, 0)),
        ],
        out_specs=[
            pl.BlockSpec((None, tm, MEM_WIDTH), lambda l, i: (l, i, 0)),
            pl.BlockSpec((None, tm, MEM_WIDTH), lambda l, i: (l, i, 0)),
        ],
        out_shape=[out, out],
        compiler_params=pltpu.CompilerParams(
            dimension_semantics=("arbitrary", "arbitrary"),
            vmem_limit_bytes=_vmem_limit(blocks, _nbytes((d, 2 * MEM_WIDTH), BF16))),
        name="mem_kv",
    )(mem, g, w)


def _cross_attention_head(q, k, v):
    s = _dot_nt(q, k) * (MEM_HEAD_DIM ** -0.5)
    p = _softmax_rows(s).astype(BF16)
    return _dot(p, v)


def _conv_prompt_kernel(z_ref, mk_ref, mv_ref, cw_ref, mix_ref, st_ref, ext_ref, *, tiles_per_seq):
    s = pl.program_id(0)
    tq = z_ref.shape[0]

    @pl.when(s % tiles_per_seq == 0)
    def _():
        ext_ref[0:CARRY_ROWS, :] = jnp.zeros((CARRY_ROWS, TOKEN_WIDTH), F32)

    c = z_ref[:, TOKEN_WIDTH:2 * TOKEN_WIDTH].astype(F32)
    u = z_ref[:, 2 * TOKEN_WIDTH:3 * TOKEN_WIDTH].astype(F32)
    cu = c * u
    ext_ref[CARRY_ROWS:CARRY_ROWS + tq, :] = cu
    conv = (cw_ref[0:1, :] * ext_ref[CARRY_ROWS - 2:CARRY_ROWS - 2 + tq, :]
            + cw_ref[1:2, :] * ext_ref[CARRY_ROWS - 1:CARRY_ROWS - 1 + tq, :]
            + cw_ref[2:3, :] * cu)
    b = z_ref[:, 0:TOKEN_WIDTH].astype(F32)
    mix_ref[:, 0:TOKEN_WIDTH] = (b * conv).astype(BF16)
    st_ref[...] = ext_ref[CARRY_ROWS + tq - 2:CARRY_ROWS + tq, :]
    ext_ref[0:CARRY_ROWS, :] = ext_ref[tq:tq + CARRY_ROWS, :]

    for h in range(MEM_HEADS):
        lo, hi = h * MEM_HEAD_DIM, (h + 1) * MEM_HEAD_DIM
        q = z_ref[:, 3 * TOKEN_WIDTH + lo:3 * TOKEN_WIDTH + hi]
        o = _cross_attention_head(q, mk_ref[:, lo:hi].astype(BF16), mv_ref[:, lo:hi].astype(BF16))
        mix_ref[:, TOKEN_WIDTH + lo:TOKEN_WIDTH + hi] = o.astype(BF16)


def _conv_prompt(z, mk, mv, conv_w, *, layer, batch, seq):
    rows, zc = z.shape
    tq = CONV_ROWS
    tiles_per_seq = seq // tq
    blocks = (_nbytes((tq, zc), BF16) + 2 * _nbytes((N_MEM, MEM_WIDTH), F32) + _nbytes((tq, D_MODEL), BF16))
    scratch = _nbytes((tq + CARRY_ROWS, TOKEN_WIDTH), F32)
    return pl.pallas_call(
        functools.partial(_conv_prompt_kernel, tiles_per_seq=tiles_per_seq),
        grid=(batch * tiles_per_seq,),
        in_specs=[
            pl.BlockSpec((tq, zc), lambda s: (s, 0)),
            pl.BlockSpec((None, N_MEM, MEM_WIDTH), lambda s: (layer, s // tiles_per_seq, 0)),
            pl.BlockSpec((None, N_MEM, MEM_WIDTH), lambda s: (layer, s // tiles_per_seq, 0)),
            pl.BlockSpec((None, CONV_WIDTH, TOKEN_WIDTH), lambda s: (0, 0, 0)),
        ],
        out_specs=[
            pl.BlockSpec((tq, D_MODEL), lambda s: (s, 0)),
            pl.BlockSpec((None, CONV_WIDTH - 1, TOKEN_WIDTH), lambda s: (s // tiles_per_seq, 0, 0)),
        ],
        out_shape=[
            jax.ShapeDtypeStruct((rows, D_MODEL), BF16),
            jax.ShapeDtypeStruct((batch, CONV_WIDTH - 1, TOKEN_WIDTH), F32),
        ],
        scratch_shapes=[pltpu.VMEM((tq + CARRY_ROWS, TOKEN_WIDTH), F32)],
        compiler_params=pltpu.CompilerParams(
            dimension_semantics=("arbitrary",),
            vmem_limit_bytes=_vmem_limit(blocks, scratch + 6 * _nbytes((tq, TOKEN_WIDTH), F32))),
        name="conv_prompt",
    )(z, mk, mv, conv_w)


def _conv_sample_kernel(z_ref, st_ref, mk_ref, mv_ref, cw_ref, mix_ref, nst_ref, ext_ref, mixf_ref, *, dec_seq):
    t = dec_seq
    for n in range(SAMPLE_GROUP):
        r0, r1 = n * t, (n + 1) * t
        c = z_ref[r0:r1, TOKEN_WIDTH:2 * TOKEN_WIDTH].astype(F32)
        u = z_ref[r0:r1, 2 * TOKEN_WIDTH:3 * TOKEN_WIDTH].astype(F32)
        cu = c * u
        ext_ref[CARRY_ROWS - 2:CARRY_ROWS, :] = st_ref[n]
        ext_ref[CARRY_ROWS:CARRY_ROWS + t, :] = cu
        conv = (cw_ref[0:1, :] * ext_ref[CARRY_ROWS - 2:CARRY_ROWS - 2 + t, :]
                + cw_ref[1:2, :] * ext_ref[CARRY_ROWS - 1:CARRY_ROWS - 1 + t, :]
                + cw_ref[2:3, :] * cu)
        b = z_ref[r0:r1, 0:TOKEN_WIDTH].astype(F32)
        mixf_ref[r0:r1, 0:TOKEN_WIDTH] = b * conv
        nst_ref[n] = ext_ref[CARRY_ROWS + t - 2:CARRY_ROWS + t, :]
        for h in range(MEM_HEADS):
            lo, hi = h * MEM_HEAD_DIM, (h + 1) * MEM_HEAD_DIM
            q = z_ref[r0:r1, 3 * TOKEN_WIDTH + lo:3 * TOKEN_WIDTH + hi]
            o = _cross_attention_head(q, mk_ref[n, :, lo:hi].astype(BF16), mv_ref[n, :, lo:hi].astype(BF16))
            mixf_ref[r0:r1, TOKEN_WIDTH + lo:TOKEN_WIDTH + hi] = o
    mix_ref[...] = mixf_ref[...].astype(BF16)


def _conv_sample(z, state, mem_k, mem_v, conv_w, *, layer, dec_seq):
    rows, zc = z.shape
    dec_batch = state.shape[1]
    g = SAMPLE_GROUP
    gr = g * dec_seq
    blocks = (_nbytes((gr, zc), BF16) + 2 * _nbytes((g, CONV_WIDTH - 1, TOKEN_WIDTH), F32)
              + 2 * _nbytes((g, N_MEM, MEM_WIDTH), F32) + _nbytes((gr, D_MODEL), BF16))
    scratch = _nbytes((2 * CARRY_ROWS, TOKEN_WIDTH), F32) + _nbytes((gr, D_MODEL), F32)
    return pl.pallas_call(
        functools.partial(_conv_sample_kernel, dec_seq=dec_seq),
        grid=(dec_batch // g,),
        in_specs=[
            pl.BlockSpec((gr, zc), lambda i: (i, 0)),
            pl.BlockSpec((None, g, CONV_WIDTH - 1, TOKEN_WIDTH), lambda i: (0, i, 0, 0)),
            pl.BlockSpec((None, g, N_MEM, MEM_WIDTH), lambda i: (layer, i, 0, 0)),
            pl.BlockSpec((None, g, N_MEM, MEM_WIDTH), lambda i: (layer, i, 0, 0)),
            pl.BlockSpec((None, CONV_WIDTH, TOKEN_WIDTH), lambda i: (0, 0, 0)),
        ],
        out_specs=[
            pl.BlockSpec((gr, D_MODEL), lambda i: (i, 0)),
            pl.BlockSpec((g, CONV_WIDTH - 1, TOKEN_WIDTH), lambda i: (i, 0, 0)),
        ],
        out_shape=[
            jax.ShapeDtypeStruct((rows, D_MODEL), BF16),
            jax.ShapeDtypeStruct((dec_batch, CONV_WIDTH - 1, TOKEN_WIDTH), F32),
        ],
        scratch_shapes=[pltpu.VMEM((2 * CARRY_ROWS, TOKEN_WIDTH), F32), pltpu.VMEM((gr, D_MODEL), F32)],
        compiler_params=pltpu.CompilerParams(
            dimension_semantics=("arbitrary",),
            vmem_limit_bytes=_vmem_limit(blocks, scratch)),
        name="conv_sample",
    )(z, state, mem_k, mem_v, conv_w)


def _band_attention_head(q, k, v, sink, mask):
    s = _dot_nt(q, k) * (HEAD_DIM ** -0.5)
    s = jnp.where(mask, s, -jnp.inf)
    m = jnp.maximum(jnp.max(s, axis=-1, keepdims=True), sink)
    e = jnp.exp(s - m)
    p = e / (jnp.sum(e, axis=-1, keepdims=True) + jnp.exp(sink - m))
    return _dot(p.astype(BF16), v)


def _swa_prompt_kernel(sink_ref, zq_ref, zp_ref, mk_ref, mv_ref, mix_ref, *, blocks_per_seq):
    s = pl.program_id(0)
    k_off = TOKEN_WIDTH
    v_off = TOKEN_WIDTH + KV_WIDTH
    qm_off = TOKEN_WIDTH + 2 * KV_WIDTH
    has_prev = (s % blocks_per_seq) > 0
    row = lax.broadcasted_iota(jnp.int32, (WINDOW, 2 * WINDOW), 0)
    col = lax.broadcasted_iota(jnp.int32, (WINDOW, 2 * WINDOW), 1)
    mask = jnp.logical_or(jnp.logical_and(jnp.logical_and(col < WINDOW, col > row), has_prev),
                          jnp.logical_and(col >= WINDOW, col - WINDOW <= row))
    for kh in range(N_KV_HEADS):
        lo, hi = kh * HEAD_DIM, (kh + 1) * HEAD_DIM
        k = jnp.concatenate([zp_ref[:, lo:hi], zq_ref[:, k_off + lo:k_off + hi]], axis=0)
        v = jnp.concatenate([zp_ref[:, KV_WIDTH + lo:KV_WIDTH + hi], zq_ref[:, v_off + lo:v_off + hi]], axis=0)
        for g in range(GROUP):
            h = kh * GROUP + g
            q = zq_ref[:, h * HEAD_DIM:(h + 1) * HEAD_DIM]
            o = _band_attention_head(q, k, v, sink_ref[h], mask)
            mix_ref[:, h * HEAD_DIM:(h + 1) * HEAD_DIM] = o.astype(BF16)
    for h in range(MEM_HEADS):
        lo, hi = h * MEM_HEAD_DIM, (h + 1) * MEM_HEAD_DIM
        q = zq_ref[:, qm_off + lo:qm_off + hi]
        o = _cross_attention_head(q, mk_ref[:, lo:hi].astype(BF16), mv_ref[:, lo:hi].astype(BF16))
        mix_ref[:, TOKEN_WIDTH + lo:TOKEN_WIDTH + hi] = o.astype(BF16)


def _swa_prompt(z, mk, mv, sinks, *, layer, batch, seq):
    rows, zc = z.shape
    tq = WINDOW
    blocks_per_seq = seq // tq
    kv_col_block = TOKEN_WIDTH // (2 * KV_WIDTH)
    blocks = (_nbytes((tq, zc), BF16) + _nbytes((tq, 2 * KV_WIDTH), BF16)
              + 2 * _nbytes((N_MEM, MEM_WIDTH), F32) + _nbytes((tq, D_MODEL), BF16))
    return pl.pallas_call(
        functools.partial(_swa_prompt_kernel, blocks_per_seq=blocks_per_seq),
        grid=(batch * blocks_per_seq,),
        in_specs=[
            pl.BlockSpec(memory_space=pltpu.SMEM),
            pl.BlockSpec((tq, zc), lambda s: (s, 0)),
            pl.BlockSpec((tq, 2 * KV_WIDTH), lambda s: (jnp.maximum(s - 1, 0), kv_col_block)),
            pl.BlockSpec((None, N_MEM, MEM_WIDTH), lambda s: (layer, s // blocks_per_seq, 0)),
            pl.BlockSpec((None, N_MEM, MEM_WIDTH), lambda s: (layer, s // blocks_per_seq, 0)),
        ],
        out_specs=pl.BlockSpec((tq, D_MODEL), lambda s: (s, 0)),
        out_shape=jax.ShapeDtypeStruct((rows, D_MODEL), BF16),
        compiler_params=pltpu.CompilerParams(
            dimension_semantics=("arbitrary",),
            vmem_limit_bytes=_vmem_limit(blocks, 0)),
        name="swa_prompt",
    )(sinks, z, z, mk, mv)


def _swa_sample_kernel(sink_ref, z_ref, ck_ref, cv_ref, mk_ref, mv_ref,
                       mix_ref, nk_ref, nv_ref, knew_ref, vnew_ref, mixf_ref, *, dec_seq):
    t = dec_seq
    k_off = TOKEN_WIDTH
    v_off = TOKEN_WIDTH + KV_WIDTH
    qm_off = TOKEN_WIDTH + 2 * KV_WIDTH
    rows = GROUP * t
    qi = lax.broadcasted_iota(jnp.int32, (rows, 2 * WINDOW), 0) % t
    col = lax.broadcasted_iota(jnp.int32, (rows, 2 * WINDOW), 1)
    mask = jnp.logical_or(jnp.logical_and(col < WINDOW, col > qi),
                          jnp.logical_and(col >= WINDOW, col - WINDOW <= qi))
    knew_ref[...] = jnp.zeros((WINDOW, KV_WIDTH), F32)
    vnew_ref[...] = jnp.zeros((WINDOW, KV_WIDTH), F32)
    for n in range(SAMPLE_GROUP):
        r0, r1 = n * t, (n + 1) * t
        k_new = z_ref[r0:r1, k_off:k_off + KV_WIDTH].astype(F32)
        v_new = z_ref[r0:r1, v_off:v_off + KV_WIDTH].astype(F32)
        knew_ref[0:t, :] = k_new
        vnew_ref[0:t, :] = v_new
        nk_ref[n, 0:WINDOW - t, :] = ck_ref[n, t:WINDOW, :]
        nv_ref[n, 0:WINDOW - t, :] = cv_ref[n, t:WINDOW, :]
        nk_ref[n, WINDOW - t:WINDOW, :] = k_new
        nv_ref[n, WINDOW - t:WINDOW, :] = v_new
        for kh in range(N_KV_HEADS):
            lo, hi = kh * HEAD_DIM, (kh + 1) * HEAD_DIM
            k = jnp.concatenate([ck_ref[n, :, lo:hi], knew_ref[:, lo:hi]], axis=0).astype(BF16)
            v = jnp.concatenate([cv_ref[n, :, lo:hi], vnew_ref[:, lo:hi]], axis=0).astype(BF16)
            q = jnp.concatenate(
                [z_ref[r0:r1, (kh * GROUP + g) * HEAD_DIM:(kh * GROUP + g + 1) * HEAD_DIM].astype(F32)
                 for g in range(GROUP)], axis=0).astype(BF16)
            sink = jnp.concatenate(
                [jnp.full((t, 1), sink_ref[kh * GROUP + g], F32) for g in range(GROUP)], axis=0)
            o = _band_attention_head(q, k, v, sink, mask)
            for g in range(GROUP):
                h = kh * GROUP + g
                mixf_ref[r0:r1, h * HEAD_DIM:(h + 1) * HEAD_DIM] = o[g * t:(g + 1) * t, :]
        for h in range(MEM_HEADS):
            lo, hi = h * MEM_HEAD_DIM, (h + 1) * MEM_HEAD_DIM
            q = z_ref[r0:r1, qm_off + lo:qm_off + hi]
            o = _cross_attention_head(q, mk_ref[n, :, lo:hi].astype(BF16), mv_ref[n, :, lo:hi].astype(BF16))
            mixf_ref[r0:r1, TOKEN_WIDTH + lo:TOKEN_WIDTH + hi] = o
    mix_ref[...] = mixf_ref[...].astype(BF16)


def _swa_sample(z, cache_k, cache_v, mem_k, mem_v, sinks, *, layer, dec_seq):
    rows, zc = z.shape
    dec_batch = cache_k.shape[0]
    g = SAMPLE_GROUP
    gr = g * dec_seq
    win = jax.ShapeDtypeStruct((dec_batch, WINDOW, KV_WIDTH), F32)
    blocks = (_nbytes((gr, zc), BF16) + 4 * _nbytes((g, WINDOW, KV_WIDTH), F32)
              + 2 * _nbytes((g, N_MEM, MEM_WIDTH), F32) + _nbytes((gr, D_MODEL), BF16))
    scratch = 2 * _nbytes((WINDOW, KV_WIDTH), F32) + _nbytes((gr, D_MODEL), F32)
    return pl.pallas_call(
        functools.partial(_swa_sample_kernel, dec_seq=dec_seq),
        grid=(dec_batch // g,),
        in_specs=[
            pl.BlockSpec(memory_space=pltpu.SMEM),
            pl.BlockSpec((gr, zc), lambda i: (i, 0)),
            pl.BlockSpec((g, WINDOW, KV_WIDTH), lambda i: (i, 0, 0)),
            pl.BlockSpec((g, WINDOW, KV_WIDTH), lambda i: (i, 0, 0)),
            pl.BlockSpec((None, g, N_MEM, MEM_WIDTH), lambda i: (layer, i, 0, 0)),
            pl.BlockSpec((None, g, N_MEM, MEM_WIDTH), lambda i: (layer, i, 0, 0)),
        ],
        out_specs=[
            pl.BlockSpec((gr, D_MODEL), lambda i: (i, 0)),
            pl.BlockSpec((g, WINDOW, KV_WIDTH), lambda i: (i, 0, 0)),
            pl.BlockSpec((g, WINDOW, KV_WIDTH), lambda i: (i, 0, 0)),
        ],
        out_shape=[jax.ShapeDtypeStruct((rows, D_MODEL), BF16), win, win],
        scratch_shapes=[pltpu.VMEM((WINDOW, KV_WIDTH), F32), pltpu.VMEM((WINDOW, KV_WIDTH), F32),
                        pltpu.VMEM((gr, D_MODEL), F32)],
        compiler_params=pltpu.CompilerParams(
            dimension_semantics=("arbitrary",),
            vmem_limit_bytes=_vmem_limit(blocks, scratch)),
        name="swa_sample",
    )(sinks, z, cache_k, cache_v, mem_k, mem_v)


def _final_norm_kernel(xp_ref, xs_ref, g_ref, yp_ref, ys_ref, *, n_pt):
    s = pl.program_id(0)

    @pl.when(s < n_pt)
    def _():
        yp_ref[...] = _rmsnorm(xp_ref[...], g_ref[...])

    @pl.when(s == n_pt)
    def _():
        ys_ref[...] = _rmsnorm(xs_ref[...], g_ref[...])


def _final_norm(xp, xs, g):
    rp, d = xp.shape
    rs = xs.shape[0]
    tm = NORM_ROWS
    n_pt = rp // tm
    return pl.pallas_call(
        functools.partial(_final_norm_kernel, n_pt=n_pt),
        grid=(n_pt + 1,),
        in_specs=[
            pl.BlockSpec((tm, d), lambda s: (jnp.minimum(s, n_pt - 1), 0)),
            pl.BlockSpec((rs, d), lambda s: (0, 0)),
            pl.BlockSpec((1, d), lambda s: (0, 0)),
        ],
        out_specs=[
            pl.BlockSpec((tm, d), lambda s: (jnp.minimum(s, n_pt - 1), 0)),
            pl.BlockSpec((rs, d), lambda s: (0, 0)),
        ],
        out_shape=[jax.ShapeDtypeStruct((rp, d), F32), jax.ShapeDtypeStruct((rs, d), F32)],
        compiler_params=pltpu.CompilerParams(
            dimension_semantics=("arbitrary",),
            vmem_limit_bytes=_vmem_limit(2 * _nbytes((tm, d), F32) + 2 * _nbytes((rs, d), F32), 0)),
        name="final_norm",
    )(xp, xs, g.reshape(1, d))


def kernel(x_prompt, x_sample, mem_prompt, state_conv, cache_win_k, cache_win_v, cache_mem_k, cache_mem_v,
           norm_mix, norm_mem, w_mem_kv, norm_ffn, w_gate, w_up, w_down,
           conv_w_in, conv_w, conv_w_out, attn_w_in, attn_sinks, attn_w_out, norm_final):
    batch, seq, d = x_prompt.shape
    dec_batch, dec_seq, _ = x_sample.shape
    depth = norm_mix.shape[0]
    d_ff = w_gate.shape[2]
    prompt_rows = batch * seq
    sample_rows = dec_batch * dec_seq
    assert d == D_MODEL and depth == 2 and seq % CONV_ROWS == 0 and seq % WINDOW == 0
    assert prompt_rows % PROMPT_TILE == 0 and prompt_rows % NORM_ROWS == 0 and dec_batch % SAMPLE_GROUP == 0
    assert dec_seq == V7X_SUBLANES and d_ff % FFN_COL_TILE == 0 and d % COL_TILE == 0 and d % DOWN_COL_TILE == 0
    assert prompt_rows % FFN_ROW_TILE == 0 and prompt_rows % sample_rows == 0

    xp = x_prompt.reshape(prompt_rows, d)
    xs = x_sample.reshape(sample_rows, d)
    mem = mem_prompt.reshape(batch * N_MEM, d)
    mem_k_s = cache_mem_k.reshape(depth, dec_batch, N_MEM, MEM_WIDTH)
    mem_v_s = cache_mem_v.reshape(depth, dec_batch, N_MEM, MEM_WIDTH)
    g_mix = norm_mix.reshape(depth, 1, d)
    g_ffn = norm_ffn.reshape(depth, 1, d)

    mk, mv = _mem_kv(mem, norm_mem.reshape(depth, 1, d), w_mem_kv)

    zp, zs = _norm_matmul(xp, xs, g_mix, conv_w_in, layer=0, w_layer=0)
    mix_s, conv_s = _conv_sample(zs, state_conv, mem_k_s, mem_v_s, conv_w, layer=0, dec_seq=dec_seq)
    mix_p, conv_p = _conv_prompt(zp, mk, mv, conv_w, layer=0, batch=batch, seq=seq)
    xp, xs, hp, hs = _out_proj(mix_p, mix_s, conv_w_out, xp, xs, g_ffn, layer=0)
    ap, as_ = _ffn_up(hp, hs, w_gate, w_up, layer=0)
    xp, xs = _matmul_residual(ap, as_, w_down, xp, xs, w_layer=0, tn=DOWN_COL_TILE)

    zp, zs = _norm_matmul(xp, xs, g_mix, attn_w_in, layer=1, w_layer=0)
    sinks = attn_sinks[0]
    mix_s, win_k_s, win_v_s = _swa_sample(
        zs, cache_win_k[0].reshape(dec_batch, WINDOW, KV_WIDTH), cache_win_v[0].reshape(dec_batch, WINDOW, KV_WIDTH),
        mem_k_s, mem_v_s, sinks, layer=1, dec_seq=dec_seq)
    mix_p = _swa_prompt(zp, mk, mv, sinks, layer=1, batch=batch, seq=seq)
    xp, xs, hp, hs = _out_proj(mix_p, mix_s, attn_w_out, xp, xs, g_ffn, layer=1)
    ap, as_ = _ffn_up(hp, hs, w_gate, w_up, layer=1)
    xp, xs = _matmul_residual(ap, as_, w_down, xp, xs, w_layer=1, tn=DOWN_COL_TILE)

    y_prompt, y_sample = _final_norm(xp, xs, norm_final)

    win_p = zp.reshape(batch, seq, -1)[:, seq - WINDOW:, TOKEN_WIDTH:TOKEN_WIDTH + 2 * KV_WIDTH].astype(F32)
    kv_shape = (1, -1, WINDOW, N_KV_HEADS, HEAD_DIM)
    mem_shape = (depth, batch, N_MEM, MEM_HEADS, MEM_HEAD_DIM)
    return (y_prompt.reshape(batch, seq, d),
            y_sample.reshape(dec_batch, dec_seq, d),
            conv_p[None],
            conv_s[None],
            win_p[..., :KV_WIDTH].reshape(kv_shape),
            win_p[..., KV_WIDTH:].reshape(kv_shape),
            win_k_s.reshape(kv_shape),
            win_v_s.reshape(kv_shape),
            mk.reshape(mem_shape),
            mv.reshape(mem_shape))
```

```python
import functools

import jax
import jax.numpy as jnp
from jax import lax
from jax.experimental import pallas as pl
from jax.experimental.pallas import tpu as pltpu

F32 = jnp.float32
BF16 = jnp.bfloat16

D_MODEL = 2048
N_MEM = 256
MEM_HEADS = 4
MEM_WIDTH = D_MODEL // 4
MEM_HEAD_DIM = MEM_WIDTH // MEM_HEADS
TOKEN_WIDTH = D_MODEL - MEM_WIDTH
CONV_WIDTH = 3
WINDOW = 128
HEAD_DIM = 64
N_HEADS = TOKEN_WIDTH // HEAD_DIM
N_KV_HEADS = 4
GROUP = N_HEADS // N_KV_HEADS
KV_WIDTH = N_KV_HEADS * HEAD_DIM
EPS = 1e-6

V7X_VMEM_BYTES = 64 * 1024 * 1024
V7X_SUBLANES = 8

PROMPT_TILE = 1024
COL_TILE = 512
DOWN_COL_TILE = 256
FFN_ROW_TILE = 2048
FFN_COL_TILE = 512
CONV_ROWS = 512
NORM_ROWS = 512
SAMPLE_GROUP = 8
CARRY_ROWS = V7X_SUBLANES


def _nbytes(shape, dtype):
    n = 1
    for s in shape:
        n *= s
    return n * jnp.dtype(dtype).itemsize


def _vmem_limit(block_bytes, scratch_bytes):
    need = 2 * block_bytes + scratch_bytes
    return int(min(need + max(need // 4, 8 << 20), V7X_VMEM_BYTES - (6 << 20)))


def _rmsnorm(x, g):
    r = lax.rsqrt(jnp.mean(x * x, axis=-1, keepdims=True) + EPS)
    return (x * r) * g


def _softmax_rows(s):
    m = jnp.max(s, axis=-1, keepdims=True)
    e = jnp.exp(s - m)
    return e / jnp.sum(e, axis=-1, keepdims=True)


def _dot(a, b):
    return jnp.dot(a, b, preferred_element_type=F32)


def _dot_nt(a, b):
    return lax.dot_general(a, b, (((1,), (1,)), ((), ())), preferred_element_type=F32)


def _prompt_rows_map(n_pt):
    return lambda i, j: (jnp.minimum(i, n_pt - 1), 0)


def _prompt_tile_map(n_pt, n_j):
    return lambda i, j: (jnp.minimum(i, n_pt - 1), jnp.where(i < n_pt, j, n_j - 1))


def _sample_tile_map(n_pt):
    return lambda i, j: (0, jnp.where(i < n_pt, 0, j))


def _const_map(i, j):
    return (0, 0)


def _norm_matmul_kernel(xp_ref, xs_ref, g_ref, w_ref, zp_ref, zs_ref, hp_ref, hs_ref, *, n_pt):
    i, j = pl.program_id(0), pl.program_id(1)

    @pl.when(i < n_pt)
    def _():
        @pl.when(j == 0)
        def _():
            hp_ref[...] = _rmsnorm(xp_ref[...], g_ref[...]).astype(BF16)

        zp_ref[...] = _dot(hp_ref[...], w_ref[...].astype(BF16)).astype(zp_ref.dtype)

    @pl.when(i == n_pt)
    def _():
        @pl.when(j == 0)
        def _():
            hs_ref[...] = _rmsnorm(xs_ref[...], g_ref[...]).astype(BF16)

        zs_ref[...] = _dot(hs_ref[...], w_ref[...].astype(BF16)).astype(zs_ref.dtype)


def _norm_matmul(xp, xs, g, w, *, layer, w_layer):
    rp, d = xp.shape
    rs = xs.shape[0]
    n = w.shape[2]
    tm, tn = PROMPT_TILE, COL_TILE
    n_pt, n_j = rp // tm, n // tn
    blocks = (_nbytes((tm, d), F32) + _nbytes((rs, d), F32) + _nbytes((d, tn), F32)
              + _nbytes((tm, tn), BF16) + _nbytes((rs, tn), BF16))
    scratch = _nbytes((tm, d), BF16) + _nbytes((rs, d), BF16) + _nbytes((d, tn), BF16)
    return pl.pallas_call(
        functools.partial(_norm_matmul_kernel, n_pt=n_pt),
        grid=(n_pt + 1, n_j),
        in_specs=[
            pl.BlockSpec((tm, d), _prompt_rows_map(n_pt)),
            pl.BlockSpec((rs, d), _const_map),
            pl.BlockSpec((None, 1, d), lambda i, j: (layer, 0, 0)),
            pl.BlockSpec((None, d, tn), lambda i, j: (w_layer, 0, j)),
        ],
        out_specs=[
            pl.BlockSpec((tm, tn), _prompt_tile_map(n_pt, n_j)),
            pl.BlockSpec((rs, tn), _sample_tile_map(n_pt)),
        ],
        out_shape=[jax.ShapeDtypeStruct((rp, n), BF16), jax.ShapeDtypeStruct((rs, n), BF16)],
        scratch_shapes=[pltpu.VMEM((tm, d), BF16), pltpu.VMEM((rs, d), BF16)],
        compiler_params=pltpu.CompilerParams(
            dimension_semantics=("arbitrary", "arbitrary"),
            vmem_limit_bytes=_vmem_limit(blocks, scratch)),
        name="norm_matmul",
    )(xp, xs, g, w)


def _swiglu(h, wg, wu):
    gate = _dot(h, wg)
    up = _dot(h, wu)
    return (gate * jax.nn.sigmoid(gate) * up).astype(BF16)


def _ffn_up_kernel(hp_ref, hs_ref, wg_ref, wu_ref, ap_ref, as_ref, *, n_pt):
    i = pl.program_id(0)

    @pl.when(i < n_pt)
    def _():
        ap_ref[...] = _swiglu(hp_ref[...], wg_ref[...].astype(BF16), wu_ref[...].astype(BF16))

    @pl.when(i == n_pt)
    def _():
        as_ref[...] = _swiglu(hs_ref[...], wg_ref[...].astype(BF16), wu_ref[...].astype(BF16))


def _ffn_up(hp, hs, wg, wu, *, layer):
    rp, d = hp.shape
    rs = hs.shape[0]
    n = wg.shape[2]
    tm, tn = FFN_ROW_TILE, FFN_COL_TILE
    n_pt, n_j = rp // tm, n // tn
    blocks = (_nbytes((tm, d), BF16) + _nbytes((rs, d), BF16) + 2 * _nbytes((d, tn), F32)
              + _nbytes((tm, tn), BF16) + _nbytes((rs, tn), BF16))
    temps = 2 * _nbytes((d, tn), BF16) + 3 * _nbytes((tm, tn), F32)
    w_spec = pl.BlockSpec((None, d, tn), lambda i, j: (layer, 0, j))
    return pl.pallas_call(
        functools.partial(_ffn_up_kernel, n_pt=n_pt),
        grid=(n_pt + 1, n_j),
        in_specs=[
            pl.BlockSpec((tm, d), _prompt_rows_map(n_pt)),
            pl.BlockSpec((rs, d), _const_map),
            w_spec,
            w_spec,
        ],
        out_specs=[
            pl.BlockSpec((tm, tn), _prompt_tile_map(n_pt, n_j)),
            pl.BlockSpec((rs, tn), _sample_tile_map(n_pt)),
        ],
        out_shape=[jax.ShapeDtypeStruct((rp, n), BF16), jax.ShapeDtypeStruct((rs, n), BF16)],
        compiler_params=pltpu.CompilerParams(
            dimension_semantics=("arbitrary", "arbitrary"),
            vmem_limit_bytes=_vmem_limit(blocks, temps)),
        name="ffn_up",
    )(hp, hs, wg, wu)


def _out_proj_kernel(ap_ref, as_ref, w_ref, xp_ref, xs_ref, g_ref, op_ref, os_ref, hp_ref, hs_ref, wb_ref, *, n_pt):
    s = pl.program_id(0)

    @pl.when(s == 0)
    def _():
        wb_ref[...] = w_ref[...].astype(BF16)

    @pl.when(s < n_pt)
    def _():
        x = xp_ref[...] + _dot(ap_ref[...], wb_ref[...])
        op_ref[...] = x
        hp_ref[...] = _rmsnorm(x, g_ref[...]).astype(BF16)

    @pl.when(s == n_pt)
    def _():
        x = xs_ref[...] + _dot(as_ref[...], wb_ref[...])
        os_ref[...] = x
        hs_ref[...] = _rmsnorm(x, g_ref[...]).astype(BF16)


def _out_proj(ap, as_, w, xp, xs, g, *, layer):
    rp, k = ap.shape
    rs = as_.shape[0]
    n = w.shape[2]
    tm = rs
    n_pt = rp // tm
    rows_map = lambda s: (jnp.minimum(s, n_pt - 1), 0)
    once = dict(pipeline_mode=pl.Buffered(1))
    blocks = _nbytes((tm, k), BF16) + 2 * _nbytes((tm, n), F32) + _nbytes((tm, n), BF16)
    resident = _nbytes((k, n), F32) + _nbytes((k, n), BF16) + blocks
    return pl.pallas_call(
        functools.partial(_out_proj_kernel, n_pt=n_pt),
        grid=(n_pt + 1,),
        in_specs=[
            pl.BlockSpec((tm, k), rows_map),
            pl.BlockSpec((rs, k), lambda s: (0, 0), **once),
            pl.BlockSpec((None, k, n), lambda s: (0, 0, 0), **once),
            pl.BlockSpec((tm, n), rows_map),
            pl.BlockSpec((rs, n), lambda s: (0, 0), **once),
            pl.BlockSpec((None, 1, n), lambda s: (layer, 0, 0)),
        ],
        out_specs=[
            pl.BlockSpec((tm, n), rows_map),
            pl.BlockSpec((rs, n), lambda s: (0, 0)),
            pl.BlockSpec((tm, n), rows_map),
            pl.BlockSpec((rs, n), lambda s: (0, 0)),
        ],
        out_shape=[jax.ShapeDtypeStruct((rp, n), F32), jax.ShapeDtypeStruct((rs, n), F32),
                   jax.ShapeDtypeStruct((rp, n), BF16), jax.ShapeDtypeStruct((rs, n), BF16)],
        scratch_shapes=[pltpu.VMEM((k, n), BF16)],
        compiler_params=pltpu.CompilerParams(
            dimension_semantics=("arbitrary",),
            vmem_limit_bytes=_vmem_limit(blocks, resident + 2 * _nbytes((tm, n), F32))),
        name="out_proj",
    )(ap, as_, w, xp, xs, g)


def _matmul_residual_kernel(ap_ref, as_ref, w_ref, xp_ref, xs_ref, op_ref, os_ref, *, n_pt):
    i = pl.program_id(0)

    @pl.when(i < n_pt)
    def _():
        op_ref[...] = xp_ref[...] + _dot(ap_ref[...], w_ref[...].astype(BF16))

    @pl.when(i == n_pt)
    def _():
        os_ref[...] = xs_ref[...] + _dot(as_ref[...], w_ref[...].astype(BF16))


def _matmul_residual(ap, as_, w, xp, xs, *, w_layer, tn):
    rp, k = ap.shape
    rs = as_.shape[0]
    n = w.shape[2]
    tm = PROMPT_TILE
    n_pt, n_j = rp // tm, n // tn
    blocks = (_nbytes((tm, k), BF16) + _nbytes((rs, k), BF16) + _nbytes((k, tn), F32)
              + 2 * _nbytes((tm, tn), F32) + 2 * _nbytes((rs, tn), F32))
    scratch = _nbytes((k, tn), BF16)
    return pl.pallas_call(
        functools.partial(_matmul_residual_kernel, n_pt=n_pt),
        grid=(n_pt + 1, n_j),
        in_specs=[
            pl.BlockSpec((tm, k), _prompt_rows_map(n_pt)),
            pl.BlockSpec((rs, k), _const_map),
            pl.BlockSpec((None, k, tn), lambda i, j: (w_layer, 0, j)),
            pl.BlockSpec((tm, tn), _prompt_tile_map(n_pt, n_j)),
            pl.BlockSpec((rs, tn), _sample_tile_map(n_pt)),
        ],
        out_specs=[
            pl.BlockSpec((tm, tn), _prompt_tile_map(n_pt, n_j)),
            pl.BlockSpec((rs, tn), _sample_tile_map(n_pt)),
        ],
        out_shape=[jax.ShapeDtypeStruct((rp, n), F32), jax.ShapeDtypeStruct((rs, n), F32)],
        compiler_params=pltpu.CompilerParams(
            dimension_semantics=("arbitrary", "arbitrary"),
            vmem_limit_bytes=_vmem_limit(blocks, scratch)),
        name="matmul_residual",
    )(ap, as_, w, xp, xs)


def _mem_kv_kernel(x_ref, g_ref, w_ref, k_ref, v_ref):
    h = _rmsnorm(x_ref[...], g_ref[...]).astype(BF16)
    kv = _dot(h, w_ref[...].astype(BF16))
    k_ref[...] = kv[:, :MEM_WIDTH]
    v_ref[...] = kv[:, MEM_WIDTH:]


def _mem_kv(mem, g, w):
    rows, d = mem.shape
    depth = w.shape[0]
    tm = 512
    out = jax.ShapeDtypeStruct((depth, rows, MEM_WIDTH), F32)
    blocks = (_nbytes((tm, d), F32) + _nbytes((d, 2 * MEM_WIDTH), F32) + 2 * _nbytes((tm, MEM_WIDTH), F32))
    return pl.pallas_call(
        _mem_kv_kernel,
        grid=(depth, rows // tm),
        in_specs=[
            pl.BlockSpec((tm, d), lambda l, i: (i, 0)),
            pl.BlockSpec((None, 1, d), lambda l, i: (l, 0, 0)),
            pl.BlockSpec((None, d, 2 * MEM_WIDTH), lambda l, i: (l, 0, 0)),
        ],
        out_specs=[
            pl.BlockSpec((None, tm, MEM_WIDTH), lambda l, i: (l, i, 0)),
            pl.BlockSpec((None, tm, MEM_WIDTH), lambda l, i: (l, i, 0)),
        ],
        out_shape=[out, out],
        compiler_params=pltpu.CompilerParams(
            dimension_semantics=("arbitrary", "arbitrary"),
            vmem_limit_bytes=_vmem_limit(blocks, _nbytes((d, 2 * MEM_WIDTH), BF16))),
        name="mem_kv",
    )(mem, g, w)


def _cross_attention_head(q, k, v):
    s = _dot_nt(q, k) * (MEM_HEAD_DIM ** -0.5)
    p = _softmax_rows(s).astype(BF16)
    return _dot(p, v)


def _conv_prompt_kernel(z_ref, mk_ref, mv_ref, cw_ref, mix_ref, st_ref, ext_ref, *, tiles_per_seq):
    s = pl.program_id(0)
    tq = z_ref.shape[0]

    @pl.when(s % tiles_per_seq == 0)
    def _():
        ext_ref[0:CARRY_ROWS, :] = jnp.zeros((CARRY_ROWS, TOKEN_WIDTH), F32)

    c = z_ref[:, TOKEN_WIDTH:2 * TOKEN_WIDTH].astype(F32)
    u = z_ref[:, 2 * TOKEN_WIDTH:3 * TOKEN_WIDTH].astype(F32)
    cu = c * u
    ext_ref[CARRY_ROWS:CARRY_ROWS + tq, :] = cu
    conv = (cw_ref[0:1, :] * ext_ref[CARRY_ROWS - 2:CARRY_ROWS - 2 + tq, :]
            + cw_ref[1:2, :] * ext_ref[CARRY_ROWS - 1:CARRY_ROWS - 1 + tq, :]
            + cw_ref[2:3, :] * cu)
    b = z_ref[:, 0:TOKEN_WIDTH].astype(F32)
    mix_ref[:, 0:TOKEN_WIDTH] = (b * conv).astype(BF16)
    st_ref[...] = ext_ref[CARRY_ROWS + tq - 2:CARRY_ROWS + tq, :]
    ext_ref[0:CARRY_ROWS, :] = ext_ref[tq:tq + CARRY_ROWS, :]

    for h in range(MEM_HEADS):
        lo, hi = h * MEM_HEAD_DIM, (h + 1) * MEM_HEAD_DIM
        q = z_ref[:, 3 * TOKEN_WIDTH + lo:3 * TOKEN_WIDTH + hi]
        o = _cross_attention_head(q, mk_ref[:, lo:hi].astype(BF16), mv_ref[:, lo:hi].astype(BF16))
        mix_ref[:, TOKEN_WIDTH + lo:TOKEN_WIDTH + hi] = o.astype(BF16)


def _conv_prompt(z, mk, mv, conv_w, *, layer, batch, seq):
    rows, zc = z.shape
    tq = CONV_ROWS
    tiles_per_seq = seq // tq
    blocks = (_nbytes((tq, zc), BF16) + 2 * _nbytes((N_MEM, MEM_WIDTH), F32) + _nbytes((tq, D_MODEL), BF16))
    scratch = _nbytes((tq + CARRY_ROWS, TOKEN_WIDTH), F32)
    return pl.pallas_call(
        functools.partial(_conv_prompt_kernel, tiles_per_seq=tiles_per_seq),
        grid=(batch * tiles_per_seq,),
        in_specs=[
            pl.BlockSpec((tq, zc), lambda s: (s, 0)),
            pl.BlockSpec((None, N_MEM, MEM_WIDTH), lambda s: (layer, s // tiles_per_seq, 0)),
            pl.BlockSpec((None, N_MEM, MEM_WIDTH), lambda s: (layer, s // tiles_per_seq, 0)),
            pl.BlockSpec((None, CONV_WIDTH, TOKEN_WIDTH), lambda s: (0, 0, 0)),
        ],
        out_specs=[
            pl.BlockSpec((tq, D_MODEL), lambda s: (s, 0)),
            pl.BlockSpec((None, CONV_WIDTH - 1, TOKEN_WIDTH), lambda s: (s // tiles_per_seq, 0, 0)),
        ],
        out_shape=[
            jax.ShapeDtypeStruct((rows, D_MODEL), BF16),
            jax.ShapeDtypeStruct((batch, CONV_WIDTH - 1, TOKEN_WIDTH), F32),
        ],
        scratch_shapes=[pltpu.VMEM((tq + CARRY_ROWS, TOKEN_WIDTH), F32)],
        compiler_params=pltpu.CompilerParams(
            dimension_semantics=("arbitrary",),
            vmem_limit_bytes=_vmem_limit(blocks, scratch + 6 * _nbytes((tq, TOKEN_WIDTH), F32))),
        name="conv_prompt",
    )(z, mk, mv, conv_w)


def _conv_sample_kernel(z_ref, st_ref, mk_ref, mv_ref, cw_ref, mix_ref, nst_ref, ext_ref, mixf_ref, *, dec_seq):
    t = dec_seq
    for n in range(SAMPLE_GROUP):
        r0, r1 = n * t, (n + 1) * t
        c = z_ref[r0:r1, TOKEN_WIDTH:2 * TOKEN_WIDTH].astype(F32)
        u = z_ref[r0:r1, 2 * TOKEN_WIDTH:3 * TOKEN_WIDTH].astype(F32)
        cu = c * u
        ext_ref[CARRY_ROWS - 2:CARRY_ROWS, :] = st_ref[n]
        ext_ref[CARRY_ROWS:CARRY_ROWS + t, :] = cu
        conv = (cw_ref[0:1, :] * ext_ref[CARRY_ROWS - 2:CARRY_ROWS - 2 + t, :]
                + cw_ref[1:2, :] * ext_ref[CARRY_ROWS - 1:CARRY_ROWS - 1 + t, :]
                + cw_ref[2:3, :] * cu)
        b = z_ref[r0:r1, 0:TOKEN_WIDTH].astype(F32)
        mixf_ref[r0:r1, 0:TOKEN_WIDTH] = b * conv
        nst_ref[n] = ext_ref[CARRY_ROWS + t - 2:CARRY_ROWS + t, :]
        for h in range(MEM_HEADS):
            lo, hi = h * MEM_HEAD_DIM, (h + 1) * MEM_HEAD_DIM
            q = z_ref[r0:r1, 3 * TOKEN_WIDTH + lo:3 * TOKEN_WIDTH + hi]
            o = _cross_attention_head(q, mk_ref[n, :, h, :].astype(BF16), mv_ref[n, :, h, :].astype(BF16))
            mixf_ref[r0:r1, TOKEN_WIDTH + lo:TOKEN_WIDTH + hi] = o
    mix_ref[...] = mixf_ref[...].astype(BF16)


def _conv_sample(z, state, mem_k, mem_v, conv_w, *, layer, dec_seq):
    rows, zc = z.shape
    dec_batch = state.shape[1]
    g = SAMPLE_GROUP
    gr = g * dec_seq
    blocks = (_nbytes((gr, zc), BF16) + 2 * _nbytes((g, CONV_WIDTH - 1, TOKEN_WIDTH), F32)
              + 2 * _nbytes((g, N_MEM, V7X_SUBLANES, MEM_HEAD_DIM), F32) + _nbytes((gr, D_MODEL), BF16))
    scratch = _nbytes((2 * CARRY_ROWS, TOKEN_WIDTH), F32) + _nbytes((gr, D_MODEL), F32)
    return pl.pallas_call(
        functools.partial(_conv_sample_kernel, dec_seq=dec_seq),
        grid=(dec_batch // g,),
        in_specs=[
            pl.BlockSpec((gr, zc), lambda i: (i, 0)),
            pl.BlockSpec((None, g, CONV_WIDTH - 1, TOKEN_WIDTH), lambda i: (0, i, 0, 0)),
            pl.BlockSpec((None, g, N_MEM, MEM_HEADS, MEM_HEAD_DIM), lambda i: (layer, i, 0, 0, 0)),
            pl.BlockSpec((None, g, N_MEM, MEM_HEADS, MEM_HEAD_DIM), lambda i: (layer, i, 0, 0, 0)),
            pl.BlockSpec((None, CONV_WIDTH, TOKEN_WIDTH), lambda i: (0, 0, 0)),
        ],
        out_specs=[
            pl.BlockSpec((gr, D_MODEL), lambda i: (i, 0)),
            pl.BlockSpec((g, CONV_WIDTH - 1, TOKEN_WIDTH), lambda i: (i, 0, 0)),
        ],
        out_shape=[
            jax.ShapeDtypeStruct((rows, D_MODEL), BF16),
            jax.ShapeDtypeStruct((dec_batch, CONV_WIDTH - 1, TOKEN_WIDTH), F32),
        ],
        scratch_shapes=[pltpu.VMEM((2 * CARRY_ROWS, TOKEN_WIDTH), F32), pltpu.VMEM((gr, D_MODEL), F32)],
        compiler_params=pltpu.CompilerParams(
            dimension_semantics=("arbitrary",),
            vmem_limit_bytes=_vmem_limit(blocks, scratch)),
        name="conv_sample",
    )(z, state, mem_k, mem_v, conv_w)


def _band_attention_head(q, k, v, sink, mask):
    s = _dot_nt(q, k) * (HEAD_DIM ** -0.5)
    s = jnp.where(mask, s, -jnp.inf)
    m = jnp.maximum(jnp.max(s, axis=-1, keepdims=True), sink)
    e = jnp.exp(s - m)
    p = e / (jnp.sum(e, axis=-1, keepdims=True) + jnp.exp(sink - m))
    return _dot(p.astype(BF16), v)


def _swa_prompt_kernel(sink_ref, zq_ref, zp_ref, mk_ref, mv_ref, mix_ref, *, blocks_per_seq):
    s = pl.program_id(0)
    k_off = TOKEN_WIDTH
    v_off = TOKEN_WIDTH + KV_WIDTH
    qm_off = TOKEN_WIDTH + 2 * KV_WIDTH
    has_prev = (s % blocks_per_seq) > 0
    row = lax.broadcasted_iota(jnp.int32, (WINDOW, 2 * WINDOW), 0)
    col = lax.broadcasted_iota(jnp.int32, (WINDOW, 2 * WINDOW), 1)
    mask = jnp.logical_or(jnp.logical_and(jnp.logical_and(col < WINDOW, col > row), has_prev),
                          jnp.logical_and(col >= WINDOW, col - WINDOW <= row))
    for kh in range(N_KV_HEADS):
        lo, hi = kh * HEAD_DIM, (kh + 1) * HEAD_DIM
        k = jnp.concatenate([zp_ref[:, lo:hi], zq_ref[:, k_off + lo:k_off + hi]], axis=0)
        v = jnp.concatenate([zp_ref[:, KV_WIDTH + lo:KV_WIDTH + hi], zq_ref[:, v_off + lo:v_off + hi]], axis=0)
        for g in range(GROUP):
            h = kh * GROUP + g
            q = zq_ref[:, h * HEAD_DIM:(h + 1) * HEAD_DIM]
            o = _band_attention_head(q, k, v, sink_ref[h], mask)
            mix_ref[:, h * HEAD_DIM:(h + 1) * HEAD_DIM] = o.astype(BF16)
    for h in range(MEM_HEADS):
        lo, hi = h * MEM_HEAD_DIM, (h + 1) * MEM_HEAD_DIM
        q = zq_ref[:, qm_off + lo:qm_off + hi]
        o = _cross_attention_head(q, mk_ref[:, lo:hi].astype(BF16), mv_ref[:, lo:hi].astype(BF16))
        mix_ref[:, TOKEN_WIDTH + lo:TOKEN_WIDTH + hi] = o.astype(BF16)


def _swa_prompt(z, mk, mv, sinks, *, layer, batch, seq):
    rows, zc = z.shape
    tq = WINDOW
    blocks_per_seq = seq // tq
    kv_col_block = TOKEN_WIDTH // (2 * KV_WIDTH)
    blocks = (_nbytes((tq, zc), BF16) + _nbytes((tq, 2 * KV_WIDTH), BF16)
              + 2 * _nbytes((N_MEM, MEM_WIDTH), F32) + _nbytes((tq, D_MODEL), BF16))
    return pl.pallas_call(
        functools.partial(_swa_prompt_kernel, blocks_per_seq=blocks_per_seq),
        grid=(batch * blocks_per_seq,),
        in_specs=[
            pl.BlockSpec(memory_space=pltpu.SMEM),
            pl.BlockSpec((tq, zc), lambda s: (s, 0)),
            pl.BlockSpec((tq, 2 * KV_WIDTH), lambda s: (jnp.maximum(s - 1, 0), kv_col_block)),
            pl.BlockSpec((None, N_MEM, MEM_WIDTH), lambda s: (layer, s // blocks_per_seq, 0)),
            pl.BlockSpec((None, N_MEM, MEM_WIDTH), lambda s: (layer, s // blocks_per_seq, 0)),
        ],
        out_specs=pl.BlockSpec((tq, D_MODEL), lambda s: (s, 0)),
        out_shape=jax.ShapeDtypeStruct((rows, D_MODEL), BF16),
        compiler_params=pltpu.CompilerParams(
            dimension_semantics=("arbitrary",),
            vmem_limit_bytes=_vmem_limit(blocks, 0)),
        name="swa_prompt",
    )(sinks, z, z, mk, mv)


def _swa_sample_kernel(sink_ref, z_ref, ck_ref, cv_ref, mk_ref, mv_ref,
                       mix_ref, nk_ref, nv_ref, knew_ref, vnew_ref, mixf_ref, *, dec_seq):
    t = dec_seq
    k_off = TOKEN_WIDTH
    v_off = TOKEN_WIDTH + KV_WIDTH
    qm_off = TOKEN_WIDTH + 2 * KV_WIDTH
    rows = GROUP * t
    qi = lax.broadcasted_iota(jnp.int32, (rows, 2 * WINDOW), 0) % t
    col = lax.broadcasted_iota(jnp.int32, (rows, 2 * WINDOW), 1)
    mask = jnp.logical_or(jnp.logical_and(col < WINDOW, col > qi),
                          jnp.logical_and(col >= WINDOW, col - WINDOW <= qi))
    knew_ref[...] = jnp.zeros((WINDOW, KV_WIDTH), F32)
    vnew_ref[...] = jnp.zeros((WINDOW, KV_WIDTH), F32)
    for n in range(SAMPLE_GROUP):
        r0, r1 = n * t, (n + 1) * t
        k_new = z_ref[r0:r1, k_off:k_off + KV_WIDTH].astype(F32)
        v_new = z_ref[r0:r1, v_off:v_off + KV_WIDTH].astype(F32)
        knew_ref[0:t, :] = k_new
        vnew_ref[0:t, :] = v_new
        nk_ref[n, 0:WINDOW - t, :] = ck_ref[n, t:WINDOW, :]
        nv_ref[n, 0:WINDOW - t, :] = cv_ref[n, t:WINDOW, :]
        nk_ref[n, WINDOW - t:WINDOW, :] = k_new
        nv_ref[n, WINDOW - t:WINDOW, :] = v_new
        for kh in range(N_KV_HEADS):
            lo, hi = kh * HEAD_DIM, (kh + 1) * HEAD_DIM
            k = jnp.concatenate([ck_ref[n, :, lo:hi], knew_ref[:, lo:hi]], axis=0).astype(BF16)
            v = jnp.concatenate([cv_ref[n, :, lo:hi], vnew_ref[:, lo:hi]], axis=0).astype(BF16)
            q = jnp.concatenate(
                [z_ref[r0:r1, (kh * GROUP + g) * HEAD_DIM:(kh * GROUP + g + 1) * HEAD_DIM].astype(F32)
                 for g in range(GROUP)], axis=0).astype(BF16)
            sink = jnp.concatenate(
                [jnp.full((t, 1), sink_ref[kh * GROUP + g], F32) for g in range(GROUP)], axis=0)
            o = _band_attention_head(q, k, v, sink, mask)
            for g in range(GROUP):
                h = kh * GROUP + g
                mixf_ref[r0:r1, h * HEAD_DIM:(h + 1) * HEAD_DIM] = o[g * t:(g + 1) * t, :]
        for h in range(MEM_HEADS):
            lo, hi = h * MEM_HEAD_DIM, (h + 1) * MEM_HEAD_DIM
            q = z_ref[r0:r1, qm_off + lo:qm_off + hi]
            o = _cross_attention_head(q, mk_ref[n, :, h, :].astype(BF16), mv_ref[n, :, h, :].astype(BF16))
            mixf_ref[r0:r1, TOKEN_WIDTH + lo:TOKEN_WIDTH + hi] = o
    mix_ref[...] = mixf_ref[...].astype(BF16)


def _swa_sample(z, cache_k, cache_v, mem_k, mem_v, sinks, *, layer, dec_seq):
    rows, zc = z.shape
    dec_batch = cache_k.shape[0]
    g = SAMPLE_GROUP
    gr = g * dec_seq
    win = jax.ShapeDtypeStruct((dec_batch, WINDOW, KV_WIDTH), F32)
    blocks = (_nbytes((gr, zc), BF16) + 4 * _nbytes((g, WINDOW, KV_WIDTH), F32)
              + 2 * _nbytes((g, N_MEM, V7X_SUBLANES, MEM_HEAD_DIM), F32) + _nbytes((gr, D_MODEL), BF16))
    scratch = 2 * _nbytes((WINDOW, KV_WIDTH), F32) + _nbytes((gr, D_MODEL), F32)
    return pl.pallas_call(
        functools.partial(_swa_sample_kernel, dec_seq=dec_seq),
        grid=(dec_batch // g,),
        in_specs=[
            pl.BlockSpec(memory_space=pltpu.SMEM),
            pl.BlockSpec((gr, zc), lambda i: (i, 0)),
            pl.BlockSpec((g, WINDOW, KV_WIDTH), lambda i: (i, 0, 0)),
            pl.BlockSpec((g, WINDOW, KV_WIDTH), lambda i: (i, 0, 0)),
            pl.BlockSpec((None, g, N_MEM, MEM_HEADS, MEM_HEAD_DIM), lambda i: (layer, i, 0, 0, 0)),
            pl.BlockSpec((None, g, N_MEM, MEM_HEADS, MEM_HEAD_DIM), lambda i: (layer, i, 0, 0, 0)),
        ],
        out_specs=[
            pl.BlockSpec((gr, D_MODEL), lambda i: (i, 0)),
            pl.BlockSpec((g, WINDOW, KV_WIDTH), lambda i: (i, 0, 0)),
            pl.BlockSpec((g, WINDOW, KV_WIDTH), lambda i: (i, 0, 0)),
        ],
        out_shape=[jax.ShapeDtypeStruct((rows, D_MODEL), BF16), win, win],
        scratch_shapes=[pltpu.VMEM((WINDOW, KV_WIDTH), F32), pltpu.VMEM((WINDOW, KV_WIDTH), F32),
                        pltpu.VMEM((gr, D_MODEL), F32)],
        compiler_params=pltpu.CompilerParams(
            dimension_semantics=("arbitrary",),
            vmem_limit_bytes=_vmem_limit(blocks, scratch)),
        name="swa_sample",
    )(sinks, z, cache_k, cache_v, mem_k, mem_v)


def _final_norm_kernel(xp_ref, xs_ref, g_ref, yp_ref, ys_ref, *, n_pt):
    s = pl.program_id(0)

    @pl.when(s < n_pt)
    def _():
        yp_ref[...] = _rmsnorm(xp_ref[...], g_ref[...])

    @pl.when(s == n_pt)
    def _():
        ys_ref[...] = _rmsnorm(xs_ref[...], g_ref[...])


def _final_norm(xp, xs, g):
    rp, d = xp.shape
    rs = xs.shape[0]
    tm = NORM_ROWS
    n_pt = rp // tm
    return pl.pallas_call(
        functools.partial(_final_norm_kernel, n_pt=n_pt),
        grid=(n_pt + 1,),
        in_specs=[
            pl.BlockSpec((tm, d), lambda s: (jnp.minimum(s, n_pt - 1), 0)),
            pl.BlockSpec((rs, d), lambda s: (0, 0)),
            pl.BlockSpec((1, d), lambda s: (0, 0)),
        ],
        out_specs=[
            pl.BlockSpec((tm, d), lambda s: (jnp.minimum(s, n_pt - 1), 0)),
            pl.BlockSpec((rs, d), lambda s: (0, 0)),
        ],
        out_shape=[jax.ShapeDtypeStruct((rp, d), F32), jax.ShapeDtypeStruct((rs, d), F32)],
        compiler_params=pltpu.CompilerParams(
            dimension_semantics=("arbitrary",),
            vmem_limit_bytes=_vmem_limit(2 * _nbytes((tm, d), F32) + 2 * _nbytes((rs, d), F32), 0)),
        name="final_norm",
    )(xp, xs, g.reshape(1, d))


def kernel(x_prompt, x_sample, mem_prompt, state_conv, cache_win_k, cache_win_v, cache_mem_k, cache_mem_v,
           norm_mix, norm_mem, w_mem_kv, norm_ffn, w_gate, w_up, w_down,
           conv_w_in, conv_w, conv_w_out, attn_w_in, attn_sinks, attn_w_out, norm_final):
    batch, seq, d = x_prompt.shape
    dec_batch, dec_seq, _ = x_sample.shape
    depth = norm_mix.shape[0]
    d_ff = w_gate.shape[2]
    prompt_rows = batch * seq
    sample_rows = dec_batch * dec_seq
    assert d == D_MODEL and depth == 2 and seq % CONV_ROWS == 0 and seq % WINDOW == 0
    assert prompt_rows % PROMPT_TILE == 0 and prompt_rows % NORM_ROWS == 0 and dec_batch % SAMPLE_GROUP == 0
    assert dec_seq == V7X_SUBLANES and d_ff % FFN_COL_TILE == 0 and d % COL_TILE == 0 and d % DOWN_COL_TILE == 0
    assert prompt_rows % FFN_ROW_TILE == 0 and prompt_rows % sample_rows == 0

    xp = x_prompt.reshape(prompt_rows, d)
    xs = x_sample.reshape(sample_rows, d)
    mem = mem_prompt.reshape(batch * N_MEM, d)
    mem_k_s, mem_v_s = cache_mem_k, cache_mem_v
    g_mix = norm_mix.reshape(depth, 1, d)
    g_ffn = norm_ffn.reshape(depth, 1, d)

    mk, mv = _mem_kv(mem, norm_mem.reshape(depth, 1, d), w_mem_kv)

    zp, zs = _norm_matmul(xp, xs, g_mix, conv_w_in, layer=0, w_layer=0)
    mix_s, conv_s = _conv_sample(zs, state_conv, mem_k_s, mem_v_s, conv_w, layer=0, dec_seq=dec_seq)
    mix_p, conv_p = _conv_prompt(zp, mk, mv, conv_w, layer=0, batch=batch, seq=seq)
    xp, xs, hp, hs = _out_proj(mix_p, mix_s, conv_w_out, xp, xs, g_ffn, layer=0)
    ap, as_ = _ffn_up(hp, hs, w_gate, w_up, layer=0)
    xp, xs = _matmul_residual(ap, as_, w_down, xp, xs, w_layer=0, tn=DOWN_COL_TILE)

    zp, zs = _norm_matmul(xp, xs, g_mix, attn_w_in, layer=1, w_layer=0)
    sinks = attn_sinks[0]
    mix_s, win_k_s, win_v_s = _swa_sample(
        zs, cache_win_k[0].reshape(dec_batch, WINDOW, KV_WIDTH), cache_win_v[0].reshape(dec_batch, WINDOW, KV_WIDTH),
        mem_k_s, mem_v_s, sinks, layer=1, dec_seq=dec_seq)
    mix_p = _swa_prompt(zp, mk, mv, sinks, layer=1, batch=batch, seq=seq)
    xp, xs, hp, hs = _out_proj(mix_p, mix_s, attn_w_out, xp, xs, g_ffn, layer=1)
    ap, as_ = _ffn_up(hp, hs, w_gate, w_up, layer=1)
    xp, xs = _matmul_residual(ap, as_, w_down, xp, xs, w_layer=1, tn=DOWN_COL_TILE)

    y_prompt, y_sample = _final_norm(xp, xs, norm_final)

    win_p = zp.reshape(batch, seq, -1)[:, seq - WINDOW:, TOKEN_WIDTH:TOKEN_WIDTH + 2 * KV_WIDTH].astype(F32)
    kv_shape = (1, -1, WINDOW, N_KV_HEADS, HEAD_DIM)
    mem_shape = (depth, batch, N_MEM, MEM_HEADS, MEM_HEAD_DIM)
    return (y_prompt.reshape(batch, seq, d),
            y_sample.reshape(dec_batch, dec_seq, d),
            conv_p[None],
            conv_s[None],
            win_p[..., :KV_WIDTH].reshape(kv_shape),
            win_p[..., KV_WIDTH:].reshape(kv_shape),
            win_k_s.reshape(kv_shape),
            win_v_s.reshape(kv_shape),
            mk.reshape(mem_shape),
            mv.reshape(mem_shape))
```

```python
import functools

import jax
import jax.numpy as jnp
from jax import lax
from jax.experimental import pallas as pl
from jax.experimental.pallas import tpu as pltpu

F32 = jnp.float32
BF16 = jnp.bfloat16

D_MODEL = 2048
N_MEM = 256
MEM_HEADS = 4
MEM_WIDTH = D_MODEL // 4
MEM_HEAD_DIM = MEM_WIDTH // MEM_HEADS
TOKEN_WIDTH = D_MODEL - MEM_WIDTH
CONV_WIDTH = 3
WINDOW = 128
HEAD_DIM = 64
N_HEADS = TOKEN_WIDTH // HEAD_DIM
N_KV_HEADS = 4
GROUP = N_HEADS // N_KV_HEADS
KV_WIDTH = N_KV_HEADS * HEAD_DIM
EPS = 1e-6

V7X_VMEM_BYTES = 64 * 1024 * 1024
V7X_SUBLANES = 8

PROMPT_TILE = 1024
COL_TILE = 512
DOWN_K_TILE = 512
DOWN_X_CHUNK = 256
FFN_ROW_TILE = 2048
FFN_COL_TILE = 512
CONV_ROWS = 512
SAMPLE_GROUP = 8
CARRY_ROWS = V7X_SUBLANES


def _nbytes(shape, dtype):
    n = 1
    for s in shape:
        n *= s
    return n * jnp.dtype(dtype).itemsize


def _vmem_limit(block_bytes, scratch_bytes):
    need = 2 * block_bytes + scratch_bytes
    return int(min(need + max(need // 4, 8 << 20), V7X_VMEM_BYTES - (6 << 20)))


def _rmsnorm(x, g):
    r = lax.rsqrt(jnp.mean(x * x, axis=-1, keepdims=True) + EPS)
    return (x * r) * g


def _softmax_rows(s):
    m = jnp.max(s, axis=-1, keepdims=True)
    e = jnp.exp(s - m)
    return e / jnp.sum(e, axis=-1, keepdims=True)


def _dot(a, b):
    return jnp.dot(a, b, preferred_element_type=F32)


def _dot_nt(a, b):
    return lax.dot_general(a, b, (((1,), (1,)), ((), ())), preferred_element_type=F32)


def _prompt_rows_map(i, j):
    return (i, 0)


def _prompt_tile_map(i, j):
    return (i, j)


def _sample_tile_map(n_pt):
    return lambda i, j: (0, jnp.where(i == n_pt - 1, j, 0))


def _const_map(i, j):
    return (0, 0)


def _norm_matmul_kernel(xp_ref, xs_ref, g_ref, w_ref, zp_ref, zs_ref, hp_ref, hs_ref, *, n_pt):
    i, j = pl.program_id(0), pl.program_id(1)

    @pl.when(j == 0)
    def _():
        hp_ref[...] = _rmsnorm(xp_ref[...], g_ref[...]).astype(BF16)

    zp_ref[...] = _dot(hp_ref[...], w_ref[...].astype(BF16)).astype(zp_ref.dtype)

    @pl.when(i == n_pt - 1)
    def _():
        @pl.when(j == 0)
        def _():
            hs_ref[...] = _rmsnorm(xs_ref[...], g_ref[...]).astype(BF16)

        zs_ref[...] = _dot(hs_ref[...], w_ref[...].astype(BF16)).astype(zs_ref.dtype)


def _norm_matmul(xp, xs, g, w, *, layer, w_layer):
    rp, d = xp.shape
    rs = xs.shape[0]
    n = w.shape[2]
    tm, tn = PROMPT_TILE, COL_TILE
    n_pt, n_j = rp // tm, n // tn
    blocks = (_nbytes((tm, d), F32) + _nbytes((rs, d), F32) + _nbytes((d, tn), F32)
              + _nbytes((tm, tn), BF16) + _nbytes((rs, tn), BF16))
    scratch = _nbytes((tm, d), BF16) + _nbytes((rs, d), BF16) + _nbytes((d, tn), BF16)
    return pl.pallas_call(
        functools.partial(_norm_matmul_kernel, n_pt=n_pt),
        grid=(n_pt, n_j),
        in_specs=[
            pl.BlockSpec((tm, d), _prompt_rows_map),
            pl.BlockSpec((rs, d), _const_map),
            pl.BlockSpec((None, 1, d), lambda i, j: (layer, 0, 0)),
            pl.BlockSpec((None, d, tn), lambda i, j: (w_layer, 0, j)),
        ],
        out_specs=[
            pl.BlockSpec((tm, tn), _prompt_tile_map),
            pl.BlockSpec((rs, tn), _sample_tile_map(n_pt)),
        ],
        out_shape=[jax.ShapeDtypeStruct((rp, n), BF16), jax.ShapeDtypeStruct((rs, n), BF16)],
        scratch_shapes=[pltpu.VMEM((tm, d), BF16), pltpu.VMEM((rs, d), BF16)],
        compiler_params=pltpu.CompilerParams(
            dimension_semantics=("arbitrary", "arbitrary"),
            vmem_limit_bytes=_vmem_limit(blocks, scratch)),
        name="norm_matmul",
    )(xp, xs, g, w)


def _swiglu(h, wg, wu):
    gate = _dot(h, wg)
    up = _dot(h, wu)
    return (gate * jax.nn.sigmoid(gate) * up).astype(BF16)


def _ffn_up_kernel(hp_ref, hs_ref, wg_ref, wu_ref, ap_ref, as_ref, *, n_pt):
    ap_ref[...] = _swiglu(hp_ref[...], wg_ref[...].astype(BF16), wu_ref[...].astype(BF16))

    @pl.when(pl.program_id(0) == n_pt - 1)
    def _():
        as_ref[...] = _swiglu(hs_ref[...], wg_ref[...].astype(BF16), wu_ref[...].astype(BF16))


def _ffn_up(hp, hs, wg, wu, *, layer):
    rp, d = hp.shape
    rs = hs.shape[0]
    n = wg.shape[2]
    tm, tn = FFN_ROW_TILE, FFN_COL_TILE
    n_pt, n_j = rp // tm, n // tn
    blocks = (_nbytes((tm, d), BF16) + _nbytes((rs, d), BF16) + 2 * _nbytes((d, tn), F32)
              + _nbytes((tm, tn), BF16) + _nbytes((rs, tn), BF16))
    temps = 2 * _nbytes((d, tn), BF16) + 3 * _nbytes((tm, tn), F32)
    w_spec = pl.BlockSpec((None, d, tn), lambda i, j: (layer, 0, j))
    return pl.pallas_call(
        functools.partial(_ffn_up_kernel, n_pt=n_pt),
        grid=(n_pt, n_j),
        in_specs=[
            pl.BlockSpec((tm, d), _prompt_rows_map),
            pl.BlockSpec((rs, d), _const_map),
            w_spec,
            w_spec,
        ],
        out_specs=[
            pl.BlockSpec((tm, tn), _prompt_tile_map),
            pl.BlockSpec((rs, tn), _sample_tile_map(n_pt)),
        ],
        out_shape=[jax.ShapeDtypeStruct((rp, n), BF16), jax.ShapeDtypeStruct((rs, n), BF16)],
        compiler_params=pltpu.CompilerParams(
            dimension_semantics=("arbitrary", "arbitrary"),
            vmem_limit_bytes=_vmem_limit(blocks, temps)),
        name="ffn_up",
    )(hp, hs, wg, wu)


def _out_proj_kernel(ap_ref, as_ref, w_ref, xp_ref, xs_ref, g_ref, op_ref, os_ref, hp_ref, hs_ref, wb_ref, *, n_pt):
    s = pl.program_id(0)

    @pl.when(s == 0)
    def _():
        wb_ref[...] = w_ref[...].astype(BF16)

    @pl.when(s < n_pt)
    def _():
        x = xp_ref[...] + _dot(ap_ref[...], wb_ref[...])
        op_ref[...] = x
        hp_ref[...] = _rmsnorm(x, g_ref[...]).astype(BF16)

    @pl.when(s == n_pt)
    def _():
        x = xs_ref[...] + _dot(as_ref[...], wb_ref[...])
        os_ref[...] = x
        hs_ref[...] = _rmsnorm(x, g_ref[...]).astype(BF16)


def _out_proj(ap, as_, w, xp, xs, g, *, layer):
    rp, k = ap.shape
    rs = as_.shape[0]
    n = w.shape[2]
    tm = rs
    n_pt = rp // tm
    rows_map = lambda s: (jnp.minimum(s, n_pt - 1), 0)
    once = dict(pipeline_mode=pl.Buffered(1))
    blocks = _nbytes((tm, k), BF16) + 2 * _nbytes((tm, n), F32) + _nbytes((tm, n), BF16)
    resident = _nbytes((k, n), F32) + _nbytes((k, n), BF16) + blocks
    return pl.pallas_call(
        functools.partial(_out_proj_kernel, n_pt=n_pt),
        grid=(n_pt + 1,),
        in_specs=[
            pl.BlockSpec((tm, k), rows_map),
            pl.BlockSpec((rs, k), lambda s: (0, 0), **once),
            pl.BlockSpec((None, k, n), lambda s: (0, 0, 0), **once),
            pl.BlockSpec((tm, n), rows_map),
            pl.BlockSpec((rs, n), lambda s: (0, 0), **once),
            pl.BlockSpec((None, 1, n), lambda s: (layer, 0, 0)),
        ],
        out_specs=[
            pl.BlockSpec((tm, n), rows_map),
            pl.BlockSpec((rs, n), lambda s: (0, 0)),
            pl.BlockSpec((tm, n), rows_map),
            pl.BlockSpec((rs, n), lambda s: (0, 0)),
        ],
        out_shape=[jax.ShapeDtypeStruct((rp, n), F32), jax.ShapeDtypeStruct((rs, n), F32),
                   jax.ShapeDtypeStruct((rp, n), BF16), jax.ShapeDtypeStruct((rs, n), BF16)],
        scratch_shapes=[pltpu.VMEM((k, n), BF16)],
        compiler_params=pltpu.CompilerParams(
            dimension_semantics=("arbitrary",),
            vmem_limit_bytes=_vmem_limit(blocks, resident + 2 * _nbytes((tm, n), F32))),
        name="out_proj",
    )(ap, as_, w, xp, xs, g)


def _matmul_kernel(hp_ref, hs_ref, w_ref, zp_ref, zs_ref, *, n_pt):
    zp_ref[...] = _dot(hp_ref[...], w_ref[...].astype(BF16)).astype(zp_ref.dtype)

    @pl.when(pl.program_id(0) == n_pt - 1)
    def _():
        zs_ref[...] = _dot(hs_ref[...], w_ref[...].astype(BF16)).astype(zs_ref.dtype)


def _matmul(hp, hs, w, *, w_layer):
    rp, d = hp.shape
    rs = hs.shape[0]
    n = w.shape[2]
    tm, tn = FFN_ROW_TILE, COL_TILE
    n_pt, n_j = rp // tm, n // tn
    blocks = (_nbytes((tm, d), BF16) + _nbytes((rs, d), BF16) + _nbytes((d, tn), F32)
              + _nbytes((tm, tn), BF16) + _nbytes((rs, tn), BF16))
    temps = _nbytes((d, tn), BF16) + _nbytes((tm, tn), F32)
    return pl.pallas_call(
        functools.partial(_matmul_kernel, n_pt=n_pt),
        grid=(n_pt, n_j),
        in_specs=[
            pl.BlockSpec((tm, d), _prompt_rows_map),
            pl.BlockSpec((rs, d), _const_map),
            pl.BlockSpec((None, d, tn), lambda i, j: (w_layer, 0, j)),
        ],
        out_specs=[
            pl.BlockSpec((tm, tn), _prompt_tile_map),
            pl.BlockSpec((rs, tn), _sample_tile_map(n_pt)),
        ],
        out_shape=[jax.ShapeDtypeStruct((rp, n), BF16), jax.ShapeDtypeStruct((rs, n), BF16)],
        compiler_params=pltpu.CompilerParams(
            dimension_semantics=("arbitrary", "arbitrary"),
            vmem_limit_bytes=_vmem_limit(blocks, temps)),
        name="matmul",
    )(hp, hs, w)


def _ffn_down_kernel(ap_ref, as_ref, w_ref, xp_ref, xs_ref, g_ref, *out_refs, n_pt, n_k, n_xc, final):
    i, k = pl.program_id(0), pl.program_id(1)
    if final:
        op_ref, os_ref = out_refs
    else:
        op_ref, os_ref, hp_ref, hs_ref = out_refs
    xc = xp_ref.shape[1]

    @pl.when(k == 0)
    def _():
        op_ref[...] = _dot(ap_ref[...], w_ref[...].astype(BF16))

    @pl.when(k > 0)
    def _():
        op_ref[...] = op_ref[...] + _dot(ap_ref[...], w_ref[...].astype(BF16))

    for c in range(n_xc):
        @pl.when(k == c)
        def _():
            op_ref[:, c * xc:(c + 1) * xc] = op_ref[:, c * xc:(c + 1) * xc] + xp_ref[...]

    @pl.when(k == n_k - 1)
    def _():
        if final:
            op_ref[...] = _rmsnorm(op_ref[...], g_ref[...])
        else:
            hp_ref[...] = _rmsnorm(op_ref[...], g_ref[...]).astype(BF16)

    @pl.when(i == n_pt - 1)
    def _():
        @pl.when(k == 0)
        def _():
            os_ref[...] = xs_ref[...] + _dot(as_ref[...], w_ref[...].astype(BF16))

        @pl.when(k > 0)
        def _():
            os_ref[...] = os_ref[...] + _dot(as_ref[...], w_ref[...].astype(BF16))

        @pl.when(k == n_k - 1)
        def _():
            if final:
                os_ref[...] = _rmsnorm(os_ref[...], g_ref[...])
            else:
                hs_ref[...] = _rmsnorm(os_ref[...], g_ref[...]).astype(BF16)


def _ffn_down(ap, as_, w, xp, xs, g, *, w_layer, g_layer, final):
    rp, kdim = ap.shape
    rs = as_.shape[0]
    n = w.shape[2]
    tm, tk, xc = PROMPT_TILE, DOWN_K_TILE, DOWN_X_CHUNK
    n_pt, n_k, n_xc = rp // tm, kdim // tk, n // xc
    assert n_xc <= n_k
    rows = lambda i, k: (i, 0)
    blocks = (_nbytes((tm, tk), BF16) + _nbytes((rs, tk), BF16) + _nbytes((tk, n), F32)
              + _nbytes((tm, xc), F32) + _nbytes((rs, n), F32)
              + _nbytes((tm, n), F32) + _nbytes((rs, n), F32))
    out_specs = [pl.BlockSpec((tm, n), rows), pl.BlockSpec((rs, n), _const_map)]
    out_shape = [jax.ShapeDtypeStruct((rp, n), F32), jax.ShapeDtypeStruct((rs, n), F32)]
    if not final:
        blocks += _nbytes((tm, n), BF16) + _nbytes((rs, n), BF16)
        out_specs += [pl.BlockSpec((tm, n), rows), pl.BlockSpec((rs, n), _const_map)]
        out_shape += [jax.ShapeDtypeStruct((rp, n), BF16), jax.ShapeDtypeStruct((rs, n), BF16)]
    return pl.pallas_call(
        functools.partial(_ffn_down_kernel, n_pt=n_pt, n_k=n_k, n_xc=n_xc, final=final),
        grid=(n_pt, n_k),
        in_specs=[
            pl.BlockSpec((tm, tk), lambda i, k: (i, k)),
            pl.BlockSpec((rs, tk), lambda i, k: (0, jnp.where(i == n_pt - 1, k, 0))),
            pl.BlockSpec((None, tk, n), lambda i, k: (w_layer, k, 0)),
            pl.BlockSpec((tm, xc), lambda i, k: (i, jnp.minimum(k, n_xc - 1))),
            pl.BlockSpec((rs, n), _const_map),
            pl.BlockSpec((None, 1, n), lambda i, k: (g_layer, 0, 0)),
        ],
        out_specs=out_specs,
        out_shape=out_shape,
        compiler_params=pltpu.CompilerParams(
            dimension_semantics=("arbitrary", "arbitrary"),
            vmem_limit_bytes=_vmem_limit(blocks, _nbytes((tk, n), BF16))),
        name="ffn_down",
    )(ap, as_, w, xp, xs, g)


def _mem_kv_kernel(x_ref, g_ref, w_ref, k_ref, v_ref):
    h = _rmsnorm(x_ref[...], g_ref[...]).astype(BF16)
    kv = _dot(h, w_ref[...].astype(BF16))
    k_ref[...] = kv[:, :MEM_WIDTH]
    v_ref[...] = kv[:, MEM_WIDTH:]


def _mem_kv(mem, g, w):
    rows, d = mem.shape
    depth = w.shape[0]
    tm = 512
    out = jax.ShapeDtypeStruct((depth, rows, MEM_WIDTH), F32)
    blocks = (_nbytes((tm, d), F32) + _nbytes((d, 2 * MEM_WIDTH), F32) + 2 * _nbytes((tm, MEM_WIDTH), F32))
    return pl.pallas_call(
        _mem_kv_kernel,
        grid=(depth, rows // tm),
        in_specs=[
            pl.BlockSpec((tm, d), lambda l, i: (i, 0)),
            pl.BlockSpec((None, 1, d), lambda l, i: (l, 0, 0)),
            pl.BlockSpec((None, d, 2 * MEM_WIDTH), lambda l, i: (l, 0, 0)),
        ],
        out_specs=[
            pl.BlockSpec((None, tm, MEM_WIDTH), lambda l, i: (l, i, 0)),
            pl.BlockSpec((None, tm, MEM_WIDTH), lambda l, i: (l, i, 0)),
        ],
        out_shape=[out, out],
        compiler_params=pltpu.CompilerParams(
            dimension_semantics=("arbitrary", "arbitrary"),
            vmem_limit_bytes=_vmem_limit(blocks, _nbytes((d, 2 * MEM_WIDTH), BF16))),
        name="mem_kv",
    )(mem, g, w)


def _cross_attention_head(q, k, v):
    s = _dot_nt(q, k) * (MEM_HEAD_DIM ** -0.5)
    p = _softmax_rows(s).astype(BF16)
    return _dot(p, v)


def _conv_prompt_kernel(z_ref, mk_ref, mv_ref, cw_ref, mix_ref, st_ref, ext_ref, *, tiles_per_seq):
    s = pl.program_id(0)
    tq = z_ref.shape[0]

    @pl.when(s % tiles_per_seq == 0)
    def _():
        ext_ref[0:CARRY_ROWS, :] = jnp.zeros((CARRY_ROWS, TOKEN_WIDTH), F32)

    c = z_ref[:, TOKEN_WIDTH:2 * TOKEN_WIDTH].astype(F32)
    u = z_ref[:, 2 * TOKEN_WIDTH:3 * TOKEN_WIDTH].astype(F32)
    cu = c * u
    ext_ref[CARRY_ROWS:CARRY_ROWS + tq, :] = cu
    conv = (cw_ref[0:1, :] * ext_ref[CARRY_ROWS - 2:CARRY_ROWS - 2 + tq, :]
            + cw_ref[1:2, :] * ext_ref[CARRY_ROWS - 1:CARRY_ROWS - 1 + tq, :]
            + cw_ref[2:3, :] * cu)
    b = z_ref[:, 0:TOKEN_WIDTH].astype(F32)
    mix_ref[:, 0:TOKEN_WIDTH] = (b * conv).astype(BF16)
    st_ref[...] = ext_ref[CARRY_ROWS + tq - 2:CARRY_ROWS + tq, :]
    ext_ref[0:CARRY_ROWS, :] = ext_ref[tq:tq + CARRY_ROWS, :]

    for h in range(MEM_HEADS):
        lo, hi = h * MEM_HEAD_DIM, (h + 1) * MEM_HEAD_DIM
        q = z_ref[:, 3 * TOKEN_WIDTH + lo:3 * TOKEN_WIDTH + hi]
        o = _cross_attention_head(q, mk_ref[:, lo:hi].astype(BF16), mv_ref[:, lo:hi].astype(BF16))
        mix_ref[:, TOKEN_WIDTH + lo:TOKEN_WIDTH + hi] = o.astype(BF16)


def _conv_prompt(z, mk, mv, conv_w, *, layer, batch, seq):
    rows, zc = z.shape
    tq = CONV_ROWS
    tiles_per_seq = seq // tq
    blocks = (_nbytes((tq, zc), BF16) + 2 * _nbytes((N_MEM, MEM_WIDTH), F32) + _nbytes((tq, D_MODEL), BF16))
    scratch = _nbytes((tq + CARRY_ROWS, TOKEN_WIDTH), F32)
    return pl.pallas_call(
        functools.partial(_conv_prompt_kernel, tiles_per_seq=tiles_per_seq),
        grid=(batch * tiles_per_seq,),
        in_specs=[
            pl.BlockSpec((tq, zc), lambda s: (s, 0)),
            pl.BlockSpec((None, N_MEM, MEM_WIDTH), lambda s: (layer, s // tiles_per_seq, 0)),
            pl.BlockSpec((None, N_MEM, MEM_WIDTH), lambda s: (layer, s // tiles_per_seq, 0)),
            pl.BlockSpec((None, CONV_WIDTH, TOKEN_WIDTH), lambda s: (0, 0, 0)),
        ],
        out_specs=[
            pl.BlockSpec((tq, D_MODEL), lambda s: (s, 0)),
            pl.BlockSpec((None, CONV_WIDTH - 1, TOKEN_WIDTH), lambda s: (s // tiles_per_seq, 0, 0)),
        ],
        out_shape=[
            jax.ShapeDtypeStruct((rows, D_MODEL), BF16),
            jax.ShapeDtypeStruct((batch, CONV_WIDTH - 1, TOKEN_WIDTH), F32),
        ],
        scratch_shapes=[pltpu.VMEM((tq + CARRY_ROWS, TOKEN_WIDTH), F32)],
        compiler_params=pltpu.CompilerParams(
            dimension_semantics=("arbitrary",),
            vmem_limit_bytes=_vmem_limit(blocks, scratch + 6 * _nbytes((tq, TOKEN_WIDTH), F32))),
        name="conv_prompt",
    )(z, mk, mv, conv_w)


def _conv_sample_kernel(z_ref, st_ref, mk_ref, mv_ref, cw_ref, mix_ref, nst_ref, ext_ref, mixf_ref, *, dec_seq):
    t = dec_seq
    for n in range(SAMPLE_GROUP):
        r0, r1 = n * t, (n + 1) * t
        c = z_ref[r0:r1, TOKEN_WIDTH:2 * TOKEN_WIDTH].astype(F32)
        u = z_ref[r0:r1, 2 * TOKEN_WIDTH:3 * TOKEN_WIDTH].astype(F32)
        cu = c * u
        ext_ref[CARRY_ROWS - 2:CARRY_ROWS, :] = st_ref[n]
        ext_ref[CARRY_ROWS:CARRY_ROWS + t, :] = cu
        conv = (cw_ref[0:1, :] * ext_ref[CARRY_ROWS - 2:CARRY_ROWS - 2 + t, :]
                + cw_ref[1:2, :] * ext_ref[CARRY_ROWS - 1:CARRY_ROWS - 1 + t, :]
                + cw_ref[2:3, :] * cu)
        b = z_ref[r0:r1, 0:TOKEN_WIDTH].astype(F32)
        mixf_ref[r0:r1, 0:TOKEN_WIDTH] = b * conv
        nst_ref[n] = ext_ref[CARRY_ROWS + t - 2:CARRY_ROWS + t, :]
        for h in range(MEM_HEADS):
            lo, hi = h * MEM_HEAD_DIM, (h + 1) * MEM_HEAD_DIM
            q = z_ref[r0:r1, 3 * TOKEN_WIDTH + lo:3 * TOKEN_WIDTH + hi]
            o = _cross_attention_head(q, mk_ref[n, :, h, :].astype(BF16), mv_ref[n, :, h, :].astype(BF16))
            mixf_ref[r0:r1, TOKEN_WIDTH + lo:TOKEN_WIDTH + hi] = o
    mix_ref[...] = mixf_ref[...].astype(BF16)


def _conv_sample(z, state, mem_k, mem_v, conv_w, *, layer, dec_seq):
    rows, zc = z.shape
    dec_batch = state.shape[1]
    g = SAMPLE_GROUP
    gr = g * dec_seq
    blocks = (_nbytes((gr, zc), BF16) + 2 * _nbytes((g, CONV_WIDTH - 1, TOKEN_WIDTH), F32)
              + 2 * _nbytes((g, N_MEM, V7X_SUBLANES, MEM_HEAD_DIM), F32) + _nbytes((gr, D_MODEL), BF16))
    scratch = _nbytes((2 * CARRY_ROWS, TOKEN_WIDTH), F32) + _nbytes((gr, D_MODEL), F32)
    return pl.pallas_call(
        functools.partial(_conv_sample_kernel, dec_seq=dec_seq),
        grid=(dec_batch // g,),
        in_specs=[
            pl.BlockSpec((gr, zc), lambda i: (i, 0)),
            pl.BlockSpec((None, g, CONV_WIDTH - 1, TOKEN_WIDTH), lambda i: (0, i, 0, 0)),
            pl.BlockSpec((None, g, N_MEM, MEM_HEADS, MEM_HEAD_DIM), lambda i: (layer, i, 0, 0, 0)),
            pl.BlockSpec((None, g, N_MEM, MEM_HEADS, MEM_HEAD_DIM), lambda i: (layer, i, 0, 0, 0)),
            pl.BlockSpec((None, CONV_WIDTH, TOKEN_WIDTH), lambda i: (0, 0, 0)),
        ],
        out_specs=[
            pl.BlockSpec((gr, D_MODEL), lambda i: (i, 0)),
            pl.BlockSpec((g, CONV_WIDTH - 1, TOKEN_WIDTH), lambda i: (i, 0, 0)),
        ],
        out_shape=[
            jax.ShapeDtypeStruct((rows, D_MODEL), BF16),
            jax.ShapeDtypeStruct((dec_batch, CONV_WIDTH - 1, TOKEN_WIDTH), F32),
        ],
        scratch_shapes=[pltpu.VMEM((2 * CARRY_ROWS, TOKEN_WIDTH), F32), pltpu.VMEM((gr, D_MODEL), F32)],
        compiler_params=pltpu.CompilerParams(
            dimension_semantics=("arbitrary",),
            vmem_limit_bytes=_vmem_limit(blocks, scratch)),
        name="conv_sample",
    )(z, state, mem_k, mem_v, conv_w)


def _band_attention_head(q, k, v, sink, mask):
    s = _dot_nt(q * (HEAD_DIM ** -0.5), k)
    s = jnp.where(mask, s, -jnp.inf)
    m = jnp.maximum(jnp.max(s, axis=-1, keepdims=True), sink)
    e = jnp.exp(s - m)
    p = e / (jnp.sum(e, axis=-1, keepdims=True) + jnp.exp(sink - m))
    return _dot(p.astype(BF16), v)


def _swa_prompt_kernel(sink_ref, zq_ref, zp_ref, mk_ref, mv_ref, mix_ref, *, blocks_per_seq):
    s = pl.program_id(0)
    k_off = TOKEN_WIDTH
    v_off = TOKEN_WIDTH + KV_WIDTH
    qm_off = TOKEN_WIDTH + 2 * KV_WIDTH
    has_prev = (s % blocks_per_seq) > 0
    row = lax.broadcasted_iota(jnp.int32, (WINDOW, 2 * WINDOW), 0)
    col = lax.broadcasted_iota(jnp.int32, (WINDOW, 2 * WINDOW), 1)
    mask = jnp.logical_or(jnp.logical_and(jnp.logical_and(col < WINDOW, col > row), has_prev),
                          jnp.logical_and(col >= WINDOW, col - WINDOW <= row))
    for kh in range(N_KV_HEADS):
        lo, hi = kh * HEAD_DIM, (kh + 1) * HEAD_DIM
        k = jnp.concatenate([zp_ref[:, lo:hi], zq_ref[:, k_off + lo:k_off + hi]], axis=0)
        v = jnp.concatenate([zp_ref[:, KV_WIDTH + lo:KV_WIDTH + hi], zq_ref[:, v_off + lo:v_off + hi]], axis=0)
        for g in range(GROUP):
            h = kh * GROUP + g
            q = zq_ref[:, h * HEAD_DIM:(h + 1) * HEAD_DIM]
            o = _band_attention_head(q, k, v, sink_ref[h], mask)
            mix_ref[:, h * HEAD_DIM:(h + 1) * HEAD_DIM] = o.astype(BF16)
    for h in range(MEM_HEADS):
        lo, hi = h * MEM_HEAD_DIM, (h + 1) * MEM_HEAD_DIM
        q = zq_ref[:, qm_off + lo:qm_off + hi]
        o = _cross_attention_head(q, mk_ref[:, lo:hi].astype(BF16), mv_ref[:, lo:hi].astype(BF16))
        mix_ref[:, TOKEN_WIDTH + lo:TOKEN_WIDTH + hi] = o.astype(BF16)


def _swa_prompt(z, mk, mv, sinks, *, layer, batch, seq):
    rows, zc = z.shape
    tq = WINDOW
    blocks_per_seq = seq // tq
    kv_col_block = TOKEN_WIDTH // (2 * KV_WIDTH)
    blocks = (_nbytes((tq, zc), BF16) + _nbytes((tq, 2 * KV_WIDTH), BF16)
              + 2 * _nbytes((N_MEM, MEM_WIDTH), F32) + _nbytes((tq, D_MODEL), BF16))
    return pl.pallas_call(
        functools.partial(_swa_prompt_kernel, blocks_per_seq=blocks_per_seq),
        grid=(batch * blocks_per_seq,),
        in_specs=[
            pl.BlockSpec(memory_space=pltpu.SMEM),
            pl.BlockSpec((tq, zc), lambda s: (s, 0)),
            pl.BlockSpec((tq, 2 * KV_WIDTH), lambda s: (jnp.maximum(s - 1, 0), kv_col_block)),
            pl.BlockSpec((None, N_MEM, MEM_WIDTH), lambda s: (layer, s // blocks_per_seq, 0)),
            pl.BlockSpec((None, N_MEM, MEM_WIDTH), lambda s: (layer, s // blocks_per_seq, 0)),
        ],
        out_specs=pl.BlockSpec((tq, D_MODEL), lambda s: (s, 0)),
        out_shape=jax.ShapeDtypeStruct((rows, D_MODEL), BF16),
        compiler_params=pltpu.CompilerParams(
            dimension_semantics=("arbitrary",),
            vmem_limit_bytes=_vmem_limit(blocks, 0)),
        name="swa_prompt",
    )(sinks, z, z, mk, mv)


def _swa_sample_kernel(sink_ref, z_ref, ck_ref, cv_ref, mk_ref, mv_ref,
                       mix_ref, nk_ref, nv_ref, knew_ref, vnew_ref, mixf_ref, *, dec_seq):
    t = dec_seq
    k_off = TOKEN_WIDTH
    v_off = TOKEN_WIDTH + KV_WIDTH
    qm_off = TOKEN_WIDTH + 2 * KV_WIDTH
    rows = GROUP * t
    qi = lax.broadcasted_iota(jnp.int32, (rows, 2 * WINDOW), 0) % t
    col = lax.broadcasted_iota(jnp.int32, (rows, 2 * WINDOW), 1)
    mask = jnp.logical_or(jnp.logical_and(col < WINDOW, col > qi),
                          jnp.logical_and(col >= WINDOW, col - WINDOW <= qi))
    knew_ref[...] = jnp.zeros((WINDOW, KV_WIDTH), F32)
    vnew_ref[...] = jnp.zeros((WINDOW, KV_WIDTH), F32)
    for n in range(SAMPLE_GROUP):
        r0, r1 = n * t, (n + 1) * t
        k_new = z_ref[r0:r1, k_off:k_off + KV_WIDTH].astype(F32)
        v_new = z_ref[r0:r1, v_off:v_off + KV_WIDTH].astype(F32)
        knew_ref[0:t, :] = k_new
        vnew_ref[0:t, :] = v_new
        nk_ref[n, 0:WINDOW - t, :] = ck_ref[n, t:WINDOW, :]
        nv_ref[n, 0:WINDOW - t, :] = cv_ref[n, t:WINDOW, :]
        nk_ref[n, WINDOW - t:WINDOW, :] = k_new
        nv_ref[n, WINDOW - t:WINDOW, :] = v_new
        for kh in range(N_KV_HEADS):
            lo, hi = kh * HEAD_DIM, (kh + 1) * HEAD_DIM
            k = jnp.concatenate([ck_ref[n, :, lo:hi], knew_ref[:, lo:hi]], axis=0).astype(BF16)
            v = jnp.concatenate([cv_ref[n, :, lo:hi], vnew_ref[:, lo:hi]], axis=0).astype(BF16)
            q = jnp.concatenate(
                [z_ref[r0:r1, (kh * GROUP + g) * HEAD_DIM:(kh * GROUP + g + 1) * HEAD_DIM].astype(F32)
                 for g in range(GROUP)], axis=0).astype(BF16)
            sink = jnp.concatenate(
                [jnp.full((t, 1), sink_ref[kh * GROUP + g], F32) for g in range(GROUP)], axis=0)
            o = _band_attention_head(q, k, v, sink, mask)
            for g in range(GROUP):
                h = kh * GROUP + g
                mixf_ref[r0:r1, h * HEAD_DIM:(h + 1) * HEAD_DIM] = o[g * t:(g + 1) * t, :]
        for h in range(MEM_HEADS):
            lo, hi = h * MEM_HEAD_DIM, (h + 1) * MEM_HEAD_DIM
            q = z_ref[r0:r1, qm_off + lo:qm_off + hi]
            o = _cross_attention_head(q, mk_ref[n, :, h, :].astype(BF16), mv_ref[n, :, h, :].astype(BF16))
            mixf_ref[r0:r1, TOKEN_WIDTH + lo:TOKEN_WIDTH + hi] = o
    mix_ref[...] = mixf_ref[...].astype(BF16)


def _swa_sample(z, cache_k, cache_v, mem_k, mem_v, sinks, *, layer, dec_seq):
    rows, zc = z.shape
    dec_batch = cache_k.shape[0]
    g = SAMPLE_GROUP
    gr = g * dec_seq
    win = jax.ShapeDtypeStruct((dec_batch, WINDOW, KV_WIDTH), F32)
    blocks = (_nbytes((gr, zc), BF16) + 4 * _nbytes((g, WINDOW, KV_WIDTH), F32)
              + 2 * _nbytes((g, N_MEM, V7X_SUBLANES, MEM_HEAD_DIM), F32) + _nbytes((gr, D_MODEL), BF16))
    scratch = 2 * _nbytes((WINDOW, KV_WIDTH), F32) + _nbytes((gr, D_MODEL), F32)
    return pl.pallas_call(
        functools.partial(_swa_sample_kernel, dec_seq=dec_seq),
        grid=(dec_batch // g,),
        in_specs=[
            pl.BlockSpec(memory_space=pltpu.SMEM),
            pl.BlockSpec((gr, zc), lambda i: (i, 0)),
            pl.BlockSpec((g, WINDOW, KV_WIDTH), lambda i: (i, 0, 0)),
            pl.BlockSpec((g, WINDOW, KV_WIDTH), lambda i: (i, 0, 0)),
            pl.BlockSpec((None, g, N_MEM, MEM_HEADS, MEM_HEAD_DIM), lambda i: (layer, i, 0, 0, 0)),
            pl.BlockSpec((None, g, N_MEM, MEM_HEADS, MEM_HEAD_DIM), lambda i: (layer, i, 0, 0, 0)),
        ],
        out_specs=[
            pl.BlockSpec((gr, D_MODEL), lambda i: (i, 0)),
            pl.BlockSpec((g, WINDOW, KV_WIDTH), lambda i: (i, 0, 0)),
            pl.BlockSpec((g, WINDOW, KV_WIDTH), lambda i: (i, 0, 0)),
        ],
        out_shape=[jax.ShapeDtypeStruct((rows, D_MODEL), BF16), win, win],
        scratch_shapes=[pltpu.VMEM((WINDOW, KV_WIDTH), F32), pltpu.VMEM((WINDOW, KV_WIDTH), F32),
                        pltpu.VMEM((gr, D_MODEL), F32)],
        compiler_params=pltpu.CompilerParams(
            dimension_semantics=("arbitrary",),
            vmem_limit_bytes=_vmem_limit(blocks, scratch)),
        name="swa_sample",
    )(sinks, z, cache_k, cache_v, mem_k, mem_v)


def kernel(x_prompt, x_sample, mem_prompt, state_conv, cache_win_k, cache_win_v, cache_mem_k, cache_mem_v,
           norm_mix, norm_mem, w_mem_kv, norm_ffn, w_gate, w_up, w_down,
           conv_w_in, conv_w, conv_w_out, attn_w_in, attn_sinks, attn_w_out, norm_final):
    batch, seq, d = x_prompt.shape
    dec_batch, dec_seq, _ = x_sample.shape
    depth = norm_mix.shape[0]
    d_ff = w_gate.shape[2]
    prompt_rows = batch * seq
    sample_rows = dec_batch * dec_seq
    assert d == D_MODEL and depth == 2 and seq % CONV_ROWS == 0 and seq % WINDOW == 0
    assert prompt_rows % PROMPT_TILE == 0 and dec_batch % SAMPLE_GROUP == 0
    assert dec_seq == V7X_SUBLANES and d_ff % FFN_COL_TILE == 0 and d % COL_TILE == 0
    assert d_ff % DOWN_K_TILE == 0 and d % DOWN_X_CHUNK == 0
    assert prompt_rows % FFN_ROW_TILE == 0 and prompt_rows % sample_rows == 0

    xp = x_prompt.reshape(prompt_rows, d)
    xs = x_sample.reshape(sample_rows, d)
    mem = mem_prompt.reshape(batch * N_MEM, d)
    mem_k_s, mem_v_s = cache_mem_k, cache_mem_v
    g_mix = norm_mix.reshape(depth, 1, d)
    g_ffn = norm_ffn.reshape(depth, 1, d)

    mk, mv = _mem_kv(mem, norm_mem.reshape(depth, 1, d), w_mem_kv)

    zp, zs = _norm_matmul(xp, xs, g_mix, conv_w_in, layer=0, w_layer=0)
    mix_s, conv_s = _conv_sample(zs, state_conv, mem_k_s, mem_v_s, conv_w, layer=0, dec_seq=dec_seq)
    mix_p, conv_p = _conv_prompt(zp, mk, mv, conv_w, layer=0, batch=batch, seq=seq)
    xp, xs, hp, hs = _out_proj(mix_p, mix_s, conv_w_out, xp, xs, g_ffn, layer=0)
    ap, as_ = _ffn_up(hp, hs, w_gate, w_up, layer=0)
    xp, xs, hp, hs = _ffn_down(ap, as_, w_down, xp, xs, g_mix, w_layer=0, g_layer=1, final=False)

    zp, zs = _matmul(hp, hs, attn_w_in, w_layer=0)
    sinks = attn_sinks[0]
    mix_s, win_k_s, win_v_s = _swa_sample(
        zs, cache_win_k[0].reshape(dec_batch, WINDOW, KV_WIDTH), cache_win_v[0].reshape(dec_batch, WINDOW, KV_WIDTH),
        mem_k_s, mem_v_s, sinks, layer=1, dec_seq=dec_seq)
    mix_p = _swa_prompt(zp, mk, mv, sinks, layer=1, batch=batch, seq=seq)
    xp, xs, hp, hs = _out_proj(mix_p, mix_s, attn_w_out, xp, xs, g_ffn, layer=1)
    ap, as_ = _ffn_up(hp, hs, w_gate, w_up, layer=1)
    y_prompt, y_sample = _ffn_down(ap, as_, w_down, xp, xs, norm_final.reshape(1, 1, d),
                                   w_layer=1, g_layer=0, final=True)

    win_p = zp.reshape(batch, seq, -1)[:, seq - WINDOW:, TOKEN_WIDTH:TOKEN_WIDTH + 2 * KV_WIDTH].astype(F32)
    kv_shape = (1, -1, WINDOW, N_KV_HEADS, HEAD_DIM)
    mem_shape = (depth, batch, N_MEM, MEM_HEADS, MEM_HEAD_DIM)
    return (y_prompt.reshape(batch, seq, d),
            y_sample.reshape(dec_batch, dec_seq, d),
            conv_p[None],
            conv_s[None],
            win_p[..., :KV_WIDTH].reshape(kv_shape),
            win_p[..., KV_WIDTH:].reshape(kv_shape),
            win_k_s.reshape(kv_shape),
            win_v_s.reshape(kv_shape),
            mk.reshape(mem_shape),
            mv.reshape(mem_shape))
```

```python
import functools

import jax
import jax.numpy as jnp
from jax import lax
from jax.experimental import pallas as pl
from jax.experimental.pallas import tpu as pltpu

F32 = jnp.float32
BF16 = jnp.bfloat16

D_MODEL = 2048
N_MEM = 256
MEM_HEADS = 4
MEM_WIDTH = D_MODEL // 4
MEM_HEAD_DIM = MEM_WIDTH // MEM_HEADS
TOKEN_WIDTH = D_MODEL - MEM_WIDTH
CONV_WIDTH = 3
WINDOW = 128
HEAD_DIM = 64
N_HEADS = TOKEN_WIDTH // HEAD_DIM
N_KV_HEADS = 4
GROUP = N_HEADS // N_KV_HEADS
KV_WIDTH = N_KV_HEADS * HEAD_DIM
EPS = 1e-6

V7X_VMEM_BYTES = 64 * 1024 * 1024
V7X_SUBLANES = 8

PROMPT_TILE = 1024
COL_TILE = 512
DOWN_K_TILE = 1408
DOWN_X_CHUNK = 512
FFN_ROW_TILE = 2048
FFN_COL_TILE = 512
CONV_ROWS = 512
SAMPLE_GROUP = 8
CARRY_ROWS = V7X_SUBLANES


def _nbytes(shape, dtype):
    n = 1
    for s in shape:
        n *= s
    return n * jnp.dtype(dtype).itemsize


def _vmem_limit(block_bytes, scratch_bytes):
    need = 2 * block_bytes + scratch_bytes
    return int(min(need + max(need // 4, 8 << 20), V7X_VMEM_BYTES - (6 << 20)))


def _rmsnorm(x, g):
    r = lax.rsqrt(jnp.mean(x * x, axis=-1, keepdims=True) + EPS)
    return (x * r) * g


def _softmax_rows(s):
    m = jnp.max(s, axis=-1, keepdims=True)
    e = jnp.exp(s - m)
    return e / jnp.sum(e, axis=-1, keepdims=True)


def _dot(a, b):
    return jnp.dot(a, b, preferred_element_type=F32)


def _dot_nt(a, b):
    return lax.dot_general(a, b, (((1,), (1,)), ((), ())), preferred_element_type=F32)


def _prompt_rows_map(i, j):
    return (i, 0)


def _prompt_tile_map(i, j):
    return (i, j)


def _sample_tile_map(n_pt):
    return lambda i, j: (0, jnp.where(i == n_pt - 1, j, 0))


def _const_map(i, j):
    return (0, 0)


def _norm_matmul_kernel(xp_ref, xs_ref, g_ref, w_ref, zp_ref, zs_ref, hp_ref, hs_ref, *, n_pt):
    i, j = pl.program_id(0), pl.program_id(1)

    @pl.when(j == 0)
    def _():
        hp_ref[...] = _rmsnorm(xp_ref[...], g_ref[...]).astype(BF16)

    zp_ref[...] = _dot(hp_ref[...], w_ref[...].astype(BF16)).astype(zp_ref.dtype)

    @pl.when(i == n_pt - 1)
    def _():
        @pl.when(j == 0)
        def _():
            hs_ref[...] = _rmsnorm(xs_ref[...], g_ref[...]).astype(BF16)

        zs_ref[...] = _dot(hs_ref[...], w_ref[...].astype(BF16)).astype(zs_ref.dtype)


def _norm_matmul(xp, xs, g, w, *, layer, w_layer):
    rp, d = xp.shape
    rs = xs.shape[0]
    n = w.shape[2]
    tm, tn = PROMPT_TILE, COL_TILE
    n_pt, n_j = rp // tm, n // tn
    blocks = (_nbytes((tm, d), F32) + _nbytes((rs, d), F32) + _nbytes((d, tn), F32)
              + _nbytes((tm, tn), BF16) + _nbytes((rs, tn), BF16))
    scratch = _nbytes((tm, d), BF16) + _nbytes((rs, d), BF16) + _nbytes((d, tn), BF16)
    return pl.pallas_call(
        functools.partial(_norm_matmul_kernel, n_pt=n_pt),
        grid=(n_pt, n_j),
        in_specs=[
            pl.BlockSpec((tm, d), _prompt_rows_map),
            pl.BlockSpec((rs, d), _const_map),
            pl.BlockSpec((None, 1, d), lambda i, j: (layer, 0, 0)),
            pl.BlockSpec((None, d, tn), lambda i, j: (w_layer, 0, j)),
        ],
        out_specs=[
            pl.BlockSpec((tm, tn), _prompt_tile_map),
            pl.BlockSpec((rs, tn), _sample_tile_map(n_pt)),
        ],
        out_shape=[jax.ShapeDtypeStruct((rp, n), BF16), jax.ShapeDtypeStruct((rs, n), BF16)],
        scratch_shapes=[pltpu.VMEM((tm, d), BF16), pltpu.VMEM((rs, d), BF16)],
        compiler_params=pltpu.CompilerParams(
            dimension_semantics=("arbitrary", "arbitrary"),
            vmem_limit_bytes=_vmem_limit(blocks, scratch)),
        name="norm_matmul",
    )(xp, xs, g, w)


def _swiglu(h, wg, wu):
    gate = _dot(h, wg)
    up = _dot(h, wu)
    return (gate * jax.nn.sigmoid(gate) * up).astype(BF16)


def _ffn_up_kernel(hp_ref, hs_ref, wg_ref, wu_ref, ap_ref, as_ref, *, n_pt):
    ap_ref[...] = _swiglu(hp_ref[...], wg_ref[...].astype(BF16), wu_ref[...].astype(BF16))

    @pl.when(pl.program_id(0) == n_pt - 1)
    def _():
        as_ref[...] = _swiglu(hs_ref[...], wg_ref[...].astype(BF16), wu_ref[...].astype(BF16))


def _ffn_up(hp, hs, wg, wu, *, layer):
    rp, d = hp.shape
    rs = hs.shape[0]
    n = wg.shape[2]
    tm, tn = FFN_ROW_TILE, FFN_COL_TILE
    n_pt, n_j = rp // tm, n // tn
    blocks = (_nbytes((tm, d), BF16) + _nbytes((rs, d), BF16) + 2 * _nbytes((d, tn), F32)
              + _nbytes((tm, tn), BF16) + _nbytes((rs, tn), BF16))
    temps = 2 * _nbytes((d, tn), BF16) + 3 * _nbytes((tm, tn), F32)
    w_spec = pl.BlockSpec((None, d, tn), lambda i, j: (layer, 0, j))
    return pl.pallas_call(
        functools.partial(_ffn_up_kernel, n_pt=n_pt),
        grid=(n_pt, n_j),
        in_specs=[
            pl.BlockSpec((tm, d), _prompt_rows_map),
            pl.BlockSpec((rs, d), _const_map),
            w_spec,
            w_spec,
        ],
        out_specs=[
            pl.BlockSpec((tm, tn), _prompt_tile_map),
            pl.BlockSpec((rs, tn), _sample_tile_map(n_pt)),
        ],
        out_shape=[jax.ShapeDtypeStruct((rp, n), BF16), jax.ShapeDtypeStruct((rs, n), BF16)],
        compiler_params=pltpu.CompilerParams(
            dimension_semantics=("arbitrary", "arbitrary"),
            vmem_limit_bytes=_vmem_limit(blocks, temps)),
        name="ffn_up",
    )(hp, hs, wg, wu)


def _out_proj_kernel(ap_ref, as_ref, w_ref, xp_ref, xs_ref, g_ref, op_ref, os_ref, hp_ref, hs_ref, wb_ref, *, n_pt):
    s = pl.program_id(0)

    @pl.when(s == 0)
    def _():
        wb_ref[...] = w_ref[...].astype(BF16)

    @pl.when(s < n_pt)
    def _():
        x = xp_ref[...] + _dot(ap_ref[...], wb_ref[...])
        op_ref[...] = x
        hp_ref[...] = _rmsnorm(x, g_ref[...]).astype(BF16)

    @pl.when(s == n_pt)
    def _():
        x = xs_ref[...] + _dot(as_ref[...], wb_ref[...])
        os_ref[...] = x
        hs_ref[...] = _rmsnorm(x, g_ref[...]).astype(BF16)


def _out_proj(ap, as_, w, xp, xs, g, *, layer):
    rp, k = ap.shape
    rs = as_.shape[0]
    n = w.shape[2]
    tm = rs
    n_pt = rp // tm
    rows_map = lambda s: (jnp.minimum(s, n_pt - 1), 0)
    once = dict(pipeline_mode=pl.Buffered(1))
    blocks = _nbytes((tm, k), BF16) + 2 * _nbytes((tm, n), F32) + _nbytes((tm, n), BF16)
    resident = _nbytes((k, n), F32) + _nbytes((k, n), BF16) + blocks
    return pl.pallas_call(
        functools.partial(_out_proj_kernel, n_pt=n_pt),
        grid=(n_pt + 1,),
        in_specs=[
            pl.BlockSpec((tm, k), rows_map),
            pl.BlockSpec((rs, k), lambda s: (0, 0), **once),
            pl.BlockSpec((None, k, n), lambda s: (0, 0, 0), **once),
            pl.BlockSpec((tm, n), rows_map),
            pl.BlockSpec((rs, n), lambda s: (0, 0), **once),
            pl.BlockSpec((None, 1, n), lambda s: (layer, 0, 0)),
        ],
        out_specs=[
            pl.BlockSpec((tm, n), rows_map),
            pl.BlockSpec((rs, n), lambda s: (0, 0)),
            pl.BlockSpec((tm, n), rows_map),
            pl.BlockSpec((rs, n), lambda s: (0, 0)),
        ],
        out_shape=[jax.ShapeDtypeStruct((rp, n), F32), jax.ShapeDtypeStruct((rs, n), F32),
                   jax.ShapeDtypeStruct((rp, n), BF16), jax.ShapeDtypeStruct((rs, n), BF16)],
        scratch_shapes=[pltpu.VMEM((k, n), BF16)],
        compiler_params=pltpu.CompilerParams(
            dimension_semantics=("arbitrary",),
            vmem_limit_bytes=_vmem_limit(blocks, resident + 2 * _nbytes((tm, n), F32))),
        name="out_proj",
    )(ap, as_, w, xp, xs, g)


def _matmul_kernel(hp_ref, hs_ref, w_ref, zp_ref, zs_ref, *, n_pt):
    zp_ref[...] = _dot(hp_ref[...], w_ref[...].astype(BF16)).astype(zp_ref.dtype)

    @pl.when(pl.program_id(0) == n_pt - 1)
    def _():
        zs_ref[...] = _dot(hs_ref[...], w_ref[...].astype(BF16)).astype(zs_ref.dtype)


def _matmul(hp, hs, w, *, w_layer):
    rp, d = hp.shape
    rs = hs.shape[0]
    n = w.shape[2]
    tm, tn = FFN_ROW_TILE, COL_TILE
    n_pt, n_j = rp // tm, n // tn
    blocks = (_nbytes((tm, d), BF16) + _nbytes((rs, d), BF16) + _nbytes((d, tn), F32)
              + _nbytes((tm, tn), BF16) + _nbytes((rs, tn), BF16))
    temps = _nbytes((d, tn), BF16) + _nbytes((tm, tn), F32)
    return pl.pallas_call(
        functools.partial(_matmul_kernel, n_pt=n_pt),
        grid=(n_pt, n_j),
        in_specs=[
            pl.BlockSpec((tm, d), _prompt_rows_map),
            pl.BlockSpec((rs, d), _const_map),
            pl.BlockSpec((None, d, tn), lambda i, j: (w_layer, 0, j)),
        ],
        out_specs=[
            pl.BlockSpec((tm, tn), _prompt_tile_map),
            pl.BlockSpec((rs, tn), _sample_tile_map(n_pt)),
        ],
        out_shape=[jax.ShapeDtypeStruct((rp, n), BF16), jax.ShapeDtypeStruct((rs, n), BF16)],
        compiler_params=pltpu.CompilerParams(
            dimension_semantics=("arbitrary", "arbitrary"),
            vmem_limit_bytes=_vmem_limit(blocks, temps)),
        name="matmul",
    )(hp, hs, w)


def _ffn_down_kernel(ap_ref, as_ref, w_ref, xp_ref, xs_ref, g_ref, *out_refs, n_pt, n_k, n_xc, final):
    i, k = pl.program_id(0), pl.program_id(1)
    if final:
        op_ref, os_ref = out_refs
    else:
        op_ref, os_ref, hp_ref, hs_ref = out_refs
    xc = xp_ref.shape[1]

    @pl.when(k == 0)
    def _():
        op_ref[...] = _dot(ap_ref[...], w_ref[...].astype(BF16))

    @pl.when(k > 0)
    def _():
        op_ref[...] = op_ref[...] + _dot(ap_ref[...], w_ref[...].astype(BF16))

    for c in range(n_xc):
        @pl.when(k == c)
        def _():
            op_ref[:, c * xc:(c + 1) * xc] = op_ref[:, c * xc:(c + 1) * xc] + xp_ref[...]

    @pl.when(k == n_k - 1)
    def _():
        if final:
            op_ref[...] = _rmsnorm(op_ref[...], g_ref[...])
        else:
            hp_ref[...] = _rmsnorm(op_ref[...], g_ref[...]).astype(BF16)

    @pl.when(i == n_pt - 1)
    def _():
        @pl.when(k == 0)
        def _():
            os_ref[...] = xs_ref[...] + _dot(as_ref[...], w_ref[...].astype(BF16))

        @pl.when(k > 0)
        def _():
            os_ref[...] = os_ref[...] + _dot(as_ref[...], w_ref[...].astype(BF16))

        @pl.when(k == n_k - 1)
        def _():
            if final:
                os_ref[...] = _rmsnorm(os_ref[...], g_ref[...])
            else:
                hs_ref[...] = _rmsnorm(os_ref[...], g_ref[...]).astype(BF16)


def _ffn_down(ap, as_, w, xp, xs, g, *, w_layer, g_layer, final):
    rp, kdim = ap.shape
    rs = as_.shape[0]
    n = w.shape[2]
    tm, tk, xc = PROMPT_TILE, DOWN_K_TILE, DOWN_X_CHUNK
    n_pt, n_k, n_xc = rp // tm, kdim // tk, n // xc
    assert n_xc <= n_k
    rows = lambda i, k: (i, 0)
    blocks = (_nbytes((tm, tk), BF16) + _nbytes((rs, tk), BF16) + _nbytes((tk, n), w.dtype)
              + _nbytes((tm, xc), F32) + _nbytes((rs, n), F32)
              + _nbytes((tm, n), F32) + _nbytes((rs, n), F32))
    out_specs = [pl.BlockSpec((tm, n), rows), pl.BlockSpec((rs, n), _const_map)]
    out_shape = [jax.ShapeDtypeStruct((rp, n), F32), jax.ShapeDtypeStruct((rs, n), F32)]
    if not final:
        blocks += _nbytes((tm, n), BF16) + _nbytes((rs, n), BF16)
        out_specs += [pl.BlockSpec((tm, n), rows), pl.BlockSpec((rs, n), _const_map)]
        out_shape += [jax.ShapeDtypeStruct((rp, n), BF16), jax.ShapeDtypeStruct((rs, n), BF16)]
    return pl.pallas_call(
        functools.partial(_ffn_down_kernel, n_pt=n_pt, n_k=n_k, n_xc=n_xc, final=final),
        grid=(n_pt, n_k),
        in_specs=[
            pl.BlockSpec((tm, tk), lambda i, k: (i, k)),
            pl.BlockSpec((rs, tk), lambda i, k: (0, jnp.where(i == n_pt - 1, k, 0))),
            pl.BlockSpec((None, tk, n), lambda i, k: (w_layer, k, 0)),
            pl.BlockSpec((tm, xc), lambda i, k: (i, jnp.minimum(k, n_xc - 1))),
            pl.BlockSpec((rs, n), _const_map),
            pl.BlockSpec((None, 1, n), lambda i, k: (g_layer, 0, 0)),
        ],
        out_specs=out_specs,
        out_shape=out_shape,
        compiler_params=pltpu.CompilerParams(
            dimension_semantics=("arbitrary", "arbitrary"),
            vmem_limit_bytes=_vmem_limit(blocks, _nbytes((tk, n), BF16))),
        name="ffn_down",
    )(ap, as_, w, xp, xs, g)


def _mem_kv_kernel(x_ref, g_ref, w_ref, k_ref, v_ref):
    h = _rmsnorm(x_ref[...], g_ref[...]).astype(BF16)
    kv = _dot(h, w_ref[...].astype(BF16))
    k_ref[...] = kv[:, :MEM_WIDTH]
    v_ref[...] = kv[:, MEM_WIDTH:]


def _mem_kv(mem, g, w):
    rows, d = mem.shape
    depth = w.shape[0]
    tm = 512
    out = jax.ShapeDtypeStruct((depth, rows, MEM_WIDTH), F32)
    blocks = (_nbytes((tm, d), F32) + _nbytes((d, 2 * MEM_WIDTH), F32) + 2 * _nbytes((tm, MEM_WIDTH), F32))
    return pl.pallas_call(
        _mem_kv_kernel,
        grid=(depth, rows // tm),
        in_specs=[
            pl.BlockSpec((tm, d), lambda l, i: (i, 0)),
            pl.BlockSpec((None, 1, d), lambda l, i: (l, 0, 0)),
            pl.BlockSpec((None, d, 2 * MEM_WIDTH), lambda l, i: (l, 0, 0)),
        ],
        out_specs=[
            pl.BlockSpec((None, tm, MEM_WIDTH), lambda l, i: (l, i, 0)),
            pl.BlockSpec((None, tm, MEM_WIDTH), lambda l, i: (l, i, 0)),
        ],
        out_shape=[out, out],
        compiler_params=pltpu.CompilerParams(
            dimension_semantics=("arbitrary", "arbitrary"),
            vmem_limit_bytes=_vmem_limit(blocks, _nbytes((d, 2 * MEM_WIDTH), BF16))),
        name="mem_kv",
    )(mem, g, w)


def _cross_attention_head(q, k, v):
    s = _dot_nt(q, k) * (MEM_HEAD_DIM ** -0.5)
    p = _softmax_rows(s).astype(BF16)
    return _dot(p, v)


def _conv_prompt_kernel(z_ref, mk_ref, mv_ref, cw_ref, wd_ref, mix_ref, st_ref, wdb_ref, ext_ref, *, tiles_per_seq):
    s = pl.program_id(0)
    tq = z_ref.shape[0]
    wdb_ref[...] = wd_ref[...].astype(BF16)

    @pl.when(s % tiles_per_seq == 0)
    def _():
        ext_ref[0:CARRY_ROWS, :] = jnp.zeros((CARRY_ROWS, TOKEN_WIDTH), F32)

    c = z_ref[:, TOKEN_WIDTH:2 * TOKEN_WIDTH].astype(F32)
    u = z_ref[:, 2 * TOKEN_WIDTH:3 * TOKEN_WIDTH].astype(F32)
    cu = c * u
    ext_ref[CARRY_ROWS:CARRY_ROWS + tq, :] = cu
    conv = (cw_ref[0:1, :] * ext_ref[CARRY_ROWS - 2:CARRY_ROWS - 2 + tq, :]
            + cw_ref[1:2, :] * ext_ref[CARRY_ROWS - 1:CARRY_ROWS - 1 + tq, :]
            + cw_ref[2:3, :] * cu)
    b = z_ref[:, 0:TOKEN_WIDTH].astype(F32)
    mix_ref[:, 0:TOKEN_WIDTH] = (b * conv).astype(BF16)
    st_ref[...] = ext_ref[CARRY_ROWS + tq - 2:CARRY_ROWS + tq, :]
    ext_ref[0:CARRY_ROWS, :] = ext_ref[tq:tq + CARRY_ROWS, :]

    for h in range(MEM_HEADS):
        lo, hi = h * MEM_HEAD_DIM, (h + 1) * MEM_HEAD_DIM
        q = z_ref[:, 3 * TOKEN_WIDTH + lo:3 * TOKEN_WIDTH + hi]
        o = _cross_attention_head(q, mk_ref[:, lo:hi].astype(BF16), mv_ref[:, lo:hi].astype(BF16))
        mix_ref[:, TOKEN_WIDTH + lo:TOKEN_WIDTH + hi] = o.astype(BF16)


def _conv_prompt(z, mk, mv, conv_w, w_down, *, layer, batch, seq):
    rows, zc = z.shape
    tq = CONV_ROWS
    tiles_per_seq = seq // tq
    n_steps = batch * tiles_per_seq
    d_ff, d = w_down.shape[1:]
    slab = d_ff // n_steps
    assert slab * n_steps == d_ff and slab % (2 * V7X_SUBLANES) == 0
    blocks = (_nbytes((tq, zc), BF16) + 2 * _nbytes((N_MEM, MEM_WIDTH), F32) + _nbytes((tq, D_MODEL), BF16)
              + _nbytes((slab, d), F32) + _nbytes((slab, d), BF16))
    scratch = _nbytes((tq + CARRY_ROWS, TOKEN_WIDTH), F32)
    return pl.pallas_call(
        functools.partial(_conv_prompt_kernel, tiles_per_seq=tiles_per_seq),
        grid=(n_steps,),
        in_specs=[
            pl.BlockSpec((tq, zc), lambda s: (s, 0)),
            pl.BlockSpec((None, N_MEM, MEM_WIDTH), lambda s: (layer, s // tiles_per_seq, 0)),
            pl.BlockSpec((None, N_MEM, MEM_WIDTH), lambda s: (layer, s // tiles_per_seq, 0)),
            pl.BlockSpec((None, CONV_WIDTH, TOKEN_WIDTH), lambda s: (0, 0, 0)),
            pl.BlockSpec((None, slab, d), lambda s: (layer, s, 0)),
        ],
        out_specs=[
            pl.BlockSpec((tq, D_MODEL), lambda s: (s, 0)),
            pl.BlockSpec((None, CONV_WIDTH - 1, TOKEN_WIDTH), lambda s: (s // tiles_per_seq, 0, 0)),
            pl.BlockSpec((slab, d), lambda s: (s, 0)),
        ],
        out_shape=[
            jax.ShapeDtypeStruct((rows, D_MODEL), BF16),
            jax.ShapeDtypeStruct((batch, CONV_WIDTH - 1, TOKEN_WIDTH), F32),
            jax.ShapeDtypeStruct((d_ff, d), BF16),
        ],
        scratch_shapes=[pltpu.VMEM((tq + CARRY_ROWS, TOKEN_WIDTH), F32)],
        compiler_params=pltpu.CompilerParams(
            dimension_semantics=("arbitrary",),
            vmem_limit_bytes=_vmem_limit(blocks, scratch + 6 * _nbytes((tq, TOKEN_WIDTH), F32))),
        name="conv_prompt",
    )(z, mk, mv, conv_w, w_down)


def _conv_sample_kernel(z_ref, st_ref, mk_ref, mv_ref, cw_ref, mix_ref, nst_ref, ext_ref, mixf_ref, *, dec_seq):
    t = dec_seq
    for n in range(SAMPLE_GROUP):
        r0, r1 = n * t, (n + 1) * t
        c = z_ref[r0:r1, TOKEN_WIDTH:2 * TOKEN_WIDTH].astype(F32)
        u = z_ref[r0:r1, 2 * TOKEN_WIDTH:3 * TOKEN_WIDTH].astype(F32)
        cu = c * u
        ext_ref[CARRY_ROWS - 2:CARRY_ROWS, :] = st_ref[n]
        ext_ref[CARRY_ROWS:CARRY_ROWS + t, :] = cu
        conv = (cw_ref[0:1, :] * ext_ref[CARRY_ROWS - 2:CARRY_ROWS - 2 + t, :]
                + cw_ref[1:2, :] * ext_ref[CARRY_ROWS - 1:CARRY_ROWS - 1 + t, :]
                + cw_ref[2:3, :] * cu)
        b = z_ref[r0:r1, 0:TOKEN_WIDTH].astype(F32)
        mixf_ref[r0:r1, 0:TOKEN_WIDTH] = b * conv
        nst_ref[n] = ext_ref[CARRY_ROWS + t - 2:CARRY_ROWS + t, :]
        for h in range(MEM_HEADS):
            lo, hi = h * MEM_HEAD_DIM, (h + 1) * MEM_HEAD_DIM
            q = z_ref[r0:r1, 3 * TOKEN_WIDTH + lo:3 * TOKEN_WIDTH + hi]
            o = _cross_attention_head(q, mk_ref[n, :, h, :].astype(BF16), mv_ref[n, :, h, :].astype(BF16))
            mixf_ref[r0:r1, TOKEN_WIDTH + lo:TOKEN_WIDTH + hi] = o
    mix_ref[...] = mixf_ref[...].astype(BF16)


def _conv_sample(z, state, mem_k, mem_v, conv_w, *, layer, dec_seq):
    rows, zc = z.shape
    dec_batch = state.shape[1]
    g = SAMPLE_GROUP
    gr = g * dec_seq
    blocks = (_nbytes((gr, zc), BF16) + 2 * _nbytes((g, CONV_WIDTH - 1, TOKEN_WIDTH), F32)
              + 2 * _nbytes((g, N_MEM, V7X_SUBLANES, MEM_HEAD_DIM), F32) + _nbytes((gr, D_MODEL), BF16))
    scratch = _nbytes((2 * CARRY_ROWS, TOKEN_WIDTH), F32) + _nbytes((gr, D_MODEL), F32)
    return pl.pallas_call(
        functools.partial(_conv_sample_kernel, dec_seq=dec_seq),
        grid=(dec_batch // g,),
        in_specs=[
            pl.BlockSpec((gr, zc), lambda i: (i, 0)),
            pl.BlockSpec((None, g, CONV_WIDTH - 1, TOKEN_WIDTH), lambda i: (0, i, 0, 0)),
            pl.BlockSpec((None, g, N_MEM, MEM_HEADS, MEM_HEAD_DIM), lambda i: (layer, i, 0, 0, 0)),
            pl.BlockSpec((None, g, N_MEM, MEM_HEADS, MEM_HEAD_DIM), lambda i: (layer, i, 0, 0, 0)),
            pl.BlockSpec((None, CONV_WIDTH, TOKEN_WIDTH), lambda i: (0, 0, 0)),
        ],
        out_specs=[
            pl.BlockSpec((gr, D_MODEL), lambda i: (i, 0)),
            pl.BlockSpec((g, CONV_WIDTH - 1, TOKEN_WIDTH), lambda i: (i, 0, 0)),
        ],
        out_shape=[
            jax.ShapeDtypeStruct((rows, D_MODEL), BF16),
            jax.ShapeDtypeStruct((dec_batch, CONV_WIDTH - 1, TOKEN_WIDTH), F32),
        ],
        scratch_shapes=[pltpu.VMEM((2 * CARRY_ROWS, TOKEN_WIDTH), F32), pltpu.VMEM((gr, D_MODEL), F32)],
        compiler_params=pltpu.CompilerParams(
            dimension_semantics=("arbitrary",),
            vmem_limit_bytes=_vmem_limit(blocks, scratch)),
        name="conv_sample",
    )(z, state, mem_k, mem_v, conv_w)


def _band_attention_head(q, k, v, sink, mask):
    s = _dot_nt(q * (HEAD_DIM ** -0.5), k)
    s = jnp.where(mask, s, -jnp.inf)
    m = jnp.maximum(jnp.max(s, axis=-1, keepdims=True), sink)
    e = jnp.exp(s - m)
    p = e / (jnp.sum(e, axis=-1, keepdims=True) + jnp.exp(sink - m))
    return _dot(p.astype(BF16), v)


def _swa_prompt_kernel(sink_ref, zq_ref, zp_ref, mk_ref, mv_ref, wd_ref, mix_ref, wdb_ref, *,
                       blocks_per_seq, n_cast_steps):
    s = pl.program_id(0)

    @pl.when(s < n_cast_steps)
    def _():
        wdb_ref[...] = wd_ref[...].astype(BF16)

    k_off = TOKEN_WIDTH
    v_off = TOKEN_WIDTH + KV_WIDTH
    qm_off = TOKEN_WIDTH + 2 * KV_WIDTH
    has_prev = (s % blocks_per_seq) > 0
    row = lax.broadcasted_iota(jnp.int32, (WINDOW, 2 * WINDOW), 0)
    col = lax.broadcasted_iota(jnp.int32, (WINDOW, 2 * WINDOW), 1)
    mask = jnp.logical_or(jnp.logical_and(jnp.logical_and(col < WINDOW, col > row), has_prev),
                          jnp.logical_and(col >= WINDOW, col - WINDOW <= row))
    for kh in range(N_KV_HEADS):
        lo, hi = kh * HEAD_DIM, (kh + 1) * HEAD_DIM
        k = jnp.concatenate([zp_ref[:, lo:hi], zq_ref[:, k_off + lo:k_off + hi]], axis=0)
        v = jnp.concatenate([zp_ref[:, KV_WIDTH + lo:KV_WIDTH + hi], zq_ref[:, v_off + lo:v_off + hi]], axis=0)
        for g in range(GROUP):
            h = kh * GROUP + g
            q = zq_ref[:, h * HEAD_DIM:(h + 1) * HEAD_DIM]
            o = _band_attention_head(q, k, v, sink_ref[h], mask)
            mix_ref[:, h * HEAD_DIM:(h + 1) * HEAD_DIM] = o.astype(BF16)
    for h in range(MEM_HEADS):
        lo, hi = h * MEM_HEAD_DIM, (h + 1) * MEM_HEAD_DIM
        q = zq_ref[:, qm_off + lo:qm_off + hi]
        o = _cross_attention_head(q, mk_ref[:, lo:hi].astype(BF16), mv_ref[:, lo:hi].astype(BF16))
        mix_ref[:, TOKEN_WIDTH + lo:TOKEN_WIDTH + hi] = o.astype(BF16)


def _swa_prompt(z, mk, mv, sinks, w_down, *, layer, batch, seq):
    rows, zc = z.shape
    tq = WINDOW
    blocks_per_seq = seq // tq
    n_steps = batch * blocks_per_seq
    kv_col_block = TOKEN_WIDTH // (2 * KV_WIDTH)
    d_ff, d = w_down.shape[1:]
    n_cast_steps = n_steps // 2
    slab = d_ff // n_cast_steps
    assert slab * n_cast_steps == d_ff and slab % (2 * V7X_SUBLANES) == 0
    slab_map = lambda s: (jnp.minimum(s, n_cast_steps - 1), 0)
    blocks = (_nbytes((tq, zc), BF16) + _nbytes((tq, 2 * KV_WIDTH), BF16)
              + 2 * _nbytes((N_MEM, MEM_WIDTH), F32) + _nbytes((tq, D_MODEL), BF16)
              + _nbytes((slab, d), F32) + _nbytes((slab, d), BF16))
    return pl.pallas_call(
        functools.partial(_swa_prompt_kernel, blocks_per_seq=blocks_per_seq, n_cast_steps=n_cast_steps),
        grid=(n_steps,),
        in_specs=[
            pl.BlockSpec(memory_space=pltpu.SMEM),
            pl.BlockSpec((tq, zc), lambda s: (s, 0)),
            pl.BlockSpec((tq, 2 * KV_WIDTH), lambda s: (jnp.maximum(s - 1, 0), kv_col_block)),
            pl.BlockSpec((None, N_MEM, MEM_WIDTH), lambda s: (layer, s // blocks_per_seq, 0)),
            pl.BlockSpec((None, N_MEM, MEM_WIDTH), lambda s: (layer, s // blocks_per_seq, 0)),
            pl.BlockSpec((None, slab, d), lambda s: (layer, jnp.minimum(s, n_cast_steps - 1), 0)),
        ],
        out_specs=[
            pl.BlockSpec((tq, D_MODEL), lambda s: (s, 0)),
            pl.BlockSpec((slab, d), slab_map),
        ],
        out_shape=[
            jax.ShapeDtypeStruct((rows, D_MODEL), BF16),
            jax.ShapeDtypeStruct((d_ff, d), BF16),
        ],
        compiler_params=pltpu.CompilerParams(
            dimension_semantics=("arbitrary",),
            vmem_limit_bytes=_vmem_limit(blocks, 0)),
        name="swa_prompt",
    )(sinks, z, z, mk, mv, w_down)


def _swa_sample_kernel(sink_ref, z_ref, ck_ref, cv_ref, mk_ref, mv_ref,
                       mix_ref, nk_ref, nv_ref, knew_ref, vnew_ref, mixf_ref, *, dec_seq):
    t = dec_seq
    k_off = TOKEN_WIDTH
    v_off = TOKEN_WIDTH + KV_WIDTH
    qm_off = TOKEN_WIDTH + 2 * KV_WIDTH
    rows = GROUP * t
    qi = lax.broadcasted_iota(jnp.int32, (rows, 2 * WINDOW), 0) % t
    col = lax.broadcasted_iota(jnp.int32, (rows, 2 * WINDOW), 1)
    mask = jnp.logical_or(jnp.logical_and(col < WINDOW, col > qi),
                          jnp.logical_and(col >= WINDOW, col - WINDOW <= qi))
    knew_ref[...] = jnp.zeros((WINDOW, KV_WIDTH), F32)
    vnew_ref[...] = jnp.zeros((WINDOW, KV_WIDTH), F32)
    for n in range(SAMPLE_GROUP):
        r0, r1 = n * t, (n + 1) * t
        k_new = z_ref[r0:r1, k_off:k_off + KV_WIDTH].astype(F32)
        v_new = z_ref[r0:r1, v_off:v_off + KV_WIDTH].astype(F32)
        knew_ref[0:t, :] = k_new
        vnew_ref[0:t, :] = v_new
        nk_ref[n, 0:WINDOW - t, :] = ck_ref[n, t:WINDOW, :]
        nv_ref[n, 0:WINDOW - t, :] = cv_ref[n, t:WINDOW, :]
        nk_ref[n, WINDOW - t:WINDOW, :] = k_new
        nv_ref[n, WINDOW - t:WINDOW, :] = v_new
        for kh in range(N_KV_HEADS):
            lo, hi = kh * HEAD_DIM, (kh + 1) * HEAD_DIM
            k = jnp.concatenate([ck_ref[n, :, lo:hi], knew_ref[:, lo:hi]], axis=0).astype(BF16)
            v = jnp.concatenate([cv_ref[n, :, lo:hi], vnew_ref[:, lo:hi]], axis=0).astype(BF16)
            q = jnp.concatenate(
                [z_ref[r0:r1, (kh * GROUP + g) * HEAD_DIM:(kh * GROUP + g + 1) * HEAD_DIM].astype(F32)
                 for g in range(GROUP)], axis=0).astype(BF16)
            sink = jnp.concatenate(
                [jnp.full((t, 1), sink_ref[kh * GROUP + g], F32) for g in range(GROUP)], axis=0)
            o = _band_attention_head(q, k, v, sink, mask)
            for g in range(GROUP):
                h = kh * GROUP + g
                mixf_ref[r0:r1, h * HEAD_DIM:(h + 1) * HEAD_DIM] = o[g * t:(g + 1) * t, :]
        for h in range(MEM_HEADS):
            lo, hi = h * MEM_HEAD_DIM, (h + 1) * MEM_HEAD_DIM
            q = z_ref[r0:r1, qm_off + lo:qm_off + hi]
            o = _cross_attention_head(q, mk_ref[n, :, h, :].astype(BF16), mv_ref[n, :, h, :].astype(BF16))
            mixf_ref[r0:r1, TOKEN_WIDTH + lo:TOKEN_WIDTH + hi] = o
    mix_ref[...] = mixf_ref[...].astype(BF16)


def _swa_sample(z, cache_k, cache_v, mem_k, mem_v, sinks, *, layer, dec_seq):
    rows, zc = z.shape
    dec_batch = cache_k.shape[0]
    g = SAMPLE_GROUP
    gr = g * dec_seq
    win = jax.ShapeDtypeStruct((dec_batch, WINDOW, KV_WIDTH), F32)
    blocks = (_nbytes((gr, zc), BF16) + 4 * _nbytes((g, WINDOW, KV_WIDTH), F32)
              + 2 * _nbytes((g, N_MEM, V7X_SUBLANES, MEM_HEAD_DIM), F32) + _nbytes((gr, D_MODEL), BF16))
    scratch = 2 * _nbytes((WINDOW, KV_WIDTH), F32) + _nbytes((gr, D_MODEL), F32)
    return pl.pallas_call(
        functools.partial(_swa_sample_kernel, dec_seq=dec_seq),
        grid=(dec_batch // g,),
        in_specs=[
            pl.BlockSpec(memory_space=pltpu.SMEM),
            pl.BlockSpec((gr, zc), lambda i: (i, 0)),
            pl.BlockSpec((g, WINDOW, KV_WIDTH), lambda i: (i, 0, 0)),
            pl.BlockSpec((g, WINDOW, KV_WIDTH), lambda i: (i, 0, 0)),
            pl.BlockSpec((None, g, N_MEM, MEM_HEADS, MEM_HEAD_DIM), lambda i: (layer, i, 0, 0, 0)),
            pl.BlockSpec((None, g, N_MEM, MEM_HEADS, MEM_HEAD_DIM), lambda i: (layer, i, 0, 0, 0)),
        ],
        out_specs=[
            pl.BlockSpec((gr, D_MODEL), lambda i: (i, 0)),
            pl.BlockSpec((g, WINDOW, KV_WIDTH), lambda i: (i, 0, 0)),
            pl.BlockSpec((g, WINDOW, KV_WIDTH), lambda i: (i, 0, 0)),
        ],
        out_shape=[jax.ShapeDtypeStruct((rows, D_MODEL), BF16), win, win],
        scratch_shapes=[pltpu.VMEM((WINDOW, KV_WIDTH), F32), pltpu.VMEM((WINDOW, KV_WIDTH), F32),
                        pltpu.VMEM((gr, D_MODEL), F32)],
        compiler_params=pltpu.CompilerParams(
            dimension_semantics=("arbitrary",),
            vmem_limit_bytes=_vmem_limit(blocks, scratch)),
        name="swa_sample",
    )(sinks, z, cache_k, cache_v, mem_k, mem_v)


def kernel(x_prompt, x_sample, mem_prompt, state_conv, cache_win_k, cache_win_v, cache_mem_k, cache_mem_v,
           norm_mix, norm_mem, w_mem_kv, norm_ffn, w_gate, w_up, w_down,
           conv_w_in, conv_w, conv_w_out, attn_w_in, attn_sinks, attn_w_out, norm_final):
    batch, seq, d = x_prompt.shape
    dec_batch, dec_seq, _ = x_sample.shape
    depth = norm_mix.shape[0]
    d_ff = w_gate.shape[2]
    prompt_rows = batch * seq
    sample_rows = dec_batch * dec_seq
    assert d == D_MODEL and depth == 2 and seq % CONV_ROWS == 0 and seq % WINDOW == 0
    assert prompt_rows % PROMPT_TILE == 0 and dec_batch % SAMPLE_GROUP == 0
    assert dec_seq == V7X_SUBLANES and d_ff % FFN_COL_TILE == 0 and d % COL_TILE == 0
    assert d_ff % DOWN_K_TILE == 0 and d % DOWN_X_CHUNK == 0
    assert prompt_rows % FFN_ROW_TILE == 0 and prompt_rows % sample_rows == 0

    xp = x_prompt.reshape(prompt_rows, d)
    xs = x_sample.reshape(sample_rows, d)
    mem = mem_prompt.reshape(batch * N_MEM, d)
    mem_k_s, mem_v_s = cache_mem_k, cache_mem_v
    g_mix = norm_mix.reshape(depth, 1, d)
    g_ffn = norm_ffn.reshape(depth, 1, d)

    mk, mv = _mem_kv(mem, norm_mem.reshape(depth, 1, d), w_mem_kv)

    zp, zs = _norm_matmul(xp, xs, g_mix, conv_w_in, layer=0, w_layer=0)
    mix_s, conv_s = _conv_sample(zs, state_conv, mem_k_s, mem_v_s, conv_w, layer=0, dec_seq=dec_seq)
    mix_p, conv_p, wd_bf = _conv_prompt(zp, mk, mv, conv_w, w_down, layer=0, batch=batch, seq=seq)
    xp, xs, hp, hs = _out_proj(mix_p, mix_s, conv_w_out, xp, xs, g_ffn, layer=0)
    ap, as_ = _ffn_up(hp, hs, w_gate, w_up, layer=0)
    xp, xs, hp, hs = _ffn_down(ap, as_, wd_bf[None], xp, xs, g_mix, w_layer=0, g_layer=1, final=False)

    zp, zs = _matmul(hp, hs, attn_w_in, w_layer=0)
    sinks = attn_sinks[0]
    mix_s, win_k_s, win_v_s = _swa_sample(
        zs, cache_win_k[0].reshape(dec_batch, WINDOW, KV_WIDTH), cache_win_v[0].reshape(dec_batch, WINDOW, KV_WIDTH),
        mem_k_s, mem_v_s, sinks, layer=1, dec_seq=dec_seq)
    mix_p, wd_bf = _swa_prompt(zp, mk, mv, sinks, w_down, layer=1, batch=batch, seq=seq)
    xp, xs, hp, hs = _out_proj(mix_p, mix_s, attn_w_out, xp, xs, g_ffn, layer=1)
    ap, as_ = _ffn_up(hp, hs, w_gate, w_up, layer=1)
    y_prompt, y_sample = _ffn_down(ap, as_, wd_bf[None], xp, xs, norm_final.reshape(1, 1, d),
                                   w_layer=0, g_layer=0, final=True)

    win_p = zp.reshape(batch, seq, -1)[:, seq - WINDOW:, TOKEN_WIDTH:TOKEN_WIDTH + 2 * KV_WIDTH].astype(F32)
    kv_shape = (1, -1, WINDOW, N_KV_HEADS, HEAD_DIM)
    mem_shape = (depth, batch, N_MEM, MEM_HEADS, MEM_HEAD_DIM)
    return (y_prompt.reshape(batch, seq, d),
            y_sample.reshape(dec_batch, dec_seq, d),
            conv_p[None],
            conv_s[None],
            win_p[..., :KV_WIDTH].reshape(kv_shape),
            win_p[..., KV_WIDTH:].reshape(kv_shape),
            win_k_s.reshape(kv_shape),
            win_v_s.reshape(kv_shape),
            mk.reshape(mem_shape),
            mv.reshape(mem_shape))
```

```python
import functools

import jax
import jax.numpy as jnp
from jax import lax
from jax.experimental import pallas as pl
from jax.experimental.pallas import tpu as pltpu

F32 = jnp.float32
BF16 = jnp.bfloat16

D_MODEL = 2048
N_MEM = 256
MEM_HEADS = 4
MEM_WIDTH = D_MODEL // 4
MEM_HEAD_DIM = MEM_WIDTH // MEM_HEADS
TOKEN_WIDTH = D_MODEL - MEM_WIDTH
CONV_WIDTH = 3
WINDOW = 128
HEAD_DIM = 64
N_HEADS = TOKEN_WIDTH // HEAD_DIM
N_KV_HEADS = 4
GROUP = N_HEADS // N_KV_HEADS
KV_WIDTH = N_KV_HEADS * HEAD_DIM
EPS = 1e-6

V7X_VMEM_BYTES = 64 * 1024 * 1024
V7X_SUBLANES = 8

PROMPT_TILE = 1024
COL_TILE = 512
DOWN_K_TILE = 1408
DOWN_X_CHUNK = 512
FFN_ROW_TILE = 2048
FFN_COL_TILE = 512
OUT_ROWS = 512
CONV_ROWS = 512
SAMPLE_GROUP = 8
CARRY_ROWS = V7X_SUBLANES


def _nbytes(shape, dtype):
    n = 1
    for s in shape:
        n *= s
    return n * jnp.dtype(dtype).itemsize


def _vmem_limit(block_bytes, scratch_bytes):
    need = 2 * block_bytes + scratch_bytes
    return int(min(need + max(need // 4, 8 << 20), V7X_VMEM_BYTES - (6 << 20)))


def _rmsnorm(x, g):
    r = lax.rsqrt(jnp.mean(x * x, axis=-1, keepdims=True) + EPS)
    return (x * r) * g


def _softmax_rows(s):
    m = jnp.max(s, axis=-1, keepdims=True)
    e = jnp.exp(s - m)
    return e / jnp.sum(e, axis=-1, keepdims=True)


def _dot(a, b):
    return jnp.dot(a, b, preferred_element_type=F32)


def _dot_nt(a, b):
    return lax.dot_general(a, b, (((1,), (1,)), ((), ())), preferred_element_type=F32)


def _prompt_rows_map(i, j):
    return (i, 0)


def _prompt_tile_map(i, j):
    return (i, j)


def _sample_tile_map(n_pt):
    return lambda i, j: (0, jnp.where(i == n_pt - 1, j, 0))


def _const_map(i, j):
    return (0, 0)


def _norm_rows_kernel(xp_ref, xs_ref, g_ref, hp_ref, hs_ref, *, n_pt):
    hp_ref[...] = _rmsnorm(xp_ref[...], g_ref[...]).astype(BF16)

    @pl.when(pl.program_id(0) == n_pt - 1)
    def _():
        hs_ref[...] = _rmsnorm(xs_ref[...], g_ref[...]).astype(BF16)


def _norm_rows(xp, xs, g, *, layer):
    rp, d = xp.shape
    rs = xs.shape[0]
    tm = OUT_ROWS
    n_pt = rp // tm
    blocks = _nbytes((tm, d), F32) + _nbytes((tm, d), BF16) + _nbytes((rs, d), F32) + _nbytes((rs, d), BF16)
    return pl.pallas_call(
        functools.partial(_norm_rows_kernel, n_pt=n_pt),
        grid=(n_pt,),
        in_specs=[
            pl.BlockSpec((tm, d), lambda s: (s, 0)),
            pl.BlockSpec((rs, d), lambda s: (0, 0)),
            pl.BlockSpec((None, 1, d), lambda s: (layer, 0, 0)),
        ],
        out_specs=[
            pl.BlockSpec((tm, d), lambda s: (s, 0)),
            pl.BlockSpec((rs, d), lambda s: (0, 0)),
        ],
        out_shape=[jax.ShapeDtypeStruct((rp, d), BF16), jax.ShapeDtypeStruct((rs, d), BF16)],
        compiler_params=pltpu.CompilerParams(
            dimension_semantics=("arbitrary",),
            vmem_limit_bytes=_vmem_limit(blocks, 0)),
        name="norm_rows",
    )(xp, xs, g)


def _swiglu(h, wg, wu):
    gate = _dot(h, wg)
    up = _dot(h, wu)
    return (gate * jax.nn.sigmoid(gate) * up).astype(BF16)


def _ffn_up_kernel(hp_ref, hs_ref, wg_ref, wu_ref, ap_ref, as_ref, *, n_pt):
    ap_ref[...] = _swiglu(hp_ref[...], wg_ref[...].astype(BF16), wu_ref[...].astype(BF16))

    @pl.when(pl.program_id(0) == n_pt - 1)
    def _():
        as_ref[...] = _swiglu(hs_ref[...], wg_ref[...].astype(BF16), wu_ref[...].astype(BF16))


def _ffn_up(hp, hs, wg, wu, *, layer):
    rp, d = hp.shape
    rs = hs.shape[0]
    n = wg.shape[2]
    tm, tn = FFN_ROW_TILE, FFN_COL_TILE
    n_pt, n_j = rp // tm, n // tn
    blocks = (_nbytes((tm, d), BF16) + _nbytes((rs, d), BF16) + 2 * _nbytes((d, tn), F32)
              + _nbytes((tm, tn), BF16) + _nbytes((rs, tn), BF16))
    temps = 2 * _nbytes((d, tn), BF16) + 3 * _nbytes((tm, tn), F32)
    w_spec = pl.BlockSpec((None, d, tn), lambda i, j: (layer, 0, j))
    return pl.pallas_call(
        functools.partial(_ffn_up_kernel, n_pt=n_pt),
        grid=(n_pt, n_j),
        in_specs=[
            pl.BlockSpec((tm, d), _prompt_rows_map),
            pl.BlockSpec((rs, d), _const_map),
            w_spec,
            w_spec,
        ],
        out_specs=[
            pl.BlockSpec((tm, tn), _prompt_tile_map),
            pl.BlockSpec((rs, tn), _sample_tile_map(n_pt)),
        ],
        out_shape=[jax.ShapeDtypeStruct((rp, n), BF16), jax.ShapeDtypeStruct((rs, n), BF16)],
        compiler_params=pltpu.CompilerParams(
            dimension_semantics=("arbitrary", "arbitrary"),
            vmem_limit_bytes=_vmem_limit(blocks, temps)),
        name="ffn_up",
    )(hp, hs, wg, wu)


def _out_proj_kernel(ap_ref, as_ref, w_ref, xp_ref, xs_ref, g_ref, op_ref, os_ref, hp_ref, hs_ref, *, n_pt):
    x = xp_ref[...] + _dot(ap_ref[...], w_ref[...])
    op_ref[...] = x
    hp_ref[...] = _rmsnorm(x, g_ref[...]).astype(BF16)

    @pl.when(pl.program_id(0) == n_pt - 1)
    def _():
        x = xs_ref[...] + _dot(as_ref[...], w_ref[...])
        os_ref[...] = x
        hs_ref[...] = _rmsnorm(x, g_ref[...]).astype(BF16)


def _out_proj(ap, as_, w, xp, xs, g, *, layer):
    rp, k = ap.shape
    rs = as_.shape[0]
    n = w.shape[1]
    tm = OUT_ROWS
    n_pt = rp // tm
    rows_map = lambda s: (s, 0)
    const = lambda s: (0, 0)
    once = dict(pipeline_mode=pl.Buffered(1))
    blocks = _nbytes((tm, k), BF16) + 2 * _nbytes((tm, n), F32) + _nbytes((tm, n), BF16)
    resident = (_nbytes((k, n), BF16) + _nbytes((rs, k), BF16) + _nbytes((rs, n), F32)
                + 2 * (_nbytes((rs, n), F32) + _nbytes((rs, n), BF16)))
    return pl.pallas_call(
        functools.partial(_out_proj_kernel, n_pt=n_pt),
        grid=(n_pt,),
        in_specs=[
            pl.BlockSpec((tm, k), rows_map),
            pl.BlockSpec((rs, k), const, **once),
            pl.BlockSpec((k, n), const, **once),
            pl.BlockSpec((tm, n), rows_map),
            pl.BlockSpec((rs, n), const, **once),
            pl.BlockSpec((None, 1, n), lambda s: (layer, 0, 0)),
        ],
        out_specs=[
            pl.BlockSpec((tm, n), rows_map),
            pl.BlockSpec((rs, n), const),
            pl.BlockSpec((tm, n), rows_map),
            pl.BlockSpec((rs, n), const),
        ],
        out_shape=[jax.ShapeDtypeStruct((rp, n), F32), jax.ShapeDtypeStruct((rs, n), F32),
                   jax.ShapeDtypeStruct((rp, n), BF16), jax.ShapeDtypeStruct((rs, n), BF16)],
        compiler_params=pltpu.CompilerParams(
            dimension_semantics=("arbitrary",),
            vmem_limit_bytes=_vmem_limit(blocks, resident + 2 * _nbytes((tm, n), F32))),
        name="out_proj",
    )(ap, as_, w, xp, xs, g)


def _matmul_kernel(hp_ref, hs_ref, w_ref, zp_ref, zs_ref, *, n_pt):
    zp_ref[...] = _dot(hp_ref[...], w_ref[...].astype(BF16)).astype(zp_ref.dtype)

    @pl.when(pl.program_id(0) == n_pt - 1)
    def _():
        zs_ref[...] = _dot(hs_ref[...], w_ref[...].astype(BF16)).astype(zs_ref.dtype)


def _matmul(hp, hs, w, *, w_layer):
    rp, d = hp.shape
    rs = hs.shape[0]
    n = w.shape[2]
    tm, tn = FFN_ROW_TILE, COL_TILE
    n_pt, n_j = rp // tm, n // tn
    blocks = (_nbytes((tm, d), BF16) + _nbytes((rs, d), BF16) + _nbytes((d, tn), F32)
              + _nbytes((tm, tn), BF16) + _nbytes((rs, tn), BF16))
    temps = _nbytes((d, tn), BF16) + _nbytes((tm, tn), F32)
    return pl.pallas_call(
        functools.partial(_matmul_kernel, n_pt=n_pt),
        grid=(n_pt, n_j),
        in_specs=[
            pl.BlockSpec((tm, d), _prompt_rows_map),
            pl.BlockSpec((rs, d), _const_map),
            pl.BlockSpec((None, d, tn), lambda i, j: (w_layer, 0, j)),
        ],
        out_specs=[
            pl.BlockSpec((tm, tn), _prompt_tile_map),
            pl.BlockSpec((rs, tn), _sample_tile_map(n_pt)),
        ],
        out_shape=[jax.ShapeDtypeStruct((rp, n), BF16), jax.ShapeDtypeStruct((rs, n), BF16)],
        compiler_params=pltpu.CompilerParams(
            dimension_semantics=("arbitrary", "arbitrary"),
            vmem_limit_bytes=_vmem_limit(blocks, temps)),
        name="matmul",
    )(hp, hs, w)


def _ffn_down_kernel(ap_ref, as_ref, w_ref, xp_ref, xs_ref, g_ref, *out_refs, n_pt, n_k, n_xc, final):
    i, k = pl.program_id(0), pl.program_id(1)
    if final:
        op_ref, os_ref = out_refs
    else:
        op_ref, os_ref, hp_ref, hs_ref = out_refs
    xc = xp_ref.shape[1]

    @pl.when(k == 0)
    def _():
        op_ref[...] = _dot(ap_ref[...], w_ref[...].astype(BF16))

    @pl.when(k > 0)
    def _():
        op_ref[...] = op_ref[...] + _dot(ap_ref[...], w_ref[...].astype(BF16))

    for c in range(n_xc):
        @pl.when(k == c)
        def _():
            op_ref[:, c * xc:(c + 1) * xc] = op_ref[:, c * xc:(c + 1) * xc] + xp_ref[...]

    @pl.when(k == n_k - 1)
    def _():
        if final:
            op_ref[...] = _rmsnorm(op_ref[...], g_ref[...])
        else:
            hp_ref[...] = _rmsnorm(op_ref[...], g_ref[...]).astype(BF16)

    @pl.when(i == n_pt - 1)
    def _():
        @pl.when(k == 0)
        def _():
            os_ref[...] = xs_ref[...] + _dot(as_ref[...], w_ref[...].astype(BF16))

        @pl.when(k > 0)
        def _():
            os_ref[...] = os_ref[...] + _dot(as_ref[...], w_ref[...].astype(BF16))

        @pl.when(k == n_k - 1)
        def _():
            if final:
                os_ref[...] = _rmsnorm(os_ref[...], g_ref[...])
            else:
                hs_ref[...] = _rmsnorm(os_ref[...], g_ref[...]).astype(BF16)


def _ffn_down(ap, as_, w, xp, xs, g, *, w_layer, g_layer, final):
    rp, kdim = ap.shape
    rs = as_.shape[0]
    n = w.shape[2]
    tm, tk, xc = PROMPT_TILE, DOWN_K_TILE, DOWN_X_CHUNK
    n_pt, n_k, n_xc = rp // tm, kdim // tk, n // xc
    assert n_xc <= n_k
    rows = lambda i, k: (i, 0)
    blocks = (_nbytes((tm, tk), BF16) + _nbytes((rs, tk), BF16) + _nbytes((tk, n), w.dtype)
              + _nbytes((tm, xc), F32) + _nbytes((rs, n), F32)
              + _nbytes((tm, n), F32) + _nbytes((rs, n), F32))
    out_specs = [pl.BlockSpec((tm, n), rows), pl.BlockSpec((rs, n), _const_map)]
    out_shape = [jax.ShapeDtypeStruct((rp, n), F32), jax.ShapeDtypeStruct((rs, n), F32)]
    if not final:
        blocks += _nbytes((tm, n), BF16) + _nbytes((rs, n), BF16)
        out_specs += [pl.BlockSpec((tm, n), rows), pl.BlockSpec((rs, n), _const_map)]
        out_shape += [jax.ShapeDtypeStruct((rp, n), BF16), jax.ShapeDtypeStruct((rs, n), BF16)]
    return pl.pallas_call(
        functools.partial(_ffn_down_kernel, n_pt=n_pt, n_k=n_k, n_xc=n_xc, final=final),
        grid=(n_pt, n_k),
        in_specs=[
            pl.BlockSpec((tm, tk), lambda i, k: (i, k)),
            pl.BlockSpec((rs, tk), lambda i, k: (0, jnp.where(i == n_pt - 1, k, 0))),
            pl.BlockSpec((None, tk, n), lambda i, k: (w_layer, k, 0)),
            pl.BlockSpec((tm, xc), lambda i, k: (i, jnp.minimum(k, n_xc - 1))),
            pl.BlockSpec((rs, n), _const_map),
            pl.BlockSpec((None, 1, n), lambda i, k: (g_layer, 0, 0)),
        ],
        out_specs=out_specs,
        out_shape=out_shape,
        compiler_params=pltpu.CompilerParams(
            dimension_semantics=("arbitrary", "arbitrary"),
            vmem_limit_bytes=_vmem_limit(blocks, _nbytes((tk, n), BF16))),
        name="ffn_down",
    )(ap, as_, w, xp, xs, g)


def _mem_kv_kernel(x_ref, g_ref, w_ref, k_ref, v_ref):
    h = _rmsnorm(x_ref[...], g_ref[...]).astype(BF16)
    kv = _dot(h, w_ref[...].astype(BF16))
    k_ref[...] = kv[:, :MEM_WIDTH]
    v_ref[...] = kv[:, MEM_WIDTH:]


def _mem_kv(mem, g, w):
    rows, d = mem.shape
    depth = w.shape[0]
    tm = 512
    out = jax.ShapeDtypeStruct((depth, rows, MEM_WIDTH), F32)
    blocks = (_nbytes((tm, d), F32) + _nbytes((d, 2 * MEM_WIDTH), F32) + 2 * _nbytes((tm, MEM_WIDTH), F32))
    return pl.pallas_call(
        _mem_kv_kernel,
        grid=(depth, rows // tm),
        in_specs=[
            pl.BlockSpec((tm, d), lambda l, i: (i, 0)),
            pl.BlockSpec((None, 1, d), lambda l, i: (l, 0, 0)),
            pl.BlockSpec((None, d, 2 * MEM_WIDTH), lambda l, i: (l, 0, 0)),
        ],
        out_specs=[
            pl.BlockSpec((None, tm, MEM_WIDTH), lambda l, i: (l, i, 0)),
            pl.BlockSpec((None, tm, MEM_WIDTH), lambda l, i: (l, i, 0)),
        ],
        out_shape=[out, out],
        compiler_params=pltpu.CompilerParams(
            dimension_semantics=("arbitrary", "arbitrary"),
            vmem_limit_bytes=_vmem_limit(blocks, _nbytes((d, 2 * MEM_WIDTH), BF16))),
        name="mem_kv",
    )(mem, g, w)


def _cross_attention_head(q, k, v):
    s = _dot_nt(q, k) * (MEM_HEAD_DIM ** -0.5)
    p = _softmax_rows(s).astype(BF16)
    return _dot(p, v)


def _conv_prompt_kernel(z_ref, mk_ref, mv_ref, cw_ref, wd_ref, mix_ref, st_ref, wdb_ref, ext_ref, *, tiles_per_seq):
    s = pl.program_id(0)
    tq = z_ref.shape[0]
    wdb_ref[...] = wd_ref[...].astype(BF16)

    @pl.when(s % tiles_per_seq == 0)
    def _():
        ext_ref[0:CARRY_ROWS, :] = jnp.zeros((CARRY_ROWS, TOKEN_WIDTH), F32)

    c = z_ref[:, TOKEN_WIDTH:2 * TOKEN_WIDTH].astype(F32)
    u = z_ref[:, 2 * TOKEN_WIDTH:3 * TOKEN_WIDTH].astype(F32)
    cu = c * u
    ext_ref[CARRY_ROWS:CARRY_ROWS + tq, :] = cu
    conv = (cw_ref[0:1, :] * ext_ref[CARRY_ROWS - 2:CARRY_ROWS - 2 + tq, :]
            + cw_ref[1:2, :] * ext_ref[CARRY_ROWS - 1:CARRY_ROWS - 1 + tq, :]
            + cw_ref[2:3, :] * cu)
    b = z_ref[:, 0:TOKEN_WIDTH].astype(F32)
    mix_ref[:, 0:TOKEN_WIDTH] = (b * conv).astype(BF16)
    st_ref[...] = ext_ref[CARRY_ROWS + tq - 2:CARRY_ROWS + tq, :]
    ext_ref[0:CARRY_ROWS, :] = ext_ref[tq:tq + CARRY_ROWS, :]

    for h in range(MEM_HEADS):
        lo, hi = h * MEM_HEAD_DIM, (h + 1) * MEM_HEAD_DIM
        q = z_ref[:, 3 * TOKEN_WIDTH + lo:3 * TOKEN_WIDTH + hi]
        o = _cross_attention_head(q, mk_ref[:, lo:hi].astype(BF16), mv_ref[:, lo:hi].astype(BF16))
        mix_ref[:, TOKEN_WIDTH + lo:TOKEN_WIDTH + hi] = o.astype(BF16)


def _conv_prompt(z, mk, mv, conv_w, w_down, *, layer, batch, seq):
    rows, zc = z.shape
    tq = CONV_ROWS
    tiles_per_seq = seq // tq
    n_steps = batch * tiles_per_seq
    d_ff, d = w_down.shape[1:]
    slab = d_ff // n_steps
    assert slab * n_steps == d_ff and slab % (2 * V7X_SUBLANES) == 0
    blocks = (_nbytes((tq, zc), BF16) + 2 * _nbytes((N_MEM, MEM_WIDTH), F32) + _nbytes((tq, D_MODEL), BF16)
              + _nbytes((slab, d), F32) + _nbytes((slab, d), BF16))
    scratch = _nbytes((tq + CARRY_ROWS, TOKEN_WIDTH), F32)
    return pl.pallas_call(
        functools.partial(_conv_prompt_kernel, tiles_per_seq=tiles_per_seq),
        grid=(n_steps,),
        in_specs=[
            pl.BlockSpec((tq, zc), lambda s: (s, 0)),
            pl.BlockSpec((None, N_MEM, MEM_WIDTH), lambda s: (layer, s // tiles_per_seq, 0)),
            pl.BlockSpec((None, N_MEM, MEM_WIDTH), lambda s: (layer, s // tiles_per_seq, 0)),
            pl.BlockSpec((None, CONV_WIDTH, TOKEN_WIDTH), lambda s: (0, 0, 0)),
            pl.BlockSpec((None, slab, d), lambda s: (layer, s, 0)),
        ],
        out_specs=[
            pl.BlockSpec((tq, D_MODEL), lambda s: (s, 0)),
            pl.BlockSpec((None, CONV_WIDTH - 1, TOKEN_WIDTH), lambda s: (s // tiles_per_seq, 0, 0)),
            pl.BlockSpec((slab, d), lambda s: (s, 0)),
        ],
        out_shape=[
            jax.ShapeDtypeStruct((rows, D_MODEL), BF16),
            jax.ShapeDtypeStruct((batch, CONV_WIDTH - 1, TOKEN_WIDTH), F32),
            jax.ShapeDtypeStruct((d_ff, d), BF16),
        ],
        scratch_shapes=[pltpu.VMEM((tq + CARRY_ROWS, TOKEN_WIDTH), F32)],
        compiler_params=pltpu.CompilerParams(
            dimension_semantics=("arbitrary",),
            vmem_limit_bytes=_vmem_limit(blocks, scratch + 6 * _nbytes((tq, TOKEN_WIDTH), F32))),
        name="conv_prompt",
    )(z, mk, mv, conv_w, w_down)


def _conv_sample_kernel(z_ref, st_ref, mk_ref, mv_ref, cw_ref, wo_ref, mix_ref, nst_ref, wob_ref,
                        ext_ref, mixf_ref, *, dec_seq):
    t = dec_seq
    wob_ref[...] = wo_ref[...].astype(BF16)
    for n in range(SAMPLE_GROUP):
        r0, r1 = n * t, (n + 1) * t
        c = z_ref[r0:r1, TOKEN_WIDTH:2 * TOKEN_WIDTH].astype(F32)
        u = z_ref[r0:r1, 2 * TOKEN_WIDTH:3 * TOKEN_WIDTH].astype(F32)
        cu = c * u
        ext_ref[CARRY_ROWS - 2:CARRY_ROWS, :] = st_ref[n]
        ext_ref[CARRY_ROWS:CARRY_ROWS + t, :] = cu
        conv = (cw_ref[0:1, :] * ext_ref[CARRY_ROWS - 2:CARRY_ROWS - 2 + t, :]
                + cw_ref[1:2, :] * ext_ref[CARRY_ROWS - 1:CARRY_ROWS - 1 + t, :]
                + cw_ref[2:3, :] * cu)
        b = z_ref[r0:r1, 0:TOKEN_WIDTH].astype(F32)
        mixf_ref[r0:r1, 0:TOKEN_WIDTH] = b * conv
        nst_ref[n] = ext_ref[CARRY_ROWS + t - 2:CARRY_ROWS + t, :]
        for h in range(MEM_HEADS):
            lo, hi = h * MEM_HEAD_DIM, (h + 1) * MEM_HEAD_DIM
            q = z_ref[r0:r1, 3 * TOKEN_WIDTH + lo:3 * TOKEN_WIDTH + hi]
            o = _cross_attention_head(q, mk_ref[n, :, h, :].astype(BF16), mv_ref[n, :, h, :].astype(BF16))
            mixf_ref[r0:r1, TOKEN_WIDTH + lo:TOKEN_WIDTH + hi] = o
    mix_ref[...] = mixf_ref[...].astype(BF16)


def _conv_sample(z, state, mem_k, mem_v, conv_w, w_out, *, layer, dec_seq):
    rows, zc = z.shape
    dec_batch = state.shape[1]
    g = SAMPLE_GROUP
    gr = g * dec_seq
    n_steps = dec_batch // g
    wk, wn = w_out.shape[1:]
    slab = wk // n_steps
    assert slab * n_steps == wk and slab % (2 * V7X_SUBLANES) == 0
    blocks = (_nbytes((gr, zc), BF16) + 2 * _nbytes((g, CONV_WIDTH - 1, TOKEN_WIDTH), F32)
              + 2 * _nbytes((g, N_MEM, V7X_SUBLANES, MEM_HEAD_DIM), F32) + _nbytes((gr, D_MODEL), BF16)
              + _nbytes((slab, wn), F32) + _nbytes((slab, wn), BF16))
    scratch = _nbytes((2 * CARRY_ROWS, TOKEN_WIDTH), F32) + _nbytes((gr, D_MODEL), F32)
    return pl.pallas_call(
        functools.partial(_conv_sample_kernel, dec_seq=dec_seq),
        grid=(n_steps,),
        in_specs=[
            pl.BlockSpec((gr, zc), lambda i: (i, 0)),
            pl.BlockSpec((None, g, CONV_WIDTH - 1, TOKEN_WIDTH), lambda i: (0, i, 0, 0)),
            pl.BlockSpec((None, g, N_MEM, MEM_HEADS, MEM_HEAD_DIM), lambda i: (layer, i, 0, 0, 0)),
            pl.BlockSpec((None, g, N_MEM, MEM_HEADS, MEM_HEAD_DIM), lambda i: (layer, i, 0, 0, 0)),
            pl.BlockSpec((None, CONV_WIDTH, TOKEN_WIDTH), lambda i: (0, 0, 0)),
            pl.BlockSpec((None, slab, wn), lambda i: (0, i, 0)),
        ],
        out_specs=[
            pl.BlockSpec((gr, D_MODEL), lambda i: (i, 0)),
            pl.BlockSpec((g, CONV_WIDTH - 1, TOKEN_WIDTH), lambda i: (i, 0, 0)),
            pl.BlockSpec((slab, wn), lambda i: (i, 0)),
        ],
        out_shape=[
            jax.ShapeDtypeStruct((rows, D_MODEL), BF16),
            jax.ShapeDtypeStruct((dec_batch, CONV_WIDTH - 1, TOKEN_WIDTH), F32),
            jax.ShapeDtypeStruct((wk, wn), BF16),
        ],
        scratch_shapes=[pltpu.VMEM((2 * CARRY_ROWS, TOKEN_WIDTH), F32), pltpu.VMEM((gr, D_MODEL), F32)],
        compiler_params=pltpu.CompilerParams(
            dimension_semantics=("arbitrary",),
            vmem_limit_bytes=_vmem_limit(blocks, scratch)),
        name="conv_sample",
    )(z, state, mem_k, mem_v, conv_w, w_out)


def _band_attention_head(q, k, v, sink, mask):
    s = _dot_nt(q * (HEAD_DIM ** -0.5), k)
    s = jnp.where(mask, s, -jnp.inf)
    m = jnp.maximum(jnp.max(s, axis=-1, keepdims=True), sink)
    e = jnp.exp(s - m)
    p = e / (jnp.sum(e, axis=-1, keepdims=True) + jnp.exp(sink - m))
    return _dot(p.astype(BF16), v)


def _swa_prompt_kernel(sink_ref, zq_ref, zp_ref, mk_ref, mv_ref, wd_ref, mix_ref, wdb_ref, *,
                       blocks_per_seq, n_cast_steps):
    s = pl.program_id(0)

    @pl.when(s < n_cast_steps)
    def _():
        wdb_ref[...] = wd_ref[...].astype(BF16)

    k_off = TOKEN_WIDTH
    v_off = TOKEN_WIDTH + KV_WIDTH
    qm_off = TOKEN_WIDTH + 2 * KV_WIDTH
    has_prev = (s % blocks_per_seq) > 0
    row = lax.broadcasted_iota(jnp.int32, (WINDOW, 2 * WINDOW), 0)
    col = lax.broadcasted_iota(jnp.int32, (WINDOW, 2 * WINDOW), 1)
    mask = jnp.logical_or(jnp.logical_and(jnp.logical_and(col < WINDOW, col > row), has_prev),
                          jnp.logical_and(col >= WINDOW, col - WINDOW <= row))
    for kh in range(N_KV_HEADS):
        lo, hi = kh * HEAD_DIM, (kh + 1) * HEAD_DIM
        k = jnp.concatenate([zp_ref[:, lo:hi], zq_ref[:, k_off + lo:k_off + hi]], axis=0)
        v = jnp.concatenate([zp_ref[:, KV_WIDTH + lo:KV_WIDTH + hi], zq_ref[:, v_off + lo:v_off + hi]], axis=0)
        for g in range(GROUP):
            h = kh * GROUP + g
            q = zq_ref[:, h * HEAD_DIM:(h + 1) * HEAD_DIM]
            o = _band_attention_head(q, k, v, sink_ref[h], mask)
            mix_ref[:, h * HEAD_DIM:(h + 1) * HEAD_DIM] = o.astype(BF16)
    for h in range(MEM_HEADS):
        lo, hi = h * MEM_HEAD_DIM, (h + 1) * MEM_HEAD_DIM
        q = zq_ref[:, qm_off + lo:qm_off + hi]
        o = _cross_attention_head(q, mk_ref[:, lo:hi].astype(BF16), mv_ref[:, lo:hi].astype(BF16))
        mix_ref[:, TOKEN_WIDTH + lo:TOKEN_WIDTH + hi] = o.astype(BF16)


def _swa_prompt(z, mk, mv, sinks, w_down, *, layer, batch, seq):
    rows, zc = z.shape
    tq = WINDOW
    blocks_per_seq = seq // tq
    n_steps = batch * blocks_per_seq
    kv_col_block = TOKEN_WIDTH // (2 * KV_WIDTH)
    d_ff, d = w_down.shape[1:]
    n_cast_steps = n_steps // 2
    slab = d_ff // n_cast_steps
    assert slab * n_cast_steps == d_ff and slab % (2 * V7X_SUBLANES) == 0
    slab_map = lambda s: (jnp.minimum(s, n_cast_steps - 1), 0)
    blocks = (_nbytes((tq, zc), BF16) + _nbytes((tq, 2 * KV_WIDTH), BF16)
              + 2 * _nbytes((N_MEM, MEM_WIDTH), F32) + _nbytes((tq, D_MODEL), BF16)
              + _nbytes((slab, d), F32) + _nbytes((slab, d), BF16))
    return pl.pallas_call(
        functools.partial(_swa_prompt_kernel, blocks_per_seq=blocks_per_seq, n_cast_steps=n_cast_steps),
        grid=(n_steps,),
        in_specs=[
            pl.BlockSpec(memory_space=pltpu.SMEM),
            pl.BlockSpec((tq, zc), lambda s: (s, 0)),
            pl.BlockSpec((tq, 2 * KV_WIDTH), lambda s: (jnp.maximum(s - 1, 0), kv_col_block)),
            pl.BlockSpec((None, N_MEM, MEM_WIDTH), lambda s: (layer, s // blocks_per_seq, 0)),
            pl.BlockSpec((None, N_MEM, MEM_WIDTH), lambda s: (layer, s // blocks_per_seq, 0)),
            pl.BlockSpec((None, slab, d), lambda s: (layer, jnp.minimum(s, n_cast_steps - 1), 0)),
        ],
        out_specs=[
            pl.BlockSpec((tq, D_MODEL), lambda s: (s, 0)),
            pl.BlockSpec((slab, d), slab_map),
        ],
        out_shape=[
            jax.ShapeDtypeStruct((rows, D_MODEL), BF16),
            jax.ShapeDtypeStruct((d_ff, d), BF16),
        ],
        compiler_params=pltpu.CompilerParams(
            dimension_semantics=("arbitrary",),
            vmem_limit_bytes=_vmem_limit(blocks, 0)),
        name="swa_prompt",
    )(sinks, z, z, mk, mv, w_down)


def _swa_sample_kernel(sink_ref, z_ref, ck_ref, cv_ref, mk_ref, mv_ref, wo_ref,
                       mix_ref, nk_ref, nv_ref, wob_ref, knew_ref, vnew_ref, mixf_ref, *, dec_seq):
    t = dec_seq
    wob_ref[...] = wo_ref[...].astype(BF16)
    k_off = TOKEN_WIDTH
    v_off = TOKEN_WIDTH + KV_WIDTH
    qm_off = TOKEN_WIDTH + 2 * KV_WIDTH
    rows = GROUP * t
    qi = lax.broadcasted_iota(jnp.int32, (rows, 2 * WINDOW), 0) % t
    col = lax.broadcasted_iota(jnp.int32, (rows, 2 * WINDOW), 1)
    mask = jnp.logical_or(jnp.logical_and(col < WINDOW, col > qi),
                          jnp.logical_and(col >= WINDOW, col - WINDOW <= qi))
    knew_ref[...] = jnp.zeros((WINDOW, KV_WIDTH), F32)
    vnew_ref[...] = jnp.zeros((WINDOW, KV_WIDTH), F32)
    for n in range(SAMPLE_GROUP):
        r0, r1 = n * t, (n + 1) * t
        k_new = z_ref[r0:r1, k_off:k_off + KV_WIDTH].astype(F32)
        v_new = z_ref[r0:r1, v_off:v_off + KV_WIDTH].astype(F32)
        knew_ref[0:t, :] = k_new
        vnew_ref[0:t, :] = v_new
        nk_ref[n, 0:WINDOW - t, :] = ck_ref[n, t:WINDOW, :]
        nv_ref[n, 0:WINDOW - t, :] = cv_ref[n, t:WINDOW, :]
        nk_ref[n, WINDOW - t:WINDOW, :] = k_new
        nv_ref[n, WINDOW - t:WINDOW, :] = v_new
        for kh in range(N_KV_HEADS):
            lo, hi = kh * HEAD_DIM, (kh + 1) * HEAD_DIM
            k = jnp.concatenate([ck_ref[n, :, lo:hi], knew_ref[:, lo:hi]], axis=0).astype(BF16)
            v = jnp.concatenate([cv_ref[n, :, lo:hi], vnew_ref[:, lo:hi]], axis=0).astype(BF16)
            q = jnp.concatenate(
                [z_ref[r0:r1, (kh * GROUP + g) * HEAD_DIM:(kh * GROUP + g + 1) * HEAD_DIM].astype(F32)
                 for g in range(GROUP)], axis=0).astype(BF16)
            sink = jnp.concatenate(
                [jnp.full((t, 1), sink_ref[kh * GROUP + g], F32) for g in range(GROUP)], axis=0)
            o = _band_attention_head(q, k, v, sink, mask)
            for g in range(GROUP):
                h = kh * GROUP + g
                mixf_ref[r0:r1, h * HEAD_DIM:(h + 1) * HEAD_DIM] = o[g * t:(g + 1) * t, :]
        for h in range(MEM_HEADS):
            lo, hi = h * MEM_HEAD_DIM, (h + 1) * MEM_HEAD_DIM
            q = z_ref[r0:r1, qm_off + lo:qm_off + hi]
            o = _cross_attention_head(q, mk_ref[n, :, h, :].astype(BF16), mv_ref[n, :, h, :].astype(BF16))
            mixf_ref[r0:r1, TOKEN_WIDTH + lo:TOKEN_WIDTH + hi] = o
    mix_ref[...] = mixf_ref[...].astype(BF16)


def _swa_sample(z, cache_k, cache_v, mem_k, mem_v, sinks, w_out, *, layer, dec_seq):
    rows, zc = z.shape
    dec_batch = cache_k.shape[0]
    g = SAMPLE_GROUP
    gr = g * dec_seq
    n_steps = dec_batch // g
    wk, wn = w_out.shape[1:]
    slab = wk // n_steps
    assert slab * n_steps == wk and slab % (2 * V7X_SUBLANES) == 0
    win = jax.ShapeDtypeStruct((dec_batch, WINDOW, KV_WIDTH), F32)
    blocks = (_nbytes((gr, zc), BF16) + 4 * _nbytes((g, WINDOW, KV_WIDTH), F32)
              + 2 * _nbytes((g, N_MEM, V7X_SUBLANES, MEM_HEAD_DIM), F32) + _nbytes((gr, D_MODEL), BF16)
              + _nbytes((slab, wn), F32) + _nbytes((slab, wn), BF16))
    scratch = 2 * _nbytes((WINDOW, KV_WIDTH), F32) + _nbytes((gr, D_MODEL), F32)
    return pl.pallas_call(
        functools.partial(_swa_sample_kernel, dec_seq=dec_seq),
        grid=(n_steps,),
        in_specs=[
            pl.BlockSpec(memory_space=pltpu.SMEM),
            pl.BlockSpec((gr, zc), lambda i: (i, 0)),
            pl.BlockSpec((g, WINDOW, KV_WIDTH), lambda i: (i, 0, 0)),
            pl.BlockSpec((g, WINDOW, KV_WIDTH), lambda i: (i, 0, 0)),
            pl.BlockSpec((None, g, N_MEM, MEM_HEADS, MEM_HEAD_DIM), lambda i: (layer, i, 0, 0, 0)),
            pl.BlockSpec((None, g, N_MEM, MEM_HEADS, MEM_HEAD_DIM), lambda i: (layer, i, 0, 0, 0)),
            pl.BlockSpec((None, slab, wn), lambda i: (0, i, 0)),
        ],
        out_specs=[
            pl.BlockSpec((gr, D_MODEL), lambda i: (i, 0)),
            pl.BlockSpec((g, WINDOW, KV_WIDTH), lambda i: (i, 0, 0)),
            pl.BlockSpec((g, WINDOW, KV_WIDTH), lambda i: (i, 0, 0)),
            pl.BlockSpec((slab, wn), lambda i: (i, 0)),
        ],
        out_shape=[jax.ShapeDtypeStruct((rows, D_MODEL), BF16), win, win,
                   jax.ShapeDtypeStruct((wk, wn), BF16)],
        scratch_shapes=[pltpu.VMEM((WINDOW, KV_WIDTH), F32), pltpu.VMEM((WINDOW, KV_WIDTH), F32),
                        pltpu.VMEM((gr, D_MODEL), F32)],
        compiler_params=pltpu.CompilerParams(
            dimension_semantics=("arbitrary",),
            vmem_limit_bytes=_vmem_limit(blocks, scratch)),
        name="swa_sample",
    )(sinks, z, cache_k, cache_v, mem_k, mem_v, w_out)


def kernel(x_prompt, x_sample, mem_prompt, state_conv, cache_win_k, cache_win_v, cache_mem_k, cache_mem_v,
           norm_mix, norm_mem, w_mem_kv, norm_ffn, w_gate, w_up, w_down,
           conv_w_in, conv_w, conv_w_out, attn_w_in, attn_sinks, attn_w_out, norm_final):
    batch, seq, d = x_prompt.shape
    dec_batch, dec_seq, _ = x_sample.shape
    depth = norm_mix.shape[0]
    d_ff = w_gate.shape[2]
    prompt_rows = batch * seq
    sample_rows = dec_batch * dec_seq
    assert d == D_MODEL and depth == 2 and seq % CONV_ROWS == 0 and seq % WINDOW == 0
    assert prompt_rows % PROMPT_TILE == 0 and dec_batch % SAMPLE_GROUP == 0
    assert dec_seq == V7X_SUBLANES and d_ff % FFN_COL_TILE == 0 and d % COL_TILE == 0
    assert d_ff % DOWN_K_TILE == 0 and d % DOWN_X_CHUNK == 0
    assert prompt_rows % FFN_ROW_TILE == 0 and prompt_rows % sample_rows == 0

    xp = x_prompt.reshape(prompt_rows, d)
    xs = x_sample.reshape(sample_rows, d)
    mem = mem_prompt.reshape(batch * N_MEM, d)
    mem_k_s, mem_v_s = cache_mem_k, cache_mem_v
    g_mix = norm_mix.reshape(depth, 1, d)
    g_ffn = norm_ffn.reshape(depth, 1, d)

    mk, mv = _mem_kv(mem, norm_mem.reshape(depth, 1, d), w_mem_kv)

    hp, hs = _norm_rows(xp, xs, g_mix, layer=0)
    zp, zs = _matmul(hp, hs, conv_w_in, w_layer=0)
    mix_s, conv_s, wo_bf = _conv_sample(zs, state_conv, mem_k_s, mem_v_s, conv_w, conv_w_out,
                                        layer=0, dec_seq=dec_seq)
    mix_p, conv_p, wd_bf = _conv_prompt(zp, mk, mv, conv_w, w_down, layer=0, batch=batch, seq=seq)
    xp, xs, hp, hs = _out_proj(mix_p, mix_s, wo_bf, xp, xs, g_ffn, layer=0)
    ap, as_ = _ffn_up(hp, hs, w_gate, w_up, layer=0)
    xp, xs, hp, hs = _ffn_down(ap, as_, wd_bf[None], xp, xs, g_mix, w_layer=0, g_layer=1, final=False)

    zp, zs = _matmul(hp, hs, attn_w_in, w_layer=0)
    sinks = attn_sinks[0]
    mix_s, win_k_s, win_v_s, wo_bf = _swa_sample(
        zs, cache_win_k[0].reshape(dec_batch, WINDOW, KV_WIDTH), cache_win_v[0].reshape(dec_batch, WINDOW, KV_WIDTH),
        mem_k_s, mem_v_s, sinks, attn_w_out, layer=1, dec_seq=dec_seq)
    mix_p, wd_bf = _swa_prompt(zp, mk, mv, sinks, w_down, layer=1, batch=batch, seq=seq)
    xp, xs, hp, hs = _out_proj(mix_p, mix_s, wo_bf, xp, xs, g_ffn, layer=1)
    ap, as_ = _ffn_up(hp, hs, w_gate, w_up, layer=1)
    y_prompt, y_sample = _ffn_down(ap, as_, wd_bf[None], xp, xs, norm_final.reshape(1, 1, d),
                                   w_layer=0, g_layer=0, final=True)

    win_p = zp.reshape(batch, seq, -1)[:, seq - WINDOW:, TOKEN_WIDTH:TOKEN_WIDTH + 2 * KV_WIDTH].astype(F32)
    kv_shape = (1, -1, WINDOW, N_KV_HEADS, HEAD_DIM)
    mem_shape = (depth, batch, N_MEM, MEM_HEADS, MEM_HEAD_DIM)
    return (y_prompt.reshape(batch, seq, d),
            y_sample.reshape(dec_batch, dec_seq, d),
            conv_p[None],
            conv_s[None],
            win_p[..., :KV_WIDTH].reshape(kv_shape),
            win_p[..., KV_WIDTH:].reshape(kv_shape),
            win_k_s.reshape(kv_shape),
            win_v_s.reshape(kv_shape),
            mk.reshape(mem_shape),
            mv.reshape(mem_shape))
```

```python
import functools

import jax
import jax.numpy as jnp
from jax import lax
from jax.experimental import pallas as pl
from jax.experimental.pallas import tpu as pltpu

F32 = jnp.float32
BF16 = jnp.bfloat16

D_MODEL = 2048
N_MEM = 256
MEM_HEADS = 4
MEM_WIDTH = D_MODEL // 4
MEM_HEAD_DIM = MEM_WIDTH // MEM_HEADS
TOKEN_WIDTH = D_MODEL - MEM_WIDTH
CONV_WIDTH = 3
WINDOW = 128
HEAD_DIM = 64
N_HEADS = TOKEN_WIDTH // HEAD_DIM
N_KV_HEADS = 4
GROUP = N_HEADS // N_KV_HEADS
KV_WIDTH = N_KV_HEADS * HEAD_DIM
EPS = 1e-6

V7X_VMEM_BYTES = 64 * 1024 * 1024
V7X_SUBLANES = 8
V7X_MXU_DEPTH = 256

PROMPT_TILE = 1024
COL_TILE = 512
DOWN_K_TILE = 1408
DOWN_X_CHUNK = 512
FFN_ROW_TILE = 2048
FFN_COL_TILE = 512
MXU_ROWS = 1024
OUT_ROWS = 512
CONV_ROWS = 512
SAMPLE_GROUP = 8
CARRY_ROWS = V7X_SUBLANES


def _nbytes(shape, dtype):
    n = 1
    for s in shape:
        n *= s
    return n * jnp.dtype(dtype).itemsize


def _vmem_limit(block_bytes, scratch_bytes):
    need = 2 * block_bytes + scratch_bytes
    return int(min(need + max(need // 4, 8 << 20), V7X_VMEM_BYTES - (6 << 20)))


def _rmsnorm(x, g):
    r = lax.rsqrt(jnp.mean(x * x, axis=-1, keepdims=True) + EPS)
    return (x * r) * g


def _softmax_rows(s):
    m = jnp.max(s, axis=-1, keepdims=True)
    e = jnp.exp(s - m)
    return e / jnp.sum(e, axis=-1, keepdims=True)


def _dot(a, b):
    return jnp.dot(a, b, preferred_element_type=F32)


def _dot_nt(a, b):
    return lax.dot_general(a, b, (((1,), (1,)), ((), ())), preferred_element_type=F32)


def _prompt_rows_map(i, j):
    return (i, 0)


def _prompt_tile_map(i, j):
    return (i, j)


def _sample_tile_map(n_pt):
    return lambda i, j: (0, jnp.where(i == n_pt - 1, j, 0))


def _const_map(i, j):
    return (0, 0)


def _norm_rows_kernel(xp_ref, xs_ref, g_ref, hp_ref, hs_ref, *, n_pt):
    hp_ref[...] = _rmsnorm(xp_ref[...], g_ref[...]).astype(BF16)

    @pl.when(pl.program_id(0) == n_pt - 1)
    def _():
        hs_ref[...] = _rmsnorm(xs_ref[...], g_ref[...]).astype(BF16)


def _norm_rows(xp, xs, g, *, layer):
    rp, d = xp.shape
    rs = xs.shape[0]
    tm = OUT_ROWS
    n_pt = rp // tm
    blocks = _nbytes((tm, d), F32) + _nbytes((tm, d), BF16) + _nbytes((rs, d), F32) + _nbytes((rs, d), BF16)
    return pl.pallas_call(
        functools.partial(_norm_rows_kernel, n_pt=n_pt),
        grid=(n_pt,),
        in_specs=[
            pl.BlockSpec((tm, d), lambda s: (s, 0)),
            pl.BlockSpec((rs, d), lambda s: (0, 0)),
            pl.BlockSpec((None, 1, d), lambda s: (layer, 0, 0)),
        ],
        out_specs=[
            pl.BlockSpec((tm, d), lambda s: (s, 0)),
            pl.BlockSpec((rs, d), lambda s: (0, 0)),
        ],
        out_shape=[jax.ShapeDtypeStruct((rp, d), BF16), jax.ShapeDtypeStruct((rs, d), BF16)],
        compiler_params=pltpu.CompilerParams(
            dimension_semantics=("arbitrary",),
            vmem_limit_bytes=_vmem_limit(blocks, 0)),
        name="norm_rows",
    )(xp, xs, g)


def _swiglu(h, wg, wu):
    gate = _dot(h, wg)
    up = _dot(h, wu)
    return (gate * jax.nn.sigmoid(gate) * up).astype(BF16)


def _ffn_up_kernel(hp_ref, hs_ref, wg_ref, wu_ref, ap_ref, as_ref, *, n_pt):
    wg, wu = wg_ref[...].astype(BF16), wu_ref[...].astype(BF16)
    for r in range(0, hp_ref.shape[0], MXU_ROWS):
        ap_ref[r:r + MXU_ROWS, :] = _swiglu(hp_ref[r:r + MXU_ROWS, :], wg, wu)

    @pl.when(pl.program_id(0) == n_pt - 1)
    def _():
        as_ref[...] = _swiglu(hs_ref[...], wg_ref[...].astype(BF16), wu_ref[...].astype(BF16))


def _ffn_up(hp, hs, wg, wu, *, layer):
    rp, d = hp.shape
    rs = hs.shape[0]
    n = wg.shape[2]
    tm, tn = FFN_ROW_TILE, FFN_COL_TILE
    n_pt, n_j = rp // tm, n // tn
    blocks = (_nbytes((tm, d), BF16) + _nbytes((rs, d), BF16) + 2 * _nbytes((d, tn), wg.dtype)
              + _nbytes((tm, tn), BF16) + _nbytes((rs, tn), BF16))
    temps = 2 * _nbytes((d, tn), BF16) + 3 * _nbytes((tm, tn), F32)
    w_spec = pl.BlockSpec((None, d, tn), lambda i, j: (layer, 0, j))
    return pl.pallas_call(
        functools.partial(_ffn_up_kernel, n_pt=n_pt),
        grid=(n_pt, n_j),
        in_specs=[
            pl.BlockSpec((tm, d), _prompt_rows_map),
            pl.BlockSpec((rs, d), _const_map),
            w_spec,
            w_spec,
        ],
        out_specs=[
            pl.BlockSpec((tm, tn), _prompt_tile_map),
            pl.BlockSpec((rs, tn), _sample_tile_map(n_pt)),
        ],
        out_shape=[jax.ShapeDtypeStruct((rp, n), BF16), jax.ShapeDtypeStruct((rs, n), BF16)],
        compiler_params=pltpu.CompilerParams(
            dimension_semantics=("arbitrary", "arbitrary"),
            vmem_limit_bytes=_vmem_limit(blocks, temps)),
        name="ffn_up",
    )(hp, hs, wg, wu)


def _out_proj_kernel(ap_ref, as_ref, w_ref, xp_ref, xs_ref, g_ref, op_ref, os_ref, hp_ref, hs_ref, *, n_pt):
    x = xp_ref[...] + _dot(ap_ref[...], w_ref[...])
    op_ref[...] = x
    hp_ref[...] = _rmsnorm(x, g_ref[...]).astype(BF16)

    @pl.when(pl.program_id(0) == n_pt - 1)
    def _():
        x = xs_ref[...] + _dot(as_ref[...], w_ref[...])
        os_ref[...] = x
        hs_ref[...] = _rmsnorm(x, g_ref[...]).astype(BF16)


def _out_proj(ap, as_, w, xp, xs, g, *, layer):
    rp, k = ap.shape
    rs = as_.shape[0]
    n = w.shape[1]
    tm = OUT_ROWS
    n_pt = rp // tm
    rows_map = lambda s: (s, 0)
    const = lambda s: (0, 0)
    once = dict(pipeline_mode=pl.Buffered(1))
    blocks = _nbytes((tm, k), BF16) + 2 * _nbytes((tm, n), F32) + _nbytes((tm, n), BF16)
    resident = (_nbytes((k, n), BF16) + _nbytes((rs, k), BF16) + _nbytes((rs, n), F32)
                + 2 * (_nbytes((rs, n), F32) + _nbytes((rs, n), BF16)))
    return pl.pallas_call(
        functools.partial(_out_proj_kernel, n_pt=n_pt),
        grid=(n_pt,),
        in_specs=[
            pl.BlockSpec((tm, k), rows_map),
            pl.BlockSpec((rs, k), const, **once),
            pl.BlockSpec((k, n), const, **once),
            pl.BlockSpec((tm, n), rows_map),
            pl.BlockSpec((rs, n), const, **once),
            pl.BlockSpec((None, 1, n), lambda s: (layer, 0, 0)),
        ],
        out_specs=[
            pl.BlockSpec((tm, n), rows_map),
            pl.BlockSpec((rs, n), const),
            pl.BlockSpec((tm, n), rows_map),
            pl.BlockSpec((rs, n), const),
        ],
        out_shape=[jax.ShapeDtypeStruct((rp, n), F32), jax.ShapeDtypeStruct((rs, n), F32),
                   jax.ShapeDtypeStruct((rp, n), BF16), jax.ShapeDtypeStruct((rs, n), BF16)],
        compiler_params=pltpu.CompilerParams(
            dimension_semantics=("arbitrary",),
            vmem_limit_bytes=_vmem_limit(blocks, resident + 2 * _nbytes((tm, n), F32))),
        name="out_proj",
    )(ap, as_, w, xp, xs, g)


def _matmul_kernel(hp_ref, hs_ref, w_ref, zp_ref, zs_ref, *, n_pt):
    w = w_ref[...].astype(BF16)
    for r in range(0, hp_ref.shape[0], MXU_ROWS):
        zp_ref[r:r + MXU_ROWS, :] = _dot(hp_ref[r:r + MXU_ROWS, :], w).astype(zp_ref.dtype)

    @pl.when(pl.program_id(0) == n_pt - 1)
    def _():
        zs_ref[...] = _dot(hs_ref[...], w_ref[...].astype(BF16)).astype(zs_ref.dtype)


def _matmul(hp, hs, w, *, w_layer):
    rp, d = hp.shape
    rs = hs.shape[0]
    n = w.shape[2]
    tm, tn = FFN_ROW_TILE, COL_TILE
    n_pt, n_j = rp // tm, n // tn
    blocks = (_nbytes((tm, d), BF16) + _nbytes((rs, d), BF16) + _nbytes((d, tn), F32)
              + _nbytes((tm, tn), BF16) + _nbytes((rs, tn), BF16))
    temps = _nbytes((d, tn), BF16) + _nbytes((tm, tn), F32)
    return pl.pallas_call(
        functools.partial(_matmul_kernel, n_pt=n_pt),
        grid=(n_pt, n_j),
        in_specs=[
            pl.BlockSpec((tm, d), _prompt_rows_map),
            pl.BlockSpec((rs, d), _const_map),
            pl.BlockSpec((None, d, tn), lambda i, j: (w_layer, 0, j)),
        ],
        out_specs=[
            pl.BlockSpec((tm, tn), _prompt_tile_map),
            pl.BlockSpec((rs, tn), _sample_tile_map(n_pt)),
        ],
        out_shape=[jax.ShapeDtypeStruct((rp, n), BF16), jax.ShapeDtypeStruct((rs, n), BF16)],
        compiler_params=pltpu.CompilerParams(
            dimension_semantics=("arbitrary", "arbitrary"),
            vmem_limit_bytes=_vmem_limit(blocks, temps)),
        name="matmul",
    )(hp, hs, w)


def _ffn_down_kernel(ap_ref, as_ref, w_ref, xp_ref, xs_ref, g_ref, *refs, n_pt, n_k, n_xc, final):
    i, k = pl.program_id(0), pl.program_id(1)
    if final:
        op_ref, os_ref, apl_ref, asl_ref, wl_ref = refs
    else:
        op_ref, os_ref, hp_ref, hs_ref, apl_ref, asl_ref, wl_ref = refs
    xc = xp_ref.shape[1]
    kf = (ap_ref.shape[1] // V7X_MXU_DEPTH) * V7X_MXU_DEPTH
    even = k % 2 == 0

    def whole_passes(a_ref):
        return _dot(a_ref[:, :kf], w_ref[:kf, :])

    def with_stash(a_ref, al_ref):
        return _dot(jnp.concatenate([al_ref[...], a_ref[...]], axis=1),
                    jnp.concatenate([wl_ref[...], w_ref[...]], axis=0))

    @pl.when(k == 0)
    def _():
        op_ref[...] = whole_passes(ap_ref)

    @pl.when(jnp.logical_and(k > 0, even))
    def _():
        op_ref[...] = op_ref[...] + whole_passes(ap_ref)

    @pl.when(jnp.logical_not(even))
    def _():
        op_ref[...] = op_ref[...] + with_stash(ap_ref, apl_ref)

    for c in range(n_xc):
        @pl.when(k == c)
        def _():
            op_ref[:, c * xc:(c + 1) * xc] = op_ref[:, c * xc:(c + 1) * xc] + xp_ref[...]

    @pl.when(k == n_k - 1)
    def _():
        if final:
            op_ref[...] = _rmsnorm(op_ref[...], g_ref[...])
        else:
            hp_ref[...] = _rmsnorm(op_ref[...], g_ref[...]).astype(BF16)

    @pl.when(i == n_pt - 1)
    def _():
        @pl.when(k == 0)
        def _():
            os_ref[...] = xs_ref[...] + whole_passes(as_ref)

        @pl.when(jnp.logical_and(k > 0, even))
        def _():
            os_ref[...] = os_ref[...] + whole_passes(as_ref)

        @pl.when(jnp.logical_not(even))
        def _():
            os_ref[...] = os_ref[...] + with_stash(as_ref, asl_ref)

        @pl.when(even)
        def _():
            asl_ref[...] = as_ref[:, kf:]

        @pl.when(k == n_k - 1)
        def _():
            if final:
                os_ref[...] = _rmsnorm(os_ref[...], g_ref[...])
            else:
                hs_ref[...] = _rmsnorm(os_ref[...], g_ref[...]).astype(BF16)

    @pl.when(even)
    def _():
        apl_ref[...] = ap_ref[:, kf:]
        wl_ref[...] = w_ref[kf:, :]


def _ffn_down(ap, as_, w, xp, xs, g, *, w_layer, g_layer, final):
    rp, kdim = ap.shape
    rs = as_.shape[0]
    n = w.shape[2]
    tm, tk, xc = PROMPT_TILE, DOWN_K_TILE, DOWN_X_CHUNK
    n_pt, n_k, n_xc = rp // tm, kdim // tk, n // xc
    k_left = tk % V7X_MXU_DEPTH
    assert n_xc <= n_k and n_k % 2 == 0 and 2 * k_left == V7X_MXU_DEPTH and w.dtype == BF16
    rows = lambda i, k: (i, 0)
    blocks = (_nbytes((tm, tk), BF16) + _nbytes((rs, tk), BF16) + _nbytes((tk, n), w.dtype)
              + _nbytes((tm, xc), F32) + _nbytes((rs, n), F32)
              + _nbytes((tm, n), F32) + _nbytes((rs, n), F32))
    out_specs = [pl.BlockSpec((tm, n), rows), pl.BlockSpec((rs, n), _const_map)]
    out_shape = [jax.ShapeDtypeStruct((rp, n), F32), jax.ShapeDtypeStruct((rs, n), F32)]
    if not final:
        blocks += _nbytes((tm, n), BF16) + _nbytes((rs, n), BF16)
        out_specs += [pl.BlockSpec((tm, n), rows), pl.BlockSpec((rs, n), _const_map)]
        out_shape += [jax.ShapeDtypeStruct((rp, n), BF16), jax.ShapeDtypeStruct((rs, n), BF16)]
    return pl.pallas_call(
        functools.partial(_ffn_down_kernel, n_pt=n_pt, n_k=n_k, n_xc=n_xc, final=final),
        grid=(n_pt, n_k),
        in_specs=[
            pl.BlockSpec((tm, tk), lambda i, k: (i, k)),
            pl.BlockSpec((rs, tk), lambda i, k: (0, jnp.where(i == n_pt - 1, k, 0))),
            pl.BlockSpec((None, tk, n), lambda i, k: (w_layer, k, 0)),
            pl.BlockSpec((tm, xc), lambda i, k: (i, jnp.minimum(k, n_xc - 1))),
            pl.BlockSpec((rs, n), _const_map),
            pl.BlockSpec((None, 1, n), lambda i, k: (g_layer, 0, 0)),
        ],
        out_specs=out_specs,
        out_shape=out_shape,
        scratch_shapes=[pltpu.VMEM((tm, k_left), BF16), pltpu.VMEM((rs, k_left), BF16),
                        pltpu.VMEM((k_left, n), BF16)],
        compiler_params=pltpu.CompilerParams(
            dimension_semantics=("arbitrary", "arbitrary"),
            vmem_limit_bytes=_vmem_limit(blocks, _nbytes((tk, n), BF16))),
        name="ffn_down",
    )(ap, as_, w, xp, xs, g)


def _mem_kv_kernel(x_ref, g_ref, w_ref, k_ref, v_ref):
    h = _rmsnorm(x_ref[...], g_ref[...]).astype(BF16)
    kv = _dot(h, w_ref[...].astype(BF16))
    k_ref[...] = kv[:, :MEM_WIDTH]
    v_ref[...] = kv[:, MEM_WIDTH:]


def _mem_kv(mem, g, w):
    rows, d = mem.shape
    depth = w.shape[0]
    tm = 512
    out = jax.ShapeDtypeStruct((depth, rows, MEM_WIDTH), F32)
    blocks = (_nbytes((tm, d), F32) + _nbytes((d, 2 * MEM_WIDTH), F32) + 2 * _nbytes((tm, MEM_WIDTH), F32))
    return pl.pallas_call(
        _mem_kv_kernel,
        grid=(depth, rows // tm),
        in_specs=[
            pl.BlockSpec((tm, d), lambda l, i: (i, 0)),
            pl.BlockSpec((None, 1, d), lambda l, i: (l, 0, 0)),
            pl.BlockSpec((None, d, 2 * MEM_WIDTH), lambda l, i: (l, 0, 0)),
        ],
        out_specs=[
            pl.BlockSpec((None, tm, MEM_WIDTH), lambda l, i: (l, i, 0)),
            pl.BlockSpec((None, tm, MEM_WIDTH), lambda l, i: (l, i, 0)),
        ],
        out_shape=[out, out],
        compiler_params=pltpu.CompilerParams(
            dimension_semantics=("arbitrary", "arbitrary"),
            vmem_limit_bytes=_vmem_limit(blocks, _nbytes((d, 2 * MEM_WIDTH), BF16))),
        name="mem_kv",
    )(mem, g, w)


def _cross_attention_head(q, k, v):
    s = _dot_nt(q, k) * (MEM_HEAD_DIM ** -0.5)
    p = _softmax_rows(s).astype(BF16)
    return _dot(p, v)


def _conv_prompt_kernel(z_ref, mk_ref, mv_ref, cw_ref, wd_ref, mix_ref, st_ref, wdb_ref, ext_ref, *, tiles_per_seq):
    s = pl.program_id(0)
    tq = z_ref.shape[0]
    wdb_ref[...] = wd_ref[...].astype(BF16)

    @pl.when(s % tiles_per_seq == 0)
    def _():
        ext_ref[0:CARRY_ROWS, :] = jnp.zeros((CARRY_ROWS, TOKEN_WIDTH), F32)

    c = z_ref[:, TOKEN_WIDTH:2 * TOKEN_WIDTH].astype(F32)
    u = z_ref[:, 2 * TOKEN_WIDTH:3 * TOKEN_WIDTH].astype(F32)
    cu = c * u
    ext_ref[CARRY_ROWS:CARRY_ROWS + tq, :] = cu
    conv = (cw_ref[0:1, :] * ext_ref[CARRY_ROWS - 2:CARRY_ROWS - 2 + tq, :]
            + cw_ref[1:2, :] * ext_ref[CARRY_ROWS - 1:CARRY_ROWS - 1 + tq, :]
            + cw_ref[2:3, :] * cu)
    b = z_ref[:, 0:TOKEN_WIDTH].astype(F32)
    mix_ref[:, 0:TOKEN_WIDTH] = (b * conv).astype(BF16)
    st_ref[...] = ext_ref[CARRY_ROWS + tq - 2:CARRY_ROWS + tq, :]
    ext_ref[0:CARRY_ROWS, :] = ext_ref[tq:tq + CARRY_ROWS, :]

    for h in range(MEM_HEADS):
        lo, hi = h * MEM_HEAD_DIM, (h + 1) * MEM_HEAD_DIM
        q = z_ref[:, 3 * TOKEN_WIDTH + lo:3 * TOKEN_WIDTH + hi]
        o = _cross_attention_head(q, mk_ref[:, lo:hi].astype(BF16), mv_ref[:, lo:hi].astype(BF16))
        mix_ref[:, TOKEN_WIDTH + lo:TOKEN_WIDTH + hi] = o.astype(BF16)


def _conv_prompt(z, mk, mv, conv_w, w_down, *, layer, batch, seq):
    rows, zc = z.shape
    tq = CONV_ROWS
    tiles_per_seq = seq // tq
    n_steps = batch * tiles_per_seq
    d_ff, d = w_down.shape[1:]
    slab = d_ff // n_steps
    assert slab * n_steps == d_ff and slab % (2 * V7X_SUBLANES) == 0
    blocks = (_nbytes((tq, zc), BF16) + 2 * _nbytes((N_MEM, MEM_WIDTH), F32) + _nbytes((tq, D_MODEL), BF16)
              + _nbytes((slab, d), F32) + _nbytes((slab, d), BF16))
    scratch = _nbytes((tq + CARRY_ROWS, TOKEN_WIDTH), F32)
    return pl.pallas_call(
        functools.partial(_conv_prompt_kernel, tiles_per_seq=tiles_per_seq),
        grid=(n_steps,),
        in_specs=[
            pl.BlockSpec((tq, zc), lambda s: (s, 0)),
            pl.BlockSpec((None, N_MEM, MEM_WIDTH), lambda s: (layer, s // tiles_per_seq, 0)),
            pl.BlockSpec((None, N_MEM, MEM_WIDTH), lambda s: (layer, s // tiles_per_seq, 0)),
            pl.BlockSpec((None, CONV_WIDTH, TOKEN_WIDTH), lambda s: (0, 0, 0)),
            pl.BlockSpec((None, slab, d), lambda s: (layer, s, 0)),
        ],
        out_specs=[
            pl.BlockSpec((tq, D_MODEL), lambda s: (s, 0)),
            pl.BlockSpec((None, CONV_WIDTH - 1, TOKEN_WIDTH), lambda s: (s // tiles_per_seq, 0, 0)),
            pl.BlockSpec((slab, d), lambda s: (s, 0)),
        ],
        out_shape=[
            jax.ShapeDtypeStruct((rows, D_MODEL), BF16),
            jax.ShapeDtypeStruct((batch, CONV_WIDTH - 1, TOKEN_WIDTH), F32),
            jax.ShapeDtypeStruct((d_ff, d), BF16),
        ],
        scratch_shapes=[pltpu.VMEM((tq + CARRY_ROWS, TOKEN_WIDTH), F32)],
        compiler_params=pltpu.CompilerParams(
            dimension_semantics=("arbitrary",),
            vmem_limit_bytes=_vmem_limit(blocks, scratch + 6 * _nbytes((tq, TOKEN_WIDTH), F32))),
        name="conv_prompt",
    )(z, mk, mv, conv_w, w_down)


def _conv_sample_kernel(z_ref, st_ref, mk_ref, mv_ref, cw_ref, wo_ref, mix_ref, nst_ref, wob_ref,
                        ext_ref, mixf_ref, *, dec_seq):
    t = dec_seq
    wob_ref[...] = wo_ref[...].astype(BF16)
    for n in range(SAMPLE_GROUP):
        r0, r1 = n * t, (n + 1) * t
        c = z_ref[r0:r1, TOKEN_WIDTH:2 * TOKEN_WIDTH].astype(F32)
        u = z_ref[r0:r1, 2 * TOKEN_WIDTH:3 * TOKEN_WIDTH].astype(F32)
        cu = c * u
        ext_ref[CARRY_ROWS - 2:CARRY_ROWS, :] = st_ref[n]
        ext_ref[CARRY_ROWS:CARRY_ROWS + t, :] = cu
        conv = (cw_ref[0:1, :] * ext_ref[CARRY_ROWS - 2:CARRY_ROWS - 2 + t, :]
                + cw_ref[1:2, :] * ext_ref[CARRY_ROWS - 1:CARRY_ROWS - 1 + t, :]
                + cw_ref[2:3, :] * cu)
        b = z_ref[r0:r1, 0:TOKEN_WIDTH].astype(F32)
        mixf_ref[r0:r1, 0:TOKEN_WIDTH] = b * conv
        nst_ref[n] = ext_ref[CARRY_ROWS + t - 2:CARRY_ROWS + t, :]
        for h in range(MEM_HEADS):
            lo, hi = h * MEM_HEAD_DIM, (h + 1) * MEM_HEAD_DIM
            q = z_ref[r0:r1, 3 * TOKEN_WIDTH + lo:3 * TOKEN_WIDTH + hi]
            o = _cross_attention_head(q, mk_ref[n, :, h, :].astype(BF16), mv_ref[n, :, h, :].astype(BF16))
            mixf_ref[r0:r1, TOKEN_WIDTH + lo:TOKEN_WIDTH + hi] = o
    mix_ref[...] = mixf_ref[...].astype(BF16)


def _conv_sample(z, state, mem_k, mem_v, conv_w, w_out, *, layer, dec_seq):
    rows, zc = z.shape
    dec_batch = state.shape[1]
    g = SAMPLE_GROUP
    gr = g * dec_seq
    n_steps = dec_batch // g
    wk, wn = w_out.shape[1:]
    slab = wk // n_steps
    assert slab * n_steps == wk and slab % (2 * V7X_SUBLANES) == 0
    blocks = (_nbytes((gr, zc), BF16) + 2 * _nbytes((g, CONV_WIDTH - 1, TOKEN_WIDTH), F32)
              + 2 * _nbytes((g, N_MEM, V7X_SUBLANES, MEM_HEAD_DIM), F32) + _nbytes((gr, D_MODEL), BF16)
              + _nbytes((slab, wn), F32) + _nbytes((slab, wn), BF16))
    scratch = _nbytes((2 * CARRY_ROWS, TOKEN_WIDTH), F32) + _nbytes((gr, D_MODEL), F32)
    return pl.pallas_call(
        functools.partial(_conv_sample_kernel, dec_seq=dec_seq),
        grid=(n_steps,),
        in_specs=[
            pl.BlockSpec((gr, zc), lambda i: (i, 0)),
            pl.BlockSpec((None, g, CONV_WIDTH - 1, TOKEN_WIDTH), lambda i: (0, i, 0, 0)),
            pl.BlockSpec((None, g, N_MEM, MEM_HEADS, MEM_HEAD_DIM), lambda i: (layer, i, 0, 0, 0)),
            pl.BlockSpec((None, g, N_MEM, MEM_HEADS, MEM_HEAD_DIM), lambda i: (layer, i, 0, 0, 0)),
            pl.BlockSpec((None, CONV_WIDTH, TOKEN_WIDTH), lambda i: (0, 0, 0)),
            pl.BlockSpec((None, slab, wn), lambda i: (0, i, 0)),
        ],
        out_specs=[
            pl.BlockSpec((gr, D_MODEL), lambda i: (i, 0)),
            pl.BlockSpec((g, CONV_WIDTH - 1, TOKEN_WIDTH), lambda i: (i, 0, 0)),
            pl.BlockSpec((slab, wn), lambda i: (i, 0)),
        ],
        out_shape=[
            jax.ShapeDtypeStruct((rows, D_MODEL), BF16),
            jax.ShapeDtypeStruct((dec_batch, CONV_WIDTH - 1, TOKEN_WIDTH), F32),
            jax.ShapeDtypeStruct((wk, wn), BF16),
        ],
        scratch_shapes=[pltpu.VMEM((2 * CARRY_ROWS, TOKEN_WIDTH), F32), pltpu.VMEM((gr, D_MODEL), F32)],
        compiler_params=pltpu.CompilerParams(
            dimension_semantics=("arbitrary",),
            vmem_limit_bytes=_vmem_limit(blocks, scratch)),
        name="conv_sample",
    )(z, state, mem_k, mem_v, conv_w, w_out)


def _band_attention_head(q, k, v, sink, mask):
    s = _dot_nt(q * (HEAD_DIM ** -0.5), k)
    s = jnp.where(mask, s, -jnp.inf)
    m = jnp.maximum(jnp.max(s, axis=-1, keepdims=True), sink)
    e = jnp.exp(s - m)
    p = e / (jnp.sum(e, axis=-1, keepdims=True) + jnp.exp(sink - m))
    return _dot(p.astype(BF16), v)


def _swa_prompt_kernel(sink_ref, zq_ref, zp_ref, mk_ref, mv_ref, *refs, blocks_per_seq, n_cast_steps):
    n_w = (len(refs) - 1) // 2
    w_refs, mix_ref, wb_refs = refs[:n_w], refs[n_w], refs[n_w + 1:]
    s = pl.program_id(0)

    @pl.when(s < n_cast_steps)
    def _():
        for w_ref, wb_ref in zip(w_refs, wb_refs):
            wb_ref[...] = w_ref[...].astype(BF16)

    k_off = TOKEN_WIDTH
    v_off = TOKEN_WIDTH + KV_WIDTH
    qm_off = TOKEN_WIDTH + 2 * KV_WIDTH
    has_prev = (s % blocks_per_seq) > 0
    row = lax.broadcasted_iota(jnp.int32, (WINDOW, 2 * WINDOW), 0)
    col = lax.broadcasted_iota(jnp.int32, (WINDOW, 2 * WINDOW), 1)
    mask = jnp.logical_or(jnp.logical_and(jnp.logical_and(col < WINDOW, col > row), has_prev),
                          jnp.logical_and(col >= WINDOW, col - WINDOW <= row))
    for kh in range(N_KV_HEADS):
        lo, hi = kh * HEAD_DIM, (kh + 1) * HEAD_DIM
        k = jnp.concatenate([zp_ref[:, lo:hi], zq_ref[:, k_off + lo:k_off + hi]], axis=0)
        v = jnp.concatenate([zp_ref[:, KV_WIDTH + lo:KV_WIDTH + hi], zq_ref[:, v_off + lo:v_off + hi]], axis=0)
        for g in range(GROUP):
            h = kh * GROUP + g
            q = zq_ref[:, h * HEAD_DIM:(h + 1) * HEAD_DIM]
            o = _band_attention_head(q, k, v, sink_ref[h], mask)
            mix_ref[:, h * HEAD_DIM:(h + 1) * HEAD_DIM] = o.astype(BF16)
    for h in range(MEM_HEADS):
        lo, hi = h * MEM_HEAD_DIM, (h + 1) * MEM_HEAD_DIM
        q = zq_ref[:, qm_off + lo:qm_off + hi]
        o = _cross_attention_head(q, mk_ref[:, lo:hi].astype(BF16), mv_ref[:, lo:hi].astype(BF16))
        mix_ref[:, TOKEN_WIDTH + lo:TOKEN_WIDTH + hi] = o.astype(BF16)


def _swa_prompt(z, mk, mv, sinks, weights, *, layer, batch, seq):
    rows, zc = z.shape
    tq = WINDOW
    blocks_per_seq = seq // tq
    n_steps = batch * blocks_per_seq
    kv_col_block = TOKEN_WIDTH // (2 * KV_WIDTH)
    n_cast_steps = n_steps // 2
    slab_map = lambda s: (jnp.minimum(s, n_cast_steps - 1), 0)
    blocks = (_nbytes((tq, zc), BF16) + _nbytes((tq, 2 * KV_WIDTH), BF16)
              + 2 * _nbytes((N_MEM, MEM_WIDTH), F32) + _nbytes((tq, D_MODEL), BF16))
    w_in_specs, w_out_specs, w_out_shapes = [], [], []
    for w in weights:
        r, c = w.shape[1:]
        slab = r // n_cast_steps
        assert slab * n_cast_steps == r and slab % (2 * V7X_SUBLANES) == 0
        blocks += _nbytes((slab, c), F32) + _nbytes((slab, c), BF16)
        w_in_specs.append(pl.BlockSpec((None, slab, c), lambda s: (layer, jnp.minimum(s, n_cast_steps - 1), 0)))
        w_out_specs.append(pl.BlockSpec((slab, c), slab_map))
        w_out_shapes.append(jax.ShapeDtypeStruct((r, c), BF16))
    return pl.pallas_call(
        functools.partial(_swa_prompt_kernel, blocks_per_seq=blocks_per_seq, n_cast_steps=n_cast_steps),
        grid=(n_steps,),
        in_specs=[
            pl.BlockSpec(memory_space=pltpu.SMEM),
            pl.BlockSpec((tq, zc), lambda s: (s, 0)),
            pl.BlockSpec((tq, 2 * KV_WIDTH), lambda s: (jnp.maximum(s - 1, 0), kv_col_block)),
            pl.BlockSpec((None, N_MEM, MEM_WIDTH), lambda s: (layer, s // blocks_per_seq, 0)),
            pl.BlockSpec((None, N_MEM, MEM_WIDTH), lambda s: (layer, s // blocks_per_seq, 0)),
        ] + w_in_specs,
        out_specs=[pl.BlockSpec((tq, D_MODEL), lambda s: (s, 0))] + w_out_specs,
        out_shape=[jax.ShapeDtypeStruct((rows, D_MODEL), BF16)] + w_out_shapes,
        compiler_params=pltpu.CompilerParams(
            dimension_semantics=("arbitrary",),
            vmem_limit_bytes=_vmem_limit(blocks, 0)),
        name="swa_prompt",
    )(sinks, z, z, mk, mv, *weights)


def _swa_sample_kernel(sink_ref, z_ref, ck_ref, cv_ref, mk_ref, mv_ref, wo_ref,
                       mix_ref, nk_ref, nv_ref, wob_ref, knew_ref, vnew_ref, mixf_ref, *, dec_seq):
    t = dec_seq
    wob_ref[...] = wo_ref[...].astype(BF16)
    k_off = TOKEN_WIDTH
    v_off = TOKEN_WIDTH + KV_WIDTH
    qm_off = TOKEN_WIDTH + 2 * KV_WIDTH
    rows = GROUP * t
    qi = lax.broadcasted_iota(jnp.int32, (rows, 2 * WINDOW), 0) % t
    col = lax.broadcasted_iota(jnp.int32, (rows, 2 * WINDOW), 1)
    mask = jnp.logical_or(jnp.logical_and(col < WINDOW, col > qi),
                          jnp.logical_and(col >= WINDOW, col - WINDOW <= qi))
    knew_ref[...] = jnp.zeros((WINDOW, KV_WIDTH), F32)
    vnew_ref[...] = jnp.zeros((WINDOW, KV_WIDTH), F32)
    for n in range(SAMPLE_GROUP):
        r0, r1 = n * t, (n + 1) * t
        k_new = z_ref[r0:r1, k_off:k_off + KV_WIDTH].astype(F32)
        v_new = z_ref[r0:r1, v_off:v_off + KV_WIDTH].astype(F32)
        knew_ref[0:t, :] = k_new
        vnew_ref[0:t, :] = v_new
        nk_ref[n, 0:WINDOW - t, :] = ck_ref[n, t:WINDOW, :]
        nv_ref[n, 0:WINDOW - t, :] = cv_ref[n, t:WINDOW, :]
        nk_ref[n, WINDOW - t:WINDOW, :] = k_new
        nv_ref[n, WINDOW - t:WINDOW, :] = v_new
        for kh in range(N_KV_HEADS):
            lo, hi = kh * HEAD_DIM, (kh + 1) * HEAD_DIM
            k = jnp.concatenate([ck_ref[n, :, lo:hi], knew_ref[:, lo:hi]], axis=0).astype(BF16)
            v = jnp.concatenate([cv_ref[n, :, lo:hi], vnew_ref[:, lo:hi]], axis=0).astype(BF16)
            q = jnp.concatenate(
                [z_ref[r0:r1, (kh * GROUP + g) * HEAD_DIM:(kh * GROUP + g + 1) * HEAD_DIM].astype(F32)
                 for g in range(GROUP)], axis=0).astype(BF16)
            sink = jnp.concatenate(
                [jnp.full((t, 1), sink_ref[kh * GROUP + g], F32) for g in range(GROUP)], axis=0)
            o = _band_attention_head(q, k, v, sink, mask)
            for g in range(GROUP):
                h = kh * GROUP + g
                mixf_ref[r0:r1, h * HEAD_DIM:(h + 1) * HEAD_DIM] = o[g * t:(g + 1) * t, :]
        for h in range(MEM_HEADS):
            lo, hi = h * MEM_HEAD_DIM, (h + 1) * MEM_HEAD_DIM
            q = z_ref[r0:r1, qm_off + lo:qm_off + hi]
            o = _cross_attention_head(q, mk_ref[n, :, h, :].astype(BF16), mv_ref[n, :, h, :].astype(BF16))
            mixf_ref[r0:r1, TOKEN_WIDTH + lo:TOKEN_WIDTH + hi] = o
    mix_ref[...] = mixf_ref[...].astype(BF16)


def _swa_sample(z, cache_k, cache_v, mem_k, mem_v, sinks, w_out, *, layer, dec_seq):
    rows, zc = z.shape
    dec_batch = cache_k.shape[0]
    g = SAMPLE_GROUP
    gr = g * dec_seq
    n_steps = dec_batch // g
    wk, wn = w_out.shape[1:]
    slab = wk // n_steps
    assert slab * n_steps == wk and slab % (2 * V7X_SUBLANES) == 0
    win = jax.ShapeDtypeStruct((dec_batch, WINDOW, KV_WIDTH), F32)
    blocks = (_nbytes((gr, zc), BF16) + 4 * _nbytes((g, WINDOW, KV_WIDTH), F32)
              + 2 * _nbytes((g, N_MEM, V7X_SUBLANES, MEM_HEAD_DIM), F32) + _nbytes((gr, D_MODEL), BF16)
              + _nbytes((slab, wn), F32) + _nbytes((slab, wn), BF16))
    scratch = 2 * _nbytes((WINDOW, KV_WIDTH), F32) + _nbytes((gr, D_MODEL), F32)
    return pl.pallas_call(
        functools.partial(_swa_sample_kernel, dec_seq=dec_seq),
        grid=(n_steps,),
        in_specs=[
            pl.BlockSpec(memory_space=pltpu.SMEM),
            pl.BlockSpec((gr, zc), lambda i: (i, 0)),
            pl.BlockSpec((g, WINDOW, KV_WIDTH), lambda i: (i, 0, 0)),
            pl.BlockSpec((g, WINDOW, KV_WIDTH), lambda i: (i, 0, 0)),
            pl.BlockSpec((None, g, N_MEM, MEM_HEADS, MEM_HEAD_DIM), lambda i: (layer, i, 0, 0, 0)),
            pl.BlockSpec((None, g, N_MEM, MEM_HEADS, MEM_HEAD_DIM), lambda i: (layer, i, 0, 0, 0)),
            pl.BlockSpec((None, slab, wn), lambda i: (0, i, 0)),
        ],
        out_specs=[
            pl.BlockSpec((gr, D_MODEL), lambda i: (i, 0)),
            pl.BlockSpec((g, WINDOW, KV_WIDTH), lambda i: (i, 0, 0)),
            pl.BlockSpec((g, WINDOW, KV_WIDTH), lambda i: (i, 0, 0)),
            pl.BlockSpec((slab, wn), lambda i: (i, 0)),
        ],
        out_shape=[jax.ShapeDtypeStruct((rows, D_MODEL), BF16), win, win,
                   jax.ShapeDtypeStruct((wk, wn), BF16)],
        scratch_shapes=[pltpu.VMEM((WINDOW, KV_WIDTH), F32), pltpu.VMEM((WINDOW, KV_WIDTH), F32),
                        pltpu.VMEM((gr, D_MODEL), F32)],
        compiler_params=pltpu.CompilerParams(
            dimension_semantics=("arbitrary",),
            vmem_limit_bytes=_vmem_limit(blocks, scratch)),
        name="swa_sample",
    )(sinks, z, cache_k, cache_v, mem_k, mem_v, w_out)


def kernel(x_prompt, x_sample, mem_prompt, state_conv, cache_win_k, cache_win_v, cache_mem_k, cache_mem_v,
           norm_mix, norm_mem, w_mem_kv, norm_ffn, w_gate, w_up, w_down,
           conv_w_in, conv_w, conv_w_out, attn_w_in, attn_sinks, attn_w_out, norm_final):
    batch, seq, d = x_prompt.shape
    dec_batch, dec_seq, _ = x_sample.shape
    depth = norm_mix.shape[0]
    d_ff = w_gate.shape[2]
    prompt_rows = batch * seq
    sample_rows = dec_batch * dec_seq
    assert d == D_MODEL and depth == 2 and seq % CONV_ROWS == 0 and seq % WINDOW == 0
    assert prompt_rows % PROMPT_TILE == 0 and dec_batch % SAMPLE_GROUP == 0
    assert dec_seq == V7X_SUBLANES and d_ff % FFN_COL_TILE == 0 and d % COL_TILE == 0
    assert d_ff % DOWN_K_TILE == 0 and d % DOWN_X_CHUNK == 0
    assert prompt_rows % FFN_ROW_TILE == 0 and prompt_rows % sample_rows == 0

    xp = x_prompt.reshape(prompt_rows, d)
    xs = x_sample.reshape(sample_rows, d)
    mem = mem_prompt.reshape(batch * N_MEM, d)
    mem_k_s, mem_v_s = cache_mem_k, cache_mem_v
    g_mix = norm_mix.reshape(depth, 1, d)
    g_ffn = norm_ffn.reshape(depth, 1, d)

    mk, mv = _mem_kv(mem, norm_mem.reshape(depth, 1, d), w_mem_kv)

    hp, hs = _norm_rows(xp, xs, g_mix, layer=0)
    zp, zs = _matmul(hp, hs, conv_w_in, w_layer=0)
    mix_s, conv_s, wo_bf = _conv_sample(zs, state_conv, mem_k_s, mem_v_s, conv_w, conv_w_out,
                                        layer=0, dec_seq=dec_seq)
    mix_p, conv_p, wd_bf = _conv_prompt(zp, mk, mv, conv_w, w_down, layer=0, batch=batch, seq=seq)
    xp, xs, hp, hs = _out_proj(mix_p, mix_s, wo_bf, xp, xs, g_ffn, layer=0)
    ap, as_ = _ffn_up(hp, hs, w_gate, w_up, layer=0)
    xp, xs, hp, hs = _ffn_down(ap, as_, wd_bf[None], xp, xs, g_mix, w_layer=0, g_layer=1, final=False)

    zp, zs = _matmul(hp, hs, attn_w_in, w_layer=0)
    sinks = attn_sinks[0]
    mix_s, win_k_s, win_v_s, wo_bf = _swa_sample(
        zs, cache_win_k[0].reshape(dec_batch, WINDOW, KV_WIDTH), cache_win_v[0].reshape(dec_batch, WINDOW, KV_WIDTH),
        mem_k_s, mem_v_s, sinks, attn_w_out, layer=1, dec_seq=dec_seq)
    mix_p, wd_bf, wg_bf, wu_bf = _swa_prompt(zp, mk, mv, sinks, [w_down, w_gate, w_up],
                                             layer=1, batch=batch, seq=seq)
    xp, xs, hp, hs = _out_proj(mix_p, mix_s, wo_bf, xp, xs, g_ffn, layer=1)
    ap, as_ = _ffn_up(hp, hs, wg_bf[None], wu_bf[None], layer=0)
    y_prompt, y_sample = _ffn_down(ap, as_, wd_bf[None], xp, xs, norm_final.reshape(1, 1, d),
                                   w_layer=0, g_layer=0, final=True)

    win_p = zp.reshape(batch, seq, -1)[:, seq - WINDOW:, TOKEN_WIDTH:TOKEN_WIDTH + 2 * KV_WIDTH].astype(F32)
    kv_shape = (1, -1, WINDOW, N_KV_HEADS, HEAD_DIM)
    mem_shape = (depth, batch, N_MEM, MEM_HEADS, MEM_HEAD_DIM)
    return (y_prompt.reshape(batch, seq, d),
            y_sample.reshape(dec_batch, dec_seq, d),
            conv_p[None],
            conv_s[None],
            win_p[..., :KV_WIDTH].reshape(kv_shape),
            win_p[..., KV_WIDTH:].reshape(kv_shape),
            win_k_s.reshape(kv_shape),
            win_v_s.reshape(kv_shape),
            mk.reshape(mem_shape),
            mv.reshape(mem_shape))
```

```python
import functools

import jax
import jax.numpy as jnp
from jax import lax
from jax.experimental import pallas as pl
from jax.experimental.pallas import tpu as pltpu

F32 = jnp.float32
BF16 = jnp.bfloat16

D_MODEL = 2048
N_MEM = 256
MEM_HEADS = 4
MEM_WIDTH = D_MODEL // 4
MEM_HEAD_DIM = MEM_WIDTH // MEM_HEADS
TOKEN_WIDTH = D_MODEL - MEM_WIDTH
CONV_WIDTH = 3
WINDOW = 128
HEAD_DIM = 64
N_HEADS = TOKEN_WIDTH // HEAD_DIM
N_KV_HEADS = 4
GROUP = N_HEADS // N_KV_HEADS
KV_WIDTH = N_KV_HEADS * HEAD_DIM
EPS = 1e-6

V7X_VMEM_BYTES = 64 * 1024 * 1024
V7X_SUBLANES = 8
V7X_MXU_DEPTH = 256

PROMPT_TILE = 1024
COL_TILE = 512
DOWN_K_TILE = 1408
DOWN_X_CHUNK = 512
FFN_ROW_TILE = 2048
FFN_COL_TILE = 512
MXU_ROWS = 1024
OUT_ROWS = 512
CONV_ROWS = 512
SCORE_LOOKAHEAD = 1
SAMPLE_GROUP = 8
CARRY_ROWS = V7X_SUBLANES


def _nbytes(shape, dtype):
    n = 1
    for s in shape:
        n *= s
    return n * jnp.dtype(dtype).itemsize


def _vmem_limit(block_bytes, scratch_bytes):
    need = 2 * block_bytes + scratch_bytes
    return int(min(need + max(need // 4, 8 << 20), V7X_VMEM_BYTES - (6 << 20)))


def _rmsnorm(x, g):
    r = lax.rsqrt(jnp.mean(x * x, axis=-1, keepdims=True) + EPS)
    return (x * r) * g


def _softmax_rows(s):
    m = jnp.max(s, axis=-1, keepdims=True)
    e = jnp.exp(s - m)
    return e / jnp.sum(e, axis=-1, keepdims=True)


def _dot(a, b):
    return jnp.dot(a, b, preferred_element_type=F32)


def _dot_nt(a, b):
    return lax.dot_general(a, b, (((1,), (1,)), ((), ())), preferred_element_type=F32)


def _prompt_rows_map(i, j):
    return (i, 0)


def _prompt_tile_map(i, j):
    return (i, j)


def _sample_tile_map(n_pt):
    return lambda i, j: (0, jnp.where(i == n_pt - 1, j, 0))


def _const_map(i, j):
    return (0, 0)


def _norm_rows_kernel(xp_ref, xs_ref, g_ref, hp_ref, hs_ref, *, n_pt):
    hp_ref[...] = _rmsnorm(xp_ref[...], g_ref[...]).astype(BF16)

    @pl.when(pl.program_id(0) == n_pt - 1)
    def _():
        hs_ref[...] = _rmsnorm(xs_ref[...], g_ref[...]).astype(BF16)


def _norm_rows(xp, xs, g, *, layer):
    rp, d = xp.shape
    rs = xs.shape[0]
    tm = OUT_ROWS
    n_pt = rp // tm
    blocks = _nbytes((tm, d), F32) + _nbytes((tm, d), BF16) + _nbytes((rs, d), F32) + _nbytes((rs, d), BF16)
    return pl.pallas_call(
        functools.partial(_norm_rows_kernel, n_pt=n_pt),
        grid=(n_pt,),
        in_specs=[
            pl.BlockSpec((tm, d), lambda s: (s, 0)),
            pl.BlockSpec((rs, d), lambda s: (0, 0)),
            pl.BlockSpec((None, 1, d), lambda s: (layer, 0, 0)),
        ],
        out_specs=[
            pl.BlockSpec((tm, d), lambda s: (s, 0)),
            pl.BlockSpec((rs, d), lambda s: (0, 0)),
        ],
        out_shape=[jax.ShapeDtypeStruct((rp, d), BF16), jax.ShapeDtypeStruct((rs, d), BF16)],
        compiler_params=pltpu.CompilerParams(
            dimension_semantics=("arbitrary",),
            vmem_limit_bytes=_vmem_limit(blocks, 0)),
        name="norm_rows",
    )(xp, xs, g)


def _swiglu(h, wg, wu):
    gate = _dot(h, wg)
    up = _dot(h, wu)
    return (gate * jax.nn.sigmoid(gate) * up).astype(BF16)


def _ffn_up_kernel(hp_ref, hs_ref, wg_ref, wu_ref, ap_ref, as_ref, *, n_pt):
    wg, wu = wg_ref[...].astype(BF16), wu_ref[...].astype(BF16)
    for r in range(0, hp_ref.shape[0], MXU_ROWS):
        ap_ref[r:r + MXU_ROWS, :] = _swiglu(hp_ref[r:r + MXU_ROWS, :], wg, wu)

    @pl.when(pl.program_id(0) == n_pt - 1)
    def _():
        as_ref[...] = _swiglu(hs_ref[...], wg_ref[...].astype(BF16), wu_ref[...].astype(BF16))


def _ffn_up(hp, hs, wg, wu, *, layer):
    rp, d = hp.shape
    rs = hs.shape[0]
    n = wg.shape[2]
    tm, tn = FFN_ROW_TILE, FFN_COL_TILE
    n_pt, n_j = rp // tm, n // tn
    blocks = (_nbytes((tm, d), BF16) + _nbytes((rs, d), BF16) + 2 * _nbytes((d, tn), wg.dtype)
              + _nbytes((tm, tn), BF16) + _nbytes((rs, tn), BF16))
    temps = 2 * _nbytes((d, tn), BF16) + 3 * _nbytes((tm, tn), F32)
    w_spec = pl.BlockSpec((None, d, tn), lambda i, j: (layer, 0, j))
    return pl.pallas_call(
        functools.partial(_ffn_up_kernel, n_pt=n_pt),
        grid=(n_pt, n_j),
        in_specs=[
            pl.BlockSpec((tm, d), _prompt_rows_map),
            pl.BlockSpec((rs, d), _const_map),
            w_spec,
            w_spec,
        ],
        out_specs=[
            pl.BlockSpec((tm, tn), _prompt_tile_map),
            pl.BlockSpec((rs, tn), _sample_tile_map(n_pt)),
        ],
        out_shape=[jax.ShapeDtypeStruct((rp, n), BF16), jax.ShapeDtypeStruct((rs, n), BF16)],
        compiler_params=pltpu.CompilerParams(
            dimension_semantics=("arbitrary", "arbitrary"),
            vmem_limit_bytes=_vmem_limit(blocks, temps)),
        name="ffn_up",
    )(hp, hs, wg, wu)


def _out_proj_kernel(ap_ref, as_ref, w_ref, xp_ref, xs_ref, g_ref, op_ref, os_ref, hp_ref, hs_ref, *, n_pt):
    x = xp_ref[...] + _dot(ap_ref[...], w_ref[...])
    op_ref[...] = x
    hp_ref[...] = _rmsnorm(x, g_ref[...]).astype(BF16)

    @pl.when(pl.program_id(0) == n_pt - 1)
    def _():
        x = xs_ref[...] + _dot(as_ref[...], w_ref[...])
        os_ref[...] = x
        hs_ref[...] = _rmsnorm(x, g_ref[...]).astype(BF16)


def _out_proj(ap, as_, w, xp, xs, g, *, layer):
    rp, k = ap.shape
    rs = as_.shape[0]
    n = w.shape[1]
    tm = OUT_ROWS
    n_pt = rp // tm
    rows_map = lambda s: (s, 0)
    const = lambda s: (0, 0)
    once = dict(pipeline_mode=pl.Buffered(1))
    blocks = _nbytes((tm, k), BF16) + 2 * _nbytes((tm, n), F32) + _nbytes((tm, n), BF16)
    resident = (_nbytes((k, n), BF16) + _nbytes((rs, k), BF16) + _nbytes((rs, n), F32)
                + 2 * (_nbytes((rs, n), F32) + _nbytes((rs, n), BF16)))
    return pl.pallas_call(
        functools.partial(_out_proj_kernel, n_pt=n_pt),
        grid=(n_pt,),
        in_specs=[
            pl.BlockSpec((tm, k), rows_map),
            pl.BlockSpec((rs, k), const, **once),
            pl.BlockSpec((k, n), const, **once),
            pl.BlockSpec((tm, n), rows_map),
            pl.BlockSpec((rs, n), const, **once),
            pl.BlockSpec((None, 1, n), lambda s: (layer, 0, 0)),
        ],
        out_specs=[
            pl.BlockSpec((tm, n), rows_map),
            pl.BlockSpec((rs, n), const),
            pl.BlockSpec((tm, n), rows_map),
            pl.BlockSpec((rs, n), const),
        ],
        out_shape=[jax.ShapeDtypeStruct((rp, n), F32), jax.ShapeDtypeStruct((rs, n), F32),
                   jax.ShapeDtypeStruct((rp, n), BF16), jax.ShapeDtypeStruct((rs, n), BF16)],
        compiler_params=pltpu.CompilerParams(
            dimension_semantics=("arbitrary",),
            vmem_limit_bytes=_vmem_limit(blocks, resident + 2 * _nbytes((tm, n), F32))),
        name="out_proj",
    )(ap, as_, w, xp, xs, g)


def _matmul_kernel(hp_ref, hs_ref, w_ref, zp_ref, zs_ref, *, n_pt):
    w = w_ref[...].astype(BF16)
    for r in range(0, hp_ref.shape[0], MXU_ROWS):
        zp_ref[r:r + MXU_ROWS, :] = _dot(hp_ref[r:r + MXU_ROWS, :], w).astype(zp_ref.dtype)

    @pl.when(pl.program_id(0) == n_pt - 1)
    def _():
        zs_ref[...] = _dot(hs_ref[...], w_ref[...].astype(BF16)).astype(zs_ref.dtype)


def _matmul(hp, hs, w, *, w_layer):
    rp, d = hp.shape
    rs = hs.shape[0]
    n = w.shape[2]
    tm, tn = FFN_ROW_TILE, COL_TILE
    n_pt, n_j = rp // tm, n // tn
    blocks = (_nbytes((tm, d), BF16) + _nbytes((rs, d), BF16) + _nbytes((d, tn), F32)
              + _nbytes((tm, tn), BF16) + _nbytes((rs, tn), BF16))
    temps = _nbytes((d, tn), BF16) + _nbytes((tm, tn), F32)
    return pl.pallas_call(
        functools.partial(_matmul_kernel, n_pt=n_pt),
        grid=(n_pt, n_j),
        in_specs=[
            pl.BlockSpec((tm, d), _prompt_rows_map),
            pl.BlockSpec((rs, d), _const_map),
            pl.BlockSpec((None, d, tn), lambda i, j: (w_layer, 0, j)),
        ],
        out_specs=[
            pl.BlockSpec((tm, tn), _prompt_tile_map),
            pl.BlockSpec((rs, tn), _sample_tile_map(n_pt)),
        ],
        out_shape=[jax.ShapeDtypeStruct((rp, n), BF16), jax.ShapeDtypeStruct((rs, n), BF16)],
        compiler_params=pltpu.CompilerParams(
            dimension_semantics=("arbitrary", "arbitrary"),
            vmem_limit_bytes=_vmem_limit(blocks, temps)),
        name="matmul",
    )(hp, hs, w)


def _ffn_down_kernel(ap_ref, as_ref, w_ref, xp_ref, xs_ref, g_ref, *refs, n_pt, n_k, n_xc, final):
    i, k = pl.program_id(0), pl.program_id(1)
    if final:
        op_ref, os_ref, apl_ref, asl_ref, wl_ref = refs
    else:
        op_ref, os_ref, hp_ref, hs_ref, apl_ref, asl_ref, wl_ref = refs
    xc = xp_ref.shape[1]
    kf = (ap_ref.shape[1] // V7X_MXU_DEPTH) * V7X_MXU_DEPTH
    even = k % 2 == 0

    def whole_passes(a_ref):
        return _dot(a_ref[:, :kf], w_ref[:kf, :])

    def with_stash(a_ref, al_ref):
        return _dot(jnp.concatenate([al_ref[...], a_ref[...]], axis=1),
                    jnp.concatenate([wl_ref[...], w_ref[...]], axis=0))

    @pl.when(k == 0)
    def _():
        op_ref[...] = whole_passes(ap_ref)

    @pl.when(jnp.logical_and(k > 0, even))
    def _():
        op_ref[...] = op_ref[...] + whole_passes(ap_ref)

    @pl.when(jnp.logical_not(even))
    def _():
        op_ref[...] = op_ref[...] + with_stash(ap_ref, apl_ref)

    for c in range(n_xc):
        @pl.when(k == c)
        def _():
            op_ref[:, c * xc:(c + 1) * xc] = op_ref[:, c * xc:(c + 1) * xc] + xp_ref[...]

    @pl.when(k == n_k - 1)
    def _():
        if final:
            op_ref[...] = _rmsnorm(op_ref[...], g_ref[...])
        else:
            hp_ref[...] = _rmsnorm(op_ref[...], g_ref[...]).astype(BF16)

    @pl.when(i == n_pt - 1)
    def _():
        @pl.when(k == 0)
        def _():
            os_ref[...] = xs_ref[...] + whole_passes(as_ref)

        @pl.when(jnp.logical_and(k > 0, even))
        def _():
            os_ref[...] = os_ref[...] + whole_passes(as_ref)

        @pl.when(jnp.logical_not(even))
        def _():
            os_ref[...] = os_ref[...] + with_stash(as_ref, asl_ref)

        @pl.when(even)
        def _():
            asl_ref[...] = as_ref[:, kf:]

        @pl.when(k == n_k - 1)
        def _():
            if final:
                os_ref[...] = _rmsnorm(os_ref[...], g_ref[...])
            else:
                hs_ref[...] = _rmsnorm(os_ref[...], g_ref[...]).astype(BF16)

    @pl.when(even)
    def _():
        apl_ref[...] = ap_ref[:, kf:]
        wl_ref[...] = w_ref[kf:, :]


def _ffn_down(ap, as_, w, xp, xs, g, *, w_layer, g_layer, final):
    rp, kdim = ap.shape
    rs = as_.shape[0]
    n = w.shape[2]
    tm, tk, xc = PROMPT_TILE, DOWN_K_TILE, DOWN_X_CHUNK
    n_pt, n_k, n_xc = rp // tm, kdim // tk, n // xc
    k_left = tk % V7X_MXU_DEPTH
    assert n_xc <= n_k and n_k % 2 == 0 and 2 * k_left == V7X_MXU_DEPTH and w.dtype == BF16
    rows = lambda i, k: (i, 0)
    blocks = (_nbytes((tm, tk), BF16) + _nbytes((rs, tk), BF16) + _nbytes((tk, n), w.dtype)
              + _nbytes((tm, xc), F32) + _nbytes((rs, n), F32)
              + _nbytes((tm, n), F32) + _nbytes((rs, n), F32))
    out_specs = [pl.BlockSpec((tm, n), rows), pl.BlockSpec((rs, n), _const_map)]
    out_shape = [jax.ShapeDtypeStruct((rp, n), F32), jax.ShapeDtypeStruct((rs, n), F32)]
    if not final:
        blocks += _nbytes((tm, n), BF16) + _nbytes((rs, n), BF16)
        out_specs += [pl.BlockSpec((tm, n), rows), pl.BlockSpec((rs, n), _const_map)]
        out_shape += [jax.ShapeDtypeStruct((rp, n), BF16), jax.ShapeDtypeStruct((rs, n), BF16)]
    return pl.pallas_call(
        functools.partial(_ffn_down_kernel, n_pt=n_pt, n_k=n_k, n_xc=n_xc, final=final),
        grid=(n_pt, n_k),
        in_specs=[
            pl.BlockSpec((tm, tk), lambda i, k: (i, k)),
            pl.BlockSpec((rs, tk), lambda i, k: (0, jnp.where(i == n_pt - 1, k, 0))),
            pl.BlockSpec((None, tk, n), lambda i, k: (w_layer, k, 0)),
            pl.BlockSpec((tm, xc), lambda i, k: (i, jnp.minimum(k, n_xc - 1))),
            pl.BlockSpec((rs, n), _const_map),
            pl.BlockSpec((None, 1, n), lambda i, k: (g_layer, 0, 0)),
        ],
        out_specs=out_specs,
        out_shape=out_shape,
        scratch_shapes=[pltpu.VMEM((tm, k_left), BF16), pltpu.VMEM((rs, k_left), BF16),
                        pltpu.VMEM((k_left, n), BF16)],
        compiler_params=pltpu.CompilerParams(
            dimension_semantics=("arbitrary", "arbitrary"),
            vmem_limit_bytes=_vmem_limit(blocks, _nbytes((tk, n), BF16))),
        name="ffn_down",
    )(ap, as_, w, xp, xs, g)


def _mem_kv_kernel(x_ref, g_ref, w_ref, k_ref, v_ref):
    h = _rmsnorm(x_ref[...], g_ref[...]).astype(BF16)
    kv = _dot(h, w_ref[...].astype(BF16))
    k_ref[...] = kv[:, :MEM_WIDTH]
    v_ref[...] = kv[:, MEM_WIDTH:]


def _mem_kv(mem, g, w):
    rows, d = mem.shape
    depth = w.shape[0]
    tm = 512
    out = jax.ShapeDtypeStruct((depth, rows, MEM_WIDTH), F32)
    blocks = (_nbytes((tm, d), F32) + _nbytes((d, 2 * MEM_WIDTH), F32) + 2 * _nbytes((tm, MEM_WIDTH), F32))
    return pl.pallas_call(
        _mem_kv_kernel,
        grid=(depth, rows // tm),
        in_specs=[
            pl.BlockSpec((tm, d), lambda l, i: (i, 0)),
            pl.BlockSpec((None, 1, d), lambda l, i: (l, 0, 0)),
            pl.BlockSpec((None, d, 2 * MEM_WIDTH), lambda l, i: (l, 0, 0)),
        ],
        out_specs=[
            pl.BlockSpec((None, tm, MEM_WIDTH), lambda l, i: (l, i, 0)),
            pl.BlockSpec((None, tm, MEM_WIDTH), lambda l, i: (l, i, 0)),
        ],
        out_shape=[out, out],
        compiler_params=pltpu.CompilerParams(
            dimension_semantics=("arbitrary", "arbitrary"),
            vmem_limit_bytes=_vmem_limit(blocks, _nbytes((d, 2 * MEM_WIDTH), BF16))),
        name="mem_kv",
    )(mem, g, w)


def _cross_attention_head(q, k, v):
    s = _dot_nt(q, k) * (MEM_HEAD_DIM ** -0.5)
    p = _softmax_rows(s).astype(BF16)
    return _dot(p, v)


def _conv_prompt_kernel(z_ref, mk_ref, mv_ref, cw_ref, wd_ref, mix_ref, st_ref, wdb_ref, ext_ref, *, tiles_per_seq):
    s = pl.program_id(0)
    tq = z_ref.shape[0]
    wdb_ref[...] = wd_ref[...].astype(BF16)

    @pl.when(s % tiles_per_seq == 0)
    def _():
        ext_ref[0:CARRY_ROWS, :] = jnp.zeros((CARRY_ROWS, TOKEN_WIDTH), F32)

    c = z_ref[:, TOKEN_WIDTH:2 * TOKEN_WIDTH].astype(F32)
    u = z_ref[:, 2 * TOKEN_WIDTH:3 * TOKEN_WIDTH].astype(F32)
    cu = c * u
    ext_ref[CARRY_ROWS:CARRY_ROWS + tq, :] = cu
    conv = (cw_ref[0:1, :] * ext_ref[CARRY_ROWS - 2:CARRY_ROWS - 2 + tq, :]
            + cw_ref[1:2, :] * ext_ref[CARRY_ROWS - 1:CARRY_ROWS - 1 + tq, :]
            + cw_ref[2:3, :] * cu)
    b = z_ref[:, 0:TOKEN_WIDTH].astype(F32)
    mix_ref[:, 0:TOKEN_WIDTH] = (b * conv).astype(BF16)
    st_ref[...] = ext_ref[CARRY_ROWS + tq - 2:CARRY_ROWS + tq, :]
    ext_ref[0:CARRY_ROWS, :] = ext_ref[tq:tq + CARRY_ROWS, :]

    for h in range(MEM_HEADS):
        lo, hi = h * MEM_HEAD_DIM, (h + 1) * MEM_HEAD_DIM
        q = z_ref[:, 3 * TOKEN_WIDTH + lo:3 * TOKEN_WIDTH + hi]
        o = _cross_attention_head(q, mk_ref[:, lo:hi].astype(BF16), mv_ref[:, lo:hi].astype(BF16))
        mix_ref[:, TOKEN_WIDTH + lo:TOKEN_WIDTH + hi] = o.astype(BF16)


def _conv_prompt(z, mk, mv, conv_w, w_down, *, layer, batch, seq):
    rows, zc = z.shape
    tq = CONV_ROWS
    tiles_per_seq = seq // tq
    n_steps = batch * tiles_per_seq
    d_ff, d = w_down.shape[1:]
    slab = d_ff // n_steps
    assert slab * n_steps == d_ff and slab % (2 * V7X_SUBLANES) == 0
    blocks = (_nbytes((tq, zc), BF16) + 2 * _nbytes((N_MEM, MEM_WIDTH), F32) + _nbytes((tq, D_MODEL), BF16)
              + _nbytes((slab, d), F32) + _nbytes((slab, d), BF16))
    scratch = _nbytes((tq + CARRY_ROWS, TOKEN_WIDTH), F32)
    return pl.pallas_call(
        functools.partial(_conv_prompt_kernel, tiles_per_seq=tiles_per_seq),
        grid=(n_steps,),
        in_specs=[
            pl.BlockSpec((tq, zc), lambda s: (s, 0)),
            pl.BlockSpec((None, N_MEM, MEM_WIDTH), lambda s: (layer, s // tiles_per_seq, 0)),
            pl.BlockSpec((None, N_MEM, MEM_WIDTH), lambda s: (layer, s // tiles_per_seq, 0)),
            pl.BlockSpec((None, CONV_WIDTH, TOKEN_WIDTH), lambda s: (0, 0, 0)),
            pl.BlockSpec((None, slab, d), lambda s: (layer, s, 0)),
        ],
        out_specs=[
            pl.BlockSpec((tq, D_MODEL), lambda s: (s, 0)),
            pl.BlockSpec((None, CONV_WIDTH - 1, TOKEN_WIDTH), lambda s: (s // tiles_per_seq, 0, 0)),
            pl.BlockSpec((slab, d), lambda s: (s, 0)),
        ],
        out_shape=[
            jax.ShapeDtypeStruct((rows, D_MODEL), BF16),
            jax.ShapeDtypeStruct((batch, CONV_WIDTH - 1, TOKEN_WIDTH), F32),
            jax.ShapeDtypeStruct((d_ff, d), BF16),
        ],
        scratch_shapes=[pltpu.VMEM((tq + CARRY_ROWS, TOKEN_WIDTH), F32)],
        compiler_params=pltpu.CompilerParams(
            dimension_semantics=("arbitrary",),
            vmem_limit_bytes=_vmem_limit(blocks, scratch + 6 * _nbytes((tq, TOKEN_WIDTH), F32))),
        name="conv_prompt",
    )(z, mk, mv, conv_w, w_down)


def _conv_sample_kernel(z_ref, st_ref, mk_ref, mv_ref, cw_ref, wo_ref, mix_ref, nst_ref, wob_ref,
                        ext_ref, mixf_ref, *, dec_seq):
    t = dec_seq
    wob_ref[...] = wo_ref[...].astype(BF16)
    for n in range(SAMPLE_GROUP):
        r0, r1 = n * t, (n + 1) * t
        c = z_ref[r0:r1, TOKEN_WIDTH:2 * TOKEN_WIDTH].astype(F32)
        u = z_ref[r0:r1, 2 * TOKEN_WIDTH:3 * TOKEN_WIDTH].astype(F32)
        cu = c * u
        ext_ref[CARRY_ROWS - 2:CARRY_ROWS, :] = st_ref[n]
        ext_ref[CARRY_ROWS:CARRY_ROWS + t, :] = cu
        conv = (cw_ref[0:1, :] * ext_ref[CARRY_ROWS - 2:CARRY_ROWS - 2 + t, :]
                + cw_ref[1:2, :] * ext_ref[CARRY_ROWS - 1:CARRY_ROWS - 1 + t, :]
                + cw_ref[2:3, :] * cu)
        b = z_ref[r0:r1, 0:TOKEN_WIDTH].astype(F32)
        mixf_ref[r0:r1, 0:TOKEN_WIDTH] = b * conv
        nst_ref[n] = ext_ref[CARRY_ROWS + t - 2:CARRY_ROWS + t, :]
        for h in range(MEM_HEADS):
            lo, hi = h * MEM_HEAD_DIM, (h + 1) * MEM_HEAD_DIM
            q = z_ref[r0:r1, 3 * TOKEN_WIDTH + lo:3 * TOKEN_WIDTH + hi]
            o = _cross_attention_head(q, mk_ref[n, :, h, :].astype(BF16), mv_ref[n, :, h, :].astype(BF16))
            mixf_ref[r0:r1, TOKEN_WIDTH + lo:TOKEN_WIDTH + hi] = o
    mix_ref[...] = mixf_ref[...].astype(BF16)


def _conv_sample(z, state, mem_k, mem_v, conv_w, w_out, *, layer, dec_seq):
    rows, zc = z.shape
    dec_batch = state.shape[1]
    g = SAMPLE_GROUP
    gr = g * dec_seq
    n_steps = dec_batch // g
    wk, wn = w_out.shape[1:]
    slab = wk // n_steps
    assert slab * n_steps == wk and slab % (2 * V7X_SUBLANES) == 0
    blocks = (_nbytes((gr, zc), BF16) + 2 * _nbytes((g, CONV_WIDTH - 1, TOKEN_WIDTH), F32)
              + 2 * _nbytes((g, N_MEM, V7X_SUBLANES, MEM_HEAD_DIM), F32) + _nbytes((gr, D_MODEL), BF16)
              + _nbytes((slab, wn), F32) + _nbytes((slab, wn), BF16))
    scratch = _nbytes((2 * CARRY_ROWS, TOKEN_WIDTH), F32) + _nbytes((gr, D_MODEL), F32)
    return pl.pallas_call(
        functools.partial(_conv_sample_kernel, dec_seq=dec_seq),
        grid=(n_steps,),
        in_specs=[
            pl.BlockSpec((gr, zc), lambda i: (i, 0)),
            pl.BlockSpec((None, g, CONV_WIDTH - 1, TOKEN_WIDTH), lambda i: (0, i, 0, 0)),
            pl.BlockSpec((None, g, N_MEM, MEM_HEADS, MEM_HEAD_DIM), lambda i: (layer, i, 0, 0, 0)),
            pl.BlockSpec((None, g, N_MEM, MEM_HEADS, MEM_HEAD_DIM), lambda i: (layer, i, 0, 0, 0)),
            pl.BlockSpec((None, CONV_WIDTH, TOKEN_WIDTH), lambda i: (0, 0, 0)),
            pl.BlockSpec((None, slab, wn), lambda i: (0, i, 0)),
        ],
        out_specs=[
            pl.BlockSpec((gr, D_MODEL), lambda i: (i, 0)),
            pl.BlockSpec((g, CONV_WIDTH - 1, TOKEN_WIDTH), lambda i: (i, 0, 0)),
            pl.BlockSpec((slab, wn), lambda i: (i, 0)),
        ],
        out_shape=[
            jax.ShapeDtypeStruct((rows, D_MODEL), BF16),
            jax.ShapeDtypeStruct((dec_batch, CONV_WIDTH - 1, TOKEN_WIDTH), F32),
            jax.ShapeDtypeStruct((wk, wn), BF16),
        ],
        scratch_shapes=[pltpu.VMEM((2 * CARRY_ROWS, TOKEN_WIDTH), F32), pltpu.VMEM((gr, D_MODEL), F32)],
        compiler_params=pltpu.CompilerParams(
            dimension_semantics=("arbitrary",),
            vmem_limit_bytes=_vmem_limit(blocks, scratch)),
        name="conv_sample",
    )(z, state, mem_k, mem_v, conv_w, w_out)


def _band_attention_head(q, k, v, sink, upper, upper_visible):
    return _band_values(_band_probs(_band_scores(q, k), sink, upper, upper_visible), v)


def _band_scores(q, k):
    return _dot_nt(q * (HEAD_DIM ** -0.5), k)


def _band_probs(s, sink, upper, upper_visible):
    s = jnp.where(upper_visible, s[:, :WINDOW], jnp.where(upper, -jnp.inf, s[:, WINDOW:]))
    m = jnp.maximum(jnp.max(s, axis=-1, keepdims=True), sink)
    e = jnp.exp(s - m)
    p = e / (jnp.sum(e, axis=-1, keepdims=True) + jnp.exp(sink - m))
    return jnp.concatenate([jnp.where(upper, p, 0.0), jnp.where(upper, 0.0, p)], axis=1).astype(BF16)


def _band_values(p, v):
    return _dot(p, v)


def _swa_prompt_kernel(sink_ref, zq_ref, zp_ref, mk_ref, mv_ref, *refs, blocks_per_seq, n_cast_steps):
    n_w = (len(refs) - 1) // 2
    w_refs, mix_ref, wb_refs = refs[:n_w], refs[n_w], refs[n_w + 1:]
    s = pl.program_id(0)

    @pl.when(s < n_cast_steps)
    def _():
        for w_ref, wb_ref in zip(w_refs, wb_refs):
            wb_ref[...] = w_ref[...].astype(BF16)

    k_off = TOKEN_WIDTH
    v_off = TOKEN_WIDTH + KV_WIDTH
    qm_off = TOKEN_WIDTH + 2 * KV_WIDTH
    has_prev = (s % blocks_per_seq) > 0
    row = lax.broadcasted_iota(jnp.int32, (WINDOW, WINDOW), 0)
    col = lax.broadcasted_iota(jnp.int32, (WINDOW, WINDOW), 1)
    upper = col > row
    upper_visible = jnp.logical_and(upper, has_prev)
    @functools.lru_cache(maxsize=None)
    def keys(kh):
        lo, hi = kh * HEAD_DIM, (kh + 1) * HEAD_DIM
        return jnp.concatenate([zp_ref[:, lo:hi], zq_ref[:, k_off + lo:k_off + hi]], axis=0)

    @functools.lru_cache(maxsize=None)
    def values(kh):
        lo, hi = kh * HEAD_DIM, (kh + 1) * HEAD_DIM
        return jnp.concatenate([zp_ref[:, KV_WIDTH + lo:KV_WIDTH + hi], zq_ref[:, v_off + lo:v_off + hi]], axis=0)

    scores = {}
    for h in range(N_HEADS + SCORE_LOOKAHEAD):
        if h < N_HEADS:
            scores[h] = _band_scores(zq_ref[:, h * HEAD_DIM:(h + 1) * HEAD_DIM], keys(h // GROUP))
        if h >= SCORE_LOOKAHEAD:
            hd = h - SCORE_LOOKAHEAD
            p = _band_probs(scores.pop(hd), sink_ref[hd], upper, upper_visible)
            mix_ref[:, hd * HEAD_DIM:(hd + 1) * HEAD_DIM] = _band_values(p, values(hd // GROUP)).astype(BF16)
    for h in range(MEM_HEADS):
        lo, hi = h * MEM_HEAD_DIM, (h + 1) * MEM_HEAD_DIM
        q = zq_ref[:, qm_off + lo:qm_off + hi]
        o = _cross_attention_head(q, mk_ref[:, lo:hi].astype(BF16), mv_ref[:, lo:hi].astype(BF16))
        mix_ref[:, TOKEN_WIDTH + lo:TOKEN_WIDTH + hi] = o.astype(BF16)


def _swa_prompt(z, mk, mv, sinks, weights, *, layer, batch, seq):
    rows, zc = z.shape
    tq = WINDOW
    blocks_per_seq = seq // tq
    n_steps = batch * blocks_per_seq
    kv_col_block = TOKEN_WIDTH // (2 * KV_WIDTH)
    n_cast_steps = n_steps // 2
    slab_map = lambda s: (jnp.minimum(s, n_cast_steps - 1), 0)
    blocks = (_nbytes((tq, zc), BF16) + _nbytes((tq, 2 * KV_WIDTH), BF16)
              + 2 * _nbytes((N_MEM, MEM_WIDTH), F32) + _nbytes((tq, D_MODEL), BF16))
    w_in_specs, w_out_specs, w_out_shapes = [], [], []
    for w in weights:
        r, c = w.shape[1:]
        slab = r // n_cast_steps
        assert slab * n_cast_steps == r and slab % (2 * V7X_SUBLANES) == 0
        blocks += _nbytes((slab, c), F32) + _nbytes((slab, c), BF16)
        w_in_specs.append(pl.BlockSpec((None, slab, c), lambda s: (layer, jnp.minimum(s, n_cast_steps - 1), 0)))
        w_out_specs.append(pl.BlockSpec((slab, c), slab_map))
        w_out_shapes.append(jax.ShapeDtypeStruct((r, c), BF16))
    return pl.pallas_call(
        functools.partial(_swa_prompt_kernel, blocks_per_seq=blocks_per_seq, n_cast_steps=n_cast_steps),
        grid=(n_steps,),
        in_specs=[
            pl.BlockSpec(memory_space=pltpu.SMEM),
            pl.BlockSpec((tq, zc), lambda s: (s, 0)),
            pl.BlockSpec((tq, 2 * KV_WIDTH), lambda s: (jnp.maximum(s - 1, 0), kv_col_block)),
            pl.BlockSpec((None, N_MEM, MEM_WIDTH), lambda s: (layer, s // blocks_per_seq, 0)),
            pl.BlockSpec((None, N_MEM, MEM_WIDTH), lambda s: (layer, s // blocks_per_seq, 0)),
        ] + w_in_specs,
        out_specs=[pl.BlockSpec((tq, D_MODEL), lambda s: (s, 0))] + w_out_specs,
        out_shape=[jax.ShapeDtypeStruct((rows, D_MODEL), BF16)] + w_out_shapes,
        compiler_params=pltpu.CompilerParams(
            dimension_semantics=("arbitrary",),
            vmem_limit_bytes=_vmem_limit(blocks, 0)),
        name="swa_prompt",
    )(sinks, z, z, mk, mv, *weights)


def _swa_sample_kernel(sink_ref, z_ref, ck_ref, cv_ref, mk_ref, mv_ref, wo_ref,
                       mix_ref, nk_ref, nv_ref, wob_ref, knew_ref, vnew_ref, mixf_ref, *, dec_seq):
    t = dec_seq
    wob_ref[...] = wo_ref[...].astype(BF16)
    k_off = TOKEN_WIDTH
    v_off = TOKEN_WIDTH + KV_WIDTH
    qm_off = TOKEN_WIDTH + 2 * KV_WIDTH
    rows = GROUP * t
    qi = lax.broadcasted_iota(jnp.int32, (rows, WINDOW), 0) % t
    col = lax.broadcasted_iota(jnp.int32, (rows, WINDOW), 1)
    upper = col > qi
    knew_ref[...] = jnp.zeros((WINDOW, KV_WIDTH), F32)
    vnew_ref[...] = jnp.zeros((WINDOW, KV_WIDTH), F32)
    for n in range(SAMPLE_GROUP):
        r0, r1 = n * t, (n + 1) * t
        k_new = z_ref[r0:r1, k_off:k_off + KV_WIDTH].astype(F32)
        v_new = z_ref[r0:r1, v_off:v_off + KV_WIDTH].astype(F32)
        knew_ref[0:t, :] = k_new
        vnew_ref[0:t, :] = v_new
        nk_ref[n, 0:WINDOW - t, :] = ck_ref[n, t:WINDOW, :]
        nv_ref[n, 0:WINDOW - t, :] = cv_ref[n, t:WINDOW, :]
        nk_ref[n, WINDOW - t:WINDOW, :] = k_new
        nv_ref[n, WINDOW - t:WINDOW, :] = v_new
        for kh in range(N_KV_HEADS):
            lo, hi = kh * HEAD_DIM, (kh + 1) * HEAD_DIM
            k = jnp.concatenate([ck_ref[n, :, lo:hi], knew_ref[:, lo:hi]], axis=0).astype(BF16)
            v = jnp.concatenate([cv_ref[n, :, lo:hi], vnew_ref[:, lo:hi]], axis=0).astype(BF16)
            q = jnp.concatenate(
                [z_ref[r0:r1, (kh * GROUP + g) * HEAD_DIM:(kh * GROUP + g + 1) * HEAD_DIM].astype(F32)
                 for g in range(GROUP)], axis=0).astype(BF16)
            sink = jnp.concatenate(
                [jnp.full((t, 1), sink_ref[kh * GROUP + g], F32) for g in range(GROUP)], axis=0)
            o = _band_attention_head(q, k, v, sink, upper, upper)
            for g in range(GROUP):
                h = kh * GROUP + g
                mixf_ref[r0:r1, h * HEAD_DIM:(h + 1) * HEAD_DIM] = o[g * t:(g + 1) * t, :]
        for h in range(MEM_HEADS):
            lo, hi = h * MEM_HEAD_DIM, (h + 1) * MEM_HEAD_DIM
            q = z_ref[r0:r1, qm_off + lo:qm_off + hi]
            o = _cross_attention_head(q, mk_ref[n, :, h, :].astype(BF16), mv_ref[n, :, h, :].astype(BF16))
            mixf_ref[r0:r1, TOKEN_WIDTH + lo:TOKEN_WIDTH + hi] = o
    mix_ref[...] = mixf_ref[...].astype(BF16)


def _swa_sample(z, cache_k, cache_v, mem_k, mem_v, sinks, w_out, *, layer, dec_seq):
    rows, zc = z.shape
    dec_batch = cache_k.shape[0]
    g = SAMPLE_GROUP
    gr = g * dec_seq
    n_steps = dec_batch // g
    wk, wn = w_out.shape[1:]
    slab = wk // n_steps
    assert slab * n_steps == wk and slab % (2 * V7X_SUBLANES) == 0
    win = jax.ShapeDtypeStruct((dec_batch, WINDOW, KV_WIDTH), F32)
    blocks = (_nbytes((gr, zc), BF16) + 4 * _nbytes((g, WINDOW, KV_WIDTH), F32)
              + 2 * _nbytes((g, N_MEM, V7X_SUBLANES, MEM_HEAD_DIM), F32) + _nbytes((gr, D_MODEL), BF16)
              + _nbytes((slab, wn), F32) + _nbytes((slab, wn), BF16))
    scratch = 2 * _nbytes((WINDOW, KV_WIDTH), F32) + _nbytes((gr, D_MODEL), F32)
    return pl.pallas_call(
        functools.partial(_swa_sample_kernel, dec_seq=dec_seq),
        grid=(n_steps,),
        in_specs=[
            pl.BlockSpec(memory_space=pltpu.SMEM),
            pl.BlockSpec((gr, zc), lambda i: (i, 0)),
            pl.BlockSpec((g, WINDOW, KV_WIDTH), lambda i: (i, 0, 0)),
            pl.BlockSpec((g, WINDOW, KV_WIDTH), lambda i: (i, 0, 0)),
            pl.BlockSpec((None, g, N_MEM, MEM_HEADS, MEM_HEAD_DIM), lambda i: (layer, i, 0, 0, 0)),
            pl.BlockSpec((None, g, N_MEM, MEM_HEADS, MEM_HEAD_DIM), lambda i: (layer, i, 0, 0, 0)),
            pl.BlockSpec((None, slab, wn), lambda i: (0, i, 0)),
        ],
        out_specs=[
            pl.BlockSpec((gr, D_MODEL), lambda i: (i, 0)),
            pl.BlockSpec((g, WINDOW, KV_WIDTH), lambda i: (i, 0, 0)),
            pl.BlockSpec((g, WINDOW, KV_WIDTH), lambda i: (i, 0, 0)),
            pl.BlockSpec((slab, wn), lambda i: (i, 0)),
        ],
        out_shape=[jax.ShapeDtypeStruct((rows, D_MODEL), BF16), win, win,
                   jax.ShapeDtypeStruct((wk, wn), BF16)],
        scratch_shapes=[pltpu.VMEM((WINDOW, KV_WIDTH), F32), pltpu.VMEM((WINDOW, KV_WIDTH), F32),
                        pltpu.VMEM((gr, D_MODEL), F32)],
        compiler_params=pltpu.CompilerParams(
            dimension_semantics=("arbitrary",),
            vmem_limit_bytes=_vmem_limit(blocks, scratch)),
        name="swa_sample",
    )(sinks, z, cache_k, cache_v, mem_k, mem_v, w_out)


def kernel(x_prompt, x_sample, mem_prompt, state_conv, cache_win_k, cache_win_v, cache_mem_k, cache_mem_v,
           norm_mix, norm_mem, w_mem_kv, norm_ffn, w_gate, w_up, w_down,
           conv_w_in, conv_w, conv_w_out, attn_w_in, attn_sinks, attn_w_out, norm_final):
    batch, seq, d = x_prompt.shape
    dec_batch, dec_seq, _ = x_sample.shape
    depth = norm_mix.shape[0]
    d_ff = w_gate.shape[2]
    prompt_rows = batch * seq
    sample_rows = dec_batch * dec_seq
    assert d == D_MODEL and depth == 2 and seq % CONV_ROWS == 0 and seq % WINDOW == 0
    assert prompt_rows % PROMPT_TILE == 0 and dec_batch % SAMPLE_GROUP == 0
    assert dec_seq == V7X_SUBLANES and d_ff % FFN_COL_TILE == 0 and d % COL_TILE == 0
    assert d_ff % DOWN_K_TILE == 0 and d % DOWN_X_CHUNK == 0
    assert prompt_rows % FFN_ROW_TILE == 0 and prompt_rows % sample_rows == 0

    xp = x_prompt.reshape(prompt_rows, d)
    xs = x_sample.reshape(sample_rows, d)
    mem = mem_prompt.reshape(batch * N_MEM, d)
    mem_k_s, mem_v_s = cache_mem_k, cache_mem_v
    g_mix = norm_mix.reshape(depth, 1, d)
    g_ffn = norm_ffn.reshape(depth, 1, d)

    mk, mv = _mem_kv(mem, norm_mem.reshape(depth, 1, d), w_mem_kv)

    hp, hs = _norm_rows(xp, xs, g_mix, layer=0)
    zp, zs = _matmul(hp, hs, conv_w_in, w_layer=0)
    mix_s, conv_s, wo_bf = _conv_sample(zs, state_conv, mem_k_s, mem_v_s, conv_w, conv_w_out,
                                        layer=0, dec_seq=dec_seq)
    mix_p, conv_p, wd_bf = _conv_prompt(zp, mk, mv, conv_w, w_down, layer=0, batch=batch, seq=seq)
    xp, xs, hp, hs = _out_proj(mix_p, mix_s, wo_bf, xp, xs, g_ffn, layer=0)
    ap, as_ = _ffn_up(hp, hs, w_gate, w_up, layer=0)
    xp, xs, hp, hs = _ffn_down(ap, as_, wd_bf[None], xp, xs, g_mix, w_layer=0, g_layer=1, final=False)

    zp, zs = _matmul(hp, hs, attn_w_in, w_layer=0)
    sinks = attn_sinks[0]
    mix_s, win_k_s, win_v_s, wo_bf = _swa_sample(
        zs, cache_win_k[0].reshape(dec_batch, WINDOW, KV_WIDTH), cache_win_v[0].reshape(dec_batch, WINDOW, KV_WIDTH),
        mem_k_s, mem_v_s, sinks, attn_w_out, layer=1, dec_seq=dec_seq)
    mix_p, wd_bf, wg_bf, wu_bf = _swa_prompt(zp, mk, mv, sinks, [w_down, w_gate, w_up],
                                             layer=1, batch=batch, seq=seq)
    xp, xs, hp, hs = _out_proj(mix_p, mix_s, wo_bf, xp, xs, g_ffn, layer=1)
    ap, as_ = _ffn_up(hp, hs, wg_bf[None], wu_bf[None], layer=0)
    y_prompt, y_sample = _ffn_down(ap, as_, wd_bf[None], xp, xs, norm_final.reshape(1, 1, d),
                                   w_layer=0, g_layer=0, final=True)

    win_p = zp.reshape(batch, seq, -1)[:, seq - WINDOW:, TOKEN_WIDTH:TOKEN_WIDTH + 2 * KV_WIDTH].astype(F32)
    kv_shape = (1, -1, WINDOW, N_KV_HEADS, HEAD_DIM)
    mem_shape = (depth, batch, N_MEM, MEM_HEADS, MEM_HEAD_DIM)
    return (y_prompt.reshape(batch, seq, d),
            y_sample.reshape(dec_batch, dec_seq, d),
            conv_p[None],
            conv_s[None],
            win_p[..., :KV_WIDTH].reshape(kv_shape),
            win_p[..., KV_WIDTH:].reshape(kv_shape),
            win_k_s.reshape(kv_shape),
            win_v_s.reshape(kv_shape),
            mk.reshape(mem_shape),
            mv.reshape(mem_shape))
```

```python
import functools

import jax
import jax.numpy as jnp
from jax import lax
from jax.experimental import pallas as pl
from jax.experimental.pallas import tpu as pltpu

F32 = jnp.float32
BF16 = jnp.bfloat16

D_MODEL = 2048
N_MEM = 256
MEM_HEADS = 4
MEM_WIDTH = D_MODEL // 4
MEM_HEAD_DIM = MEM_WIDTH // MEM_HEADS
TOKEN_WIDTH = D_MODEL - MEM_WIDTH
CONV_WIDTH = 3
WINDOW = 128
HEAD_DIM = 64
N_HEADS = TOKEN_WIDTH // HEAD_DIM
N_KV_HEADS = 4
GROUP = N_HEADS // N_KV_HEADS
KV_WIDTH = N_KV_HEADS * HEAD_DIM
EPS = 1e-6

V7X_VMEM_BYTES = 64 * 1024 * 1024
V7X_SUBLANES = 8
V7X_MXU_DEPTH = 256

PROMPT_TILE = 1024
COL_TILE = 512
DOWN_K_TILE = 1408
DOWN_X_CHUNK = 512
FFN_ROW_TILE = 2048
FFN_COL_TILE = 512
MXU_ROWS = 1024
OUT_ROWS = 512
CONV_ROWS = 512
SCORE_LOOKAHEAD = 1
SAMPLE_LOOKAHEAD = 4
SAMPLE_GROUP = 8
CARRY_ROWS = V7X_SUBLANES


def _nbytes(shape, dtype):
    n = 1
    for s in shape:
        n *= s
    return n * jnp.dtype(dtype).itemsize


def _vmem_limit(block_bytes, scratch_bytes):
    need = 2 * block_bytes + scratch_bytes
    return int(min(need + max(need // 4, 8 << 20), V7X_VMEM_BYTES - (6 << 20)))


def _rmsnorm(x, g):
    r = lax.rsqrt(jnp.mean(x * x, axis=-1, keepdims=True) + EPS)
    return (x * r) * g


def _softmax_rows(s):
    m = jnp.max(s, axis=-1, keepdims=True)
    e = jnp.exp(s - m)
    return e / jnp.sum(e, axis=-1, keepdims=True)


def _dot(a, b):
    return jnp.dot(a, b, preferred_element_type=F32)


def _dot_nt(a, b):
    return lax.dot_general(a, b, (((1,), (1,)), ((), ())), preferred_element_type=F32)


def _prompt_rows_map(i, j):
    return (i, 0)


def _prompt_tile_map(i, j):
    return (i, j)


def _sample_tile_map(n_pt):
    return lambda i, j: (0, jnp.where(i == n_pt - 1, j, 0))


def _const_map(i, j):
    return (0, 0)


def _norm_rows_kernel(xp_ref, xs_ref, g_ref, hp_ref, hs_ref, *, n_pt):
    hp_ref[...] = _rmsnorm(xp_ref[...], g_ref[...]).astype(BF16)

    @pl.when(pl.program_id(0) == n_pt - 1)
    def _():
        hs_ref[...] = _rmsnorm(xs_ref[...], g_ref[...]).astype(BF16)


def _norm_rows(xp, xs, g, *, layer):
    rp, d = xp.shape
    rs = xs.shape[0]
    tm = OUT_ROWS
    n_pt = rp // tm
    blocks = _nbytes((tm, d), F32) + _nbytes((tm, d), BF16) + _nbytes((rs, d), F32) + _nbytes((rs, d), BF16)
    return pl.pallas_call(
        functools.partial(_norm_rows_kernel, n_pt=n_pt),
        grid=(n_pt,),
        in_specs=[
            pl.BlockSpec((tm, d), lambda s: (s, 0)),
            pl.BlockSpec((rs, d), lambda s: (0, 0)),
            pl.BlockSpec((None, 1, d), lambda s: (layer, 0, 0)),
        ],
        out_specs=[
            pl.BlockSpec((tm, d), lambda s: (s, 0)),
            pl.BlockSpec((rs, d), lambda s: (0, 0)),
        ],
        out_shape=[jax.ShapeDtypeStruct((rp, d), BF16), jax.ShapeDtypeStruct((rs, d), BF16)],
        compiler_params=pltpu.CompilerParams(
            dimension_semantics=("arbitrary",),
            vmem_limit_bytes=_vmem_limit(blocks, 0)),
        name="norm_rows",
    )(xp, xs, g)


def _swiglu(h, wg, wu):
    gate = _dot(h, wg)
    up = _dot(h, wu)
    return (gate * jax.nn.sigmoid(gate) * up).astype(BF16)


def _ffn_up_kernel(hp_ref, hs_ref, wg_ref, wu_ref, ap_ref, as_ref, *, n_pt):
    wg, wu = wg_ref[...].astype(BF16), wu_ref[...].astype(BF16)
    for r in range(0, hp_ref.shape[0], MXU_ROWS):
        ap_ref[r:r + MXU_ROWS, :] = _swiglu(hp_ref[r:r + MXU_ROWS, :], wg, wu)

    @pl.when(pl.program_id(0) == n_pt - 1)
    def _():
        as_ref[...] = _swiglu(hs_ref[...], wg_ref[...].astype(BF16), wu_ref[...].astype(BF16))


def _ffn_up(hp, hs, wg, wu, *, layer):
    rp, d = hp.shape
    rs = hs.shape[0]
    n = wg.shape[2]
    tm, tn = FFN_ROW_TILE, FFN_COL_TILE
    n_pt, n_j = rp // tm, n // tn
    blocks = (_nbytes((tm, d), BF16) + _nbytes((rs, d), BF16) + 2 * _nbytes((d, tn), wg.dtype)
              + _nbytes((tm, tn), BF16) + _nbytes((rs, tn), BF16))
    temps = 2 * _nbytes((d, tn), BF16) + 3 * _nbytes((tm, tn), F32)
    w_spec = pl.BlockSpec((None, d, tn), lambda i, j: (layer, 0, j))
    return pl.pallas_call(
        functools.partial(_ffn_up_kernel, n_pt=n_pt),
        grid=(n_pt, n_j),
        in_specs=[
            pl.BlockSpec((tm, d), _prompt_rows_map),
            pl.BlockSpec((rs, d), _const_map),
            w_spec,
            w_spec,
        ],
        out_specs=[
            pl.BlockSpec((tm, tn), _prompt_tile_map),
            pl.BlockSpec((rs, tn), _sample_tile_map(n_pt)),
        ],
        out_shape=[jax.ShapeDtypeStruct((rp, n), BF16), jax.ShapeDtypeStruct((rs, n), BF16)],
        compiler_params=pltpu.CompilerParams(
            dimension_semantics=("arbitrary", "arbitrary"),
            vmem_limit_bytes=_vmem_limit(blocks, temps)),
        name="ffn_up",
    )(hp, hs, wg, wu)


def _out_proj_kernel(ap_ref, as_ref, w_ref, xp_ref, xs_ref, g_ref, op_ref, os_ref, hp_ref, hs_ref, *, n_pt):
    x = xp_ref[...] + _dot(ap_ref[...], w_ref[...])
    op_ref[...] = x
    hp_ref[...] = _rmsnorm(x, g_ref[...]).astype(BF16)

    @pl.when(pl.program_id(0) == n_pt - 1)
    def _():
        x = xs_ref[...] + _dot(as_ref[...], w_ref[...])
        os_ref[...] = x
        hs_ref[...] = _rmsnorm(x, g_ref[...]).astype(BF16)


def _out_proj(ap, as_, w, xp, xs, g, *, layer):
    rp, k = ap.shape
    rs = as_.shape[0]
    n = w.shape[1]
    tm = OUT_ROWS
    n_pt = rp // tm
    rows_map = lambda s: (s, 0)
    const = lambda s: (0, 0)
    once = dict(pipeline_mode=pl.Buffered(1))
    blocks = _nbytes((tm, k), BF16) + 2 * _nbytes((tm, n), F32) + _nbytes((tm, n), BF16)
    resident = (_nbytes((k, n), BF16) + _nbytes((rs, k), BF16) + _nbytes((rs, n), F32)
                + 2 * (_nbytes((rs, n), F32) + _nbytes((rs, n), BF16)))
    return pl.pallas_call(
        functools.partial(_out_proj_kernel, n_pt=n_pt),
        grid=(n_pt,),
        in_specs=[
            pl.BlockSpec((tm, k), rows_map),
            pl.BlockSpec((rs, k), const, **once),
            pl.BlockSpec((k, n), const, **once),
            pl.BlockSpec((tm, n), rows_map),
            pl.BlockSpec((rs, n), const, **once),
            pl.BlockSpec((None, 1, n), lambda s: (layer, 0, 0)),
        ],
        out_specs=[
            pl.BlockSpec((tm, n), rows_map),
            pl.BlockSpec((rs, n), const),
            pl.BlockSpec((tm, n), rows_map),
            pl.BlockSpec((rs, n), const),
        ],
        out_shape=[jax.ShapeDtypeStruct((rp, n), F32), jax.ShapeDtypeStruct((rs, n), F32),
                   jax.ShapeDtypeStruct((rp, n), BF16), jax.ShapeDtypeStruct((rs, n), BF16)],
        compiler_params=pltpu.CompilerParams(
            dimension_semantics=("arbitrary",),
            vmem_limit_bytes=_vmem_limit(blocks, resident + 2 * _nbytes((tm, n), F32))),
        name="out_proj",
    )(ap, as_, w, xp, xs, g)


def _matmul_kernel(hp_ref, hs_ref, w_ref, zp_ref, zs_ref, *, n_pt):
    w = w_ref[...].astype(BF16)
    for r in range(0, hp_ref.shape[0], MXU_ROWS):
        zp_ref[r:r + MXU_ROWS, :] = _dot(hp_ref[r:r + MXU_ROWS, :], w).astype(zp_ref.dtype)

    @pl.when(pl.program_id(0) == n_pt - 1)
    def _():
        zs_ref[...] = _dot(hs_ref[...], w_ref[...].astype(BF16)).astype(zs_ref.dtype)


def _matmul(hp, hs, w, *, w_layer):
    rp, d = hp.shape
    rs = hs.shape[0]
    n = w.shape[2]
    tm, tn = FFN_ROW_TILE, COL_TILE
    n_pt, n_j = rp // tm, n // tn
    blocks = (_nbytes((tm, d), BF16) + _nbytes((rs, d), BF16) + _nbytes((d, tn), F32)
              + _nbytes((tm, tn), BF16) + _nbytes((rs, tn), BF16))
    temps = _nbytes((d, tn), BF16) + _nbytes((tm, tn), F32)
    return pl.pallas_call(
        functools.partial(_matmul_kernel, n_pt=n_pt),
        grid=(n_pt, n_j),
        in_specs=[
            pl.BlockSpec((tm, d), _prompt_rows_map),
            pl.BlockSpec((rs, d), _const_map),
            pl.BlockSpec((None, d, tn), lambda i, j: (w_layer, 0, j)),
        ],
        out_specs=[
            pl.BlockSpec((tm, tn), _prompt_tile_map),
            pl.BlockSpec((rs, tn), _sample_tile_map(n_pt)),
        ],
        out_shape=[jax.ShapeDtypeStruct((rp, n), BF16), jax.ShapeDtypeStruct((rs, n), BF16)],
        compiler_params=pltpu.CompilerParams(
            dimension_semantics=("arbitrary", "arbitrary"),
            vmem_limit_bytes=_vmem_limit(blocks, temps)),
        name="matmul",
    )(hp, hs, w)


def _ffn_down_kernel(ap_ref, as_ref, w_ref, xp_ref, xs_ref, g_ref, *refs, n_pt, n_k, n_xc, final):
    i, k = pl.program_id(0), pl.program_id(1)
    if final:
        op_ref, os_ref, apl_ref, asl_ref, wl_ref = refs
    else:
        op_ref, os_ref, hp_ref, hs_ref, apl_ref, asl_ref, wl_ref = refs
    xc = xp_ref.shape[1]
    kf = (ap_ref.shape[1] // V7X_MXU_DEPTH) * V7X_MXU_DEPTH
    even = k % 2 == 0

    def whole_passes(a_ref):
        return _dot(a_ref[:, :kf], w_ref[:kf, :])

    def with_stash(a_ref, al_ref):
        return _dot(jnp.concatenate([al_ref[...], a_ref[...]], axis=1),
                    jnp.concatenate([wl_ref[...], w_ref[...]], axis=0))

    @pl.when(k == 0)
    def _():
        op_ref[...] = whole_passes(ap_ref)

    @pl.when(jnp.logical_and(k > 0, even))
    def _():
        op_ref[...] = op_ref[...] + whole_passes(ap_ref)

    @pl.when(jnp.logical_not(even))
    def _():
        op_ref[...] = op_ref[...] + with_stash(ap_ref, apl_ref)

    for c in range(n_xc):
        @pl.when(k == c)
        def _():
            op_ref[:, c * xc:(c + 1) * xc] = op_ref[:, c * xc:(c + 1) * xc] + xp_ref[...]

    @pl.when(k == n_k - 1)
    def _():
        if final:
            op_ref[...] = _rmsnorm(op_ref[...], g_ref[...])
        else:
            hp_ref[...] = _rmsnorm(op_ref[...], g_ref[...]).astype(BF16)

    @pl.when(i == n_pt - 1)
    def _():
        @pl.when(k == 0)
        def _():
            os_ref[...] = xs_ref[...] + whole_passes(as_ref)

        @pl.when(jnp.logical_and(k > 0, even))
        def _():
            os_ref[...] = os_ref[...] + whole_passes(as_ref)

        @pl.when(jnp.logical_not(even))
        def _():
            os_ref[...] = os_ref[...] + with_stash(as_ref, asl_ref)

        @pl.when(even)
        def _():
            asl_ref[...] = as_ref[:, kf:]

        @pl.when(k == n_k - 1)
        def _():
            if final:
                os_ref[...] = _rmsnorm(os_ref[...], g_ref[...])
            else:
                hs_ref[...] = _rmsnorm(os_ref[...], g_ref[...]).astype(BF16)

    @pl.when(even)
    def _():
        apl_ref[...] = ap_ref[:, kf:]
        wl_ref[...] = w_ref[kf:, :]


def _ffn_down(ap, as_, w, xp, xs, g, *, w_layer, g_layer, final):
    rp, kdim = ap.shape
    rs = as_.shape[0]
    n = w.shape[2]
    tm, tk, xc = PROMPT_TILE, DOWN_K_TILE, DOWN_X_CHUNK
    n_pt, n_k, n_xc = rp // tm, kdim // tk, n // xc
    k_left = tk % V7X_MXU_DEPTH
    assert n_xc <= n_k and n_k % 2 == 0 and 2 * k_left == V7X_MXU_DEPTH and w.dtype == BF16
    rows = lambda i, k: (i, 0)
    blocks = (_nbytes((tm, tk), BF16) + _nbytes((rs, tk), BF16) + _nbytes((tk, n), w.dtype)
              + _nbytes((tm, xc), F32) + _nbytes((rs, n), F32)
              + _nbytes((tm, n), F32) + _nbytes((rs, n), F32))
    out_specs = [pl.BlockSpec((tm, n), rows), pl.BlockSpec((rs, n), _const_map)]
    out_shape = [jax.ShapeDtypeStruct((rp, n), F32), jax.ShapeDtypeStruct((rs, n), F32)]
    if not final:
        blocks += _nbytes((tm, n), BF16) + _nbytes((rs, n), BF16)
        out_specs += [pl.BlockSpec((tm, n), rows), pl.BlockSpec((rs, n), _const_map)]
        out_shape += [jax.ShapeDtypeStruct((rp, n), BF16), jax.ShapeDtypeStruct((rs, n), BF16)]
    return pl.pallas_call(
        functools.partial(_ffn_down_kernel, n_pt=n_pt, n_k=n_k, n_xc=n_xc, final=final),
        grid=(n_pt, n_k),
        in_specs=[
            pl.BlockSpec((tm, tk), lambda i, k: (i, k)),
            pl.BlockSpec((rs, tk), lambda i, k: (0, jnp.where(i == n_pt - 1, k, 0))),
            pl.BlockSpec((None, tk, n), lambda i, k: (w_layer, k, 0)),
            pl.BlockSpec((tm, xc), lambda i, k: (i, jnp.minimum(k, n_xc - 1))),
            pl.BlockSpec((rs, n), _const_map),
            pl.BlockSpec((None, 1, n), lambda i, k: (g_layer, 0, 0)),
        ],
        out_specs=out_specs,
        out_shape=out_shape,
        scratch_shapes=[pltpu.VMEM((tm, k_left), BF16), pltpu.VMEM((rs, k_left), BF16),
                        pltpu.VMEM((k_left, n), BF16)],
        compiler_params=pltpu.CompilerParams(
            dimension_semantics=("arbitrary", "arbitrary"),
            vmem_limit_bytes=_vmem_limit(blocks, _nbytes((tk, n), BF16))),
        name="ffn_down",
    )(ap, as_, w, xp, xs, g)


def _mem_kv_kernel(x_ref, g_ref, w_ref, k_ref, v_ref):
    h = _rmsnorm(x_ref[...], g_ref[...]).astype(BF16)
    kv = _dot(h, w_ref[...].astype(BF16))
    k_ref[...] = kv[:, :MEM_WIDTH]
    v_ref[...] = kv[:, MEM_WIDTH:]


def _mem_kv(mem, g, w):
    rows, d = mem.shape
    depth = w.shape[0]
    tm = 512
    out = jax.ShapeDtypeStruct((depth, rows, MEM_WIDTH), F32)
    blocks = (_nbytes((tm, d), F32) + _nbytes((d, 2 * MEM_WIDTH), F32) + 2 * _nbytes((tm, MEM_WIDTH), F32))
    return pl.pallas_call(
        _mem_kv_kernel,
        grid=(depth, rows // tm),
        in_specs=[
            pl.BlockSpec((tm, d), lambda l, i: (i, 0)),
            pl.BlockSpec((None, 1, d), lambda l, i: (l, 0, 0)),
            pl.BlockSpec((None, d, 2 * MEM_WIDTH), lambda l, i: (l, 0, 0)),
        ],
        out_specs=[
            pl.BlockSpec((None, tm, MEM_WIDTH), lambda l, i: (l, i, 0)),
            pl.BlockSpec((None, tm, MEM_WIDTH), lambda l, i: (l, i, 0)),
        ],
        out_shape=[out, out],
        compiler_params=pltpu.CompilerParams(
            dimension_semantics=("arbitrary", "arbitrary"),
            vmem_limit_bytes=_vmem_limit(blocks, _nbytes((d, 2 * MEM_WIDTH), BF16))),
        name="mem_kv",
    )(mem, g, w)


def _cross_scores(q, k):
    return _dot_nt(q, k) * (MEM_HEAD_DIM ** -0.5)


def _cross_values(s, v):
    return _dot(_softmax_rows(s).astype(BF16), v)


def _run_ahead(n_items, lookahead, first, second):
    pending = {}
    for i in range(n_items + lookahead):
        if i < n_items:
            pending[i] = first(i)
        if i >= lookahead:
            second(i - lookahead, pending.pop(i - lookahead))


def _conv_prompt_kernel(z_ref, mk_ref, mv_ref, cw_ref, wd_ref, mix_ref, st_ref, wdb_ref, ext_ref, *, tiles_per_seq):
    s = pl.program_id(0)
    tq = z_ref.shape[0]
    wdb_ref[...] = wd_ref[...].astype(BF16)

    @pl.when(s % tiles_per_seq == 0)
    def _():
        ext_ref[0:CARRY_ROWS, :] = jnp.zeros((CARRY_ROWS, TOKEN_WIDTH), F32)

    c = z_ref[:, TOKEN_WIDTH:2 * TOKEN_WIDTH].astype(F32)
    u = z_ref[:, 2 * TOKEN_WIDTH:3 * TOKEN_WIDTH].astype(F32)
    cu = c * u
    ext_ref[CARRY_ROWS:CARRY_ROWS + tq, :] = cu
    conv = (cw_ref[0:1, :] * ext_ref[CARRY_ROWS - 2:CARRY_ROWS - 2 + tq, :]
            + cw_ref[1:2, :] * ext_ref[CARRY_ROWS - 1:CARRY_ROWS - 1 + tq, :]
            + cw_ref[2:3, :] * cu)
    b = z_ref[:, 0:TOKEN_WIDTH].astype(F32)
    mix_ref[:, 0:TOKEN_WIDTH] = (b * conv).astype(BF16)
    st_ref[...] = ext_ref[CARRY_ROWS + tq - 2:CARRY_ROWS + tq, :]
    ext_ref[0:CARRY_ROWS, :] = ext_ref[tq:tq + CARRY_ROWS, :]

    def scores(h):
        lo, hi = h * MEM_HEAD_DIM, (h + 1) * MEM_HEAD_DIM
        return _cross_scores(z_ref[:, 3 * TOKEN_WIDTH + lo:3 * TOKEN_WIDTH + hi], mk_ref[:, lo:hi].astype(BF16))

    def finish(h, s):
        lo, hi = h * MEM_HEAD_DIM, (h + 1) * MEM_HEAD_DIM
        mix_ref[:, TOKEN_WIDTH + lo:TOKEN_WIDTH + hi] = _cross_values(s, mv_ref[:, lo:hi].astype(BF16)).astype(BF16)

    _run_ahead(MEM_HEADS, SCORE_LOOKAHEAD, scores, finish)


def _conv_prompt(z, mk, mv, conv_w, w_down, *, layer, batch, seq):
    rows, zc = z.shape
    tq = CONV_ROWS
    tiles_per_seq = seq // tq
    n_steps = batch * tiles_per_seq
    d_ff, d = w_down.shape[1:]
    slab = d_ff // n_steps
    assert slab * n_steps == d_ff and slab % (2 * V7X_SUBLANES) == 0
    blocks = (_nbytes((tq, zc), BF16) + 2 * _nbytes((N_MEM, MEM_WIDTH), F32) + _nbytes((tq, D_MODEL), BF16)
              + _nbytes((slab, d), F32) + _nbytes((slab, d), BF16))
    scratch = _nbytes((tq + CARRY_ROWS, TOKEN_WIDTH), F32)
    return pl.pallas_call(
        functools.partial(_conv_prompt_kernel, tiles_per_seq=tiles_per_seq),
        grid=(n_steps,),
        in_specs=[
            pl.BlockSpec((tq, zc), lambda s: (s, 0)),
            pl.BlockSpec((None, N_MEM, MEM_WIDTH), lambda s: (layer, s // tiles_per_seq, 0)),
            pl.BlockSpec((None, N_MEM, MEM_WIDTH), lambda s: (layer, s // tiles_per_seq, 0)),
            pl.BlockSpec((None, CONV_WIDTH, TOKEN_WIDTH), lambda s: (0, 0, 0)),
            pl.BlockSpec((None, slab, d), lambda s: (layer, s, 0)),
        ],
        out_specs=[
            pl.BlockSpec((tq, D_MODEL), lambda s: (s, 0)),
            pl.BlockSpec((None, CONV_WIDTH - 1, TOKEN_WIDTH), lambda s: (s // tiles_per_seq, 0, 0)),
            pl.BlockSpec((slab, d), lambda s: (s, 0)),
        ],
        out_shape=[
            jax.ShapeDtypeStruct((rows, D_MODEL), BF16),
            jax.ShapeDtypeStruct((batch, CONV_WIDTH - 1, TOKEN_WIDTH), F32),
            jax.ShapeDtypeStruct((d_ff, d), BF16),
        ],
        scratch_shapes=[pltpu.VMEM((tq + CARRY_ROWS, TOKEN_WIDTH), F32)],
        compiler_params=pltpu.CompilerParams(
            dimension_semantics=("arbitrary",),
            vmem_limit_bytes=_vmem_limit(blocks, scratch + 6 * _nbytes((tq, TOKEN_WIDTH), F32))),
        name="conv_prompt",
    )(z, mk, mv, conv_w, w_down)


def _conv_sample_kernel(z_ref, st_ref, mk_ref, mv_ref, cw_ref, wo_ref, mix_ref, nst_ref, wob_ref,
                        ext_ref, mixf_ref, *, dec_seq):
    t = dec_seq
    wob_ref[...] = wo_ref[...].astype(BF16)
    for n in range(SAMPLE_GROUP):
        r0, r1 = n * t, (n + 1) * t
        c = z_ref[r0:r1, TOKEN_WIDTH:2 * TOKEN_WIDTH].astype(F32)
        u = z_ref[r0:r1, 2 * TOKEN_WIDTH:3 * TOKEN_WIDTH].astype(F32)
        cu = c * u
        ext_ref[CARRY_ROWS - 2:CARRY_ROWS, :] = st_ref[n]
        ext_ref[CARRY_ROWS:CARRY_ROWS + t, :] = cu
        conv = (cw_ref[0:1, :] * ext_ref[CARRY_ROWS - 2:CARRY_ROWS - 2 + t, :]
                + cw_ref[1:2, :] * ext_ref[CARRY_ROWS - 1:CARRY_ROWS - 1 + t, :]
                + cw_ref[2:3, :] * cu)
        b = z_ref[r0:r1, 0:TOKEN_WIDTH].astype(F32)
        mixf_ref[r0:r1, 0:TOKEN_WIDTH] = b * conv
        nst_ref[n] = ext_ref[CARRY_ROWS + t - 2:CARRY_ROWS + t, :]
    _sample_cross_attention(z_ref, mk_ref, mv_ref, mixf_ref, 3 * TOKEN_WIDTH, t)
    mix_ref[...] = mixf_ref[...].astype(BF16)


def _sample_cross_attention(z_ref, mk_ref, mv_ref, mixf_ref, qm_off, t):
    def scores(i):
        n, h = divmod(i, MEM_HEADS)
        q = z_ref[n * t:(n + 1) * t, qm_off + h * MEM_HEAD_DIM:qm_off + (h + 1) * MEM_HEAD_DIM]
        return _cross_scores(q, mk_ref[n, pl.ds(h, N_MEM, stride=MEM_HEADS), :].astype(BF16))

    def finish(i, s):
        n, h = divmod(i, MEM_HEADS)
        o = _cross_values(s, mv_ref[n, pl.ds(h, N_MEM, stride=MEM_HEADS), :].astype(BF16))
        mixf_ref[n * t:(n + 1) * t, TOKEN_WIDTH + h * MEM_HEAD_DIM:TOKEN_WIDTH + (h + 1) * MEM_HEAD_DIM] = o

    _run_ahead(SAMPLE_GROUP * MEM_HEADS, SAMPLE_LOOKAHEAD, scores, finish)


def _conv_sample(z, state, mem_k, mem_v, conv_w, w_out, *, layer, dec_seq):
    rows, zc = z.shape
    dec_batch = state.shape[1]
    g = SAMPLE_GROUP
    gr = g * dec_seq
    n_steps = dec_batch // g
    wk, wn = w_out.shape[1:]
    slab = wk // n_steps
    assert slab * n_steps == wk and slab % (2 * V7X_SUBLANES) == 0
    blocks = (_nbytes((gr, zc), BF16) + 2 * _nbytes((g, CONV_WIDTH - 1, TOKEN_WIDTH), F32)
              + 2 * _nbytes((g, N_MEM, V7X_SUBLANES, MEM_HEAD_DIM), F32) + _nbytes((gr, D_MODEL), BF16)
              + _nbytes((slab, wn), F32) + _nbytes((slab, wn), BF16))
    scratch = _nbytes((2 * CARRY_ROWS, TOKEN_WIDTH), F32) + _nbytes((gr, D_MODEL), F32)
    return pl.pallas_call(
        functools.partial(_conv_sample_kernel, dec_seq=dec_seq),
        grid=(n_steps,),
        in_specs=[
            pl.BlockSpec((gr, zc), lambda i: (i, 0)),
            pl.BlockSpec((None, g, CONV_WIDTH - 1, TOKEN_WIDTH), lambda i: (0, i, 0, 0)),
            pl.BlockSpec((None, g, N_MEM * MEM_HEADS, MEM_HEAD_DIM), lambda i: (layer, i, 0, 0)),
            pl.BlockSpec((None, g, N_MEM * MEM_HEADS, MEM_HEAD_DIM), lambda i: (layer, i, 0, 0)),
            pl.BlockSpec((None, CONV_WIDTH, TOKEN_WIDTH), lambda i: (0, 0, 0)),
            pl.BlockSpec((None, slab, wn), lambda i: (0, i, 0)),
        ],
        out_specs=[
            pl.BlockSpec((gr, D_MODEL), lambda i: (i, 0)),
            pl.BlockSpec((g, CONV_WIDTH - 1, TOKEN_WIDTH), lambda i: (i, 0, 0)),
            pl.BlockSpec((slab, wn), lambda i: (i, 0)),
        ],
        out_shape=[
            jax.ShapeDtypeStruct((rows, D_MODEL), BF16),
            jax.ShapeDtypeStruct((dec_batch, CONV_WIDTH - 1, TOKEN_WIDTH), F32),
            jax.ShapeDtypeStruct((wk, wn), BF16),
        ],
        scratch_shapes=[pltpu.VMEM((2 * CARRY_ROWS, TOKEN_WIDTH), F32), pltpu.VMEM((gr, D_MODEL), F32)],
        compiler_params=pltpu.CompilerParams(
            dimension_semantics=("arbitrary",),
            vmem_limit_bytes=_vmem_limit(blocks, scratch)),
        name="conv_sample",
    )(z, state, mem_k, mem_v, conv_w, w_out)


def _band_scores(q, k):
    return _dot_nt(q * (HEAD_DIM ** -0.5), k)


def _band_probs(s, sink, upper, upper_visible):
    s = jnp.where(upper_visible, s[:, :WINDOW], jnp.where(upper, -jnp.inf, s[:, WINDOW:]))
    m = jnp.maximum(jnp.max(s, axis=-1, keepdims=True), sink)
    e = jnp.exp(s - m)
    p = e / (jnp.sum(e, axis=-1, keepdims=True) + jnp.exp(sink - m))
    return jnp.concatenate([jnp.where(upper, p, 0.0), jnp.where(upper, 0.0, p)], axis=1).astype(BF16)


def _band_values(p, v):
    return _dot(p, v)


def _swa_prompt_kernel(sink_ref, zq_ref, zp_ref, mk_ref, mv_ref, *refs, blocks_per_seq, n_cast_steps):
    n_w = (len(refs) - 1) // 2
    w_refs, mix_ref, wb_refs = refs[:n_w], refs[n_w], refs[n_w + 1:]
    s = pl.program_id(0)

    @pl.when(s < n_cast_steps)
    def _():
        for w_ref, wb_ref in zip(w_refs, wb_refs):
            wb_ref[...] = w_ref[...].astype(BF16)

    k_off = TOKEN_WIDTH
    v_off = TOKEN_WIDTH + KV_WIDTH
    qm_off = TOKEN_WIDTH + 2 * KV_WIDTH
    has_prev = (s % blocks_per_seq) > 0
    row = lax.broadcasted_iota(jnp.int32, (WINDOW, WINDOW), 0)
    col = lax.broadcasted_iota(jnp.int32, (WINDOW, WINDOW), 1)
    upper = col > row
    upper_visible = jnp.logical_and(upper, has_prev)
    @functools.lru_cache(maxsize=None)
    def keys(kh):
        lo, hi = kh * HEAD_DIM, (kh + 1) * HEAD_DIM
        return jnp.concatenate([zp_ref[:, lo:hi], zq_ref[:, k_off + lo:k_off + hi]], axis=0)

    @functools.lru_cache(maxsize=None)
    def values(kh):
        lo, hi = kh * HEAD_DIM, (kh + 1) * HEAD_DIM
        return jnp.concatenate([zp_ref[:, KV_WIDTH + lo:KV_WIDTH + hi], zq_ref[:, v_off + lo:v_off + hi]], axis=0)

    def band_scores(h):
        return _band_scores(zq_ref[:, h * HEAD_DIM:(h + 1) * HEAD_DIM], keys(h // GROUP))

    def band_finish(h, s):
        p = _band_probs(s, sink_ref[h], upper, upper_visible)
        mix_ref[:, h * HEAD_DIM:(h + 1) * HEAD_DIM] = _band_values(p, values(h // GROUP)).astype(BF16)

    def cross_scores(h):
        lo, hi = h * MEM_HEAD_DIM, (h + 1) * MEM_HEAD_DIM
        return _cross_scores(zq_ref[:, qm_off + lo:qm_off + hi], mk_ref[:, lo:hi].astype(BF16))

    def cross_finish(h, s):
        lo, hi = h * MEM_HEAD_DIM, (h + 1) * MEM_HEAD_DIM
        mix_ref[:, TOKEN_WIDTH + lo:TOKEN_WIDTH + hi] = _cross_values(s, mv_ref[:, lo:hi].astype(BF16)).astype(BF16)

    _run_ahead(N_HEADS, SCORE_LOOKAHEAD, band_scores, band_finish)
    _run_ahead(MEM_HEADS, SCORE_LOOKAHEAD, cross_scores, cross_finish)


def _swa_prompt(z, mk, mv, sinks, weights, *, layer, batch, seq):
    rows, zc = z.shape
    tq = WINDOW
    blocks_per_seq = seq // tq
    n_steps = batch * blocks_per_seq
    kv_col_block = TOKEN_WIDTH // (2 * KV_WIDTH)
    n_cast_steps = n_steps // 2
    slab_map = lambda s: (jnp.minimum(s, n_cast_steps - 1), 0)
    blocks = (_nbytes((tq, zc), BF16) + _nbytes((tq, 2 * KV_WIDTH), BF16)
              + 2 * _nbytes((N_MEM, MEM_WIDTH), F32) + _nbytes((tq, D_MODEL), BF16))
    w_in_specs, w_out_specs, w_out_shapes = [], [], []
    for w in weights:
        r, c = w.shape[1:]
        slab = r // n_cast_steps
        assert slab * n_cast_steps == r and slab % (2 * V7X_SUBLANES) == 0
        blocks += _nbytes((slab, c), F32) + _nbytes((slab, c), BF16)
        w_in_specs.append(pl.BlockSpec((None, slab, c), lambda s: (layer, jnp.minimum(s, n_cast_steps - 1), 0)))
        w_out_specs.append(pl.BlockSpec((slab, c), slab_map))
        w_out_shapes.append(jax.ShapeDtypeStruct((r, c), BF16))
    return pl.pallas_call(
        functools.partial(_swa_prompt_kernel, blocks_per_seq=blocks_per_seq, n_cast_steps=n_cast_steps),
        grid=(n_steps,),
        in_specs=[
            pl.BlockSpec(memory_space=pltpu.SMEM),
            pl.BlockSpec((tq, zc), lambda s: (s, 0)),
            pl.BlockSpec((tq, 2 * KV_WIDTH), lambda s: (jnp.maximum(s - 1, 0), kv_col_block)),
            pl.BlockSpec((None, N_MEM, MEM_WIDTH), lambda s: (layer, s // blocks_per_seq, 0)),
            pl.BlockSpec((None, N_MEM, MEM_WIDTH), lambda s: (layer, s // blocks_per_seq, 0)),
        ] + w_in_specs,
        out_specs=[pl.BlockSpec((tq, D_MODEL), lambda s: (s, 0))] + w_out_specs,
        out_shape=[jax.ShapeDtypeStruct((rows, D_MODEL), BF16)] + w_out_shapes,
        compiler_params=pltpu.CompilerParams(
            dimension_semantics=("arbitrary",),
            vmem_limit_bytes=_vmem_limit(blocks, 0)),
        name="swa_prompt",
    )(sinks, z, z, mk, mv, *weights)


def _swa_sample_kernel(sink_ref, z_ref, ck_ref, cv_ref, mk_ref, mv_ref, wo_ref,
                       mix_ref, nk_ref, nv_ref, wob_ref, knew_ref, vnew_ref, mixf_ref, *, dec_seq):
    t = dec_seq
    wob_ref[...] = wo_ref[...].astype(BF16)
    k_off = TOKEN_WIDTH
    v_off = TOKEN_WIDTH + KV_WIDTH
    qm_off = TOKEN_WIDTH + 2 * KV_WIDTH
    rows = GROUP * t
    qi = lax.broadcasted_iota(jnp.int32, (rows, WINDOW), 0) % t
    col = lax.broadcasted_iota(jnp.int32, (rows, WINDOW), 1)
    upper = col > qi
    knew_ref[...] = jnp.zeros(knew_ref.shape, F32)
    vnew_ref[...] = jnp.zeros(vnew_ref.shape, F32)
    for n in range(SAMPLE_GROUP):
        r0, r1 = n * t, (n + 1) * t
        k_new = z_ref[r0:r1, k_off:k_off + KV_WIDTH].astype(F32)
        v_new = z_ref[r0:r1, v_off:v_off + KV_WIDTH].astype(F32)
        knew_ref[n, 0:t, :] = k_new
        vnew_ref[n, 0:t, :] = v_new
        nk_ref[n, 0:WINDOW - t, :] = ck_ref[n, t:WINDOW, :]
        nv_ref[n, 0:WINDOW - t, :] = cv_ref[n, t:WINDOW, :]
        nk_ref[n, WINDOW - t:WINDOW, :] = k_new
        nv_ref[n, WINDOW - t:WINDOW, :] = v_new

    def scores(i):
        n, kh = divmod(i, N_KV_HEADS)
        lo, hi = kh * HEAD_DIM, (kh + 1) * HEAD_DIM
        k = jnp.concatenate([ck_ref[n, :, lo:hi], knew_ref[n, :, lo:hi]], axis=0).astype(BF16)
        q = jnp.concatenate(
            [z_ref[n * t:(n + 1) * t, (kh * GROUP + g) * HEAD_DIM:(kh * GROUP + g + 1) * HEAD_DIM].astype(F32)
             for g in range(GROUP)], axis=0).astype(BF16)
        return _band_scores(q, k)

    def finish(i, s):
        n, kh = divmod(i, N_KV_HEADS)
        lo, hi = kh * HEAD_DIM, (kh + 1) * HEAD_DIM
        v = jnp.concatenate([cv_ref[n, :, lo:hi], vnew_ref[n, :, lo:hi]], axis=0).astype(BF16)
        sink = jnp.concatenate(
            [jnp.full((t, 1), sink_ref[kh * GROUP + g], F32) for g in range(GROUP)], axis=0)
        o = _band_values(_band_probs(s, sink, upper, upper), v)
        for g in range(GROUP):
            h = kh * GROUP + g
            mixf_ref[n * t:(n + 1) * t, h * HEAD_DIM:(h + 1) * HEAD_DIM] = o[g * t:(g + 1) * t, :]

    _run_ahead(SAMPLE_GROUP * N_KV_HEADS, SAMPLE_LOOKAHEAD, scores, finish)
    _sample_cross_attention(z_ref, mk_ref, mv_ref, mixf_ref, qm_off, t)
    mix_ref[...] = mixf_ref[...].astype(BF16)


def _swa_sample(z, cache_k, cache_v, mem_k, mem_v, sinks, w_out, *, layer, dec_seq):
    rows, zc = z.shape
    dec_batch = cache_k.shape[0]
    g = SAMPLE_GROUP
    gr = g * dec_seq
    n_steps = dec_batch // g
    wk, wn = w_out.shape[1:]
    slab = wk // n_steps
    assert slab * n_steps == wk and slab % (2 * V7X_SUBLANES) == 0
    win = jax.ShapeDtypeStruct((dec_batch, WINDOW, KV_WIDTH), F32)
    blocks = (_nbytes((gr, zc), BF16) + 4 * _nbytes((g, WINDOW, KV_WIDTH), F32)
              + 2 * _nbytes((g, N_MEM, V7X_SUBLANES, MEM_HEAD_DIM), F32) + _nbytes((gr, D_MODEL), BF16)
              + _nbytes((slab, wn), F32) + _nbytes((slab, wn), BF16))
    scratch = 2 * _nbytes((g, WINDOW, KV_WIDTH), F32) + _nbytes((gr, D_MODEL), F32)
    return pl.pallas_call(
        functools.partial(_swa_sample_kernel, dec_seq=dec_seq),
        grid=(n_steps,),
        in_specs=[
            pl.BlockSpec(memory_space=pltpu.SMEM),
            pl.BlockSpec((gr, zc), lambda i: (i, 0)),
            pl.BlockSpec((g, WINDOW, KV_WIDTH), lambda i: (i, 0, 0)),
            pl.BlockSpec((g, WINDOW, KV_WIDTH), lambda i: (i, 0, 0)),
            pl.BlockSpec((None, g, N_MEM * MEM_HEADS, MEM_HEAD_DIM), lambda i: (layer, i, 0, 0)),
            pl.BlockSpec((None, g, N_MEM * MEM_HEADS, MEM_HEAD_DIM), lambda i: (layer, i, 0, 0)),
            pl.BlockSpec((None, slab, wn), lambda i: (0, i, 0)),
        ],
        out_specs=[
            pl.BlockSpec((gr, D_MODEL), lambda i: (i, 0)),
            pl.BlockSpec((g, WINDOW, KV_WIDTH), lambda i: (i, 0, 0)),
            pl.BlockSpec((g, WINDOW, KV_WIDTH), lambda i: (i, 0, 0)),
            pl.BlockSpec((slab, wn), lambda i: (i, 0)),
        ],
        out_shape=[jax.ShapeDtypeStruct((rows, D_MODEL), BF16), win, win,
                   jax.ShapeDtypeStruct((wk, wn), BF16)],
        scratch_shapes=[pltpu.VMEM((g, WINDOW, KV_WIDTH), F32), pltpu.VMEM((g, WINDOW, KV_WIDTH), F32),
                        pltpu.VMEM((gr, D_MODEL), F32)],
        compiler_params=pltpu.CompilerParams(
            dimension_semantics=("arbitrary",),
            vmem_limit_bytes=_vmem_limit(blocks, scratch)),
        name="swa_sample",
    )(sinks, z, cache_k, cache_v, mem_k, mem_v, w_out)


def kernel(x_prompt, x_sample, mem_prompt, state_conv, cache_win_k, cache_win_v, cache_mem_k, cache_mem_v,
           norm_mix, norm_mem, w_mem_kv, norm_ffn, w_gate, w_up, w_down,
           conv_w_in, conv_w, conv_w_out, attn_w_in, attn_sinks, attn_w_out, norm_final):
    batch, seq, d = x_prompt.shape
    dec_batch, dec_seq, _ = x_sample.shape
    depth = norm_mix.shape[0]
    d_ff = w_gate.shape[2]
    prompt_rows = batch * seq
    sample_rows = dec_batch * dec_seq
    assert d == D_MODEL and depth == 2 and seq % CONV_ROWS == 0 and seq % WINDOW == 0
    assert prompt_rows % PROMPT_TILE == 0 and dec_batch % SAMPLE_GROUP == 0
    assert dec_seq == V7X_SUBLANES and d_ff % FFN_COL_TILE == 0 and d % COL_TILE == 0
    assert d_ff % DOWN_K_TILE == 0 and d % DOWN_X_CHUNK == 0
    assert prompt_rows % FFN_ROW_TILE == 0 and prompt_rows % sample_rows == 0

    xp = x_prompt.reshape(prompt_rows, d)
    xs = x_sample.reshape(sample_rows, d)
    mem = mem_prompt.reshape(batch * N_MEM, d)
    mem_k_s = cache_mem_k.reshape(depth, dec_batch, N_MEM * MEM_HEADS, MEM_HEAD_DIM)
    mem_v_s = cache_mem_v.reshape(depth, dec_batch, N_MEM * MEM_HEADS, MEM_HEAD_DIM)
    g_mix = norm_mix.reshape(depth, 1, d)
    g_ffn = norm_ffn.reshape(depth, 1, d)

    mk, mv = _mem_kv(mem, norm_mem.reshape(depth, 1, d), w_mem_kv)

    hp, hs = _norm_rows(xp, xs, g_mix, layer=0)
    zp, zs = _matmul(hp, hs, conv_w_in, w_layer=0)
    mix_s, conv_s, wo_bf = _conv_sample(zs, state_conv, mem_k_s, mem_v_s, conv_w, conv_w_out,
                                        layer=0, dec_seq=dec_seq)
    mix_p, conv_p, wd_bf = _conv_prompt(zp, mk, mv, conv_w, w_down, layer=0, batch=batch, seq=seq)
    xp, xs, hp, hs = _out_proj(mix_p, mix_s, wo_bf, xp, xs, g_ffn, layer=0)
    ap, as_ = _ffn_up(hp, hs, w_gate, w_up, layer=0)
    xp, xs, hp, hs = _ffn_down(ap, as_, wd_bf[None], xp, xs, g_mix, w_layer=0, g_layer=1, final=False)

    zp, zs = _matmul(hp, hs, attn_w_in, w_layer=0)
    sinks = attn_sinks[0]
    mix_s, win_k_s, win_v_s, wo_bf = _swa_sample(
        zs, cache_win_k[0].reshape(dec_batch, WINDOW, KV_WIDTH), cache_win_v[0].reshape(dec_batch, WINDOW, KV_WIDTH),
        mem_k_s, mem_v_s, sinks, attn_w_out, layer=1, dec_seq=dec_seq)
    mix_p, wd_bf, wg_bf, wu_bf = _swa_prompt(zp, mk, mv, sinks, [w_down, w_gate, w_up],
                                             layer=1, batch=batch, seq=seq)
    xp, xs, hp, hs = _out_proj(mix_p, mix_s, wo_bf, xp, xs, g_ffn, layer=1)
    ap, as_ = _ffn_up(hp, hs, wg_bf[None], wu_bf[None], layer=0)
    y_prompt, y_sample = _ffn_down(ap, as_, wd_bf[None], xp, xs, norm_final.reshape(1, 1, d),
                                   w_layer=0, g_layer=0, final=True)

    win_p = zp.reshape(batch, seq, -1)[:, seq - WINDOW:, TOKEN_WIDTH:TOKEN_WIDTH + 2 * KV_WIDTH].astype(F32)
    kv_shape = (1, -1, WINDOW, N_KV_HEADS, HEAD_DIM)
    mem_shape = (depth, batch, N_MEM, MEM_HEADS, MEM_HEAD_DIM)
    return (y_prompt.reshape(batch, seq, d),
            y_sample.reshape(dec_batch, dec_seq, d),
            conv_p[None],
            conv_s[None],
            win_p[..., :KV_WIDTH].reshape(kv_shape),
            win_p[..., KV_WIDTH:].reshape(kv_shape),
            win_k_s.reshape(kv_shape),
            win_v_s.reshape(kv_shape),
            mk.reshape(mem_shape),
            mv.reshape(mem_shape))
```

```python
import functools

import jax
import jax.numpy as jnp
from jax import lax
from jax.experimental import pallas as pl
from jax.experimental.pallas import tpu as pltpu

F32 = jnp.float32
BF16 = jnp.bfloat16

D_MODEL = 2048
N_MEM = 256
MEM_HEADS = 4
MEM_WIDTH = D_MODEL // 4
MEM_HEAD_DIM = MEM_WIDTH // MEM_HEADS
TOKEN_WIDTH = D_MODEL - MEM_WIDTH
CONV_WIDTH = 3
WINDOW = 128
HEAD_DIM = 64
N_HEADS = TOKEN_WIDTH // HEAD_DIM
N_KV_HEADS = 4
GROUP = N_HEADS // N_KV_HEADS
KV_WIDTH = N_KV_HEADS * HEAD_DIM
EPS = 1e-6

V7X_VMEM_BYTES = 64 * 1024 * 1024
V7X_SUBLANES = 8
V7X_MXU_DEPTH = 256

PROMPT_TILE = 1024
COL_TILE = 512
DOWN_K_TILE = 1408
DOWN_X_CHUNK = 512
FFN_ROW_TILE = 2048
FFN_COL_TILE = 512
MXU_ROWS = 1024
OUT_ROWS = 512
CONV_ROWS = 512
SCORE_LOOKAHEAD = 3
SAMPLE_LOOKAHEAD = 4
SAMPLE_GROUP = 8
CARRY_ROWS = V7X_SUBLANES


def _nbytes(shape, dtype):
    n = 1
    for s in shape:
        n *= s
    return n * jnp.dtype(dtype).itemsize


def _vmem_limit(block_bytes, scratch_bytes):
    need = 2 * block_bytes + scratch_bytes
    return int(min(need + max(need // 4, 8 << 20), V7X_VMEM_BYTES - (6 << 20)))


def _rmsnorm(x, g):
    r = lax.rsqrt(jnp.mean(x * x, axis=-1, keepdims=True) + EPS)
    return (x * r) * g


def _dot(a, b):
    return jnp.dot(a, b, preferred_element_type=F32)


def _dot_nt(a, b):
    return lax.dot_general(a, b, (((1,), (1,)), ((), ())), preferred_element_type=F32)


def _prompt_rows_map(i, j):
    return (i, 0)


def _prompt_tile_map(i, j):
    return (i, j)


def _sample_tile_map(n_pt):
    return lambda i, j: (0, jnp.where(i == n_pt - 1, j, 0))


def _const_map(i, j):
    return (0, 0)


def _norm_rows_kernel(xp_ref, xs_ref, g_ref, hp_ref, hs_ref, *, n_pt):
    hp_ref[...] = _rmsnorm(xp_ref[...], g_ref[...]).astype(BF16)

    @pl.when(pl.program_id(0) == n_pt - 1)
    def _():
        hs_ref[...] = _rmsnorm(xs_ref[...], g_ref[...]).astype(BF16)


def _norm_rows(xp, xs, g, *, layer):
    rp, d = xp.shape
    rs = xs.shape[0]
    tm = OUT_ROWS
    n_pt = rp // tm
    blocks = _nbytes((tm, d), F32) + _nbytes((tm, d), BF16) + _nbytes((rs, d), F32) + _nbytes((rs, d), BF16)
    return pl.pallas_call(
        functools.partial(_norm_rows_kernel, n_pt=n_pt),
        grid=(n_pt,),
        in_specs=[
            pl.BlockSpec((tm, d), lambda s: (s, 0)),
            pl.BlockSpec((rs, d), lambda s: (0, 0)),
            pl.BlockSpec((None, 1, d), lambda s: (layer, 0, 0)),
        ],
        out_specs=[
            pl.BlockSpec((tm, d), lambda s: (s, 0)),
            pl.BlockSpec((rs, d), lambda s: (0, 0)),
        ],
        out_shape=[jax.ShapeDtypeStruct((rp, d), BF16), jax.ShapeDtypeStruct((rs, d), BF16)],
        compiler_params=pltpu.CompilerParams(
            dimension_semantics=("arbitrary",),
            vmem_limit_bytes=_vmem_limit(blocks, 0)),
        name="norm_rows",
    )(xp, xs, g)


def _swiglu(h, wg, wu):
    gate = _dot(h, wg)
    up = _dot(h, wu)
    return (gate * jax.nn.sigmoid(gate) * up).astype(BF16)


def _ffn_up_kernel(hp_ref, hs_ref, wg_ref, wu_ref, ap_ref, as_ref, *, n_pt):
    wg, wu = wg_ref[...].astype(BF16), wu_ref[...].astype(BF16)
    for r in range(0, hp_ref.shape[0], MXU_ROWS):
        ap_ref[r:r + MXU_ROWS, :] = _swiglu(hp_ref[r:r + MXU_ROWS, :], wg, wu)

    @pl.when(pl.program_id(0) == n_pt - 1)
    def _():
        as_ref[...] = _swiglu(hs_ref[...], wg_ref[...].astype(BF16), wu_ref[...].astype(BF16))


def _ffn_up(hp, hs, wg, wu, *, layer):
    rp, d = hp.shape
    rs = hs.shape[0]
    n = wg.shape[2]
    tm, tn = FFN_ROW_TILE, FFN_COL_TILE
    n_pt, n_j = rp // tm, n // tn
    blocks = (_nbytes((tm, d), BF16) + _nbytes((rs, d), BF16) + 2 * _nbytes((d, tn), wg.dtype)
              + _nbytes((tm, tn), BF16) + _nbytes((rs, tn), BF16))
    temps = 2 * _nbytes((d, tn), BF16) + 3 * _nbytes((tm, tn), F32)
    w_spec = pl.BlockSpec((None, d, tn), lambda i, j: (layer, 0, j))
    return pl.pallas_call(
        functools.partial(_ffn_up_kernel, n_pt=n_pt),
        grid=(n_pt, n_j),
        in_specs=[
            pl.BlockSpec((tm, d), _prompt_rows_map),
            pl.BlockSpec((rs, d), _const_map),
            w_spec,
            w_spec,
        ],
        out_specs=[
            pl.BlockSpec((tm, tn), _prompt_tile_map),
            pl.BlockSpec((rs, tn), _sample_tile_map(n_pt)),
        ],
        out_shape=[jax.ShapeDtypeStruct((rp, n), BF16), jax.ShapeDtypeStruct((rs, n), BF16)],
        compiler_params=pltpu.CompilerParams(
            dimension_semantics=("arbitrary", "arbitrary"),
            vmem_limit_bytes=_vmem_limit(blocks, temps)),
        name="ffn_up",
    )(hp, hs, wg, wu)


def _out_proj_kernel(ap_ref, as_ref, w_ref, xp_ref, xs_ref, g_ref, op_ref, os_ref, hp_ref, hs_ref, *, n_pt):
    x = xp_ref[...] + _dot(ap_ref[...], w_ref[...])
    op_ref[...] = x
    hp_ref[...] = _rmsnorm(x, g_ref[...]).astype(BF16)

    @pl.when(pl.program_id(0) == n_pt - 1)
    def _():
        x = xs_ref[...] + _dot(as_ref[...], w_ref[...])
        os_ref[...] = x
        hs_ref[...] = _rmsnorm(x, g_ref[...]).astype(BF16)


def _out_proj(ap, as_, w, xp, xs, g, *, layer):
    rp, k = ap.shape
    rs = as_.shape[0]
    n = w.shape[1]
    tm = OUT_ROWS
    n_pt = rp // tm
    rows_map = lambda s: (s, 0)
    const = lambda s: (0, 0)
    once = dict(pipeline_mode=pl.Buffered(1))
    blocks = _nbytes((tm, k), BF16) + 2 * _nbytes((tm, n), F32) + _nbytes((tm, n), BF16)
    resident = (_nbytes((k, n), BF16) + _nbytes((rs, k), BF16) + _nbytes((rs, n), F32)
                + 2 * (_nbytes((rs, n), F32) + _nbytes((rs, n), BF16)))
    return pl.pallas_call(
        functools.partial(_out_proj_kernel, n_pt=n_pt),
        grid=(n_pt,),
        in_specs=[
            pl.BlockSpec((tm, k), rows_map),
            pl.BlockSpec((rs, k), const, **once),
            pl.BlockSpec((k, n), const, **once),
            pl.BlockSpec((tm, n), rows_map),
            pl.BlockSpec((rs, n), const, **once),
            pl.BlockSpec((None, 1, n), lambda s: (layer, 0, 0)),
        ],
        out_specs=[
            pl.BlockSpec((tm, n), rows_map),
            pl.BlockSpec((rs, n), const),
            pl.BlockSpec((tm, n), rows_map),
            pl.BlockSpec((rs, n), const),
        ],
        out_shape=[jax.ShapeDtypeStruct((rp, n), F32), jax.ShapeDtypeStruct((rs, n), F32),
                   jax.ShapeDtypeStruct((rp, n), BF16), jax.ShapeDtypeStruct((rs, n), BF16)],
        compiler_params=pltpu.CompilerParams(
            dimension_semantics=("arbitrary",),
            vmem_limit_bytes=_vmem_limit(blocks, resident + 2 * _nbytes((tm, n), F32))),
        name="out_proj",
    )(ap, as_, w, xp, xs, g)


def _matmul_kernel(hp_ref, hs_ref, w_ref, zp_ref, zs_ref, *, n_pt):
    w = w_ref[...].astype(BF16)
    for r in range(0, hp_ref.shape[0], MXU_ROWS):
        zp_ref[r:r + MXU_ROWS, :] = _dot(hp_ref[r:r + MXU_ROWS, :], w).astype(zp_ref.dtype)

    @pl.when(pl.program_id(0) == n_pt - 1)
    def _():
        zs_ref[...] = _dot(hs_ref[...], w_ref[...].astype(BF16)).astype(zs_ref.dtype)


def _matmul(hp, hs, w, *, w_layer):
    rp, d = hp.shape
    rs = hs.shape[0]
    n = w.shape[2]
    tm, tn = FFN_ROW_TILE, COL_TILE
    n_pt, n_j = rp // tm, n // tn
    blocks = (_nbytes((tm, d), BF16) + _nbytes((rs, d), BF16) + _nbytes((d, tn), F32)
              + _nbytes((tm, tn), BF16) + _nbytes((rs, tn), BF16))
    temps = _nbytes((d, tn), BF16) + _nbytes((tm, tn), F32)
    return pl.pallas_call(
        functools.partial(_matmul_kernel, n_pt=n_pt),
        grid=(n_pt, n_j),
        in_specs=[
            pl.BlockSpec((tm, d), _prompt_rows_map),
            pl.BlockSpec((rs, d), _const_map),
            pl.BlockSpec((None, d, tn), lambda i, j: (w_layer, 0, j)),
        ],
        out_specs=[
            pl.BlockSpec((tm, tn), _prompt_tile_map),
            pl.BlockSpec((rs, tn), _sample_tile_map(n_pt)),
        ],
        out_shape=[jax.ShapeDtypeStruct((rp, n), BF16), jax.ShapeDtypeStruct((rs, n), BF16)],
        compiler_params=pltpu.CompilerParams(
            dimension_semantics=("arbitrary", "arbitrary"),
            vmem_limit_bytes=_vmem_limit(blocks, temps)),
        name="matmul",
    )(hp, hs, w)


def _ffn_down_kernel(ap_ref, as_ref, w_ref, xp_ref, xs_ref, g_ref, *refs, n_pt, n_k, n_xc, final):
    i, k = pl.program_id(0), pl.program_id(1)
    if final:
        op_ref, os_ref, apl_ref, asl_ref, wl_ref = refs
    else:
        op_ref, os_ref, hp_ref, hs_ref, apl_ref, asl_ref, wl_ref = refs
    xc = xp_ref.shape[1]
    kf = (ap_ref.shape[1] // V7X_MXU_DEPTH) * V7X_MXU_DEPTH
    even = k % 2 == 0

    def whole_passes(a_ref):
        return _dot(a_ref[:, :kf], w_ref[:kf, :])

    def with_stash(a_ref, al_ref):
        return _dot(jnp.concatenate([al_ref[...], a_ref[...]], axis=1),
                    jnp.concatenate([wl_ref[...], w_ref[...]], axis=0))

    @pl.when(k == 0)
    def _():
        op_ref[...] = whole_passes(ap_ref)

    @pl.when(jnp.logical_and(k > 0, even))
    def _():
        op_ref[...] = op_ref[...] + whole_passes(ap_ref)

    @pl.when(jnp.logical_not(even))
    def _():
        op_ref[...] = op_ref[...] + with_stash(ap_ref, apl_ref)

    for c in range(n_xc):
        @pl.when(k == c)
        def _():
            op_ref[:, c * xc:(c + 1) * xc] = op_ref[:, c * xc:(c + 1) * xc] + xp_ref[...]

    @pl.when(k == n_k - 1)
    def _():
        if final:
            op_ref[...] = _rmsnorm(op_ref[...], g_ref[...])
        else:
            hp_ref[...] = _rmsnorm(op_ref[...], g_ref[...]).astype(BF16)

    @pl.when(i == n_pt - 1)
    def _():
        @pl.when(k == 0)
        def _():
            os_ref[...] = xs_ref[...] + whole_passes(as_ref)

        @pl.when(jnp.logical_and(k > 0, even))
        def _():
            os_ref[...] = os_ref[...] + whole_passes(as_ref)

        @pl.when(jnp.logical_not(even))
        def _():
            os_ref[...] = os_ref[...] + with_stash(as_ref, asl_ref)

        @pl.when(even)
        def _():
            asl_ref[...] = as_ref[:, kf:]

        @pl.when(k == n_k - 1)
        def _():
            if final:
                os_ref[...] = _rmsnorm(os_ref[...], g_ref[...])
            else:
                hs_ref[...] = _rmsnorm(os_ref[...], g_ref[...]).astype(BF16)

    @pl.when(even)
    def _():
        apl_ref[...] = ap_ref[:, kf:]
        wl_ref[...] = w_ref[kf:, :]


def _ffn_down(ap, as_, w, xp, xs, g, *, w_layer, g_layer, final):
    rp, kdim = ap.shape
    rs = as_.shape[0]
    n = w.shape[2]
    tm, tk, xc = PROMPT_TILE, DOWN_K_TILE, DOWN_X_CHUNK
    n_pt, n_k, n_xc = rp // tm, kdim // tk, n // xc
    k_left = tk % V7X_MXU_DEPTH
    assert n_xc <= n_k and n_k % 2 == 0 and 2 * k_left == V7X_MXU_DEPTH and w.dtype == BF16
    rows = lambda i, k: (i, 0)
    blocks = (_nbytes((tm, tk), BF16) + _nbytes((rs, tk), BF16) + _nbytes((tk, n), w.dtype)
              + _nbytes((tm, xc), F32) + _nbytes((rs, n), F32)
              + _nbytes((tm, n), F32) + _nbytes((rs, n), F32))
    out_specs = [pl.BlockSpec((tm, n), rows), pl.BlockSpec((rs, n), _const_map)]
    out_shape = [jax.ShapeDtypeStruct((rp, n), F32), jax.ShapeDtypeStruct((rs, n), F32)]
    if not final:
        blocks += _nbytes((tm, n), BF16) + _nbytes((rs, n), BF16)
        out_specs += [pl.BlockSpec((tm, n), rows), pl.BlockSpec((rs, n), _const_map)]
        out_shape += [jax.ShapeDtypeStruct((rp, n), BF16), jax.ShapeDtypeStruct((rs, n), BF16)]
    return pl.pallas_call(
        functools.partial(_ffn_down_kernel, n_pt=n_pt, n_k=n_k, n_xc=n_xc, final=final),
        grid=(n_pt, n_k),
        in_specs=[
            pl.BlockSpec((tm, tk), lambda i, k: (i, k)),
            pl.BlockSpec((rs, tk), lambda i, k: (0, jnp.where(i == n_pt - 1, k, 0))),
            pl.BlockSpec((None, tk, n), lambda i, k: (w_layer, k, 0)),
            pl.BlockSpec((tm, xc), lambda i, k: (i, jnp.minimum(k, n_xc - 1))),
            pl.BlockSpec((rs, n), _const_map),
            pl.BlockSpec((None, 1, n), lambda i, k: (g_layer, 0, 0)),
        ],
        out_specs=out_specs,
        out_shape=out_shape,
        scratch_shapes=[pltpu.VMEM((tm, k_left), BF16), pltpu.VMEM((rs, k_left), BF16),
                        pltpu.VMEM((k_left, n), BF16)],
        compiler_params=pltpu.CompilerParams(
            dimension_semantics=("arbitrary", "arbitrary"),
            vmem_limit_bytes=_vmem_limit(blocks, _nbytes((tk, n), BF16))),
        name="ffn_down",
    )(ap, as_, w, xp, xs, g)


def _mem_kv_kernel(x_ref, g_ref, w_ref, k_ref, v_ref):
    h = _rmsnorm(x_ref[...], g_ref[...]).astype(BF16)
    kv = _dot(h, w_ref[...].astype(BF16))
    k_ref[...] = kv[:, :MEM_WIDTH]
    v_ref[...] = kv[:, MEM_WIDTH:]


def _mem_kv(mem, g, w):
    rows, d = mem.shape
    depth = w.shape[0]
    tm = 512
    out = jax.ShapeDtypeStruct((depth, rows, MEM_WIDTH), F32)
    blocks = (_nbytes((tm, d), F32) + _nbytes((d, 2 * MEM_WIDTH), F32) + 2 * _nbytes((tm, MEM_WIDTH), F32))
    return pl.pallas_call(
        _mem_kv_kernel,
        grid=(depth, rows // tm),
        in_specs=[
            pl.BlockSpec((tm, d), lambda l, i: (i, 0)),
            pl.BlockSpec((None, 1, d), lambda l, i: (l, 0, 0)),
            pl.BlockSpec((None, d, 2 * MEM_WIDTH), lambda l, i: (l, 0, 0)),
        ],
        out_specs=[
            pl.BlockSpec((None, tm, MEM_WIDTH), lambda l, i: (l, i, 0)),
            pl.BlockSpec((None, tm, MEM_WIDTH), lambda l, i: (l, i, 0)),
        ],
        out_shape=[out, out],
        compiler_params=pltpu.CompilerParams(
            dimension_semantics=("arbitrary", "arbitrary"),
            vmem_limit_bytes=_vmem_limit(blocks, _nbytes((d, 2 * MEM_WIDTH), BF16))),
        name="mem_kv",
    )(mem, g, w)


def _cross_scores(q, k):
    return _dot_nt(q, k) * (MEM_HEAD_DIM ** -0.5)


def _cross_values(s, v):
    e = jnp.exp(s - jnp.max(s, axis=-1, keepdims=True)).astype(BF16)
    return _dot(e, v) / _dot(e, jnp.ones(v.shape, BF16))


def _run_ahead(n_items, lookahead, first, second):
    pending = {}
    for i in range(n_items + lookahead):
        if i < n_items:
            pending[i] = first(i)
        if i >= lookahead:
            second(i - lookahead, pending.pop(i - lookahead))


def _conv_prompt_kernel(z_ref, mk_ref, mv_ref, cw_ref, wd_ref, mix_ref, st_ref, wdb_ref, ext_ref, *, tiles_per_seq):
    s = pl.program_id(0)
    tq = z_ref.shape[0]
    wdb_ref[...] = wd_ref[...].astype(BF16)

    @pl.when(s % tiles_per_seq == 0)
    def _():
        ext_ref[0:CARRY_ROWS, :] = jnp.zeros((CARRY_ROWS, TOKEN_WIDTH), F32)

    c = z_ref[:, TOKEN_WIDTH:2 * TOKEN_WIDTH].astype(F32)
    u = z_ref[:, 2 * TOKEN_WIDTH:3 * TOKEN_WIDTH].astype(F32)
    cu = c * u
    ext_ref[CARRY_ROWS:CARRY_ROWS + tq, :] = cu
    conv = (cw_ref[0:1, :] * ext_ref[CARRY_ROWS - 2:CARRY_ROWS - 2 + tq, :]
            + cw_ref[1:2, :] * ext_ref[CARRY_ROWS - 1:CARRY_ROWS - 1 + tq, :]
            + cw_ref[2:3, :] * cu)
    b = z_ref[:, 0:TOKEN_WIDTH].astype(F32)
    mix_ref[:, 0:TOKEN_WIDTH] = (b * conv).astype(BF16)
    st_ref[...] = ext_ref[CARRY_ROWS + tq - 2:CARRY_ROWS + tq, :]
    ext_ref[0:CARRY_ROWS, :] = ext_ref[tq:tq + CARRY_ROWS, :]

    def scores(h):
        lo, hi = h * MEM_HEAD_DIM, (h + 1) * MEM_HEAD_DIM
        return _cross_scores(z_ref[:, 3 * TOKEN_WIDTH + lo:3 * TOKEN_WIDTH + hi], mk_ref[:, lo:hi].astype(BF16))

    def finish(h, s):
        lo, hi = h * MEM_HEAD_DIM, (h + 1) * MEM_HEAD_DIM
        mix_ref[:, TOKEN_WIDTH + lo:TOKEN_WIDTH + hi] = _cross_values(s, mv_ref[:, lo:hi].astype(BF16)).astype(BF16)

    _run_ahead(MEM_HEADS, SCORE_LOOKAHEAD, scores, finish)


def _conv_prompt(z, mk, mv, conv_w, w_down, *, layer, batch, seq):
    rows, zc = z.shape
    tq = CONV_ROWS
    tiles_per_seq = seq // tq
    n_steps = batch * tiles_per_seq
    d_ff, d = w_down.shape[1:]
    slab = d_ff // n_steps
    assert slab * n_steps == d_ff and slab % (2 * V7X_SUBLANES) == 0
    blocks = (_nbytes((tq, zc), BF16) + 2 * _nbytes((N_MEM, MEM_WIDTH), F32) + _nbytes((tq, D_MODEL), BF16)
              + _nbytes((slab, d), F32) + _nbytes((slab, d), BF16))
    scratch = _nbytes((tq + CARRY_ROWS, TOKEN_WIDTH), F32)
    return pl.pallas_call(
        functools.partial(_conv_prompt_kernel, tiles_per_seq=tiles_per_seq),
        grid=(n_steps,),
        in_specs=[
            pl.BlockSpec((tq, zc), lambda s: (s, 0)),
            pl.BlockSpec((None, N_MEM, MEM_WIDTH), lambda s: (layer, s // tiles_per_seq, 0)),
            pl.BlockSpec((None, N_MEM, MEM_WIDTH), lambda s: (layer, s // tiles_per_seq, 0)),
            pl.BlockSpec((None, CONV_WIDTH, TOKEN_WIDTH), lambda s: (0, 0, 0)),
            pl.BlockSpec((None, slab, d), lambda s: (layer, s, 0)),
        ],
        out_specs=[
            pl.BlockSpec((tq, D_MODEL), lambda s: (s, 0)),
            pl.BlockSpec((None, CONV_WIDTH - 1, TOKEN_WIDTH), lambda s: (s // tiles_per_seq, 0, 0)),
            pl.BlockSpec((slab, d), lambda s: (s, 0)),
        ],
        out_shape=[
            jax.ShapeDtypeStruct((rows, D_MODEL), BF16),
            jax.ShapeDtypeStruct((batch, CONV_WIDTH - 1, TOKEN_WIDTH), F32),
            jax.ShapeDtypeStruct((d_ff, d), BF16),
        ],
        scratch_shapes=[pltpu.VMEM((tq + CARRY_ROWS, TOKEN_WIDTH), F32)],
        compiler_params=pltpu.CompilerParams(
            dimension_semantics=("arbitrary",),
            vmem_limit_bytes=_vmem_limit(blocks, scratch + 6 * _nbytes((tq, TOKEN_WIDTH), F32))),
        name="conv_prompt",
    )(z, mk, mv, conv_w, w_down)


def _conv_sample_kernel(z_ref, st_ref, mk_ref, mv_ref, cw_ref, wo_ref, mix_ref, nst_ref, wob_ref,
                        ext_ref, mixf_ref, *, dec_seq):
    t = dec_seq
    wob_ref[...] = wo_ref[...].astype(BF16)
    for n in range(SAMPLE_GROUP):
        r0, r1 = n * t, (n + 1) * t
        c = z_ref[r0:r1, TOKEN_WIDTH:2 * TOKEN_WIDTH].astype(F32)
        u = z_ref[r0:r1, 2 * TOKEN_WIDTH:3 * TOKEN_WIDTH].astype(F32)
        cu = c * u
        ext_ref[CARRY_ROWS - 2:CARRY_ROWS, :] = st_ref[n]
        ext_ref[CARRY_ROWS:CARRY_ROWS + t, :] = cu
        conv = (cw_ref[0:1, :] * ext_ref[CARRY_ROWS - 2:CARRY_ROWS - 2 + t, :]
                + cw_ref[1:2, :] * ext_ref[CARRY_ROWS - 1:CARRY_ROWS - 1 + t, :]
                + cw_ref[2:3, :] * cu)
        b = z_ref[r0:r1, 0:TOKEN_WIDTH].astype(F32)
        mixf_ref[r0:r1, 0:TOKEN_WIDTH] = b * conv
        nst_ref[n] = ext_ref[CARRY_ROWS + t - 2:CARRY_ROWS + t, :]
    _sample_cross_attention(z_ref, mk_ref, mv_ref, mixf_ref, 3 * TOKEN_WIDTH, t)
    mix_ref[...] = mixf_ref[...].astype(BF16)


def _sample_cross_attention(z_ref, mk_ref, mv_ref, mixf_ref, qm_off, t):
    def scores(i):
        n, h = divmod(i, MEM_HEADS)
        q = z_ref[n * t:(n + 1) * t, qm_off + h * MEM_HEAD_DIM:qm_off + (h + 1) * MEM_HEAD_DIM]
        return _cross_scores(q, mk_ref[n, pl.ds(h, N_MEM, stride=MEM_HEADS), :].astype(BF16))

    def finish(i, s):
        n, h = divmod(i, MEM_HEADS)
        o = _cross_values(s, mv_ref[n, pl.ds(h, N_MEM, stride=MEM_HEADS), :].astype(BF16))
        mixf_ref[n * t:(n + 1) * t, TOKEN_WIDTH + h * MEM_HEAD_DIM:TOKEN_WIDTH + (h + 1) * MEM_HEAD_DIM] = o

    _run_ahead(SAMPLE_GROUP * MEM_HEADS, SAMPLE_LOOKAHEAD, scores, finish)


def _conv_sample(z, state, mem_k, mem_v, conv_w, w_out, *, layer, dec_seq):
    rows, zc = z.shape
    dec_batch = state.shape[1]
    g = SAMPLE_GROUP
    gr = g * dec_seq
    n_steps = dec_batch // g
    wk, wn = w_out.shape[1:]
    slab = wk // n_steps
    assert slab * n_steps == wk and slab % (2 * V7X_SUBLANES) == 0
    blocks = (_nbytes((gr, zc), BF16) + 2 * _nbytes((g, CONV_WIDTH - 1, TOKEN_WIDTH), F32)
              + 2 * _nbytes((g, N_MEM, V7X_SUBLANES, MEM_HEAD_DIM), F32) + _nbytes((gr, D_MODEL), BF16)
              + _nbytes((slab, wn), F32) + _nbytes((slab, wn), BF16))
    scratch = _nbytes((2 * CARRY_ROWS, TOKEN_WIDTH), F32) + _nbytes((gr, D_MODEL), F32)
    return pl.pallas_call(
        functools.partial(_conv_sample_kernel, dec_seq=dec_seq),
        grid=(n_steps,),
        in_specs=[
            pl.BlockSpec((gr, zc), lambda i: (i, 0)),
            pl.BlockSpec((None, g, CONV_WIDTH - 1, TOKEN_WIDTH), lambda i: (0, i, 0, 0)),
            pl.BlockSpec((None, g, N_MEM * MEM_HEADS, MEM_HEAD_DIM), lambda i: (layer, i, 0, 0)),
            pl.BlockSpec((None, g, N_MEM * MEM_HEADS, MEM_HEAD_DIM), lambda i: (layer, i, 0, 0)),
            pl.BlockSpec((None, CONV_WIDTH, TOKEN_WIDTH), lambda i: (0, 0, 0)),
            pl.BlockSpec((None, slab, wn), lambda i: (0, i, 0)),
        ],
        out_specs=[
            pl.BlockSpec((gr, D_MODEL), lambda i: (i, 0)),
            pl.BlockSpec((g, CONV_WIDTH - 1, TOKEN_WIDTH), lambda i: (i, 0, 0)),
            pl.BlockSpec((slab, wn), lambda i: (i, 0)),
        ],
        out_shape=[
            jax.ShapeDtypeStruct((rows, D_MODEL), BF16),
            jax.ShapeDtypeStruct((dec_batch, CONV_WIDTH - 1, TOKEN_WIDTH), F32),
            jax.ShapeDtypeStruct((wk, wn), BF16),
        ],
        scratch_shapes=[pltpu.VMEM((2 * CARRY_ROWS, TOKEN_WIDTH), F32), pltpu.VMEM((gr, D_MODEL), F32)],
        compiler_params=pltpu.CompilerParams(
            dimension_semantics=("arbitrary",),
            vmem_limit_bytes=_vmem_limit(blocks, scratch)),
        name="conv_sample",
    )(z, state, mem_k, mem_v, conv_w, w_out)


def _band_scores(q, k):
    return _dot_nt(q * (HEAD_DIM ** -0.5), k)


def _band_probs(s, sink, upper, upper_visible):
    s = jnp.where(upper_visible, s[:, :WINDOW], jnp.where(upper, -jnp.inf, s[:, WINDOW:]))
    m = jnp.maximum(jnp.max(s, axis=-1, keepdims=True), sink)
    e = jnp.exp(s - m)
    e = jnp.concatenate([jnp.where(upper, e, 0.0), jnp.where(upper, 0.0, e)], axis=1).astype(BF16)
    return e, jnp.exp(sink - m)


def _band_values(probs, v, ones):
    e, sink_term = probs
    return _dot(e, v) / (_dot(e, ones) + sink_term)


def _swa_prompt_kernel(sink_ref, zq_ref, zp_ref, mk_ref, mv_ref, *refs, blocks_per_seq, n_cast_steps):
    n_w = (len(refs) - 1) // 2
    w_refs, mix_ref, wb_refs = refs[:n_w], refs[n_w], refs[n_w + 1:]
    s = pl.program_id(0)

    @pl.when(s < n_cast_steps)
    def _():
        for w_ref, wb_ref in zip(w_refs, wb_refs):
            wb_ref[...] = w_ref[...].astype(BF16)

    k_off = TOKEN_WIDTH
    v_off = TOKEN_WIDTH + KV_WIDTH
    qm_off = TOKEN_WIDTH + 2 * KV_WIDTH
    has_prev = (s % blocks_per_seq) > 0
    row = lax.broadcasted_iota(jnp.int32, (WINDOW, WINDOW), 0)
    col = lax.broadcasted_iota(jnp.int32, (WINDOW, WINDOW), 1)
    upper = col > row
    upper_visible = jnp.logical_and(upper, has_prev)
    @functools.lru_cache(maxsize=None)
    def keys(kh):
        lo, hi = kh * HEAD_DIM, (kh + 1) * HEAD_DIM
        return jnp.concatenate([zp_ref[:, lo:hi], zq_ref[:, k_off + lo:k_off + hi]], axis=0)

    @functools.lru_cache(maxsize=None)
    def values(kh):
        lo, hi = kh * HEAD_DIM, (kh + 1) * HEAD_DIM
        return jnp.concatenate([zp_ref[:, KV_WIDTH + lo:KV_WIDTH + hi], zq_ref[:, v_off + lo:v_off + hi]], axis=0)

    ones = jnp.ones((2 * WINDOW, HEAD_DIM), BF16)

    def band_scores(h):
        return _band_scores(zq_ref[:, h * HEAD_DIM:(h + 1) * HEAD_DIM], keys(h // GROUP))

    def band_finish(h, s):
        p = _band_probs(s, sink_ref[h], upper, upper_visible)
        mix_ref[:, h * HEAD_DIM:(h + 1) * HEAD_DIM] = _band_values(p, values(h // GROUP), ones).astype(BF16)

    def cross_scores(h):
        lo, hi = h * MEM_HEAD_DIM, (h + 1) * MEM_HEAD_DIM
        return _cross_scores(zq_ref[:, qm_off + lo:qm_off + hi], mk_ref[:, lo:hi].astype(BF16))

    def cross_finish(h, s):
        lo, hi = h * MEM_HEAD_DIM, (h + 1) * MEM_HEAD_DIM
        mix_ref[:, TOKEN_WIDTH + lo:TOKEN_WIDTH + hi] = _cross_values(s, mv_ref[:, lo:hi].astype(BF16)).astype(BF16)

    _run_ahead(N_HEADS, SCORE_LOOKAHEAD, band_scores, band_finish)
    _run_ahead(MEM_HEADS, SCORE_LOOKAHEAD, cross_scores, cross_finish)


def _swa_prompt(z, mk, mv, sinks, weights, *, layer, batch, seq):
    rows, zc = z.shape
    tq = WINDOW
    blocks_per_seq = seq // tq
    n_steps = batch * blocks_per_seq
    kv_col_block = TOKEN_WIDTH // (2 * KV_WIDTH)
    n_cast_steps = n_steps // 2
    slab_map = lambda s: (jnp.minimum(s, n_cast_steps - 1), 0)
    blocks = (_nbytes((tq, zc), BF16) + _nbytes((tq, 2 * KV_WIDTH), BF16)
              + 2 * _nbytes((N_MEM, MEM_WIDTH), F32) + _nbytes((tq, D_MODEL), BF16))
    w_in_specs, w_out_specs, w_out_shapes = [], [], []
    for w in weights:
        r, c = w.shape[1:]
        slab = r // n_cast_steps
        assert slab * n_cast_steps == r and slab % (2 * V7X_SUBLANES) == 0
        blocks += _nbytes((slab, c), F32) + _nbytes((slab, c), BF16)
        w_in_specs.append(pl.BlockSpec((None, slab, c), lambda s: (layer, jnp.minimum(s, n_cast_steps - 1), 0)))
        w_out_specs.append(pl.BlockSpec((slab, c), slab_map))
        w_out_shapes.append(jax.ShapeDtypeStruct((r, c), BF16))
    return pl.pallas_call(
        functools.partial(_swa_prompt_kernel, blocks_per_seq=blocks_per_seq, n_cast_steps=n_cast_steps),
        grid=(n_steps,),
        in_specs=[
            pl.BlockSpec(memory_space=pltpu.SMEM),
            pl.BlockSpec((tq, zc), lambda s: (s, 0)),
            pl.BlockSpec((tq, 2 * KV_WIDTH), lambda s: (jnp.maximum(s - 1, 0), kv_col_block)),
            pl.BlockSpec((None, N_MEM, MEM_WIDTH), lambda s: (layer, s // blocks_per_seq, 0)),
            pl.BlockSpec((None, N_MEM, MEM_WIDTH), lambda s: (layer, s // blocks_per_seq, 0)),
        ] + w_in_specs,
        out_specs=[pl.BlockSpec((tq, D_MODEL), lambda s: (s, 0))] + w_out_specs,
        out_shape=[jax.ShapeDtypeStruct((rows, D_MODEL), BF16)] + w_out_shapes,
        compiler_params=pltpu.CompilerParams(
            dimension_semantics=("arbitrary",),
            vmem_limit_bytes=_vmem_limit(blocks, 0)),
        name="swa_prompt",
    )(sinks, z, z, mk, mv, *weights)


def _swa_sample_kernel(sink_ref, z_ref, ck_ref, cv_ref, mk_ref, mv_ref, wo_ref,
                       mix_ref, nk_ref, nv_ref, wob_ref, knew_ref, vnew_ref, mixf_ref, *, dec_seq):
    t = dec_seq
    wob_ref[...] = wo_ref[...].astype(BF16)
    k_off = TOKEN_WIDTH
    v_off = TOKEN_WIDTH + KV_WIDTH
    qm_off = TOKEN_WIDTH + 2 * KV_WIDTH
    rows = GROUP * t
    qi = lax.broadcasted_iota(jnp.int32, (rows, WINDOW), 0) % t
    col = lax.broadcasted_iota(jnp.int32, (rows, WINDOW), 1)
    upper = col > qi
    knew_ref[...] = jnp.zeros(knew_ref.shape, F32)
    vnew_ref[...] = jnp.zeros(vnew_ref.shape, F32)
    for n in range(SAMPLE_GROUP):
        r0, r1 = n * t, (n + 1) * t
        k_new = z_ref[r0:r1, k_off:k_off + KV_WIDTH].astype(F32)
        v_new = z_ref[r0:r1, v_off:v_off + KV_WIDTH].astype(F32)
        knew_ref[n, 0:t, :] = k_new
        vnew_ref[n, 0:t, :] = v_new
        nk_ref[n, 0:WINDOW - t, :] = ck_ref[n, t:WINDOW, :]
        nv_ref[n, 0:WINDOW - t, :] = cv_ref[n, t:WINDOW, :]
        nk_ref[n, WINDOW - t:WINDOW, :] = k_new
        nv_ref[n, WINDOW - t:WINDOW, :] = v_new

    def scores(i):
        n, kh = divmod(i, N_KV_HEADS)
        lo, hi = kh * HEAD_DIM, (kh + 1) * HEAD_DIM
        k = jnp.concatenate([ck_ref[n, :, lo:hi], knew_ref[n, :, lo:hi]], axis=0).astype(BF16)
        q = jnp.concatenate(
            [z_ref[n * t:(n + 1) * t, (kh * GROUP + g) * HEAD_DIM:(kh * GROUP + g + 1) * HEAD_DIM].astype(F32)
             for g in range(GROUP)], axis=0).astype(BF16)
        return _band_scores(q, k)

    def finish(i, s):
        n, kh = divmod(i, N_KV_HEADS)
        lo, hi = kh * HEAD_DIM, (kh + 1) * HEAD_DIM
        v = jnp.concatenate([cv_ref[n, :, lo:hi], vnew_ref[n, :, lo:hi]], axis=0).astype(BF16)
        sink = jnp.concatenate(
            [jnp.full((t, 1), sink_ref[kh * GROUP + g], F32) for g in range(GROUP)], axis=0)
        o = _band_values(_band_probs(s, sink, upper, upper), v, jnp.ones((2 * WINDOW, HEAD_DIM), BF16))
        for g in range(GROUP):
            h = kh * GROUP + g
            mixf_ref[n * t:(n + 1) * t, h * HEAD_DIM:(h + 1) * HEAD_DIM] = o[g * t:(g + 1) * t, :]

    _run_ahead(SAMPLE_GROUP * N_KV_HEADS, SAMPLE_LOOKAHEAD, scores, finish)
    _sample_cross_attention(z_ref, mk_ref, mv_ref, mixf_ref, qm_off, t)
    mix_ref[...] = mixf_ref[...].astype(BF16)


def _swa_sample(z, cache_k, cache_v, mem_k, mem_v, sinks, w_out, *, layer, dec_seq):
    rows, zc = z.shape
    dec_batch = cache_k.shape[0]
    g = SAMPLE_GROUP
    gr = g * dec_seq
    n_steps = dec_batch // g
    wk, wn = w_out.shape[1:]
    slab = wk // n_steps
    assert slab * n_steps == wk and slab % (2 * V7X_SUBLANES) == 0
    win = jax.ShapeDtypeStruct((dec_batch, WINDOW, KV_WIDTH), F32)
    blocks = (_nbytes((gr, zc), BF16) + 4 * _nbytes((g, WINDOW, KV_WIDTH), F32)
              + 2 * _nbytes((g, N_MEM, V7X_SUBLANES, MEM_HEAD_DIM), F32) + _nbytes((gr, D_MODEL), BF16)
              + _nbytes((slab, wn), F32) + _nbytes((slab, wn), BF16))
    scratch = 2 * _nbytes((g, WINDOW, KV_WIDTH), F32) + _nbytes((gr, D_MODEL), F32)
    return pl.pallas_call(
        functools.partial(_swa_sample_kernel, dec_seq=dec_seq),
        grid=(n_steps,),
        in_specs=[
            pl.BlockSpec(memory_space=pltpu.SMEM),
            pl.BlockSpec((gr, zc), lambda i: (i, 0)),
            pl.BlockSpec((g, WINDOW, KV_WIDTH), lambda i: (i, 0, 0)),
            pl.BlockSpec((g, WINDOW, KV_WIDTH), lambda i: (i, 0, 0)),
            pl.BlockSpec((None, g, N_MEM * MEM_HEADS, MEM_HEAD_DIM), lambda i: (layer, i, 0, 0)),
            pl.BlockSpec((None, g, N_MEM * MEM_HEADS, MEM_HEAD_DIM), lambda i: (layer, i, 0, 0)),
            pl.BlockSpec((None, slab, wn), lambda i: (0, i, 0)),
        ],
        out_specs=[
            pl.BlockSpec((gr, D_MODEL), lambda i: (i, 0)),
            pl.BlockSpec((g, WINDOW, KV_WIDTH), lambda i: (i, 0, 0)),
            pl.BlockSpec((g, WINDOW, KV_WIDTH), lambda i: (i, 0, 0)),
            pl.BlockSpec((slab, wn), lambda i: (i, 0)),
        ],
        out_shape=[jax.ShapeDtypeStruct((rows, D_MODEL), BF16), win, win,
                   jax.ShapeDtypeStruct((wk, wn), BF16)],
        scratch_shapes=[pltpu.VMEM((g, WINDOW, KV_WIDTH), F32), pltpu.VMEM((g, WINDOW, KV_WIDTH), F32),
                        pltpu.VMEM((gr, D_MODEL), F32)],
        compiler_params=pltpu.CompilerParams(
            dimension_semantics=("arbitrary",),
            vmem_limit_bytes=_vmem_limit(blocks, scratch)),
        name="swa_sample",
    )(sinks, z, cache_k, cache_v, mem_k, mem_v, w_out)


def kernel(x_prompt, x_sample, mem_prompt, state_conv, cache_win_k, cache_win_v, cache_mem_k, cache_mem_v,
           norm_mix, norm_mem, w_mem_kv, norm_ffn, w_gate, w_up, w_down,
           conv_w_in, conv_w, conv_w_out, attn_w_in, attn_sinks, attn_w_out, norm_final):
    batch, seq, d = x_prompt.shape
    dec_batch, dec_seq, _ = x_sample.shape
    depth = norm_mix.shape[0]
    d_ff = w_gate.shape[2]
    prompt_rows = batch * seq
    sample_rows = dec_batch * dec_seq
    assert d == D_MODEL and depth == 2 and seq % CONV_ROWS == 0 and seq % WINDOW == 0
    assert prompt_rows % PROMPT_TILE == 0 and dec_batch % SAMPLE_GROUP == 0
    assert dec_seq == V7X_SUBLANES and d_ff % FFN_COL_TILE == 0 and d % COL_TILE == 0
    assert d_ff % DOWN_K_TILE == 0 and d % DOWN_X_CHUNK == 0
    assert prompt_rows % FFN_ROW_TILE == 0 and prompt_rows % sample_rows == 0

    xp = x_prompt.reshape(prompt_rows, d)
    xs = x_sample.reshape(sample_rows, d)
    mem = mem_prompt.reshape(batch * N_MEM, d)
    mem_k_s = cache_mem_k.reshape(depth, dec_batch, N_MEM * MEM_HEADS, MEM_HEAD_DIM)
    mem_v_s = cache_mem_v.reshape(depth, dec_batch, N_MEM * MEM_HEADS, MEM_HEAD_DIM)
    g_mix = norm_mix.reshape(depth, 1, d)
    g_ffn = norm_ffn.reshape(depth, 1, d)

    mk, mv = _mem_kv(mem, norm_mem.reshape(depth, 1, d), w_mem_kv)

    hp, hs = _norm_rows(xp, xs, g_mix, layer=0)
    zp, zs = _matmul(hp, hs, conv_w_in, w_layer=0)
    mix_s, conv_s, wo_bf = _conv_sample(zs, state_conv, mem_k_s, mem_v_s, conv_w, conv_w_out,
                                        layer=0, dec_seq=dec_seq)
    mix_p, conv_p, wd_bf = _conv_prompt(zp, mk, mv, conv_w, w_down, layer=0, batch=batch, seq=seq)
    xp, xs, hp, hs = _out_proj(mix_p, mix_s, wo_bf, xp, xs, g_ffn, layer=0)
    ap, as_ = _ffn_up(hp, hs, w_gate, w_up, layer=0)
    xp, xs, hp, hs = _ffn_down(ap, as_, wd_bf[None], xp, xs, g_mix, w_layer=0, g_layer=1, final=False)

    zp, zs = _matmul(hp, hs, attn_w_in, w_layer=0)
    sinks = attn_sinks[0]
    mix_s, win_k_s, win_v_s, wo_bf = _swa_sample(
        zs, cache_win_k[0].reshape(dec_batch, WINDOW, KV_WIDTH), cache_win_v[0].reshape(dec_batch, WINDOW, KV_WIDTH),
        mem_k_s, mem_v_s, sinks, attn_w_out, layer=1, dec_seq=dec_seq)
    mix_p, wd_bf, wg_bf, wu_bf = _swa_prompt(zp, mk, mv, sinks, [w_down, w_gate, w_up],
                                             layer=1, batch=batch, seq=seq)
    xp, xs, hp, hs = _out_proj(mix_p, mix_s, wo_bf, xp, xs, g_ffn, layer=1)
    ap, as_ = _ffn_up(hp, hs, wg_bf[None], wu_bf[None], layer=0)
    y_prompt, y_sample = _ffn_down(ap, as_, wd_bf[None], xp, xs, norm_final.reshape(1, 1, d),
                                   w_layer=0, g_layer=0, final=True)

    win_p = zp.reshape(batch, seq, -1)[:, seq - WINDOW:, TOKEN_WIDTH:TOKEN_WIDTH + 2 * KV_WIDTH].astype(F32)
    kv_shape = (1, -1, WINDOW, N_KV_HEADS, HEAD_DIM)
    mem_shape = (depth, batch, N_MEM, MEM_HEADS, MEM_HEAD_DIM)
    return (y_prompt.reshape(batch, seq, d),
            y_sample.reshape(dec_batch, dec_seq, d),
            conv_p[None],
            conv_s[None],
            win_p[..., :KV_WIDTH].reshape(kv_shape),
            win_p[..., KV_WIDTH:].reshape(kv_shape),
            win_k_s.reshape(kv_shape),
            win_v_s.reshape(kv_shape),
            mk.reshape(mem_shape),
            mv.reshape(mem_shape))
```

```python
import functools

import jax
import jax.numpy as jnp
from jax import lax
from jax.experimental import pallas as pl
from jax.experimental.pallas import tpu as pltpu

F32 = jnp.float32
BF16 = jnp.bfloat16

D_MODEL = 2048
N_MEM = 256
MEM_HEADS = 4
MEM_WIDTH = D_MODEL // 4
MEM_HEAD_DIM = MEM_WIDTH // MEM_HEADS
TOKEN_WIDTH = D_MODEL - MEM_WIDTH
CONV_WIDTH = 3
WINDOW = 128
HEAD_DIM = 64
N_HEADS = TOKEN_WIDTH // HEAD_DIM
N_KV_HEADS = 4
GROUP = N_HEADS // N_KV_HEADS
KV_WIDTH = N_KV_HEADS * HEAD_DIM
EPS = 1e-6

V7X_VMEM_BYTES = 64 * 1024 * 1024
V7X_SUBLANES = 8
V7X_MXU_DEPTH = 256

PROMPT_TILE = 1024
COL_TILE = 512
DOWN_K_TILE = 1408
DOWN_X_CHUNK = 512
FFN_ROW_TILE = 2048
FFN_COL_TILE = 512
MXU_ROWS = 1024
OUT_ROWS = 512
CONV_ROWS = 512
SWA_ROWS = 256
SCORE_LOOKAHEAD = 3
SAMPLE_LOOKAHEAD = 4
SAMPLE_GROUP = 8
CARRY_ROWS = V7X_SUBLANES


def _nbytes(shape, dtype):
    n = 1
    for s in shape:
        n *= s
    return n * jnp.dtype(dtype).itemsize


def _vmem_limit(block_bytes, scratch_bytes):
    need = 2 * block_bytes + scratch_bytes
    return int(min(need + max(need // 4, 8 << 20), V7X_VMEM_BYTES - (6 << 20)))


def _rmsnorm(x, g):
    r = lax.rsqrt(jnp.mean(x * x, axis=-1, keepdims=True) + EPS)
    return (x * r) * g


def _dot(a, b):
    return jnp.dot(a, b, preferred_element_type=F32)


def _dot_nt(a, b):
    return lax.dot_general(a, b, (((1,), (1,)), ((), ())), preferred_element_type=F32)


def _prompt_rows_map(i, j):
    return (i, 0)


def _prompt_tile_map(i, j):
    return (i, j)


def _sample_tile_map(n_pt):
    return lambda i, j: (0, jnp.where(i == n_pt - 1, j, 0))


def _const_map(i, j):
    return (0, 0)


def _norm_rows_kernel(xp_ref, xs_ref, g_ref, hp_ref, hs_ref, *, n_pt):
    hp_ref[...] = _rmsnorm(xp_ref[...], g_ref[...]).astype(BF16)

    @pl.when(pl.program_id(0) == n_pt - 1)
    def _():
        hs_ref[...] = _rmsnorm(xs_ref[...], g_ref[...]).astype(BF16)


def _norm_rows(xp, xs, g, *, layer):
    rp, d = xp.shape
    rs = xs.shape[0]
    tm = OUT_ROWS
    n_pt = rp // tm
    blocks = _nbytes((tm, d), F32) + _nbytes((tm, d), BF16) + _nbytes((rs, d), F32) + _nbytes((rs, d), BF16)
    return pl.pallas_call(
        functools.partial(_norm_rows_kernel, n_pt=n_pt),
        grid=(n_pt,),
        in_specs=[
            pl.BlockSpec((tm, d), lambda s: (s, 0)),
            pl.BlockSpec((rs, d), lambda s: (0, 0)),
            pl.BlockSpec((None, 1, d), lambda s: (layer, 0, 0)),
        ],
        out_specs=[
            pl.BlockSpec((tm, d), lambda s: (s, 0)),
            pl.BlockSpec((rs, d), lambda s: (0, 0)),
        ],
        out_shape=[jax.ShapeDtypeStruct((rp, d), BF16), jax.ShapeDtypeStruct((rs, d), BF16)],
        compiler_params=pltpu.CompilerParams(
            dimension_semantics=("arbitrary",),
            vmem_limit_bytes=_vmem_limit(blocks, 0)),
        name="norm_rows",
    )(xp, xs, g)


def _swiglu(h, wg, wu):
    gate = _dot(h, wg)
    up = _dot(h, wu)
    return (gate * jax.nn.sigmoid(gate) * up).astype(BF16)


def _ffn_up_kernel(hp_ref, hs_ref, wg_ref, wu_ref, ap_ref, as_ref, *, n_pt):
    wg, wu = wg_ref[...].astype(BF16), wu_ref[...].astype(BF16)
    for r in range(0, hp_ref.shape[0], MXU_ROWS):
        ap_ref[r:r + MXU_ROWS, :] = _swiglu(hp_ref[r:r + MXU_ROWS, :], wg, wu)

    @pl.when(pl.program_id(0) == n_pt - 1)
    def _():
        as_ref[...] = _swiglu(hs_ref[...], wg_ref[...].astype(BF16), wu_ref[...].astype(BF16))


def _ffn_up(hp, hs, wg, wu, *, layer):
    rp, d = hp.shape
    rs = hs.shape[0]
    n = wg.shape[2]
    tm, tn = FFN_ROW_TILE, FFN_COL_TILE
    n_pt, n_j = rp // tm, n // tn
    blocks = (_nbytes((tm, d), BF16) + _nbytes((rs, d), BF16) + 2 * _nbytes((d, tn), wg.dtype)
              + _nbytes((tm, tn), BF16) + _nbytes((rs, tn), BF16))
    temps = 2 * _nbytes((d, tn), BF16) + 3 * _nbytes((tm, tn), F32)
    w_spec = pl.BlockSpec((None, d, tn), lambda i, j: (layer, 0, j))
    return pl.pallas_call(
        functools.partial(_ffn_up_kernel, n_pt=n_pt),
        grid=(n_pt, n_j),
        in_specs=[
            pl.BlockSpec((tm, d), _prompt_rows_map),
            pl.BlockSpec((rs, d), _const_map),
            w_spec,
            w_spec,
        ],
        out_specs=[
            pl.BlockSpec((tm, tn), _prompt_tile_map),
            pl.BlockSpec((rs, tn), _sample_tile_map(n_pt)),
        ],
        out_shape=[jax.ShapeDtypeStruct((rp, n), BF16), jax.ShapeDtypeStruct((rs, n), BF16)],
        compiler_params=pltpu.CompilerParams(
            dimension_semantics=("arbitrary", "arbitrary"),
            vmem_limit_bytes=_vmem_limit(blocks, temps)),
        name="ffn_up",
    )(hp, hs, wg, wu)


def _out_proj_kernel(ap_ref, as_ref, w_ref, xp_ref, xs_ref, g_ref, op_ref, os_ref, hp_ref, hs_ref, *, n_pt):
    x = xp_ref[...] + _dot(ap_ref[...], w_ref[...])
    op_ref[...] = x
    hp_ref[...] = _rmsnorm(x, g_ref[...]).astype(BF16)

    @pl.when(pl.program_id(0) == n_pt - 1)
    def _():
        x = xs_ref[...] + _dot(as_ref[...], w_ref[...])
        os_ref[...] = x
        hs_ref[...] = _rmsnorm(x, g_ref[...]).astype(BF16)


def _out_proj(ap, as_, w, xp, xs, g, *, layer):
    rp, k = ap.shape
    rs = as_.shape[0]
    n = w.shape[1]
    tm = OUT_ROWS
    n_pt = rp // tm
    rows_map = lambda s: (s, 0)
    const = lambda s: (0, 0)
    once = dict(pipeline_mode=pl.Buffered(1))
    blocks = _nbytes((tm, k), BF16) + 2 * _nbytes((tm, n), F32) + _nbytes((tm, n), BF16)
    resident = (_nbytes((k, n), BF16) + _nbytes((rs, k), BF16) + _nbytes((rs, n), F32)
                + 2 * (_nbytes((rs, n), F32) + _nbytes((rs, n), BF16)))
    return pl.pallas_call(
        functools.partial(_out_proj_kernel, n_pt=n_pt),
        grid=(n_pt,),
        in_specs=[
            pl.BlockSpec((tm, k), rows_map),
            pl.BlockSpec((rs, k), const, **once),
            pl.BlockSpec((k, n), const, **once),
            pl.BlockSpec((tm, n), rows_map),
            pl.BlockSpec((rs, n), const, **once),
            pl.BlockSpec((None, 1, n), lambda s: (layer, 0, 0)),
        ],
        out_specs=[
            pl.BlockSpec((tm, n), rows_map),
            pl.BlockSpec((rs, n), const),
            pl.BlockSpec((tm, n), rows_map),
            pl.BlockSpec((rs, n), const),
        ],
        out_shape=[jax.ShapeDtypeStruct((rp, n), F32), jax.ShapeDtypeStruct((rs, n), F32),
                   jax.ShapeDtypeStruct((rp, n), BF16), jax.ShapeDtypeStruct((rs, n), BF16)],
        compiler_params=pltpu.CompilerParams(
            dimension_semantics=("arbitrary",),
            vmem_limit_bytes=_vmem_limit(blocks, resident + 2 * _nbytes((tm, n), F32))),
        name="out_proj",
    )(ap, as_, w, xp, xs, g)


def _matmul_kernel(hp_ref, hs_ref, w_ref, zp_ref, zs_ref, *, n_pt):
    w = w_ref[...].astype(BF16)
    for r in range(0, hp_ref.shape[0], MXU_ROWS):
        zp_ref[r:r + MXU_ROWS, :] = _dot(hp_ref[r:r + MXU_ROWS, :], w).astype(zp_ref.dtype)

    @pl.when(pl.program_id(0) == n_pt - 1)
    def _():
        zs_ref[...] = _dot(hs_ref[...], w_ref[...].astype(BF16)).astype(zs_ref.dtype)


def _matmul(hp, hs, w, *, w_layer):
    rp, d = hp.shape
    rs = hs.shape[0]
    n = w.shape[2]
    tm, tn = FFN_ROW_TILE, COL_TILE
    n_pt, n_j = rp // tm, n // tn
    blocks = (_nbytes((tm, d), BF16) + _nbytes((rs, d), BF16) + _nbytes((d, tn), F32)
              + _nbytes((tm, tn), BF16) + _nbytes((rs, tn), BF16))
    temps = _nbytes((d, tn), BF16) + _nbytes((tm, tn), F32)
    return pl.pallas_call(
        functools.partial(_matmul_kernel, n_pt=n_pt),
        grid=(n_pt, n_j),
        in_specs=[
            pl.BlockSpec((tm, d), _prompt_rows_map),
            pl.BlockSpec((rs, d), _const_map),
            pl.BlockSpec((None, d, tn), lambda i, j: (w_layer, 0, j)),
        ],
        out_specs=[
            pl.BlockSpec((tm, tn), _prompt_tile_map),
            pl.BlockSpec((rs, tn), _sample_tile_map(n_pt)),
        ],
        out_shape=[jax.ShapeDtypeStruct((rp, n), BF16), jax.ShapeDtypeStruct((rs, n), BF16)],
        compiler_params=pltpu.CompilerParams(
            dimension_semantics=("arbitrary", "arbitrary"),
            vmem_limit_bytes=_vmem_limit(blocks, temps)),
        name="matmul",
    )(hp, hs, w)


def _ffn_down_kernel(ap_ref, as_ref, w_ref, xp_ref, xs_ref, g_ref, *refs, n_pt, n_k, n_xc, final):
    i, k = pl.program_id(0), pl.program_id(1)
    if final:
        op_ref, os_ref, apl_ref, asl_ref, wl_ref = refs
    else:
        op_ref, os_ref, hp_ref, hs_ref, apl_ref, asl_ref, wl_ref = refs
    xc = xp_ref.shape[1]
    kf = (ap_ref.shape[1] // V7X_MXU_DEPTH) * V7X_MXU_DEPTH
    even = k % 2 == 0

    def whole_passes(a_ref):
        return _dot(a_ref[:, :kf], w_ref[:kf, :])

    def with_stash(a_ref, al_ref):
        return _dot(jnp.concatenate([al_ref[...], a_ref[...]], axis=1),
                    jnp.concatenate([wl_ref[...], w_ref[...]], axis=0))

    @pl.when(k == 0)
    def _():
        op_ref[...] = whole_passes(ap_ref)

    @pl.when(jnp.logical_and(k > 0, even))
    def _():
        op_ref[...] = op_ref[...] + whole_passes(ap_ref)

    @pl.when(jnp.logical_not(even))
    def _():
        op_ref[...] = op_ref[...] + with_stash(ap_ref, apl_ref)

    for c in range(n_xc):
        @pl.when(k == c)
        def _():
            op_ref[:, c * xc:(c + 1) * xc] = op_ref[:, c * xc:(c + 1) * xc] + xp_ref[...]

    @pl.when(k == n_k - 1)
    def _():
        if final:
            op_ref[...] = _rmsnorm(op_ref[...], g_ref[...])
        else:
            hp_ref[...] = _rmsnorm(op_ref[...], g_ref[...]).astype(BF16)

    @pl.when(i == n_pt - 1)
    def _():
        @pl.when(k == 0)
        def _():
            os_ref[...] = xs_ref[...] + whole_passes(as_ref)

        @pl.when(jnp.logical_and(k > 0, even))
        def _():
            os_ref[...] = os_ref[...] + whole_passes(as_ref)

        @pl.when(jnp.logical_not(even))
        def _():
            os_ref[...] = os_ref[...] + with_stash(as_ref, asl_ref)

        @pl.when(even)
        def _():
            asl_ref[...] = as_ref[:, kf:]

        @pl.when(k == n_k - 1)
        def _():
            if final:
                os_ref[...] = _rmsnorm(os_ref[...], g_ref[...])
            else:
                hs_ref[...] = _rmsnorm(os_ref[...], g_ref[...]).astype(BF16)

    @pl.when(even)
    def _():
        apl_ref[...] = ap_ref[:, kf:]
        wl_ref[...] = w_ref[kf:, :]


def _ffn_down(ap, as_, w, xp, xs, g, *, w_layer, g_layer, final):
    rp, kdim = ap.shape
    rs = as_.shape[0]
    n = w.shape[2]
    tm, tk, xc = PROMPT_TILE, DOWN_K_TILE, DOWN_X_CHUNK
    n_pt, n_k, n_xc = rp // tm, kdim // tk, n // xc
    k_left = tk % V7X_MXU_DEPTH
    assert n_xc <= n_k and n_k % 2 == 0 and 2 * k_left == V7X_MXU_DEPTH and w.dtype == BF16
    rows = lambda i, k: (i, 0)
    blocks = (_nbytes((tm, tk), BF16) + _nbytes((rs, tk), BF16) + _nbytes((tk, n), w.dtype)
              + _nbytes((tm, xc), F32) + _nbytes((rs, n), F32)
              + _nbytes((tm, n), F32) + _nbytes((rs, n), F32))
    out_specs = [pl.BlockSpec((tm, n), rows), pl.BlockSpec((rs, n), _const_map)]
    out_shape = [jax.ShapeDtypeStruct((rp, n), F32), jax.ShapeDtypeStruct((rs, n), F32)]
    if not final:
        blocks += _nbytes((tm, n), BF16) + _nbytes((rs, n), BF16)
        out_specs += [pl.BlockSpec((tm, n), rows), pl.BlockSpec((rs, n), _const_map)]
        out_shape += [jax.ShapeDtypeStruct((rp, n), BF16), jax.ShapeDtypeStruct((rs, n), BF16)]
    return pl.pallas_call(
        functools.partial(_ffn_down_kernel, n_pt=n_pt, n_k=n_k, n_xc=n_xc, final=final),
        grid=(n_pt, n_k),
        in_specs=[
            pl.BlockSpec((tm, tk), lambda i, k: (i, k)),
            pl.BlockSpec((rs, tk), lambda i, k: (0, jnp.where(i == n_pt - 1, k, 0))),
            pl.BlockSpec((None, tk, n), lambda i, k: (w_layer, k, 0)),
            pl.BlockSpec((tm, xc), lambda i, k: (i, jnp.minimum(k, n_xc - 1))),
            pl.BlockSpec((rs, n), _const_map),
            pl.BlockSpec((None, 1, n), lambda i, k: (g_layer, 0, 0)),
        ],
        out_specs=out_specs,
        out_shape=out_shape,
        scratch_shapes=[pltpu.VMEM((tm, k_left), BF16), pltpu.VMEM((rs, k_left), BF16),
                        pltpu.VMEM((k_left, n), BF16)],
        compiler_params=pltpu.CompilerParams(
            dimension_semantics=("arbitrary", "arbitrary"),
            vmem_limit_bytes=_vmem_limit(blocks, _nbytes((tk, n), BF16))),
        name="ffn_down",
    )(ap, as_, w, xp, xs, g)


def _mem_kv_kernel(x_ref, g_ref, w_ref, k_ref, v_ref):
    h = _rmsnorm(x_ref[...], g_ref[...]).astype(BF16)
    kv = _dot(h, w_ref[...].astype(BF16))
    k_ref[...] = kv[:, :MEM_WIDTH]
    v_ref[...] = kv[:, MEM_WIDTH:]


def _mem_kv(mem, g, w):
    rows, d = mem.shape
    depth = w.shape[0]
    tm = 512
    out = jax.ShapeDtypeStruct((depth, rows, MEM_WIDTH), F32)
    blocks = (_nbytes((tm, d), F32) + _nbytes((d, 2 * MEM_WIDTH), F32) + 2 * _nbytes((tm, MEM_WIDTH), F32))
    return pl.pallas_call(
        _mem_kv_kernel,
        grid=(depth, rows // tm),
        in_specs=[
            pl.BlockSpec((tm, d), lambda l, i: (i, 0)),
            pl.BlockSpec((None, 1, d), lambda l, i: (l, 0, 0)),
            pl.BlockSpec((None, d, 2 * MEM_WIDTH), lambda l, i: (l, 0, 0)),
        ],
        out_specs=[
            pl.BlockSpec((None, tm, MEM_WIDTH), lambda l, i: (l, i, 0)),
            pl.BlockSpec((None, tm, MEM_WIDTH), lambda l, i: (l, i, 0)),
        ],
        out_shape=[out, out],
        compiler_params=pltpu.CompilerParams(
            dimension_semantics=("arbitrary", "arbitrary"),
            vmem_limit_bytes=_vmem_limit(blocks, _nbytes((d, 2 * MEM_WIDTH), BF16))),
        name="mem_kv",
    )(mem, g, w)


def _cross_scores(q, k):
    return _dot_nt(q, k) * (MEM_HEAD_DIM ** -0.5)


def _cross_values(s, v):
    e = jnp.exp(s - jnp.max(s, axis=-1, keepdims=True)).astype(BF16)
    return _dot(e, v) / _dot(e, jnp.ones(v.shape, BF16))


def _run_ahead(n_items, lookahead, first, second):
    pending = {}
    for i in range(n_items + lookahead):
        if i < n_items:
            pending[i] = first(i)
        if i >= lookahead:
            second(i - lookahead, pending.pop(i - lookahead))


def _conv_prompt_kernel(z_ref, mk_ref, mv_ref, cw_ref, wd_ref, mix_ref, st_ref, wdb_ref, ext_ref, *, tiles_per_seq):
    s = pl.program_id(0)
    tq = z_ref.shape[0]
    wdb_ref[...] = wd_ref[...].astype(BF16)

    @pl.when(s % tiles_per_seq == 0)
    def _():
        ext_ref[0:CARRY_ROWS, :] = jnp.zeros((CARRY_ROWS, TOKEN_WIDTH), F32)

    c = z_ref[:, TOKEN_WIDTH:2 * TOKEN_WIDTH].astype(F32)
    u = z_ref[:, 2 * TOKEN_WIDTH:3 * TOKEN_WIDTH].astype(F32)
    cu = c * u
    ext_ref[CARRY_ROWS:CARRY_ROWS + tq, :] = cu
    conv = (cw_ref[0:1, :] * ext_ref[CARRY_ROWS - 2:CARRY_ROWS - 2 + tq, :]
            + cw_ref[1:2, :] * ext_ref[CARRY_ROWS - 1:CARRY_ROWS - 1 + tq, :]
            + cw_ref[2:3, :] * cu)
    b = z_ref[:, 0:TOKEN_WIDTH].astype(F32)
    mix_ref[:, 0:TOKEN_WIDTH] = (b * conv).astype(BF16)
    st_ref[...] = ext_ref[CARRY_ROWS + tq - 2:CARRY_ROWS + tq, :]
    ext_ref[0:CARRY_ROWS, :] = ext_ref[tq:tq + CARRY_ROWS, :]

    def scores(h):
        lo, hi = h * MEM_HEAD_DIM, (h + 1) * MEM_HEAD_DIM
        return _cross_scores(z_ref[:, 3 * TOKEN_WIDTH + lo:3 * TOKEN_WIDTH + hi], mk_ref[:, lo:hi].astype(BF16))

    def finish(h, s):
        lo, hi = h * MEM_HEAD_DIM, (h + 1) * MEM_HEAD_DIM
        mix_ref[:, TOKEN_WIDTH + lo:TOKEN_WIDTH + hi] = _cross_values(s, mv_ref[:, lo:hi].astype(BF16)).astype(BF16)

    _run_ahead(MEM_HEADS, SCORE_LOOKAHEAD, scores, finish)


def _conv_prompt(z, mk, mv, conv_w, w_down, *, layer, batch, seq):
    rows, zc = z.shape
    tq = CONV_ROWS
    tiles_per_seq = seq // tq
    n_steps = batch * tiles_per_seq
    d_ff, d = w_down.shape[1:]
    slab = d_ff // n_steps
    assert slab * n_steps == d_ff and slab % (2 * V7X_SUBLANES) == 0
    blocks = (_nbytes((tq, zc), BF16) + 2 * _nbytes((N_MEM, MEM_WIDTH), F32) + _nbytes((tq, D_MODEL), BF16)
              + _nbytes((slab, d), F32) + _nbytes((slab, d), BF16))
    scratch = _nbytes((tq + CARRY_ROWS, TOKEN_WIDTH), F32)
    return pl.pallas_call(
        functools.partial(_conv_prompt_kernel, tiles_per_seq=tiles_per_seq),
        grid=(n_steps,),
        in_specs=[
            pl.BlockSpec((tq, zc), lambda s: (s, 0)),
            pl.BlockSpec((None, N_MEM, MEM_WIDTH), lambda s: (layer, s // tiles_per_seq, 0)),
            pl.BlockSpec((None, N_MEM, MEM_WIDTH), lambda s: (layer, s // tiles_per_seq, 0)),
            pl.BlockSpec((None, CONV_WIDTH, TOKEN_WIDTH), lambda s: (0, 0, 0)),
            pl.BlockSpec((None, slab, d), lambda s: (layer, s, 0)),
        ],
        out_specs=[
            pl.BlockSpec((tq, D_MODEL), lambda s: (s, 0)),
            pl.BlockSpec((None, CONV_WIDTH - 1, TOKEN_WIDTH), lambda s: (s // tiles_per_seq, 0, 0)),
            pl.BlockSpec((slab, d), lambda s: (s, 0)),
        ],
        out_shape=[
            jax.ShapeDtypeStruct((rows, D_MODEL), BF16),
            jax.ShapeDtypeStruct((batch, CONV_WIDTH - 1, TOKEN_WIDTH), F32),
            jax.ShapeDtypeStruct((d_ff, d), BF16),
        ],
        scratch_shapes=[pltpu.VMEM((tq + CARRY_ROWS, TOKEN_WIDTH), F32)],
        compiler_params=pltpu.CompilerParams(
            dimension_semantics=("arbitrary",),
            vmem_limit_bytes=_vmem_limit(blocks, scratch + 6 * _nbytes((tq, TOKEN_WIDTH), F32))),
        name="conv_prompt",
    )(z, mk, mv, conv_w, w_down)


def _conv_sample_kernel(z_ref, st_ref, mk_ref, mv_ref, cw_ref, wo_ref, mix_ref, nst_ref, wob_ref,
                        ext_ref, mixf_ref, *, dec_seq):
    t = dec_seq
    wob_ref[...] = wo_ref[...].astype(BF16)
    for n in range(SAMPLE_GROUP):
        r0, r1 = n * t, (n + 1) * t
        c = z_ref[r0:r1, TOKEN_WIDTH:2 * TOKEN_WIDTH].astype(F32)
        u = z_ref[r0:r1, 2 * TOKEN_WIDTH:3 * TOKEN_WIDTH].astype(F32)
        cu = c * u
        ext_ref[CARRY_ROWS - 2:CARRY_ROWS, :] = st_ref[n]
        ext_ref[CARRY_ROWS:CARRY_ROWS + t, :] = cu
        conv = (cw_ref[0:1, :] * ext_ref[CARRY_ROWS - 2:CARRY_ROWS - 2 + t, :]
                + cw_ref[1:2, :] * ext_ref[CARRY_ROWS - 1:CARRY_ROWS - 1 + t, :]
                + cw_ref[2:3, :] * cu)
        b = z_ref[r0:r1, 0:TOKEN_WIDTH].astype(F32)
        mixf_ref[r0:r1, 0:TOKEN_WIDTH] = b * conv
        nst_ref[n] = ext_ref[CARRY_ROWS + t - 2:CARRY_ROWS + t, :]
    _sample_cross_attention(z_ref, mk_ref, mv_ref, mixf_ref, 3 * TOKEN_WIDTH, t)
    mix_ref[...] = mixf_ref[...].astype(BF16)


def _sample_cross_attention(z_ref, mk_ref, mv_ref, mixf_ref, qm_off, t):
    def scores(i):
        n, h = divmod(i, MEM_HEADS)
        q = z_ref[n * t:(n + 1) * t, qm_off + h * MEM_HEAD_DIM:qm_off + (h + 1) * MEM_HEAD_DIM]
        return _cross_scores(q, mk_ref[n, pl.ds(h, N_MEM, stride=MEM_HEADS), :].astype(BF16))

    def finish(i, s):
        n, h = divmod(i, MEM_HEADS)
        o = _cross_values(s, mv_ref[n, pl.ds(h, N_MEM, stride=MEM_HEADS), :].astype(BF16))
        mixf_ref[n * t:(n + 1) * t, TOKEN_WIDTH + h * MEM_HEAD_DIM:TOKEN_WIDTH + (h + 1) * MEM_HEAD_DIM] = o

    _run_ahead(SAMPLE_GROUP * MEM_HEADS, SAMPLE_LOOKAHEAD, scores, finish)


def _conv_sample(z, state, mem_k, mem_v, conv_w, w_out, *, layer, dec_seq):
    rows, zc = z.shape
    dec_batch = state.shape[1]
    g = SAMPLE_GROUP
    gr = g * dec_seq
    n_steps = dec_batch // g
    wk, wn = w_out.shape[1:]
    slab = wk // n_steps
    assert slab * n_steps == wk and slab % (2 * V7X_SUBLANES) == 0
    blocks = (_nbytes((gr, zc), BF16) + 2 * _nbytes((g, CONV_WIDTH - 1, TOKEN_WIDTH), F32)
              + 2 * _nbytes((g, N_MEM, V7X_SUBLANES, MEM_HEAD_DIM), F32) + _nbytes((gr, D_MODEL), BF16)
              + _nbytes((slab, wn), F32) + _nbytes((slab, wn), BF16))
    scratch = _nbytes((2 * CARRY_ROWS, TOKEN_WIDTH), F32) + _nbytes((gr, D_MODEL), F32)
    return pl.pallas_call(
        functools.partial(_conv_sample_kernel, dec_seq=dec_seq),
        grid=(n_steps,),
        in_specs=[
            pl.BlockSpec((gr, zc), lambda i: (i, 0)),
            pl.BlockSpec((None, g, CONV_WIDTH - 1, TOKEN_WIDTH), lambda i: (0, i, 0, 0)),
            pl.BlockSpec((None, g, N_MEM * MEM_HEADS, MEM_HEAD_DIM), lambda i: (layer, i, 0, 0)),
            pl.BlockSpec((None, g, N_MEM * MEM_HEADS, MEM_HEAD_DIM), lambda i: (layer, i, 0, 0)),
            pl.BlockSpec((None, CONV_WIDTH, TOKEN_WIDTH), lambda i: (0, 0, 0)),
            pl.BlockSpec((None, slab, wn), lambda i: (0, i, 0)),
        ],
        out_specs=[
            pl.BlockSpec((gr, D_MODEL), lambda i: (i, 0)),
            pl.BlockSpec((g, CONV_WIDTH - 1, TOKEN_WIDTH), lambda i: (i, 0, 0)),
            pl.BlockSpec((slab, wn), lambda i: (i, 0)),
        ],
        out_shape=[
            jax.ShapeDtypeStruct((rows, D_MODEL), BF16),
            jax.ShapeDtypeStruct((dec_batch, CONV_WIDTH - 1, TOKEN_WIDTH), F32),
            jax.ShapeDtypeStruct((wk, wn), BF16),
        ],
        scratch_shapes=[pltpu.VMEM((2 * CARRY_ROWS, TOKEN_WIDTH), F32), pltpu.VMEM((gr, D_MODEL), F32)],
        compiler_params=pltpu.CompilerParams(
            dimension_semantics=("arbitrary",),
            vmem_limit_bytes=_vmem_limit(blocks, scratch)),
        name="conv_sample",
    )(z, state, mem_k, mem_v, conv_w, w_out)


def _band_scores(q, k):
    return _dot_nt(q * (HEAD_DIM ** -0.5), k)


def _band_probs(s, sink, upper, upper_visible):
    s = jnp.where(upper_visible, s[:, :WINDOW], jnp.where(upper, -jnp.inf, s[:, WINDOW:]))
    m = jnp.maximum(jnp.max(s, axis=-1, keepdims=True), sink)
    e = jnp.exp(s - m)
    e = jnp.concatenate([jnp.where(upper, e, 0.0), jnp.where(upper, 0.0, e)], axis=1).astype(BF16)
    return e, jnp.exp(sink - m)


def _band_values(probs, v, ones):
    e, sink_term = probs
    return _dot(e, v) / (_dot(e, ones) + sink_term)


def _swa_prompt_kernel(sink_ref, zq_ref, zp_ref, mk_ref, mv_ref, *refs, blocks_per_seq, cast_steps):
    n_w = (len(refs) - 1) // 2
    w_refs, mix_ref, wb_refs = refs[:n_w], refs[n_w], refs[n_w + 1:]
    s = pl.program_id(0)

    for w_ref, wb_ref, n_cast in zip(w_refs, wb_refs, cast_steps):
        @pl.when(s < n_cast)
        def _():
            wb_ref[...] = w_ref[...].astype(BF16)

    k_off = TOKEN_WIDTH
    v_off = TOKEN_WIDTH + KV_WIDTH
    qm_off = TOKEN_WIDTH + 2 * KV_WIDTH
    n_blocks = zq_ref.shape[0] // WINDOW
    first_has_prev = (s % (blocks_per_seq // n_blocks)) > 0
    row = lax.broadcasted_iota(jnp.int32, (WINDOW, WINDOW), 0)
    col = lax.broadcasted_iota(jnp.int32, (WINDOW, WINDOW), 1)
    upper = col > row
    upper_first = jnp.logical_and(upper, first_has_prev)
    ones = jnp.ones((2 * WINDOW, HEAD_DIM), BF16)

    def block_rows(b):
        return slice(b * WINDOW, (b + 1) * WINDOW)

    @functools.lru_cache(maxsize=None)
    def window(b, kh, col_prev, col_cur):
        lo, hi = kh * HEAD_DIM, (kh + 1) * HEAD_DIM
        prev = (zp_ref[:, col_prev + lo:col_prev + hi] if b == 0
                else zq_ref[block_rows(b - 1), col_cur + lo:col_cur + hi])
        return jnp.concatenate([prev, zq_ref[block_rows(b), col_cur + lo:col_cur + hi]], axis=0)

    def band_scores(i):
        b, h = divmod(i, N_HEADS)
        return _band_scores(zq_ref[block_rows(b), h * HEAD_DIM:(h + 1) * HEAD_DIM], window(b, h // GROUP, 0, k_off))

    def band_finish(i, s):
        b, h = divmod(i, N_HEADS)
        p = _band_probs(s, sink_ref[h], upper, upper_first if b == 0 else upper)
        o = _band_values(p, window(b, h // GROUP, KV_WIDTH, v_off), ones)
        mix_ref[block_rows(b), h * HEAD_DIM:(h + 1) * HEAD_DIM] = o.astype(BF16)

    def cross_scores(h):
        lo, hi = h * MEM_HEAD_DIM, (h + 1) * MEM_HEAD_DIM
        return _cross_scores(zq_ref[:, qm_off + lo:qm_off + hi], mk_ref[:, lo:hi].astype(BF16))

    def cross_finish(h, s):
        lo, hi = h * MEM_HEAD_DIM, (h + 1) * MEM_HEAD_DIM
        mix_ref[:, TOKEN_WIDTH + lo:TOKEN_WIDTH + hi] = _cross_values(s, mv_ref[:, lo:hi].astype(BF16)).astype(BF16)

    _run_ahead(n_blocks * N_HEADS, SCORE_LOOKAHEAD, band_scores, band_finish)
    _run_ahead(MEM_HEADS, SCORE_LOOKAHEAD, cross_scores, cross_finish)


def _swa_prompt(z, mk, mv, sinks, weights, *, layer, batch, seq):
    rows, zc = z.shape
    tq = SWA_ROWS
    blocks_per_seq = seq // WINDOW
    steps_per_seq = seq // tq
    blocks_per_step = tq // WINDOW
    n_steps = batch * steps_per_seq
    kv_col_block = TOKEN_WIDTH // (2 * KV_WIDTH)
    bf16_rows = 2 * V7X_SUBLANES
    blocks = (_nbytes((tq, zc), BF16) + _nbytes((WINDOW, 2 * KV_WIDTH), BF16)
              + 2 * _nbytes((N_MEM, MEM_WIDTH), F32) + _nbytes((tq, D_MODEL), BF16))
    w_in_specs, w_out_specs, w_out_shapes, cast_steps = [], [], [], []
    for w in weights:
        r, c = w.shape[1:]
        n_cast = n_steps if r % (n_steps * bf16_rows) == 0 else n_steps // 2
        slab = r // n_cast
        assert slab * n_cast == r and slab % bf16_rows == 0
        blocks += _nbytes((slab, c), F32) + _nbytes((slab, c), BF16)
        w_in_specs.append(pl.BlockSpec((None, slab, c), lambda s, n=n_cast: (layer, jnp.minimum(s, n - 1), 0)))
        w_out_specs.append(pl.BlockSpec((slab, c), lambda s, n=n_cast: (jnp.minimum(s, n - 1), 0)))
        w_out_shapes.append(jax.ShapeDtypeStruct((r, c), BF16))
        cast_steps.append(n_cast)
    return pl.pallas_call(
        functools.partial(_swa_prompt_kernel, blocks_per_seq=blocks_per_seq, cast_steps=tuple(cast_steps)),
        grid=(n_steps,),
        in_specs=[
            pl.BlockSpec(memory_space=pltpu.SMEM),
            pl.BlockSpec((tq, zc), lambda s: (s, 0)),
            pl.BlockSpec((WINDOW, 2 * KV_WIDTH), lambda s: (jnp.maximum(s * blocks_per_step - 1, 0), kv_col_block)),
            pl.BlockSpec((None, N_MEM, MEM_WIDTH), lambda s: (layer, s // steps_per_seq, 0)),
            pl.BlockSpec((None, N_MEM, MEM_WIDTH), lambda s: (layer, s // steps_per_seq, 0)),
        ] + w_in_specs,
        out_specs=[pl.BlockSpec((tq, D_MODEL), lambda s: (s, 0))] + w_out_specs,
        out_shape=[jax.ShapeDtypeStruct((rows, D_MODEL), BF16)] + w_out_shapes,
        compiler_params=pltpu.CompilerParams(
            dimension_semantics=("arbitrary",),
            vmem_limit_bytes=_vmem_limit(blocks, 0)),
        name="swa_prompt",
    )(sinks, z, z, mk, mv, *weights)


def _swa_sample_kernel(sink_ref, z_ref, ck_ref, cv_ref, mk_ref, mv_ref, wo_ref,
                       mix_ref, nk_ref, nv_ref, wob_ref, knew_ref, vnew_ref, mixf_ref, *, dec_seq):
    t = dec_seq
    wob_ref[...] = wo_ref[...].astype(BF16)
    k_off = TOKEN_WIDTH
    v_off = TOKEN_WIDTH + KV_WIDTH
    qm_off = TOKEN_WIDTH + 2 * KV_WIDTH
    rows = GROUP * t
    qi = lax.broadcasted_iota(jnp.int32, (rows, WINDOW), 0) % t
    col = lax.broadcasted_iota(jnp.int32, (rows, WINDOW), 1)
    upper = col > qi
    knew_ref[...] = jnp.zeros(knew_ref.shape, F32)
    vnew_ref[...] = jnp.zeros(vnew_ref.shape, F32)
    for n in range(SAMPLE_GROUP):
        r0, r1 = n * t, (n + 1) * t
        k_new = z_ref[r0:r1, k_off:k_off + KV_WIDTH].astype(F32)
        v_new = z_ref[r0:r1, v_off:v_off + KV_WIDTH].astype(F32)
        knew_ref[n, 0:t, :] = k_new
        vnew_ref[n, 0:t, :] = v_new
        nk_ref[n, 0:WINDOW - t, :] = ck_ref[n, t:WINDOW, :]
        nv_ref[n, 0:WINDOW - t, :] = cv_ref[n, t:WINDOW, :]
        nk_ref[n, WINDOW - t:WINDOW, :] = k_new
        nv_ref[n, WINDOW - t:WINDOW, :] = v_new

    def scores(i):
        n, kh = divmod(i, N_KV_HEADS)
        lo, hi = kh * HEAD_DIM, (kh + 1) * HEAD_DIM
        k = jnp.concatenate([ck_ref[n, :, lo:hi], knew_ref[n, :, lo:hi]], axis=0).astype(BF16)
        q = jnp.concatenate(
            [z_ref[n * t:(n + 1) * t, (kh * GROUP + g) * HEAD_DIM:(kh * GROUP + g + 1) * HEAD_DIM].astype(F32)
             for g in range(GROUP)], axis=0).astype(BF16)
        return _band_scores(q, k)

    def finish(i, s):
        n, kh = divmod(i, N_KV_HEADS)
        lo, hi = kh * HEAD_DIM, (kh + 1) * HEAD_DIM
        v = jnp.concatenate([cv_ref[n, :, lo:hi], vnew_ref[n, :, lo:hi]], axis=0).astype(BF16)
        sink = jnp.concatenate(
            [jnp.full((t, 1), sink_ref[kh * GROUP + g], F32) for g in range(GROUP)], axis=0)
        o = _band_values(_band_probs(s, sink, upper, upper), v, jnp.ones((2 * WINDOW, HEAD_DIM), BF16))
        for g in range(GROUP):
            h = kh * GROUP + g
            mixf_ref[n * t:(n + 1) * t, h * HEAD_DIM:(h + 1) * HEAD_DIM] = o[g * t:(g + 1) * t, :]

    _run_ahead(SAMPLE_GROUP * N_KV_HEADS, SAMPLE_LOOKAHEAD, scores, finish)
    _sample_cross_attention(z_ref, mk_ref, mv_ref, mixf_ref, qm_off, t)
    mix_ref[...] = mixf_ref[...].astype(BF16)


def _swa_sample(z, cache_k, cache_v, mem_k, mem_v, sinks, w_out, *, layer, dec_seq):
    rows, zc = z.shape
    dec_batch = cache_k.shape[0]
    g = SAMPLE_GROUP
    gr = g * dec_seq
    n_steps = dec_batch // g
    wk, wn = w_out.shape[1:]
    slab = wk // n_steps
    assert slab * n_steps == wk and slab % (2 * V7X_SUBLANES) == 0
    win = jax.ShapeDtypeStruct((dec_batch, WINDOW, KV_WIDTH), F32)
    blocks = (_nbytes((gr, zc), BF16) + 4 * _nbytes((g, WINDOW, KV_WIDTH), F32)
              + 2 * _nbytes((g, N_MEM, V7X_SUBLANES, MEM_HEAD_DIM), F32) + _nbytes((gr, D_MODEL), BF16)
              + _nbytes((slab, wn), F32) + _nbytes((slab, wn), BF16))
    scratch = 2 * _nbytes((g, WINDOW, KV_WIDTH), F32) + _nbytes((gr, D_MODEL), F32)
    return pl.pallas_call(
        functools.partial(_swa_sample_kernel, dec_seq=dec_seq),
        grid=(n_steps,),
        in_specs=[
            pl.BlockSpec(memory_space=pltpu.SMEM),
            pl.BlockSpec((gr, zc), lambda i: (i, 0)),
            pl.BlockSpec((g, WINDOW, KV_WIDTH), lambda i: (i, 0, 0)),
            pl.BlockSpec((g, WINDOW, KV_WIDTH), lambda i: (i, 0, 0)),
            pl.BlockSpec((None, g, N_MEM * MEM_HEADS, MEM_HEAD_DIM), lambda i: (layer, i, 0, 0)),
            pl.BlockSpec((None, g, N_MEM * MEM_HEADS, MEM_HEAD_DIM), lambda i: (layer, i, 0, 0)),
            pl.BlockSpec((None, slab, wn), lambda i: (0, i, 0)),
        ],
        out_specs=[
            pl.BlockSpec((gr, D_MODEL), lambda i: (i, 0)),
            pl.BlockSpec((g, WINDOW, KV_WIDTH), lambda i: (i, 0, 0)),
            pl.BlockSpec((g, WINDOW, KV_WIDTH), lambda i: (i, 0, 0)),
            pl.BlockSpec((slab, wn), lambda i: (i, 0)),
        ],
        out_shape=[jax.ShapeDtypeStruct((rows, D_MODEL), BF16), win, win,
                   jax.ShapeDtypeStruct((wk, wn), BF16)],
        scratch_shapes=[pltpu.VMEM((g, WINDOW, KV_WIDTH), F32), pltpu.VMEM((g, WINDOW, KV_WIDTH), F32),
                        pltpu.VMEM((gr, D_MODEL), F32)],
        compiler_params=pltpu.CompilerParams(
            dimension_semantics=("arbitrary",),
            vmem_limit_bytes=_vmem_limit(blocks, scratch)),
        name="swa_sample",
    )(sinks, z, cache_k, cache_v, mem_k, mem_v, w_out)


def kernel(x_prompt, x_sample, mem_prompt, state_conv, cache_win_k, cache_win_v, cache_mem_k, cache_mem_v,
           norm_mix, norm_mem, w_mem_kv, norm_ffn, w_gate, w_up, w_down,
           conv_w_in, conv_w, conv_w_out, attn_w_in, attn_sinks, attn_w_out, norm_final):
    batch, seq, d = x_prompt.shape
    dec_batch, dec_seq, _ = x_sample.shape
    depth = norm_mix.shape[0]
    d_ff = w_gate.shape[2]
    prompt_rows = batch * seq
    sample_rows = dec_batch * dec_seq
    assert d == D_MODEL and depth == 2 and seq % CONV_ROWS == 0 and seq % SWA_ROWS == 0 and SWA_ROWS % WINDOW == 0
    assert prompt_rows % PROMPT_TILE == 0 and dec_batch % SAMPLE_GROUP == 0
    assert dec_seq == V7X_SUBLANES and d_ff % FFN_COL_TILE == 0 and d % COL_TILE == 0
    assert d_ff % DOWN_K_TILE == 0 and d % DOWN_X_CHUNK == 0
    assert prompt_rows % FFN_ROW_TILE == 0 and prompt_rows % sample_rows == 0

    xp = x_prompt.reshape(prompt_rows, d)
    xs = x_sample.reshape(sample_rows, d)
    mem = mem_prompt.reshape(batch * N_MEM, d)
    mem_k_s = cache_mem_k.reshape(depth, dec_batch, N_MEM * MEM_HEADS, MEM_HEAD_DIM)
    mem_v_s = cache_mem_v.reshape(depth, dec_batch, N_MEM * MEM_HEADS, MEM_HEAD_DIM)
    g_mix = norm_mix.reshape(depth, 1, d)
    g_ffn = norm_ffn.reshape(depth, 1, d)

    mk, mv = _mem_kv(mem, norm_mem.reshape(depth, 1, d), w_mem_kv)

    hp, hs = _norm_rows(xp, xs, g_mix, layer=0)
    zp, zs = _matmul(hp, hs, conv_w_in, w_layer=0)
    mix_s, conv_s, wo_bf = _conv_sample(zs, state_conv, mem_k_s, mem_v_s, conv_w, conv_w_out,
                                        layer=0, dec_seq=dec_seq)
    mix_p, conv_p, wd_bf = _conv_prompt(zp, mk, mv, conv_w, w_down, layer=0, batch=batch, seq=seq)
    xp, xs, hp, hs = _out_proj(mix_p, mix_s, wo_bf, xp, xs, g_ffn, layer=0)
    ap, as_ = _ffn_up(hp, hs, w_gate, w_up, layer=0)
    xp, xs, hp, hs = _ffn_down(ap, as_, wd_bf[None], xp, xs, g_mix, w_layer=0, g_layer=1, final=False)

    zp, zs = _matmul(hp, hs, attn_w_in, w_layer=0)
    sinks = attn_sinks[0]
    mix_s, win_k_s, win_v_s, wo_bf = _swa_sample(
        zs, cache_win_k[0].reshape(dec_batch, WINDOW, KV_WIDTH), cache_win_v[0].reshape(dec_batch, WINDOW, KV_WIDTH),
        mem_k_s, mem_v_s, sinks, attn_w_out, layer=1, dec_seq=dec_seq)
    mix_p, wd_bf, wg_bf, wu_bf = _swa_prompt(zp, mk, mv, sinks, [w_down, w_gate, w_up],
                                             layer=1, batch=batch, seq=seq)
    xp, xs, hp, hs = _out_proj(mix_p, mix_s, wo_bf, xp, xs, g_ffn, layer=1)
    ap, as_ = _ffn_up(hp, hs, wg_bf[None], wu_bf[None], layer=0)
    y_prompt, y_sample = _ffn_down(ap, as_, wd_bf[None], xp, xs, norm_final.reshape(1, 1, d),
                                   w_layer=0, g_layer=0, final=True)

    win_p = zp.reshape(batch, seq, -1)[:, seq - WINDOW:, TOKEN_WIDTH:TOKEN_WIDTH + 2 * KV_WIDTH].astype(F32)
    kv_shape = (1, -1, WINDOW, N_KV_HEADS, HEAD_DIM)
    mem_shape = (depth, batch, N_MEM, MEM_HEADS, MEM_HEAD_DIM)
    return (y_prompt.reshape(batch, seq, d),
            y_sample.reshape(dec_batch, dec_seq, d),
            conv_p[None],
            conv_s[None],
            win_p[..., :KV_WIDTH].reshape(kv_shape),
            win_p[..., KV_WIDTH:].reshape(kv_shape),
            win_k_s.reshape(kv_shape),
            win_v_s.reshape(kv_shape),
            mk.reshape(mem_shape),
            mv.reshape(mem_shape))
```

```python
import functools

import jax
import jax.numpy as jnp
from jax import lax
from jax.experimental import pallas as pl
from jax.experimental.pallas import tpu as pltpu

F32 = jnp.float32
BF16 = jnp.bfloat16

D_MODEL = 2048
N_MEM = 256
MEM_HEADS = 4
MEM_WIDTH = D_MODEL // 4
MEM_HEAD_DIM = MEM_WIDTH // MEM_HEADS
TOKEN_WIDTH = D_MODEL - MEM_WIDTH
CONV_WIDTH = 3
WINDOW = 128
HEAD_DIM = 64
N_HEADS = TOKEN_WIDTH // HEAD_DIM
N_KV_HEADS = 4
GROUP = N_HEADS // N_KV_HEADS
KV_WIDTH = N_KV_HEADS * HEAD_DIM
EPS = 1e-6

V7X_VMEM_BYTES = 64 * 1024 * 1024
V7X_SUBLANES = 8
V7X_MXU_DEPTH = 256

PROMPT_TILE = 1024
COL_TILE = 512
DOWN_K_TILE = 1408
DOWN_X_CHUNK = 512
FFN_ROW_TILE = 2048
FFN_COL_TILE = 512
MXU_ROWS = 1024
OUT_ROWS = 512
CONV_ROWS = 512
SWA_ROWS = 256
SCORE_LOOKAHEAD = 5
SAMPLE_LOOKAHEAD = 4
SAMPLE_GROUP = 8
CARRY_ROWS = V7X_SUBLANES


def _nbytes(shape, dtype):
    n = 1
    for s in shape:
        n *= s
    return n * jnp.dtype(dtype).itemsize


def _vmem_limit(block_bytes, scratch_bytes):
    need = 2 * block_bytes + scratch_bytes
    return int(min(need + max(need // 4, 8 << 20), V7X_VMEM_BYTES - (6 << 20)))


def _rmsnorm(x, g):
    r = lax.rsqrt(jnp.mean(x * x, axis=-1, keepdims=True) + EPS)
    return (x * r) * g


def _dot(a, b):
    return jnp.dot(a, b, preferred_element_type=F32)


def _dot_nt(a, b):
    return lax.dot_general(a, b, (((1,), (1,)), ((), ())), preferred_element_type=F32)


def _prompt_rows_map(i, j):
    return (i, 0)


def _prompt_tile_map(i, j):
    return (i, j)


def _sample_tile_map(n_pt):
    return lambda i, j: (0, jnp.where(i == n_pt - 1, j, 0))


def _const_map(i, j):
    return (0, 0)


def _norm_rows_kernel(xp_ref, xs_ref, g_ref, hp_ref, hs_ref, *, n_pt):
    hp_ref[...] = _rmsnorm(xp_ref[...], g_ref[...]).astype(BF16)

    @pl.when(pl.program_id(0) == n_pt - 1)
    def _():
        hs_ref[...] = _rmsnorm(xs_ref[...], g_ref[...]).astype(BF16)


def _norm_rows(xp, xs, g, *, layer):
    rp, d = xp.shape
    rs = xs.shape[0]
    tm = OUT_ROWS
    n_pt = rp // tm
    blocks = _nbytes((tm, d), F32) + _nbytes((tm, d), BF16) + _nbytes((rs, d), F32) + _nbytes((rs, d), BF16)
    return pl.pallas_call(
        functools.partial(_norm_rows_kernel, n_pt=n_pt),
        grid=(n_pt,),
        in_specs=[
            pl.BlockSpec((tm, d), lambda s: (s, 0)),
            pl.BlockSpec((rs, d), lambda s: (0, 0)),
            pl.BlockSpec((None, 1, d), lambda s: (layer, 0, 0)),
        ],
        out_specs=[
            pl.BlockSpec((tm, d), lambda s: (s, 0)),
            pl.BlockSpec((rs, d), lambda s: (0, 0)),
        ],
        out_shape=[jax.ShapeDtypeStruct((rp, d), BF16), jax.ShapeDtypeStruct((rs, d), BF16)],
        compiler_params=pltpu.CompilerParams(
            dimension_semantics=("arbitrary",),
            vmem_limit_bytes=_vmem_limit(blocks, 0)),
        name="norm_rows",
    )(xp, xs, g)


def _swiglu(h, wg, wu):
    gate = _dot(h, wg)
    up = _dot(h, wu)
    return (gate * jax.nn.sigmoid(gate) * up).astype(BF16)


def _ffn_up_kernel(hp_ref, hs_ref, wg_ref, wu_ref, ap_ref, as_ref, *, n_pt):
    wg, wu = wg_ref[...].astype(BF16), wu_ref[...].astype(BF16)
    for r in range(0, hp_ref.shape[0], MXU_ROWS):
        ap_ref[r:r + MXU_ROWS, :] = _swiglu(hp_ref[r:r + MXU_ROWS, :], wg, wu)

    @pl.when(pl.program_id(0) == n_pt - 1)
    def _():
        as_ref[...] = _swiglu(hs_ref[...], wg_ref[...].astype(BF16), wu_ref[...].astype(BF16))


def _ffn_up(hp, hs, wg, wu, *, layer):
    rp, d = hp.shape
    rs = hs.shape[0]
    n = wg.shape[2]
    tm, tn = FFN_ROW_TILE, FFN_COL_TILE
    n_pt, n_j = rp // tm, n // tn
    blocks = (_nbytes((tm, d), BF16) + _nbytes((rs, d), BF16) + 2 * _nbytes((d, tn), wg.dtype)
              + _nbytes((tm, tn), BF16) + _nbytes((rs, tn), BF16))
    temps = 2 * _nbytes((d, tn), BF16) + 3 * _nbytes((tm, tn), F32)
    w_spec = pl.BlockSpec((None, d, tn), lambda i, j: (layer, 0, j))
    return pl.pallas_call(
        functools.partial(_ffn_up_kernel, n_pt=n_pt),
        grid=(n_pt, n_j),
        in_specs=[
            pl.BlockSpec((tm, d), _prompt_rows_map),
            pl.BlockSpec((rs, d), _const_map),
            w_spec,
            w_spec,
        ],
        out_specs=[
            pl.BlockSpec((tm, tn), _prompt_tile_map),
            pl.BlockSpec((rs, tn), _sample_tile_map(n_pt)),
        ],
        out_shape=[jax.ShapeDtypeStruct((rp, n), BF16), jax.ShapeDtypeStruct((rs, n), BF16)],
        compiler_params=pltpu.CompilerParams(
            dimension_semantics=("arbitrary", "arbitrary"),
            vmem_limit_bytes=_vmem_limit(blocks, temps)),
        name="ffn_up",
    )(hp, hs, wg, wu)


def _out_proj_kernel(ap_ref, as_ref, w_ref, xp_ref, xs_ref, g_ref, op_ref, os_ref, hp_ref, hs_ref, *, n_pt):
    x = xp_ref[...] + _dot(ap_ref[...], w_ref[...])
    op_ref[...] = x
    hp_ref[...] = _rmsnorm(x, g_ref[...]).astype(BF16)

    @pl.when(pl.program_id(0) == n_pt - 1)
    def _():
        x = xs_ref[...] + _dot(as_ref[...], w_ref[...])
        os_ref[...] = x
        hs_ref[...] = _rmsnorm(x, g_ref[...]).astype(BF16)


def _out_proj(ap, as_, w, xp, xs, g, *, layer):
    rp, k = ap.shape
    rs = as_.shape[0]
    n = w.shape[1]
    tm = OUT_ROWS
    n_pt = rp // tm
    rows_map = lambda s: (s, 0)
    const = lambda s: (0, 0)
    once = dict(pipeline_mode=pl.Buffered(1))
    blocks = _nbytes((tm, k), BF16) + 2 * _nbytes((tm, n), F32) + _nbytes((tm, n), BF16)
    resident = (_nbytes((k, n), BF16) + _nbytes((rs, k), BF16) + _nbytes((rs, n), F32)
                + 2 * (_nbytes((rs, n), F32) + _nbytes((rs, n), BF16)))
    return pl.pallas_call(
        functools.partial(_out_proj_kernel, n_pt=n_pt),
        grid=(n_pt,),
        in_specs=[
            pl.BlockSpec((tm, k), rows_map),
            pl.BlockSpec((rs, k), const, **once),
            pl.BlockSpec((k, n), const, **once),
            pl.BlockSpec((tm, n), rows_map),
            pl.BlockSpec((rs, n), const, **once),
            pl.BlockSpec((None, 1, n), lambda s: (layer, 0, 0)),
        ],
        out_specs=[
            pl.BlockSpec((tm, n), rows_map),
            pl.BlockSpec((rs, n), const),
            pl.BlockSpec((tm, n), rows_map),
            pl.BlockSpec((rs, n), const),
        ],
        out_shape=[jax.ShapeDtypeStruct((rp, n), F32), jax.ShapeDtypeStruct((rs, n), F32),
                   jax.ShapeDtypeStruct((rp, n), BF16), jax.ShapeDtypeStruct((rs, n), BF16)],
        compiler_params=pltpu.CompilerParams(
            dimension_semantics=("arbitrary",),
            vmem_limit_bytes=_vmem_limit(blocks, resident + 2 * _nbytes((tm, n), F32))),
        name="out_proj",
    )(ap, as_, w, xp, xs, g)


def _matmul_kernel(hp_ref, hs_ref, w_ref, zp_ref, zs_ref, *, n_pt):
    w = w_ref[...].astype(BF16)
    for r in range(0, hp_ref.shape[0], MXU_ROWS):
        zp_ref[r:r + MXU_ROWS, :] = _dot(hp_ref[r:r + MXU_ROWS, :], w).astype(zp_ref.dtype)

    @pl.when(pl.program_id(0) == n_pt - 1)
    def _():
        zs_ref[...] = _dot(hs_ref[...], w_ref[...].astype(BF16)).astype(zs_ref.dtype)


def _matmul(hp, hs, w, *, w_layer):
    rp, d = hp.shape
    rs = hs.shape[0]
    n = w.shape[2]
    tm, tn = FFN_ROW_TILE, COL_TILE
    n_pt, n_j = rp // tm, n // tn
    blocks = (_nbytes((tm, d), BF16) + _nbytes((rs, d), BF16) + _nbytes((d, tn), F32)
              + _nbytes((tm, tn), BF16) + _nbytes((rs, tn), BF16))
    temps = _nbytes((d, tn), BF16) + _nbytes((tm, tn), F32)
    return pl.pallas_call(
        functools.partial(_matmul_kernel, n_pt=n_pt),
        grid=(n_pt, n_j),
        in_specs=[
            pl.BlockSpec((tm, d), _prompt_rows_map),
            pl.BlockSpec((rs, d), _const_map),
            pl.BlockSpec((None, d, tn), lambda i, j: (w_layer, 0, j)),
        ],
        out_specs=[
            pl.BlockSpec((tm, tn), _prompt_tile_map),
            pl.BlockSpec((rs, tn), _sample_tile_map(n_pt)),
        ],
        out_shape=[jax.ShapeDtypeStruct((rp, n), BF16), jax.ShapeDtypeStruct((rs, n), BF16)],
        compiler_params=pltpu.CompilerParams(
            dimension_semantics=("arbitrary", "arbitrary"),
            vmem_limit_bytes=_vmem_limit(blocks, temps)),
        name="matmul",
    )(hp, hs, w)


def _ffn_down_kernel(ap_ref, as_ref, w_ref, xp_ref, xs_ref, g_ref, *refs, n_pt, n_k, n_xc, final):
    i, k = pl.program_id(0), pl.program_id(1)
    if final:
        op_ref, os_ref, apl_ref, asl_ref, wl_ref = refs
    else:
        op_ref, os_ref, hp_ref, hs_ref, apl_ref, asl_ref, wl_ref = refs
    xc = xp_ref.shape[1]
    kf = (ap_ref.shape[1] // V7X_MXU_DEPTH) * V7X_MXU_DEPTH
    even = k % 2 == 0

    def whole_passes(a_ref):
        return _dot(a_ref[:, :kf], w_ref[:kf, :])

    def with_stash(a_ref, al_ref):
        return _dot(jnp.concatenate([al_ref[...], a_ref[...]], axis=1),
                    jnp.concatenate([wl_ref[...], w_ref[...]], axis=0))

    @pl.when(k == 0)
    def _():
        op_ref[...] = whole_passes(ap_ref)

    @pl.when(jnp.logical_and(k > 0, even))
    def _():
        op_ref[...] = op_ref[...] + whole_passes(ap_ref)

    @pl.when(jnp.logical_not(even))
    def _():
        op_ref[...] = op_ref[...] + with_stash(ap_ref, apl_ref)

    for c in range(n_xc):
        @pl.when(k == c)
        def _():
            op_ref[:, c * xc:(c + 1) * xc] = op_ref[:, c * xc:(c + 1) * xc] + xp_ref[...]

    @pl.when(k == n_k - 1)
    def _():
        if final:
            op_ref[...] = _rmsnorm(op_ref[...], g_ref[...])
        else:
            hp_ref[...] = _rmsnorm(op_ref[...], g_ref[...]).astype(BF16)

    @pl.when(i == n_pt - 1)
    def _():
        @pl.when(k == 0)
        def _():
            os_ref[...] = xs_ref[...] + whole_passes(as_ref)

        @pl.when(jnp.logical_and(k > 0, even))
        def _():
            os_ref[...] = os_ref[...] + whole_passes(as_ref)

        @pl.when(jnp.logical_not(even))
        def _():
            os_ref[...] = os_ref[...] + with_stash(as_ref, asl_ref)

        @pl.when(even)
        def _():
            asl_ref[...] = as_ref[:, kf:]

        @pl.when(k == n_k - 1)
        def _():
            if final:
                os_ref[...] = _rmsnorm(os_ref[...], g_ref[...])
            else:
                hs_ref[...] = _rmsnorm(os_ref[...], g_ref[...]).astype(BF16)

    @pl.when(even)
    def _():
        apl_ref[...] = ap_ref[:, kf:]
        wl_ref[...] = w_ref[kf:, :]


def _ffn_down(ap, as_, w, xp, xs, g, *, w_layer, g_layer, final):
    rp, kdim = ap.shape
    rs = as_.shape[0]
    n = w.shape[2]
    tm, tk, xc = PROMPT_TILE, DOWN_K_TILE, DOWN_X_CHUNK
    n_pt, n_k, n_xc = rp // tm, kdim // tk, n // xc
    k_left = tk % V7X_MXU_DEPTH
    assert n_xc <= n_k and n_k % 2 == 0 and 2 * k_left == V7X_MXU_DEPTH and w.dtype == BF16
    rows = lambda i, k: (i, 0)
    blocks = (_nbytes((tm, tk), BF16) + _nbytes((rs, tk), BF16) + _nbytes((tk, n), w.dtype)
              + _nbytes((tm, xc), F32) + _nbytes((rs, n), F32)
              + _nbytes((tm, n), F32) + _nbytes((rs, n), F32))
    out_specs = [pl.BlockSpec((tm, n), rows), pl.BlockSpec((rs, n), _const_map)]
    out_shape = [jax.ShapeDtypeStruct((rp, n), F32), jax.ShapeDtypeStruct((rs, n), F32)]
    if not final:
        blocks += _nbytes((tm, n), BF16) + _nbytes((rs, n), BF16)
        out_specs += [pl.BlockSpec((tm, n), rows), pl.BlockSpec((rs, n), _const_map)]
        out_shape += [jax.ShapeDtypeStruct((rp, n), BF16), jax.ShapeDtypeStruct((rs, n), BF16)]
    return pl.pallas_call(
        functools.partial(_ffn_down_kernel, n_pt=n_pt, n_k=n_k, n_xc=n_xc, final=final),
        grid=(n_pt, n_k),
        in_specs=[
            pl.BlockSpec((tm, tk), lambda i, k: (i, k)),
            pl.BlockSpec((rs, tk), lambda i, k: (0, jnp.where(i == n_pt - 1, k, 0))),
            pl.BlockSpec((None, tk, n), lambda i, k: (w_layer, k, 0)),
            pl.BlockSpec((tm, xc), lambda i, k: (i, jnp.minimum(k, n_xc - 1))),
            pl.BlockSpec((rs, n), _const_map),
            pl.BlockSpec((None, 1, n), lambda i, k: (g_layer, 0, 0)),
        ],
        out_specs=out_specs,
        out_shape=out_shape,
        scratch_shapes=[pltpu.VMEM((tm, k_left), BF16), pltpu.VMEM((rs, k_left), BF16),
                        pltpu.VMEM((k_left, n), BF16)],
        compiler_params=pltpu.CompilerParams(
            dimension_semantics=("arbitrary", "arbitrary"),
            vmem_limit_bytes=_vmem_limit(blocks, _nbytes((tk, n), BF16))),
        name="ffn_down",
    )(ap, as_, w, xp, xs, g)


def _mem_kv_kernel(x_ref, g_ref, w_ref, k_ref, v_ref):
    h = _rmsnorm(x_ref[...], g_ref[...]).astype(BF16)
    kv = _dot(h, w_ref[...].astype(BF16))
    k_ref[...] = kv[:, :MEM_WIDTH]
    v_ref[...] = kv[:, MEM_WIDTH:]


def _mem_kv(mem, g, w):
    rows, d = mem.shape
    depth = w.shape[0]
    tm = 512
    out = jax.ShapeDtypeStruct((depth, rows, MEM_WIDTH), F32)
    blocks = (_nbytes((tm, d), F32) + _nbytes((d, 2 * MEM_WIDTH), F32) + 2 * _nbytes((tm, MEM_WIDTH), F32))
    return pl.pallas_call(
        _mem_kv_kernel,
        grid=(depth, rows // tm),
        in_specs=[
            pl.BlockSpec((tm, d), lambda l, i: (i, 0)),
            pl.BlockSpec((None, 1, d), lambda l, i: (l, 0, 0)),
            pl.BlockSpec((None, d, 2 * MEM_WIDTH), lambda l, i: (l, 0, 0)),
        ],
        out_specs=[
            pl.BlockSpec((None, tm, MEM_WIDTH), lambda l, i: (l, i, 0)),
            pl.BlockSpec((None, tm, MEM_WIDTH), lambda l, i: (l, i, 0)),
        ],
        out_shape=[out, out],
        compiler_params=pltpu.CompilerParams(
            dimension_semantics=("arbitrary", "arbitrary"),
            vmem_limit_bytes=_vmem_limit(blocks, _nbytes((d, 2 * MEM_WIDTH), BF16))),
        name="mem_kv",
    )(mem, g, w)


def _cross_scores(q, k):
    return _dot_nt(q, k) * (MEM_HEAD_DIM ** -0.5)


def _cross_values(s, v):
    e = jnp.exp(s - jnp.max(s, axis=-1, keepdims=True)).astype(BF16)
    return _dot(e, v) / _dot(e, jnp.ones(v.shape, BF16))


def _run_ahead(n_items, lookahead, first, second):
    pending = {}
    for i in range(n_items + lookahead):
        if i < n_items:
            pending[i] = first(i)
        if i >= lookahead:
            second(i - lookahead, pending.pop(i - lookahead))


def _conv_prompt_kernel(z_ref, mk_ref, mv_ref, cw_ref, wd_ref, mix_ref, st_ref, wdb_ref, ext_ref, *, tiles_per_seq):
    s = pl.program_id(0)
    tq = z_ref.shape[0]
    wdb_ref[...] = wd_ref[...].astype(BF16)

    @pl.when(s % tiles_per_seq == 0)
    def _():
        ext_ref[0:CARRY_ROWS, :] = jnp.zeros((CARRY_ROWS, TOKEN_WIDTH), F32)

    c = z_ref[:, TOKEN_WIDTH:2 * TOKEN_WIDTH].astype(F32)
    u = z_ref[:, 2 * TOKEN_WIDTH:3 * TOKEN_WIDTH].astype(F32)
    cu = c * u
    ext_ref[CARRY_ROWS:CARRY_ROWS + tq, :] = cu
    conv = (cw_ref[0:1, :] * ext_ref[CARRY_ROWS - 2:CARRY_ROWS - 2 + tq, :]
            + cw_ref[1:2, :] * ext_ref[CARRY_ROWS - 1:CARRY_ROWS - 1 + tq, :]
            + cw_ref[2:3, :] * cu)
    b = z_ref[:, 0:TOKEN_WIDTH].astype(F32)
    mix_ref[:, 0:TOKEN_WIDTH] = (b * conv).astype(BF16)
    st_ref[...] = ext_ref[CARRY_ROWS + tq - 2:CARRY_ROWS + tq, :]
    ext_ref[0:CARRY_ROWS, :] = ext_ref[tq:tq + CARRY_ROWS, :]

    def scores(h):
        lo, hi = h * MEM_HEAD_DIM, (h + 1) * MEM_HEAD_DIM
        return _cross_scores(z_ref[:, 3 * TOKEN_WIDTH + lo:3 * TOKEN_WIDTH + hi], mk_ref[:, lo:hi].astype(BF16))

    def finish(h, s):
        lo, hi = h * MEM_HEAD_DIM, (h + 1) * MEM_HEAD_DIM
        mix_ref[:, TOKEN_WIDTH + lo:TOKEN_WIDTH + hi] = _cross_values(s, mv_ref[:, lo:hi].astype(BF16)).astype(BF16)

    _run_ahead(MEM_HEADS, SCORE_LOOKAHEAD, scores, finish)


def _conv_prompt(z, mk, mv, conv_w, w_down, *, layer, batch, seq):
    rows, zc = z.shape
    tq = CONV_ROWS
    tiles_per_seq = seq // tq
    n_steps = batch * tiles_per_seq
    d_ff, d = w_down.shape[1:]
    slab = d_ff // n_steps
    assert slab * n_steps == d_ff and slab % (2 * V7X_SUBLANES) == 0
    blocks = (_nbytes((tq, zc), BF16) + 2 * _nbytes((N_MEM, MEM_WIDTH), F32) + _nbytes((tq, D_MODEL), BF16)
              + _nbytes((slab, d), F32) + _nbytes((slab, d), BF16))
    scratch = _nbytes((tq + CARRY_ROWS, TOKEN_WIDTH), F32)
    return pl.pallas_call(
        functools.partial(_conv_prompt_kernel, tiles_per_seq=tiles_per_seq),
        grid=(n_steps,),
        in_specs=[
            pl.BlockSpec((tq, zc), lambda s: (s, 0)),
            pl.BlockSpec((None, N_MEM, MEM_WIDTH), lambda s: (layer, s // tiles_per_seq, 0)),
            pl.BlockSpec((None, N_MEM, MEM_WIDTH), lambda s: (layer, s // tiles_per_seq, 0)),
            pl.BlockSpec((None, CONV_WIDTH, TOKEN_WIDTH), lambda s: (0, 0, 0)),
            pl.BlockSpec((None, slab, d), lambda s: (layer, s, 0)),
        ],
        out_specs=[
            pl.BlockSpec((tq, D_MODEL), lambda s: (s, 0)),
            pl.BlockSpec((None, CONV_WIDTH - 1, TOKEN_WIDTH), lambda s: (s // tiles_per_seq, 0, 0)),
            pl.BlockSpec((slab, d), lambda s: (s, 0)),
        ],
        out_shape=[
            jax.ShapeDtypeStruct((rows, D_MODEL), BF16),
            jax.ShapeDtypeStruct((batch, CONV_WIDTH - 1, TOKEN_WIDTH), F32),
            jax.ShapeDtypeStruct((d_ff, d), BF16),
        ],
        scratch_shapes=[pltpu.VMEM((tq + CARRY_ROWS, TOKEN_WIDTH), F32)],
        compiler_params=pltpu.CompilerParams(
            dimension_semantics=("arbitrary",),
            vmem_limit_bytes=_vmem_limit(blocks, scratch + 6 * _nbytes((tq, TOKEN_WIDTH), F32))),
        name="conv_prompt",
    )(z, mk, mv, conv_w, w_down)


def _conv_sample_kernel(z_ref, st_ref, mk_ref, mv_ref, cw_ref, wo_ref, mix_ref, nst_ref, wob_ref,
                        ext_ref, mixf_ref, *, dec_seq):
    t = dec_seq
    wob_ref[...] = wo_ref[...].astype(BF16)
    for n in range(SAMPLE_GROUP):
        r0, r1 = n * t, (n + 1) * t
        c = z_ref[r0:r1, TOKEN_WIDTH:2 * TOKEN_WIDTH].astype(F32)
        u = z_ref[r0:r1, 2 * TOKEN_WIDTH:3 * TOKEN_WIDTH].astype(F32)
        cu = c * u
        ext_ref[CARRY_ROWS - 2:CARRY_ROWS, :] = st_ref[n]
        ext_ref[CARRY_ROWS:CARRY_ROWS + t, :] = cu
        conv = (cw_ref[0:1, :] * ext_ref[CARRY_ROWS - 2:CARRY_ROWS - 2 + t, :]
                + cw_ref[1:2, :] * ext_ref[CARRY_ROWS - 1:CARRY_ROWS - 1 + t, :]
                + cw_ref[2:3, :] * cu)
        b = z_ref[r0:r1, 0:TOKEN_WIDTH].astype(F32)
        mixf_ref[r0:r1, 0:TOKEN_WIDTH] = b * conv
        nst_ref[n] = ext_ref[CARRY_ROWS + t - 2:CARRY_ROWS + t, :]
    _sample_cross_attention(z_ref, mk_ref, mv_ref, mixf_ref, 3 * TOKEN_WIDTH, t)
    mix_ref[...] = mixf_ref[...].astype(BF16)


def _sample_cross_attention(z_ref, mk_ref, mv_ref, mixf_ref, qm_off, t):
    def scores(i):
        n, h = divmod(i, MEM_HEADS)
        q = z_ref[n * t:(n + 1) * t, qm_off + h * MEM_HEAD_DIM:qm_off + (h + 1) * MEM_HEAD_DIM]
        return _cross_scores(q, mk_ref[n, pl.ds(h, N_MEM, stride=MEM_HEADS), :].astype(BF16))

    def finish(i, s):
        n, h = divmod(i, MEM_HEADS)
        o = _cross_values(s, mv_ref[n, pl.ds(h, N_MEM, stride=MEM_HEADS), :].astype(BF16))
        mixf_ref[n * t:(n + 1) * t, TOKEN_WIDTH + h * MEM_HEAD_DIM:TOKEN_WIDTH + (h + 1) * MEM_HEAD_DIM] = o

    _run_ahead(SAMPLE_GROUP * MEM_HEADS, SAMPLE_LOOKAHEAD, scores, finish)


def _conv_sample(z, state, mem_k, mem_v, conv_w, w_out, *, layer, dec_seq):
    rows, zc = z.shape
    dec_batch = state.shape[1]
    g = SAMPLE_GROUP
    gr = g * dec_seq
    n_steps = dec_batch // g
    wk, wn = w_out.shape[1:]
    slab = wk // n_steps
    assert slab * n_steps == wk and slab % (2 * V7X_SUBLANES) == 0
    blocks = (_nbytes((gr, zc), BF16) + 2 * _nbytes((g, CONV_WIDTH - 1, TOKEN_WIDTH), F32)
              + 2 * _nbytes((g, N_MEM, V7X_SUBLANES, MEM_HEAD_DIM), F32) + _nbytes((gr, D_MODEL), BF16)
              + _nbytes((slab, wn), F32) + _nbytes((slab, wn), BF16))
    scratch = _nbytes((2 * CARRY_ROWS, TOKEN_WIDTH), F32) + _nbytes((gr, D_MODEL), F32)
    return pl.pallas_call(
        functools.partial(_conv_sample_kernel, dec_seq=dec_seq),
        grid=(n_steps,),
        in_specs=[
            pl.BlockSpec((gr, zc), lambda i: (i, 0)),
            pl.BlockSpec((None, g, CONV_WIDTH - 1, TOKEN_WIDTH), lambda i: (0, i, 0, 0)),
            pl.BlockSpec((None, g, N_MEM * MEM_HEADS, MEM_HEAD_DIM), lambda i: (layer, i, 0, 0)),
            pl.BlockSpec((None, g, N_MEM * MEM_HEADS, MEM_HEAD_DIM), lambda i: (layer, i, 0, 0)),
            pl.BlockSpec((None, CONV_WIDTH, TOKEN_WIDTH), lambda i: (0, 0, 0)),
            pl.BlockSpec((None, slab, wn), lambda i: (0, i, 0)),
        ],
        out_specs=[
            pl.BlockSpec((gr, D_MODEL), lambda i: (i, 0)),
            pl.BlockSpec((g, CONV_WIDTH - 1, TOKEN_WIDTH), lambda i: (i, 0, 0)),
            pl.BlockSpec((slab, wn), lambda i: (i, 0)),
        ],
        out_shape=[
            jax.ShapeDtypeStruct((rows, D_MODEL), BF16),
            jax.ShapeDtypeStruct((dec_batch, CONV_WIDTH - 1, TOKEN_WIDTH), F32),
            jax.ShapeDtypeStruct((wk, wn), BF16),
        ],
        scratch_shapes=[pltpu.VMEM((2 * CARRY_ROWS, TOKEN_WIDTH), F32), pltpu.VMEM((gr, D_MODEL), F32)],
        compiler_params=pltpu.CompilerParams(
            dimension_semantics=("arbitrary",),
            vmem_limit_bytes=_vmem_limit(blocks, scratch)),
        name="conv_sample",
    )(z, state, mem_k, mem_v, conv_w, w_out)


def _band_scores(q, k):
    return _dot_nt(q * (HEAD_DIM ** -0.5), k)


def _band_probs(s, sink, upper, upper_visible):
    s = jnp.where(upper_visible, s[:, :WINDOW], jnp.where(upper, -jnp.inf, s[:, WINDOW:]))
    m = jnp.maximum(jnp.max(s, axis=-1, keepdims=True), sink)
    e = jnp.exp(s - m)
    e = jnp.concatenate([jnp.where(upper, e, 0.0), jnp.where(upper, 0.0, e)], axis=1).astype(BF16)
    return e, jnp.exp(sink - m)


def _band_values(probs, v, ones):
    e, sink_term = probs
    return _dot(e, v) / (_dot(e, ones) + sink_term)


def _swa_prompt_kernel(sink_ref, zq_ref, zp_ref, mk_ref, mv_ref, *refs, blocks_per_seq, cast_steps):
    n_w = (len(refs) - 1) // 2
    w_refs, mix_ref, wb_refs = refs[:n_w], refs[n_w], refs[n_w + 1:]
    s = pl.program_id(0)

    for w_ref, wb_ref, n_cast in zip(w_refs, wb_refs, cast_steps):
        @pl.when(s < n_cast)
        def _():
            wb_ref[...] = w_ref[...].astype(BF16)

    k_off = TOKEN_WIDTH
    v_off = TOKEN_WIDTH + KV_WIDTH
    qm_off = TOKEN_WIDTH + 2 * KV_WIDTH
    n_blocks = zq_ref.shape[0] // WINDOW
    first_has_prev = (s % (blocks_per_seq // n_blocks)) > 0
    row = lax.broadcasted_iota(jnp.int32, (WINDOW, WINDOW), 0)
    col = lax.broadcasted_iota(jnp.int32, (WINDOW, WINDOW), 1)
    upper = col > row
    upper_first = jnp.logical_and(upper, first_has_prev)
    ones = jnp.ones((2 * WINDOW, HEAD_DIM), BF16)

    def block_rows(b):
        return slice(b * WINDOW, (b + 1) * WINDOW)

    @functools.lru_cache(maxsize=None)
    def window(b, kh, col_prev, col_cur):
        lo, hi = kh * HEAD_DIM, (kh + 1) * HEAD_DIM
        prev = (zp_ref[:, col_prev + lo:col_prev + hi] if b == 0
                else zq_ref[block_rows(b - 1), col_cur + lo:col_cur + hi])
        return jnp.concatenate([prev, zq_ref[block_rows(b), col_cur + lo:col_cur + hi]], axis=0)

    def band_scores(i):
        b, h = divmod(i, N_HEADS)
        return _band_scores(zq_ref[block_rows(b), h * HEAD_DIM:(h + 1) * HEAD_DIM], window(b, h // GROUP, 0, k_off))

    def band_finish(i, s):
        b, h = divmod(i, N_HEADS)
        p = _band_probs(s, sink_ref[h], upper, upper_first if b == 0 else upper)
        o = _band_values(p, window(b, h // GROUP, KV_WIDTH, v_off), ones)
        mix_ref[block_rows(b), h * HEAD_DIM:(h + 1) * HEAD_DIM] = o.astype(BF16)

    def cross_scores(h):
        lo, hi = h * MEM_HEAD_DIM, (h + 1) * MEM_HEAD_DIM
        return _cross_scores(zq_ref[:, qm_off + lo:qm_off + hi], mk_ref[:, lo:hi].astype(BF16))

    def cross_finish(h, s):
        lo, hi = h * MEM_HEAD_DIM, (h + 1) * MEM_HEAD_DIM
        mix_ref[:, TOKEN_WIDTH + lo:TOKEN_WIDTH + hi] = _cross_values(s, mv_ref[:, lo:hi].astype(BF16)).astype(BF16)

    _run_ahead(n_blocks * N_HEADS, SCORE_LOOKAHEAD, band_scores, band_finish)
    _run_ahead(MEM_HEADS, SCORE_LOOKAHEAD, cross_scores, cross_finish)


def _swa_prompt(z, mk, mv, sinks, weights, *, layer, batch, seq):
    rows, zc = z.shape
    tq = SWA_ROWS
    blocks_per_seq = seq // WINDOW
    steps_per_seq = seq // tq
    blocks_per_step = tq // WINDOW
    n_steps = batch * steps_per_seq
    kv_col_block = TOKEN_WIDTH // (2 * KV_WIDTH)
    bf16_rows = 2 * V7X_SUBLANES
    blocks = (_nbytes((tq, zc), BF16) + _nbytes((WINDOW, 2 * KV_WIDTH), BF16)
              + 2 * _nbytes((N_MEM, MEM_WIDTH), F32) + _nbytes((tq, D_MODEL), BF16))
    w_in_specs, w_out_specs, w_out_shapes, cast_steps = [], [], [], []
    for w in weights:
        r, c = w.shape[1:]
        n_cast = n_steps if r % (n_steps * bf16_rows) == 0 else n_steps // 2
        slab = r // n_cast
        assert slab * n_cast == r and slab % bf16_rows == 0
        blocks += _nbytes((slab, c), F32) + _nbytes((slab, c), BF16)
        w_in_specs.append(pl.BlockSpec((None, slab, c), lambda s, n=n_cast: (layer, jnp.minimum(s, n - 1), 0)))
        w_out_specs.append(pl.BlockSpec((slab, c), lambda s, n=n_cast: (jnp.minimum(s, n - 1), 0)))
        w_out_shapes.append(jax.ShapeDtypeStruct((r, c), BF16))
        cast_steps.append(n_cast)
    return pl.pallas_call(
        functools.partial(_swa_prompt_kernel, blocks_per_seq=blocks_per_seq, cast_steps=tuple(cast_steps)),
        grid=(n_steps,),
        in_specs=[
            pl.BlockSpec(memory_space=pltpu.SMEM),
            pl.BlockSpec((tq, zc), lambda s: (s, 0)),
            pl.BlockSpec((WINDOW, 2 * KV_WIDTH), lambda s: (jnp.maximum(s * blocks_per_step - 1, 0), kv_col_block)),
            pl.BlockSpec((None, N_MEM, MEM_WIDTH), lambda s: (layer, s // steps_per_seq, 0)),
            pl.BlockSpec((None, N_MEM, MEM_WIDTH), lambda s: (layer, s // steps_per_seq, 0)),
        ] + w_in_specs,
        out_specs=[pl.BlockSpec((tq, D_MODEL), lambda s: (s, 0))] + w_out_specs,
        out_shape=[jax.ShapeDtypeStruct((rows, D_MODEL), BF16)] + w_out_shapes,
        compiler_params=pltpu.CompilerParams(
            dimension_semantics=("arbitrary",),
            vmem_limit_bytes=_vmem_limit(blocks, 0)),
        name="swa_prompt",
    )(sinks, z, z, mk, mv, *weights)


def _swa_sample_kernel(sink_ref, z_ref, ck_ref, cv_ref, mk_ref, mv_ref, wo_ref,
                       mix_ref, nk_ref, nv_ref, wob_ref, knew_ref, vnew_ref, mixf_ref, *, dec_seq):
    t = dec_seq
    wob_ref[...] = wo_ref[...].astype(BF16)
    k_off = TOKEN_WIDTH
    v_off = TOKEN_WIDTH + KV_WIDTH
    qm_off = TOKEN_WIDTH + 2 * KV_WIDTH
    rows = GROUP * t
    qi = lax.broadcasted_iota(jnp.int32, (rows, WINDOW), 0) % t
    col = lax.broadcasted_iota(jnp.int32, (rows, WINDOW), 1)
    upper = col > qi
    knew_ref[...] = jnp.zeros(knew_ref.shape, F32)
    vnew_ref[...] = jnp.zeros(vnew_ref.shape, F32)
    for n in range(SAMPLE_GROUP):
        r0, r1 = n * t, (n + 1) * t
        k_new = z_ref[r0:r1, k_off:k_off + KV_WIDTH].astype(F32)
        v_new = z_ref[r0:r1, v_off:v_off + KV_WIDTH].astype(F32)
        knew_ref[n, 0:t, :] = k_new
        vnew_ref[n, 0:t, :] = v_new
        nk_ref[n, 0:WINDOW - t, :] = ck_ref[n, t:WINDOW, :]
        nv_ref[n, 0:WINDOW - t, :] = cv_ref[n, t:WINDOW, :]
        nk_ref[n, WINDOW - t:WINDOW, :] = k_new
        nv_ref[n, WINDOW - t:WINDOW, :] = v_new

    def scores(i):
        n, kh = divmod(i, N_KV_HEADS)
        lo, hi = kh * HEAD_DIM, (kh + 1) * HEAD_DIM
        k = jnp.concatenate([ck_ref[n, :, lo:hi], knew_ref[n, :, lo:hi]], axis=0).astype(BF16)
        q = jnp.concatenate(
            [z_ref[n * t:(n + 1) * t, (kh * GROUP + g) * HEAD_DIM:(kh * GROUP + g + 1) * HEAD_DIM].astype(F32)
             for g in range(GROUP)], axis=0).astype(BF16)
        return _band_scores(q, k)

    def finish(i, s):
        n, kh = divmod(i, N_KV_HEADS)
        lo, hi = kh * HEAD_DIM, (kh + 1) * HEAD_DIM
        v = jnp.concatenate([cv_ref[n, :, lo:hi], vnew_ref[n, :, lo:hi]], axis=0).astype(BF16)
        sink = jnp.concatenate(
            [jnp.full((t, 1), sink_ref[kh * GROUP + g], F32) for g in range(GROUP)], axis=0)
        o = _band_values(_band_probs(s, sink, upper, upper), v, jnp.ones((2 * WINDOW, HEAD_DIM), BF16))
        for g in range(GROUP):
            h = kh * GROUP + g
            mixf_ref[n * t:(n + 1) * t, h * HEAD_DIM:(h + 1) * HEAD_DIM] = o[g * t:(g + 1) * t, :]

    _run_ahead(SAMPLE_GROUP * N_KV_HEADS, SAMPLE_LOOKAHEAD, scores, finish)
    _sample_cross_attention(z_ref, mk_ref, mv_ref, mixf_ref, qm_off, t)
    mix_ref[...] = mixf_ref[...].astype(BF16)


def _swa_sample(z, cache_k, cache_v, mem_k, mem_v, sinks, w_out, *, layer, dec_seq):
    rows, zc = z.shape
    dec_batch = cache_k.shape[0]
    g = SAMPLE_GROUP
    gr = g * dec_seq
    n_steps = dec_batch // g
    wk, wn = w_out.shape[1:]
    slab = wk // n_steps
    assert slab * n_steps == wk and slab % (2 * V7X_SUBLANES) == 0
    win = jax.ShapeDtypeStruct((dec_batch, WINDOW, KV_WIDTH), F32)
    blocks = (_nbytes((gr, zc), BF16) + 4 * _nbytes((g, WINDOW, KV_WIDTH), F32)
              + 2 * _nbytes((g, N_MEM, V7X_SUBLANES, MEM_HEAD_DIM), F32) + _nbytes((gr, D_MODEL), BF16)
              + _nbytes((slab, wn), F32) + _nbytes((slab, wn), BF16))
    scratch = 2 * _nbytes((g, WINDOW, KV_WIDTH), F32) + _nbytes((gr, D_MODEL), F32)
    return pl.pallas_call(
        functools.partial(_swa_sample_kernel, dec_seq=dec_seq),
        grid=(n_steps,),
        in_specs=[
            pl.BlockSpec(memory_space=pltpu.SMEM),
            pl.BlockSpec((gr, zc), lambda i: (i, 0)),
            pl.BlockSpec((g, WINDOW, KV_WIDTH), lambda i: (i, 0, 0)),
            pl.BlockSpec((g, WINDOW, KV_WIDTH), lambda i: (i, 0, 0)),
            pl.BlockSpec((None, g, N_MEM * MEM_HEADS, MEM_HEAD_DIM), lambda i: (layer, i, 0, 0)),
            pl.BlockSpec((None, g, N_MEM * MEM_HEADS, MEM_HEAD_DIM), lambda i: (layer, i, 0, 0)),
            pl.BlockSpec((None, slab, wn), lambda i: (0, i, 0)),
        ],
        out_specs=[
            pl.BlockSpec((gr, D_MODEL), lambda i: (i, 0)),
            pl.BlockSpec((g, WINDOW, KV_WIDTH), lambda i: (i, 0, 0)),
            pl.BlockSpec((g, WINDOW, KV_WIDTH), lambda i: (i, 0, 0)),
            pl.BlockSpec((slab, wn), lambda i: (i, 0)),
        ],
        out_shape=[jax.ShapeDtypeStruct((rows, D_MODEL), BF16), win, win,
                   jax.ShapeDtypeStruct((wk, wn), BF16)],
        scratch_shapes=[pltpu.VMEM((g, WINDOW, KV_WIDTH), F32), pltpu.VMEM((g, WINDOW, KV_WIDTH), F32),
                        pltpu.VMEM((gr, D_MODEL), F32)],
        compiler_params=pltpu.CompilerParams(
            dimension_semantics=("arbitrary",),
            vmem_limit_bytes=_vmem_limit(blocks, scratch)),
        name="swa_sample",
    )(sinks, z, cache_k, cache_v, mem_k, mem_v, w_out)


def kernel(x_prompt, x_sample, mem_prompt, state_conv, cache_win_k, cache_win_v, cache_mem_k, cache_mem_v,
           norm_mix, norm_mem, w_mem_kv, norm_ffn, w_gate, w_up, w_down,
           conv_w_in, conv_w, conv_w_out, attn_w_in, attn_sinks, attn_w_out, norm_final):
    batch, seq, d = x_prompt.shape
    dec_batch, dec_seq, _ = x_sample.shape
    depth = norm_mix.shape[0]
    d_ff = w_gate.shape[2]
    prompt_rows = batch * seq
    sample_rows = dec_batch * dec_seq
    assert d == D_MODEL and depth == 2 and seq % CONV_ROWS == 0 and seq % SWA_ROWS == 0 and SWA_ROWS % WINDOW == 0
    assert prompt_rows % PROMPT_TILE == 0 and dec_batch % SAMPLE_GROUP == 0
    assert dec_seq == V7X_SUBLANES and d_ff % FFN_COL_TILE == 0 and d % COL_TILE == 0
    assert d_ff % DOWN_K_TILE == 0 and d % DOWN_X_CHUNK == 0
    assert prompt_rows % FFN_ROW_TILE == 0 and prompt_rows % sample_rows == 0

    xp = x_prompt.reshape(prompt_rows, d)
    xs = x_sample.reshape(sample_rows, d)
    mem = mem_prompt.reshape(batch * N_MEM, d)
    mem_k_s = cache_mem_k.reshape(depth, dec_batch, N_MEM * MEM_HEADS, MEM_HEAD_DIM)
    mem_v_s = cache_mem_v.reshape(depth, dec_batch, N_MEM * MEM_HEADS, MEM_HEAD_DIM)
    g_mix = norm_mix.reshape(depth, 1, d)
    g_ffn = norm_ffn.reshape(depth, 1, d)

    mk, mv = _mem_kv(mem, norm_mem.reshape(depth, 1, d), w_mem_kv)

    hp, hs = _norm_rows(xp, xs, g_mix, layer=0)
    zp, zs = _matmul(hp, hs, conv_w_in, w_layer=0)
    mix_s, conv_s, wo_bf = _conv_sample(zs, state_conv, mem_k_s, mem_v_s, conv_w, conv_w_out,
                                        layer=0, dec_seq=dec_seq)
    mix_p, conv_p, wd_bf = _conv_prompt(zp, mk, mv, conv_w, w_down, layer=0, batch=batch, seq=seq)
    xp, xs, hp, hs = _out_proj(mix_p, mix_s, wo_bf, xp, xs, g_ffn, layer=0)
    ap, as_ = _ffn_up(hp, hs, w_gate, w_up, layer=0)
    xp, xs, hp, hs = _ffn_down(ap, as_, wd_bf[None], xp, xs, g_mix, w_layer=0, g_layer=1, final=False)

    zp, zs = _matmul(hp, hs, attn_w_in, w_layer=0)
    sinks = attn_sinks[0]
    mix_s, win_k_s, win_v_s, wo_bf = _swa_sample(
        zs, cache_win_k[0].reshape(dec_batch, WINDOW, KV_WIDTH), cache_win_v[0].reshape(dec_batch, WINDOW, KV_WIDTH),
        mem_k_s, mem_v_s, sinks, attn_w_out, layer=1, dec_seq=dec_seq)
    mix_p, wd_bf = _swa_prompt(zp, mk, mv, sinks, [w_down], layer=1, batch=batch, seq=seq)
    xp, xs, hp, hs = _out_proj(mix_p, mix_s, wo_bf, xp, xs, g_ffn, layer=1)
    ap, as_ = _ffn_up(hp, hs, w_gate, w_up, layer=1)
    y_prompt, y_sample = _ffn_down(ap, as_, wd_bf[None], xp, xs, norm_final.reshape(1, 1, d),
                                   w_layer=0, g_layer=0, final=True)

    win_p = zp.reshape(batch, seq, -1)[:, seq - WINDOW:, TOKEN_WIDTH:TOKEN_WIDTH + 2 * KV_WIDTH].astype(F32)
    kv_shape = (1, -1, WINDOW, N_KV_HEADS, HEAD_DIM)
    mem_shape = (depth, batch, N_MEM, MEM_HEADS, MEM_HEAD_DIM)
    return (y_prompt.reshape(batch, seq, d),
            y_sample.reshape(dec_batch, dec_seq, d),
            conv_p[None],
            conv_s[None],
            win_p[..., :KV_WIDTH].reshape(kv_shape),
            win_p[..., KV_WIDTH:].reshape(kv_shape),
            win_k_s.reshape(kv_shape),
            win_v_s.reshape(kv_shape),
            mk.reshape(mem_shape),
            mv.reshape(mem_shape))
```

```python
import functools

import jax
import jax.numpy as jnp
from jax import lax
from jax.experimental import pallas as pl
from jax.experimental.pallas import tpu as pltpu

F32 = jnp.float32
BF16 = jnp.bfloat16

D_MODEL = 2048
N_MEM = 256
MEM_HEADS = 4
MEM_WIDTH = D_MODEL // 4
MEM_HEAD_DIM = MEM_WIDTH // MEM_HEADS
TOKEN_WIDTH = D_MODEL - MEM_WIDTH
CONV_WIDTH = 3
WINDOW = 128
HEAD_DIM = 64
N_HEADS = TOKEN_WIDTH // HEAD_DIM
N_KV_HEADS = 4
GROUP = N_HEADS // N_KV_HEADS
KV_WIDTH = N_KV_HEADS * HEAD_DIM
EPS = 1e-6

V7X_VMEM_BYTES = 64 * 1024 * 1024
V7X_SUBLANES = 8
V7X_MXU_DEPTH = 256

PROMPT_TILE = 1024
COL_TILE = 512
DOWN_K_TILE = 1408
DOWN_X_CHUNK = 512
FFN_ROW_TILE = 2048
FFN_COL_TILE = 512
MXU_ROWS = 1024
OUT_ROWS = 512
CONV_ROWS = 512
SWA_ROWS = 512
SCORE_LOOKAHEAD = 5
SAMPLE_LOOKAHEAD = 4
SAMPLE_GROUP = 8
CARRY_ROWS = V7X_SUBLANES


def _nbytes(shape, dtype):
    n = 1
    for s in shape:
        n *= s
    return n * jnp.dtype(dtype).itemsize


def _vmem_limit(block_bytes, scratch_bytes):
    need = 2 * block_bytes + scratch_bytes
    return int(min(need + max(need // 4, 8 << 20), V7X_VMEM_BYTES - (6 << 20)))


def _rmsnorm(x, g):
    r = lax.rsqrt(jnp.mean(x * x, axis=-1, keepdims=True) + EPS)
    return (x * r) * g


def _dot(a, b):
    return jnp.dot(a, b, preferred_element_type=F32)


def _dot_nt(a, b):
    return lax.dot_general(a, b, (((1,), (1,)), ((), ())), preferred_element_type=F32)


def _prompt_rows_map(i, j):
    return (i, 0)


def _prompt_tile_map(i, j):
    return (i, j)


def _sample_tile_map(n_pt):
    return lambda i, j: (0, jnp.where(i == n_pt - 1, j, 0))


def _const_map(i, j):
    return (0, 0)


def _norm_rows_kernel(xp_ref, xs_ref, g_ref, hp_ref, hs_ref, *, n_pt):
    hp_ref[...] = _rmsnorm(xp_ref[...], g_ref[...]).astype(BF16)

    @pl.when(pl.program_id(0) == n_pt - 1)
    def _():
        hs_ref[...] = _rmsnorm(xs_ref[...], g_ref[...]).astype(BF16)


def _norm_rows(xp, xs, g, *, layer):
    rp, d = xp.shape
    rs = xs.shape[0]
    tm = OUT_ROWS
    n_pt = rp // tm
    blocks = _nbytes((tm, d), F32) + _nbytes((tm, d), BF16) + _nbytes((rs, d), F32) + _nbytes((rs, d), BF16)
    return pl.pallas_call(
        functools.partial(_norm_rows_kernel, n_pt=n_pt),
        grid=(n_pt,),
        in_specs=[
            pl.BlockSpec((tm, d), lambda s: (s, 0)),
            pl.BlockSpec((rs, d), lambda s: (0, 0)),
            pl.BlockSpec((None, 1, d), lambda s: (layer, 0, 0)),
        ],
        out_specs=[
            pl.BlockSpec((tm, d), lambda s: (s, 0)),
            pl.BlockSpec((rs, d), lambda s: (0, 0)),
        ],
        out_shape=[jax.ShapeDtypeStruct((rp, d), BF16), jax.ShapeDtypeStruct((rs, d), BF16)],
        compiler_params=pltpu.CompilerParams(
            dimension_semantics=("arbitrary",),
            vmem_limit_bytes=_vmem_limit(blocks, 0)),
        name="norm_rows",
    )(xp, xs, g)


def _swiglu(h, wg, wu):
    gate = _dot(h, wg)
    up = _dot(h, wu)
    return (gate * jax.nn.sigmoid(gate) * up).astype(BF16)


def _ffn_up_kernel(hp_ref, hs_ref, wg_ref, wu_ref, ap_ref, as_ref, *, n_pt):
    wg, wu = wg_ref[...].astype(BF16), wu_ref[...].astype(BF16)
    for r in range(0, hp_ref.shape[0], MXU_ROWS):
        ap_ref[r:r + MXU_ROWS, :] = _swiglu(hp_ref[r:r + MXU_ROWS, :], wg, wu)

    @pl.when(pl.program_id(0) == n_pt - 1)
    def _():
        as_ref[...] = _swiglu(hs_ref[...], wg_ref[...].astype(BF16), wu_ref[...].astype(BF16))


def _ffn_up(hp, hs, wg, wu, *, layer):
    rp, d = hp.shape
    rs = hs.shape[0]
    n = wg.shape[2]
    tm, tn = FFN_ROW_TILE, FFN_COL_TILE
    n_pt, n_j = rp // tm, n // tn
    blocks = (_nbytes((tm, d), BF16) + _nbytes((rs, d), BF16) + 2 * _nbytes((d, tn), wg.dtype)
              + _nbytes((tm, tn), BF16) + _nbytes((rs, tn), BF16))
    temps = 2 * _nbytes((d, tn), BF16) + 3 * _nbytes((tm, tn), F32)
    w_spec = pl.BlockSpec((None, d, tn), lambda i, j: (layer, 0, j))
    return pl.pallas_call(
        functools.partial(_ffn_up_kernel, n_pt=n_pt),
        grid=(n_pt, n_j),
        in_specs=[
            pl.BlockSpec((tm, d), _prompt_rows_map),
            pl.BlockSpec((rs, d), _const_map),
            w_spec,
            w_spec,
        ],
        out_specs=[
            pl.BlockSpec((tm, tn), _prompt_tile_map),
            pl.BlockSpec((rs, tn), _sample_tile_map(n_pt)),
        ],
        out_shape=[jax.ShapeDtypeStruct((rp, n), BF16), jax.ShapeDtypeStruct((rs, n), BF16)],
        compiler_params=pltpu.CompilerParams(
            dimension_semantics=("arbitrary", "arbitrary"),
            vmem_limit_bytes=_vmem_limit(blocks, temps)),
        name="ffn_up",
    )(hp, hs, wg, wu)


def _out_proj_kernel(ap_ref, as_ref, w_ref, xp_ref, xs_ref, g_ref, op_ref, os_ref, hp_ref, hs_ref, *, n_pt):
    x = xp_ref[...] + _dot(ap_ref[...], w_ref[...])
    op_ref[...] = x
    hp_ref[...] = _rmsnorm(x, g_ref[...]).astype(BF16)

    @pl.when(pl.program_id(0) == n_pt - 1)
    def _():
        x = xs_ref[...] + _dot(as_ref[...], w_ref[...])
        os_ref[...] = x
        hs_ref[...] = _rmsnorm(x, g_ref[...]).astype(BF16)


def _out_proj(ap, as_, w, xp, xs, g, *, layer):
    rp, k = ap.shape
    rs = as_.shape[0]
    n = w.shape[1]
    tm = OUT_ROWS
    n_pt = rp // tm
    rows_map = lambda s: (s, 0)
    const = lambda s: (0, 0)
    once = dict(pipeline_mode=pl.Buffered(1))
    blocks = _nbytes((tm, k), BF16) + 2 * _nbytes((tm, n), F32) + _nbytes((tm, n), BF16)
    resident = (_nbytes((k, n), BF16) + _nbytes((rs, k), BF16) + _nbytes((rs, n), F32)
                + 2 * (_nbytes((rs, n), F32) + _nbytes((rs, n), BF16)))
    return pl.pallas_call(
        functools.partial(_out_proj_kernel, n_pt=n_pt),
        grid=(n_pt,),
        in_specs=[
            pl.BlockSpec((tm, k), rows_map),
            pl.BlockSpec((rs, k), const, **once),
            pl.BlockSpec((k, n), const, **once),
            pl.BlockSpec((tm, n), rows_map),
            pl.BlockSpec((rs, n), const, **once),
            pl.BlockSpec((None, 1, n), lambda s: (layer, 0, 0)),
        ],
        out_specs=[
            pl.BlockSpec((tm, n), rows_map),
            pl.BlockSpec((rs, n), const),
            pl.BlockSpec((tm, n), rows_map),
            pl.BlockSpec((rs, n), const),
        ],
        out_shape=[jax.ShapeDtypeStruct((rp, n), F32), jax.ShapeDtypeStruct((rs, n), F32),
                   jax.ShapeDtypeStruct((rp, n), BF16), jax.ShapeDtypeStruct((rs, n), BF16)],
        compiler_params=pltpu.CompilerParams(
            dimension_semantics=("arbitrary",),
            vmem_limit_bytes=_vmem_limit(blocks, resident + 2 * _nbytes((tm, n), F32))),
        name="out_proj",
    )(ap, as_, w, xp, xs, g)


def _matmul_kernel(hp_ref, hs_ref, w_ref, zp_ref, zs_ref, *, n_pt):
    w = w_ref[...].astype(BF16)
    for r in range(0, hp_ref.shape[0], MXU_ROWS):
        zp_ref[r:r + MXU_ROWS, :] = _dot(hp_ref[r:r + MXU_ROWS, :], w).astype(zp_ref.dtype)

    @pl.when(pl.program_id(0) == n_pt - 1)
    def _():
        zs_ref[...] = _dot(hs_ref[...], w_ref[...].astype(BF16)).astype(zs_ref.dtype)


def _matmul(hp, hs, w, *, w_layer):
    rp, d = hp.shape
    rs = hs.shape[0]
    n = w.shape[2]
    tm, tn = FFN_ROW_TILE, COL_TILE
    n_pt, n_j = rp // tm, n // tn
    blocks = (_nbytes((tm, d), BF16) + _nbytes((rs, d), BF16) + _nbytes((d, tn), F32)
              + _nbytes((tm, tn), BF16) + _nbytes((rs, tn), BF16))
    temps = _nbytes((d, tn), BF16) + _nbytes((tm, tn), F32)
    return pl.pallas_call(
        functools.partial(_matmul_kernel, n_pt=n_pt),
        grid=(n_pt, n_j),
        in_specs=[
            pl.BlockSpec((tm, d), _prompt_rows_map),
            pl.BlockSpec((rs, d), _const_map),
            pl.BlockSpec((None, d, tn), lambda i, j: (w_layer, 0, j)),
        ],
        out_specs=[
            pl.BlockSpec((tm, tn), _prompt_tile_map),
            pl.BlockSpec((rs, tn), _sample_tile_map(n_pt)),
        ],
        out_shape=[jax.ShapeDtypeStruct((rp, n), BF16), jax.ShapeDtypeStruct((rs, n), BF16)],
        compiler_params=pltpu.CompilerParams(
            dimension_semantics=("arbitrary", "arbitrary"),
            vmem_limit_bytes=_vmem_limit(blocks, temps)),
        name="matmul",
    )(hp, hs, w)


def _ffn_down_kernel(ap_ref, as_ref, w_ref, xp_ref, xs_ref, g_ref, *refs, n_pt, n_k, n_xc, final):
    i, k = pl.program_id(0), pl.program_id(1)
    if final:
        op_ref, os_ref, apl_ref, asl_ref, wl_ref = refs
    else:
        op_ref, os_ref, hp_ref, hs_ref, apl_ref, asl_ref, wl_ref = refs
    xc = xp_ref.shape[1]
    kf = (ap_ref.shape[1] // V7X_MXU_DEPTH) * V7X_MXU_DEPTH
    even = k % 2 == 0

    def whole_passes(a_ref):
        return _dot(a_ref[:, :kf], w_ref[:kf, :])

    def with_stash(a_ref, al_ref):
        return _dot(jnp.concatenate([al_ref[...], a_ref[...]], axis=1),
                    jnp.concatenate([wl_ref[...], w_ref[...]], axis=0))

    @pl.when(k == 0)
    def _():
        op_ref[...] = whole_passes(ap_ref)

    @pl.when(jnp.logical_and(k > 0, even))
    def _():
        op_ref[...] = op_ref[...] + whole_passes(ap_ref)

    @pl.when(jnp.logical_not(even))
    def _():
        op_ref[...] = op_ref[...] + with_stash(ap_ref, apl_ref)

    for c in range(n_xc):
        @pl.when(k == c)
        def _():
            op_ref[:, c * xc:(c + 1) * xc] = op_ref[:, c * xc:(c + 1) * xc] + xp_ref[...]

    @pl.when(k == n_k - 1)
    def _():
        if final:
            op_ref[...] = _rmsnorm(op_ref[...], g_ref[...])
        else:
            hp_ref[...] = _rmsnorm(op_ref[...], g_ref[...]).astype(BF16)

    @pl.when(i == n_pt - 1)
    def _():
        @pl.when(k == 0)
        def _():
            os_ref[...] = xs_ref[...] + whole_passes(as_ref)

        @pl.when(jnp.logical_and(k > 0, even))
        def _():
            os_ref[...] = os_ref[...] + whole_passes(as_ref)

        @pl.when(jnp.logical_not(even))
        def _():
            os_ref[...] = os_ref[...] + with_stash(as_ref, asl_ref)

        @pl.when(even)
        def _():
            asl_ref[...] = as_ref[:, kf:]

        @pl.when(k == n_k - 1)
        def _():
            if final:
                os_ref[...] = _rmsnorm(os_ref[...], g_ref[...])
            else:
                hs_ref[...] = _rmsnorm(os_ref[...], g_ref[...]).astype(BF16)

    @pl.when(even)
    def _():
        apl_ref[...] = ap_ref[:, kf:]
        wl_ref[...] = w_ref[kf:, :]


def _ffn_down(ap, as_, w, xp, xs, g, *, w_layer, g_layer, final):
    rp, kdim = ap.shape
    rs = as_.shape[0]
    n = w.shape[2]
    tm, tk, xc = PROMPT_TILE, DOWN_K_TILE, DOWN_X_CHUNK
    n_pt, n_k, n_xc = rp // tm, kdim // tk, n // xc
    k_left = tk % V7X_MXU_DEPTH
    assert n_xc <= n_k and n_k % 2 == 0 and 2 * k_left == V7X_MXU_DEPTH and w.dtype == BF16
    rows = lambda i, k: (i, 0)
    blocks = (_nbytes((tm, tk), BF16) + _nbytes((rs, tk), BF16) + _nbytes((tk, n), w.dtype)
              + _nbytes((tm, xc), F32) + _nbytes((rs, n), F32)
              + _nbytes((tm, n), F32) + _nbytes((rs, n), F32))
    out_specs = [pl.BlockSpec((tm, n), rows), pl.BlockSpec((rs, n), _const_map)]
    out_shape = [jax.ShapeDtypeStruct((rp, n), F32), jax.ShapeDtypeStruct((rs, n), F32)]
    if not final:
        blocks += _nbytes((tm, n), BF16) + _nbytes((rs, n), BF16)
        out_specs += [pl.BlockSpec((tm, n), rows), pl.BlockSpec((rs, n), _const_map)]
        out_shape += [jax.ShapeDtypeStruct((rp, n), BF16), jax.ShapeDtypeStruct((rs, n), BF16)]
    return pl.pallas_call(
        functools.partial(_ffn_down_kernel, n_pt=n_pt, n_k=n_k, n_xc=n_xc, final=final),
        grid=(n_pt, n_k),
        in_specs=[
            pl.BlockSpec((tm, tk), lambda i, k: (i, k)),
            pl.BlockSpec((rs, tk), lambda i, k: (0, jnp.where(i == n_pt - 1, k, 0))),
            pl.BlockSpec((None, tk, n), lambda i, k: (w_layer, k, 0)),
            pl.BlockSpec((tm, xc), lambda i, k: (i, jnp.minimum(k, n_xc - 1))),
            pl.BlockSpec((rs, n), _const_map),
            pl.BlockSpec((None, 1, n), lambda i, k: (g_layer, 0, 0)),
        ],
        out_specs=out_specs,
        out_shape=out_shape,
        scratch_shapes=[pltpu.VMEM((tm, k_left), BF16), pltpu.VMEM((rs, k_left), BF16),
                        pltpu.VMEM((k_left, n), BF16)],
        compiler_params=pltpu.CompilerParams(
            dimension_semantics=("arbitrary", "arbitrary"),
            vmem_limit_bytes=_vmem_limit(blocks, _nbytes((tk, n), BF16))),
        name="ffn_down",
    )(ap, as_, w, xp, xs, g)


def _mem_kv_kernel(x_ref, g_ref, w_ref, k_ref, v_ref):
    h = _rmsnorm(x_ref[...], g_ref[...]).astype(BF16)
    kv = _dot(h, w_ref[...].astype(BF16))
    k_ref[...] = kv[:, :MEM_WIDTH]
    v_ref[...] = kv[:, MEM_WIDTH:]


def _mem_kv(mem, g, w):
    rows, d = mem.shape
    depth = w.shape[0]
    tm = 512
    out = jax.ShapeDtypeStruct((depth, rows, MEM_WIDTH), F32)
    blocks = (_nbytes((tm, d), F32) + _nbytes((d, 2 * MEM_WIDTH), F32) + 2 * _nbytes((tm, MEM_WIDTH), F32))
    return pl.pallas_call(
        _mem_kv_kernel,
        grid=(depth, rows // tm),
        in_specs=[
            pl.BlockSpec((tm, d), lambda l, i: (i, 0)),
            pl.BlockSpec((None, 1, d), lambda l, i: (l, 0, 0)),
            pl.BlockSpec((None, d, 2 * MEM_WIDTH), lambda l, i: (l, 0, 0)),
        ],
        out_specs=[
            pl.BlockSpec((None, tm, MEM_WIDTH), lambda l, i: (l, i, 0)),
            pl.BlockSpec((None, tm, MEM_WIDTH), lambda l, i: (l, i, 0)),
        ],
        out_shape=[out, out],
        compiler_params=pltpu.CompilerParams(
            dimension_semantics=("arbitrary", "arbitrary"),
            vmem_limit_bytes=_vmem_limit(blocks, _nbytes((d, 2 * MEM_WIDTH), BF16))),
        name="mem_kv",
    )(mem, g, w)


def _cross_scores(q, k):
    return _dot_nt(q, k) * (MEM_HEAD_DIM ** -0.5)


def _cross_values(s, v):
    e = jnp.exp(s - jnp.max(s, axis=-1, keepdims=True)).astype(BF16)
    return _dot(e, v) / _dot(e, jnp.ones(v.shape, BF16))


def _run_ahead(n_items, lookahead, first, second):
    pending = {}
    for i in range(n_items + lookahead):
        if i < n_items:
            pending[i] = first(i)
        if i >= lookahead:
            second(i - lookahead, pending.pop(i - lookahead))


def _conv_prompt_kernel(z_ref, mk_ref, mv_ref, cw_ref, wd_ref, mix_ref, st_ref, wdb_ref, ext_ref, *, tiles_per_seq):
    s = pl.program_id(0)
    tq = z_ref.shape[0]
    wdb_ref[...] = wd_ref[...].astype(BF16)

    @pl.when(s % tiles_per_seq == 0)
    def _():
        ext_ref[0:CARRY_ROWS, :] = jnp.zeros((CARRY_ROWS, TOKEN_WIDTH), F32)

    c = z_ref[:, TOKEN_WIDTH:2 * TOKEN_WIDTH].astype(F32)
    u = z_ref[:, 2 * TOKEN_WIDTH:3 * TOKEN_WIDTH].astype(F32)
    cu = c * u
    ext_ref[CARRY_ROWS:CARRY_ROWS + tq, :] = cu
    conv = (cw_ref[0:1, :] * ext_ref[CARRY_ROWS - 2:CARRY_ROWS - 2 + tq, :]
            + cw_ref[1:2, :] * ext_ref[CARRY_ROWS - 1:CARRY_ROWS - 1 + tq, :]
            + cw_ref[2:3, :] * cu)
    b = z_ref[:, 0:TOKEN_WIDTH].astype(F32)
    mix_ref[:, 0:TOKEN_WIDTH] = (b * conv).astype(BF16)
    st_ref[...] = ext_ref[CARRY_ROWS + tq - 2:CARRY_ROWS + tq, :]
    ext_ref[0:CARRY_ROWS, :] = ext_ref[tq:tq + CARRY_ROWS, :]

    def scores(h):
        lo, hi = h * MEM_HEAD_DIM, (h + 1) * MEM_HEAD_DIM
        return _cross_scores(z_ref[:, 3 * TOKEN_WIDTH + lo:3 * TOKEN_WIDTH + hi], mk_ref[:, lo:hi].astype(BF16))

    def finish(h, s):
        lo, hi = h * MEM_HEAD_DIM, (h + 1) * MEM_HEAD_DIM
        mix_ref[:, TOKEN_WIDTH + lo:TOKEN_WIDTH + hi] = _cross_values(s, mv_ref[:, lo:hi].astype(BF16)).astype(BF16)

    _run_ahead(MEM_HEADS, SCORE_LOOKAHEAD, scores, finish)


def _conv_prompt(z, mk, mv, conv_w, w_down, *, layer, batch, seq):
    rows, zc = z.shape
    tq = CONV_ROWS
    tiles_per_seq = seq // tq
    n_steps = batch * tiles_per_seq
    d_ff, d = w_down.shape[1:]
    slab = d_ff // n_steps
    assert slab * n_steps == d_ff and slab % (2 * V7X_SUBLANES) == 0
    blocks = (_nbytes((tq, zc), BF16) + 2 * _nbytes((N_MEM, MEM_WIDTH), F32) + _nbytes((tq, D_MODEL), BF16)
              + _nbytes((slab, d), F32) + _nbytes((slab, d), BF16))
    scratch = _nbytes((tq + CARRY_ROWS, TOKEN_WIDTH), F32)
    return pl.pallas_call(
        functools.partial(_conv_prompt_kernel, tiles_per_seq=tiles_per_seq),
        grid=(n_steps,),
        in_specs=[
            pl.BlockSpec((tq, zc), lambda s: (s, 0)),
            pl.BlockSpec((None, N_MEM, MEM_WIDTH), lambda s: (layer, s // tiles_per_seq, 0)),
            pl.BlockSpec((None, N_MEM, MEM_WIDTH), lambda s: (layer, s // tiles_per_seq, 0)),
            pl.BlockSpec((None, CONV_WIDTH, TOKEN_WIDTH), lambda s: (0, 0, 0)),
            pl.BlockSpec((None, slab, d), lambda s: (layer, s, 0)),
        ],
        out_specs=[
            pl.BlockSpec((tq, D_MODEL), lambda s: (s, 0)),
            pl.BlockSpec((None, CONV_WIDTH - 1, TOKEN_WIDTH), lambda s: (s // tiles_per_seq, 0, 0)),
            pl.BlockSpec((slab, d), lambda s: (s, 0)),
        ],
        out_shape=[
            jax.ShapeDtypeStruct((rows, D_MODEL), BF16),
            jax.ShapeDtypeStruct((batch, CONV_WIDTH - 1, TOKEN_WIDTH), F32),
            jax.ShapeDtypeStruct((d_ff, d), BF16),
        ],
        scratch_shapes=[pltpu.VMEM((tq + CARRY_ROWS, TOKEN_WIDTH), F32)],
        compiler_params=pltpu.CompilerParams(
            dimension_semantics=("arbitrary",),
            vmem_limit_bytes=_vmem_limit(blocks, scratch + 6 * _nbytes((tq, TOKEN_WIDTH), F32))),
        name="conv_prompt",
    )(z, mk, mv, conv_w, w_down)


def _conv_sample_kernel(z_ref, st_ref, mk_ref, mv_ref, cw_ref, wo_ref, mix_ref, nst_ref, wob_ref,
                        ext_ref, mixf_ref, *, dec_seq):
    t = dec_seq
    wob_ref[...] = wo_ref[...].astype(BF16)
    for n in range(SAMPLE_GROUP):
        r0, r1 = n * t, (n + 1) * t
        c = z_ref[r0:r1, TOKEN_WIDTH:2 * TOKEN_WIDTH].astype(F32)
        u = z_ref[r0:r1, 2 * TOKEN_WIDTH:3 * TOKEN_WIDTH].astype(F32)
        cu = c * u
        ext_ref[CARRY_ROWS - 2:CARRY_ROWS, :] = st_ref[n]
        ext_ref[CARRY_ROWS:CARRY_ROWS + t, :] = cu
        conv = (cw_ref[0:1, :] * ext_ref[CARRY_ROWS - 2:CARRY_ROWS - 2 + t, :]
                + cw_ref[1:2, :] * ext_ref[CARRY_ROWS - 1:CARRY_ROWS - 1 + t, :]
                + cw_ref[2:3, :] * cu)
        b = z_ref[r0:r1, 0:TOKEN_WIDTH].astype(F32)
        mixf_ref[r0:r1, 0:TOKEN_WIDTH] = b * conv
        nst_ref[n] = ext_ref[CARRY_ROWS + t - 2:CARRY_ROWS + t, :]
    _sample_cross_attention(z_ref, mk_ref, mv_ref, mixf_ref, 3 * TOKEN_WIDTH, t)
    mix_ref[...] = mixf_ref[...].astype(BF16)


def _sample_cross_attention(z_ref, mk_ref, mv_ref, mixf_ref, qm_off, t):
    def scores(i):
        n, h = divmod(i, MEM_HEADS)
        q = z_ref[n * t:(n + 1) * t, qm_off + h * MEM_HEAD_DIM:qm_off + (h + 1) * MEM_HEAD_DIM]
        return _cross_scores(q, mk_ref[n, pl.ds(h, N_MEM, stride=MEM_HEADS), :].astype(BF16))

    def finish(i, s):
        n, h = divmod(i, MEM_HEADS)
        o = _cross_values(s, mv_ref[n, pl.ds(h, N_MEM, stride=MEM_HEADS), :].astype(BF16))
        mixf_ref[n * t:(n + 1) * t, TOKEN_WIDTH + h * MEM_HEAD_DIM:TOKEN_WIDTH + (h + 1) * MEM_HEAD_DIM] = o

    _run_ahead(SAMPLE_GROUP * MEM_HEADS, SAMPLE_LOOKAHEAD, scores, finish)


def _conv_sample(z, state, mem_k, mem_v, conv_w, w_out, *, layer, dec_seq):
    rows, zc = z.shape
    dec_batch = state.shape[1]
    g = SAMPLE_GROUP
    gr = g * dec_seq
    n_steps = dec_batch // g
    wk, wn = w_out.shape[1:]
    slab = wk // n_steps
    assert slab * n_steps == wk and slab % (2 * V7X_SUBLANES) == 0
    blocks = (_nbytes((gr, zc), BF16) + 2 * _nbytes((g, CONV_WIDTH - 1, TOKEN_WIDTH), F32)
              + 2 * _nbytes((g, N_MEM, V7X_SUBLANES, MEM_HEAD_DIM), F32) + _nbytes((gr, D_MODEL), BF16)
              + _nbytes((slab, wn), F32) + _nbytes((slab, wn), BF16))
    scratch = _nbytes((2 * CARRY_ROWS, TOKEN_WIDTH), F32) + _nbytes((gr, D_MODEL), F32)
    return pl.pallas_call(
        functools.partial(_conv_sample_kernel, dec_seq=dec_seq),
        grid=(n_steps,),
        in_specs=[
            pl.BlockSpec((gr, zc), lambda i: (i, 0)),
            pl.BlockSpec((None, g, CONV_WIDTH - 1, TOKEN_WIDTH), lambda i: (0, i, 0, 0)),
            pl.BlockSpec((None, g, N_MEM * MEM_HEADS, MEM_HEAD_DIM), lambda i: (layer, i, 0, 0)),
            pl.BlockSpec((None, g, N_MEM * MEM_HEADS, MEM_HEAD_DIM), lambda i: (layer, i, 0, 0)),
            pl.BlockSpec((None, CONV_WIDTH, TOKEN_WIDTH), lambda i: (0, 0, 0)),
            pl.BlockSpec((None, slab, wn), lambda i: (0, i, 0)),
        ],
        out_specs=[
            pl.BlockSpec((gr, D_MODEL), lambda i: (i, 0)),
            pl.BlockSpec((g, CONV_WIDTH - 1, TOKEN_WIDTH), lambda i: (i, 0, 0)),
            pl.BlockSpec((slab, wn), lambda i: (i, 0)),
        ],
        out_shape=[
            jax.ShapeDtypeStruct((rows, D_MODEL), BF16),
            jax.ShapeDtypeStruct((dec_batch, CONV_WIDTH - 1, TOKEN_WIDTH), F32),
            jax.ShapeDtypeStruct((wk, wn), BF16),
        ],
        scratch_shapes=[pltpu.VMEM((2 * CARRY_ROWS, TOKEN_WIDTH), F32), pltpu.VMEM((gr, D_MODEL), F32)],
        compiler_params=pltpu.CompilerParams(
            dimension_semantics=("arbitrary",),
            vmem_limit_bytes=_vmem_limit(blocks, scratch)),
        name="conv_sample",
    )(z, state, mem_k, mem_v, conv_w, w_out)


def _band_scores(q, k):
    return _dot_nt(q * (HEAD_DIM ** -0.5), k)


def _band_probs(s, sink, upper, upper_visible):
    s = jnp.where(upper_visible, s[:, :WINDOW], jnp.where(upper, -jnp.inf, s[:, WINDOW:]))
    m = jnp.maximum(jnp.max(s, axis=-1, keepdims=True), sink)
    e = jnp.exp(s - m)
    e = jnp.concatenate([jnp.where(upper, e, 0.0), jnp.where(upper, 0.0, e)], axis=1).astype(BF16)
    return e, jnp.exp(sink - m)


def _band_values(probs, v, ones):
    e, sink_term = probs
    return _dot(e, v) / (_dot(e, ones) + sink_term)


def _swa_prompt_kernel(sink_ref, zq_ref, zp_ref, mk_ref, mv_ref, *refs, blocks_per_seq, cast_steps):
    n_w = (len(refs) - 1) // 2
    w_refs, mix_ref, wb_refs = refs[:n_w], refs[n_w], refs[n_w + 1:]
    s = pl.program_id(0)

    for w_ref, wb_ref, n_cast in zip(w_refs, wb_refs, cast_steps):
        @pl.when(s < n_cast)
        def _():
            wb_ref[...] = w_ref[...].astype(BF16)

    k_off = TOKEN_WIDTH
    v_off = TOKEN_WIDTH + KV_WIDTH
    qm_off = TOKEN_WIDTH + 2 * KV_WIDTH
    n_blocks = zq_ref.shape[0] // WINDOW
    first_has_prev = (s % (blocks_per_seq // n_blocks)) > 0
    row = lax.broadcasted_iota(jnp.int32, (WINDOW, WINDOW), 0)
    col = lax.broadcasted_iota(jnp.int32, (WINDOW, WINDOW), 1)
    upper = col > row
    upper_first = jnp.logical_and(upper, first_has_prev)
    ones = jnp.ones((2 * WINDOW, HEAD_DIM), BF16)

    def block_rows(b):
        return slice(b * WINDOW, (b + 1) * WINDOW)

    @functools.lru_cache(maxsize=None)
    def window(b, kh, col_prev, col_cur):
        lo, hi = kh * HEAD_DIM, (kh + 1) * HEAD_DIM
        prev = (zp_ref[:, col_prev + lo:col_prev + hi] if b == 0
                else zq_ref[block_rows(b - 1), col_cur + lo:col_cur + hi])
        return jnp.concatenate([prev, zq_ref[block_rows(b), col_cur + lo:col_cur + hi]], axis=0)

    def band_scores(i):
        b, h = divmod(i, N_HEADS)
        return _band_scores(zq_ref[block_rows(b), h * HEAD_DIM:(h + 1) * HEAD_DIM], window(b, h // GROUP, 0, k_off))

    def band_finish(i, s):
        b, h = divmod(i, N_HEADS)
        p = _band_probs(s, sink_ref[h], upper, upper_first if b == 0 else upper)
        o = _band_values(p, window(b, h // GROUP, KV_WIDTH, v_off), ones)
        mix_ref[block_rows(b), h * HEAD_DIM:(h + 1) * HEAD_DIM] = o.astype(BF16)

    def cross_scores(h):
        lo, hi = h * MEM_HEAD_DIM, (h + 1) * MEM_HEAD_DIM
        return _cross_scores(zq_ref[:, qm_off + lo:qm_off + hi], mk_ref[:, lo:hi].astype(BF16))

    def cross_finish(h, s):
        lo, hi = h * MEM_HEAD_DIM, (h + 1) * MEM_HEAD_DIM
        mix_ref[:, TOKEN_WIDTH + lo:TOKEN_WIDTH + hi] = _cross_values(s, mv_ref[:, lo:hi].astype(BF16)).astype(BF16)

    _run_ahead(n_blocks * N_HEADS, SCORE_LOOKAHEAD, band_scores, band_finish)
    _run_ahead(MEM_HEADS, SCORE_LOOKAHEAD, cross_scores, cross_finish)


def _swa_prompt(z, mk, mv, sinks, weights, *, layer, batch, seq):
    rows, zc = z.shape
    tq = SWA_ROWS
    blocks_per_seq = seq // WINDOW
    steps_per_seq = seq // tq
    blocks_per_step = tq // WINDOW
    n_steps = batch * steps_per_seq
    kv_col_block = TOKEN_WIDTH // (2 * KV_WIDTH)
    bf16_rows = 2 * V7X_SUBLANES
    blocks = (_nbytes((tq, zc), BF16) + _nbytes((WINDOW, 2 * KV_WIDTH), BF16)
              + 2 * _nbytes((N_MEM, MEM_WIDTH), F32) + _nbytes((tq, D_MODEL), BF16))
    w_in_specs, w_out_specs, w_out_shapes, cast_steps = [], [], [], []
    for w in weights:
        r, c = w.shape[1:]
        n_cast = n_steps if r % (n_steps * bf16_rows) == 0 else n_steps // 2
        slab = r // n_cast
        assert slab * n_cast == r and slab % bf16_rows == 0
        blocks += _nbytes((slab, c), F32) + _nbytes((slab, c), BF16)
        w_in_specs.append(pl.BlockSpec((None, slab, c), lambda s, n=n_cast: (layer, jnp.minimum(s, n - 1), 0)))
        w_out_specs.append(pl.BlockSpec((slab, c), lambda s, n=n_cast: (jnp.minimum(s, n - 1), 0)))
        w_out_shapes.append(jax.ShapeDtypeStruct((r, c), BF16))
        cast_steps.append(n_cast)
    return pl.pallas_call(
        functools.partial(_swa_prompt_kernel, blocks_per_seq=blocks_per_seq, cast_steps=tuple(cast_steps)),
        grid=(n_steps,),
        in_specs=[
            pl.BlockSpec(memory_space=pltpu.SMEM),
            pl.BlockSpec((tq, zc), lambda s: (s, 0)),
            pl.BlockSpec((WINDOW, 2 * KV_WIDTH), lambda s: (jnp.maximum(s * blocks_per_step - 1, 0), kv_col_block)),
            pl.BlockSpec((None, N_MEM, MEM_WIDTH), lambda s: (layer, s // steps_per_seq, 0)),
            pl.BlockSpec((None, N_MEM, MEM_WIDTH), lambda s: (layer, s // steps_per_seq, 0)),
        ] + w_in_specs,
        out_specs=[pl.BlockSpec((tq, D_MODEL), lambda s: (s, 0))] + w_out_specs,
        out_shape=[jax.ShapeDtypeStruct((rows, D_MODEL), BF16)] + w_out_shapes,
        compiler_params=pltpu.CompilerParams(
            dimension_semantics=("arbitrary",),
            vmem_limit_bytes=_vmem_limit(blocks, 0)),
        name="swa_prompt",
    )(sinks, z, z, mk, mv, *weights)


def _swa_sample_kernel(sink_ref, z_ref, ck_ref, cv_ref, mk_ref, mv_ref, wo_ref,
                       mix_ref, nk_ref, nv_ref, wob_ref, knew_ref, vnew_ref, mixf_ref, *, dec_seq):
    t = dec_seq
    wob_ref[...] = wo_ref[...].astype(BF16)
    k_off = TOKEN_WIDTH
    v_off = TOKEN_WIDTH + KV_WIDTH
    qm_off = TOKEN_WIDTH + 2 * KV_WIDTH
    rows = GROUP * t
    qi = lax.broadcasted_iota(jnp.int32, (rows, WINDOW), 0) % t
    col = lax.broadcasted_iota(jnp.int32, (rows, WINDOW), 1)
    upper = col > qi
    knew_ref[...] = jnp.zeros(knew_ref.shape, F32)
    vnew_ref[...] = jnp.zeros(vnew_ref.shape, F32)
    for n in range(SAMPLE_GROUP):
        r0, r1 = n * t, (n + 1) * t
        k_new = z_ref[r0:r1, k_off:k_off + KV_WIDTH].astype(F32)
        v_new = z_ref[r0:r1, v_off:v_off + KV_WIDTH].astype(F32)
        knew_ref[n, 0:t, :] = k_new
        vnew_ref[n, 0:t, :] = v_new
        nk_ref[n, 0:WINDOW - t, :] = ck_ref[n, t:WINDOW, :]
        nv_ref[n, 0:WINDOW - t, :] = cv_ref[n, t:WINDOW, :]
        nk_ref[n, WINDOW - t:WINDOW, :] = k_new
        nv_ref[n, WINDOW - t:WINDOW, :] = v_new

    def scores(i):
        n, kh = divmod(i, N_KV_HEADS)
        lo, hi = kh * HEAD_DIM, (kh + 1) * HEAD_DIM
        k = jnp.concatenate([ck_ref[n, :, lo:hi], knew_ref[n, :, lo:hi]], axis=0).astype(BF16)
        q = jnp.concatenate(
            [z_ref[n * t:(n + 1) * t, (kh * GROUP + g) * HEAD_DIM:(kh * GROUP + g + 1) * HEAD_DIM].astype(F32)
             for g in range(GROUP)], axis=0).astype(BF16)
        return _band_scores(q, k)

    def finish(i, s):
        n, kh = divmod(i, N_KV_HEADS)
        lo, hi = kh * HEAD_DIM, (kh + 1) * HEAD_DIM
        v = jnp.concatenate([cv_ref[n, :, lo:hi], vnew_ref[n, :, lo:hi]], axis=0).astype(BF16)
        sink = jnp.concatenate(
            [jnp.full((t, 1), sink_ref[kh * GROUP + g], F32) for g in range(GROUP)], axis=0)
        o = _band_values(_band_probs(s, sink, upper, upper), v, jnp.ones((2 * WINDOW, HEAD_DIM), BF16))
        for g in range(GROUP):
            h = kh * GROUP + g
            mixf_ref[n * t:(n + 1) * t, h * HEAD_DIM:(h + 1) * HEAD_DIM] = o[g * t:(g + 1) * t, :]

    _run_ahead(SAMPLE_GROUP * N_KV_HEADS, SAMPLE_LOOKAHEAD, scores, finish)
    _sample_cross_attention(z_ref, mk_ref, mv_ref, mixf_ref, qm_off, t)
    mix_ref[...] = mixf_ref[...].astype(BF16)


def _swa_sample(z, cache_k, cache_v, mem_k, mem_v, sinks, w_out, *, layer, dec_seq):
    rows, zc = z.shape
    dec_batch = cache_k.shape[0]
    g = SAMPLE_GROUP
    gr = g * dec_seq
    n_steps = dec_batch // g
    wk, wn = w_out.shape[1:]
    slab = wk // n_steps
    assert slab * n_steps == wk and slab % (2 * V7X_SUBLANES) == 0
    win = jax.ShapeDtypeStruct((dec_batch, WINDOW, KV_WIDTH), F32)
    blocks = (_nbytes((gr, zc), BF16) + 4 * _nbytes((g, WINDOW, KV_WIDTH), F32)
              + 2 * _nbytes((g, N_MEM, V7X_SUBLANES, MEM_HEAD_DIM), F32) + _nbytes((gr, D_MODEL), BF16)
              + _nbytes((slab, wn), F32) + _nbytes((slab, wn), BF16))
    scratch = 2 * _nbytes((g, WINDOW, KV_WIDTH), F32) + _nbytes((gr, D_MODEL), F32)
    return pl.pallas_call(
        functools.partial(_swa_sample_kernel, dec_seq=dec_seq),
        grid=(n_steps,),
        in_specs=[
            pl.BlockSpec(memory_space=pltpu.SMEM),
            pl.BlockSpec((gr, zc), lambda i: (i, 0)),
            pl.BlockSpec((g, WINDOW, KV_WIDTH), lambda i: (i, 0, 0)),
            pl.BlockSpec((g, WINDOW, KV_WIDTH), lambda i: (i, 0, 0)),
            pl.BlockSpec((None, g, N_MEM * MEM_HEADS, MEM_HEAD_DIM), lambda i: (layer, i, 0, 0)),
            pl.BlockSpec((None, g, N_MEM * MEM_HEADS, MEM_HEAD_DIM), lambda i: (layer, i, 0, 0)),
            pl.BlockSpec((None, slab, wn), lambda i: (0, i, 0)),
        ],
        out_specs=[
            pl.BlockSpec((gr, D_MODEL), lambda i: (i, 0)),
            pl.BlockSpec((g, WINDOW, KV_WIDTH), lambda i: (i, 0, 0)),
            pl.BlockSpec((g, WINDOW, KV_WIDTH), lambda i: (i, 0, 0)),
            pl.BlockSpec((slab, wn), lambda i: (i, 0)),
        ],
        out_shape=[jax.ShapeDtypeStruct((rows, D_MODEL), BF16), win, win,
                   jax.ShapeDtypeStruct((wk, wn), BF16)],
        scratch_shapes=[pltpu.VMEM((g, WINDOW, KV_WIDTH), F32), pltpu.VMEM((g, WINDOW, KV_WIDTH), F32),
                        pltpu.VMEM((gr, D_MODEL), F32)],
        compiler_params=pltpu.CompilerParams(
            dimension_semantics=("arbitrary",),
            vmem_limit_bytes=_vmem_limit(blocks, scratch)),
        name="swa_sample",
    )(sinks, z, cache_k, cache_v, mem_k, mem_v, w_out)


def kernel(x_prompt, x_sample, mem_prompt, state_conv, cache_win_k, cache_win_v, cache_mem_k, cache_mem_v,
           norm_mix, norm_mem, w_mem_kv, norm_ffn, w_gate, w_up, w_down,
           conv_w_in, conv_w, conv_w_out, attn_w_in, attn_sinks, attn_w_out, norm_final):
    batch, seq, d = x_prompt.shape
    dec_batch, dec_seq, _ = x_sample.shape
    depth = norm_mix.shape[0]
    d_ff = w_gate.shape[2]
    prompt_rows = batch * seq
    sample_rows = dec_batch * dec_seq
    assert d == D_MODEL and depth == 2 and seq % CONV_ROWS == 0 and seq % SWA_ROWS == 0 and SWA_ROWS % WINDOW == 0
    assert prompt_rows % PROMPT_TILE == 0 and dec_batch % SAMPLE_GROUP == 0
    assert dec_seq == V7X_SUBLANES and d_ff % FFN_COL_TILE == 0 and d % COL_TILE == 0
    assert d_ff % DOWN_K_TILE == 0 and d % DOWN_X_CHUNK == 0
    assert prompt_rows % FFN_ROW_TILE == 0 and prompt_rows % sample_rows == 0

    xp = x_prompt.reshape(prompt_rows, d)
    xs = x_sample.reshape(sample_rows, d)
    mem = mem_prompt.reshape(batch * N_MEM, d)
    mem_k_s = cache_mem_k.reshape(depth, dec_batch, N_MEM * MEM_HEADS, MEM_HEAD_DIM)
    mem_v_s = cache_mem_v.reshape(depth, dec_batch, N_MEM * MEM_HEADS, MEM_HEAD_DIM)
    g_mix = norm_mix.reshape(depth, 1, d)
    g_ffn = norm_ffn.reshape(depth, 1, d)

    mk, mv = _mem_kv(mem, norm_mem.reshape(depth, 1, d), w_mem_kv)

    hp, hs = _norm_rows(xp, xs, g_mix, layer=0)
    zp, zs = _matmul(hp, hs, conv_w_in, w_layer=0)
    mix_s, conv_s, wo_bf = _conv_sample(zs, state_conv, mem_k_s, mem_v_s, conv_w, conv_w_out,
                                        layer=0, dec_seq=dec_seq)
    mix_p, conv_p, wd_bf = _conv_prompt(zp, mk, mv, conv_w, w_down, layer=0, batch=batch, seq=seq)
    xp, xs, hp, hs = _out_proj(mix_p, mix_s, wo_bf, xp, xs, g_ffn, layer=0)
    ap, as_ = _ffn_up(hp, hs, w_gate, w_up, layer=0)
    xp, xs, hp, hs = _ffn_down(ap, as_, wd_bf[None], xp, xs, g_mix, w_layer=0, g_layer=1, final=False)

    zp, zs = _matmul(hp, hs, attn_w_in, w_layer=0)
    sinks = attn_sinks[0]
    mix_s, win_k_s, win_v_s, wo_bf = _swa_sample(
        zs, cache_win_k[0].reshape(dec_batch, WINDOW, KV_WIDTH), cache_win_v[0].reshape(dec_batch, WINDOW, KV_WIDTH),
        mem_k_s, mem_v_s, sinks, attn_w_out, layer=1, dec_seq=dec_seq)
    mix_p, wd_bf = _swa_prompt(zp, mk, mv, sinks, [w_down], layer=1, batch=batch, seq=seq)
    xp, xs, hp, hs = _out_proj(mix_p, mix_s, wo_bf, xp, xs, g_ffn, layer=1)
    ap, as_ = _ffn_up(hp, hs, w_gate, w_up, layer=1)
    y_prompt, y_sample = _ffn_down(ap, as_, wd_bf[None], xp, xs, norm_final.reshape(1, 1, d),
                                   w_layer=0, g_layer=0, final=True)

    win_p = zp.reshape(batch, seq, -1)[:, seq - WINDOW:, TOKEN_WIDTH:TOKEN_WIDTH + 2 * KV_WIDTH].astype(F32)
    kv_shape = (1, -1, WINDOW, N_KV_HEADS, HEAD_DIM)
    mem_shape = (depth, batch, N_MEM, MEM_HEADS, MEM_HEAD_DIM)
    return (y_prompt.reshape(batch, seq, d),
            y_sample.reshape(dec_batch, dec_seq, d),
            conv_p[None],
            conv_s[None],
            win_p[..., :KV_WIDTH].reshape(kv_shape),
            win_p[..., KV_WIDTH:].reshape(kv_shape),
            win_k_s.reshape(kv_shape),
            win_v_s.reshape(kv_shape),
            mk.reshape(mem_shape),
            mv.reshape(mem_shape))
```

```python
import functools

import jax
import jax.numpy as jnp
from jax import lax
from jax.experimental import pallas as pl
from jax.experimental.pallas import tpu as pltpu

F32 = jnp.float32
BF16 = jnp.bfloat16

D_MODEL = 2048
N_MEM = 256
MEM_HEADS = 4
MEM_WIDTH = D_MODEL // 4
MEM_HEAD_DIM = MEM_WIDTH // MEM_HEADS
TOKEN_WIDTH = D_MODEL - MEM_WIDTH
CONV_WIDTH = 3
WINDOW = 128
HEAD_DIM = 64
N_HEADS = TOKEN_WIDTH // HEAD_DIM
N_KV_HEADS = 4
GROUP = N_HEADS // N_KV_HEADS
KV_WIDTH = N_KV_HEADS * HEAD_DIM
EPS = 1e-6

V7X_VMEM_BYTES = 64 * 1024 * 1024
V7X_SUBLANES = 8
V7X_MXU_DEPTH = 256

PROMPT_TILE = 1024
COL_TILE = 512
DOWN_K_TILE = 1408
DOWN_X_CHUNK = 512
FFN_ROW_TILE = 2048
FFN_COL_TILE = 512
MXU_ROWS = 1024
OUT_ROWS = 512
CONV_ROWS = 512
SWA_ROWS = 512
SCORE_LOOKAHEAD = 5
SAMPLE_LOOKAHEAD = 4
SAMPLE_GROUP = 8
CARRY_ROWS = V7X_SUBLANES


def _nbytes(shape, dtype):
    n = 1
    for s in shape:
        n *= s
    return n * jnp.dtype(dtype).itemsize


def _vmem_limit(block_bytes, scratch_bytes):
    need = 2 * block_bytes + scratch_bytes
    return int(min(need + max(need // 4, 8 << 20), V7X_VMEM_BYTES - (6 << 20)))


def _rmsnorm(x, g):
    r = lax.rsqrt(jnp.mean(x * x, axis=-1, keepdims=True) + EPS)
    return (x * r) * g


def _dot(a, b):
    return jnp.dot(a, b, preferred_element_type=F32)


def _dot_nt(a, b):
    return lax.dot_general(a, b, (((1,), (1,)), ((), ())), preferred_element_type=F32)


def _prompt_rows_map(i, j):
    return (i, 0)


def _prompt_tile_map(i, j):
    return (i, j)


def _sample_tile_map(n_pt):
    return lambda i, j: (0, jnp.where(i == n_pt - 1, j, 0))


def _const_map(i, j):
    return (0, 0)


def _norm_rows_kernel(xp_ref, xs_ref, g_ref, hp_ref, hs_ref, *, n_pt):
    hp_ref[...] = _rmsnorm(xp_ref[...], g_ref[...]).astype(BF16)

    @pl.when(pl.program_id(0) == n_pt - 1)
    def _():
        hs_ref[...] = _rmsnorm(xs_ref[...], g_ref[...]).astype(BF16)


def _norm_rows(xp, xs, g, *, layer):
    rp, d = xp.shape
    rs = xs.shape[0]
    tm = OUT_ROWS
    n_pt = rp // tm
    blocks = _nbytes((tm, d), F32) + _nbytes((tm, d), BF16) + _nbytes((rs, d), F32) + _nbytes((rs, d), BF16)
    return pl.pallas_call(
        functools.partial(_norm_rows_kernel, n_pt=n_pt),
        grid=(n_pt,),
        in_specs=[
            pl.BlockSpec((tm, d), lambda s: (s, 0)),
            pl.BlockSpec((rs, d), lambda s: (0, 0)),
            pl.BlockSpec((None, 1, d), lambda s: (layer, 0, 0)),
        ],
        out_specs=[
            pl.BlockSpec((tm, d), lambda s: (s, 0)),
            pl.BlockSpec((rs, d), lambda s: (0, 0)),
        ],
        out_shape=[jax.ShapeDtypeStruct((rp, d), BF16), jax.ShapeDtypeStruct((rs, d), BF16)],
        compiler_params=pltpu.CompilerParams(
            dimension_semantics=("arbitrary",),
            vmem_limit_bytes=_vmem_limit(blocks, 0)),
        name="norm_rows",
    )(xp, xs, g)


def _swiglu(h, wg, wu):
    gate = _dot(h, wg)
    up = _dot(h, wu)
    return (gate * jax.nn.sigmoid(gate) * up).astype(BF16)


def _ffn_up_kernel(hp_ref, hs_ref, wg_ref, wu_ref, ap_ref, as_ref, *, n_pt):
    wg, wu = wg_ref[...].astype(BF16), wu_ref[...].astype(BF16)
    for r in range(0, hp_ref.shape[0], MXU_ROWS):
        ap_ref[r:r + MXU_ROWS, :] = _swiglu(hp_ref[r:r + MXU_ROWS, :], wg, wu)

    @pl.when(pl.program_id(0) == n_pt - 1)
    def _():
        as_ref[...] = _swiglu(hs_ref[...], wg_ref[...].astype(BF16), wu_ref[...].astype(BF16))


def _ffn_up(hp, hs, wg, wu, *, layer):
    rp, d = hp.shape
    rs = hs.shape[0]
    n = wg.shape[2]
    tm, tn = FFN_ROW_TILE, FFN_COL_TILE
    n_pt, n_j = rp // tm, n // tn
    blocks = (_nbytes((tm, d), BF16) + _nbytes((rs, d), BF16) + 2 * _nbytes((d, tn), wg.dtype)
              + _nbytes((tm, tn), BF16) + _nbytes((rs, tn), BF16))
    temps = 2 * _nbytes((d, tn), BF16) + 3 * _nbytes((tm, tn), F32)
    w_spec = pl.BlockSpec((None, d, tn), lambda i, j: (layer, 0, j))
    return pl.pallas_call(
        functools.partial(_ffn_up_kernel, n_pt=n_pt),
        grid=(n_pt, n_j),
        in_specs=[
            pl.BlockSpec((tm, d), _prompt_rows_map),
            pl.BlockSpec((rs, d), _const_map),
            w_spec,
            w_spec,
        ],
        out_specs=[
            pl.BlockSpec((tm, tn), _prompt_tile_map),
            pl.BlockSpec((rs, tn), _sample_tile_map(n_pt)),
        ],
        out_shape=[jax.ShapeDtypeStruct((rp, n), BF16), jax.ShapeDtypeStruct((rs, n), BF16)],
        compiler_params=pltpu.CompilerParams(
            dimension_semantics=("arbitrary", "arbitrary"),
            vmem_limit_bytes=_vmem_limit(blocks, temps)),
        name="ffn_up",
    )(hp, hs, wg, wu)


def _out_proj_kernel(ap_ref, as_ref, w_ref, xp_ref, xs_ref, g_ref, op_ref, os_ref, hp_ref, hs_ref, *, n_pt):
    x = xp_ref[...] + _dot(ap_ref[...], w_ref[...])
    op_ref[...] = x
    hp_ref[...] = _rmsnorm(x, g_ref[...]).astype(BF16)

    @pl.when(pl.program_id(0) == n_pt - 1)
    def _():
        x = xs_ref[...] + _dot(as_ref[...], w_ref[...])
        os_ref[...] = x
        hs_ref[...] = _rmsnorm(x, g_ref[...]).astype(BF16)


def _out_proj(ap, as_, w, xp, xs, g, *, layer):
    rp, k = ap.shape
    rs = as_.shape[0]
    n = w.shape[1]
    tm = OUT_ROWS
    n_pt = rp // tm
    rows_map = lambda s: (s, 0)
    const = lambda s: (0, 0)
    once = dict(pipeline_mode=pl.Buffered(1))
    blocks = _nbytes((tm, k), BF16) + 2 * _nbytes((tm, n), F32) + _nbytes((tm, n), BF16)
    resident = (_nbytes((k, n), BF16) + _nbytes((rs, k), BF16) + _nbytes((rs, n), F32)
                + 2 * (_nbytes((rs, n), F32) + _nbytes((rs, n), BF16)))
    return pl.pallas_call(
        functools.partial(_out_proj_kernel, n_pt=n_pt),
        grid=(n_pt,),
        in_specs=[
            pl.BlockSpec((tm, k), rows_map),
            pl.BlockSpec((rs, k), const, **once),
            pl.BlockSpec((k, n), const, **once),
            pl.BlockSpec((tm, n), rows_map),
            pl.BlockSpec((rs, n), const, **once),
            pl.BlockSpec((None, 1, n), lambda s: (layer, 0, 0)),
        ],
        out_specs=[
            pl.BlockSpec((tm, n), rows_map),
            pl.BlockSpec((rs, n), const),
            pl.BlockSpec((tm, n), rows_map),
            pl.BlockSpec((rs, n), const),
        ],
        out_shape=[jax.ShapeDtypeStruct((rp, n), F32), jax.ShapeDtypeStruct((rs, n), F32),
                   jax.ShapeDtypeStruct((rp, n), BF16), jax.ShapeDtypeStruct((rs, n), BF16)],
        compiler_params=pltpu.CompilerParams(
            dimension_semantics=("arbitrary",),
            vmem_limit_bytes=_vmem_limit(blocks, resident + 2 * _nbytes((tm, n), F32))),
        name="out_proj",
    )(ap, as_, w, xp, xs, g)


def _matmul_kernel(hp_ref, hs_ref, w_ref, zp_ref, zs_ref, *, n_pt):
    w = w_ref[...].astype(BF16)
    for r in range(0, hp_ref.shape[0], MXU_ROWS):
        zp_ref[r:r + MXU_ROWS, :] = _dot(hp_ref[r:r + MXU_ROWS, :], w).astype(zp_ref.dtype)

    @pl.when(pl.program_id(0) == n_pt - 1)
    def _():
        zs_ref[...] = _dot(hs_ref[...], w_ref[...].astype(BF16)).astype(zs_ref.dtype)


def _matmul(hp, hs, w, *, w_layer):
    rp, d = hp.shape
    rs = hs.shape[0]
    n = w.shape[2]
    tm, tn = FFN_ROW_TILE, COL_TILE
    n_pt, n_j = rp // tm, n // tn
    blocks = (_nbytes((tm, d), BF16) + _nbytes((rs, d), BF16) + _nbytes((d, tn), F32)
              + _nbytes((tm, tn), BF16) + _nbytes((rs, tn), BF16))
    temps = _nbytes((d, tn), BF16) + _nbytes((tm, tn), F32)
    return pl.pallas_call(
        functools.partial(_matmul_kernel, n_pt=n_pt),
        grid=(n_pt, n_j),
        in_specs=[
            pl.BlockSpec((tm, d), _prompt_rows_map),
            pl.BlockSpec((rs, d), _const_map),
            pl.BlockSpec((None, d, tn), lambda i, j: (w_layer, 0, j)),
        ],
        out_specs=[
            pl.BlockSpec((tm, tn), _prompt_tile_map),
            pl.BlockSpec((rs, tn), _sample_tile_map(n_pt)),
        ],
        out_shape=[jax.ShapeDtypeStruct((rp, n), BF16), jax.ShapeDtypeStruct((rs, n), BF16)],
        compiler_params=pltpu.CompilerParams(
            dimension_semantics=("arbitrary", "arbitrary"),
            vmem_limit_bytes=_vmem_limit(blocks, temps)),
        name="matmul",
    )(hp, hs, w)


def _ffn_down_kernel(ap_ref, as_ref, w_ref, xp_ref, xs_ref, g_ref, *refs, n_pt, n_k, n_xc, final):
    i, k = pl.program_id(0), pl.program_id(1)
    if final:
        op_ref, os_ref, apl_ref, asl_ref, wl_ref = refs
    else:
        op_ref, os_ref, hp_ref, hs_ref, apl_ref, asl_ref, wl_ref = refs
    xc = xp_ref.shape[1]
    kf = (ap_ref.shape[1] // V7X_MXU_DEPTH) * V7X_MXU_DEPTH
    even = k % 2 == 0

    def whole_passes(a_ref):
        return _dot(a_ref[:, :kf], w_ref[:kf, :])

    def with_stash(a_ref, al_ref):
        return _dot(jnp.concatenate([al_ref[...], a_ref[...]], axis=1),
                    jnp.concatenate([wl_ref[...], w_ref[...]], axis=0))

    for kk in range(n_k):
        @pl.when(k == kk)
        def _():
            cols = slice(kk * xc, (kk + 1) * xc)
            if 0 < kk < n_xc:
                op_ref[:, cols] = op_ref[:, cols] + xp_ref[...]
            contrib = whole_passes(ap_ref) if kk % 2 == 0 else with_stash(ap_ref, apl_ref)
            if kk == 0:
                op_ref[...] = contrib
                op_ref[:, cols] = op_ref[:, cols] + xp_ref[...]
            else:
                op_ref[...] = op_ref[...] + contrib
            if kk == n_k - 1:
                if final:
                    op_ref[...] = _rmsnorm(op_ref[...], g_ref[...])
                else:
                    hp_ref[...] = _rmsnorm(op_ref[...], g_ref[...]).astype(BF16)

    @pl.when(i == n_pt - 1)
    def _():
        @pl.when(k == 0)
        def _():
            os_ref[...] = xs_ref[...] + whole_passes(as_ref)

        @pl.when(jnp.logical_and(k > 0, even))
        def _():
            os_ref[...] = os_ref[...] + whole_passes(as_ref)

        @pl.when(jnp.logical_not(even))
        def _():
            os_ref[...] = os_ref[...] + with_stash(as_ref, asl_ref)

        @pl.when(even)
        def _():
            asl_ref[...] = as_ref[:, kf:]

        @pl.when(k == n_k - 1)
        def _():
            if final:
                os_ref[...] = _rmsnorm(os_ref[...], g_ref[...])
            else:
                hs_ref[...] = _rmsnorm(os_ref[...], g_ref[...]).astype(BF16)

    @pl.when(even)
    def _():
        apl_ref[...] = ap_ref[:, kf:]
        wl_ref[...] = w_ref[kf:, :]


def _ffn_down(ap, as_, w, xp, xs, g, *, w_layer, g_layer, final):
    rp, kdim = ap.shape
    rs = as_.shape[0]
    n = w.shape[2]
    tm, tk, xc = PROMPT_TILE, DOWN_K_TILE, DOWN_X_CHUNK
    n_pt, n_k, n_xc = rp // tm, kdim // tk, n // xc
    k_left = tk % V7X_MXU_DEPTH
    assert n_xc <= n_k and n_k % 2 == 0 and 2 * k_left == V7X_MXU_DEPTH and w.dtype == BF16
    rows = lambda i, k: (i, 0)
    blocks = (_nbytes((tm, tk), BF16) + _nbytes((rs, tk), BF16) + _nbytes((tk, n), w.dtype)
              + _nbytes((tm, xc), F32) + _nbytes((rs, n), F32)
              + _nbytes((tm, n), F32) + _nbytes((rs, n), F32))
    out_specs = [pl.BlockSpec((tm, n), rows), pl.BlockSpec((rs, n), _const_map)]
    out_shape = [jax.ShapeDtypeStruct((rp, n), F32), jax.ShapeDtypeStruct((rs, n), F32)]
    if not final:
        blocks += _nbytes((tm, n), BF16) + _nbytes((rs, n), BF16)
        out_specs += [pl.BlockSpec((tm, n), rows), pl.BlockSpec((rs, n), _const_map)]
        out_shape += [jax.ShapeDtypeStruct((rp, n), BF16), jax.ShapeDtypeStruct((rs, n), BF16)]
    return pl.pallas_call(
        functools.partial(_ffn_down_kernel, n_pt=n_pt, n_k=n_k, n_xc=n_xc, final=final),
        grid=(n_pt, n_k),
        in_specs=[
            pl.BlockSpec((tm, tk), lambda i, k: (i, k)),
            pl.BlockSpec((rs, tk), lambda i, k: (0, jnp.where(i == n_pt - 1, k, 0))),
            pl.BlockSpec((None, tk, n), lambda i, k: (w_layer, k, 0)),
            pl.BlockSpec((tm, xc), lambda i, k: (i, jnp.minimum(k, n_xc - 1))),
            pl.BlockSpec((rs, n), _const_map),
            pl.BlockSpec((None, 1, n), lambda i, k: (g_layer, 0, 0)),
        ],
        out_specs=out_specs,
        out_shape=out_shape,
        scratch_shapes=[pltpu.VMEM((tm, k_left), BF16), pltpu.VMEM((rs, k_left), BF16),
                        pltpu.VMEM((k_left, n), BF16)],
        compiler_params=pltpu.CompilerParams(
            dimension_semantics=("arbitrary", "arbitrary"),
            vmem_limit_bytes=_vmem_limit(blocks, _nbytes((tk, n), BF16))),
        name="ffn_down",
    )(ap, as_, w, xp, xs, g)


def _mem_kv_kernel(x_ref, g_ref, w_ref, k_ref, v_ref):
    h = _rmsnorm(x_ref[...], g_ref[...]).astype(BF16)
    kv = _dot(h, w_ref[...].astype(BF16))
    k_ref[...] = kv[:, :MEM_WIDTH]
    v_ref[...] = kv[:, MEM_WIDTH:]


def _mem_kv(mem, g, w):
    rows, d = mem.shape
    depth = w.shape[0]
    tm = 512
    out = jax.ShapeDtypeStruct((depth, rows, MEM_WIDTH), F32)
    blocks = (_nbytes((tm, d), F32) + _nbytes((d, 2 * MEM_WIDTH), F32) + 2 * _nbytes((tm, MEM_WIDTH), F32))
    return pl.pallas_call(
        _mem_kv_kernel,
        grid=(depth, rows // tm),
        in_specs=[
            pl.BlockSpec((tm, d), lambda l, i: (i, 0)),
            pl.BlockSpec((None, 1, d), lambda l, i: (l, 0, 0)),
            pl.BlockSpec((None, d, 2 * MEM_WIDTH), lambda l, i: (l, 0, 0)),
        ],
        out_specs=[
            pl.BlockSpec((None, tm, MEM_WIDTH), lambda l, i: (l, i, 0)),
            pl.BlockSpec((None, tm, MEM_WIDTH), lambda l, i: (l, i, 0)),
        ],
        out_shape=[out, out],
        compiler_params=pltpu.CompilerParams(
            dimension_semantics=("arbitrary", "arbitrary"),
            vmem_limit_bytes=_vmem_limit(blocks, _nbytes((d, 2 * MEM_WIDTH), BF16))),
        name="mem_kv",
    )(mem, g, w)


def _cross_scores(q, k):
    return _dot_nt(q, k) * (MEM_HEAD_DIM ** -0.5)


def _cross_values(s, v):
    e = jnp.exp(s - jnp.max(s, axis=-1, keepdims=True)).astype(BF16)
    return _dot(e, v) / _dot(e, jnp.ones(v.shape, BF16))


def _run_ahead(n_items, lookahead, first, second):
    pending = {}
    for i in range(n_items + lookahead):
        if i < n_items:
            pending[i] = first(i)
        if i >= lookahead:
            second(i - lookahead, pending.pop(i - lookahead))


def _conv_prompt_kernel(z_ref, mk_ref, mv_ref, cw_ref, wd_ref, mix_ref, st_ref, wdb_ref, ext_ref, *, tiles_per_seq):
    s = pl.program_id(0)
    tq = z_ref.shape[0]
    wdb_ref[...] = wd_ref[...].astype(BF16)

    @pl.when(s % tiles_per_seq == 0)
    def _():
        ext_ref[0:CARRY_ROWS, :] = jnp.zeros((CARRY_ROWS, TOKEN_WIDTH), F32)

    c = z_ref[:, TOKEN_WIDTH:2 * TOKEN_WIDTH].astype(F32)
    u = z_ref[:, 2 * TOKEN_WIDTH:3 * TOKEN_WIDTH].astype(F32)
    cu = c * u
    ext_ref[CARRY_ROWS:CARRY_ROWS + tq, :] = cu
    conv = (cw_ref[0:1, :] * ext_ref[CARRY_ROWS - 2:CARRY_ROWS - 2 + tq, :]
            + cw_ref[1:2, :] * ext_ref[CARRY_ROWS - 1:CARRY_ROWS - 1 + tq, :]
            + cw_ref[2:3, :] * cu)
    b = z_ref[:, 0:TOKEN_WIDTH].astype(F32)
    mix_ref[:, 0:TOKEN_WIDTH] = (b * conv).astype(BF16)
    st_ref[...] = ext_ref[CARRY_ROWS + tq - 2:CARRY_ROWS + tq, :]
    ext_ref[0:CARRY_ROWS, :] = ext_ref[tq:tq + CARRY_ROWS, :]

    def scores(h):
        lo, hi = h * MEM_HEAD_DIM, (h + 1) * MEM_HEAD_DIM
        return _cross_scores(z_ref[:, 3 * TOKEN_WIDTH + lo:3 * TOKEN_WIDTH + hi], mk_ref[:, lo:hi].astype(BF16))

    def finish(h, s):
        lo, hi = h * MEM_HEAD_DIM, (h + 1) * MEM_HEAD_DIM
        mix_ref[:, TOKEN_WIDTH + lo:TOKEN_WIDTH + hi] = _cross_values(s, mv_ref[:, lo:hi].astype(BF16)).astype(BF16)

    _run_ahead(MEM_HEADS, SCORE_LOOKAHEAD, scores, finish)


def _conv_prompt(z, mk, mv, conv_w, w_down, *, layer, batch, seq):
    rows, zc = z.shape
    tq = CONV_ROWS
    tiles_per_seq = seq // tq
    n_steps = batch * tiles_per_seq
    d_ff, d = w_down.shape[1:]
    slab = d_ff // n_steps
    assert slab * n_steps == d_ff and slab % (2 * V7X_SUBLANES) == 0
    blocks = (_nbytes((tq, zc), BF16) + 2 * _nbytes((N_MEM, MEM_WIDTH), F32) + _nbytes((tq, D_MODEL), BF16)
              + _nbytes((slab, d), F32) + _nbytes((slab, d), BF16))
    scratch = _nbytes((tq + CARRY_ROWS, TOKEN_WIDTH), F32)
    return pl.pallas_call(
        functools.partial(_conv_prompt_kernel, tiles_per_seq=tiles_per_seq),
        grid=(n_steps,),
        in_specs=[
            pl.BlockSpec((tq, zc), lambda s: (s, 0)),
            pl.BlockSpec((None, N_MEM, MEM_WIDTH), lambda s: (layer, s // tiles_per_seq, 0)),
            pl.BlockSpec((None, N_MEM, MEM_WIDTH), lambda s: (layer, s // tiles_per_seq, 0)),
            pl.BlockSpec((None, CONV_WIDTH, TOKEN_WIDTH), lambda s: (0, 0, 0)),
            pl.BlockSpec((None, slab, d), lambda s: (layer, s, 0)),
        ],
        out_specs=[
            pl.BlockSpec((tq, D_MODEL), lambda s: (s, 0)),
            pl.BlockSpec((None, CONV_WIDTH - 1, TOKEN_WIDTH), lambda s: (s // tiles_per_seq, 0, 0)),
            pl.BlockSpec((slab, d), lambda s: (s, 0)),
        ],
        out_shape=[
            jax.ShapeDtypeStruct((rows, D_MODEL), BF16),
            jax.ShapeDtypeStruct((batch, CONV_WIDTH - 1, TOKEN_WIDTH), F32),
            jax.ShapeDtypeStruct((d_ff, d), BF16),
        ],
        scratch_shapes=[pltpu.VMEM((tq + CARRY_ROWS, TOKEN_WIDTH), F32)],
        compiler_params=pltpu.CompilerParams(
            dimension_semantics=("arbitrary",),
            vmem_limit_bytes=_vmem_limit(blocks, scratch + 6 * _nbytes((tq, TOKEN_WIDTH), F32))),
        name="conv_prompt",
    )(z, mk, mv, conv_w, w_down)


def _conv_sample_kernel(z_ref, st_ref, mk_ref, mv_ref, cw_ref, wo_ref, mix_ref, nst_ref, wob_ref,
                        ext_ref, mixf_ref, *, dec_seq):
    t = dec_seq
    wob_ref[...] = wo_ref[...].astype(BF16)
    for n in range(SAMPLE_GROUP):
        r0, r1 = n * t, (n + 1) * t
        c = z_ref[r0:r1, TOKEN_WIDTH:2 * TOKEN_WIDTH].astype(F32)
        u = z_ref[r0:r1, 2 * TOKEN_WIDTH:3 * TOKEN_WIDTH].astype(F32)
        cu = c * u
        ext_ref[CARRY_ROWS - 2:CARRY_ROWS, :] = st_ref[n]
        ext_ref[CARRY_ROWS:CARRY_ROWS + t, :] = cu
        conv = (cw_ref[0:1, :] * ext_ref[CARRY_ROWS - 2:CARRY_ROWS - 2 + t, :]
                + cw_ref[1:2, :] * ext_ref[CARRY_ROWS - 1:CARRY_ROWS - 1 + t, :]
                + cw_ref[2:3, :] * cu)
        b = z_ref[r0:r1, 0:TOKEN_WIDTH].astype(F32)
        mixf_ref[r0:r1, 0:TOKEN_WIDTH] = b * conv
        nst_ref[n] = ext_ref[CARRY_ROWS + t - 2:CARRY_ROWS + t, :]
    _sample_cross_attention(z_ref, mk_ref, mv_ref, mixf_ref, 3 * TOKEN_WIDTH, t)
    mix_ref[...] = mixf_ref[...].astype(BF16)


def _sample_cross_attention(z_ref, mk_ref, mv_ref, mixf_ref, qm_off, t):
    def scores(i):
        n, h = divmod(i, MEM_HEADS)
        q = z_ref[n * t:(n + 1) * t, qm_off + h * MEM_HEAD_DIM:qm_off + (h + 1) * MEM_HEAD_DIM]
        return _cross_scores(q, mk_ref[n, pl.ds(h, N_MEM, stride=MEM_HEADS), :].astype(BF16))

    def finish(i, s):
        n, h = divmod(i, MEM_HEADS)
        o = _cross_values(s, mv_ref[n, pl.ds(h, N_MEM, stride=MEM_HEADS), :].astype(BF16))
        mixf_ref[n * t:(n + 1) * t, TOKEN_WIDTH + h * MEM_HEAD_DIM:TOKEN_WIDTH + (h + 1) * MEM_HEAD_DIM] = o

    _run_ahead(SAMPLE_GROUP * MEM_HEADS, SAMPLE_LOOKAHEAD, scores, finish)


def _conv_sample(z, state, mem_k, mem_v, conv_w, w_out, *, layer, dec_seq):
    rows, zc = z.shape
    dec_batch = state.shape[1]
    g = SAMPLE_GROUP
    gr = g * dec_seq
    n_steps = dec_batch // g
    wk, wn = w_out.shape[1:]
    slab = wk // n_steps
    assert slab * n_steps == wk and slab % (2 * V7X_SUBLANES) == 0
    blocks = (_nbytes((gr, zc), BF16) + 2 * _nbytes((g, CONV_WIDTH - 1, TOKEN_WIDTH), F32)
              + 2 * _nbytes((g, N_MEM, V7X_SUBLANES, MEM_HEAD_DIM), F32) + _nbytes((gr, D_MODEL), BF16)
              + _nbytes((slab, wn), F32) + _nbytes((slab, wn), BF16))
    scratch = _nbytes((2 * CARRY_ROWS, TOKEN_WIDTH), F32) + _nbytes((gr, D_MODEL), F32)
    return pl.pallas_call(
        functools.partial(_conv_sample_kernel, dec_seq=dec_seq),
        grid=(n_steps,),
        in_specs=[
            pl.BlockSpec((gr, zc), lambda i: (i, 0)),
            pl.BlockSpec((None, g, CONV_WIDTH - 1, TOKEN_WIDTH), lambda i: (0, i, 0, 0)),
            pl.BlockSpec((None, g, N_MEM * MEM_HEADS, MEM_HEAD_DIM), lambda i: (layer, i, 0, 0)),
            pl.BlockSpec((None, g, N_MEM * MEM_HEADS, MEM_HEAD_DIM), lambda i: (layer, i, 0, 0)),
            pl.BlockSpec((None, CONV_WIDTH, TOKEN_WIDTH), lambda i: (0, 0, 0)),
            pl.BlockSpec((None, slab, wn), lambda i: (0, i, 0)),
        ],
        out_specs=[
            pl.BlockSpec((gr, D_MODEL), lambda i: (i, 0)),
            pl.BlockSpec((g, CONV_WIDTH - 1, TOKEN_WIDTH), lambda i: (i, 0, 0)),
            pl.BlockSpec((slab, wn), lambda i: (i, 0)),
        ],
        out_shape=[
            jax.ShapeDtypeStruct((rows, D_MODEL), BF16),
            jax.ShapeDtypeStruct((dec_batch, CONV_WIDTH - 1, TOKEN_WIDTH), F32),
            jax.ShapeDtypeStruct((wk, wn), BF16),
        ],
        scratch_shapes=[pltpu.VMEM((2 * CARRY_ROWS, TOKEN_WIDTH), F32), pltpu.VMEM((gr, D_MODEL), F32)],
        compiler_params=pltpu.CompilerParams(
            dimension_semantics=("arbitrary",),
            vmem_limit_bytes=_vmem_limit(blocks, scratch)),
        name="conv_sample",
    )(z, state, mem_k, mem_v, conv_w, w_out)


def _band_scores(q, k):
    return _dot_nt(q * (HEAD_DIM ** -0.5), k)


def _band_probs(s, sink, upper, upper_visible):
    s = jnp.where(upper_visible, s[:, :WINDOW], jnp.where(upper, -jnp.inf, s[:, WINDOW:]))
    m = jnp.maximum(jnp.max(s, axis=-1, keepdims=True), sink)
    e = jnp.exp(s - m)
    e = jnp.concatenate([jnp.where(upper, e, 0.0), jnp.where(upper, 0.0, e)], axis=1).astype(BF16)
    return e, jnp.exp(sink - m)


def _band_values(probs, v, ones):
    e, sink_term = probs
    return _dot(e, v) / (_dot(e, ones) + sink_term)


def _swa_prompt_kernel(sink_ref, zq_ref, zp_ref, mk_ref, mv_ref, *refs, blocks_per_seq, cast_steps):
    n_w = (len(refs) - 1) // 2
    w_refs, mix_ref, wb_refs = refs[:n_w], refs[n_w], refs[n_w + 1:]
    s = pl.program_id(0)

    for w_ref, wb_ref, n_cast in zip(w_refs, wb_refs, cast_steps):
        @pl.when(s < n_cast)
        def _():
            wb_ref[...] = w_ref[...].astype(BF16)

    k_off = TOKEN_WIDTH
    v_off = TOKEN_WIDTH + KV_WIDTH
    qm_off = TOKEN_WIDTH + 2 * KV_WIDTH
    n_blocks = zq_ref.shape[0] // WINDOW
    first_has_prev = (s % (blocks_per_seq // n_blocks)) > 0
    row = lax.broadcasted_iota(jnp.int32, (WINDOW, WINDOW), 0)
    col = lax.broadcasted_iota(jnp.int32, (WINDOW, WINDOW), 1)
    upper = col > row
    upper_first = jnp.logical_and(upper, first_has_prev)
    ones = jnp.ones((2 * WINDOW, HEAD_DIM), BF16)

    def block_rows(b):
        return slice(b * WINDOW, (b + 1) * WINDOW)

    @functools.lru_cache(maxsize=None)
    def window(b, kh, col_prev, col_cur):
        lo, hi = kh * HEAD_DIM, (kh + 1) * HEAD_DIM
        prev = (zp_ref[:, col_prev + lo:col_prev + hi] if b == 0
                else zq_ref[block_rows(b - 1), col_cur + lo:col_cur + hi])
        return jnp.concatenate([prev, zq_ref[block_rows(b), col_cur + lo:col_cur + hi]], axis=0)

    def band_scores(i):
        b, h = divmod(i, N_HEADS)
        return _band_scores(zq_ref[block_rows(b), h * HEAD_DIM:(h + 1) * HEAD_DIM], window(b, h // GROUP, 0, k_off))

    def band_finish(i, s):
        b, h = divmod(i, N_HEADS)
        p = _band_probs(s, sink_ref[h], upper, upper_first if b == 0 else upper)
        o = _band_values(p, window(b, h // GROUP, KV_WIDTH, v_off), ones)
        mix_ref[block_rows(b), h * HEAD_DIM:(h + 1) * HEAD_DIM] = o.astype(BF16)

    def cross_scores(h):
        lo, hi = h * MEM_HEAD_DIM, (h + 1) * MEM_HEAD_DIM
        return _cross_scores(zq_ref[:, qm_off + lo:qm_off + hi], mk_ref[:, lo:hi].astype(BF16))

    def cross_finish(h, s):
        lo, hi = h * MEM_HEAD_DIM, (h + 1) * MEM_HEAD_DIM
        mix_ref[:, TOKEN_WIDTH + lo:TOKEN_WIDTH + hi] = _cross_values(s, mv_ref[:, lo:hi].astype(BF16)).astype(BF16)

    _run_ahead(n_blocks * N_HEADS, SCORE_LOOKAHEAD, band_scores, band_finish)
    _run_ahead(MEM_HEADS, SCORE_LOOKAHEAD, cross_scores, cross_finish)


def _swa_prompt(z, mk, mv, sinks, weights, *, layer, batch, seq):
    rows, zc = z.shape
    tq = SWA_ROWS
    blocks_per_seq = seq // WINDOW
    steps_per_seq = seq // tq
    blocks_per_step = tq // WINDOW
    n_steps = batch * steps_per_seq
    kv_col_block = TOKEN_WIDTH // (2 * KV_WIDTH)
    bf16_rows = 2 * V7X_SUBLANES
    blocks = (_nbytes((tq, zc), BF16) + _nbytes((WINDOW, 2 * KV_WIDTH), BF16)
              + 2 * _nbytes((N_MEM, MEM_WIDTH), F32) + _nbytes((tq, D_MODEL), BF16))
    w_in_specs, w_out_specs, w_out_shapes, cast_steps = [], [], [], []
    for w in weights:
        r, c = w.shape[1:]
        n_cast = n_steps if r % (n_steps * bf16_rows) == 0 else n_steps // 2
        slab = r // n_cast
        assert slab * n_cast == r and slab % bf16_rows == 0
        blocks += _nbytes((slab, c), F32) + _nbytes((slab, c), BF16)
        w_in_specs.append(pl.BlockSpec((None, slab, c), lambda s, n=n_cast: (layer, jnp.minimum(s, n - 1), 0)))
        w_out_specs.append(pl.BlockSpec((slab, c), lambda s, n=n_cast: (jnp.minimum(s, n - 1), 0)))
        w_out_shapes.append(jax.ShapeDtypeStruct((r, c), BF16))
        cast_steps.append(n_cast)
    return pl.pallas_call(
        functools.partial(_swa_prompt_kernel, blocks_per_seq=blocks_per_seq, cast_steps=tuple(cast_steps)),
        grid=(n_steps,),
        in_specs=[
            pl.BlockSpec(memory_space=pltpu.SMEM),
            pl.BlockSpec((tq, zc), lambda s: (s, 0)),
            pl.BlockSpec((WINDOW, 2 * KV_WIDTH), lambda s: (jnp.maximum(s * blocks_per_step - 1, 0), kv_col_block)),
            pl.BlockSpec((None, N_MEM, MEM_WIDTH), lambda s: (layer, s // steps_per_seq, 0)),
            pl.BlockSpec((None, N_MEM, MEM_WIDTH), lambda s: (layer, s // steps_per_seq, 0)),
        ] + w_in_specs,
        out_specs=[pl.BlockSpec((tq, D_MODEL), lambda s: (s, 0))] + w_out_specs,
        out_shape=[jax.ShapeDtypeStruct((rows, D_MODEL), BF16)] + w_out_shapes,
        compiler_params=pltpu.CompilerParams(
            dimension_semantics=("arbitrary",),
            vmem_limit_bytes=_vmem_limit(blocks, 0)),
        name="swa_prompt",
    )(sinks, z, z, mk, mv, *weights)


def _swa_sample_kernel(sink_ref, z_ref, ck_ref, cv_ref, mk_ref, mv_ref, wo_ref,
                       mix_ref, nk_ref, nv_ref, wob_ref, knew_ref, vnew_ref, mixf_ref, *, dec_seq):
    t = dec_seq
    wob_ref[...] = wo_ref[...].astype(BF16)
    k_off = TOKEN_WIDTH
    v_off = TOKEN_WIDTH + KV_WIDTH
    qm_off = TOKEN_WIDTH + 2 * KV_WIDTH
    rows = GROUP * t
    qi = lax.broadcasted_iota(jnp.int32, (rows, WINDOW), 0) % t
    col = lax.broadcasted_iota(jnp.int32, (rows, WINDOW), 1)
    upper = col > qi
    knew_ref[...] = jnp.zeros(knew_ref.shape, F32)
    vnew_ref[...] = jnp.zeros(vnew_ref.shape, F32)
    for n in range(SAMPLE_GROUP):
        r0, r1 = n * t, (n + 1) * t
        k_new = z_ref[r0:r1, k_off:k_off + KV_WIDTH].astype(F32)
        v_new = z_ref[r0:r1, v_off:v_off + KV_WIDTH].astype(F32)
        knew_ref[n, 0:t, :] = k_new
        vnew_ref[n, 0:t, :] = v_new
        nk_ref[n, 0:WINDOW - t, :] = ck_ref[n, t:WINDOW, :]
        nv_ref[n, 0:WINDOW - t, :] = cv_ref[n, t:WINDOW, :]
        nk_ref[n, WINDOW - t:WINDOW, :] = k_new
        nv_ref[n, WINDOW - t:WINDOW, :] = v_new

    def scores(i):
        n, kh = divmod(i, N_KV_HEADS)
        lo, hi = kh * HEAD_DIM, (kh + 1) * HEAD_DIM
        k = jnp.concatenate([ck_ref[n, :, lo:hi], knew_ref[n, :, lo:hi]], axis=0).astype(BF16)
        q = jnp.concatenate(
            [z_ref[n * t:(n + 1) * t, (kh * GROUP + g) * HEAD_DIM:(kh * GROUP + g + 1) * HEAD_DIM].astype(F32)
             for g in range(GROUP)], axis=0).astype(BF16)
        return _band_scores(q, k)

    def finish(i, s):
        n, kh = divmod(i, N_KV_HEADS)
        lo, hi = kh * HEAD_DIM, (kh + 1) * HEAD_DIM
        v = jnp.concatenate([cv_ref[n, :, lo:hi], vnew_ref[n, :, lo:hi]], axis=0).astype(BF16)
        sink = jnp.concatenate(
            [jnp.full((t, 1), sink_ref[kh * GROUP + g], F32) for g in range(GROUP)], axis=0)
        o = _band_values(_band_probs(s, sink, upper, upper), v, jnp.ones((2 * WINDOW, HEAD_DIM), BF16))
        for g in range(GROUP):
            h = kh * GROUP + g
            mixf_ref[n * t:(n + 1) * t, h * HEAD_DIM:(h + 1) * HEAD_DIM] = o[g * t:(g + 1) * t, :]

    _run_ahead(SAMPLE_GROUP * N_KV_HEADS, SAMPLE_LOOKAHEAD, scores, finish)
    _sample_cross_attention(z_ref, mk_ref, mv_ref, mixf_ref, qm_off, t)
    mix_ref[...] = mixf_ref[...].astype(BF16)


def _swa_sample(z, cache_k, cache_v, mem_k, mem_v, sinks, w_out, *, layer, dec_seq):
    rows, zc = z.shape
    dec_batch = cache_k.shape[0]
    g = SAMPLE_GROUP
    gr = g * dec_seq
    n_steps = dec_batch // g
    wk, wn = w_out.shape[1:]
    slab = wk // n_steps
    assert slab * n_steps == wk and slab % (2 * V7X_SUBLANES) == 0
    win = jax.ShapeDtypeStruct((dec_batch, WINDOW, KV_WIDTH), F32)
    blocks = (_nbytes((gr, zc), BF16) + 4 * _nbytes((g, WINDOW, KV_WIDTH), F32)
              + 2 * _nbytes((g, N_MEM, V7X_SUBLANES, MEM_HEAD_DIM), F32) + _nbytes((gr, D_MODEL), BF16)
              + _nbytes((slab, wn), F32) + _nbytes((slab, wn), BF16))
    scratch = 2 * _nbytes((g, WINDOW, KV_WIDTH), F32) + _nbytes((gr, D_MODEL), F32)
    return pl.pallas_call(
        functools.partial(_swa_sample_kernel, dec_seq=dec_seq),
        grid=(n_steps,),
        in_specs=[
            pl.BlockSpec(memory_space=pltpu.SMEM),
            pl.BlockSpec((gr, zc), lambda i: (i, 0)),
            pl.BlockSpec((g, WINDOW, KV_WIDTH), lambda i: (i, 0, 0)),
            pl.BlockSpec((g, WINDOW, KV_WIDTH), lambda i: (i, 0, 0)),
            pl.BlockSpec((None, g, N_MEM * MEM_HEADS, MEM_HEAD_DIM), lambda i: (layer, i, 0, 0)),
            pl.BlockSpec((None, g, N_MEM * MEM_HEADS, MEM_HEAD_DIM), lambda i: (layer, i, 0, 0)),
            pl.BlockSpec((None, slab, wn), lambda i: (0, i, 0)),
        ],
        out_specs=[
            pl.BlockSpec((gr, D_MODEL), lambda i: (i, 0)),
            pl.BlockSpec((g, WINDOW, KV_WIDTH), lambda i: (i, 0, 0)),
            pl.BlockSpec((g, WINDOW, KV_WIDTH), lambda i: (i, 0, 0)),
            pl.BlockSpec((slab, wn), lambda i: (i, 0)),
        ],
        out_shape=[jax.ShapeDtypeStruct((rows, D_MODEL), BF16), win, win,
                   jax.ShapeDtypeStruct((wk, wn), BF16)],
        scratch_shapes=[pltpu.VMEM((g, WINDOW, KV_WIDTH), F32), pltpu.VMEM((g, WINDOW, KV_WIDTH), F32),
                        pltpu.VMEM((gr, D_MODEL), F32)],
        compiler_params=pltpu.CompilerParams(
            dimension_semantics=("arbitrary",),
            vmem_limit_bytes=_vmem_limit(blocks, scratch)),
        name="swa_sample",
    )(sinks, z, cache_k, cache_v, mem_k, mem_v, w_out)


def kernel(x_prompt, x_sample, mem_prompt, state_conv, cache_win_k, cache_win_v, cache_mem_k, cache_mem_v,
           norm_mix, norm_mem, w_mem_kv, norm_ffn, w_gate, w_up, w_down,
           conv_w_in, conv_w, conv_w_out, attn_w_in, attn_sinks, attn_w_out, norm_final):
    batch, seq, d = x_prompt.shape
    dec_batch, dec_seq, _ = x_sample.shape
    depth = norm_mix.shape[0]
    d_ff = w_gate.shape[2]
    prompt_rows = batch * seq
    sample_rows = dec_batch * dec_seq
    assert d == D_MODEL and depth == 2 and seq % CONV_ROWS == 0 and seq % SWA_ROWS == 0 and SWA_ROWS % WINDOW == 0
    assert prompt_rows % PROMPT_TILE == 0 and dec_batch % SAMPLE_GROUP == 0
    assert dec_seq == V7X_SUBLANES and d_ff % FFN_COL_TILE == 0 and d % COL_TILE == 0
    assert d_ff % DOWN_K_TILE == 0 and d % DOWN_X_CHUNK == 0
    assert prompt_rows % FFN_ROW_TILE == 0 and prompt_rows % sample_rows == 0

    xp = x_prompt.reshape(prompt_rows, d)
    xs = x_sample.reshape(sample_rows, d)
    mem = mem_prompt.reshape(batch * N_MEM, d)
    mem_k_s = cache_mem_k.reshape(depth, dec_batch, N_MEM * MEM_HEADS, MEM_HEAD_DIM)
    mem_v_s = cache_mem_v.reshape(depth, dec_batch, N_MEM * MEM_HEADS, MEM_HEAD_DIM)
    g_mix = norm_mix.reshape(depth, 1, d)
    g_ffn = norm_ffn.reshape(depth, 1, d)

    mk, mv = _mem_kv(mem, norm_mem.reshape(depth, 1, d), w_mem_kv)

    hp, hs = _norm_rows(xp, xs, g_mix, layer=0)
    zp, zs = _matmul(hp, hs, conv_w_in, w_layer=0)
    mix_s, conv_s, wo_bf = _conv_sample(zs, state_conv, mem_k_s, mem_v_s, conv_w, conv_w_out,
                                        layer=0, dec_seq=dec_seq)
    mix_p, conv_p, wd_bf = _conv_prompt(zp, mk, mv, conv_w, w_down, layer=0, batch=batch, seq=seq)
    xp, xs, hp, hs = _out_proj(mix_p, mix_s, wo_bf, xp, xs, g_ffn, layer=0)
    ap, as_ = _ffn_up(hp, hs, w_gate, w_up, layer=0)
    xp, xs, hp, hs = _ffn_down(ap, as_, wd_bf[None], xp, xs, g_mix, w_layer=0, g_layer=1, final=False)

    zp, zs = _matmul(hp, hs, attn_w_in, w_layer=0)
    sinks = attn_sinks[0]
    mix_s, win_k_s, win_v_s, wo_bf = _swa_sample(
        zs, cache_win_k[0].reshape(dec_batch, WINDOW, KV_WIDTH), cache_win_v[0].reshape(dec_batch, WINDOW, KV_WIDTH),
        mem_k_s, mem_v_s, sinks, attn_w_out, layer=1, dec_seq=dec_seq)
    mix_p, wd_bf = _swa_prompt(zp, mk, mv, sinks, [w_down], layer=1, batch=batch, seq=seq)
    xp, xs, hp, hs = _out_proj(mix_p, mix_s, wo_bf, xp, xs, g_ffn, layer=1)
    ap, as_ = _ffn_up(hp, hs, w_gate, w_up, layer=1)
    y_prompt, y_sample = _ffn_down(ap, as_, wd_bf[None], xp, xs, norm_final.reshape(1, 1, d),
                                   w_layer=0, g_layer=0, final=True)

    win_p = zp.reshape(batch, seq, -1)[:, seq - WINDOW:, TOKEN_WIDTH:TOKEN_WIDTH + 2 * KV_WIDTH].astype(F32)
    kv_shape = (1, -1, WINDOW, N_KV_HEADS, HEAD_DIM)
    mem_shape = (depth, batch, N_MEM, MEM_HEADS, MEM_HEAD_DIM)
    return (y_prompt.reshape(batch, seq, d),
            y_sample.reshape(dec_batch, dec_seq, d),
            conv_p[None],
            conv_s[None],
            win_p[..., :KV_WIDTH].reshape(kv_shape),
            win_p[..., KV_WIDTH:].reshape(kv_shape),
            win_k_s.reshape(kv_shape),
            win_v_s.reshape(kv_shape),
            mk.reshape(mem_shape),
            mv.reshape(mem_shape))
```

```python
import functools

import jax
import jax.numpy as jnp
from jax import lax
from jax.experimental import pallas as pl
from jax.experimental.pallas import tpu as pltpu

F32 = jnp.float32
BF16 = jnp.bfloat16

D_MODEL = 2048
N_MEM = 256
MEM_HEADS = 4
MEM_WIDTH = D_MODEL // 4
MEM_HEAD_DIM = MEM_WIDTH // MEM_HEADS
TOKEN_WIDTH = D_MODEL - MEM_WIDTH
CONV_WIDTH = 3
WINDOW = 128
HEAD_DIM = 64
N_HEADS = TOKEN_WIDTH // HEAD_DIM
N_KV_HEADS = 4
GROUP = N_HEADS // N_KV_HEADS
KV_WIDTH = N_KV_HEADS * HEAD_DIM
EPS = 1e-6

V7X_VMEM_BYTES = 64 * 1024 * 1024
V7X_SUBLANES = 8
V7X_MXU_DEPTH = 256

PROMPT_TILE = 1024
COL_TILE = 512
DOWN_K_TILE = 1408
DOWN_X_CHUNK = 512
FFN_ROW_TILE = 2048
FFN_COL_TILE = 512
MXU_ROWS = 1024
OUT_ROWS = 512
CONV_ROWS = 512
SWA_ROWS = 512
SCORE_LOOKAHEAD = 5
SAMPLE_LOOKAHEAD = 4
SAMPLE_GROUP = 8
CARRY_ROWS = V7X_SUBLANES
MIN_TEMP_BYTES = 8 << 20
VMEM_RESERVE_BYTES = 6 << 20


def _nbytes(shape, dtype):
    n = 1
    for s in shape:
        n *= s
    return n * jnp.dtype(dtype).itemsize


def _vmem_limit(block_bytes, scratch_bytes):
    need = 2 * block_bytes + scratch_bytes
    temporaries = max(need // 4, MIN_TEMP_BYTES)
    return int(min(need + temporaries, V7X_VMEM_BYTES - VMEM_RESERVE_BYTES))


def _rmsnorm(x, g):
    r = lax.rsqrt(jnp.mean(x * x, axis=-1, keepdims=True) + EPS)
    return (x * r) * g


def _dot(a, b):
    return jnp.dot(a, b, preferred_element_type=F32)


def _dot_nt(a, b):
    return lax.dot_general(a, b, (((1,), (1,)), ((), ())), preferred_element_type=F32)


def _prompt_rows_map(i, j):
    return (i, 0)


def _prompt_tile_map(i, j):
    return (i, j)


def _sample_tile_map(n_pt):
    return lambda i, j: (0, jnp.where(i == n_pt - 1, j, 0))


def _const_map(i, j):
    return (0, 0)


def _norm_rows_kernel(xp_ref, xs_ref, g_ref, hp_ref, hs_ref, *, n_pt):
    hp_ref[...] = _rmsnorm(xp_ref[...], g_ref[...]).astype(BF16)

    @pl.when(pl.program_id(0) == n_pt - 1)
    def _():
        hs_ref[...] = _rmsnorm(xs_ref[...], g_ref[...]).astype(BF16)


def _norm_rows(xp, xs, g, *, layer):
    rp, d = xp.shape
    rs = xs.shape[0]
    tm = OUT_ROWS
    n_pt = rp // tm
    blocks = _nbytes((tm, d), F32) + _nbytes((tm, d), BF16) + _nbytes((rs, d), F32) + _nbytes((rs, d), BF16)
    return pl.pallas_call(
        functools.partial(_norm_rows_kernel, n_pt=n_pt),
        grid=(n_pt,),
        in_specs=[
            pl.BlockSpec((tm, d), lambda s: (s, 0)),
            pl.BlockSpec((rs, d), lambda s: (0, 0)),
            pl.BlockSpec((None, 1, d), lambda s: (layer, 0, 0)),
        ],
        out_specs=[
            pl.BlockSpec((tm, d), lambda s: (s, 0)),
            pl.BlockSpec((rs, d), lambda s: (0, 0)),
        ],
        out_shape=[jax.ShapeDtypeStruct((rp, d), BF16), jax.ShapeDtypeStruct((rs, d), BF16)],
        compiler_params=pltpu.CompilerParams(
            dimension_semantics=("arbitrary",),
            vmem_limit_bytes=_vmem_limit(blocks, 0)),
        name="norm_rows",
    )(xp, xs, g)


def _swiglu(h, wg, wu):
    gate = _dot(h, wg)
    up = _dot(h, wu)
    return (gate * jax.nn.sigmoid(gate) * up).astype(BF16)


def _ffn_up_kernel(hp_ref, hs_ref, wg_ref, wu_ref, ap_ref, as_ref, *, n_pt):
    wg, wu = wg_ref[...].astype(BF16), wu_ref[...].astype(BF16)
    for r in range(0, hp_ref.shape[0], MXU_ROWS):
        ap_ref[r:r + MXU_ROWS, :] = _swiglu(hp_ref[r:r + MXU_ROWS, :], wg, wu)

    @pl.when(pl.program_id(0) == n_pt - 1)
    def _():
        as_ref[...] = _swiglu(hs_ref[...], wg_ref[...].astype(BF16), wu_ref[...].astype(BF16))


def _ffn_up(hp, hs, wg, wu, *, layer):
    rp, d = hp.shape
    rs = hs.shape[0]
    n = wg.shape[2]
    tm, tn = FFN_ROW_TILE, FFN_COL_TILE
    n_pt, n_j = rp // tm, n // tn
    blocks = (_nbytes((tm, d), BF16) + _nbytes((rs, d), BF16) + 2 * _nbytes((d, tn), wg.dtype)
              + _nbytes((tm, tn), BF16) + _nbytes((rs, tn), BF16))
    temps = 2 * _nbytes((d, tn), BF16) + 3 * _nbytes((tm, tn), F32)
    w_spec = pl.BlockSpec((None, d, tn), lambda i, j: (layer, 0, j))
    return pl.pallas_call(
        functools.partial(_ffn_up_kernel, n_pt=n_pt),
        grid=(n_pt, n_j),
        in_specs=[
            pl.BlockSpec((tm, d), _prompt_rows_map),
            pl.BlockSpec((rs, d), _const_map),
            w_spec,
            w_spec,
        ],
        out_specs=[
            pl.BlockSpec((tm, tn), _prompt_tile_map),
            pl.BlockSpec((rs, tn), _sample_tile_map(n_pt)),
        ],
        out_shape=[jax.ShapeDtypeStruct((rp, n), BF16), jax.ShapeDtypeStruct((rs, n), BF16)],
        compiler_params=pltpu.CompilerParams(
            dimension_semantics=("arbitrary", "arbitrary"),
            vmem_limit_bytes=_vmem_limit(blocks, temps)),
        name="ffn_up",
    )(hp, hs, wg, wu)


def _out_proj_kernel(ap_ref, as_ref, w_ref, xp_ref, xs_ref, g_ref, op_ref, os_ref, hp_ref, hs_ref, *, n_pt):
    x = xp_ref[...] + _dot(ap_ref[...], w_ref[...])
    op_ref[...] = x
    hp_ref[...] = _rmsnorm(x, g_ref[...]).astype(BF16)

    @pl.when(pl.program_id(0) == n_pt - 1)
    def _():
        x = xs_ref[...] + _dot(as_ref[...], w_ref[...])
        os_ref[...] = x
        hs_ref[...] = _rmsnorm(x, g_ref[...]).astype(BF16)


def _out_proj(ap, as_, w, xp, xs, g, *, layer):
    rp, k = ap.shape
    rs = as_.shape[0]
    n = w.shape[1]
    tm = OUT_ROWS
    n_pt = rp // tm
    rows_map = lambda s: (s, 0)
    const = lambda s: (0, 0)
    once = dict(pipeline_mode=pl.Buffered(1))
    blocks = _nbytes((tm, k), BF16) + 2 * _nbytes((tm, n), F32) + _nbytes((tm, n), BF16)
    resident = (_nbytes((k, n), BF16) + _nbytes((rs, k), BF16) + _nbytes((rs, n), F32)
                + 2 * (_nbytes((rs, n), F32) + _nbytes((rs, n), BF16)))
    return pl.pallas_call(
        functools.partial(_out_proj_kernel, n_pt=n_pt),
        grid=(n_pt,),
        in_specs=[
            pl.BlockSpec((tm, k), rows_map),
            pl.BlockSpec((rs, k), const, **once),
            pl.BlockSpec((k, n), const, **once),
            pl.BlockSpec((tm, n), rows_map),
            pl.BlockSpec((rs, n), const, **once),
            pl.BlockSpec((None, 1, n), lambda s: (layer, 0, 0)),
        ],
        out_specs=[
            pl.BlockSpec((tm, n), rows_map),
            pl.BlockSpec((rs, n), const),
            pl.BlockSpec((tm, n), rows_map),
            pl.BlockSpec((rs, n), const),
        ],
        out_shape=[jax.ShapeDtypeStruct((rp, n), F32), jax.ShapeDtypeStruct((rs, n), F32),
                   jax.ShapeDtypeStruct((rp, n), BF16), jax.ShapeDtypeStruct((rs, n), BF16)],
        compiler_params=pltpu.CompilerParams(
            dimension_semantics=("arbitrary",),
            vmem_limit_bytes=_vmem_limit(blocks, resident + 2 * _nbytes((tm, n), F32))),
        name="out_proj",
    )(ap, as_, w, xp, xs, g)


def _matmul_kernel(hp_ref, hs_ref, w_ref, zp_ref, zs_ref, *, n_pt):
    w = w_ref[...].astype(BF16)
    for r in range(0, hp_ref.shape[0], MXU_ROWS):
        zp_ref[r:r + MXU_ROWS, :] = _dot(hp_ref[r:r + MXU_ROWS, :], w).astype(zp_ref.dtype)

    @pl.when(pl.program_id(0) == n_pt - 1)
    def _():
        zs_ref[...] = _dot(hs_ref[...], w_ref[...].astype(BF16)).astype(zs_ref.dtype)


def _matmul(hp, hs, w, *, w_layer):
    rp, d = hp.shape
    rs = hs.shape[0]
    n = w.shape[2]
    tm, tn = FFN_ROW_TILE, COL_TILE
    n_pt, n_j = rp // tm, n // tn
    blocks = (_nbytes((tm, d), BF16) + _nbytes((rs, d), BF16) + _nbytes((d, tn), F32)
              + _nbytes((tm, tn), BF16) + _nbytes((rs, tn), BF16))
    temps = _nbytes((d, tn), BF16) + _nbytes((tm, tn), F32)
    return pl.pallas_call(
        functools.partial(_matmul_kernel, n_pt=n_pt),
        grid=(n_pt, n_j),
        in_specs=[
            pl.BlockSpec((tm, d), _prompt_rows_map),
            pl.BlockSpec((rs, d), _const_map),
            pl.BlockSpec((None, d, tn), lambda i, j: (w_layer, 0, j)),
        ],
        out_specs=[
            pl.BlockSpec((tm, tn), _prompt_tile_map),
            pl.BlockSpec((rs, tn), _sample_tile_map(n_pt)),
        ],
        out_shape=[jax.ShapeDtypeStruct((rp, n), BF16), jax.ShapeDtypeStruct((rs, n), BF16)],
        compiler_params=pltpu.CompilerParams(
            dimension_semantics=("arbitrary", "arbitrary"),
            vmem_limit_bytes=_vmem_limit(blocks, temps)),
        name="matmul",
    )(hp, hs, w)


def _ffn_down_kernel(ap_ref, as_ref, w_ref, xp_ref, xs_ref, g_ref, *refs, n_pt, n_k, n_xc, final):
    i, k = pl.program_id(0), pl.program_id(1)
    if final:
        op_ref, os_ref, apl_ref, asl_ref, wl_ref = refs
    else:
        op_ref, os_ref, hp_ref, hs_ref, apl_ref, asl_ref, wl_ref = refs
    xc = xp_ref.shape[1]
    kf = (ap_ref.shape[1] // V7X_MXU_DEPTH) * V7X_MXU_DEPTH
    even = k % 2 == 0

    def whole_passes(a_ref):
        return _dot(a_ref[:, :kf], w_ref[:kf, :])

    def with_stash(a_ref, al_ref):
        return _dot(jnp.concatenate([al_ref[...], a_ref[...]], axis=1),
                    jnp.concatenate([wl_ref[...], w_ref[...]], axis=0))

    for kk in range(n_k):
        @pl.when(k == kk)
        def _():
            cols = slice(kk * xc, (kk + 1) * xc)
            if 0 < kk < n_xc:
                op_ref[:, cols] = op_ref[:, cols] + xp_ref[...]
            contrib = whole_passes(ap_ref) if kk % 2 == 0 else with_stash(ap_ref, apl_ref)
            if kk == 0:
                op_ref[...] = contrib
                op_ref[:, cols] = op_ref[:, cols] + xp_ref[...]
            else:
                op_ref[...] = op_ref[...] + contrib
            if kk == n_k - 1:
                if final:
                    op_ref[...] = _rmsnorm(op_ref[...], g_ref[...])
                else:
                    hp_ref[...] = _rmsnorm(op_ref[...], g_ref[...]).astype(BF16)

    @pl.when(i == n_pt - 1)
    def _():
        @pl.when(k == 0)
        def _():
            os_ref[...] = xs_ref[...] + whole_passes(as_ref)

        @pl.when(jnp.logical_and(k > 0, even))
        def _():
            os_ref[...] = os_ref[...] + whole_passes(as_ref)

        @pl.when(jnp.logical_not(even))
        def _():
            os_ref[...] = os_ref[...] + with_stash(as_ref, asl_ref)

        @pl.when(even)
        def _():
            asl_ref[...] = as_ref[:, kf:]

        @pl.when(k == n_k - 1)
        def _():
            if final:
                os_ref[...] = _rmsnorm(os_ref[...], g_ref[...])
            else:
                hs_ref[...] = _rmsnorm(os_ref[...], g_ref[...]).astype(BF16)

    @pl.when(even)
    def _():
        apl_ref[...] = ap_ref[:, kf:]
        wl_ref[...] = w_ref[kf:, :]


def _ffn_down(ap, as_, w, xp, xs, g, *, w_layer, g_layer, final):
    rp, kdim = ap.shape
    rs = as_.shape[0]
    n = w.shape[2]
    tm, tk, xc = PROMPT_TILE, DOWN_K_TILE, DOWN_X_CHUNK
    n_pt, n_k, n_xc = rp // tm, kdim // tk, n // xc
    k_left = tk % V7X_MXU_DEPTH
    assert n_xc <= n_k and n_k % 2 == 0 and 2 * k_left == V7X_MXU_DEPTH and w.dtype == BF16
    rows = lambda i, k: (i, 0)
    blocks = (_nbytes((tm, tk), BF16) + _nbytes((rs, tk), BF16) + _nbytes((tk, n), w.dtype)
              + _nbytes((tm, xc), F32) + _nbytes((rs, n), F32)
              + _nbytes((tm, n), F32) + _nbytes((rs, n), F32))
    out_specs = [pl.BlockSpec((tm, n), rows), pl.BlockSpec((rs, n), _const_map)]
    out_shape = [jax.ShapeDtypeStruct((rp, n), F32), jax.ShapeDtypeStruct((rs, n), F32)]
    if not final:
        blocks += _nbytes((tm, n), BF16) + _nbytes((rs, n), BF16)
        out_specs += [pl.BlockSpec((tm, n), rows), pl.BlockSpec((rs, n), _const_map)]
        out_shape += [jax.ShapeDtypeStruct((rp, n), BF16), jax.ShapeDtypeStruct((rs, n), BF16)]
    return pl.pallas_call(
        functools.partial(_ffn_down_kernel, n_pt=n_pt, n_k=n_k, n_xc=n_xc, final=final),
        grid=(n_pt, n_k),
        in_specs=[
            pl.BlockSpec((tm, tk), lambda i, k: (i, k)),
            pl.BlockSpec((rs, tk), lambda i, k: (0, jnp.where(i == n_pt - 1, k, 0))),
            pl.BlockSpec((None, tk, n), lambda i, k: (w_layer, k, 0)),
            pl.BlockSpec((tm, xc), lambda i, k: (i, jnp.minimum(k, n_xc - 1))),
            pl.BlockSpec((rs, n), _const_map),
            pl.BlockSpec((None, 1, n), lambda i, k: (g_layer, 0, 0)),
        ],
        out_specs=out_specs,
        out_shape=out_shape,
        scratch_shapes=[pltpu.VMEM((tm, k_left), BF16), pltpu.VMEM((rs, k_left), BF16),
                        pltpu.VMEM((k_left, n), BF16)],
        compiler_params=pltpu.CompilerParams(
            dimension_semantics=("arbitrary", "arbitrary"),
            vmem_limit_bytes=_vmem_limit(blocks, _nbytes((tk, n), BF16))),
        name="ffn_down",
    )(ap, as_, w, xp, xs, g)


def _mem_kv_kernel(x_ref, g_ref, w_ref, k_ref, v_ref):
    h = _rmsnorm(x_ref[...], g_ref[...]).astype(BF16)
    kv = _dot(h, w_ref[...].astype(BF16))
    tm = x_ref.shape[0]
    for hd in range(MEM_HEADS):
        head_rows = pl.ds(hd, tm, stride=MEM_HEADS)
        k_ref[head_rows, :] = kv[:, hd * MEM_HEAD_DIM:(hd + 1) * MEM_HEAD_DIM]
        v_ref[head_rows, :] = kv[:, MEM_WIDTH + hd * MEM_HEAD_DIM:MEM_WIDTH + (hd + 1) * MEM_HEAD_DIM]


def _mem_kv(mem, g, w):
    rows, d = mem.shape
    depth = w.shape[0]
    tm = 512
    out = jax.ShapeDtypeStruct((depth, rows * MEM_HEADS, MEM_HEAD_DIM), F32)
    blocks = (_nbytes((tm, d), F32) + _nbytes((d, 2 * MEM_WIDTH), F32) + 2 * _nbytes((tm, MEM_WIDTH), F32))
    return pl.pallas_call(
        _mem_kv_kernel,
        grid=(depth, rows // tm),
        in_specs=[
            pl.BlockSpec((tm, d), lambda l, i: (i, 0)),
            pl.BlockSpec((None, 1, d), lambda l, i: (l, 0, 0)),
            pl.BlockSpec((None, d, 2 * MEM_WIDTH), lambda l, i: (l, 0, 0)),
        ],
        out_specs=[
            pl.BlockSpec((None, tm * MEM_HEADS, MEM_HEAD_DIM), lambda l, i: (l, i, 0)),
            pl.BlockSpec((None, tm * MEM_HEADS, MEM_HEAD_DIM), lambda l, i: (l, i, 0)),
        ],
        out_shape=[out, out],
        compiler_params=pltpu.CompilerParams(
            dimension_semantics=("arbitrary", "arbitrary"),
            vmem_limit_bytes=_vmem_limit(blocks, _nbytes((d, 2 * MEM_WIDTH), BF16))),
        name="mem_kv",
    )(mem, g, w)


def _cross_scores(q, k):
    return _dot_nt(q, k) * (MEM_HEAD_DIM ** -0.5)


def _cross_values(s, v):
    e = jnp.exp(s - jnp.max(s, axis=-1, keepdims=True)).astype(BF16)
    return _dot(e, v) / _dot(e, jnp.ones(v.shape, BF16))


def _run_ahead(n_items, lookahead, first, second):
    pending = {}
    for i in range(n_items + lookahead):
        if i < n_items:
            pending[i] = first(i)
        if i >= lookahead:
            second(i - lookahead, pending.pop(i - lookahead))


def _conv_prompt_kernel(z_ref, mk_ref, mv_ref, cw_ref, wd_ref, mix_ref, st_ref, wdb_ref, ext_ref, *, tiles_per_seq):
    s = pl.program_id(0)
    tq = z_ref.shape[0]
    wdb_ref[...] = wd_ref[...].astype(BF16)

    @pl.when(s % tiles_per_seq == 0)
    def _():
        ext_ref[0:CARRY_ROWS, :] = jnp.zeros((CARRY_ROWS, TOKEN_WIDTH), F32)

    c = z_ref[:, TOKEN_WIDTH:2 * TOKEN_WIDTH].astype(F32)
    u = z_ref[:, 2 * TOKEN_WIDTH:3 * TOKEN_WIDTH].astype(F32)
    cu = c * u
    ext_ref[CARRY_ROWS:CARRY_ROWS + tq, :] = cu
    conv = (cw_ref[0:1, :] * ext_ref[CARRY_ROWS - 2:CARRY_ROWS - 2 + tq, :]
            + cw_ref[1:2, :] * ext_ref[CARRY_ROWS - 1:CARRY_ROWS - 1 + tq, :]
            + cw_ref[2:3, :] * cu)
    b = z_ref[:, 0:TOKEN_WIDTH].astype(F32)
    mix_ref[:, 0:TOKEN_WIDTH] = (b * conv).astype(BF16)
    st_ref[...] = ext_ref[CARRY_ROWS + tq - 2:CARRY_ROWS + tq, :]
    ext_ref[0:CARRY_ROWS, :] = ext_ref[tq:tq + CARRY_ROWS, :]

    def scores(h):
        lo, hi = h * MEM_HEAD_DIM, (h + 1) * MEM_HEAD_DIM
        return _cross_scores(z_ref[:, 3 * TOKEN_WIDTH + lo:3 * TOKEN_WIDTH + hi], mk_ref[pl.ds(h, N_MEM, stride=MEM_HEADS), :].astype(BF16))

    def finish(h, s):
        lo, hi = h * MEM_HEAD_DIM, (h + 1) * MEM_HEAD_DIM
        mix_ref[:, TOKEN_WIDTH + lo:TOKEN_WIDTH + hi] = _cross_values(
            s, mv_ref[pl.ds(h, N_MEM, stride=MEM_HEADS), :].astype(BF16)).astype(BF16)

    _run_ahead(MEM_HEADS, SCORE_LOOKAHEAD, scores, finish)


def _conv_prompt(z, mk, mv, conv_w, w_down, *, layer, batch, seq):
    rows, zc = z.shape
    tq = CONV_ROWS
    tiles_per_seq = seq // tq
    n_steps = batch * tiles_per_seq
    d_ff, d = w_down.shape[1:]
    slab = d_ff // n_steps
    assert slab * n_steps == d_ff and slab % (2 * V7X_SUBLANES) == 0
    blocks = (_nbytes((tq, zc), BF16) + 2 * _nbytes((N_MEM, MEM_WIDTH), F32) + _nbytes((tq, D_MODEL), BF16)
              + _nbytes((slab, d), F32) + _nbytes((slab, d), BF16))
    scratch = _nbytes((tq + CARRY_ROWS, TOKEN_WIDTH), F32)
    return pl.pallas_call(
        functools.partial(_conv_prompt_kernel, tiles_per_seq=tiles_per_seq),
        grid=(n_steps,),
        in_specs=[
            pl.BlockSpec((tq, zc), lambda s: (s, 0)),
            pl.BlockSpec((None, N_MEM * MEM_HEADS, MEM_HEAD_DIM), lambda s: (layer, s // tiles_per_seq, 0)),
            pl.BlockSpec((None, N_MEM * MEM_HEADS, MEM_HEAD_DIM), lambda s: (layer, s // tiles_per_seq, 0)),
            pl.BlockSpec((None, CONV_WIDTH, TOKEN_WIDTH), lambda s: (0, 0, 0)),
            pl.BlockSpec((None, slab, d), lambda s: (layer, s, 0)),
        ],
        out_specs=[
            pl.BlockSpec((tq, D_MODEL), lambda s: (s, 0)),
            pl.BlockSpec((None, CONV_WIDTH - 1, TOKEN_WIDTH), lambda s: (s // tiles_per_seq, 0, 0)),
            pl.BlockSpec((slab, d), lambda s: (s, 0)),
        ],
        out_shape=[
            jax.ShapeDtypeStruct((rows, D_MODEL), BF16),
            jax.ShapeDtypeStruct((batch, CONV_WIDTH - 1, TOKEN_WIDTH), F32),
            jax.ShapeDtypeStruct((d_ff, d), BF16),
        ],
        scratch_shapes=[pltpu.VMEM((tq + CARRY_ROWS, TOKEN_WIDTH), F32)],
        compiler_params=pltpu.CompilerParams(
            dimension_semantics=("arbitrary",),
            vmem_limit_bytes=_vmem_limit(blocks, scratch + 6 * _nbytes((tq, TOKEN_WIDTH), F32))),
        name="conv_prompt",
    )(z, mk, mv, conv_w, w_down)


def _conv_sample_kernel(z_ref, st_ref, mk_ref, mv_ref, cw_ref, wo_ref, mix_ref, nst_ref, wob_ref,
                        ext_ref, mixf_ref, *, dec_seq):
    t = dec_seq
    wob_ref[...] = wo_ref[...].astype(BF16)
    for n in range(SAMPLE_GROUP):
        r0, r1 = n * t, (n + 1) * t
        c = z_ref[r0:r1, TOKEN_WIDTH:2 * TOKEN_WIDTH].astype(F32)
        u = z_ref[r0:r1, 2 * TOKEN_WIDTH:3 * TOKEN_WIDTH].astype(F32)
        cu = c * u
        ext_ref[CARRY_ROWS - 2:CARRY_ROWS, :] = st_ref[n]
        ext_ref[CARRY_ROWS:CARRY_ROWS + t, :] = cu
        conv = (cw_ref[0:1, :] * ext_ref[CARRY_ROWS - 2:CARRY_ROWS - 2 + t, :]
                + cw_ref[1:2, :] * ext_ref[CARRY_ROWS - 1:CARRY_ROWS - 1 + t, :]
                + cw_ref[2:3, :] * cu)
        b = z_ref[r0:r1, 0:TOKEN_WIDTH].astype(F32)
        mixf_ref[r0:r1, 0:TOKEN_WIDTH] = b * conv
        nst_ref[n] = ext_ref[CARRY_ROWS + t - 2:CARRY_ROWS + t, :]
    _sample_cross_attention(z_ref, mk_ref, mv_ref, mixf_ref, 3 * TOKEN_WIDTH, t)
    mix_ref[...] = mixf_ref[...].astype(BF16)


def _sample_cross_attention(z_ref, mk_ref, mv_ref, mixf_ref, qm_off, t):
    def scores(i):
        n, h = divmod(i, MEM_HEADS)
        q = z_ref[n * t:(n + 1) * t, qm_off + h * MEM_HEAD_DIM:qm_off + (h + 1) * MEM_HEAD_DIM]
        return _cross_scores(q, mk_ref[n, pl.ds(h, N_MEM, stride=MEM_HEADS), :].astype(BF16))

    def finish(i, s):
        n, h = divmod(i, MEM_HEADS)
        o = _cross_values(s, mv_ref[n, pl.ds(h, N_MEM, stride=MEM_HEADS), :].astype(BF16))
        mixf_ref[n * t:(n + 1) * t, TOKEN_WIDTH + h * MEM_HEAD_DIM:TOKEN_WIDTH + (h + 1) * MEM_HEAD_DIM] = o

    _run_ahead(SAMPLE_GROUP * MEM_HEADS, SAMPLE_LOOKAHEAD, scores, finish)


def _conv_sample(z, state, mem_k, mem_v, conv_w, w_out, *, layer, dec_seq):
    rows, zc = z.shape
    dec_batch = state.shape[1]
    g = SAMPLE_GROUP
    gr = g * dec_seq
    n_steps = dec_batch // g
    wk, wn = w_out.shape[1:]
    slab = wk // n_steps
    assert slab * n_steps == wk and slab % (2 * V7X_SUBLANES) == 0
    blocks = (_nbytes((gr, zc), BF16) + 2 * _nbytes((g, CONV_WIDTH - 1, TOKEN_WIDTH), F32)
              + 2 * _nbytes((g, N_MEM, V7X_SUBLANES, MEM_HEAD_DIM), F32) + _nbytes((gr, D_MODEL), BF16)
              + _nbytes((slab, wn), F32) + _nbytes((slab, wn), BF16))
    scratch = _nbytes((2 * CARRY_ROWS, TOKEN_WIDTH), F32) + _nbytes((gr, D_MODEL), F32)
    return pl.pallas_call(
        functools.partial(_conv_sample_kernel, dec_seq=dec_seq),
        grid=(n_steps,),
        in_specs=[
            pl.BlockSpec((gr, zc), lambda i: (i, 0)),
            pl.BlockSpec((None, g, CONV_WIDTH - 1, TOKEN_WIDTH), lambda i: (0, i, 0, 0)),
            pl.BlockSpec((None, g, N_MEM * MEM_HEADS, MEM_HEAD_DIM), lambda i: (layer, i, 0, 0)),
            pl.BlockSpec((None, g, N_MEM * MEM_HEADS, MEM_HEAD_DIM), lambda i: (layer, i, 0, 0)),
            pl.BlockSpec((None, CONV_WIDTH, TOKEN_WIDTH), lambda i: (0, 0, 0)),
            pl.BlockSpec((None, slab, wn), lambda i: (0, i, 0)),
        ],
        out_specs=[
            pl.BlockSpec((gr, D_MODEL), lambda i: (i, 0)),
            pl.BlockSpec((g, CONV_WIDTH - 1, TOKEN_WIDTH), lambda i: (i, 0, 0)),
            pl.BlockSpec((slab, wn), lambda i: (i, 0)),
        ],
        out_shape=[
            jax.ShapeDtypeStruct((rows, D_MODEL), BF16),
            jax.ShapeDtypeStruct((dec_batch, CONV_WIDTH - 1, TOKEN_WIDTH), F32),
            jax.ShapeDtypeStruct((wk, wn), BF16),
        ],
        scratch_shapes=[pltpu.VMEM((2 * CARRY_ROWS, TOKEN_WIDTH), F32), pltpu.VMEM((gr, D_MODEL), F32)],
        compiler_params=pltpu.CompilerParams(
            dimension_semantics=("arbitrary",),
            vmem_limit_bytes=_vmem_limit(blocks, scratch)),
        name="conv_sample",
    )(z, state, mem_k, mem_v, conv_w, w_out)


def _band_scores(q, k):
    return _dot_nt(q * (HEAD_DIM ** -0.5), k)


def _band_probs(s, sink, upper, upper_visible):
    s = jnp.where(upper_visible, s[:, :WINDOW], jnp.where(upper, -jnp.inf, s[:, WINDOW:]))
    m = jnp.maximum(jnp.max(s, axis=-1, keepdims=True), sink)
    e = jnp.exp(s - m)
    e = jnp.concatenate([jnp.where(upper, e, 0.0), jnp.where(upper, 0.0, e)], axis=1).astype(BF16)
    return e, jnp.exp(sink - m)


def _band_values(probs, v, ones):
    e, sink_term = probs
    return _dot(e, v) / (_dot(e, ones) + sink_term)


def _swa_prompt_kernel(sink_ref, zq_ref, zp_ref, mk_ref, mv_ref, *refs, blocks_per_seq, cast_steps):
    n_w = (len(refs) - 1) // 2
    w_refs, mix_ref, wb_refs = refs[:n_w], refs[n_w], refs[n_w + 1:]
    s = pl.program_id(0)

    for w_ref, wb_ref, n_cast in zip(w_refs, wb_refs, cast_steps):
        @pl.when(s < n_cast)
        def _():
            wb_ref[...] = w_ref[...].astype(BF16)

    k_off = TOKEN_WIDTH
    v_off = TOKEN_WIDTH + KV_WIDTH
    qm_off = TOKEN_WIDTH + 2 * KV_WIDTH
    n_blocks = zq_ref.shape[0] // WINDOW
    first_has_prev = (s % (blocks_per_seq // n_blocks)) > 0
    row = lax.broadcasted_iota(jnp.int32, (WINDOW, WINDOW), 0)
    col = lax.broadcasted_iota(jnp.int32, (WINDOW, WINDOW), 1)
    upper = col > row
    upper_first = jnp.logical_and(upper, first_has_prev)
    ones = jnp.ones((2 * WINDOW, HEAD_DIM), BF16)

    def block_rows(b):
        return slice(b * WINDOW, (b + 1) * WINDOW)

    @functools.lru_cache(maxsize=None)
    def window(b, kh, col_prev, col_cur):
        lo, hi = kh * HEAD_DIM, (kh + 1) * HEAD_DIM
        prev = (zp_ref[:, col_prev + lo:col_prev + hi] if b == 0
                else zq_ref[block_rows(b - 1), col_cur + lo:col_cur + hi])
        return jnp.concatenate([prev, zq_ref[block_rows(b), col_cur + lo:col_cur + hi]], axis=0)

    def band_scores(i):
        b, h = divmod(i, N_HEADS)
        return _band_scores(zq_ref[block_rows(b), h * HEAD_DIM:(h + 1) * HEAD_DIM], window(b, h // GROUP, 0, k_off))

    def band_finish(i, s):
        b, h = divmod(i, N_HEADS)
        p = _band_probs(s, sink_ref[h], upper, upper_first if b == 0 else upper)
        o = _band_values(p, window(b, h // GROUP, KV_WIDTH, v_off), ones)
        mix_ref[block_rows(b), h * HEAD_DIM:(h + 1) * HEAD_DIM] = o.astype(BF16)

    def cross_scores(h):
        lo, hi = h * MEM_HEAD_DIM, (h + 1) * MEM_HEAD_DIM
        return _cross_scores(zq_ref[:, qm_off + lo:qm_off + hi], mk_ref[pl.ds(h, N_MEM, stride=MEM_HEADS), :].astype(BF16))

    def cross_finish(h, s):
        lo, hi = h * MEM_HEAD_DIM, (h + 1) * MEM_HEAD_DIM
        mix_ref[:, TOKEN_WIDTH + lo:TOKEN_WIDTH + hi] = _cross_values(
            s, mv_ref[pl.ds(h, N_MEM, stride=MEM_HEADS), :].astype(BF16)).astype(BF16)

    _run_ahead(n_blocks * N_HEADS, SCORE_LOOKAHEAD, band_scores, band_finish)
    _run_ahead(MEM_HEADS, SCORE_LOOKAHEAD, cross_scores, cross_finish)


def _swa_prompt(z, mk, mv, sinks, weights, *, layer, batch, seq):
    rows, zc = z.shape
    tq = SWA_ROWS
    blocks_per_seq = seq // WINDOW
    steps_per_seq = seq // tq
    blocks_per_step = tq // WINDOW
    n_steps = batch * steps_per_seq
    kv_col_block = TOKEN_WIDTH // (2 * KV_WIDTH)
    bf16_rows = 2 * V7X_SUBLANES
    blocks = (_nbytes((tq, zc), BF16) + _nbytes((WINDOW, 2 * KV_WIDTH), BF16)
              + 2 * _nbytes((N_MEM, MEM_WIDTH), F32) + _nbytes((tq, D_MODEL), BF16))
    w_in_specs, w_out_specs, w_out_shapes, cast_steps = [], [], [], []
    for w in weights:
        r, c = w.shape[1:]
        n_cast = n_steps if r % (n_steps * bf16_rows) == 0 else n_steps // 2
        slab = r // n_cast
        assert slab * n_cast == r and slab % bf16_rows == 0
        blocks += _nbytes((slab, c), F32) + _nbytes((slab, c), BF16)
        w_in_specs.append(pl.BlockSpec((None, slab, c), lambda s, n=n_cast: (layer, jnp.minimum(s, n - 1), 0)))
        w_out_specs.append(pl.BlockSpec((slab, c), lambda s, n=n_cast: (jnp.minimum(s, n - 1), 0)))
        w_out_shapes.append(jax.ShapeDtypeStruct((r, c), BF16))
        cast_steps.append(n_cast)
    return pl.pallas_call(
        functools.partial(_swa_prompt_kernel, blocks_per_seq=blocks_per_seq, cast_steps=tuple(cast_steps)),
        grid=(n_steps,),
        in_specs=[
            pl.BlockSpec(memory_space=pltpu.SMEM),
            pl.BlockSpec((tq, zc), lambda s: (s, 0)),
            pl.BlockSpec((WINDOW, 2 * KV_WIDTH), lambda s: (jnp.maximum(s * blocks_per_step - 1, 0), kv_col_block)),
            pl.BlockSpec((None, N_MEM * MEM_HEADS, MEM_HEAD_DIM), lambda s: (layer, s // steps_per_seq, 0)),
            pl.BlockSpec((None, N_MEM * MEM_HEADS, MEM_HEAD_DIM), lambda s: (layer, s // steps_per_seq, 0)),
        ] + w_in_specs,
        out_specs=[pl.BlockSpec((tq, D_MODEL), lambda s: (s, 0))] + w_out_specs,
        out_shape=[jax.ShapeDtypeStruct((rows, D_MODEL), BF16)] + w_out_shapes,
        compiler_params=pltpu.CompilerParams(
            dimension_semantics=("arbitrary",),
            vmem_limit_bytes=_vmem_limit(blocks, 0)),
        name="swa_prompt",
    )(sinks, z, z, mk, mv, *weights)


def _swa_sample_kernel(sink_ref, z_ref, ck_ref, cv_ref, mk_ref, mv_ref, wo_ref,
                       mix_ref, nk_ref, nv_ref, wob_ref, knew_ref, vnew_ref, mixf_ref, *, dec_seq):
    t = dec_seq
    wob_ref[...] = wo_ref[...].astype(BF16)
    k_off = TOKEN_WIDTH
    v_off = TOKEN_WIDTH + KV_WIDTH
    qm_off = TOKEN_WIDTH + 2 * KV_WIDTH
    rows = GROUP * t
    qi = lax.broadcasted_iota(jnp.int32, (rows, WINDOW), 0) % t
    col = lax.broadcasted_iota(jnp.int32, (rows, WINDOW), 1)
    upper = col > qi
    knew_ref[...] = jnp.zeros(knew_ref.shape, F32)
    vnew_ref[...] = jnp.zeros(vnew_ref.shape, F32)
    for n in range(SAMPLE_GROUP):
        r0, r1 = n * t, (n + 1) * t
        k_new = z_ref[r0:r1, k_off:k_off + KV_WIDTH].astype(F32)
        v_new = z_ref[r0:r1, v_off:v_off + KV_WIDTH].astype(F32)
        knew_ref[n, 0:t, :] = k_new
        vnew_ref[n, 0:t, :] = v_new
        nk_ref[n, 0:WINDOW - t, :] = ck_ref[n, t:WINDOW, :]
        nv_ref[n, 0:WINDOW - t, :] = cv_ref[n, t:WINDOW, :]
        nk_ref[n, WINDOW - t:WINDOW, :] = k_new
        nv_ref[n, WINDOW - t:WINDOW, :] = v_new

    def scores(i):
        n, kh = divmod(i, N_KV_HEADS)
        lo, hi = kh * HEAD_DIM, (kh + 1) * HEAD_DIM
        k = jnp.concatenate([ck_ref[n, :, lo:hi], knew_ref[n, :, lo:hi]], axis=0).astype(BF16)
        q = jnp.concatenate(
            [z_ref[n * t:(n + 1) * t, (kh * GROUP + g) * HEAD_DIM:(kh * GROUP + g + 1) * HEAD_DIM].astype(F32)
             for g in range(GROUP)], axis=0).astype(BF16)
        return _band_scores(q, k)

    def finish(i, s):
        n, kh = divmod(i, N_KV_HEADS)
        lo, hi = kh * HEAD_DIM, (kh + 1) * HEAD_DIM
        v = jnp.concatenate([cv_ref[n, :, lo:hi], vnew_ref[n, :, lo:hi]], axis=0).astype(BF16)
        sink = jnp.concatenate(
            [jnp.full((t, 1), sink_ref[kh * GROUP + g], F32) for g in range(GROUP)], axis=0)
        o = _band_values(_band_probs(s, sink, upper, upper), v, jnp.ones((2 * WINDOW, HEAD_DIM), BF16))
        for g in range(GROUP):
            h = kh * GROUP + g
            mixf_ref[n * t:(n + 1) * t, h * HEAD_DIM:(h + 1) * HEAD_DIM] = o[g * t:(g + 1) * t, :]

    _run_ahead(SAMPLE_GROUP * N_KV_HEADS, SAMPLE_LOOKAHEAD, scores, finish)
    _sample_cross_attention(z_ref, mk_ref, mv_ref, mixf_ref, qm_off, t)
    mix_ref[...] = mixf_ref[...].astype(BF16)


def _swa_sample(z, cache_k, cache_v, mem_k, mem_v, sinks, w_out, *, layer, dec_seq):
    rows, zc = z.shape
    dec_batch = cache_k.shape[0]
    g = SAMPLE_GROUP
    gr = g * dec_seq
    n_steps = dec_batch // g
    wk, wn = w_out.shape[1:]
    slab = wk // n_steps
    assert slab * n_steps == wk and slab % (2 * V7X_SUBLANES) == 0
    win = jax.ShapeDtypeStruct((dec_batch, WINDOW, KV_WIDTH), F32)
    blocks = (_nbytes((gr, zc), BF16) + 4 * _nbytes((g, WINDOW, KV_WIDTH), F32)
              + 2 * _nbytes((g, N_MEM, V7X_SUBLANES, MEM_HEAD_DIM), F32) + _nbytes((gr, D_MODEL), BF16)
              + _nbytes((slab, wn), F32) + _nbytes((slab, wn), BF16))
    scratch = 2 * _nbytes((g, WINDOW, KV_WIDTH), F32) + _nbytes((gr, D_MODEL), F32)
    return pl.pallas_call(
        functools.partial(_swa_sample_kernel, dec_seq=dec_seq),
        grid=(n_steps,),
        in_specs=[
            pl.BlockSpec(memory_space=pltpu.SMEM),
            pl.BlockSpec((gr, zc), lambda i: (i, 0)),
            pl.BlockSpec((g, WINDOW, KV_WIDTH), lambda i: (i, 0, 0)),
            pl.BlockSpec((g, WINDOW, KV_WIDTH), lambda i: (i, 0, 0)),
            pl.BlockSpec((None, g, N_MEM * MEM_HEADS, MEM_HEAD_DIM), lambda i: (layer, i, 0, 0)),
            pl.BlockSpec((None, g, N_MEM * MEM_HEADS, MEM_HEAD_DIM), lambda i: (layer, i, 0, 0)),
            pl.BlockSpec((None, slab, wn), lambda i: (0, i, 0)),
        ],
        out_specs=[
            pl.BlockSpec((gr, D_MODEL), lambda i: (i, 0)),
            pl.BlockSpec((g, WINDOW, KV_WIDTH), lambda i: (i, 0, 0)),
            pl.BlockSpec((g, WINDOW, KV_WIDTH), lambda i: (i, 0, 0)),
            pl.BlockSpec((slab, wn), lambda i: (i, 0)),
        ],
        out_shape=[jax.ShapeDtypeStruct((rows, D_MODEL), BF16), win, win,
                   jax.ShapeDtypeStruct((wk, wn), BF16)],
        scratch_shapes=[pltpu.VMEM((g, WINDOW, KV_WIDTH), F32), pltpu.VMEM((g, WINDOW, KV_WIDTH), F32),
                        pltpu.VMEM((gr, D_MODEL), F32)],
        compiler_params=pltpu.CompilerParams(
            dimension_semantics=("arbitrary",),
            vmem_limit_bytes=_vmem_limit(blocks, scratch)),
        name="swa_sample",
    )(sinks, z, cache_k, cache_v, mem_k, mem_v, w_out)


def kernel(x_prompt, x_sample, mem_prompt, state_conv, cache_win_k, cache_win_v, cache_mem_k, cache_mem_v,
           norm_mix, norm_mem, w_mem_kv, norm_ffn, w_gate, w_up, w_down,
           conv_w_in, conv_w, conv_w_out, attn_w_in, attn_sinks, attn_w_out, norm_final):
    batch, seq, d = x_prompt.shape
    dec_batch, dec_seq, _ = x_sample.shape
    depth = norm_mix.shape[0]
    d_ff = w_gate.shape[2]
    prompt_rows = batch * seq
    sample_rows = dec_batch * dec_seq
    assert d == D_MODEL and depth == 2 and seq % CONV_ROWS == 0 and seq % SWA_ROWS == 0 and SWA_ROWS % WINDOW == 0
    assert prompt_rows % PROMPT_TILE == 0 and dec_batch % SAMPLE_GROUP == 0
    assert dec_seq == V7X_SUBLANES and d_ff % FFN_COL_TILE == 0 and d % COL_TILE == 0
    assert d_ff % DOWN_K_TILE == 0 and d % DOWN_X_CHUNK == 0
    assert prompt_rows % FFN_ROW_TILE == 0 and prompt_rows % sample_rows == 0

    xp = x_prompt.reshape(prompt_rows, d)
    xs = x_sample.reshape(sample_rows, d)
    mem = mem_prompt.reshape(batch * N_MEM, d)
    mem_k_s = cache_mem_k.reshape(depth, dec_batch, N_MEM * MEM_HEADS, MEM_HEAD_DIM)
    mem_v_s = cache_mem_v.reshape(depth, dec_batch, N_MEM * MEM_HEADS, MEM_HEAD_DIM)
    g_mix = norm_mix.reshape(depth, 1, d)
    g_ffn = norm_ffn.reshape(depth, 1, d)

    mk, mv = _mem_kv(mem, norm_mem.reshape(depth, 1, d), w_mem_kv)

    hp, hs = _norm_rows(xp, xs, g_mix, layer=0)
    zp, zs = _matmul(hp, hs, conv_w_in, w_layer=0)
    mix_s, conv_s, wo_bf = _conv_sample(zs, state_conv, mem_k_s, mem_v_s, conv_w, conv_w_out,
                                        layer=0, dec_seq=dec_seq)
    mix_p, conv_p, wd_bf = _conv_prompt(zp, mk, mv, conv_w, w_down, layer=0, batch=batch, seq=seq)
    xp, xs, hp, hs = _out_proj(mix_p, mix_s, wo_bf, xp, xs, g_ffn, layer=0)
    ap, as_ = _ffn_up(hp, hs, w_gate, w_up, layer=0)
    xp, xs, hp, hs = _ffn_down(ap, as_, wd_bf[None], xp, xs, g_mix, w_layer=0, g_layer=1, final=False)

    zp, zs = _matmul(hp, hs, attn_w_in, w_layer=0)
    sinks = attn_sinks[0]
    mix_s, win_k_s, win_v_s, wo_bf = _swa_sample(
        zs, cache_win_k[0].reshape(dec_batch, WINDOW, KV_WIDTH), cache_win_v[0].reshape(dec_batch, WINDOW, KV_WIDTH),
        mem_k_s, mem_v_s, sinks, attn_w_out, layer=1, dec_seq=dec_seq)
    mix_p, wd_bf = _swa_prompt(zp, mk, mv, sinks, [w_down], layer=1, batch=batch, seq=seq)
    xp, xs, hp, hs = _out_proj(mix_p, mix_s, wo_bf, xp, xs, g_ffn, layer=1)
    ap, as_ = _ffn_up(hp, hs, w_gate, w_up, layer=1)
    y_prompt, y_sample = _ffn_down(ap, as_, wd_bf[None], xp, xs, norm_final.reshape(1, 1, d),
                                   w_layer=0, g_layer=0, final=True)

    win_p = zp.reshape(batch, seq, -1)[:, seq - WINDOW:, TOKEN_WIDTH:TOKEN_WIDTH + 2 * KV_WIDTH].astype(F32)
    kv_shape = (1, -1, WINDOW, N_KV_HEADS, HEAD_DIM)
    mem_shape = (depth, batch, N_MEM, MEM_HEADS, MEM_HEAD_DIM)
    return (y_prompt.reshape(batch, seq, d),
            y_sample.reshape(dec_batch, dec_seq, d),
            conv_p[None],
            conv_s[None],
            win_p[..., :KV_WIDTH].reshape(kv_shape),
            win_p[..., KV_WIDTH:].reshape(kv_shape),
            win_k_s.reshape(kv_shape),
            win_v_s.reshape(kv_shape),
            mk.reshape(mem_shape),
            mv.reshape(mem_shape))
```

```python
import functools

import jax
import jax.numpy as jnp
from jax import lax
from jax.experimental import pallas as pl
from jax.experimental.pallas import tpu as pltpu

F32 = jnp.float32
BF16 = jnp.bfloat16

D_MODEL = 2048
N_MEM = 256
MEM_HEADS = 4
MEM_WIDTH = D_MODEL // 4
MEM_HEAD_DIM = MEM_WIDTH // MEM_HEADS
TOKEN_WIDTH = D_MODEL - MEM_WIDTH
CONV_WIDTH = 3
WINDOW = 128
HEAD_DIM = 64
N_HEADS = TOKEN_WIDTH // HEAD_DIM
N_KV_HEADS = 4
GROUP = N_HEADS // N_KV_HEADS
KV_WIDTH = N_KV_HEADS * HEAD_DIM
EPS = 1e-6

V7X_VMEM_BYTES = 64 * 1024 * 1024
V7X_SUBLANES = 8
V7X_MXU_DEPTH = 256

PROMPT_TILE = 1024
COL_TILE = 512
DOWN_K_TILE = 1408
DOWN_X_CHUNK = 512
FFN_ROW_TILE = 2048
FFN_COL_TILE = 512
MXU_ROWS = 1024
OUT_ROWS = 512
CONV_ROWS = 512
SWA_ROWS = 512
SCORE_LOOKAHEAD = 5
SAMPLE_LOOKAHEAD = 4
SAMPLE_GROUP = 8
SIDE_CAST_STEPS = 32
CARRY_ROWS = V7X_SUBLANES
MIN_TEMP_BYTES = 8 << 20
VMEM_RESERVE_BYTES = 6 << 20


def _nbytes(shape, dtype):
    n = 1
    for s in shape:
        n *= s
    return n * jnp.dtype(dtype).itemsize


def _vmem_limit(block_bytes, scratch_bytes):
    need = 2 * block_bytes + scratch_bytes
    temporaries = max(need // 4, MIN_TEMP_BYTES)
    return int(min(need + temporaries, V7X_VMEM_BYTES - VMEM_RESERVE_BYTES))


def _rmsnorm(x, g):
    r = lax.rsqrt(jnp.mean(x * x, axis=-1, keepdims=True) + EPS)
    return (x * r) * g


def _dot(a, b):
    return jnp.dot(a, b, preferred_element_type=F32)


def _dot_nt(a, b):
    return lax.dot_general(a, b, (((1,), (1,)), ((), ())), preferred_element_type=F32)


def _prompt_rows_map(i, j):
    return (i, 0)


def _prompt_tile_map(i, j):
    return (i, j)


def _sample_tile_map(n_pt):
    return lambda i, j: (0, jnp.where(i == n_pt - 1, j, 0))


def _const_map(i, j):
    return (0, 0)


def _norm_rows_kernel(xp_ref, xs_ref, g_ref, hp_ref, hs_ref, *, n_pt):
    hp_ref[...] = _rmsnorm(xp_ref[...], g_ref[...]).astype(BF16)

    @pl.when(pl.program_id(0) == n_pt - 1)
    def _():
        hs_ref[...] = _rmsnorm(xs_ref[...], g_ref[...]).astype(BF16)


def _norm_rows(xp, xs, g, *, layer):
    rp, d = xp.shape
    rs = xs.shape[0]
    tm = OUT_ROWS
    n_pt = rp // tm
    blocks = _nbytes((tm, d), F32) + _nbytes((tm, d), BF16) + _nbytes((rs, d), F32) + _nbytes((rs, d), BF16)
    return pl.pallas_call(
        functools.partial(_norm_rows_kernel, n_pt=n_pt),
        grid=(n_pt,),
        in_specs=[
            pl.BlockSpec((tm, d), lambda s: (s, 0)),
            pl.BlockSpec((rs, d), lambda s: (0, 0)),
            pl.BlockSpec((None, 1, d), lambda s: (layer, 0, 0)),
        ],
        out_specs=[
            pl.BlockSpec((tm, d), lambda s: (s, 0)),
            pl.BlockSpec((rs, d), lambda s: (0, 0)),
        ],
        out_shape=[jax.ShapeDtypeStruct((rp, d), BF16), jax.ShapeDtypeStruct((rs, d), BF16)],
        compiler_params=pltpu.CompilerParams(
            dimension_semantics=("arbitrary",),
            vmem_limit_bytes=_vmem_limit(blocks, 0)),
        name="norm_rows",
    )(xp, xs, g)


def _swiglu(h, wg, wu):
    gate = _dot(h, wg)
    up = _dot(h, wu)
    return (gate * jax.nn.sigmoid(gate) * up).astype(BF16)


def _ffn_up_kernel(hp_ref, hs_ref, wg_ref, wu_ref, ap_ref, as_ref, *, n_pt):
    wg, wu = wg_ref[...].astype(BF16), wu_ref[...].astype(BF16)
    for r in range(0, hp_ref.shape[0], MXU_ROWS):
        ap_ref[r:r + MXU_ROWS, :] = _swiglu(hp_ref[r:r + MXU_ROWS, :], wg, wu)

    @pl.when(pl.program_id(0) == n_pt - 1)
    def _():
        as_ref[...] = _swiglu(hs_ref[...], wg_ref[...].astype(BF16), wu_ref[...].astype(BF16))


def _ffn_up(hp, hs, wg, wu, *, layer):
    rp, d = hp.shape
    rs = hs.shape[0]
    n = wg.shape[2]
    tm, tn = FFN_ROW_TILE, FFN_COL_TILE
    n_pt, n_j = rp // tm, n // tn
    blocks = (_nbytes((tm, d), BF16) + _nbytes((rs, d), BF16) + 2 * _nbytes((d, tn), wg.dtype)
              + _nbytes((tm, tn), BF16) + _nbytes((rs, tn), BF16))
    temps = 2 * _nbytes((d, tn), BF16) + 3 * _nbytes((tm, tn), F32)
    w_spec = pl.BlockSpec((None, d, tn), lambda i, j: (layer, 0, j))
    return pl.pallas_call(
        functools.partial(_ffn_up_kernel, n_pt=n_pt),
        grid=(n_pt, n_j),
        in_specs=[
            pl.BlockSpec((tm, d), _prompt_rows_map),
            pl.BlockSpec((rs, d), _const_map),
            w_spec,
            w_spec,
        ],
        out_specs=[
            pl.BlockSpec((tm, tn), _prompt_tile_map),
            pl.BlockSpec((rs, tn), _sample_tile_map(n_pt)),
        ],
        out_shape=[jax.ShapeDtypeStruct((rp, n), BF16), jax.ShapeDtypeStruct((rs, n), BF16)],
        compiler_params=pltpu.CompilerParams(
            dimension_semantics=("arbitrary", "arbitrary"),
            vmem_limit_bytes=_vmem_limit(blocks, temps)),
        name="ffn_up",
    )(hp, hs, wg, wu)


def _out_proj_kernel(ap_ref, as_ref, w_ref, xp_ref, xs_ref, g_ref, op_ref, os_ref, hp_ref, hs_ref, *, n_pt):
    x = xp_ref[...] + _dot(ap_ref[...], w_ref[...])
    op_ref[...] = x
    hp_ref[...] = _rmsnorm(x, g_ref[...]).astype(BF16)

    @pl.when(pl.program_id(0) == n_pt - 1)
    def _():
        x = xs_ref[...] + _dot(as_ref[...], w_ref[...])
        os_ref[...] = x
        hs_ref[...] = _rmsnorm(x, g_ref[...]).astype(BF16)


def _out_proj(ap, as_, w, xp, xs, g, *, layer):
    rp, k = ap.shape
    rs = as_.shape[0]
    n = w.shape[1]
    tm = OUT_ROWS
    n_pt = rp // tm
    rows_map = lambda s: (s, 0)
    const = lambda s: (0, 0)
    once = dict(pipeline_mode=pl.Buffered(1))
    blocks = _nbytes((tm, k), BF16) + 2 * _nbytes((tm, n), F32) + _nbytes((tm, n), BF16)
    resident = (_nbytes((k, n), BF16) + _nbytes((rs, k), BF16) + _nbytes((rs, n), F32)
                + 2 * (_nbytes((rs, n), F32) + _nbytes((rs, n), BF16)))
    return pl.pallas_call(
        functools.partial(_out_proj_kernel, n_pt=n_pt),
        grid=(n_pt,),
        in_specs=[
            pl.BlockSpec((tm, k), rows_map),
            pl.BlockSpec((rs, k), const, **once),
            pl.BlockSpec((k, n), const, **once),
            pl.BlockSpec((tm, n), rows_map),
            pl.BlockSpec((rs, n), const, **once),
            pl.BlockSpec((None, 1, n), lambda s: (layer, 0, 0)),
        ],
        out_specs=[
            pl.BlockSpec((tm, n), rows_map),
            pl.BlockSpec((rs, n), const),
            pl.BlockSpec((tm, n), rows_map),
            pl.BlockSpec((rs, n), const),
        ],
        out_shape=[jax.ShapeDtypeStruct((rp, n), F32), jax.ShapeDtypeStruct((rs, n), F32),
                   jax.ShapeDtypeStruct((rp, n), BF16), jax.ShapeDtypeStruct((rs, n), BF16)],
        compiler_params=pltpu.CompilerParams(
            dimension_semantics=("arbitrary",),
            vmem_limit_bytes=_vmem_limit(blocks, resident + 2 * _nbytes((tm, n), F32))),
        name="out_proj",
    )(ap, as_, w, xp, xs, g)


def _matmul_kernel(hp_ref, hs_ref, w_ref, *refs, n_pt, n_j, n_cast):
    if n_cast:
        side_ref, zp_ref, zs_ref, side_bf_ref = refs

        @pl.when(pl.program_id(0) * n_j + pl.program_id(1) < n_cast)
        def _():
            side_bf_ref[...] = side_ref[...].astype(BF16)
    else:
        zp_ref, zs_ref = refs

    w = w_ref[...].astype(BF16)
    for r in range(0, hp_ref.shape[0], MXU_ROWS):
        zp_ref[r:r + MXU_ROWS, :] = _dot(hp_ref[r:r + MXU_ROWS, :], w).astype(zp_ref.dtype)

    @pl.when(pl.program_id(0) == n_pt - 1)
    def _():
        zs_ref[...] = _dot(hs_ref[...], w_ref[...].astype(BF16)).astype(zs_ref.dtype)


def _matmul(hp, hs, w, *, w_layer, side=None, side_layer=0):
    rp, d = hp.shape
    rs = hs.shape[0]
    n = w.shape[2]
    tm, tn = FFN_ROW_TILE, COL_TILE
    n_pt, n_j = rp // tm, n // tn
    blocks = (_nbytes((tm, d), BF16) + _nbytes((rs, d), BF16) + _nbytes((d, tn), F32)
              + _nbytes((tm, tn), BF16) + _nbytes((rs, tn), BF16))
    temps = _nbytes((d, tn), BF16) + _nbytes((tm, tn), F32)
    in_specs = [
        pl.BlockSpec((tm, d), _prompt_rows_map),
        pl.BlockSpec((rs, d), _const_map),
        pl.BlockSpec((None, d, tn), lambda i, j: (w_layer, 0, j)),
    ]
    out_specs = [
        pl.BlockSpec((tm, tn), _prompt_tile_map),
        pl.BlockSpec((rs, tn), _sample_tile_map(n_pt)),
    ]
    out_shape = [jax.ShapeDtypeStruct((rp, n), BF16), jax.ShapeDtypeStruct((rs, n), BF16)]
    operands = [hp, hs, w]
    n_cast = 0
    if side is not None:
        r, c = side.shape[1:]
        n_cast = SIDE_CAST_STEPS
        slab = r // n_cast
        assert n_cast <= n_pt * n_j and slab * n_cast == r and slab % (2 * V7X_SUBLANES) == 0
        slab_idx = lambda i, j: jnp.minimum(i * n_j + j, n_cast - 1)
        in_specs.append(pl.BlockSpec((None, slab, c), lambda i, j: (side_layer, slab_idx(i, j), 0)))
        out_specs.append(pl.BlockSpec((slab, c), lambda i, j: (slab_idx(i, j), 0)))
        out_shape.append(jax.ShapeDtypeStruct((r, c), BF16))
        operands.append(side)
        blocks += _nbytes((slab, c), F32) + _nbytes((slab, c), BF16)
    return pl.pallas_call(
        functools.partial(_matmul_kernel, n_pt=n_pt, n_j=n_j, n_cast=n_cast),
        grid=(n_pt, n_j),
        in_specs=in_specs,
        out_specs=out_specs,
        out_shape=out_shape,
        compiler_params=pltpu.CompilerParams(
            dimension_semantics=("arbitrary", "arbitrary"),
            vmem_limit_bytes=_vmem_limit(blocks, temps)),
        name="matmul",
    )(*operands)


def _ffn_down_kernel(ap_ref, as_ref, w_ref, xp_ref, xs_ref, g_ref, *refs, n_pt, n_k, n_xc, final):
    i, k = pl.program_id(0), pl.program_id(1)
    if final:
        op_ref, os_ref, apl_ref, asl_ref, wl_ref = refs
    else:
        op_ref, os_ref, hp_ref, hs_ref, apl_ref, asl_ref, wl_ref = refs
    xc = xp_ref.shape[1]
    kf = (ap_ref.shape[1] // V7X_MXU_DEPTH) * V7X_MXU_DEPTH
    even = k % 2 == 0

    def whole_passes(a_ref):
        return _dot(a_ref[:, :kf], w_ref[:kf, :])

    def with_stash(a_ref, al_ref):
        return _dot(jnp.concatenate([al_ref[...], a_ref[...]], axis=1),
                    jnp.concatenate([wl_ref[...], w_ref[...]], axis=0))

    for kk in range(n_k):
        @pl.when(k == kk)
        def _():
            cols = slice(kk * xc, (kk + 1) * xc)
            if 0 < kk < n_xc:
                op_ref[:, cols] = op_ref[:, cols] + xp_ref[...]
            contrib = whole_passes(ap_ref) if kk % 2 == 0 else with_stash(ap_ref, apl_ref)
            if kk == 0:
                op_ref[...] = contrib
                op_ref[:, cols] = op_ref[:, cols] + xp_ref[...]
            else:
                op_ref[...] = op_ref[...] + contrib
            if kk == n_k - 1:
                if final:
                    op_ref[...] = _rmsnorm(op_ref[...], g_ref[...])
                else:
                    hp_ref[...] = _rmsnorm(op_ref[...], g_ref[...]).astype(BF16)

    @pl.when(i == n_pt - 1)
    def _():
        @pl.when(k == 0)
        def _():
            os_ref[...] = xs_ref[...] + whole_passes(as_ref)

        @pl.when(jnp.logical_and(k > 0, even))
        def _():
            os_ref[...] = os_ref[...] + whole_passes(as_ref)

        @pl.when(jnp.logical_not(even))
        def _():
            os_ref[...] = os_ref[...] + with_stash(as_ref, asl_ref)

        @pl.when(even)
        def _():
            asl_ref[...] = as_ref[:, kf:]

        @pl.when(k == n_k - 1)
        def _():
            if final:
                os_ref[...] = _rmsnorm(os_ref[...], g_ref[...])
            else:
                hs_ref[...] = _rmsnorm(os_ref[...], g_ref[...]).astype(BF16)

    @pl.when(even)
    def _():
        apl_ref[...] = ap_ref[:, kf:]
        wl_ref[...] = w_ref[kf:, :]


def _ffn_down(ap, as_, w, xp, xs, g, *, w_layer, g_layer, final):
    rp, kdim = ap.shape
    rs = as_.shape[0]
    n = w.shape[2]
    tm, tk, xc = PROMPT_TILE, DOWN_K_TILE, DOWN_X_CHUNK
    n_pt, n_k, n_xc = rp // tm, kdim // tk, n // xc
    k_left = tk % V7X_MXU_DEPTH
    assert n_xc <= n_k and n_k % 2 == 0 and 2 * k_left == V7X_MXU_DEPTH and w.dtype == BF16
    rows = lambda i, k: (i, 0)
    blocks = (_nbytes((tm, tk), BF16) + _nbytes((rs, tk), BF16) + _nbytes((tk, n), w.dtype)
              + _nbytes((tm, xc), F32) + _nbytes((rs, n), F32)
              + _nbytes((tm, n), F32) + _nbytes((rs, n), F32))
    out_specs = [pl.BlockSpec((tm, n), rows), pl.BlockSpec((rs, n), _const_map)]
    out_shape = [jax.ShapeDtypeStruct((rp, n), F32), jax.ShapeDtypeStruct((rs, n), F32)]
    if not final:
        blocks += _nbytes((tm, n), BF16) + _nbytes((rs, n), BF16)
        out_specs += [pl.BlockSpec((tm, n), rows), pl.BlockSpec((rs, n), _const_map)]
        out_shape += [jax.ShapeDtypeStruct((rp, n), BF16), jax.ShapeDtypeStruct((rs, n), BF16)]
    return pl.pallas_call(
        functools.partial(_ffn_down_kernel, n_pt=n_pt, n_k=n_k, n_xc=n_xc, final=final),
        grid=(n_pt, n_k),
        in_specs=[
            pl.BlockSpec((tm, tk), lambda i, k: (i, k)),
            pl.BlockSpec((rs, tk), lambda i, k: (0, jnp.where(i == n_pt - 1, k, 0))),
            pl.BlockSpec((None, tk, n), lambda i, k: (w_layer, k, 0)),
            pl.BlockSpec((tm, xc), lambda i, k: (i, jnp.minimum(k, n_xc - 1))),
            pl.BlockSpec((rs, n), _const_map),
            pl.BlockSpec((None, 1, n), lambda i, k: (g_layer, 0, 0)),
        ],
        out_specs=out_specs,
        out_shape=out_shape,
        scratch_shapes=[pltpu.VMEM((tm, k_left), BF16), pltpu.VMEM((rs, k_left), BF16),
                        pltpu.VMEM((k_left, n), BF16)],
        compiler_params=pltpu.CompilerParams(
            dimension_semantics=("arbitrary", "arbitrary"),
            vmem_limit_bytes=_vmem_limit(blocks, _nbytes((tk, n), BF16))),
        name="ffn_down",
    )(ap, as_, w, xp, xs, g)


def _mem_kv_kernel(x_ref, g_ref, w_ref, k_ref, v_ref):
    h = _rmsnorm(x_ref[...], g_ref[...]).astype(BF16)
    kv = _dot(h, w_ref[...].astype(BF16))
    tm = x_ref.shape[0]
    for hd in range(MEM_HEADS):
        head_rows = pl.ds(hd, tm, stride=MEM_HEADS)
        k_ref[head_rows, :] = kv[:, hd * MEM_HEAD_DIM:(hd + 1) * MEM_HEAD_DIM]
        v_ref[head_rows, :] = kv[:, MEM_WIDTH + hd * MEM_HEAD_DIM:MEM_WIDTH + (hd + 1) * MEM_HEAD_DIM]


def _mem_kv(mem, g, w):
    rows, d = mem.shape
    depth = w.shape[0]
    tm = 512
    out = jax.ShapeDtypeStruct((depth, rows * MEM_HEADS, MEM_HEAD_DIM), F32)
    blocks = (_nbytes((tm, d), F32) + _nbytes((d, 2 * MEM_WIDTH), F32) + 2 * _nbytes((tm, MEM_WIDTH), F32))
    return pl.pallas_call(
        _mem_kv_kernel,
        grid=(depth, rows // tm),
        in_specs=[
            pl.BlockSpec((tm, d), lambda l, i: (i, 0)),
            pl.BlockSpec((None, 1, d), lambda l, i: (l, 0, 0)),
            pl.BlockSpec((None, d, 2 * MEM_WIDTH), lambda l, i: (l, 0, 0)),
        ],
        out_specs=[
            pl.BlockSpec((None, tm * MEM_HEADS, MEM_HEAD_DIM), lambda l, i: (l, i, 0)),
            pl.BlockSpec((None, tm * MEM_HEADS, MEM_HEAD_DIM), lambda l, i: (l, i, 0)),
        ],
        out_shape=[out, out],
        compiler_params=pltpu.CompilerParams(
            dimension_semantics=("arbitrary", "arbitrary"),
            vmem_limit_bytes=_vmem_limit(blocks, _nbytes((d, 2 * MEM_WIDTH), BF16))),
        name="mem_kv",
    )(mem, g, w)


def _cross_scores(q, k):
    return _dot_nt(q, k) * (MEM_HEAD_DIM ** -0.5)


def _cross_values(s, v):
    e = jnp.exp(s - jnp.max(s, axis=-1, keepdims=True)).astype(BF16)
    return _dot(e, v) / _dot(e, jnp.ones(v.shape, BF16))


def _run_ahead(n_items, lookahead, first, second):
    pending = {}
    for i in range(n_items + lookahead):
        if i < n_items:
            pending[i] = first(i)
        if i >= lookahead:
            second(i - lookahead, pending.pop(i - lookahead))


def _conv_prompt_kernel(z_ref, mk_ref, mv_ref, cw_ref, mix_ref, st_ref, ext_ref, *, tiles_per_seq):
    s = pl.program_id(0)
    tq = z_ref.shape[0]

    @pl.when(s % tiles_per_seq == 0)
    def _():
        ext_ref[0:CARRY_ROWS, :] = jnp.zeros((CARRY_ROWS, TOKEN_WIDTH), F32)

    c = z_ref[:, TOKEN_WIDTH:2 * TOKEN_WIDTH].astype(F32)
    u = z_ref[:, 2 * TOKEN_WIDTH:3 * TOKEN_WIDTH].astype(F32)
    cu = c * u
    ext_ref[CARRY_ROWS:CARRY_ROWS + tq, :] = cu
    conv = (cw_ref[0:1, :] * ext_ref[CARRY_ROWS - 2:CARRY_ROWS - 2 + tq, :]
            + cw_ref[1:2, :] * ext_ref[CARRY_ROWS - 1:CARRY_ROWS - 1 + tq, :]
            + cw_ref[2:3, :] * cu)
    b = z_ref[:, 0:TOKEN_WIDTH].astype(F32)
    mix_ref[:, 0:TOKEN_WIDTH] = (b * conv).astype(BF16)
    st_ref[...] = ext_ref[CARRY_ROWS + tq - 2:CARRY_ROWS + tq, :]
    ext_ref[0:CARRY_ROWS, :] = ext_ref[tq:tq + CARRY_ROWS, :]

    def scores(h):
        lo, hi = h * MEM_HEAD_DIM, (h + 1) * MEM_HEAD_DIM
        return _cross_scores(z_ref[:, 3 * TOKEN_WIDTH + lo:3 * TOKEN_WIDTH + hi], mk_ref[pl.ds(h, N_MEM, stride=MEM_HEADS), :].astype(BF16))

    def finish(h, s):
        lo, hi = h * MEM_HEAD_DIM, (h + 1) * MEM_HEAD_DIM
        mix_ref[:, TOKEN_WIDTH + lo:TOKEN_WIDTH + hi] = _cross_values(
            s, mv_ref[pl.ds(h, N_MEM, stride=MEM_HEADS), :].astype(BF16)).astype(BF16)

    _run_ahead(MEM_HEADS, SCORE_LOOKAHEAD, scores, finish)


def _conv_prompt(z, mk, mv, conv_w, *, layer, batch, seq):
    rows, zc = z.shape
    tq = CONV_ROWS
    tiles_per_seq = seq // tq
    n_steps = batch * tiles_per_seq
    blocks = _nbytes((tq, zc), BF16) + 2 * _nbytes((N_MEM, MEM_WIDTH), F32) + _nbytes((tq, D_MODEL), BF16)
    scratch = _nbytes((tq + CARRY_ROWS, TOKEN_WIDTH), F32)
    return pl.pallas_call(
        functools.partial(_conv_prompt_kernel, tiles_per_seq=tiles_per_seq),
        grid=(n_steps,),
        in_specs=[
            pl.BlockSpec((tq, zc), lambda s: (s, 0)),
            pl.BlockSpec((None, N_MEM * MEM_HEADS, MEM_HEAD_DIM), lambda s: (layer, s // tiles_per_seq, 0)),
            pl.BlockSpec((None, N_MEM * MEM_HEADS, MEM_HEAD_DIM), lambda s: (layer, s // tiles_per_seq, 0)),
            pl.BlockSpec((None, CONV_WIDTH, TOKEN_WIDTH), lambda s: (0, 0, 0)),
        ],
        out_specs=[
            pl.BlockSpec((tq, D_MODEL), lambda s: (s, 0)),
            pl.BlockSpec((None, CONV_WIDTH - 1, TOKEN_WIDTH), lambda s: (s // tiles_per_seq, 0, 0)),
        ],
        out_shape=[
            jax.ShapeDtypeStruct((rows, D_MODEL), BF16),
            jax.ShapeDtypeStruct((batch, CONV_WIDTH - 1, TOKEN_WIDTH), F32),
        ],
        scratch_shapes=[pltpu.VMEM((tq + CARRY_ROWS, TOKEN_WIDTH), F32)],
        compiler_params=pltpu.CompilerParams(
            dimension_semantics=("arbitrary",),
            vmem_limit_bytes=_vmem_limit(blocks, scratch + 6 * _nbytes((tq, TOKEN_WIDTH), F32))),
        name="conv_prompt",
    )(z, mk, mv, conv_w)


def _conv_sample_kernel(z_ref, st_ref, mk_ref, mv_ref, cw_ref, wo_ref, mix_ref, nst_ref, wob_ref,
                        ext_ref, mixf_ref, *, dec_seq):
    t = dec_seq
    wob_ref[...] = wo_ref[...].astype(BF16)
    for n in range(SAMPLE_GROUP):
        r0, r1 = n * t, (n + 1) * t
        c = z_ref[r0:r1, TOKEN_WIDTH:2 * TOKEN_WIDTH].astype(F32)
        u = z_ref[r0:r1, 2 * TOKEN_WIDTH:3 * TOKEN_WIDTH].astype(F32)
        cu = c * u
        ext_ref[CARRY_ROWS - 2:CARRY_ROWS, :] = st_ref[n]
        ext_ref[CARRY_ROWS:CARRY_ROWS + t, :] = cu
        conv = (cw_ref[0:1, :] * ext_ref[CARRY_ROWS - 2:CARRY_ROWS - 2 + t, :]
                + cw_ref[1:2, :] * ext_ref[CARRY_ROWS - 1:CARRY_ROWS - 1 + t, :]
                + cw_ref[2:3, :] * cu)
        b = z_ref[r0:r1, 0:TOKEN_WIDTH].astype(F32)
        mixf_ref[r0:r1, 0:TOKEN_WIDTH] = b * conv
        nst_ref[n] = ext_ref[CARRY_ROWS + t - 2:CARRY_ROWS + t, :]
    _sample_cross_attention(z_ref, mk_ref, mv_ref, mixf_ref, 3 * TOKEN_WIDTH, t)
    mix_ref[...] = mixf_ref[...].astype(BF16)


def _sample_cross_attention(z_ref, mk_ref, mv_ref, mixf_ref, qm_off, t):
    def scores(i):
        n, h = divmod(i, MEM_HEADS)
        q = z_ref[n * t:(n + 1) * t, qm_off + h * MEM_HEAD_DIM:qm_off + (h + 1) * MEM_HEAD_DIM]
        return _cross_scores(q, mk_ref[n, pl.ds(h, N_MEM, stride=MEM_HEADS), :].astype(BF16))

    def finish(i, s):
        n, h = divmod(i, MEM_HEADS)
        o = _cross_values(s, mv_ref[n, pl.ds(h, N_MEM, stride=MEM_HEADS), :].astype(BF16))
        mixf_ref[n * t:(n + 1) * t, TOKEN_WIDTH + h * MEM_HEAD_DIM:TOKEN_WIDTH + (h + 1) * MEM_HEAD_DIM] = o

    _run_ahead(SAMPLE_GROUP * MEM_HEADS, SAMPLE_LOOKAHEAD, scores, finish)


def _conv_sample(z, state, mem_k, mem_v, conv_w, w_out, *, layer, dec_seq):
    rows, zc = z.shape
    dec_batch = state.shape[1]
    g = SAMPLE_GROUP
    gr = g * dec_seq
    n_steps = dec_batch // g
    wk, wn = w_out.shape[1:]
    slab = wk // n_steps
    assert slab * n_steps == wk and slab % (2 * V7X_SUBLANES) == 0
    blocks = (_nbytes((gr, zc), BF16) + 2 * _nbytes((g, CONV_WIDTH - 1, TOKEN_WIDTH), F32)
              + 2 * _nbytes((g, N_MEM, V7X_SUBLANES, MEM_HEAD_DIM), F32) + _nbytes((gr, D_MODEL), BF16)
              + _nbytes((slab, wn), F32) + _nbytes((slab, wn), BF16))
    scratch = _nbytes((2 * CARRY_ROWS, TOKEN_WIDTH), F32) + _nbytes((gr, D_MODEL), F32)
    return pl.pallas_call(
        functools.partial(_conv_sample_kernel, dec_seq=dec_seq),
        grid=(n_steps,),
        in_specs=[
            pl.BlockSpec((gr, zc), lambda i: (i, 0)),
            pl.BlockSpec((None, g, CONV_WIDTH - 1, TOKEN_WIDTH), lambda i: (0, i, 0, 0)),
            pl.BlockSpec((None, g, N_MEM * MEM_HEADS, MEM_HEAD_DIM), lambda i: (layer, i, 0, 0)),
            pl.BlockSpec((None, g, N_MEM * MEM_HEADS, MEM_HEAD_DIM), lambda i: (layer, i, 0, 0)),
            pl.BlockSpec((None, CONV_WIDTH, TOKEN_WIDTH), lambda i: (0, 0, 0)),
            pl.BlockSpec((None, slab, wn), lambda i: (0, i, 0)),
        ],
        out_specs=[
            pl.BlockSpec((gr, D_MODEL), lambda i: (i, 0)),
            pl.BlockSpec((g, CONV_WIDTH - 1, TOKEN_WIDTH), lambda i: (i, 0, 0)),
            pl.BlockSpec((slab, wn), lambda i: (i, 0)),
        ],
        out_shape=[
            jax.ShapeDtypeStruct((rows, D_MODEL), BF16),
            jax.ShapeDtypeStruct((dec_batch, CONV_WIDTH - 1, TOKEN_WIDTH), F32),
            jax.ShapeDtypeStruct((wk, wn), BF16),
        ],
        scratch_shapes=[pltpu.VMEM((2 * CARRY_ROWS, TOKEN_WIDTH), F32), pltpu.VMEM((gr, D_MODEL), F32)],
        compiler_params=pltpu.CompilerParams(
            dimension_semantics=("arbitrary",),
            vmem_limit_bytes=_vmem_limit(blocks, scratch)),
        name="conv_sample",
    )(z, state, mem_k, mem_v, conv_w, w_out)


def _band_scores(q, k):
    return _dot_nt(q * (HEAD_DIM ** -0.5), k)


def _band_probs(s, sink, upper, upper_visible):
    s = jnp.where(upper_visible, s[:, :WINDOW], jnp.where(upper, -jnp.inf, s[:, WINDOW:]))
    m = jnp.maximum(jnp.max(s, axis=-1, keepdims=True), sink)
    e = jnp.exp(s - m)
    e = jnp.concatenate([jnp.where(upper, e, 0.0), jnp.where(upper, 0.0, e)], axis=1).astype(BF16)
    return e, jnp.exp(sink - m)


def _band_values(probs, v, ones):
    e, sink_term = probs
    return _dot(e, v) / (_dot(e, ones) + sink_term)


def _swa_prompt_kernel(sink_ref, zq_ref, zp_ref, mk_ref, mv_ref, *refs, blocks_per_seq, cast_steps):
    n_w = (len(refs) - 1) // 2
    w_refs, mix_ref, wb_refs = refs[:n_w], refs[n_w], refs[n_w + 1:]
    s = pl.program_id(0)

    for w_ref, wb_ref, n_cast in zip(w_refs, wb_refs, cast_steps):
        @pl.when(s < n_cast)
        def _():
            wb_ref[...] = w_ref[...].astype(BF16)

    k_off = TOKEN_WIDTH
    v_off = TOKEN_WIDTH + KV_WIDTH
    qm_off = TOKEN_WIDTH + 2 * KV_WIDTH
    n_blocks = zq_ref.shape[0] // WINDOW
    first_has_prev = (s % (blocks_per_seq // n_blocks)) > 0
    row = lax.broadcasted_iota(jnp.int32, (WINDOW, WINDOW), 0)
    col = lax.broadcasted_iota(jnp.int32, (WINDOW, WINDOW), 1)
    upper = col > row
    upper_first = jnp.logical_and(upper, first_has_prev)
    ones = jnp.ones((2 * WINDOW, HEAD_DIM), BF16)

    def block_rows(b):
        return slice(b * WINDOW, (b + 1) * WINDOW)

    @functools.lru_cache(maxsize=None)
    def window(b, kh, col_prev, col_cur):
        lo, hi = kh * HEAD_DIM, (kh + 1) * HEAD_DIM
        prev = (zp_ref[:, col_prev + lo:col_prev + hi] if b == 0
                else zq_ref[block_rows(b - 1), col_cur + lo:col_cur + hi])
        return jnp.concatenate([prev, zq_ref[block_rows(b), col_cur + lo:col_cur + hi]], axis=0)

    def band_scores(i):
        b, h = divmod(i, N_HEADS)
        return _band_scores(zq_ref[block_rows(b), h * HEAD_DIM:(h + 1) * HEAD_DIM], window(b, h // GROUP, 0, k_off))

    def band_finish(i, s):
        b, h = divmod(i, N_HEADS)
        p = _band_probs(s, sink_ref[h], upper, upper_first if b == 0 else upper)
        o = _band_values(p, window(b, h // GROUP, KV_WIDTH, v_off), ones)
        mix_ref[block_rows(b), h * HEAD_DIM:(h + 1) * HEAD_DIM] = o.astype(BF16)

    def cross_scores(h):
        lo, hi = h * MEM_HEAD_DIM, (h + 1) * MEM_HEAD_DIM
        return _cross_scores(zq_ref[:, qm_off + lo:qm_off + hi], mk_ref[pl.ds(h, N_MEM, stride=MEM_HEADS), :].astype(BF16))

    def cross_finish(h, s):
        lo, hi = h * MEM_HEAD_DIM, (h + 1) * MEM_HEAD_DIM
        mix_ref[:, TOKEN_WIDTH + lo:TOKEN_WIDTH + hi] = _cross_values(
            s, mv_ref[pl.ds(h, N_MEM, stride=MEM_HEADS), :].astype(BF16)).astype(BF16)

    _run_ahead(n_blocks * N_HEADS, SCORE_LOOKAHEAD, band_scores, band_finish)
    _run_ahead(MEM_HEADS, SCORE_LOOKAHEAD, cross_scores, cross_finish)


def _swa_prompt(z, mk, mv, sinks, weights, *, layer, batch, seq):
    rows, zc = z.shape
    tq = SWA_ROWS
    blocks_per_seq = seq // WINDOW
    steps_per_seq = seq // tq
    blocks_per_step = tq // WINDOW
    n_steps = batch * steps_per_seq
    kv_col_block = TOKEN_WIDTH // (2 * KV_WIDTH)
    bf16_rows = 2 * V7X_SUBLANES
    blocks = (_nbytes((tq, zc), BF16) + _nbytes((WINDOW, 2 * KV_WIDTH), BF16)
              + 2 * _nbytes((N_MEM, MEM_WIDTH), F32) + _nbytes((tq, D_MODEL), BF16))
    w_in_specs, w_out_specs, w_out_shapes, cast_steps = [], [], [], []
    for w in weights:
        r, c = w.shape[1:]
        n_cast = n_steps if r % (n_steps * bf16_rows) == 0 else n_steps // 2
        slab = r // n_cast
        assert slab * n_cast == r and slab % bf16_rows == 0
        blocks += _nbytes((slab, c), F32) + _nbytes((slab, c), BF16)
        w_in_specs.append(pl.BlockSpec((None, slab, c), lambda s, n=n_cast: (layer, jnp.minimum(s, n - 1), 0)))
        w_out_specs.append(pl.BlockSpec((slab, c), lambda s, n=n_cast: (jnp.minimum(s, n - 1), 0)))
        w_out_shapes.append(jax.ShapeDtypeStruct((r, c), BF16))
        cast_steps.append(n_cast)
    return pl.pallas_call(
        functools.partial(_swa_prompt_kernel, blocks_per_seq=blocks_per_seq, cast_steps=tuple(cast_steps)),
        grid=(n_steps,),
        in_specs=[
            pl.BlockSpec(memory_space=pltpu.SMEM),
            pl.BlockSpec((tq, zc), lambda s: (s, 0)),
            pl.BlockSpec((WINDOW, 2 * KV_WIDTH), lambda s: (jnp.maximum(s * blocks_per_step - 1, 0), kv_col_block)),
            pl.BlockSpec((None, N_MEM * MEM_HEADS, MEM_HEAD_DIM), lambda s: (layer, s // steps_per_seq, 0)),
            pl.BlockSpec((None, N_MEM * MEM_HEADS, MEM_HEAD_DIM), lambda s: (layer, s // steps_per_seq, 0)),
        ] + w_in_specs,
        out_specs=[pl.BlockSpec((tq, D_MODEL), lambda s: (s, 0))] + w_out_specs,
        out_shape=[jax.ShapeDtypeStruct((rows, D_MODEL), BF16)] + w_out_shapes,
        compiler_params=pltpu.CompilerParams(
            dimension_semantics=("arbitrary",),
            vmem_limit_bytes=_vmem_limit(blocks, 0)),
        name="swa_prompt",
    )(sinks, z, z, mk, mv, *weights)


def _swa_sample_kernel(sink_ref, z_ref, ck_ref, cv_ref, mk_ref, mv_ref, wo_ref,
                       mix_ref, nk_ref, nv_ref, wob_ref, knew_ref, vnew_ref, mixf_ref, *, dec_seq):
    t = dec_seq
    wob_ref[...] = wo_ref[...].astype(BF16)
    k_off = TOKEN_WIDTH
    v_off = TOKEN_WIDTH + KV_WIDTH
    qm_off = TOKEN_WIDTH + 2 * KV_WIDTH
    rows = GROUP * t
    qi = lax.broadcasted_iota(jnp.int32, (rows, WINDOW), 0) % t
    col = lax.broadcasted_iota(jnp.int32, (rows, WINDOW), 1)
    upper = col > qi
    knew_ref[...] = jnp.zeros(knew_ref.shape, F32)
    vnew_ref[...] = jnp.zeros(vnew_ref.shape, F32)
    for n in range(SAMPLE_GROUP):
        r0, r1 = n * t, (n + 1) * t
        k_new = z_ref[r0:r1, k_off:k_off + KV_WIDTH].astype(F32)
        v_new = z_ref[r0:r1, v_off:v_off + KV_WIDTH].astype(F32)
        knew_ref[n, 0:t, :] = k_new
        vnew_ref[n, 0:t, :] = v_new
        nk_ref[n, 0:WINDOW - t, :] = ck_ref[n, t:WINDOW, :]
        nv_ref[n, 0:WINDOW - t, :] = cv_ref[n, t:WINDOW, :]
        nk_ref[n, WINDOW - t:WINDOW, :] = k_new
        nv_ref[n, WINDOW - t:WINDOW, :] = v_new

    def scores(i):
        n, kh = divmod(i, N_KV_HEADS)
        lo, hi = kh * HEAD_DIM, (kh + 1) * HEAD_DIM
        k = jnp.concatenate([ck_ref[n, :, lo:hi], knew_ref[n, :, lo:hi]], axis=0).astype(BF16)
        q = jnp.concatenate(
            [z_ref[n * t:(n + 1) * t, (kh * GROUP + g) * HEAD_DIM:(kh * GROUP + g + 1) * HEAD_DIM].astype(F32)
             for g in range(GROUP)], axis=0).astype(BF16)
        return _band_scores(q, k)

    def finish(i, s):
        n, kh = divmod(i, N_KV_HEADS)
        lo, hi = kh * HEAD_DIM, (kh + 1) * HEAD_DIM
        v = jnp.concatenate([cv_ref[n, :, lo:hi], vnew_ref[n, :, lo:hi]], axis=0).astype(BF16)
        sink = jnp.concatenate(
            [jnp.full((t, 1), sink_ref[kh * GROUP + g], F32) for g in range(GROUP)], axis=0)
        o = _band_values(_band_probs(s, sink, upper, upper), v, jnp.ones((2 * WINDOW, HEAD_DIM), BF16))
        for g in range(GROUP):
            h = kh * GROUP + g
            mixf_ref[n * t:(n + 1) * t, h * HEAD_DIM:(h + 1) * HEAD_DIM] = o[g * t:(g + 1) * t, :]

    _run_ahead(SAMPLE_GROUP * N_KV_HEADS, SAMPLE_LOOKAHEAD, scores, finish)
    _sample_cross_attention(z_ref, mk_ref, mv_ref, mixf_ref, qm_off, t)
    mix_ref[...] = mixf_ref[...].astype(BF16)


def _swa_sample(z, cache_k, cache_v, mem_k, mem_v, sinks, w_out, *, layer, dec_seq):
    rows, zc = z.shape
    dec_batch = cache_k.shape[0]
    g = SAMPLE_GROUP
    gr = g * dec_seq
    n_steps = dec_batch // g
    wk, wn = w_out.shape[1:]
    slab = wk // n_steps
    assert slab * n_steps == wk and slab % (2 * V7X_SUBLANES) == 0
    win = jax.ShapeDtypeStruct((dec_batch, WINDOW, KV_WIDTH), F32)
    blocks = (_nbytes((gr, zc), BF16) + 4 * _nbytes((g, WINDOW, KV_WIDTH), F32)
              + 2 * _nbytes((g, N_MEM, V7X_SUBLANES, MEM_HEAD_DIM), F32) + _nbytes((gr, D_MODEL), BF16)
              + _nbytes((slab, wn), F32) + _nbytes((slab, wn), BF16))
    scratch = 2 * _nbytes((g, WINDOW, KV_WIDTH), F32) + _nbytes((gr, D_MODEL), F32)
    return pl.pallas_call(
        functools.partial(_swa_sample_kernel, dec_seq=dec_seq),
        grid=(n_steps,),
        in_specs=[
            pl.BlockSpec(memory_space=pltpu.SMEM),
            pl.BlockSpec((gr, zc), lambda i: (i, 0)),
            pl.BlockSpec((g, WINDOW, KV_WIDTH), lambda i: (i, 0, 0)),
            pl.BlockSpec((g, WINDOW, KV_WIDTH), lambda i: (i, 0, 0)),
            pl.BlockSpec((None, g, N_MEM * MEM_HEADS, MEM_HEAD_DIM), lambda i: (layer, i, 0, 0)),
            pl.BlockSpec((None, g, N_MEM * MEM_HEADS, MEM_HEAD_DIM), lambda i: (layer, i, 0, 0)),
            pl.BlockSpec((None, slab, wn), lambda i: (0, i, 0)),
        ],
        out_specs=[
            pl.BlockSpec((gr, D_MODEL), lambda i: (i, 0)),
            pl.BlockSpec((g, WINDOW, KV_WIDTH), lambda i: (i, 0, 0)),
            pl.BlockSpec((g, WINDOW, KV_WIDTH), lambda i: (i, 0, 0)),
            pl.BlockSpec((slab, wn), lambda i: (i, 0)),
        ],
        out_shape=[jax.ShapeDtypeStruct((rows, D_MODEL), BF16), win, win,
                   jax.ShapeDtypeStruct((wk, wn), BF16)],
        scratch_shapes=[pltpu.VMEM((g, WINDOW, KV_WIDTH), F32), pltpu.VMEM((g, WINDOW, KV_WIDTH), F32),
                        pltpu.VMEM((gr, D_MODEL), F32)],
        compiler_params=pltpu.CompilerParams(
            dimension_semantics=("arbitrary",),
            vmem_limit_bytes=_vmem_limit(blocks, scratch)),
        name="swa_sample",
    )(sinks, z, cache_k, cache_v, mem_k, mem_v, w_out)


def kernel(x_prompt, x_sample, mem_prompt, state_conv, cache_win_k, cache_win_v, cache_mem_k, cache_mem_v,
           norm_mix, norm_mem, w_mem_kv, norm_ffn, w_gate, w_up, w_down,
           conv_w_in, conv_w, conv_w_out, attn_w_in, attn_sinks, attn_w_out, norm_final):
    batch, seq, d = x_prompt.shape
    dec_batch, dec_seq, _ = x_sample.shape
    depth = norm_mix.shape[0]
    d_ff = w_gate.shape[2]
    prompt_rows = batch * seq
    sample_rows = dec_batch * dec_seq
    assert d == D_MODEL and depth == 2 and seq % CONV_ROWS == 0 and seq % SWA_ROWS == 0 and SWA_ROWS % WINDOW == 0
    assert prompt_rows % PROMPT_TILE == 0 and dec_batch % SAMPLE_GROUP == 0
    assert dec_seq == V7X_SUBLANES and d_ff % FFN_COL_TILE == 0 and d % COL_TILE == 0
    assert d_ff % DOWN_K_TILE == 0 and d % DOWN_X_CHUNK == 0
    assert prompt_rows % FFN_ROW_TILE == 0 and prompt_rows % sample_rows == 0

    xp = x_prompt.reshape(prompt_rows, d)
    xs = x_sample.reshape(sample_rows, d)
    mem = mem_prompt.reshape(batch * N_MEM, d)
    mem_k_s = cache_mem_k.reshape(depth, dec_batch, N_MEM * MEM_HEADS, MEM_HEAD_DIM)
    mem_v_s = cache_mem_v.reshape(depth, dec_batch, N_MEM * MEM_HEADS, MEM_HEAD_DIM)
    g_mix = norm_mix.reshape(depth, 1, d)
    g_ffn = norm_ffn.reshape(depth, 1, d)

    mk, mv = _mem_kv(mem, norm_mem.reshape(depth, 1, d), w_mem_kv)

    hp, hs = _norm_rows(xp, xs, g_mix, layer=0)
    zp, zs, wd_bf = _matmul(hp, hs, conv_w_in, w_layer=0, side=w_down, side_layer=0)
    mix_s, conv_s, wo_bf = _conv_sample(zs, state_conv, mem_k_s, mem_v_s, conv_w, conv_w_out,
                                        layer=0, dec_seq=dec_seq)
    mix_p, conv_p = _conv_prompt(zp, mk, mv, conv_w, layer=0, batch=batch, seq=seq)
    xp, xs, hp, hs = _out_proj(mix_p, mix_s, wo_bf, xp, xs, g_ffn, layer=0)
    ap, as_ = _ffn_up(hp, hs, w_gate, w_up, layer=0)
    xp, xs, hp, hs = _ffn_down(ap, as_, wd_bf[None], xp, xs, g_mix, w_layer=0, g_layer=1, final=False)

    zp, zs = _matmul(hp, hs, attn_w_in, w_layer=0)
    sinks = attn_sinks[0]
    mix_s, win_k_s, win_v_s, wo_bf = _swa_sample(
        zs, cache_win_k[0].reshape(dec_batch, WINDOW, KV_WIDTH), cache_win_v[0].reshape(dec_batch, WINDOW, KV_WIDTH),
        mem_k_s, mem_v_s, sinks, attn_w_out, layer=1, dec_seq=dec_seq)
    mix_p, wd_bf = _swa_prompt(zp, mk, mv, sinks, [w_down], layer=1, batch=batch, seq=seq)
    xp, xs, hp, hs = _out_proj(mix_p, mix_s, wo_bf, xp, xs, g_ffn, layer=1)
    ap, as_ = _ffn_up(hp, hs, w_gate, w_up, layer=1)
    y_prompt, y_sample = _ffn_down(ap, as_, wd_bf[None], xp, xs, norm_final.reshape(1, 1, d),
                                   w_layer=0, g_layer=0, final=True)

    win_p = zp.reshape(batch, seq, -1)[:, seq - WINDOW:, TOKEN_WIDTH:TOKEN_WIDTH + 2 * KV_WIDTH].astype(F32)
    kv_shape = (1, -1, WINDOW, N_KV_HEADS, HEAD_DIM)
    mem_shape = (depth, batch, N_MEM, MEM_HEADS, MEM_HEAD_DIM)
    return (y_prompt.reshape(batch, seq, d),
            y_sample.reshape(dec_batch, dec_seq, d),
            conv_p[None],
            conv_s[None],
            win_p[..., :KV_WIDTH].reshape(kv_shape),
            win_p[..., KV_WIDTH:].reshape(kv_shape),
            win_k_s.reshape(kv_shape),
            win_v_s.reshape(kv_shape),
            mk.reshape(mem_shape),
            mv.reshape(mem_shape))
```

```python
import functools

import jax
import jax.numpy as jnp
from jax import lax
from jax.experimental import pallas as pl
from jax.experimental.pallas import tpu as pltpu

F32 = jnp.float32
BF16 = jnp.bfloat16

D_MODEL = 2048
N_MEM = 256
MEM_HEADS = 4
MEM_WIDTH = D_MODEL // 4
MEM_HEAD_DIM = MEM_WIDTH // MEM_HEADS
TOKEN_WIDTH = D_MODEL - MEM_WIDTH
CONV_WIDTH = 3
WINDOW = 128
HEAD_DIM = 64
N_HEADS = TOKEN_WIDTH // HEAD_DIM
N_KV_HEADS = 4
GROUP = N_HEADS // N_KV_HEADS
KV_WIDTH = N_KV_HEADS * HEAD_DIM
EPS = 1e-6

V7X_VMEM_BYTES = 64 * 1024 * 1024
V7X_SUBLANES = 8
V7X_MXU_DEPTH = 256

PROMPT_TILE = 1024
COL_TILE = 512
DOWN_K_TILE = 1408
DOWN_X_CHUNK = 512
FFN_ROW_TILE = 2048
FFN_COL_TILE = 512
MXU_ROWS = 1024
OUT_ROWS = 512
CONV_ROWS = 512
SWA_ROWS = 512
SCORE_LOOKAHEAD = 5
SAMPLE_LOOKAHEAD = 4
SAMPLE_GROUP = 8
SIDE_CAST_STEPS = 32
CARRY_ROWS = V7X_SUBLANES
MIN_TEMP_BYTES = 8 << 20
VMEM_RESERVE_BYTES = 6 << 20


def _nbytes(shape, dtype):
    n = 1
    for s in shape:
        n *= s
    return n * jnp.dtype(dtype).itemsize


def _vmem_limit(block_bytes, scratch_bytes):
    need = 2 * block_bytes + scratch_bytes
    temporaries = max(need // 4, MIN_TEMP_BYTES)
    return int(min(need + temporaries, V7X_VMEM_BYTES - VMEM_RESERVE_BYTES))


def _rmsnorm(x, g):
    r = lax.rsqrt(jnp.mean(x * x, axis=-1, keepdims=True) + EPS)
    return (x * r) * g


def _dot(a, b):
    return jnp.dot(a, b, preferred_element_type=F32)


def _dot_nt(a, b):
    return lax.dot_general(a, b, (((1,), (1,)), ((), ())), preferred_element_type=F32)


def _prompt_rows_map(i, j):
    return (i, 0)


def _prompt_tile_map(i, j):
    return (i, j)


def _sample_tile_map(n_pt):
    return lambda i, j: (0, jnp.where(i == n_pt - 1, j, 0))


def _const_map(i, j):
    return (0, 0)


def _norm_rows_kernel(xp_ref, xs_ref, g_ref, hp_ref, hs_ref, *, n_pt):
    hp_ref[...] = _rmsnorm(xp_ref[...], g_ref[...]).astype(BF16)

    @pl.when(pl.program_id(0) == n_pt - 1)
    def _():
        hs_ref[...] = _rmsnorm(xs_ref[...], g_ref[...]).astype(BF16)


def _norm_rows(xp, xs, g, *, layer):
    rp, d = xp.shape
    rs = xs.shape[0]
    tm = OUT_ROWS
    n_pt = rp // tm
    blocks = _nbytes((tm, d), F32) + _nbytes((tm, d), BF16) + _nbytes((rs, d), F32) + _nbytes((rs, d), BF16)
    return pl.pallas_call(
        functools.partial(_norm_rows_kernel, n_pt=n_pt),
        grid=(n_pt,),
        in_specs=[
            pl.BlockSpec((tm, d), lambda s: (s, 0)),
            pl.BlockSpec((rs, d), lambda s: (0, 0)),
            pl.BlockSpec((None, 1, d), lambda s: (layer, 0, 0)),
        ],
        out_specs=[
            pl.BlockSpec((tm, d), lambda s: (s, 0)),
            pl.BlockSpec((rs, d), lambda s: (0, 0)),
        ],
        out_shape=[jax.ShapeDtypeStruct((rp, d), BF16), jax.ShapeDtypeStruct((rs, d), BF16)],
        compiler_params=pltpu.CompilerParams(
            dimension_semantics=("arbitrary",),
            vmem_limit_bytes=_vmem_limit(blocks, 0)),
        name="norm_rows",
    )(xp, xs, g)


def _swiglu(h, wg, wu):
    gate = _dot(h, wg)
    up = _dot(h, wu)
    return (gate * jax.nn.sigmoid(gate) * up).astype(BF16)


def _ffn_up_kernel(hp_ref, hs_ref, wg_ref, wu_ref, ap_ref, as_ref, *, n_pt):
    wg, wu = wg_ref[...].astype(BF16), wu_ref[...].astype(BF16)
    for r in range(0, hp_ref.shape[0], MXU_ROWS):
        ap_ref[r:r + MXU_ROWS, :] = _swiglu(hp_ref[r:r + MXU_ROWS, :], wg, wu)

    @pl.when(pl.program_id(0) == n_pt - 1)
    def _():
        as_ref[...] = _swiglu(hs_ref[...], wg_ref[...].astype(BF16), wu_ref[...].astype(BF16))


def _ffn_up(hp, hs, wg, wu, *, layer):
    rp, d = hp.shape
    rs = hs.shape[0]
    n = wg.shape[2]
    tm, tn = FFN_ROW_TILE, FFN_COL_TILE
    n_pt, n_j = rp // tm, n // tn
    blocks = (_nbytes((tm, d), BF16) + _nbytes((rs, d), BF16) + 2 * _nbytes((d, tn), wg.dtype)
              + _nbytes((tm, tn), BF16) + _nbytes((rs, tn), BF16))
    temps = 2 * _nbytes((d, tn), BF16) + 3 * _nbytes((tm, tn), F32)
    w_spec = pl.BlockSpec((None, d, tn), lambda i, j: (layer, 0, j))
    return pl.pallas_call(
        functools.partial(_ffn_up_kernel, n_pt=n_pt),
        grid=(n_pt, n_j),
        in_specs=[
            pl.BlockSpec((tm, d), _prompt_rows_map),
            pl.BlockSpec((rs, d), _const_map),
            w_spec,
            w_spec,
        ],
        out_specs=[
            pl.BlockSpec((tm, tn), _prompt_tile_map),
            pl.BlockSpec((rs, tn), _sample_tile_map(n_pt)),
        ],
        out_shape=[jax.ShapeDtypeStruct((rp, n), BF16), jax.ShapeDtypeStruct((rs, n), BF16)],
        compiler_params=pltpu.CompilerParams(
            dimension_semantics=("arbitrary", "arbitrary"),
            vmem_limit_bytes=_vmem_limit(blocks, temps)),
        name="ffn_up",
    )(hp, hs, wg, wu)


def _out_proj_kernel(ap_ref, as_ref, w_ref, xp_ref, xs_ref, g_ref, op_ref, os_ref, hp_ref, hs_ref, *, n_pt):
    x = xp_ref[...] + _dot(ap_ref[...], w_ref[...])
    op_ref[...] = x
    hp_ref[...] = _rmsnorm(x, g_ref[...]).astype(BF16)

    @pl.when(pl.program_id(0) == n_pt - 1)
    def _():
        x = xs_ref[...] + _dot(as_ref[...], w_ref[...])
        os_ref[...] = x
        hs_ref[...] = _rmsnorm(x, g_ref[...]).astype(BF16)


def _out_proj(ap, as_, w, xp, xs, g, *, layer):
    rp, k = ap.shape
    rs = as_.shape[0]
    n = w.shape[1]
    tm = OUT_ROWS
    n_pt = rp // tm
    rows_map = lambda s: (s, 0)
    const = lambda s: (0, 0)
    once = dict(pipeline_mode=pl.Buffered(1))
    blocks = _nbytes((tm, k), BF16) + 2 * _nbytes((tm, n), F32) + _nbytes((tm, n), BF16)
    resident = (_nbytes((k, n), BF16) + _nbytes((rs, k), BF16) + _nbytes((rs, n), F32)
                + 2 * (_nbytes((rs, n), F32) + _nbytes((rs, n), BF16)))
    return pl.pallas_call(
        functools.partial(_out_proj_kernel, n_pt=n_pt),
        grid=(n_pt,),
        in_specs=[
            pl.BlockSpec((tm, k), rows_map),
            pl.BlockSpec((rs, k), const, **once),
            pl.BlockSpec((k, n), const, **once),
            pl.BlockSpec((tm, n), rows_map),
            pl.BlockSpec((rs, n), const, **once),
            pl.BlockSpec((None, 1, n), lambda s: (layer, 0, 0)),
        ],
        out_specs=[
            pl.BlockSpec((tm, n), rows_map),
            pl.BlockSpec((rs, n), const),
            pl.BlockSpec((tm, n), rows_map),
            pl.BlockSpec((rs, n), const),
        ],
        out_shape=[jax.ShapeDtypeStruct((rp, n), F32), jax.ShapeDtypeStruct((rs, n), F32),
                   jax.ShapeDtypeStruct((rp, n), BF16), jax.ShapeDtypeStruct((rs, n), BF16)],
        compiler_params=pltpu.CompilerParams(
            dimension_semantics=("arbitrary",),
            vmem_limit_bytes=_vmem_limit(blocks, resident + 2 * _nbytes((tm, n), F32))),
        name="out_proj",
    )(ap, as_, w, xp, xs, g)


def _matmul_kernel(hp_ref, hs_ref, w_ref, *refs, n_pt, n_j, n_cast):
    if n_cast:
        side_ref, zp_ref, zs_ref, side_bf_ref = refs

        @pl.when(pl.program_id(0) * n_j + pl.program_id(1) < n_cast)
        def _():
            side_bf_ref[...] = side_ref[...].astype(BF16)
    else:
        zp_ref, zs_ref = refs

    w = w_ref[...].astype(BF16)
    for r in range(0, hp_ref.shape[0], MXU_ROWS):
        zp_ref[r:r + MXU_ROWS, :] = _dot(hp_ref[r:r + MXU_ROWS, :], w).astype(zp_ref.dtype)

    @pl.when(pl.program_id(0) == n_pt - 1)
    def _():
        zs_ref[...] = _dot(hs_ref[...], w_ref[...].astype(BF16)).astype(zs_ref.dtype)


def _matmul(hp, hs, w, *, w_layer, side=None, side_layer=0):
    rp, d = hp.shape
    rs = hs.shape[0]
    n = w.shape[2]
    tm, tn = FFN_ROW_TILE, COL_TILE
    n_pt, n_j = rp // tm, n // tn
    blocks = (_nbytes((tm, d), BF16) + _nbytes((rs, d), BF16) + _nbytes((d, tn), F32)
              + _nbytes((tm, tn), BF16) + _nbytes((rs, tn), BF16))
    temps = _nbytes((d, tn), BF16) + _nbytes((tm, tn), F32)
    in_specs = [
        pl.BlockSpec((tm, d), _prompt_rows_map),
        pl.BlockSpec((rs, d), _const_map),
        pl.BlockSpec((None, d, tn), lambda i, j: (w_layer, 0, j)),
    ]
    out_specs = [
        pl.BlockSpec((tm, tn), _prompt_tile_map),
        pl.BlockSpec((rs, tn), _sample_tile_map(n_pt)),
    ]
    out_shape = [jax.ShapeDtypeStruct((rp, n), BF16), jax.ShapeDtypeStruct((rs, n), BF16)]
    operands = [hp, hs, w]
    n_cast = 0
    if side is not None:
        r, c = side.shape[1:]
        n_cast = SIDE_CAST_STEPS if n_pt * n_j >= SIDE_CAST_STEPS else SIDE_CAST_STEPS // 2
        slab = r // n_cast
        assert n_cast <= n_pt * n_j and slab * n_cast == r and slab % (2 * V7X_SUBLANES) == 0
        slab_idx = lambda i, j: jnp.minimum(i * n_j + j, n_cast - 1)
        in_specs.append(pl.BlockSpec((None, slab, c), lambda i, j: (side_layer, slab_idx(i, j), 0)))
        out_specs.append(pl.BlockSpec((slab, c), lambda i, j: (slab_idx(i, j), 0)))
        out_shape.append(jax.ShapeDtypeStruct((r, c), BF16))
        operands.append(side)
        blocks += _nbytes((slab, c), F32) + _nbytes((slab, c), BF16)
    return pl.pallas_call(
        functools.partial(_matmul_kernel, n_pt=n_pt, n_j=n_j, n_cast=n_cast),
        grid=(n_pt, n_j),
        in_specs=in_specs,
        out_specs=out_specs,
        out_shape=out_shape,
        compiler_params=pltpu.CompilerParams(
            dimension_semantics=("arbitrary", "arbitrary"),
            vmem_limit_bytes=_vmem_limit(blocks, temps)),
        name="matmul",
    )(*operands)


def _ffn_down_kernel(ap_ref, as_ref, w_ref, xp_ref, xs_ref, g_ref, *refs, n_pt, n_k, n_xc, final):
    i, k = pl.program_id(0), pl.program_id(1)
    if final:
        op_ref, os_ref, apl_ref, asl_ref, wl_ref = refs
    else:
        op_ref, os_ref, hp_ref, hs_ref, apl_ref, asl_ref, wl_ref = refs
    xc = xp_ref.shape[1]
    kf = (ap_ref.shape[1] // V7X_MXU_DEPTH) * V7X_MXU_DEPTH
    even = k % 2 == 0

    def whole_passes(a_ref):
        return _dot(a_ref[:, :kf], w_ref[:kf, :])

    def with_stash(a_ref, al_ref):
        return _dot(jnp.concatenate([al_ref[...], a_ref[...]], axis=1),
                    jnp.concatenate([wl_ref[...], w_ref[...]], axis=0))

    for kk in range(n_k):
        @pl.when(k == kk)
        def _():
            cols = slice(kk * xc, (kk + 1) * xc)
            if 0 < kk < n_xc:
                op_ref[:, cols] = op_ref[:, cols] + xp_ref[...]
            contrib = whole_passes(ap_ref) if kk % 2 == 0 else with_stash(ap_ref, apl_ref)
            if kk == 0:
                op_ref[...] = contrib
                op_ref[:, cols] = op_ref[:, cols] + xp_ref[...]
            else:
                op_ref[...] = op_ref[...] + contrib
            if kk == n_k - 1:
                if final:
                    op_ref[...] = _rmsnorm(op_ref[...], g_ref[...])
                else:
                    hp_ref[...] = _rmsnorm(op_ref[...], g_ref[...]).astype(BF16)

    @pl.when(i == n_pt - 1)
    def _():
        @pl.when(k == 0)
        def _():
            os_ref[...] = xs_ref[...] + whole_passes(as_ref)

        @pl.when(jnp.logical_and(k > 0, even))
        def _():
            os_ref[...] = os_ref[...] + whole_passes(as_ref)

        @pl.when(jnp.logical_not(even))
        def _():
            os_ref[...] = os_ref[...] + with_stash(as_ref, asl_ref)

        @pl.when(even)
        def _():
            asl_ref[...] = as_ref[:, kf:]

        @pl.when(k == n_k - 1)
        def _():
            if final:
                os_ref[...] = _rmsnorm(os_ref[...], g_ref[...])
            else:
                hs_ref[...] = _rmsnorm(os_ref[...], g_ref[...]).astype(BF16)

    @pl.when(even)
    def _():
        apl_ref[...] = ap_ref[:, kf:]
        wl_ref[...] = w_ref[kf:, :]


def _ffn_down(ap, as_, w, xp, xs, g, *, w_layer, g_layer, final):
    rp, kdim = ap.shape
    rs = as_.shape[0]
    n = w.shape[2]
    tm, tk, xc = PROMPT_TILE, DOWN_K_TILE, DOWN_X_CHUNK
    n_pt, n_k, n_xc = rp // tm, kdim // tk, n // xc
    k_left = tk % V7X_MXU_DEPTH
    assert n_xc <= n_k and n_k % 2 == 0 and 2 * k_left == V7X_MXU_DEPTH and w.dtype == BF16
    rows = lambda i, k: (i, 0)
    blocks = (_nbytes((tm, tk), BF16) + _nbytes((rs, tk), BF16) + _nbytes((tk, n), w.dtype)
              + _nbytes((tm, xc), F32) + _nbytes((rs, n), F32)
              + _nbytes((tm, n), F32) + _nbytes((rs, n), F32))
    out_specs = [pl.BlockSpec((tm, n), rows), pl.BlockSpec((rs, n), _const_map)]
    out_shape = [jax.ShapeDtypeStruct((rp, n), F32), jax.ShapeDtypeStruct((rs, n), F32)]
    if not final:
        blocks += _nbytes((tm, n), BF16) + _nbytes((rs, n), BF16)
        out_specs += [pl.BlockSpec((tm, n), rows), pl.BlockSpec((rs, n), _const_map)]
        out_shape += [jax.ShapeDtypeStruct((rp, n), BF16), jax.ShapeDtypeStruct((rs, n), BF16)]
    return pl.pallas_call(
        functools.partial(_ffn_down_kernel, n_pt=n_pt, n_k=n_k, n_xc=n_xc, final=final),
        grid=(n_pt, n_k),
        in_specs=[
            pl.BlockSpec((tm, tk), lambda i, k: (i, k)),
            pl.BlockSpec((rs, tk), lambda i, k: (0, jnp.where(i == n_pt - 1, k, 0))),
            pl.BlockSpec((None, tk, n), lambda i, k: (w_layer, k, 0)),
            pl.BlockSpec((tm, xc), lambda i, k: (i, jnp.minimum(k, n_xc - 1))),
            pl.BlockSpec((rs, n), _const_map),
            pl.BlockSpec((None, 1, n), lambda i, k: (g_layer, 0, 0)),
        ],
        out_specs=out_specs,
        out_shape=out_shape,
        scratch_shapes=[pltpu.VMEM((tm, k_left), BF16), pltpu.VMEM((rs, k_left), BF16),
                        pltpu.VMEM((k_left, n), BF16)],
        compiler_params=pltpu.CompilerParams(
            dimension_semantics=("arbitrary", "arbitrary"),
            vmem_limit_bytes=_vmem_limit(blocks, _nbytes((tk, n), BF16))),
        name="ffn_down",
    )(ap, as_, w, xp, xs, g)


def _mem_kv_kernel(x_ref, g_ref, w_ref, k_ref, v_ref):
    h = _rmsnorm(x_ref[...], g_ref[...]).astype(BF16)
    kv = _dot(h, w_ref[...].astype(BF16))
    tm = x_ref.shape[0]
    for hd in range(MEM_HEADS):
        head_rows = pl.ds(hd, tm, stride=MEM_HEADS)
        k_ref[head_rows, :] = kv[:, hd * MEM_HEAD_DIM:(hd + 1) * MEM_HEAD_DIM]
        v_ref[head_rows, :] = kv[:, MEM_WIDTH + hd * MEM_HEAD_DIM:MEM_WIDTH + (hd + 1) * MEM_HEAD_DIM]


def _mem_kv(mem, g, w):
    rows, d = mem.shape
    depth = w.shape[0]
    tm = 512
    out = jax.ShapeDtypeStruct((depth, rows * MEM_HEADS, MEM_HEAD_DIM), F32)
    blocks = (_nbytes((tm, d), F32) + _nbytes((d, 2 * MEM_WIDTH), F32) + 2 * _nbytes((tm, MEM_WIDTH), F32))
    return pl.pallas_call(
        _mem_kv_kernel,
        grid=(depth, rows // tm),
        in_specs=[
            pl.BlockSpec((tm, d), lambda l, i: (i, 0)),
            pl.BlockSpec((None, 1, d), lambda l, i: (l, 0, 0)),
            pl.BlockSpec((None, d, 2 * MEM_WIDTH), lambda l, i: (l, 0, 0)),
        ],
        out_specs=[
            pl.BlockSpec((None, tm * MEM_HEADS, MEM_HEAD_DIM), lambda l, i: (l, i, 0)),
            pl.BlockSpec((None, tm * MEM_HEADS, MEM_HEAD_DIM), lambda l, i: (l, i, 0)),
        ],
        out_shape=[out, out],
        compiler_params=pltpu.CompilerParams(
            dimension_semantics=("arbitrary", "arbitrary"),
            vmem_limit_bytes=_vmem_limit(blocks, _nbytes((d, 2 * MEM_WIDTH), BF16))),
        name="mem_kv",
    )(mem, g, w)


def _cross_scores(q, k):
    return _dot_nt(q, k) * (MEM_HEAD_DIM ** -0.5)


def _cross_values(s, v):
    e = jnp.exp(s - jnp.max(s, axis=-1, keepdims=True)).astype(BF16)
    return _dot(e, v) / _dot(e, jnp.ones(v.shape, BF16))


def _run_ahead(n_items, lookahead, first, second):
    pending = {}
    for i in range(n_items + lookahead):
        if i < n_items:
            pending[i] = first(i)
        if i >= lookahead:
            second(i - lookahead, pending.pop(i - lookahead))


def _conv_prompt_kernel(z_ref, mk_ref, mv_ref, cw_ref, mix_ref, st_ref, ext_ref, *, tiles_per_seq):
    s = pl.program_id(0)
    tq = z_ref.shape[0]

    @pl.when(s % tiles_per_seq == 0)
    def _():
        ext_ref[0:CARRY_ROWS, :] = jnp.zeros((CARRY_ROWS, TOKEN_WIDTH), F32)

    c = z_ref[:, TOKEN_WIDTH:2 * TOKEN_WIDTH].astype(F32)
    u = z_ref[:, 2 * TOKEN_WIDTH:3 * TOKEN_WIDTH].astype(F32)
    cu = c * u
    ext_ref[CARRY_ROWS:CARRY_ROWS + tq, :] = cu
    conv = (cw_ref[0:1, :] * ext_ref[CARRY_ROWS - 2:CARRY_ROWS - 2 + tq, :]
            + cw_ref[1:2, :] * ext_ref[CARRY_ROWS - 1:CARRY_ROWS - 1 + tq, :]
            + cw_ref[2:3, :] * cu)
    b = z_ref[:, 0:TOKEN_WIDTH].astype(F32)
    mix_ref[:, 0:TOKEN_WIDTH] = (b * conv).astype(BF16)
    st_ref[...] = ext_ref[CARRY_ROWS + tq - 2:CARRY_ROWS + tq, :]
    ext_ref[0:CARRY_ROWS, :] = ext_ref[tq:tq + CARRY_ROWS, :]

    def scores(h):
        lo, hi = h * MEM_HEAD_DIM, (h + 1) * MEM_HEAD_DIM
        return _cross_scores(z_ref[:, 3 * TOKEN_WIDTH + lo:3 * TOKEN_WIDTH + hi], mk_ref[pl.ds(h, N_MEM, stride=MEM_HEADS), :].astype(BF16))

    def finish(h, s):
        lo, hi = h * MEM_HEAD_DIM, (h + 1) * MEM_HEAD_DIM
        mix_ref[:, TOKEN_WIDTH + lo:TOKEN_WIDTH + hi] = _cross_values(
            s, mv_ref[pl.ds(h, N_MEM, stride=MEM_HEADS), :].astype(BF16)).astype(BF16)

    _run_ahead(MEM_HEADS, SCORE_LOOKAHEAD, scores, finish)


def _conv_prompt(z, mk, mv, conv_w, *, layer, batch, seq):
    rows, zc = z.shape
    tq = CONV_ROWS
    tiles_per_seq = seq // tq
    n_steps = batch * tiles_per_seq
    blocks = _nbytes((tq, zc), BF16) + 2 * _nbytes((N_MEM, MEM_WIDTH), F32) + _nbytes((tq, D_MODEL), BF16)
    scratch = _nbytes((tq + CARRY_ROWS, TOKEN_WIDTH), F32)
    return pl.pallas_call(
        functools.partial(_conv_prompt_kernel, tiles_per_seq=tiles_per_seq),
        grid=(n_steps,),
        in_specs=[
            pl.BlockSpec((tq, zc), lambda s: (s, 0)),
            pl.BlockSpec((None, N_MEM * MEM_HEADS, MEM_HEAD_DIM), lambda s: (layer, s // tiles_per_seq, 0)),
            pl.BlockSpec((None, N_MEM * MEM_HEADS, MEM_HEAD_DIM), lambda s: (layer, s // tiles_per_seq, 0)),
            pl.BlockSpec((None, CONV_WIDTH, TOKEN_WIDTH), lambda s: (0, 0, 0)),
        ],
        out_specs=[
            pl.BlockSpec((tq, D_MODEL), lambda s: (s, 0)),
            pl.BlockSpec((None, CONV_WIDTH - 1, TOKEN_WIDTH), lambda s: (s // tiles_per_seq, 0, 0)),
        ],
        out_shape=[
            jax.ShapeDtypeStruct((rows, D_MODEL), BF16),
            jax.ShapeDtypeStruct((batch, CONV_WIDTH - 1, TOKEN_WIDTH), F32),
        ],
        scratch_shapes=[pltpu.VMEM((tq + CARRY_ROWS, TOKEN_WIDTH), F32)],
        compiler_params=pltpu.CompilerParams(
            dimension_semantics=("arbitrary",),
            vmem_limit_bytes=_vmem_limit(blocks, scratch + 6 * _nbytes((tq, TOKEN_WIDTH), F32))),
        name="conv_prompt",
    )(z, mk, mv, conv_w)


def _conv_sample_kernel(z_ref, st_ref, mk_ref, mv_ref, cw_ref, wo_ref, mix_ref, nst_ref, wob_ref,
                        ext_ref, mixf_ref, *, dec_seq):
    t = dec_seq
    wob_ref[...] = wo_ref[...].astype(BF16)
    for n in range(SAMPLE_GROUP):
        r0, r1 = n * t, (n + 1) * t
        c = z_ref[r0:r1, TOKEN_WIDTH:2 * TOKEN_WIDTH].astype(F32)
        u = z_ref[r0:r1, 2 * TOKEN_WIDTH:3 * TOKEN_WIDTH].astype(F32)
        cu = c * u
        ext_ref[CARRY_ROWS - 2:CARRY_ROWS, :] = st_ref[n]
        ext_ref[CARRY_ROWS:CARRY_ROWS + t, :] = cu
        conv = (cw_ref[0:1, :] * ext_ref[CARRY_ROWS - 2:CARRY_ROWS - 2 + t, :]
                + cw_ref[1:2, :] * ext_ref[CARRY_ROWS - 1:CARRY_ROWS - 1 + t, :]
                + cw_ref[2:3, :] * cu)
        b = z_ref[r0:r1, 0:TOKEN_WIDTH].astype(F32)
        mixf_ref[r0:r1, 0:TOKEN_WIDTH] = b * conv
        nst_ref[n] = ext_ref[CARRY_ROWS + t - 2:CARRY_ROWS + t, :]
    _sample_cross_attention(z_ref, mk_ref, mv_ref, mixf_ref, 3 * TOKEN_WIDTH, t)
    mix_ref[...] = mixf_ref[...].astype(BF16)


def _sample_cross_attention(z_ref, mk_ref, mv_ref, mixf_ref, qm_off, t):
    def scores(i):
        n, h = divmod(i, MEM_HEADS)
        q = z_ref[n * t:(n + 1) * t, qm_off + h * MEM_HEAD_DIM:qm_off + (h + 1) * MEM_HEAD_DIM]
        return _cross_scores(q, mk_ref[n, pl.ds(h, N_MEM, stride=MEM_HEADS), :].astype(BF16))

    def finish(i, s):
        n, h = divmod(i, MEM_HEADS)
        o = _cross_values(s, mv_ref[n, pl.ds(h, N_MEM, stride=MEM_HEADS), :].astype(BF16))
        mixf_ref[n * t:(n + 1) * t, TOKEN_WIDTH + h * MEM_HEAD_DIM:TOKEN_WIDTH + (h + 1) * MEM_HEAD_DIM] = o

    _run_ahead(SAMPLE_GROUP * MEM_HEADS, SAMPLE_LOOKAHEAD, scores, finish)


def _conv_sample(z, state, mem_k, mem_v, conv_w, w_out, *, layer, dec_seq):
    rows, zc = z.shape
    dec_batch = state.shape[1]
    g = SAMPLE_GROUP
    gr = g * dec_seq
    n_steps = dec_batch // g
    wk, wn = w_out.shape[1:]
    slab = wk // n_steps
    assert slab * n_steps == wk and slab % (2 * V7X_SUBLANES) == 0
    blocks = (_nbytes((gr, zc), BF16) + 2 * _nbytes((g, CONV_WIDTH - 1, TOKEN_WIDTH), F32)
              + 2 * _nbytes((g, N_MEM, V7X_SUBLANES, MEM_HEAD_DIM), F32) + _nbytes((gr, D_MODEL), BF16)
              + _nbytes((slab, wn), F32) + _nbytes((slab, wn), BF16))
    scratch = _nbytes((2 * CARRY_ROWS, TOKEN_WIDTH), F32) + _nbytes((gr, D_MODEL), F32)
    return pl.pallas_call(
        functools.partial(_conv_sample_kernel, dec_seq=dec_seq),
        grid=(n_steps,),
        in_specs=[
            pl.BlockSpec((gr, zc), lambda i: (i, 0)),
            pl.BlockSpec((None, g, CONV_WIDTH - 1, TOKEN_WIDTH), lambda i: (0, i, 0, 0)),
            pl.BlockSpec((None, g, N_MEM * MEM_HEADS, MEM_HEAD_DIM), lambda i: (layer, i, 0, 0)),
            pl.BlockSpec((None, g, N_MEM * MEM_HEADS, MEM_HEAD_DIM), lambda i: (layer, i, 0, 0)),
            pl.BlockSpec((None, CONV_WIDTH, TOKEN_WIDTH), lambda i: (0, 0, 0)),
            pl.BlockSpec((None, slab, wn), lambda i: (0, i, 0)),
        ],
        out_specs=[
            pl.BlockSpec((gr, D_MODEL), lambda i: (i, 0)),
            pl.BlockSpec((g, CONV_WIDTH - 1, TOKEN_WIDTH), lambda i: (i, 0, 0)),
            pl.BlockSpec((slab, wn), lambda i: (i, 0)),
        ],
        out_shape=[
            jax.ShapeDtypeStruct((rows, D_MODEL), BF16),
            jax.ShapeDtypeStruct((dec_batch, CONV_WIDTH - 1, TOKEN_WIDTH), F32),
            jax.ShapeDtypeStruct((wk, wn), BF16),
        ],
        scratch_shapes=[pltpu.VMEM((2 * CARRY_ROWS, TOKEN_WIDTH), F32), pltpu.VMEM((gr, D_MODEL), F32)],
        compiler_params=pltpu.CompilerParams(
            dimension_semantics=("arbitrary",),
            vmem_limit_bytes=_vmem_limit(blocks, scratch)),
        name="conv_sample",
    )(z, state, mem_k, mem_v, conv_w, w_out)


def _band_scores(q, k):
    return _dot_nt(q * (HEAD_DIM ** -0.5), k)


def _band_probs(s, sink, upper, upper_visible):
    s = jnp.where(upper_visible, s[:, :WINDOW], jnp.where(upper, -jnp.inf, s[:, WINDOW:]))
    m = jnp.maximum(jnp.max(s, axis=-1, keepdims=True), sink)
    e = jnp.exp(s - m)
    e = jnp.concatenate([jnp.where(upper, e, 0.0), jnp.where(upper, 0.0, e)], axis=1).astype(BF16)
    return e, jnp.exp(sink - m)


def _band_values(probs, v, ones):
    e, sink_term = probs
    return _dot(e, v) / (_dot(e, ones) + sink_term)


def _swa_prompt_kernel(sink_ref, zq_ref, zp_ref, mk_ref, mv_ref, *refs, blocks_per_seq, cast_steps):
    n_w = (len(refs) - 1) // 2
    w_refs, mix_ref, wb_refs = refs[:n_w], refs[n_w], refs[n_w + 1:]
    s = pl.program_id(0)

    for w_ref, wb_ref, n_cast in zip(w_refs, wb_refs, cast_steps):
        @pl.when(s < n_cast)
        def _():
            wb_ref[...] = w_ref[...].astype(BF16)

    k_off = TOKEN_WIDTH
    v_off = TOKEN_WIDTH + KV_WIDTH
    qm_off = TOKEN_WIDTH + 2 * KV_WIDTH
    n_blocks = zq_ref.shape[0] // WINDOW
    first_has_prev = (s % (blocks_per_seq // n_blocks)) > 0
    row = lax.broadcasted_iota(jnp.int32, (WINDOW, WINDOW), 0)
    col = lax.broadcasted_iota(jnp.int32, (WINDOW, WINDOW), 1)
    upper = col > row
    upper_first = jnp.logical_and(upper, first_has_prev)
    ones = jnp.ones((2 * WINDOW, HEAD_DIM), BF16)

    def block_rows(b):
        return slice(b * WINDOW, (b + 1) * WINDOW)

    @functools.lru_cache(maxsize=None)
    def window(b, kh, col_prev, col_cur):
        lo, hi = kh * HEAD_DIM, (kh + 1) * HEAD_DIM
        prev = (zp_ref[:, col_prev + lo:col_prev + hi] if b == 0
                else zq_ref[block_rows(b - 1), col_cur + lo:col_cur + hi])
        return jnp.concatenate([prev, zq_ref[block_rows(b), col_cur + lo:col_cur + hi]], axis=0)

    def band_scores(i):
        b, h = divmod(i, N_HEADS)
        return _band_scores(zq_ref[block_rows(b), h * HEAD_DIM:(h + 1) * HEAD_DIM], window(b, h // GROUP, 0, k_off))

    def band_finish(i, s):
        b, h = divmod(i, N_HEADS)
        p = _band_probs(s, sink_ref[h], upper, upper_first if b == 0 else upper)
        o = _band_values(p, window(b, h // GROUP, KV_WIDTH, v_off), ones)
        mix_ref[block_rows(b), h * HEAD_DIM:(h + 1) * HEAD_DIM] = o.astype(BF16)

    def cross_scores(h):
        lo, hi = h * MEM_HEAD_DIM, (h + 1) * MEM_HEAD_DIM
        return _cross_scores(zq_ref[:, qm_off + lo:qm_off + hi], mk_ref[pl.ds(h, N_MEM, stride=MEM_HEADS), :].astype(BF16))

    def cross_finish(h, s):
        lo, hi = h * MEM_HEAD_DIM, (h + 1) * MEM_HEAD_DIM
        mix_ref[:, TOKEN_WIDTH + lo:TOKEN_WIDTH + hi] = _cross_values(
            s, mv_ref[pl.ds(h, N_MEM, stride=MEM_HEADS), :].astype(BF16)).astype(BF16)

    _run_ahead(n_blocks * N_HEADS, SCORE_LOOKAHEAD, band_scores, band_finish)
    _run_ahead(MEM_HEADS, SCORE_LOOKAHEAD, cross_scores, cross_finish)


def _swa_prompt(z, mk, mv, sinks, weights, *, layer, batch, seq):
    rows, zc = z.shape
    tq = SWA_ROWS
    blocks_per_seq = seq // WINDOW
    steps_per_seq = seq // tq
    blocks_per_step = tq // WINDOW
    n_steps = batch * steps_per_seq
    kv_col_block = TOKEN_WIDTH // (2 * KV_WIDTH)
    bf16_rows = 2 * V7X_SUBLANES
    blocks = (_nbytes((tq, zc), BF16) + _nbytes((WINDOW, 2 * KV_WIDTH), BF16)
              + 2 * _nbytes((N_MEM, MEM_WIDTH), F32) + _nbytes((tq, D_MODEL), BF16))
    w_in_specs, w_out_specs, w_out_shapes, cast_steps = [], [], [], []
    for w in weights:
        r, c = w.shape[1:]
        n_cast = n_steps if r % (n_steps * bf16_rows) == 0 else n_steps // 2
        slab = r // n_cast
        assert slab * n_cast == r and slab % bf16_rows == 0
        blocks += _nbytes((slab, c), F32) + _nbytes((slab, c), BF16)
        w_in_specs.append(pl.BlockSpec((None, slab, c), lambda s, n=n_cast: (layer, jnp.minimum(s, n - 1), 0)))
        w_out_specs.append(pl.BlockSpec((slab, c), lambda s, n=n_cast: (jnp.minimum(s, n - 1), 0)))
        w_out_shapes.append(jax.ShapeDtypeStruct((r, c), BF16))
        cast_steps.append(n_cast)
    return pl.pallas_call(
        functools.partial(_swa_prompt_kernel, blocks_per_seq=blocks_per_seq, cast_steps=tuple(cast_steps)),
        grid=(n_steps,),
        in_specs=[
            pl.BlockSpec(memory_space=pltpu.SMEM),
            pl.BlockSpec((tq, zc), lambda s: (s, 0)),
            pl.BlockSpec((WINDOW, 2 * KV_WIDTH), lambda s: (jnp.maximum(s * blocks_per_step - 1, 0), kv_col_block)),
            pl.BlockSpec((None, N_MEM * MEM_HEADS, MEM_HEAD_DIM), lambda s: (layer, s // steps_per_seq, 0)),
            pl.BlockSpec((None, N_MEM * MEM_HEADS, MEM_HEAD_DIM), lambda s: (layer, s // steps_per_seq, 0)),
        ] + w_in_specs,
        out_specs=[pl.BlockSpec((tq, D_MODEL), lambda s: (s, 0))] + w_out_specs,
        out_shape=[jax.ShapeDtypeStruct((rows, D_MODEL), BF16)] + w_out_shapes,
        compiler_params=pltpu.CompilerParams(
            dimension_semantics=("arbitrary",),
            vmem_limit_bytes=_vmem_limit(blocks, 0)),
        name="swa_prompt",
    )(sinks, z, z, mk, mv, *weights)


def _swa_sample_kernel(sink_ref, z_ref, ck_ref, cv_ref, mk_ref, mv_ref, wo_ref,
                       mix_ref, nk_ref, nv_ref, wob_ref, knew_ref, vnew_ref, mixf_ref, *, dec_seq):
    t = dec_seq
    wob_ref[...] = wo_ref[...].astype(BF16)
    k_off = TOKEN_WIDTH
    v_off = TOKEN_WIDTH + KV_WIDTH
    qm_off = TOKEN_WIDTH + 2 * KV_WIDTH
    rows = GROUP * t
    qi = lax.broadcasted_iota(jnp.int32, (rows, WINDOW), 0) % t
    col = lax.broadcasted_iota(jnp.int32, (rows, WINDOW), 1)
    upper = col > qi
    knew_ref[...] = jnp.zeros(knew_ref.shape, F32)
    vnew_ref[...] = jnp.zeros(vnew_ref.shape, F32)
    for n in range(SAMPLE_GROUP):
        r0, r1 = n * t, (n + 1) * t
        k_new = z_ref[r0:r1, k_off:k_off + KV_WIDTH].astype(F32)
        v_new = z_ref[r0:r1, v_off:v_off + KV_WIDTH].astype(F32)
        knew_ref[n, 0:t, :] = k_new
        vnew_ref[n, 0:t, :] = v_new
        nk_ref[n, 0:WINDOW - t, :] = ck_ref[n, t:WINDOW, :]
        nv_ref[n, 0:WINDOW - t, :] = cv_ref[n, t:WINDOW, :]
        nk_ref[n, WINDOW - t:WINDOW, :] = k_new
        nv_ref[n, WINDOW - t:WINDOW, :] = v_new

    def scores(i):
        n, kh = divmod(i, N_KV_HEADS)
        lo, hi = kh * HEAD_DIM, (kh + 1) * HEAD_DIM
        k = jnp.concatenate([ck_ref[n, :, lo:hi], knew_ref[n, :, lo:hi]], axis=0).astype(BF16)
        q = jnp.concatenate(
            [z_ref[n * t:(n + 1) * t, (kh * GROUP + g) * HEAD_DIM:(kh * GROUP + g + 1) * HEAD_DIM].astype(F32)
             for g in range(GROUP)], axis=0).astype(BF16)
        return _band_scores(q, k)

    def finish(i, s):
        n, kh = divmod(i, N_KV_HEADS)
        lo, hi = kh * HEAD_DIM, (kh + 1) * HEAD_DIM
        v = jnp.concatenate([cv_ref[n, :, lo:hi], vnew_ref[n, :, lo:hi]], axis=0).astype(BF16)
        sink = jnp.concatenate(
            [jnp.full((t, 1), sink_ref[kh * GROUP + g], F32) for g in range(GROUP)], axis=0)
        o = _band_values(_band_probs(s, sink, upper, upper), v, jnp.ones((2 * WINDOW, HEAD_DIM), BF16))
        for g in range(GROUP):
            h = kh * GROUP + g
            mixf_ref[n * t:(n + 1) * t, h * HEAD_DIM:(h + 1) * HEAD_DIM] = o[g * t:(g + 1) * t, :]

    _run_ahead(SAMPLE_GROUP * N_KV_HEADS, SAMPLE_LOOKAHEAD, scores, finish)
    _sample_cross_attention(z_ref, mk_ref, mv_ref, mixf_ref, qm_off, t)
    mix_ref[...] = mixf_ref[...].astype(BF16)


def _swa_sample(z, cache_k, cache_v, mem_k, mem_v, sinks, w_out, *, layer, dec_seq):
    rows, zc = z.shape
    dec_batch = cache_k.shape[0]
    g = SAMPLE_GROUP
    gr = g * dec_seq
    n_steps = dec_batch // g
    wk, wn = w_out.shape[1:]
    slab = wk // n_steps
    assert slab * n_steps == wk and slab % (2 * V7X_SUBLANES) == 0
    win = jax.ShapeDtypeStruct((dec_batch, WINDOW, KV_WIDTH), F32)
    blocks = (_nbytes((gr, zc), BF16) + 4 * _nbytes((g, WINDOW, KV_WIDTH), F32)
              + 2 * _nbytes((g, N_MEM, V7X_SUBLANES, MEM_HEAD_DIM), F32) + _nbytes((gr, D_MODEL), BF16)
              + _nbytes((slab, wn), F32) + _nbytes((slab, wn), BF16))
    scratch = 2 * _nbytes((g, WINDOW, KV_WIDTH), F32) + _nbytes((gr, D_MODEL), F32)
    return pl.pallas_call(
        functools.partial(_swa_sample_kernel, dec_seq=dec_seq),
        grid=(n_steps,),
        in_specs=[
            pl.BlockSpec(memory_space=pltpu.SMEM),
            pl.BlockSpec((gr, zc), lambda i: (i, 0)),
            pl.BlockSpec((g, WINDOW, KV_WIDTH), lambda i: (i, 0, 0)),
            pl.BlockSpec((g, WINDOW, KV_WIDTH), lambda i: (i, 0, 0)),
            pl.BlockSpec((None, g, N_MEM * MEM_HEADS, MEM_HEAD_DIM), lambda i: (layer, i, 0, 0)),
            pl.BlockSpec((None, g, N_MEM * MEM_HEADS, MEM_HEAD_DIM), lambda i: (layer, i, 0, 0)),
            pl.BlockSpec((None, slab, wn), lambda i: (0, i, 0)),
        ],
        out_specs=[
            pl.BlockSpec((gr, D_MODEL), lambda i: (i, 0)),
            pl.BlockSpec((g, WINDOW, KV_WIDTH), lambda i: (i, 0, 0)),
            pl.BlockSpec((g, WINDOW, KV_WIDTH), lambda i: (i, 0, 0)),
            pl.BlockSpec((slab, wn), lambda i: (i, 0)),
        ],
        out_shape=[jax.ShapeDtypeStruct((rows, D_MODEL), BF16), win, win,
                   jax.ShapeDtypeStruct((wk, wn), BF16)],
        scratch_shapes=[pltpu.VMEM((g, WINDOW, KV_WIDTH), F32), pltpu.VMEM((g, WINDOW, KV_WIDTH), F32),
                        pltpu.VMEM((gr, D_MODEL), F32)],
        compiler_params=pltpu.CompilerParams(
            dimension_semantics=("arbitrary",),
            vmem_limit_bytes=_vmem_limit(blocks, scratch)),
        name="swa_sample",
    )(sinks, z, cache_k, cache_v, mem_k, mem_v, w_out)


def kernel(x_prompt, x_sample, mem_prompt, state_conv, cache_win_k, cache_win_v, cache_mem_k, cache_mem_v,
           norm_mix, norm_mem, w_mem_kv, norm_ffn, w_gate, w_up, w_down,
           conv_w_in, conv_w, conv_w_out, attn_w_in, attn_sinks, attn_w_out, norm_final):
    batch, seq, d = x_prompt.shape
    dec_batch, dec_seq, _ = x_sample.shape
    depth = norm_mix.shape[0]
    d_ff = w_gate.shape[2]
    prompt_rows = batch * seq
    sample_rows = dec_batch * dec_seq
    assert d == D_MODEL and depth == 2 and seq % CONV_ROWS == 0 and seq % SWA_ROWS == 0 and SWA_ROWS % WINDOW == 0
    assert prompt_rows % PROMPT_TILE == 0 and dec_batch % SAMPLE_GROUP == 0
    assert dec_seq == V7X_SUBLANES and d_ff % FFN_COL_TILE == 0 and d % COL_TILE == 0
    assert d_ff % DOWN_K_TILE == 0 and d % DOWN_X_CHUNK == 0
    assert prompt_rows % FFN_ROW_TILE == 0 and prompt_rows % sample_rows == 0

    xp = x_prompt.reshape(prompt_rows, d)
    xs = x_sample.reshape(sample_rows, d)
    mem = mem_prompt.reshape(batch * N_MEM, d)
    mem_k_s = cache_mem_k.reshape(depth, dec_batch, N_MEM * MEM_HEADS, MEM_HEAD_DIM)
    mem_v_s = cache_mem_v.reshape(depth, dec_batch, N_MEM * MEM_HEADS, MEM_HEAD_DIM)
    g_mix = norm_mix.reshape(depth, 1, d)
    g_ffn = norm_ffn.reshape(depth, 1, d)

    mk, mv = _mem_kv(mem, norm_mem.reshape(depth, 1, d), w_mem_kv)

    hp, hs = _norm_rows(xp, xs, g_mix, layer=0)
    zp, zs, wd_bf = _matmul(hp, hs, conv_w_in, w_layer=0, side=w_down, side_layer=0)
    mix_s, conv_s, wo_bf = _conv_sample(zs, state_conv, mem_k_s, mem_v_s, conv_w, conv_w_out,
                                        layer=0, dec_seq=dec_seq)
    mix_p, conv_p = _conv_prompt(zp, mk, mv, conv_w, layer=0, batch=batch, seq=seq)
    xp, xs, hp, hs = _out_proj(mix_p, mix_s, wo_bf, xp, xs, g_ffn, layer=0)
    ap, as_ = _ffn_up(hp, hs, w_gate, w_up, layer=0)
    xp, xs, hp, hs = _ffn_down(ap, as_, wd_bf[None], xp, xs, g_mix, w_layer=0, g_layer=1, final=False)

    zp, zs, wd_bf = _matmul(hp, hs, attn_w_in, w_layer=0, side=w_down, side_layer=1)
    sinks = attn_sinks[0]
    mix_s, win_k_s, win_v_s, wo_bf = _swa_sample(
        zs, cache_win_k[0].reshape(dec_batch, WINDOW, KV_WIDTH), cache_win_v[0].reshape(dec_batch, WINDOW, KV_WIDTH),
        mem_k_s, mem_v_s, sinks, attn_w_out, layer=1, dec_seq=dec_seq)
    (mix_p,) = _swa_prompt(zp, mk, mv, sinks, [], layer=1, batch=batch, seq=seq)
    xp, xs, hp, hs = _out_proj(mix_p, mix_s, wo_bf, xp, xs, g_ffn, layer=1)
    ap, as_ = _ffn_up(hp, hs, w_gate, w_up, layer=1)
    y_prompt, y_sample = _ffn_down(ap, as_, wd_bf[None], xp, xs, norm_final.reshape(1, 1, d),
                                   w_layer=0, g_layer=0, final=True)

    win_p = zp.reshape(batch, seq, -1)[:, seq - WINDOW:, TOKEN_WIDTH:TOKEN_WIDTH + 2 * KV_WIDTH].astype(F32)
    kv_shape = (1, -1, WINDOW, N_KV_HEADS, HEAD_DIM)
    mem_shape = (depth, batch, N_MEM, MEM_HEADS, MEM_HEAD_DIM)
    return (y_prompt.reshape(batch, seq, d),
            y_sample.reshape(dec_batch, dec_seq, d),
            conv_p[None],
            conv_s[None],
            win_p[..., :KV_WIDTH].reshape(kv_shape),
            win_p[..., KV_WIDTH:].reshape(kv_shape),
            win_k_s.reshape(kv_shape),
            win_v_s.reshape(kv_shape),
            mk.reshape(mem_shape),
            mv.reshape(mem_shape))
```

```python
import functools

import jax
import jax.numpy as jnp
from jax import lax
from jax.experimental import pallas as pl
from jax.experimental.pallas import tpu as pltpu

F32 = jnp.float32
BF16 = jnp.bfloat16

D_MODEL = 2048
N_MEM = 256
MEM_HEADS = 4
MEM_WIDTH = D_MODEL // 4
MEM_HEAD_DIM = MEM_WIDTH // MEM_HEADS
TOKEN_WIDTH = D_MODEL - MEM_WIDTH
CONV_WIDTH = 3
WINDOW = 128
HEAD_DIM = 64
N_HEADS = TOKEN_WIDTH // HEAD_DIM
N_KV_HEADS = 4
GROUP = N_HEADS // N_KV_HEADS
KV_WIDTH = N_KV_HEADS * HEAD_DIM
EPS = 1e-6

V7X_VMEM_BYTES = 64 * 1024 * 1024
V7X_SUBLANES = 8
V7X_MXU_DEPTH = 256

PROMPT_TILE = 1024
COL_TILE = 512
DOWN_K_TILE = 1408
DOWN_X_CHUNK = 512
FFN_ROW_TILE = 2048
FFN_COL_TILE = 512
MXU_ROWS = 1024
OUT_ROWS = 512
CONV_ROWS = 512
CONV_CHUNK = 16
SWA_ROWS = 512
SCORE_LOOKAHEAD = 5
SAMPLE_LOOKAHEAD = 4
SAMPLE_GROUP = 8
SIDE_CAST_STEPS = 32
CARRY_ROWS = V7X_SUBLANES
MIN_TEMP_BYTES = 8 << 20
VMEM_RESERVE_BYTES = 6 << 20


def _nbytes(shape, dtype):
    n = 1
    for s in shape:
        n *= s
    return n * jnp.dtype(dtype).itemsize


def _vmem_limit(block_bytes, scratch_bytes):
    need = 2 * block_bytes + scratch_bytes
    temporaries = max(need // 4, MIN_TEMP_BYTES)
    return int(min(need + temporaries, V7X_VMEM_BYTES - VMEM_RESERVE_BYTES))


def _rmsnorm(x, g):
    r = lax.rsqrt(jnp.mean(x * x, axis=-1, keepdims=True) + EPS)
    return (x * r) * g


def _dot(a, b):
    return jnp.dot(a, b, preferred_element_type=F32)


def _dot_nt(a, b):
    return lax.dot_general(a, b, (((1,), (1,)), ((), ())), preferred_element_type=F32)


def _prompt_rows_map(i, j):
    return (i, 0)


def _prompt_tile_map(i, j):
    return (i, j)


def _sample_tile_map(n_pt):
    return lambda i, j: (0, jnp.where(i == n_pt - 1, j, 0))


def _const_map(i, j):
    return (0, 0)


def _norm_rows_kernel(xp_ref, xs_ref, g_ref, hp_ref, hs_ref, *, n_pt):
    hp_ref[...] = _rmsnorm(xp_ref[...], g_ref[...]).astype(BF16)

    @pl.when(pl.program_id(0) == n_pt - 1)
    def _():
        hs_ref[...] = _rmsnorm(xs_ref[...], g_ref[...]).astype(BF16)


def _norm_rows(xp, xs, g, *, layer):
    rp, d = xp.shape
    rs = xs.shape[0]
    tm = OUT_ROWS
    n_pt = rp // tm
    blocks = _nbytes((tm, d), F32) + _nbytes((tm, d), BF16) + _nbytes((rs, d), F32) + _nbytes((rs, d), BF16)
    return pl.pallas_call(
        functools.partial(_norm_rows_kernel, n_pt=n_pt),
        grid=(n_pt,),
        in_specs=[
            pl.BlockSpec((tm, d), lambda s: (s, 0)),
            pl.BlockSpec((rs, d), lambda s: (0, 0)),
            pl.BlockSpec((None, 1, d), lambda s: (layer, 0, 0)),
        ],
        out_specs=[
            pl.BlockSpec((tm, d), lambda s: (s, 0)),
            pl.BlockSpec((rs, d), lambda s: (0, 0)),
        ],
        out_shape=[jax.ShapeDtypeStruct((rp, d), BF16), jax.ShapeDtypeStruct((rs, d), BF16)],
        compiler_params=pltpu.CompilerParams(
            dimension_semantics=("arbitrary",),
            vmem_limit_bytes=_vmem_limit(blocks, 0)),
        name="norm_rows",
    )(xp, xs, g)


def _swiglu(h, wg, wu):
    gate = _dot(h, wg)
    up = _dot(h, wu)
    return (gate * jax.nn.sigmoid(gate) * up).astype(BF16)


def _ffn_up_kernel(hp_ref, hs_ref, wg_ref, wu_ref, ap_ref, as_ref, *, n_pt):
    wg, wu = wg_ref[...].astype(BF16), wu_ref[...].astype(BF16)
    for r in range(0, hp_ref.shape[0], MXU_ROWS):
        ap_ref[r:r + MXU_ROWS, :] = _swiglu(hp_ref[r:r + MXU_ROWS, :], wg, wu)

    @pl.when(pl.program_id(0) == n_pt - 1)
    def _():
        as_ref[...] = _swiglu(hs_ref[...], wg_ref[...].astype(BF16), wu_ref[...].astype(BF16))


def _ffn_up(hp, hs, wg, wu, *, layer):
    rp, d = hp.shape
    rs = hs.shape[0]
    n = wg.shape[2]
    tm, tn = FFN_ROW_TILE, FFN_COL_TILE
    n_pt, n_j = rp // tm, n // tn
    blocks = (_nbytes((tm, d), BF16) + _nbytes((rs, d), BF16) + 2 * _nbytes((d, tn), wg.dtype)
              + _nbytes((tm, tn), BF16) + _nbytes((rs, tn), BF16))
    temps = 2 * _nbytes((d, tn), BF16) + 3 * _nbytes((tm, tn), F32)
    w_spec = pl.BlockSpec((None, d, tn), lambda i, j: (layer, 0, j))
    return pl.pallas_call(
        functools.partial(_ffn_up_kernel, n_pt=n_pt),
        grid=(n_pt, n_j),
        in_specs=[
            pl.BlockSpec((tm, d), _prompt_rows_map),
            pl.BlockSpec((rs, d), _const_map),
            w_spec,
            w_spec,
        ],
        out_specs=[
            pl.BlockSpec((tm, tn), _prompt_tile_map),
            pl.BlockSpec((rs, tn), _sample_tile_map(n_pt)),
        ],
        out_shape=[jax.ShapeDtypeStruct((rp, n), BF16), jax.ShapeDtypeStruct((rs, n), BF16)],
        compiler_params=pltpu.CompilerParams(
            dimension_semantics=("arbitrary", "arbitrary"),
            vmem_limit_bytes=_vmem_limit(blocks, temps)),
        name="ffn_up",
    )(hp, hs, wg, wu)


def _out_proj_kernel(ap_ref, as_ref, w_ref, xp_ref, xs_ref, g_ref, op_ref, os_ref, hp_ref, hs_ref, *, n_pt):
    x = xp_ref[...] + _dot(ap_ref[...], w_ref[...])
    op_ref[...] = x
    hp_ref[...] = _rmsnorm(x, g_ref[...]).astype(BF16)

    @pl.when(pl.program_id(0) == n_pt - 1)
    def _():
        x = xs_ref[...] + _dot(as_ref[...], w_ref[...])
        os_ref[...] = x
        hs_ref[...] = _rmsnorm(x, g_ref[...]).astype(BF16)


def _out_proj(ap, as_, w, xp, xs, g, *, layer):
    rp, k = ap.shape
    rs = as_.shape[0]
    n = w.shape[1]
    tm = OUT_ROWS
    n_pt = rp // tm
    rows_map = lambda s: (s, 0)
    const = lambda s: (0, 0)
    once = dict(pipeline_mode=pl.Buffered(1))
    blocks = _nbytes((tm, k), BF16) + 2 * _nbytes((tm, n), F32) + _nbytes((tm, n), BF16)
    resident = (_nbytes((k, n), BF16) + _nbytes((rs, k), BF16) + _nbytes((rs, n), F32)
                + 2 * (_nbytes((rs, n), F32) + _nbytes((rs, n), BF16)))
    return pl.pallas_call(
        functools.partial(_out_proj_kernel, n_pt=n_pt),
        grid=(n_pt,),
        in_specs=[
            pl.BlockSpec((tm, k), rows_map),
            pl.BlockSpec((rs, k), const, **once),
            pl.BlockSpec((k, n), const, **once),
            pl.BlockSpec((tm, n), rows_map),
            pl.BlockSpec((rs, n), const, **once),
            pl.BlockSpec((None, 1, n), lambda s: (layer, 0, 0)),
        ],
        out_specs=[
            pl.BlockSpec((tm, n), rows_map),
            pl.BlockSpec((rs, n), const),
            pl.BlockSpec((tm, n), rows_map),
            pl.BlockSpec((rs, n), const),
        ],
        out_shape=[jax.ShapeDtypeStruct((rp, n), F32), jax.ShapeDtypeStruct((rs, n), F32),
                   jax.ShapeDtypeStruct((rp, n), BF16), jax.ShapeDtypeStruct((rs, n), BF16)],
        compiler_params=pltpu.CompilerParams(
            dimension_semantics=("arbitrary",),
            vmem_limit_bytes=_vmem_limit(blocks, resident + 2 * _nbytes((tm, n), F32))),
        name="out_proj",
    )(ap, as_, w, xp, xs, g)


def _matmul_kernel(hp_ref, hs_ref, w_ref, *refs, n_pt, n_j, n_cast):
    if n_cast:
        side_ref, zp_ref, zs_ref, side_bf_ref = refs

        @pl.when(pl.program_id(0) * n_j + pl.program_id(1) < n_cast)
        def _():
            side_bf_ref[...] = side_ref[...].astype(BF16)
    else:
        zp_ref, zs_ref = refs

    w = w_ref[...].astype(BF16)
    for r in range(0, hp_ref.shape[0], MXU_ROWS):
        zp_ref[r:r + MXU_ROWS, :] = _dot(hp_ref[r:r + MXU_ROWS, :], w).astype(zp_ref.dtype)

    @pl.when(pl.program_id(0) == n_pt - 1)
    def _():
        zs_ref[...] = _dot(hs_ref[...], w_ref[...].astype(BF16)).astype(zs_ref.dtype)


def _matmul(hp, hs, w, *, w_layer, side=None, side_layer=0):
    rp, d = hp.shape
    rs = hs.shape[0]
    n = w.shape[2]
    tm, tn = FFN_ROW_TILE, COL_TILE
    n_pt, n_j = rp // tm, n // tn
    blocks = (_nbytes((tm, d), BF16) + _nbytes((rs, d), BF16) + _nbytes((d, tn), F32)
              + _nbytes((tm, tn), BF16) + _nbytes((rs, tn), BF16))
    temps = _nbytes((d, tn), BF16) + _nbytes((tm, tn), F32)
    in_specs = [
        pl.BlockSpec((tm, d), _prompt_rows_map),
        pl.BlockSpec((rs, d), _const_map),
        pl.BlockSpec((None, d, tn), lambda i, j: (w_layer, 0, j)),
    ]
    out_specs = [
        pl.BlockSpec((tm, tn), _prompt_tile_map),
        pl.BlockSpec((rs, tn), _sample_tile_map(n_pt)),
    ]
    out_shape = [jax.ShapeDtypeStruct((rp, n), BF16), jax.ShapeDtypeStruct((rs, n), BF16)]
    operands = [hp, hs, w]
    n_cast = 0
    if side is not None:
        r, c = side.shape[1:]
        n_cast = SIDE_CAST_STEPS
        slab = r // n_cast
        assert n_cast <= n_pt * n_j and slab * n_cast == r and slab % (2 * V7X_SUBLANES) == 0
        slab_idx = lambda i, j: jnp.minimum(i * n_j + j, n_cast - 1)
        in_specs.append(pl.BlockSpec((None, slab, c), lambda i, j: (side_layer, slab_idx(i, j), 0)))
        out_specs.append(pl.BlockSpec((slab, c), lambda i, j: (slab_idx(i, j), 0)))
        out_shape.append(jax.ShapeDtypeStruct((r, c), BF16))
        operands.append(side)
        blocks += _nbytes((slab, c), F32) + _nbytes((slab, c), BF16)
    return pl.pallas_call(
        functools.partial(_matmul_kernel, n_pt=n_pt, n_j=n_j, n_cast=n_cast),
        grid=(n_pt, n_j),
        in_specs=in_specs,
        out_specs=out_specs,
        out_shape=out_shape,
        compiler_params=pltpu.CompilerParams(
            dimension_semantics=("arbitrary", "arbitrary"),
            vmem_limit_bytes=_vmem_limit(blocks, temps)),
        name="matmul",
    )(*operands)


def _ffn_down_kernel(ap_ref, as_ref, w_ref, xp_ref, xs_ref, g_ref, *refs, n_pt, n_k, n_xc, final):
    i, k = pl.program_id(0), pl.program_id(1)
    if final:
        op_ref, os_ref, apl_ref, asl_ref, wl_ref = refs
    else:
        op_ref, os_ref, hp_ref, hs_ref, apl_ref, asl_ref, wl_ref = refs
    xc = xp_ref.shape[1]
    kf = (ap_ref.shape[1] // V7X_MXU_DEPTH) * V7X_MXU_DEPTH
    even = k % 2 == 0

    def whole_passes(a_ref):
        return _dot(a_ref[:, :kf], w_ref[:kf, :])

    def with_stash(a_ref, al_ref):
        return _dot(jnp.concatenate([al_ref[...], a_ref[...]], axis=1),
                    jnp.concatenate([wl_ref[...], w_ref[...]], axis=0))

    for kk in range(n_k):
        @pl.when(k == kk)
        def _():
            cols = slice(kk * xc, (kk + 1) * xc)
            if 0 < kk < n_xc:
                op_ref[:, cols] = op_ref[:, cols] + xp_ref[...]
            contrib = whole_passes(ap_ref) if kk % 2 == 0 else with_stash(ap_ref, apl_ref)
            if kk == 0:
                op_ref[...] = contrib
                op_ref[:, cols] = op_ref[:, cols] + xp_ref[...]
            else:
                op_ref[...] = op_ref[...] + contrib
            if kk == n_k - 1:
                if final:
                    op_ref[...] = _rmsnorm(op_ref[...], g_ref[...])
                else:
                    hp_ref[...] = _rmsnorm(op_ref[...], g_ref[...]).astype(BF16)

    @pl.when(i == n_pt - 1)
    def _():
        @pl.when(k == 0)
        def _():
            os_ref[...] = xs_ref[...] + whole_passes(as_ref)

        @pl.when(jnp.logical_and(k > 0, even))
        def _():
            os_ref[...] = os_ref[...] + whole_passes(as_ref)

        @pl.when(jnp.logical_not(even))
        def _():
            os_ref[...] = os_ref[...] + with_stash(as_ref, asl_ref)

        @pl.when(even)
        def _():
            asl_ref[...] = as_ref[:, kf:]

        @pl.when(k == n_k - 1)
        def _():
            if final:
                os_ref[...] = _rmsnorm(os_ref[...], g_ref[...])
            else:
                hs_ref[...] = _rmsnorm(os_ref[...], g_ref[...]).astype(BF16)

    @pl.when(even)
    def _():
        apl_ref[...] = ap_ref[:, kf:]
        wl_ref[...] = w_ref[kf:, :]


def _ffn_down(ap, as_, w, xp, xs, g, *, w_layer, g_layer, final):
    rp, kdim = ap.shape
    rs = as_.shape[0]
    n = w.shape[2]
    tm, tk, xc = PROMPT_TILE, DOWN_K_TILE, DOWN_X_CHUNK
    n_pt, n_k, n_xc = rp // tm, kdim // tk, n // xc
    k_left = tk % V7X_MXU_DEPTH
    assert n_xc <= n_k and n_k % 2 == 0 and 2 * k_left == V7X_MXU_DEPTH and w.dtype == BF16
    rows = lambda i, k: (i, 0)
    blocks = (_nbytes((tm, tk), BF16) + _nbytes((rs, tk), BF16) + _nbytes((tk, n), w.dtype)
              + _nbytes((tm, xc), F32) + _nbytes((rs, n), F32)
              + _nbytes((tm, n), F32) + _nbytes((rs, n), F32))
    out_specs = [pl.BlockSpec((tm, n), rows), pl.BlockSpec((rs, n), _const_map)]
    out_shape = [jax.ShapeDtypeStruct((rp, n), F32), jax.ShapeDtypeStruct((rs, n), F32)]
    if not final:
        blocks += _nbytes((tm, n), BF16) + _nbytes((rs, n), BF16)
        out_specs += [pl.BlockSpec((tm, n), rows), pl.BlockSpec((rs, n), _const_map)]
        out_shape += [jax.ShapeDtypeStruct((rp, n), BF16), jax.ShapeDtypeStruct((rs, n), BF16)]
    return pl.pallas_call(
        functools.partial(_ffn_down_kernel, n_pt=n_pt, n_k=n_k, n_xc=n_xc, final=final),
        grid=(n_pt, n_k),
        in_specs=[
            pl.BlockSpec((tm, tk), lambda i, k: (i, k)),
            pl.BlockSpec((rs, tk), lambda i, k: (0, jnp.where(i == n_pt - 1, k, 0))),
            pl.BlockSpec((None, tk, n), lambda i, k: (w_layer, k, 0)),
            pl.BlockSpec((tm, xc), lambda i, k: (i, jnp.minimum(k, n_xc - 1))),
            pl.BlockSpec((rs, n), _const_map),
            pl.BlockSpec((None, 1, n), lambda i, k: (g_layer, 0, 0)),
        ],
        out_specs=out_specs,
        out_shape=out_shape,
        scratch_shapes=[pltpu.VMEM((tm, k_left), BF16), pltpu.VMEM((rs, k_left), BF16),
                        pltpu.VMEM((k_left, n), BF16)],
        compiler_params=pltpu.CompilerParams(
            dimension_semantics=("arbitrary", "arbitrary"),
            vmem_limit_bytes=_vmem_limit(blocks, _nbytes((tk, n), BF16))),
        name="ffn_down",
    )(ap, as_, w, xp, xs, g)


def _mem_kv_kernel(x_ref, g_ref, w_ref, k_ref, v_ref):
    h = _rmsnorm(x_ref[...], g_ref[...]).astype(BF16)
    kv = _dot(h, w_ref[...].astype(BF16))
    tm = x_ref.shape[0]
    for hd in range(MEM_HEADS):
        head_rows = pl.ds(hd, tm, stride=MEM_HEADS)
        k_ref[head_rows, :] = kv[:, hd * MEM_HEAD_DIM:(hd + 1) * MEM_HEAD_DIM]
        v_ref[head_rows, :] = kv[:, MEM_WIDTH + hd * MEM_HEAD_DIM:MEM_WIDTH + (hd + 1) * MEM_HEAD_DIM]


def _mem_kv(mem, g, w):
    rows, d = mem.shape
    depth = w.shape[0]
    tm = 512
    out = jax.ShapeDtypeStruct((depth, rows * MEM_HEADS, MEM_HEAD_DIM), F32)
    blocks = (_nbytes((tm, d), F32) + _nbytes((d, 2 * MEM_WIDTH), F32) + 2 * _nbytes((tm, MEM_WIDTH), F32))
    return pl.pallas_call(
        _mem_kv_kernel,
        grid=(depth, rows // tm),
        in_specs=[
            pl.BlockSpec((tm, d), lambda l, i: (i, 0)),
            pl.BlockSpec((None, 1, d), lambda l, i: (l, 0, 0)),
            pl.BlockSpec((None, d, 2 * MEM_WIDTH), lambda l, i: (l, 0, 0)),
        ],
        out_specs=[
            pl.BlockSpec((None, tm * MEM_HEADS, MEM_HEAD_DIM), lambda l, i: (l, i, 0)),
            pl.BlockSpec((None, tm * MEM_HEADS, MEM_HEAD_DIM), lambda l, i: (l, i, 0)),
        ],
        out_shape=[out, out],
        compiler_params=pltpu.CompilerParams(
            dimension_semantics=("arbitrary", "arbitrary"),
            vmem_limit_bytes=_vmem_limit(blocks, _nbytes((d, 2 * MEM_WIDTH), BF16))),
        name="mem_kv",
    )(mem, g, w)


def _cross_scores(q, k):
    return _dot_nt(q, k) * (MEM_HEAD_DIM ** -0.5)


def _cross_values(s, v):
    e = jnp.exp(s - jnp.max(s, axis=-1, keepdims=True)).astype(BF16)
    return _dot(e, v) / _dot(e, jnp.ones(v.shape, BF16))


def _run_ahead(n_items, lookahead, first, second):
    pending = {}
    for i in range(n_items + lookahead):
        if i < n_items:
            pending[i] = first(i)
        if i >= lookahead:
            second(i - lookahead, pending.pop(i - lookahead))


def _conv_prompt_kernel(z_ref, mk_ref, mv_ref, cw_ref, mix_ref, st_ref, ext_ref, *, tiles_per_seq):
    s = pl.program_id(0)
    tq = z_ref.shape[0]

    @pl.when(s % tiles_per_seq == 0)
    def _():
        ext_ref[0:CARRY_ROWS, :] = jnp.zeros((CARRY_ROWS, TOKEN_WIDTH), F32)

    def scores(h):
        lo, hi = h * MEM_HEAD_DIM, (h + 1) * MEM_HEAD_DIM
        return _cross_scores(z_ref[:, 3 * TOKEN_WIDTH + lo:3 * TOKEN_WIDTH + hi],
                             mk_ref[pl.ds(h, N_MEM, stride=MEM_HEADS), :].astype(BF16))

    def finish(h, s):
        lo, hi = h * MEM_HEAD_DIM, (h + 1) * MEM_HEAD_DIM
        mix_ref[:, TOKEN_WIDTH + lo:TOKEN_WIDTH + hi] = _cross_values(
            s, mv_ref[pl.ds(h, N_MEM, stride=MEM_HEADS), :].astype(BF16)).astype(BF16)

    for r in range(0, tq, CONV_CHUNK):
        rows = slice(r, r + CONV_CHUNK)
        c = z_ref[rows, TOKEN_WIDTH:2 * TOKEN_WIDTH].astype(F32)
        u = z_ref[rows, 2 * TOKEN_WIDTH:3 * TOKEN_WIDTH].astype(F32)
        cu = c * u
        ext_ref[CARRY_ROWS + r:CARRY_ROWS + r + CONV_CHUNK, :] = cu
        conv = (cw_ref[0:1, :] * ext_ref[CARRY_ROWS - 2 + r:CARRY_ROWS - 2 + r + CONV_CHUNK, :]
                + cw_ref[1:2, :] * ext_ref[CARRY_ROWS - 1 + r:CARRY_ROWS - 1 + r + CONV_CHUNK, :]
                + cw_ref[2:3, :] * cu)
        b = z_ref[rows, 0:TOKEN_WIDTH].astype(F32)
        mix_ref[rows, 0:TOKEN_WIDTH] = (b * conv).astype(BF16)
    st_ref[...] = ext_ref[CARRY_ROWS + tq - 2:CARRY_ROWS + tq, :]
    ext_ref[0:CARRY_ROWS, :] = ext_ref[tq:tq + CARRY_ROWS, :]
    _run_ahead(MEM_HEADS, SCORE_LOOKAHEAD, scores, finish)


def _conv_prompt(z, mk, mv, conv_w, *, layer, batch, seq):
    rows, zc = z.shape
    tq = CONV_ROWS
    tiles_per_seq = seq // tq
    n_steps = batch * tiles_per_seq
    blocks = _nbytes((tq, zc), BF16) + 2 * _nbytes((N_MEM, MEM_WIDTH), F32) + _nbytes((tq, D_MODEL), BF16)
    scratch = _nbytes((tq + CARRY_ROWS, TOKEN_WIDTH), F32)
    return pl.pallas_call(
        functools.partial(_conv_prompt_kernel, tiles_per_seq=tiles_per_seq),
        grid=(n_steps,),
        in_specs=[
            pl.BlockSpec((tq, zc), lambda s: (s, 0)),
            pl.BlockSpec((None, N_MEM * MEM_HEADS, MEM_HEAD_DIM), lambda s: (layer, s // tiles_per_seq, 0)),
            pl.BlockSpec((None, N_MEM * MEM_HEADS, MEM_HEAD_DIM), lambda s: (layer, s // tiles_per_seq, 0)),
            pl.BlockSpec((None, CONV_WIDTH, TOKEN_WIDTH), lambda s: (0, 0, 0)),
        ],
        out_specs=[
            pl.BlockSpec((tq, D_MODEL), lambda s: (s, 0)),
            pl.BlockSpec((None, CONV_WIDTH - 1, TOKEN_WIDTH), lambda s: (s // tiles_per_seq, 0, 0)),
        ],
        out_shape=[
            jax.ShapeDtypeStruct((rows, D_MODEL), BF16),
            jax.ShapeDtypeStruct((batch, CONV_WIDTH - 1, TOKEN_WIDTH), F32),
        ],
        scratch_shapes=[pltpu.VMEM((tq + CARRY_ROWS, TOKEN_WIDTH), F32)],
        compiler_params=pltpu.CompilerParams(
            dimension_semantics=("arbitrary",),
            vmem_limit_bytes=_vmem_limit(blocks, scratch + 6 * _nbytes((tq, TOKEN_WIDTH), F32))),
        name="conv_prompt",
    )(z, mk, mv, conv_w)


def _conv_sample_kernel(z_ref, st_ref, mk_ref, mv_ref, cw_ref, wo_ref, mix_ref, nst_ref, wob_ref,
                        ext_ref, mixf_ref, *, dec_seq):
    t = dec_seq
    wob_ref[...] = wo_ref[...].astype(BF16)
    for n in range(SAMPLE_GROUP):
        r0, r1 = n * t, (n + 1) * t
        c = z_ref[r0:r1, TOKEN_WIDTH:2 * TOKEN_WIDTH].astype(F32)
        u = z_ref[r0:r1, 2 * TOKEN_WIDTH:3 * TOKEN_WIDTH].astype(F32)
        cu = c * u
        ext_ref[CARRY_ROWS - 2:CARRY_ROWS, :] = st_ref[n]
        ext_ref[CARRY_ROWS:CARRY_ROWS + t, :] = cu
        conv = (cw_ref[0:1, :] * ext_ref[CARRY_ROWS - 2:CARRY_ROWS - 2 + t, :]
                + cw_ref[1:2, :] * ext_ref[CARRY_ROWS - 1:CARRY_ROWS - 1 + t, :]
                + cw_ref[2:3, :] * cu)
        b = z_ref[r0:r1, 0:TOKEN_WIDTH].astype(F32)
        mixf_ref[r0:r1, 0:TOKEN_WIDTH] = b * conv
        nst_ref[n] = ext_ref[CARRY_ROWS + t - 2:CARRY_ROWS + t, :]
    _sample_cross_attention(z_ref, mk_ref, mv_ref, mixf_ref, 3 * TOKEN_WIDTH, t)
    mix_ref[...] = mixf_ref[...].astype(BF16)


def _sample_cross_attention(z_ref, mk_ref, mv_ref, mixf_ref, qm_off, t):
    def scores(i):
        n, h = divmod(i, MEM_HEADS)
        q = z_ref[n * t:(n + 1) * t, qm_off + h * MEM_HEAD_DIM:qm_off + (h + 1) * MEM_HEAD_DIM]
        return _cross_scores(q, mk_ref[n, pl.ds(h, N_MEM, stride=MEM_HEADS), :].astype(BF16))

    def finish(i, s):
        n, h = divmod(i, MEM_HEADS)
        o = _cross_values(s, mv_ref[n, pl.ds(h, N_MEM, stride=MEM_HEADS), :].astype(BF16))
        mixf_ref[n * t:(n + 1) * t, TOKEN_WIDTH + h * MEM_HEAD_DIM:TOKEN_WIDTH + (h + 1) * MEM_HEAD_DIM] = o

    _run_ahead(SAMPLE_GROUP * MEM_HEADS, SAMPLE_LOOKAHEAD, scores, finish)


def _conv_sample(z, state, mem_k, mem_v, conv_w, w_out, *, layer, dec_seq):
    rows, zc = z.shape
    dec_batch = state.shape[1]
    g = SAMPLE_GROUP
    gr = g * dec_seq
    n_steps = dec_batch // g
    wk, wn = w_out.shape[1:]
    slab = wk // n_steps
    assert slab * n_steps == wk and slab % (2 * V7X_SUBLANES) == 0
    blocks = (_nbytes((gr, zc), BF16) + 2 * _nbytes((g, CONV_WIDTH - 1, TOKEN_WIDTH), F32)
              + 2 * _nbytes((g, N_MEM, V7X_SUBLANES, MEM_HEAD_DIM), F32) + _nbytes((gr, D_MODEL), BF16)
              + _nbytes((slab, wn), F32) + _nbytes((slab, wn), BF16))
    scratch = _nbytes((2 * CARRY_ROWS, TOKEN_WIDTH), F32) + _nbytes((gr, D_MODEL), F32)
    return pl.pallas_call(
        functools.partial(_conv_sample_kernel, dec_seq=dec_seq),
        grid=(n_steps,),
        in_specs=[
            pl.BlockSpec((gr, zc), lambda i: (i, 0)),
            pl.BlockSpec((None, g, CONV_WIDTH - 1, TOKEN_WIDTH), lambda i: (0, i, 0, 0)),
            pl.BlockSpec((None, g, N_MEM * MEM_HEADS, MEM_HEAD_DIM), lambda i: (layer, i, 0, 0)),
            pl.BlockSpec((None, g, N_MEM * MEM_HEADS, MEM_HEAD_DIM), lambda i: (layer, i, 0, 0)),
            pl.BlockSpec((None, CONV_WIDTH, TOKEN_WIDTH), lambda i: (0, 0, 0)),
            pl.BlockSpec((None, slab, wn), lambda i: (0, i, 0)),
        ],
        out_specs=[
            pl.BlockSpec((gr, D_MODEL), lambda i: (i, 0)),
            pl.BlockSpec((g, CONV_WIDTH - 1, TOKEN_WIDTH), lambda i: (i, 0, 0)),
            pl.BlockSpec((slab, wn), lambda i: (i, 0)),
        ],
        out_shape=[
            jax.ShapeDtypeStruct((rows, D_MODEL), BF16),
            jax.ShapeDtypeStruct((dec_batch, CONV_WIDTH - 1, TOKEN_WIDTH), F32),
            jax.ShapeDtypeStruct((wk, wn), BF16),
        ],
        scratch_shapes=[pltpu.VMEM((2 * CARRY_ROWS, TOKEN_WIDTH), F32), pltpu.VMEM((gr, D_MODEL), F32)],
        compiler_params=pltpu.CompilerParams(
            dimension_semantics=("arbitrary",),
            vmem_limit_bytes=_vmem_limit(blocks, scratch)),
        name="conv_sample",
    )(z, state, mem_k, mem_v, conv_w, w_out)


def _band_scores(q, k):
    return _dot_nt(q * (HEAD_DIM ** -0.5), k)


def _band_probs(s, sink, upper, upper_visible):
    s = jnp.where(upper_visible, s[:, :WINDOW], jnp.where(upper, -jnp.inf, s[:, WINDOW:]))
    m = jnp.maximum(jnp.max(s, axis=-1, keepdims=True), sink)
    e = jnp.exp(s - m)
    e = jnp.concatenate([jnp.where(upper, e, 0.0), jnp.where(upper, 0.0, e)], axis=1).astype(BF16)
    return e, jnp.exp(sink - m)


def _band_values(probs, v, ones):
    e, sink_term = probs
    return _dot(e, v) / (_dot(e, ones) + sink_term)


def _swa_prompt_kernel(sink_ref, zq_ref, zp_ref, mk_ref, mv_ref, *refs, blocks_per_seq, cast_steps):
    n_w = (len(refs) - 1) // 2
    w_refs, mix_ref, wb_refs = refs[:n_w], refs[n_w], refs[n_w + 1:]
    s = pl.program_id(0)

    for w_ref, wb_ref, n_cast in zip(w_refs, wb_refs, cast_steps):
        @pl.when(s < n_cast)
        def _():
            wb_ref[...] = w_ref[...].astype(BF16)

    k_off = TOKEN_WIDTH
    v_off = TOKEN_WIDTH + KV_WIDTH
    qm_off = TOKEN_WIDTH + 2 * KV_WIDTH
    n_blocks = zq_ref.shape[0] // WINDOW
    first_has_prev = (s % (blocks_per_seq // n_blocks)) > 0
    row = lax.broadcasted_iota(jnp.int32, (WINDOW, WINDOW), 0)
    col = lax.broadcasted_iota(jnp.int32, (WINDOW, WINDOW), 1)
    upper = col > row
    upper_first = jnp.logical_and(upper, first_has_prev)
    ones = jnp.ones((2 * WINDOW, HEAD_DIM), BF16)

    def block_rows(b):
        return slice(b * WINDOW, (b + 1) * WINDOW)

    @functools.lru_cache(maxsize=None)
    def window(b, kh, col_prev, col_cur):
        lo, hi = kh * HEAD_DIM, (kh + 1) * HEAD_DIM
        prev = (zp_ref[:, col_prev + lo:col_prev + hi] if b == 0
                else zq_ref[block_rows(b - 1), col_cur + lo:col_cur + hi])
        return jnp.concatenate([prev, zq_ref[block_rows(b), col_cur + lo:col_cur + hi]], axis=0)

    def band_scores(i):
        b, h = divmod(i, N_HEADS)
        return _band_scores(zq_ref[block_rows(b), h * HEAD_DIM:(h + 1) * HEAD_DIM], window(b, h // GROUP, 0, k_off))

    def band_finish(i, s):
        b, h = divmod(i, N_HEADS)
        p = _band_probs(s, sink_ref[h], upper, upper_first if b == 0 else upper)
        o = _band_values(p, window(b, h // GROUP, KV_WIDTH, v_off), ones)
        mix_ref[block_rows(b), h * HEAD_DIM:(h + 1) * HEAD_DIM] = o.astype(BF16)

    def cross_scores(h):
        lo, hi = h * MEM_HEAD_DIM, (h + 1) * MEM_HEAD_DIM
        return _cross_scores(zq_ref[:, qm_off + lo:qm_off + hi], mk_ref[pl.ds(h, N_MEM, stride=MEM_HEADS), :].astype(BF16))

    def cross_finish(h, s):
        lo, hi = h * MEM_HEAD_DIM, (h + 1) * MEM_HEAD_DIM
        mix_ref[:, TOKEN_WIDTH + lo:TOKEN_WIDTH + hi] = _cross_values(
            s, mv_ref[pl.ds(h, N_MEM, stride=MEM_HEADS), :].astype(BF16)).astype(BF16)

    _run_ahead(n_blocks * N_HEADS, SCORE_LOOKAHEAD, band_scores, band_finish)
    _run_ahead(MEM_HEADS, SCORE_LOOKAHEAD, cross_scores, cross_finish)


def _swa_prompt(z, mk, mv, sinks, weights, *, layer, batch, seq):
    rows, zc = z.shape
    tq = SWA_ROWS
    blocks_per_seq = seq // WINDOW
    steps_per_seq = seq // tq
    blocks_per_step = tq // WINDOW
    n_steps = batch * steps_per_seq
    kv_col_block = TOKEN_WIDTH // (2 * KV_WIDTH)
    bf16_rows = 2 * V7X_SUBLANES
    blocks = (_nbytes((tq, zc), BF16) + _nbytes((WINDOW, 2 * KV_WIDTH), BF16)
              + 2 * _nbytes((N_MEM, MEM_WIDTH), F32) + _nbytes((tq, D_MODEL), BF16))
    w_in_specs, w_out_specs, w_out_shapes, cast_steps = [], [], [], []
    for w in weights:
        r, c = w.shape[1:]
        n_cast = n_steps if r % (n_steps * bf16_rows) == 0 else n_steps // 2
        slab = r // n_cast
        assert slab * n_cast == r and slab % bf16_rows == 0
        blocks += _nbytes((slab, c), F32) + _nbytes((slab, c), BF16)
        w_in_specs.append(pl.BlockSpec((None, slab, c), lambda s, n=n_cast: (layer, jnp.minimum(s, n - 1), 0)))
        w_out_specs.append(pl.BlockSpec((slab, c), lambda s, n=n_cast: (jnp.minimum(s, n - 1), 0)))
        w_out_shapes.append(jax.ShapeDtypeStruct((r, c), BF16))
        cast_steps.append(n_cast)
    return pl.pallas_call(
        functools.partial(_swa_prompt_kernel, blocks_per_seq=blocks_per_seq, cast_steps=tuple(cast_steps)),
        grid=(n_steps,),
        in_specs=[
            pl.BlockSpec(memory_space=pltpu.SMEM),
            pl.BlockSpec((tq, zc), lambda s: (s, 0)),
            pl.BlockSpec((WINDOW, 2 * KV_WIDTH), lambda s: (jnp.maximum(s * blocks_per_step - 1, 0), kv_col_block)),
            pl.BlockSpec((None, N_MEM * MEM_HEADS, MEM_HEAD_DIM), lambda s: (layer, s // steps_per_seq, 0)),
            pl.BlockSpec((None, N_MEM * MEM_HEADS, MEM_HEAD_DIM), lambda s: (layer, s // steps_per_seq, 0)),
        ] + w_in_specs,
        out_specs=[pl.BlockSpec((tq, D_MODEL), lambda s: (s, 0))] + w_out_specs,
        out_shape=[jax.ShapeDtypeStruct((rows, D_MODEL), BF16)] + w_out_shapes,
        compiler_params=pltpu.CompilerParams(
            dimension_semantics=("arbitrary",),
            vmem_limit_bytes=_vmem_limit(blocks, 0)),
        name="swa_prompt",
    )(sinks, z, z, mk, mv, *weights)


def _swa_sample_kernel(sink_ref, z_ref, ck_ref, cv_ref, mk_ref, mv_ref, wo_ref,
                       mix_ref, nk_ref, nv_ref, wob_ref, knew_ref, vnew_ref, mixf_ref, *, dec_seq):
    t = dec_seq
    wob_ref[...] = wo_ref[...].astype(BF16)
    k_off = TOKEN_WIDTH
    v_off = TOKEN_WIDTH + KV_WIDTH
    qm_off = TOKEN_WIDTH + 2 * KV_WIDTH
    rows = GROUP * t
    qi = lax.broadcasted_iota(jnp.int32, (rows, WINDOW), 0) % t
    col = lax.broadcasted_iota(jnp.int32, (rows, WINDOW), 1)
    upper = col > qi
    knew_ref[...] = jnp.zeros(knew_ref.shape, F32)
    vnew_ref[...] = jnp.zeros(vnew_ref.shape, F32)
    for n in range(SAMPLE_GROUP):
        r0, r1 = n * t, (n + 1) * t
        k_new = z_ref[r0:r1, k_off:k_off + KV_WIDTH].astype(F32)
        v_new = z_ref[r0:r1, v_off:v_off + KV_WIDTH].astype(F32)
        knew_ref[n, 0:t, :] = k_new
        vnew_ref[n, 0:t, :] = v_new
        nk_ref[n, 0:WINDOW - t, :] = ck_ref[n, t:WINDOW, :]
        nv_ref[n, 0:WINDOW - t, :] = cv_ref[n, t:WINDOW, :]
        nk_ref[n, WINDOW - t:WINDOW, :] = k_new
        nv_ref[n, WINDOW - t:WINDOW, :] = v_new

    def scores(i):
        n, kh = divmod(i, N_KV_HEADS)
        lo, hi = kh * HEAD_DIM, (kh + 1) * HEAD_DIM
        k = jnp.concatenate([ck_ref[n, :, lo:hi], knew_ref[n, :, lo:hi]], axis=0).astype(BF16)
        q = jnp.concatenate(
            [z_ref[n * t:(n + 1) * t, (kh * GROUP + g) * HEAD_DIM:(kh * GROUP + g + 1) * HEAD_DIM].astype(F32)
             for g in range(GROUP)], axis=0).astype(BF16)
        return _band_scores(q, k)

    def finish(i, s):
        n, kh = divmod(i, N_KV_HEADS)
        lo, hi = kh * HEAD_DIM, (kh + 1) * HEAD_DIM
        v = jnp.concatenate([cv_ref[n, :, lo:hi], vnew_ref[n, :, lo:hi]], axis=0).astype(BF16)
        sink = jnp.concatenate(
            [jnp.full((t, 1), sink_ref[kh * GROUP + g], F32) for g in range(GROUP)], axis=0)
        o = _band_values(_band_probs(s, sink, upper, upper), v, jnp.ones((2 * WINDOW, HEAD_DIM), BF16))
        for g in range(GROUP):
            h = kh * GROUP + g
            mixf_ref[n * t:(n + 1) * t, h * HEAD_DIM:(h + 1) * HEAD_DIM] = o[g * t:(g + 1) * t, :]

    _run_ahead(SAMPLE_GROUP * N_KV_HEADS, SAMPLE_LOOKAHEAD, scores, finish)
    _sample_cross_attention(z_ref, mk_ref, mv_ref, mixf_ref, qm_off, t)
    mix_ref[...] = mixf_ref[...].astype(BF16)


def _swa_sample(z, cache_k, cache_v, mem_k, mem_v, sinks, w_out, *, layer, dec_seq):
    rows, zc = z.shape
    dec_batch = cache_k.shape[0]
    g = SAMPLE_GROUP
    gr = g * dec_seq
    n_steps = dec_batch // g
    wk, wn = w_out.shape[1:]
    slab = wk // n_steps
    assert slab * n_steps == wk and slab % (2 * V7X_SUBLANES) == 0
    win = jax.ShapeDtypeStruct((dec_batch, WINDOW, KV_WIDTH), F32)
    blocks = (_nbytes((gr, zc), BF16) + 4 * _nbytes((g, WINDOW, KV_WIDTH), F32)
              + 2 * _nbytes((g, N_MEM, V7X_SUBLANES, MEM_HEAD_DIM), F32) + _nbytes((gr, D_MODEL), BF16)
              + _nbytes((slab, wn), F32) + _nbytes((slab, wn), BF16))
    scratch = 2 * _nbytes((g, WINDOW, KV_WIDTH), F32) + _nbytes((gr, D_MODEL), F32)
    return pl.pallas_call(
        functools.partial(_swa_sample_kernel, dec_seq=dec_seq),
        grid=(n_steps,),
        in_specs=[
            pl.BlockSpec(memory_space=pltpu.SMEM),
            pl.BlockSpec((gr, zc), lambda i: (i, 0)),
            pl.BlockSpec((g, WINDOW, KV_WIDTH), lambda i: (i, 0, 0)),
            pl.BlockSpec((g, WINDOW, KV_WIDTH), lambda i: (i, 0, 0)),
            pl.BlockSpec((None, g, N_MEM * MEM_HEADS, MEM_HEAD_DIM), lambda i: (layer, i, 0, 0)),
            pl.BlockSpec((None, g, N_MEM * MEM_HEADS, MEM_HEAD_DIM), lambda i: (layer, i, 0, 0)),
            pl.BlockSpec((None, slab, wn), lambda i: (0, i, 0)),
        ],
        out_specs=[
            pl.BlockSpec((gr, D_MODEL), lambda i: (i, 0)),
            pl.BlockSpec((g, WINDOW, KV_WIDTH), lambda i: (i, 0, 0)),
            pl.BlockSpec((g, WINDOW, KV_WIDTH), lambda i: (i, 0, 0)),
            pl.BlockSpec((slab, wn), lambda i: (i, 0)),
        ],
        out_shape=[jax.ShapeDtypeStruct((rows, D_MODEL), BF16), win, win,
                   jax.ShapeDtypeStruct((wk, wn), BF16)],
        scratch_shapes=[pltpu.VMEM((g, WINDOW, KV_WIDTH), F32), pltpu.VMEM((g, WINDOW, KV_WIDTH), F32),
                        pltpu.VMEM((gr, D_MODEL), F32)],
        compiler_params=pltpu.CompilerParams(
            dimension_semantics=("arbitrary",),
            vmem_limit_bytes=_vmem_limit(blocks, scratch)),
        name="swa_sample",
    )(sinks, z, cache_k, cache_v, mem_k, mem_v, w_out)


def kernel(x_prompt, x_sample, mem_prompt, state_conv, cache_win_k, cache_win_v, cache_mem_k, cache_mem_v,
           norm_mix, norm_mem, w_mem_kv, norm_ffn, w_gate, w_up, w_down,
           conv_w_in, conv_w, conv_w_out, attn_w_in, attn_sinks, attn_w_out, norm_final):
    batch, seq, d = x_prompt.shape
    dec_batch, dec_seq, _ = x_sample.shape
    depth = norm_mix.shape[0]
    d_ff = w_gate.shape[2]
    prompt_rows = batch * seq
    sample_rows = dec_batch * dec_seq
    assert d == D_MODEL and depth == 2 and seq % CONV_ROWS == 0 and seq % SWA_ROWS == 0 and SWA_ROWS % WINDOW == 0
    assert prompt_rows % PROMPT_TILE == 0 and dec_batch % SAMPLE_GROUP == 0
    assert dec_seq == V7X_SUBLANES and d_ff % FFN_COL_TILE == 0 and d % COL_TILE == 0
    assert d_ff % DOWN_K_TILE == 0 and d % DOWN_X_CHUNK == 0
    assert prompt_rows % FFN_ROW_TILE == 0 and prompt_rows % sample_rows == 0

    xp = x_prompt.reshape(prompt_rows, d)
    xs = x_sample.reshape(sample_rows, d)
    mem = mem_prompt.reshape(batch * N_MEM, d)
    mem_k_s = cache_mem_k.reshape(depth, dec_batch, N_MEM * MEM_HEADS, MEM_HEAD_DIM)
    mem_v_s = cache_mem_v.reshape(depth, dec_batch, N_MEM * MEM_HEADS, MEM_HEAD_DIM)
    g_mix = norm_mix.reshape(depth, 1, d)
    g_ffn = norm_ffn.reshape(depth, 1, d)

    mk, mv = _mem_kv(mem, norm_mem.reshape(depth, 1, d), w_mem_kv)

    hp, hs = _norm_rows(xp, xs, g_mix, layer=0)
    zp, zs, wd_bf = _matmul(hp, hs, conv_w_in, w_layer=0, side=w_down, side_layer=0)
    mix_s, conv_s, wo_bf = _conv_sample(zs, state_conv, mem_k_s, mem_v_s, conv_w, conv_w_out,
                                        layer=0, dec_seq=dec_seq)
    mix_p, conv_p = _conv_prompt(zp, mk, mv, conv_w, layer=0, batch=batch, seq=seq)
    xp, xs, hp, hs = _out_proj(mix_p, mix_s, wo_bf, xp, xs, g_ffn, layer=0)
    ap, as_ = _ffn_up(hp, hs, w_gate, w_up, layer=0)
    xp, xs, hp, hs = _ffn_down(ap, as_, wd_bf[None], xp, xs, g_mix, w_layer=0, g_layer=1, final=False)

    zp, zs = _matmul(hp, hs, attn_w_in, w_layer=0)
    sinks = attn_sinks[0]
    mix_s, win_k_s, win_v_s, wo_bf = _swa_sample(
        zs, cache_win_k[0].reshape(dec_batch, WINDOW, KV_WIDTH), cache_win_v[0].reshape(dec_batch, WINDOW, KV_WIDTH),
        mem_k_s, mem_v_s, sinks, attn_w_out, layer=1, dec_seq=dec_seq)
    mix_p, wd_bf = _swa_prompt(zp, mk, mv, sinks, [w_down], layer=1, batch=batch, seq=seq)
    xp, xs, hp, hs = _out_proj(mix_p, mix_s, wo_bf, xp, xs, g_ffn, layer=1)
    ap, as_ = _ffn_up(hp, hs, w_gate, w_up, layer=1)
    y_prompt, y_sample = _ffn_down(ap, as_, wd_bf[None], xp, xs, norm_final.reshape(1, 1, d),
                                   w_layer=0, g_layer=0, final=True)

    win_p = zp.reshape(batch, seq, -1)[:, seq - WINDOW:, TOKEN_WIDTH:TOKEN_WIDTH + 2 * KV_WIDTH].astype(F32)
    kv_shape = (1, -1, WINDOW, N_KV_HEADS, HEAD_DIM)
    mem_shape = (depth, batch, N_MEM, MEM_HEADS, MEM_HEAD_DIM)
    return (y_prompt.reshape(batch, seq, d),
            y_sample.reshape(dec_batch, dec_seq, d),
            conv_p[None],
            conv_s[None],
            win_p[..., :KV_WIDTH].reshape(kv_shape),
            win_p[..., KV_WIDTH:].reshape(kv_shape),
            win_k_s.reshape(kv_shape),
            win_v_s.reshape(kv_shape),
            mk.reshape(mem_shape),
            mv.reshape(mem_shape))
```

```python
import functools

import jax
import jax.numpy as jnp
from jax import lax
from jax.experimental import pallas as pl
from jax.experimental.pallas import tpu as pltpu

F32 = jnp.float32
BF16 = jnp.bfloat16

D_MODEL = 2048
N_MEM = 256
MEM_HEADS = 4
MEM_WIDTH = D_MODEL // 4
MEM_HEAD_DIM = MEM_WIDTH // MEM_HEADS
TOKEN_WIDTH = D_MODEL - MEM_WIDTH
CONV_WIDTH = 3
WINDOW = 128
HEAD_DIM = 64
N_HEADS = TOKEN_WIDTH // HEAD_DIM
N_KV_HEADS = 4
GROUP = N_HEADS // N_KV_HEADS
KV_WIDTH = N_KV_HEADS * HEAD_DIM
EPS = 1e-6

V7X_VMEM_BYTES = 64 * 1024 * 1024
V7X_SUBLANES = 8
V7X_MXU_DEPTH = 256

PROMPT_TILE = 1024
COL_TILE = 512
DOWN_K_TILE = 1408
DOWN_X_CHUNK = 512
FFN_ROW_TILE = 2048
FFN_COL_TILE = 512
MXU_ROWS = 1024
OUT_ROWS = 512
CONV_ROWS = 512
CONV_CHUNK = 16
SWA_ROWS = 512
SCORE_LOOKAHEAD = 48
SAMPLE_LOOKAHEAD = 4
SAMPLE_GROUP = 8
SIDE_CAST_STEPS = 32
CARRY_ROWS = V7X_SUBLANES
MIN_TEMP_BYTES = 8 << 20
VMEM_RESERVE_BYTES = 6 << 20


def _nbytes(shape, dtype):
    n = 1
    for s in shape:
        n *= s
    return n * jnp.dtype(dtype).itemsize


def _vmem_limit(block_bytes, scratch_bytes):
    need = 2 * block_bytes + scratch_bytes
    temporaries = max(need // 4, MIN_TEMP_BYTES)
    return int(min(need + temporaries, V7X_VMEM_BYTES - VMEM_RESERVE_BYTES))


def _rmsnorm(x, g):
    r = lax.rsqrt(jnp.mean(x * x, axis=-1, keepdims=True) + EPS)
    return (x * r) * g


def _dot(a, b):
    return jnp.dot(a, b, preferred_element_type=F32)


def _dot_nt(a, b):
    return lax.dot_general(a, b, (((1,), (1,)), ((), ())), preferred_element_type=F32)


def _prompt_rows_map(i, j):
    return (i, 0)


def _prompt_tile_map(i, j):
    return (i, j)


def _sample_tile_map(n_pt):
    return lambda i, j: (0, jnp.where(i == n_pt - 1, j, 0))


def _const_map(i, j):
    return (0, 0)


def _norm_rows_kernel(xp_ref, xs_ref, g_ref, hp_ref, hs_ref, *, n_pt):
    hp_ref[...] = _rmsnorm(xp_ref[...], g_ref[...]).astype(BF16)

    @pl.when(pl.program_id(0) == n_pt - 1)
    def _():
        hs_ref[...] = _rmsnorm(xs_ref[...], g_ref[...]).astype(BF16)


def _norm_rows(xp, xs, g, *, layer):
    rp, d = xp.shape
    rs = xs.shape[0]
    tm = OUT_ROWS
    n_pt = rp // tm
    blocks = _nbytes((tm, d), F32) + _nbytes((tm, d), BF16) + _nbytes((rs, d), F32) + _nbytes((rs, d), BF16)
    return pl.pallas_call(
        functools.partial(_norm_rows_kernel, n_pt=n_pt),
        grid=(n_pt,),
        in_specs=[
            pl.BlockSpec((tm, d), lambda s: (s, 0)),
            pl.BlockSpec((rs, d), lambda s: (0, 0)),
            pl.BlockSpec((None, 1, d), lambda s: (layer, 0, 0)),
        ],
        out_specs=[
            pl.BlockSpec((tm, d), lambda s: (s, 0)),
            pl.BlockSpec((rs, d), lambda s: (0, 0)),
        ],
        out_shape=[jax.ShapeDtypeStruct((rp, d), BF16), jax.ShapeDtypeStruct((rs, d), BF16)],
        compiler_params=pltpu.CompilerParams(
            dimension_semantics=("arbitrary",),
            vmem_limit_bytes=_vmem_limit(blocks, 0)),
        name="norm_rows",
    )(xp, xs, g)


def _swiglu(h, wg, wu):
    gate = _dot(h, wg)
    up = _dot(h, wu)
    return (gate * jax.nn.sigmoid(gate) * up).astype(BF16)


def _ffn_up_kernel(hp_ref, hs_ref, wg_ref, wu_ref, ap_ref, as_ref, *, n_pt):
    wg, wu = wg_ref[...].astype(BF16), wu_ref[...].astype(BF16)
    for r in range(0, hp_ref.shape[0], MXU_ROWS):
        ap_ref[r:r + MXU_ROWS, :] = _swiglu(hp_ref[r:r + MXU_ROWS, :], wg, wu)

    @pl.when(pl.program_id(0) == n_pt - 1)
    def _():
        as_ref[...] = _swiglu(hs_ref[...], wg_ref[...].astype(BF16), wu_ref[...].astype(BF16))


def _ffn_up(hp, hs, wg, wu, *, layer):
    rp, d = hp.shape
    rs = hs.shape[0]
    n = wg.shape[2]
    tm, tn = FFN_ROW_TILE, FFN_COL_TILE
    n_pt, n_j = rp // tm, n // tn
    blocks = (_nbytes((tm, d), BF16) + _nbytes((rs, d), BF16) + 2 * _nbytes((d, tn), wg.dtype)
              + _nbytes((tm, tn), BF16) + _nbytes((rs, tn), BF16))
    temps = 2 * _nbytes((d, tn), BF16) + 3 * _nbytes((tm, tn), F32)
    w_spec = pl.BlockSpec((None, d, tn), lambda i, j: (layer, 0, j))
    return pl.pallas_call(
        functools.partial(_ffn_up_kernel, n_pt=n_pt),
        grid=(n_pt, n_j),
        in_specs=[
            pl.BlockSpec((tm, d), _prompt_rows_map),
            pl.BlockSpec((rs, d), _const_map),
            w_spec,
            w_spec,
        ],
        out_specs=[
            pl.BlockSpec((tm, tn), _prompt_tile_map),
            pl.BlockSpec((rs, tn), _sample_tile_map(n_pt)),
        ],
        out_shape=[jax.ShapeDtypeStruct((rp, n), BF16), jax.ShapeDtypeStruct((rs, n), BF16)],
        compiler_params=pltpu.CompilerParams(
            dimension_semantics=("arbitrary", "arbitrary"),
            vmem_limit_bytes=_vmem_limit(blocks, temps)),
        name="ffn_up",
    )(hp, hs, wg, wu)


def _out_proj_kernel(ap_ref, as_ref, w_ref, xp_ref, xs_ref, g_ref, op_ref, os_ref, hp_ref, hs_ref, *, n_pt):
    x = xp_ref[...] + _dot(ap_ref[...], w_ref[...])
    op_ref[...] = x
    hp_ref[...] = _rmsnorm(x, g_ref[...]).astype(BF16)

    @pl.when(pl.program_id(0) == n_pt - 1)
    def _():
        x = xs_ref[...] + _dot(as_ref[...], w_ref[...])
        os_ref[...] = x
        hs_ref[...] = _rmsnorm(x, g_ref[...]).astype(BF16)


def _out_proj(ap, as_, w, xp, xs, g, *, layer):
    rp, k = ap.shape
    rs = as_.shape[0]
    n = w.shape[1]
    tm = OUT_ROWS
    n_pt = rp // tm
    rows_map = lambda s: (s, 0)
    const = lambda s: (0, 0)
    once = dict(pipeline_mode=pl.Buffered(1))
    blocks = _nbytes((tm, k), BF16) + 2 * _nbytes((tm, n), F32) + _nbytes((tm, n), BF16)
    resident = (_nbytes((k, n), BF16) + _nbytes((rs, k), BF16) + _nbytes((rs, n), F32)
                + 2 * (_nbytes((rs, n), F32) + _nbytes((rs, n), BF16)))
    return pl.pallas_call(
        functools.partial(_out_proj_kernel, n_pt=n_pt),
        grid=(n_pt,),
        in_specs=[
            pl.BlockSpec((tm, k), rows_map),
            pl.BlockSpec((rs, k), const, **once),
            pl.BlockSpec((k, n), const, **once),
            pl.BlockSpec((tm, n), rows_map),
            pl.BlockSpec((rs, n), const, **once),
            pl.BlockSpec((None, 1, n), lambda s: (layer, 0, 0)),
        ],
        out_specs=[
            pl.BlockSpec((tm, n), rows_map),
            pl.BlockSpec((rs, n), const),
            pl.BlockSpec((tm, n), rows_map),
            pl.BlockSpec((rs, n), const),
        ],
        out_shape=[jax.ShapeDtypeStruct((rp, n), F32), jax.ShapeDtypeStruct((rs, n), F32),
                   jax.ShapeDtypeStruct((rp, n), BF16), jax.ShapeDtypeStruct((rs, n), BF16)],
        compiler_params=pltpu.CompilerParams(
            dimension_semantics=("arbitrary",),
            vmem_limit_bytes=_vmem_limit(blocks, resident + 2 * _nbytes((tm, n), F32))),
        name="out_proj",
    )(ap, as_, w, xp, xs, g)


def _matmul_kernel(hp_ref, hs_ref, w_ref, *refs, n_pt, n_j, n_cast):
    if n_cast:
        side_ref, zp_ref, zs_ref, side_bf_ref = refs

        @pl.when(pl.program_id(0) * n_j + pl.program_id(1) < n_cast)
        def _():
            side_bf_ref[...] = side_ref[...].astype(BF16)
    else:
        zp_ref, zs_ref = refs

    w = w_ref[...].astype(BF16)
    for r in range(0, hp_ref.shape[0], MXU_ROWS):
        zp_ref[r:r + MXU_ROWS, :] = _dot(hp_ref[r:r + MXU_ROWS, :], w).astype(zp_ref.dtype)

    @pl.when(pl.program_id(0) == n_pt - 1)
    def _():
        zs_ref[...] = _dot(hs_ref[...], w_ref[...].astype(BF16)).astype(zs_ref.dtype)


def _matmul(hp, hs, w, *, w_layer, side=None, side_layer=0):
    rp, d = hp.shape
    rs = hs.shape[0]
    n = w.shape[2]
    tm, tn = FFN_ROW_TILE, COL_TILE
    n_pt, n_j = rp // tm, n // tn
    blocks = (_nbytes((tm, d), BF16) + _nbytes((rs, d), BF16) + _nbytes((d, tn), F32)
              + _nbytes((tm, tn), BF16) + _nbytes((rs, tn), BF16))
    temps = _nbytes((d, tn), BF16) + _nbytes((tm, tn), F32)
    in_specs = [
        pl.BlockSpec((tm, d), _prompt_rows_map),
        pl.BlockSpec((rs, d), _const_map),
        pl.BlockSpec((None, d, tn), lambda i, j: (w_layer, 0, j)),
    ]
    out_specs = [
        pl.BlockSpec((tm, tn), _prompt_tile_map),
        pl.BlockSpec((rs, tn), _sample_tile_map(n_pt)),
    ]
    out_shape = [jax.ShapeDtypeStruct((rp, n), BF16), jax.ShapeDtypeStruct((rs, n), BF16)]
    operands = [hp, hs, w]
    n_cast = 0
    if side is not None:
        r, c = side.shape[1:]
        n_cast = SIDE_CAST_STEPS
        slab = r // n_cast
        assert n_cast <= n_pt * n_j and slab * n_cast == r and slab % (2 * V7X_SUBLANES) == 0
        slab_idx = lambda i, j: jnp.minimum(i * n_j + j, n_cast - 1)
        in_specs.append(pl.BlockSpec((None, slab, c), lambda i, j: (side_layer, slab_idx(i, j), 0)))
        out_specs.append(pl.BlockSpec((slab, c), lambda i, j: (slab_idx(i, j), 0)))
        out_shape.append(jax.ShapeDtypeStruct((r, c), BF16))
        operands.append(side)
        blocks += _nbytes((slab, c), F32) + _nbytes((slab, c), BF16)
    return pl.pallas_call(
        functools.partial(_matmul_kernel, n_pt=n_pt, n_j=n_j, n_cast=n_cast),
        grid=(n_pt, n_j),
        in_specs=in_specs,
        out_specs=out_specs,
        out_shape=out_shape,
        compiler_params=pltpu.CompilerParams(
            dimension_semantics=("arbitrary", "arbitrary"),
            vmem_limit_bytes=_vmem_limit(blocks, temps)),
        name="matmul",
    )(*operands)


def _ffn_down_kernel(ap_ref, as_ref, w_ref, xp_ref, xs_ref, g_ref, *refs, n_pt, n_k, n_xc, final):
    i, k = pl.program_id(0), pl.program_id(1)
    if final:
        op_ref, os_ref, apl_ref, asl_ref, wl_ref = refs
    else:
        op_ref, os_ref, hp_ref, hs_ref, apl_ref, asl_ref, wl_ref = refs
    xc = xp_ref.shape[1]
    kf = (ap_ref.shape[1] // V7X_MXU_DEPTH) * V7X_MXU_DEPTH
    even = k % 2 == 0

    def whole_passes(a_ref):
        return _dot(a_ref[:, :kf], w_ref[:kf, :])

    def with_stash(a_ref, al_ref):
        return _dot(jnp.concatenate([al_ref[...], a_ref[...]], axis=1),
                    jnp.concatenate([wl_ref[...], w_ref[...]], axis=0))

    for kk in range(n_k):
        @pl.when(k == kk)
        def _():
            cols = slice(kk * xc, (kk + 1) * xc)
            if 0 < kk < n_xc:
                op_ref[:, cols] = op_ref[:, cols] + xp_ref[...]
            contrib = whole_passes(ap_ref) if kk % 2 == 0 else with_stash(ap_ref, apl_ref)
            if kk == 0:
                op_ref[...] = contrib
                op_ref[:, cols] = op_ref[:, cols] + xp_ref[...]
            else:
                op_ref[...] = op_ref[...] + contrib
            if kk == n_k - 1:
                if final:
                    op_ref[...] = _rmsnorm(op_ref[...], g_ref[...])
                else:
                    hp_ref[...] = _rmsnorm(op_ref[...], g_ref[...]).astype(BF16)

    @pl.when(i == n_pt - 1)
    def _():
        @pl.when(k == 0)
        def _():
            os_ref[...] = xs_ref[...] + whole_passes(as_ref)

        @pl.when(jnp.logical_and(k > 0, even))
        def _():
            os_ref[...] = os_ref[...] + whole_passes(as_ref)

        @pl.when(jnp.logical_not(even))
        def _():
            os_ref[...] = os_ref[...] + with_stash(as_ref, asl_ref)

        @pl.when(even)
        def _():
            asl_ref[...] = as_ref[:, kf:]

        @pl.when(k == n_k - 1)
        def _():
            if final:
                os_ref[...] = _rmsnorm(os_ref[...], g_ref[...])
            else:
                hs_ref[...] = _rmsnorm(os_ref[...], g_ref[...]).astype(BF16)

    @pl.when(even)
    def _():
        apl_ref[...] = ap_ref[:, kf:]
        wl_ref[...] = w_ref[kf:, :]


def _ffn_down(ap, as_, w, xp, xs, g, *, w_layer, g_layer, final):
    rp, kdim = ap.shape
    rs = as_.shape[0]
    n = w.shape[2]
    tm, tk, xc = PROMPT_TILE, DOWN_K_TILE, DOWN_X_CHUNK
    n_pt, n_k, n_xc = rp // tm, kdim // tk, n // xc
    k_left = tk % V7X_MXU_DEPTH
    assert n_xc <= n_k and n_k % 2 == 0 and 2 * k_left == V7X_MXU_DEPTH and w.dtype == BF16
    rows = lambda i, k: (i, 0)
    blocks = (_nbytes((tm, tk), BF16) + _nbytes((rs, tk), BF16) + _nbytes((tk, n), w.dtype)
              + _nbytes((tm, xc), F32) + _nbytes((rs, n), F32)
              + _nbytes((tm, n), F32) + _nbytes((rs, n), F32))
    out_specs = [pl.BlockSpec((tm, n), rows), pl.BlockSpec((rs, n), _const_map)]
    out_shape = [jax.ShapeDtypeStruct((rp, n), F32), jax.ShapeDtypeStruct((rs, n), F32)]
    if not final:
        blocks += _nbytes((tm, n), BF16) + _nbytes((rs, n), BF16)
        out_specs += [pl.BlockSpec((tm, n), rows), pl.BlockSpec((rs, n), _const_map)]
        out_shape += [jax.ShapeDtypeStruct((rp, n), BF16), jax.ShapeDtypeStruct((rs, n), BF16)]
    return pl.pallas_call(
        functools.partial(_ffn_down_kernel, n_pt=n_pt, n_k=n_k, n_xc=n_xc, final=final),
        grid=(n_pt, n_k),
        in_specs=[
            pl.BlockSpec((tm, tk), lambda i, k: (i, k)),
            pl.BlockSpec((rs, tk), lambda i, k: (0, jnp.where(i == n_pt - 1, k, 0))),
            pl.BlockSpec((None, tk, n), lambda i, k: (w_layer, k, 0)),
            pl.BlockSpec((tm, xc), lambda i, k: (i, jnp.minimum(k, n_xc - 1))),
            pl.BlockSpec((rs, n), _const_map),
            pl.BlockSpec((None, 1, n), lambda i, k: (g_layer, 0, 0)),
        ],
        out_specs=out_specs,
        out_shape=out_shape,
        scratch_shapes=[pltpu.VMEM((tm, k_left), BF16), pltpu.VMEM((rs, k_left), BF16),
                        pltpu.VMEM((k_left, n), BF16)],
        compiler_params=pltpu.CompilerParams(
            dimension_semantics=("arbitrary", "arbitrary"),
            vmem_limit_bytes=_vmem_limit(blocks, _nbytes((tk, n), BF16))),
        name="ffn_down",
    )(ap, as_, w, xp, xs, g)


def _mem_kv_kernel(x_ref, g_ref, w_ref, k_ref, v_ref):
    h = _rmsnorm(x_ref[...], g_ref[...]).astype(BF16)
    kv = _dot(h, w_ref[...].astype(BF16))
    tm = x_ref.shape[0]
    for hd in range(MEM_HEADS):
        head_rows = pl.ds(hd, tm, stride=MEM_HEADS)
        k_ref[head_rows, :] = kv[:, hd * MEM_HEAD_DIM:(hd + 1) * MEM_HEAD_DIM]
        v_ref[head_rows, :] = kv[:, MEM_WIDTH + hd * MEM_HEAD_DIM:MEM_WIDTH + (hd + 1) * MEM_HEAD_DIM]


def _mem_kv(mem, g, w):
    rows, d = mem.shape
    depth = w.shape[0]
    tm = 512
    out = jax.ShapeDtypeStruct((depth, rows * MEM_HEADS, MEM_HEAD_DIM), F32)
    blocks = (_nbytes((tm, d), F32) + _nbytes((d, 2 * MEM_WIDTH), F32) + 2 * _nbytes((tm, MEM_WIDTH), F32))
    return pl.pallas_call(
        _mem_kv_kernel,
        grid=(depth, rows // tm),
        in_specs=[
            pl.BlockSpec((tm, d), lambda l, i: (i, 0)),
            pl.BlockSpec((None, 1, d), lambda l, i: (l, 0, 0)),
            pl.BlockSpec((None, d, 2 * MEM_WIDTH), lambda l, i: (l, 0, 0)),
        ],
        out_specs=[
            pl.BlockSpec((None, tm * MEM_HEADS, MEM_HEAD_DIM), lambda l, i: (l, i, 0)),
            pl.BlockSpec((None, tm * MEM_HEADS, MEM_HEAD_DIM), lambda l, i: (l, i, 0)),
        ],
        out_shape=[out, out],
        compiler_params=pltpu.CompilerParams(
            dimension_semantics=("arbitrary", "arbitrary"),
            vmem_limit_bytes=_vmem_limit(blocks, _nbytes((d, 2 * MEM_WIDTH), BF16))),
        name="mem_kv",
    )(mem, g, w)


def _cross_scores(q, k):
    return _dot_nt(q, k) * (MEM_HEAD_DIM ** -0.5)


def _cross_values(s, v):
    e = jnp.exp(s - jnp.max(s, axis=-1, keepdims=True)).astype(BF16)
    return _dot(e, v) / _dot(e, jnp.ones(v.shape, BF16))


def _run_ahead(n_items, lookahead, first, second):
    pending = {}
    for i in range(n_items + lookahead):
        if i < n_items:
            pending[i] = first(i)
        if i >= lookahead:
            second(i - lookahead, pending.pop(i - lookahead))


def _conv_prompt_kernel(z_ref, mk_ref, mv_ref, cw_ref, mix_ref, st_ref, ext_ref, *, tiles_per_seq):
    s = pl.program_id(0)
    tq = z_ref.shape[0]

    @pl.when(s % tiles_per_seq == 0)
    def _():
        ext_ref[0:CARRY_ROWS, :] = jnp.zeros((CARRY_ROWS, TOKEN_WIDTH), F32)

    def scores(h):
        lo, hi = h * MEM_HEAD_DIM, (h + 1) * MEM_HEAD_DIM
        return _cross_scores(z_ref[:, 3 * TOKEN_WIDTH + lo:3 * TOKEN_WIDTH + hi],
                             mk_ref[pl.ds(h, N_MEM, stride=MEM_HEADS), :].astype(BF16))

    def finish(h, s):
        lo, hi = h * MEM_HEAD_DIM, (h + 1) * MEM_HEAD_DIM
        mix_ref[:, TOKEN_WIDTH + lo:TOKEN_WIDTH + hi] = _cross_values(
            s, mv_ref[pl.ds(h, N_MEM, stride=MEM_HEADS), :].astype(BF16)).astype(BF16)

    for r in range(0, tq, CONV_CHUNK):
        rows = slice(r, r + CONV_CHUNK)
        c = z_ref[rows, TOKEN_WIDTH:2 * TOKEN_WIDTH].astype(F32)
        u = z_ref[rows, 2 * TOKEN_WIDTH:3 * TOKEN_WIDTH].astype(F32)
        cu = c * u
        ext_ref[CARRY_ROWS + r:CARRY_ROWS + r + CONV_CHUNK, :] = cu
        conv = (cw_ref[0:1, :] * ext_ref[CARRY_ROWS - 2 + r:CARRY_ROWS - 2 + r + CONV_CHUNK, :]
                + cw_ref[1:2, :] * ext_ref[CARRY_ROWS - 1 + r:CARRY_ROWS - 1 + r + CONV_CHUNK, :]
                + cw_ref[2:3, :] * cu)
        b = z_ref[rows, 0:TOKEN_WIDTH].astype(F32)
        mix_ref[rows, 0:TOKEN_WIDTH] = (b * conv).astype(BF16)
    st_ref[...] = ext_ref[CARRY_ROWS + tq - 2:CARRY_ROWS + tq, :]
    ext_ref[0:CARRY_ROWS, :] = ext_ref[tq:tq + CARRY_ROWS, :]
    _run_ahead(MEM_HEADS, SCORE_LOOKAHEAD, scores, finish)


def _conv_prompt(z, mk, mv, conv_w, *, layer, batch, seq):
    rows, zc = z.shape
    tq = CONV_ROWS
    tiles_per_seq = seq // tq
    n_steps = batch * tiles_per_seq
    blocks = _nbytes((tq, zc), BF16) + 2 * _nbytes((N_MEM, MEM_WIDTH), F32) + _nbytes((tq, D_MODEL), BF16)
    scratch = _nbytes((tq + CARRY_ROWS, TOKEN_WIDTH), F32)
    return pl.pallas_call(
        functools.partial(_conv_prompt_kernel, tiles_per_seq=tiles_per_seq),
        grid=(n_steps,),
        in_specs=[
            pl.BlockSpec((tq, zc), lambda s: (s, 0)),
            pl.BlockSpec((None, N_MEM * MEM_HEADS, MEM_HEAD_DIM), lambda s: (layer, s // tiles_per_seq, 0)),
            pl.BlockSpec((None, N_MEM * MEM_HEADS, MEM_HEAD_DIM), lambda s: (layer, s // tiles_per_seq, 0)),
            pl.BlockSpec((None, CONV_WIDTH, TOKEN_WIDTH), lambda s: (0, 0, 0)),
        ],
        out_specs=[
            pl.BlockSpec((tq, D_MODEL), lambda s: (s, 0)),
            pl.BlockSpec((None, CONV_WIDTH - 1, TOKEN_WIDTH), lambda s: (s // tiles_per_seq, 0, 0)),
        ],
        out_shape=[
            jax.ShapeDtypeStruct((rows, D_MODEL), BF16),
            jax.ShapeDtypeStruct((batch, CONV_WIDTH - 1, TOKEN_WIDTH), F32),
        ],
        scratch_shapes=[pltpu.VMEM((tq + CARRY_ROWS, TOKEN_WIDTH), F32)],
        compiler_params=pltpu.CompilerParams(
            dimension_semantics=("arbitrary",),
            vmem_limit_bytes=_vmem_limit(blocks, scratch + 6 * _nbytes((tq, TOKEN_WIDTH), F32))),
        name="conv_prompt",
    )(z, mk, mv, conv_w)


def _conv_sample_kernel(z_ref, st_ref, mk_ref, mv_ref, cw_ref, wo_ref, mix_ref, nst_ref, wob_ref,
                        ext_ref, mixf_ref, *, dec_seq):
    t = dec_seq
    wob_ref[...] = wo_ref[...].astype(BF16)
    for n in range(SAMPLE_GROUP):
        r0, r1 = n * t, (n + 1) * t
        c = z_ref[r0:r1, TOKEN_WIDTH:2 * TOKEN_WIDTH].astype(F32)
        u = z_ref[r0:r1, 2 * TOKEN_WIDTH:3 * TOKEN_WIDTH].astype(F32)
        cu = c * u
        ext_ref[CARRY_ROWS - 2:CARRY_ROWS, :] = st_ref[n]
        ext_ref[CARRY_ROWS:CARRY_ROWS + t, :] = cu
        conv = (cw_ref[0:1, :] * ext_ref[CARRY_ROWS - 2:CARRY_ROWS - 2 + t, :]
                + cw_ref[1:2, :] * ext_ref[CARRY_ROWS - 1:CARRY_ROWS - 1 + t, :]
                + cw_ref[2:3, :] * cu)
        b = z_ref[r0:r1, 0:TOKEN_WIDTH].astype(F32)
        mixf_ref[r0:r1, 0:TOKEN_WIDTH] = b * conv
        nst_ref[n] = ext_ref[CARRY_ROWS + t - 2:CARRY_ROWS + t, :]
    _sample_cross_attention(z_ref, mk_ref, mv_ref, mixf_ref, 3 * TOKEN_WIDTH, t)
    mix_ref[...] = mixf_ref[...].astype(BF16)


def _sample_cross_attention(z_ref, mk_ref, mv_ref, mixf_ref, qm_off, t):
    def scores(i):
        n, h = divmod(i, MEM_HEADS)
        q = z_ref[n * t:(n + 1) * t, qm_off + h * MEM_HEAD_DIM:qm_off + (h + 1) * MEM_HEAD_DIM]
        return _cross_scores(q, mk_ref[n, pl.ds(h, N_MEM, stride=MEM_HEADS), :].astype(BF16))

    def finish(i, s):
        n, h = divmod(i, MEM_HEADS)
        o = _cross_values(s, mv_ref[n, pl.ds(h, N_MEM, stride=MEM_HEADS), :].astype(BF16))
        mixf_ref[n * t:(n + 1) * t, TOKEN_WIDTH + h * MEM_HEAD_DIM:TOKEN_WIDTH + (h + 1) * MEM_HEAD_DIM] = o

    _run_ahead(SAMPLE_GROUP * MEM_HEADS, SAMPLE_LOOKAHEAD, scores, finish)


def _conv_sample(z, state, mem_k, mem_v, conv_w, w_out, *, layer, dec_seq):
    rows, zc = z.shape
    dec_batch = state.shape[1]
    g = SAMPLE_GROUP
    gr = g * dec_seq
    n_steps = dec_batch // g
    wk, wn = w_out.shape[1:]
    slab = wk // n_steps
    assert slab * n_steps == wk and slab % (2 * V7X_SUBLANES) == 0
    blocks = (_nbytes((gr, zc), BF16) + 2 * _nbytes((g, CONV_WIDTH - 1, TOKEN_WIDTH), F32)
              + 2 * _nbytes((g, N_MEM, V7X_SUBLANES, MEM_HEAD_DIM), F32) + _nbytes((gr, D_MODEL), BF16)
              + _nbytes((slab, wn), F32) + _nbytes((slab, wn), BF16))
    scratch = _nbytes((2 * CARRY_ROWS, TOKEN_WIDTH), F32) + _nbytes((gr, D_MODEL), F32)
    return pl.pallas_call(
        functools.partial(_conv_sample_kernel, dec_seq=dec_seq),
        grid=(n_steps,),
        in_specs=[
            pl.BlockSpec((gr, zc), lambda i: (i, 0)),
            pl.BlockSpec((None, g, CONV_WIDTH - 1, TOKEN_WIDTH), lambda i: (0, i, 0, 0)),
            pl.BlockSpec((None, g, N_MEM * MEM_HEADS, MEM_HEAD_DIM), lambda i: (layer, i, 0, 0)),
            pl.BlockSpec((None, g, N_MEM * MEM_HEADS, MEM_HEAD_DIM), lambda i: (layer, i, 0, 0)),
            pl.BlockSpec((None, CONV_WIDTH, TOKEN_WIDTH), lambda i: (0, 0, 0)),
            pl.BlockSpec((None, slab, wn), lambda i: (0, i, 0)),
        ],
        out_specs=[
            pl.BlockSpec((gr, D_MODEL), lambda i: (i, 0)),
            pl.BlockSpec((g, CONV_WIDTH - 1, TOKEN_WIDTH), lambda i: (i, 0, 0)),
            pl.BlockSpec((slab, wn), lambda i: (i, 0)),
        ],
        out_shape=[
            jax.ShapeDtypeStruct((rows, D_MODEL), BF16),
            jax.ShapeDtypeStruct((dec_batch, CONV_WIDTH - 1, TOKEN_WIDTH), F32),
            jax.ShapeDtypeStruct((wk, wn), BF16),
        ],
        scratch_shapes=[pltpu.VMEM((2 * CARRY_ROWS, TOKEN_WIDTH), F32), pltpu.VMEM((gr, D_MODEL), F32)],
        compiler_params=pltpu.CompilerParams(
            dimension_semantics=("arbitrary",),
            vmem_limit_bytes=_vmem_limit(blocks, scratch)),
        name="conv_sample",
    )(z, state, mem_k, mem_v, conv_w, w_out)


def _band_scores(q, k):
    return _dot_nt(q * (HEAD_DIM ** -0.5), k)


def _band_probs(s, sink, upper, upper_visible):
    s = jnp.where(upper_visible, s[:, :WINDOW], jnp.where(upper, -jnp.inf, s[:, WINDOW:]))
    m = jnp.maximum(jnp.max(s, axis=-1, keepdims=True), sink)
    e = jnp.exp(s - m)
    e = jnp.concatenate([jnp.where(upper, e, 0.0), jnp.where(upper, 0.0, e)], axis=1).astype(BF16)
    return e, jnp.exp(sink - m)


def _band_values(probs, v, ones):
    e, sink_term = probs
    return _dot(e, v) / (_dot(e, ones) + sink_term)


def _swa_prompt_kernel(sink_ref, zq_ref, zp_ref, mk_ref, mv_ref, *refs, blocks_per_seq, cast_steps):
    n_w = (len(refs) - 1) // 2
    w_refs, mix_ref, wb_refs = refs[:n_w], refs[n_w], refs[n_w + 1:]
    s = pl.program_id(0)

    for w_ref, wb_ref, n_cast in zip(w_refs, wb_refs, cast_steps):
        @pl.when(s < n_cast)
        def _():
            wb_ref[...] = w_ref[...].astype(BF16)

    k_off = TOKEN_WIDTH
    v_off = TOKEN_WIDTH + KV_WIDTH
    qm_off = TOKEN_WIDTH + 2 * KV_WIDTH
    n_blocks = zq_ref.shape[0] // WINDOW
    first_has_prev = (s % (blocks_per_seq // n_blocks)) > 0
    row = lax.broadcasted_iota(jnp.int32, (WINDOW, WINDOW), 0)
    col = lax.broadcasted_iota(jnp.int32, (WINDOW, WINDOW), 1)
    upper = col > row
    upper_first = jnp.logical_and(upper, first_has_prev)
    ones = jnp.ones((2 * WINDOW, HEAD_DIM), BF16)

    def block_rows(b):
        return slice(b * WINDOW, (b + 1) * WINDOW)

    @functools.lru_cache(maxsize=None)
    def window(b, kh, col_prev, col_cur):
        lo, hi = kh * HEAD_DIM, (kh + 1) * HEAD_DIM
        prev = (zp_ref[:, col_prev + lo:col_prev + hi] if b == 0
                else zq_ref[block_rows(b - 1), col_cur + lo:col_cur + hi])
        return jnp.concatenate([prev, zq_ref[block_rows(b), col_cur + lo:col_cur + hi]], axis=0)

    def band_scores(i):
        b, h = divmod(i, N_HEADS)
        return _band_scores(zq_ref[block_rows(b), h * HEAD_DIM:(h + 1) * HEAD_DIM], window(b, h // GROUP, 0, k_off))

    def band_finish(i, s):
        b, h = divmod(i, N_HEADS)
        p = _band_probs(s, sink_ref[h], upper, upper_first if b == 0 else upper)
        o = _band_values(p, window(b, h // GROUP, KV_WIDTH, v_off), ones)
        mix_ref[block_rows(b), h * HEAD_DIM:(h + 1) * HEAD_DIM] = o.astype(BF16)

    def cross_scores(h):
        lo, hi = h * MEM_HEAD_DIM, (h + 1) * MEM_HEAD_DIM
        return _cross_scores(zq_ref[:, qm_off + lo:qm_off + hi], mk_ref[pl.ds(h, N_MEM, stride=MEM_HEADS), :].astype(BF16))

    def cross_finish(h, s):
        lo, hi = h * MEM_HEAD_DIM, (h + 1) * MEM_HEAD_DIM
        mix_ref[:, TOKEN_WIDTH + lo:TOKEN_WIDTH + hi] = _cross_values(
            s, mv_ref[pl.ds(h, N_MEM, stride=MEM_HEADS), :].astype(BF16)).astype(BF16)

    _run_ahead(n_blocks * N_HEADS, SCORE_LOOKAHEAD, band_scores, band_finish)
    _run_ahead(MEM_HEADS, SCORE_LOOKAHEAD, cross_scores, cross_finish)


def _swa_prompt(z, mk, mv, sinks, weights, *, layer, batch, seq):
    rows, zc = z.shape
    tq = SWA_ROWS
    blocks_per_seq = seq // WINDOW
    steps_per_seq = seq // tq
    blocks_per_step = tq // WINDOW
    n_steps = batch * steps_per_seq
    kv_col_block = TOKEN_WIDTH // (2 * KV_WIDTH)
    bf16_rows = 2 * V7X_SUBLANES
    blocks = (_nbytes((tq, zc), BF16) + _nbytes((WINDOW, 2 * KV_WIDTH), BF16)
              + 2 * _nbytes((N_MEM, MEM_WIDTH), F32) + _nbytes((tq, D_MODEL), BF16))
    w_in_specs, w_out_specs, w_out_shapes, cast_steps = [], [], [], []
    for w in weights:
        r, c = w.shape[1:]
        n_cast = n_steps if r % (n_steps * bf16_rows) == 0 else n_steps // 2
        slab = r // n_cast
        assert slab * n_cast == r and slab % bf16_rows == 0
        blocks += _nbytes((slab, c), F32) + _nbytes((slab, c), BF16)
        w_in_specs.append(pl.BlockSpec((None, slab, c), lambda s, n=n_cast: (layer, jnp.minimum(s, n - 1), 0)))
        w_out_specs.append(pl.BlockSpec((slab, c), lambda s, n=n_cast: (jnp.minimum(s, n - 1), 0)))
        w_out_shapes.append(jax.ShapeDtypeStruct((r, c), BF16))
        cast_steps.append(n_cast)
    return pl.pallas_call(
        functools.partial(_swa_prompt_kernel, blocks_per_seq=blocks_per_seq, cast_steps=tuple(cast_steps)),
        grid=(n_steps,),
        in_specs=[
            pl.BlockSpec(memory_space=pltpu.SMEM),
            pl.BlockSpec((tq, zc), lambda s: (s, 0)),
            pl.BlockSpec((WINDOW, 2 * KV_WIDTH), lambda s: (jnp.maximum(s * blocks_per_step - 1, 0), kv_col_block)),
            pl.BlockSpec((None, N_MEM * MEM_HEADS, MEM_HEAD_DIM), lambda s: (layer, s // steps_per_seq, 0)),
            pl.BlockSpec((None, N_MEM * MEM_HEADS, MEM_HEAD_DIM), lambda s: (layer, s // steps_per_seq, 0)),
        ] + w_in_specs,
        out_specs=[pl.BlockSpec((tq, D_MODEL), lambda s: (s, 0))] + w_out_specs,
        out_shape=[jax.ShapeDtypeStruct((rows, D_MODEL), BF16)] + w_out_shapes,
        compiler_params=pltpu.CompilerParams(
            dimension_semantics=("arbitrary",),
            vmem_limit_bytes=_vmem_limit(blocks, SCORE_LOOKAHEAD * _nbytes((WINDOW, 2 * WINDOW), F32))),
        name="swa_prompt",
    )(sinks, z, z, mk, mv, *weights)


def _swa_sample_kernel(sink_ref, z_ref, ck_ref, cv_ref, mk_ref, mv_ref, wo_ref,
                       mix_ref, nk_ref, nv_ref, wob_ref, knew_ref, vnew_ref, mixf_ref, *, dec_seq):
    t = dec_seq
    wob_ref[...] = wo_ref[...].astype(BF16)
    k_off = TOKEN_WIDTH
    v_off = TOKEN_WIDTH + KV_WIDTH
    qm_off = TOKEN_WIDTH + 2 * KV_WIDTH
    rows = GROUP * t
    qi = lax.broadcasted_iota(jnp.int32, (rows, WINDOW), 0) % t
    col = lax.broadcasted_iota(jnp.int32, (rows, WINDOW), 1)
    upper = col > qi
    knew_ref[...] = jnp.zeros(knew_ref.shape, F32)
    vnew_ref[...] = jnp.zeros(vnew_ref.shape, F32)
    for n in range(SAMPLE_GROUP):
        r0, r1 = n * t, (n + 1) * t
        k_new = z_ref[r0:r1, k_off:k_off + KV_WIDTH].astype(F32)
        v_new = z_ref[r0:r1, v_off:v_off + KV_WIDTH].astype(F32)
        knew_ref[n, 0:t, :] = k_new
        vnew_ref[n, 0:t, :] = v_new
        nk_ref[n, 0:WINDOW - t, :] = ck_ref[n, t:WINDOW, :]
        nv_ref[n, 0:WINDOW - t, :] = cv_ref[n, t:WINDOW, :]
        nk_ref[n, WINDOW - t:WINDOW, :] = k_new
        nv_ref[n, WINDOW - t:WINDOW, :] = v_new

    def scores(i):
        n, kh = divmod(i, N_KV_HEADS)
        lo, hi = kh * HEAD_DIM, (kh + 1) * HEAD_DIM
        k = jnp.concatenate([ck_ref[n, :, lo:hi], knew_ref[n, :, lo:hi]], axis=0).astype(BF16)
        q = jnp.concatenate(
            [z_ref[n * t:(n + 1) * t, (kh * GROUP + g) * HEAD_DIM:(kh * GROUP + g + 1) * HEAD_DIM].astype(F32)
             for g in range(GROUP)], axis=0).astype(BF16)
        return _band_scores(q, k)

    def finish(i, s):
        n, kh = divmod(i, N_KV_HEADS)
        lo, hi = kh * HEAD_DIM, (kh + 1) * HEAD_DIM
        v = jnp.concatenate([cv_ref[n, :, lo:hi], vnew_ref[n, :, lo:hi]], axis=0).astype(BF16)
        sink = jnp.concatenate(
            [jnp.full((t, 1), sink_ref[kh * GROUP + g], F32) for g in range(GROUP)], axis=0)
        o = _band_values(_band_probs(s, sink, upper, upper), v, jnp.ones((2 * WINDOW, HEAD_DIM), BF16))
        for g in range(GROUP):
            h = kh * GROUP + g
            mixf_ref[n * t:(n + 1) * t, h * HEAD_DIM:(h + 1) * HEAD_DIM] = o[g * t:(g + 1) * t, :]

    _run_ahead(SAMPLE_GROUP * N_KV_HEADS, SAMPLE_LOOKAHEAD, scores, finish)
    _sample_cross_attention(z_ref, mk_ref, mv_ref, mixf_ref, qm_off, t)
    mix_ref[...] = mixf_ref[...].astype(BF16)


def _swa_sample(z, cache_k, cache_v, mem_k, mem_v, sinks, w_out, *, layer, dec_seq):
    rows, zc = z.shape
    dec_batch = cache_k.shape[0]
    g = SAMPLE_GROUP
    gr = g * dec_seq
    n_steps = dec_batch // g
    wk, wn = w_out.shape[1:]
    slab = wk // n_steps
    assert slab * n_steps == wk and slab % (2 * V7X_SUBLANES) == 0
    win = jax.ShapeDtypeStruct((dec_batch, WINDOW, KV_WIDTH), F32)
    blocks = (_nbytes((gr, zc), BF16) + 4 * _nbytes((g, WINDOW, KV_WIDTH), F32)
              + 2 * _nbytes((g, N_MEM, V7X_SUBLANES, MEM_HEAD_DIM), F32) + _nbytes((gr, D_MODEL), BF16)
              + _nbytes((slab, wn), F32) + _nbytes((slab, wn), BF16))
    scratch = 2 * _nbytes((g, WINDOW, KV_WIDTH), F32) + _nbytes((gr, D_MODEL), F32)
    return pl.pallas_call(
        functools.partial(_swa_sample_kernel, dec_seq=dec_seq),
        grid=(n_steps,),
        in_specs=[
            pl.BlockSpec(memory_space=pltpu.SMEM),
            pl.BlockSpec((gr, zc), lambda i: (i, 0)),
            pl.BlockSpec((g, WINDOW, KV_WIDTH), lambda i: (i, 0, 0)),
            pl.BlockSpec((g, WINDOW, KV_WIDTH), lambda i: (i, 0, 0)),
            pl.BlockSpec((None, g, N_MEM * MEM_HEADS, MEM_HEAD_DIM), lambda i: (layer, i, 0, 0)),
            pl.BlockSpec((None, g, N_MEM * MEM_HEADS, MEM_HEAD_DIM), lambda i: (layer, i, 0, 0)),
            pl.BlockSpec((None, slab, wn), lambda i: (0, i, 0)),
        ],
        out_specs=[
            pl.BlockSpec((gr, D_MODEL), lambda i: (i, 0)),
            pl.BlockSpec((g, WINDOW, KV_WIDTH), lambda i: (i, 0, 0)),
            pl.BlockSpec((g, WINDOW, KV_WIDTH), lambda i: (i, 0, 0)),
            pl.BlockSpec((slab, wn), lambda i: (i, 0)),
        ],
        out_shape=[jax.ShapeDtypeStruct((rows, D_MODEL), BF16), win, win,
                   jax.ShapeDtypeStruct((wk, wn), BF16)],
        scratch_shapes=[pltpu.VMEM((g, WINDOW, KV_WIDTH), F32), pltpu.VMEM((g, WINDOW, KV_WIDTH), F32),
                        pltpu.VMEM((gr, D_MODEL), F32)],
        compiler_params=pltpu.CompilerParams(
            dimension_semantics=("arbitrary",),
            vmem_limit_bytes=_vmem_limit(blocks, scratch)),
        name="swa_sample",
    )(sinks, z, cache_k, cache_v, mem_k, mem_v, w_out)


def kernel(x_prompt, x_sample, mem_prompt, state_conv, cache_win_k, cache_win_v, cache_mem_k, cache_mem_v,
           norm_mix, norm_mem, w_mem_kv, norm_ffn, w_gate, w_up, w_down,
           conv_w_in, conv_w, conv_w_out, attn_w_in, attn_sinks, attn_w_out, norm_final):
    batch, seq, d = x_prompt.shape
    dec_batch, dec_seq, _ = x_sample.shape
    depth = norm_mix.shape[0]
    d_ff = w_gate.shape[2]
    prompt_rows = batch * seq
    sample_rows = dec_batch * dec_seq
    assert d == D_MODEL and depth == 2 and seq % CONV_ROWS == 0 and seq % SWA_ROWS == 0 and SWA_ROWS % WINDOW == 0
    assert prompt_rows % PROMPT_TILE == 0 and dec_batch % SAMPLE_GROUP == 0
    assert dec_seq == V7X_SUBLANES and d_ff % FFN_COL_TILE == 0 and d % COL_TILE == 0
    assert d_ff % DOWN_K_TILE == 0 and d % DOWN_X_CHUNK == 0
    assert prompt_rows % FFN_ROW_TILE == 0 and prompt_rows % sample_rows == 0

    xp = x_prompt.reshape(prompt_rows, d)
    xs = x_sample.reshape(sample_rows, d)
    mem = mem_prompt.reshape(batch * N_MEM, d)
    mem_k_s = cache_mem_k.reshape(depth, dec_batch, N_MEM * MEM_HEADS, MEM_HEAD_DIM)
    mem_v_s = cache_mem_v.reshape(depth, dec_batch, N_MEM * MEM_HEADS, MEM_HEAD_DIM)
    g_mix = norm_mix.reshape(depth, 1, d)
    g_ffn = norm_ffn.reshape(depth, 1, d)

    mk, mv = _mem_kv(mem, norm_mem.reshape(depth, 1, d), w_mem_kv)

    hp, hs = _norm_rows(xp, xs, g_mix, layer=0)
    zp, zs, wd_bf = _matmul(hp, hs, conv_w_in, w_layer=0, side=w_down, side_layer=0)
    mix_s, conv_s, wo_bf = _conv_sample(zs, state_conv, mem_k_s, mem_v_s, conv_w, conv_w_out,
                                        layer=0, dec_seq=dec_seq)
    mix_p, conv_p = _conv_prompt(zp, mk, mv, conv_w, layer=0, batch=batch, seq=seq)
    xp, xs, hp, hs = _out_proj(mix_p, mix_s, wo_bf, xp, xs, g_ffn, layer=0)
    ap, as_ = _ffn_up(hp, hs, w_gate, w_up, layer=0)
    xp, xs, hp, hs = _ffn_down(ap, as_, wd_bf[None], xp, xs, g_mix, w_layer=0, g_layer=1, final=False)

    zp, zs = _matmul(hp, hs, attn_w_in, w_layer=0)
    sinks = attn_sinks[0]
    mix_s, win_k_s, win_v_s, wo_bf = _swa_sample(
        zs, cache_win_k[0].reshape(dec_batch, WINDOW, KV_WIDTH), cache_win_v[0].reshape(dec_batch, WINDOW, KV_WIDTH),
        mem_k_s, mem_v_s, sinks, attn_w_out, layer=1, dec_seq=dec_seq)
    mix_p, wd_bf = _swa_prompt(zp, mk, mv, sinks, [w_down], layer=1, batch=batch, seq=seq)
    xp, xs, hp, hs = _out_proj(mix_p, mix_s, wo_bf, xp, xs, g_ffn, layer=1)
    ap, as_ = _ffn_up(hp, hs, w_gate, w_up, layer=1)
    y_prompt, y_sample = _ffn_down(ap, as_, wd_bf[None], xp, xs, norm_final.reshape(1, 1, d),
                                   w_layer=0, g_layer=0, final=True)

    win_p = zp.reshape(batch, seq, -1)[:, seq - WINDOW:, TOKEN_WIDTH:TOKEN_WIDTH + 2 * KV_WIDTH].astype(F32)
    kv_shape = (1, -1, WINDOW, N_KV_HEADS, HEAD_DIM)
    mem_shape = (depth, batch, N_MEM, MEM_HEADS, MEM_HEAD_DIM)
    return (y_prompt.reshape(batch, seq, d),
            y_sample.reshape(dec_batch, dec_seq, d),
            conv_p[None],
            conv_s[None],
            win_p[..., :KV_WIDTH].reshape(kv_shape),
            win_p[..., KV_WIDTH:].reshape(kv_shape),
            win_k_s.reshape(kv_shape),
            win_v_s.reshape(kv_shape),
            mk.reshape(mem_shape),
            mv.reshape(mem_shape))
```

```python
import functools

import jax
import jax.numpy as jnp
from jax import lax
from jax.experimental import pallas as pl
from jax.experimental.pallas import tpu as pltpu

F32 = jnp.float32
BF16 = jnp.bfloat16

D_MODEL = 2048
N_MEM = 256
MEM_HEADS = 4
MEM_WIDTH = D_MODEL // 4
MEM_HEAD_DIM = MEM_WIDTH // MEM_HEADS
TOKEN_WIDTH = D_MODEL - MEM_WIDTH
CONV_WIDTH = 3
WINDOW = 128
HEAD_DIM = 64
N_HEADS = TOKEN_WIDTH // HEAD_DIM
N_KV_HEADS = 4
GROUP = N_HEADS // N_KV_HEADS
KV_WIDTH = N_KV_HEADS * HEAD_DIM
EPS = 1e-6

V7X_VMEM_BYTES = 64 * 1024 * 1024
V7X_SUBLANES = 8
V7X_MXU_DEPTH = 256

PROMPT_TILE = 1024
COL_TILE = 512
DOWN_K_TILE = 1408
DOWN_X_CHUNK = 512
FFN_ROW_TILE = 2048
FFN_COL_TILE = 512
MXU_ROWS = 1024
OUT_ROWS = 512
CONV_ROWS = 512
CONV_CHUNK = 16
SWA_ROWS = 512
SCORE_LOOKAHEAD = 96
SAMPLE_LOOKAHEAD = 32
SAMPLE_GROUP = 8
SIDE_CAST_STEPS = 32
CARRY_ROWS = V7X_SUBLANES
MIN_TEMP_BYTES = 8 << 20
VMEM_RESERVE_BYTES = 6 << 20


def _nbytes(shape, dtype):
    n = 1
    for s in shape:
        n *= s
    return n * jnp.dtype(dtype).itemsize


def _vmem_limit(block_bytes, scratch_bytes):
    need = 2 * block_bytes + scratch_bytes
    temporaries = max(need // 4, MIN_TEMP_BYTES)
    return int(min(need + temporaries, V7X_VMEM_BYTES - VMEM_RESERVE_BYTES))


def _rmsnorm(x, g):
    r = lax.rsqrt(jnp.mean(x * x, axis=-1, keepdims=True) + EPS)
    return (x * r) * g


def _dot(a, b):
    return jnp.dot(a, b, preferred_element_type=F32)


def _dot_nt(a, b):
    return lax.dot_general(a, b, (((1,), (1,)), ((), ())), preferred_element_type=F32)


def _prompt_rows_map(i, j):
    return (i, 0)


def _prompt_tile_map(i, j):
    return (i, j)


def _sample_tile_map(n_pt):
    return lambda i, j: (0, jnp.where(i == n_pt - 1, j, 0))


def _const_map(i, j):
    return (0, 0)


def _norm_rows_kernel(xp_ref, xs_ref, g_ref, hp_ref, hs_ref, *, n_pt):
    hp_ref[...] = _rmsnorm(xp_ref[...], g_ref[...]).astype(BF16)

    @pl.when(pl.program_id(0) == n_pt - 1)
    def _():
        hs_ref[...] = _rmsnorm(xs_ref[...], g_ref[...]).astype(BF16)


def _norm_rows(xp, xs, g, *, layer):
    rp, d = xp.shape
    rs = xs.shape[0]
    tm = OUT_ROWS
    n_pt = rp // tm
    blocks = _nbytes((tm, d), F32) + _nbytes((tm, d), BF16) + _nbytes((rs, d), F32) + _nbytes((rs, d), BF16)
    return pl.pallas_call(
        functools.partial(_norm_rows_kernel, n_pt=n_pt),
        grid=(n_pt,),
        in_specs=[
            pl.BlockSpec((tm, d), lambda s: (s, 0)),
            pl.BlockSpec((rs, d), lambda s: (0, 0)),
            pl.BlockSpec((None, 1, d), lambda s: (layer, 0, 0)),
        ],
        out_specs=[
            pl.BlockSpec((tm, d), lambda s: (s, 0)),
            pl.BlockSpec((rs, d), lambda s: (0, 0)),
        ],
        out_shape=[jax.ShapeDtypeStruct((rp, d), BF16), jax.ShapeDtypeStruct((rs, d), BF16)],
        compiler_params=pltpu.CompilerParams(
            dimension_semantics=("arbitrary",),
            vmem_limit_bytes=_vmem_limit(blocks, 0)),
        name="norm_rows",
    )(xp, xs, g)


def _swiglu(h, wg, wu):
    gate = _dot(h, wg)
    up = _dot(h, wu)
    return (gate * jax.nn.sigmoid(gate) * up).astype(BF16)


def _ffn_up_kernel(hp_ref, hs_ref, wg_ref, wu_ref, ap_ref, as_ref, *, n_pt):
    wg, wu = wg_ref[...].astype(BF16), wu_ref[...].astype(BF16)
    for r in range(0, hp_ref.shape[0], MXU_ROWS):
        ap_ref[r:r + MXU_ROWS, :] = _swiglu(hp_ref[r:r + MXU_ROWS, :], wg, wu)

    @pl.when(pl.program_id(0) == n_pt - 1)
    def _():
        as_ref[...] = _swiglu(hs_ref[...], wg_ref[...].astype(BF16), wu_ref[...].astype(BF16))


def _ffn_up(hp, hs, wg, wu, *, layer):
    rp, d = hp.shape
    rs = hs.shape[0]
    n = wg.shape[2]
    tm, tn = FFN_ROW_TILE, FFN_COL_TILE
    n_pt, n_j = rp // tm, n // tn
    blocks = (_nbytes((tm, d), BF16) + _nbytes((rs, d), BF16) + 2 * _nbytes((d, tn), wg.dtype)
              + _nbytes((tm, tn), BF16) + _nbytes((rs, tn), BF16))
    temps = 2 * _nbytes((d, tn), BF16) + 3 * _nbytes((tm, tn), F32)
    w_spec = pl.BlockSpec((None, d, tn), lambda i, j: (layer, 0, j))
    return pl.pallas_call(
        functools.partial(_ffn_up_kernel, n_pt=n_pt),
        grid=(n_pt, n_j),
        in_specs=[
            pl.BlockSpec((tm, d), _prompt_rows_map),
            pl.BlockSpec((rs, d), _const_map),
            w_spec,
            w_spec,
        ],
        out_specs=[
            pl.BlockSpec((tm, tn), _prompt_tile_map),
            pl.BlockSpec((rs, tn), _sample_tile_map(n_pt)),
        ],
        out_shape=[jax.ShapeDtypeStruct((rp, n), BF16), jax.ShapeDtypeStruct((rs, n), BF16)],
        compiler_params=pltpu.CompilerParams(
            dimension_semantics=("arbitrary", "arbitrary"),
            vmem_limit_bytes=_vmem_limit(blocks, temps)),
        name="ffn_up",
    )(hp, hs, wg, wu)


def _out_proj_kernel(ap_ref, as_ref, w_ref, xp_ref, xs_ref, g_ref, op_ref, os_ref, hp_ref, hs_ref, *, n_pt):
    x = xp_ref[...] + _dot(ap_ref[...], w_ref[...])
    op_ref[...] = x
    hp_ref[...] = _rmsnorm(x, g_ref[...]).astype(BF16)

    @pl.when(pl.program_id(0) == n_pt - 1)
    def _():
        x = xs_ref[...] + _dot(as_ref[...], w_ref[...])
        os_ref[...] = x
        hs_ref[...] = _rmsnorm(x, g_ref[...]).astype(BF16)


def _out_proj(ap, as_, w, xp, xs, g, *, layer):
    rp, k = ap.shape
    rs = as_.shape[0]
    n = w.shape[1]
    tm = OUT_ROWS
    n_pt = rp // tm
    rows_map = lambda s: (s, 0)
    const = lambda s: (0, 0)
    once = dict(pipeline_mode=pl.Buffered(1))
    blocks = _nbytes((tm, k), BF16) + 2 * _nbytes((tm, n), F32) + _nbytes((tm, n), BF16)
    resident = (_nbytes((k, n), BF16) + _nbytes((rs, k), BF16) + _nbytes((rs, n), F32)
                + 2 * (_nbytes((rs, n), F32) + _nbytes((rs, n), BF16)))
    return pl.pallas_call(
        functools.partial(_out_proj_kernel, n_pt=n_pt),
        grid=(n_pt,),
        in_specs=[
            pl.BlockSpec((tm, k), rows_map),
            pl.BlockSpec((rs, k), const, **once),
            pl.BlockSpec((k, n), const, **once),
            pl.BlockSpec((tm, n), rows_map),
            pl.BlockSpec((rs, n), const, **once),
            pl.BlockSpec((None, 1, n), lambda s: (layer, 0, 0)),
        ],
        out_specs=[
            pl.BlockSpec((tm, n), rows_map),
            pl.BlockSpec((rs, n), const),
            pl.BlockSpec((tm, n), rows_map),
            pl.BlockSpec((rs, n), const),
        ],
        out_shape=[jax.ShapeDtypeStruct((rp, n), F32), jax.ShapeDtypeStruct((rs, n), F32),
                   jax.ShapeDtypeStruct((rp, n), BF16), jax.ShapeDtypeStruct((rs, n), BF16)],
        compiler_params=pltpu.CompilerParams(
            dimension_semantics=("arbitrary",),
            vmem_limit_bytes=_vmem_limit(blocks, resident + 2 * _nbytes((tm, n), F32))),
        name="out_proj",
    )(ap, as_, w, xp, xs, g)


def _matmul_kernel(hp_ref, hs_ref, w_ref, *refs, n_pt, n_j, n_cast):
    if n_cast:
        side_ref, zp_ref, zs_ref, side_bf_ref = refs

        @pl.when(pl.program_id(0) * n_j + pl.program_id(1) < n_cast)
        def _():
            side_bf_ref[...] = side_ref[...].astype(BF16)
    else:
        zp_ref, zs_ref = refs

    w = w_ref[...].astype(BF16)
    for r in range(0, hp_ref.shape[0], MXU_ROWS):
        zp_ref[r:r + MXU_ROWS, :] = _dot(hp_ref[r:r + MXU_ROWS, :], w).astype(zp_ref.dtype)

    @pl.when(pl.program_id(0) == n_pt - 1)
    def _():
        zs_ref[...] = _dot(hs_ref[...], w_ref[...].astype(BF16)).astype(zs_ref.dtype)


def _matmul(hp, hs, w, *, w_layer, side=None, side_layer=0):
    rp, d = hp.shape
    rs = hs.shape[0]
    n = w.shape[2]
    tm, tn = FFN_ROW_TILE, COL_TILE
    n_pt, n_j = rp // tm, n // tn
    blocks = (_nbytes((tm, d), BF16) + _nbytes((rs, d), BF16) + _nbytes((d, tn), F32)
              + _nbytes((tm, tn), BF16) + _nbytes((rs, tn), BF16))
    temps = _nbytes((d, tn), BF16) + _nbytes((tm, tn), F32)
    in_specs = [
        pl.BlockSpec((tm, d), _prompt_rows_map),
        pl.BlockSpec((rs, d), _const_map),
        pl.BlockSpec((None, d, tn), lambda i, j: (w_layer, 0, j)),
    ]
    out_specs = [
        pl.BlockSpec((tm, tn), _prompt_tile_map),
        pl.BlockSpec((rs, tn), _sample_tile_map(n_pt)),
    ]
    out_shape = [jax.ShapeDtypeStruct((rp, n), BF16), jax.ShapeDtypeStruct((rs, n), BF16)]
    operands = [hp, hs, w]
    n_cast = 0
    if side is not None:
        r, c = side.shape[1:]
        n_cast = SIDE_CAST_STEPS
        slab = r // n_cast
        assert n_cast <= n_pt * n_j and slab * n_cast == r and slab % (2 * V7X_SUBLANES) == 0
        slab_idx = lambda i, j: jnp.minimum(i * n_j + j, n_cast - 1)
        in_specs.append(pl.BlockSpec((None, slab, c), lambda i, j: (side_layer, slab_idx(i, j), 0)))
        out_specs.append(pl.BlockSpec((slab, c), lambda i, j: (slab_idx(i, j), 0)))
        out_shape.append(jax.ShapeDtypeStruct((r, c), BF16))
        operands.append(side)
        blocks += _nbytes((slab, c), F32) + _nbytes((slab, c), BF16)
    return pl.pallas_call(
        functools.partial(_matmul_kernel, n_pt=n_pt, n_j=n_j, n_cast=n_cast),
        grid=(n_pt, n_j),
        in_specs=in_specs,
        out_specs=out_specs,
        out_shape=out_shape,
        compiler_params=pltpu.CompilerParams(
            dimension_semantics=("arbitrary", "arbitrary"),
            vmem_limit_bytes=_vmem_limit(blocks, temps)),
        name="matmul",
    )(*operands)


def _ffn_down_kernel(ap_ref, as_ref, w_ref, xp_ref, xs_ref, g_ref, *refs, n_pt, n_k, n_xc, final):
    i, k = pl.program_id(0), pl.program_id(1)
    if final:
        op_ref, os_ref, apl_ref, asl_ref, wl_ref = refs
    else:
        op_ref, os_ref, hp_ref, hs_ref, apl_ref, asl_ref, wl_ref = refs
    xc = xp_ref.shape[1]
    kf = (ap_ref.shape[1] // V7X_MXU_DEPTH) * V7X_MXU_DEPTH
    even = k % 2 == 0

    def whole_passes(a_ref):
        return _dot(a_ref[:, :kf], w_ref[:kf, :])

    def with_stash(a_ref, al_ref):
        return _dot(jnp.concatenate([al_ref[...], a_ref[...]], axis=1),
                    jnp.concatenate([wl_ref[...], w_ref[...]], axis=0))

    for kk in range(n_k):
        @pl.when(k == kk)
        def _():
            cols = slice(kk * xc, (kk + 1) * xc)
            if 0 < kk < n_xc:
                op_ref[:, cols] = op_ref[:, cols] + xp_ref[...]
            contrib = whole_passes(ap_ref) if kk % 2 == 0 else with_stash(ap_ref, apl_ref)
            if kk == 0:
                op_ref[...] = contrib
                op_ref[:, cols] = op_ref[:, cols] + xp_ref[...]
            else:
                op_ref[...] = op_ref[...] + contrib
            if kk == n_k - 1:
                if final:
                    op_ref[...] = _rmsnorm(op_ref[...], g_ref[...])
                else:
                    hp_ref[...] = _rmsnorm(op_ref[...], g_ref[...]).astype(BF16)

    @pl.when(i == n_pt - 1)
    def _():
        @pl.when(k == 0)
        def _():
            os_ref[...] = xs_ref[...] + whole_passes(as_ref)

        @pl.when(jnp.logical_and(k > 0, even))
        def _():
            os_ref[...] = os_ref[...] + whole_passes(as_ref)

        @pl.when(jnp.logical_not(even))
        def _():
            os_ref[...] = os_ref[...] + with_stash(as_ref, asl_ref)

        @pl.when(even)
        def _():
            asl_ref[...] = as_ref[:, kf:]

        @pl.when(k == n_k - 1)
        def _():
            if final:
                os_ref[...] = _rmsnorm(os_ref[...], g_ref[...])
            else:
                hs_ref[...] = _rmsnorm(os_ref[...], g_ref[...]).astype(BF16)

    @pl.when(even)
    def _():
        apl_ref[...] = ap_ref[:, kf:]
        wl_ref[...] = w_ref[kf:, :]


def _ffn_down(ap, as_, w, xp, xs, g, *, w_layer, g_layer, final):
    rp, kdim = ap.shape
    rs = as_.shape[0]
    n = w.shape[2]
    tm, tk, xc = PROMPT_TILE, DOWN_K_TILE, DOWN_X_CHUNK
    n_pt, n_k, n_xc = rp // tm, kdim // tk, n // xc
    k_left = tk % V7X_MXU_DEPTH
    assert n_xc <= n_k and n_k % 2 == 0 and 2 * k_left == V7X_MXU_DEPTH and w.dtype == BF16
    rows = lambda i, k: (i, 0)
    blocks = (_nbytes((tm, tk), BF16) + _nbytes((rs, tk), BF16) + _nbytes((tk, n), w.dtype)
              + _nbytes((tm, xc), F32) + _nbytes((rs, n), F32)
              + _nbytes((tm, n), F32) + _nbytes((rs, n), F32))
    out_specs = [pl.BlockSpec((tm, n), rows), pl.BlockSpec((rs, n), _const_map)]
    out_shape = [jax.ShapeDtypeStruct((rp, n), F32), jax.ShapeDtypeStruct((rs, n), F32)]
    if not final:
        blocks += _nbytes((tm, n), BF16) + _nbytes((rs, n), BF16)
        out_specs += [pl.BlockSpec((tm, n), rows), pl.BlockSpec((rs, n), _const_map)]
        out_shape += [jax.ShapeDtypeStruct((rp, n), BF16), jax.ShapeDtypeStruct((rs, n), BF16)]
    return pl.pallas_call(
        functools.partial(_ffn_down_kernel, n_pt=n_pt, n_k=n_k, n_xc=n_xc, final=final),
        grid=(n_pt, n_k),
        in_specs=[
            pl.BlockSpec((tm, tk), lambda i, k: (i, k)),
            pl.BlockSpec((rs, tk), lambda i, k: (0, jnp.where(i == n_pt - 1, k, 0))),
            pl.BlockSpec((None, tk, n), lambda i, k: (w_layer, k, 0)),
            pl.BlockSpec((tm, xc), lambda i, k: (i, jnp.minimum(k, n_xc - 1))),
            pl.BlockSpec((rs, n), _const_map),
            pl.BlockSpec((None, 1, n), lambda i, k: (g_layer, 0, 0)),
        ],
        out_specs=out_specs,
        out_shape=out_shape,
        scratch_shapes=[pltpu.VMEM((tm, k_left), BF16), pltpu.VMEM((rs, k_left), BF16),
                        pltpu.VMEM((k_left, n), BF16)],
        compiler_params=pltpu.CompilerParams(
            dimension_semantics=("arbitrary", "arbitrary"),
            vmem_limit_bytes=_vmem_limit(blocks, _nbytes((tk, n), BF16))),
        name="ffn_down",
    )(ap, as_, w, xp, xs, g)


def _mem_kv_kernel(x_ref, g_ref, w_ref, k_ref, v_ref):
    h = _rmsnorm(x_ref[...], g_ref[...]).astype(BF16)
    kv = _dot(h, w_ref[...].astype(BF16))
    tm = x_ref.shape[0]
    for hd in range(MEM_HEADS):
        head_rows = pl.ds(hd, tm, stride=MEM_HEADS)
        k_ref[head_rows, :] = kv[:, hd * MEM_HEAD_DIM:(hd + 1) * MEM_HEAD_DIM]
        v_ref[head_rows, :] = kv[:, MEM_WIDTH + hd * MEM_HEAD_DIM:MEM_WIDTH + (hd + 1) * MEM_HEAD_DIM]


def _mem_kv(mem, g, w):
    rows, d = mem.shape
    depth = w.shape[0]
    tm = 512
    out = jax.ShapeDtypeStruct((depth, rows * MEM_HEADS, MEM_HEAD_DIM), F32)
    blocks = (_nbytes((tm, d), F32) + _nbytes((d, 2 * MEM_WIDTH), F32) + 2 * _nbytes((tm, MEM_WIDTH), F32))
    return pl.pallas_call(
        _mem_kv_kernel,
        grid=(depth, rows // tm),
        in_specs=[
            pl.BlockSpec((tm, d), lambda l, i: (i, 0)),
            pl.BlockSpec((None, 1, d), lambda l, i: (l, 0, 0)),
            pl.BlockSpec((None, d, 2 * MEM_WIDTH), lambda l, i: (l, 0, 0)),
        ],
        out_specs=[
            pl.BlockSpec((None, tm * MEM_HEADS, MEM_HEAD_DIM), lambda l, i: (l, i, 0)),
            pl.BlockSpec((None, tm * MEM_HEADS, MEM_HEAD_DIM), lambda l, i: (l, i, 0)),
        ],
        out_shape=[out, out],
        compiler_params=pltpu.CompilerParams(
            dimension_semantics=("arbitrary", "arbitrary"),
            vmem_limit_bytes=_vmem_limit(blocks, _nbytes((d, 2 * MEM_WIDTH), BF16))),
        name="mem_kv",
    )(mem, g, w)


def _cross_scores(q, k):
    return _dot_nt(q, k) * (MEM_HEAD_DIM ** -0.5)


def _cross_values(s, v):
    e = jnp.exp(s - jnp.max(s, axis=-1, keepdims=True)).astype(BF16)
    return _dot(e, v) / _dot(e, jnp.ones(v.shape, BF16))


def _run_ahead(n_items, lookahead, first, second):
    pending = {}
    for i in range(n_items + lookahead):
        if i < n_items:
            pending[i] = first(i)
        if i >= lookahead:
            second(i - lookahead, pending.pop(i - lookahead))


def _conv_prompt_kernel(z_ref, mk_ref, mv_ref, cw_ref, mix_ref, st_ref, ext_ref, *, tiles_per_seq):
    s = pl.program_id(0)
    tq = z_ref.shape[0]

    @pl.when(s % tiles_per_seq == 0)
    def _():
        ext_ref[0:CARRY_ROWS, :] = jnp.zeros((CARRY_ROWS, TOKEN_WIDTH), F32)

    def scores(h):
        lo, hi = h * MEM_HEAD_DIM, (h + 1) * MEM_HEAD_DIM
        return _cross_scores(z_ref[:, 3 * TOKEN_WIDTH + lo:3 * TOKEN_WIDTH + hi],
                             mk_ref[pl.ds(h, N_MEM, stride=MEM_HEADS), :].astype(BF16))

    def finish(h, s):
        lo, hi = h * MEM_HEAD_DIM, (h + 1) * MEM_HEAD_DIM
        mix_ref[:, TOKEN_WIDTH + lo:TOKEN_WIDTH + hi] = _cross_values(
            s, mv_ref[pl.ds(h, N_MEM, stride=MEM_HEADS), :].astype(BF16)).astype(BF16)

    for r in range(0, tq, CONV_CHUNK):
        rows = slice(r, r + CONV_CHUNK)
        c = z_ref[rows, TOKEN_WIDTH:2 * TOKEN_WIDTH].astype(F32)
        u = z_ref[rows, 2 * TOKEN_WIDTH:3 * TOKEN_WIDTH].astype(F32)
        cu = c * u
        ext_ref[CARRY_ROWS + r:CARRY_ROWS + r + CONV_CHUNK, :] = cu
        conv = (cw_ref[0:1, :] * ext_ref[CARRY_ROWS - 2 + r:CARRY_ROWS - 2 + r + CONV_CHUNK, :]
                + cw_ref[1:2, :] * ext_ref[CARRY_ROWS - 1 + r:CARRY_ROWS - 1 + r + CONV_CHUNK, :]
                + cw_ref[2:3, :] * cu)
        b = z_ref[rows, 0:TOKEN_WIDTH].astype(F32)
        mix_ref[rows, 0:TOKEN_WIDTH] = (b * conv).astype(BF16)
    st_ref[...] = ext_ref[CARRY_ROWS + tq - 2:CARRY_ROWS + tq, :]
    ext_ref[0:CARRY_ROWS, :] = ext_ref[tq:tq + CARRY_ROWS, :]
    _run_ahead(MEM_HEADS, SCORE_LOOKAHEAD, scores, finish)


def _conv_prompt(z, mk, mv, conv_w, *, layer, batch, seq):
    rows, zc = z.shape
    tq = CONV_ROWS
    tiles_per_seq = seq // tq
    n_steps = batch * tiles_per_seq
    blocks = _nbytes((tq, zc), BF16) + 2 * _nbytes((N_MEM, MEM_WIDTH), F32) + _nbytes((tq, D_MODEL), BF16)
    scratch = _nbytes((tq + CARRY_ROWS, TOKEN_WIDTH), F32)
    return pl.pallas_call(
        functools.partial(_conv_prompt_kernel, tiles_per_seq=tiles_per_seq),
        grid=(n_steps,),
        in_specs=[
            pl.BlockSpec((tq, zc), lambda s: (s, 0)),
            pl.BlockSpec((None, N_MEM * MEM_HEADS, MEM_HEAD_DIM), lambda s: (layer, s // tiles_per_seq, 0)),
            pl.BlockSpec((None, N_MEM * MEM_HEADS, MEM_HEAD_DIM), lambda s: (layer, s // tiles_per_seq, 0)),
            pl.BlockSpec((None, CONV_WIDTH, TOKEN_WIDTH), lambda s: (0, 0, 0)),
        ],
        out_specs=[
            pl.BlockSpec((tq, D_MODEL), lambda s: (s, 0)),
            pl.BlockSpec((None, CONV_WIDTH - 1, TOKEN_WIDTH), lambda s: (s // tiles_per_seq, 0, 0)),
        ],
        out_shape=[
            jax.ShapeDtypeStruct((rows, D_MODEL), BF16),
            jax.ShapeDtypeStruct((batch, CONV_WIDTH - 1, TOKEN_WIDTH), F32),
        ],
        scratch_shapes=[pltpu.VMEM((tq + CARRY_ROWS, TOKEN_WIDTH), F32)],
        compiler_params=pltpu.CompilerParams(
            dimension_semantics=("arbitrary",),
            vmem_limit_bytes=_vmem_limit(blocks, scratch + 6 * _nbytes((tq, TOKEN_WIDTH), F32))),
        name="conv_prompt",
    )(z, mk, mv, conv_w)


def _conv_sample_kernel(z_ref, st_ref, mk_ref, mv_ref, cw_ref, wo_ref, mix_ref, nst_ref, wob_ref,
                        ext_ref, mixf_ref, *, dec_seq):
    t = dec_seq
    wob_ref[...] = wo_ref[...].astype(BF16)
    for n in range(SAMPLE_GROUP):
        r0, r1 = n * t, (n + 1) * t
        c = z_ref[r0:r1, TOKEN_WIDTH:2 * TOKEN_WIDTH].astype(F32)
        u = z_ref[r0:r1, 2 * TOKEN_WIDTH:3 * TOKEN_WIDTH].astype(F32)
        cu = c * u
        ext_ref[CARRY_ROWS - 2:CARRY_ROWS, :] = st_ref[n]
        ext_ref[CARRY_ROWS:CARRY_ROWS + t, :] = cu
        conv = (cw_ref[0:1, :] * ext_ref[CARRY_ROWS - 2:CARRY_ROWS - 2 + t, :]
                + cw_ref[1:2, :] * ext_ref[CARRY_ROWS - 1:CARRY_ROWS - 1 + t, :]
                + cw_ref[2:3, :] * cu)
        b = z_ref[r0:r1, 0:TOKEN_WIDTH].astype(F32)
        mixf_ref[r0:r1, 0:TOKEN_WIDTH] = b * conv
        nst_ref[n] = ext_ref[CARRY_ROWS + t - 2:CARRY_ROWS + t, :]
    _sample_cross_attention(z_ref, mk_ref, mv_ref, mixf_ref, 3 * TOKEN_WIDTH, t)
    mix_ref[...] = mixf_ref[...].astype(BF16)


def _sample_cross_attention(z_ref, mk_ref, mv_ref, mixf_ref, qm_off, t):
    def scores(i):
        n, h = divmod(i, MEM_HEADS)
        q = z_ref[n * t:(n + 1) * t, qm_off + h * MEM_HEAD_DIM:qm_off + (h + 1) * MEM_HEAD_DIM]
        return _cross_scores(q, mk_ref[n, pl.ds(h, N_MEM, stride=MEM_HEADS), :].astype(BF16))

    def finish(i, s):
        n, h = divmod(i, MEM_HEADS)
        o = _cross_values(s, mv_ref[n, pl.ds(h, N_MEM, stride=MEM_HEADS), :].astype(BF16))
        mixf_ref[n * t:(n + 1) * t, TOKEN_WIDTH + h * MEM_HEAD_DIM:TOKEN_WIDTH + (h + 1) * MEM_HEAD_DIM] = o

    _run_ahead(SAMPLE_GROUP * MEM_HEADS, SAMPLE_LOOKAHEAD, scores, finish)


def _conv_sample(z, state, mem_k, mem_v, conv_w, w_out, *, layer, dec_seq):
    rows, zc = z.shape
    dec_batch = state.shape[1]
    g = SAMPLE_GROUP
    gr = g * dec_seq
    n_steps = dec_batch // g
    wk, wn = w_out.shape[1:]
    slab = wk // n_steps
    assert slab * n_steps == wk and slab % (2 * V7X_SUBLANES) == 0
    blocks = (_nbytes((gr, zc), BF16) + 2 * _nbytes((g, CONV_WIDTH - 1, TOKEN_WIDTH), F32)
              + 2 * _nbytes((g, N_MEM, V7X_SUBLANES, MEM_HEAD_DIM), F32) + _nbytes((gr, D_MODEL), BF16)
              + _nbytes((slab, wn), F32) + _nbytes((slab, wn), BF16))
    scratch = _nbytes((2 * CARRY_ROWS, TOKEN_WIDTH), F32) + _nbytes((gr, D_MODEL), F32)
    return pl.pallas_call(
        functools.partial(_conv_sample_kernel, dec_seq=dec_seq),
        grid=(n_steps,),
        in_specs=[
            pl.BlockSpec((gr, zc), lambda i: (i, 0)),
            pl.BlockSpec((None, g, CONV_WIDTH - 1, TOKEN_WIDTH), lambda i: (0, i, 0, 0)),
            pl.BlockSpec((None, g, N_MEM * MEM_HEADS, MEM_HEAD_DIM), lambda i: (layer, i, 0, 0)),
            pl.BlockSpec((None, g, N_MEM * MEM_HEADS, MEM_HEAD_DIM), lambda i: (layer, i, 0, 0)),
            pl.BlockSpec((None, CONV_WIDTH, TOKEN_WIDTH), lambda i: (0, 0, 0)),
            pl.BlockSpec((None, slab, wn), lambda i: (0, i, 0)),
        ],
        out_specs=[
            pl.BlockSpec((gr, D_MODEL), lambda i: (i, 0)),
            pl.BlockSpec((g, CONV_WIDTH - 1, TOKEN_WIDTH), lambda i: (i, 0, 0)),
            pl.BlockSpec((slab, wn), lambda i: (i, 0)),
        ],
        out_shape=[
            jax.ShapeDtypeStruct((rows, D_MODEL), BF16),
            jax.ShapeDtypeStruct((dec_batch, CONV_WIDTH - 1, TOKEN_WIDTH), F32),
            jax.ShapeDtypeStruct((wk, wn), BF16),
        ],
        scratch_shapes=[pltpu.VMEM((2 * CARRY_ROWS, TOKEN_WIDTH), F32), pltpu.VMEM((gr, D_MODEL), F32)],
        compiler_params=pltpu.CompilerParams(
            dimension_semantics=("arbitrary",),
            vmem_limit_bytes=_vmem_limit(blocks, scratch)),
        name="conv_sample",
    )(z, state, mem_k, mem_v, conv_w, w_out)


def _band_scores(q, k):
    return _dot_nt(q * (HEAD_DIM ** -0.5), k)


def _band_probs(s, sink, upper, upper_visible):
    s = jnp.where(upper_visible, s[:, :WINDOW], jnp.where(upper, -jnp.inf, s[:, WINDOW:]))
    m = jnp.maximum(jnp.max(s, axis=-1, keepdims=True), sink)
    e = jnp.exp(s - m)
    e = jnp.concatenate([jnp.where(upper, e, 0.0), jnp.where(upper, 0.0, e)], axis=1).astype(BF16)
    return e, jnp.exp(sink - m)


def _band_values(probs, v, ones):
    e, sink_term = probs
    return _dot(e, v) / (_dot(e, ones) + sink_term)


def _swa_prompt_kernel(sink_ref, zq_ref, zp_ref, mk_ref, mv_ref, *refs, blocks_per_seq, cast_steps):
    n_w = (len(refs) - 1) // 2
    w_refs, mix_ref, wb_refs = refs[:n_w], refs[n_w], refs[n_w + 1:]
    s = pl.program_id(0)

    for w_ref, wb_ref, n_cast in zip(w_refs, wb_refs, cast_steps):
        @pl.when(s < n_cast)
        def _():
            wb_ref[...] = w_ref[...].astype(BF16)

    k_off = TOKEN_WIDTH
    v_off = TOKEN_WIDTH + KV_WIDTH
    qm_off = TOKEN_WIDTH + 2 * KV_WIDTH
    n_blocks = zq_ref.shape[0] // WINDOW
    first_has_prev = (s % (blocks_per_seq // n_blocks)) > 0
    row = lax.broadcasted_iota(jnp.int32, (WINDOW, WINDOW), 0)
    col = lax.broadcasted_iota(jnp.int32, (WINDOW, WINDOW), 1)
    upper = col > row
    upper_first = jnp.logical_and(upper, first_has_prev)
    ones = jnp.ones((2 * WINDOW, HEAD_DIM), BF16)

    def block_rows(b):
        return slice(b * WINDOW, (b + 1) * WINDOW)

    @functools.lru_cache(maxsize=None)
    def window(b, kh, col_prev, col_cur):
        lo, hi = kh * HEAD_DIM, (kh + 1) * HEAD_DIM
        prev = (zp_ref[:, col_prev + lo:col_prev + hi] if b == 0
                else zq_ref[block_rows(b - 1), col_cur + lo:col_cur + hi])
        return jnp.concatenate([prev, zq_ref[block_rows(b), col_cur + lo:col_cur + hi]], axis=0)

    def band_scores(i):
        b, h = divmod(i, N_HEADS)
        return _band_scores(zq_ref[block_rows(b), h * HEAD_DIM:(h + 1) * HEAD_DIM], window(b, h // GROUP, 0, k_off))

    def band_finish(i, s):
        b, h = divmod(i, N_HEADS)
        p = _band_probs(s, sink_ref[h], upper, upper_first if b == 0 else upper)
        o = _band_values(p, window(b, h // GROUP, KV_WIDTH, v_off), ones)
        mix_ref[block_rows(b), h * HEAD_DIM:(h + 1) * HEAD_DIM] = o.astype(BF16)

    def cross_scores(h):
        lo, hi = h * MEM_HEAD_DIM, (h + 1) * MEM_HEAD_DIM
        return _cross_scores(zq_ref[:, qm_off + lo:qm_off + hi], mk_ref[pl.ds(h, N_MEM, stride=MEM_HEADS), :].astype(BF16))

    def cross_finish(h, s):
        lo, hi = h * MEM_HEAD_DIM, (h + 1) * MEM_HEAD_DIM
        mix_ref[:, TOKEN_WIDTH + lo:TOKEN_WIDTH + hi] = _cross_values(
            s, mv_ref[pl.ds(h, N_MEM, stride=MEM_HEADS), :].astype(BF16)).astype(BF16)

    _run_ahead(n_blocks * N_HEADS, SCORE_LOOKAHEAD, band_scores, band_finish)
    _run_ahead(MEM_HEADS, SCORE_LOOKAHEAD, cross_scores, cross_finish)


def _swa_prompt(z, mk, mv, sinks, weights, *, layer, batch, seq):
    rows, zc = z.shape
    tq = SWA_ROWS
    blocks_per_seq = seq // WINDOW
    steps_per_seq = seq // tq
    blocks_per_step = tq // WINDOW
    n_steps = batch * steps_per_seq
    kv_col_block = TOKEN_WIDTH // (2 * KV_WIDTH)
    bf16_rows = 2 * V7X_SUBLANES
    blocks = (_nbytes((tq, zc), BF16) + _nbytes((WINDOW, 2 * KV_WIDTH), BF16)
              + 2 * _nbytes((N_MEM, MEM_WIDTH), F32) + _nbytes((tq, D_MODEL), BF16))
    w_in_specs, w_out_specs, w_out_shapes, cast_steps = [], [], [], []
    for w in weights:
        r, c = w.shape[1:]
        n_cast = n_steps if r % (n_steps * bf16_rows) == 0 else n_steps // 2
        slab = r // n_cast
        assert slab * n_cast == r and slab % bf16_rows == 0
        blocks += _nbytes((slab, c), F32) + _nbytes((slab, c), BF16)
        w_in_specs.append(pl.BlockSpec((None, slab, c), lambda s, n=n_cast: (layer, jnp.minimum(s, n - 1), 0)))
        w_out_specs.append(pl.BlockSpec((slab, c), lambda s, n=n_cast: (jnp.minimum(s, n - 1), 0)))
        w_out_shapes.append(jax.ShapeDtypeStruct((r, c), BF16))
        cast_steps.append(n_cast)
    return pl.pallas_call(
        functools.partial(_swa_prompt_kernel, blocks_per_seq=blocks_per_seq, cast_steps=tuple(cast_steps)),
        grid=(n_steps,),
        in_specs=[
            pl.BlockSpec(memory_space=pltpu.SMEM),
            pl.BlockSpec((tq, zc), lambda s: (s, 0)),
            pl.BlockSpec((WINDOW, 2 * KV_WIDTH), lambda s: (jnp.maximum(s * blocks_per_step - 1, 0), kv_col_block)),
            pl.BlockSpec((None, N_MEM * MEM_HEADS, MEM_HEAD_DIM), lambda s: (layer, s // steps_per_seq, 0)),
            pl.BlockSpec((None, N_MEM * MEM_HEADS, MEM_HEAD_DIM), lambda s: (layer, s // steps_per_seq, 0)),
        ] + w_in_specs,
        out_specs=[pl.BlockSpec((tq, D_MODEL), lambda s: (s, 0))] + w_out_specs,
        out_shape=[jax.ShapeDtypeStruct((rows, D_MODEL), BF16)] + w_out_shapes,
        compiler_params=pltpu.CompilerParams(
            dimension_semantics=("arbitrary",),
            vmem_limit_bytes=_vmem_limit(blocks, SCORE_LOOKAHEAD * _nbytes((WINDOW, 2 * WINDOW), F32))),
        name="swa_prompt",
    )(sinks, z, z, mk, mv, *weights)


def _swa_sample_kernel(sink_ref, z_ref, ck_ref, cv_ref, mk_ref, mv_ref, wo_ref,
                       mix_ref, nk_ref, nv_ref, wob_ref, knew_ref, vnew_ref, mixf_ref, *, dec_seq):
    t = dec_seq
    wob_ref[...] = wo_ref[...].astype(BF16)
    k_off = TOKEN_WIDTH
    v_off = TOKEN_WIDTH + KV_WIDTH
    qm_off = TOKEN_WIDTH + 2 * KV_WIDTH
    rows = GROUP * t
    qi = lax.broadcasted_iota(jnp.int32, (rows, WINDOW), 0) % t
    col = lax.broadcasted_iota(jnp.int32, (rows, WINDOW), 1)
    upper = col > qi
    knew_ref[...] = jnp.zeros(knew_ref.shape, F32)
    vnew_ref[...] = jnp.zeros(vnew_ref.shape, F32)
    for n in range(SAMPLE_GROUP):
        r0, r1 = n * t, (n + 1) * t
        k_new = z_ref[r0:r1, k_off:k_off + KV_WIDTH].astype(F32)
        v_new = z_ref[r0:r1, v_off:v_off + KV_WIDTH].astype(F32)
        knew_ref[n, 0:t, :] = k_new
        vnew_ref[n, 0:t, :] = v_new
        nk_ref[n, 0:WINDOW - t, :] = ck_ref[n, t:WINDOW, :]
        nv_ref[n, 0:WINDOW - t, :] = cv_ref[n, t:WINDOW, :]
        nk_ref[n, WINDOW - t:WINDOW, :] = k_new
        nv_ref[n, WINDOW - t:WINDOW, :] = v_new

    def scores(i):
        n, kh = divmod(i, N_KV_HEADS)
        lo, hi = kh * HEAD_DIM, (kh + 1) * HEAD_DIM
        k = jnp.concatenate([ck_ref[n, :, lo:hi], knew_ref[n, :, lo:hi]], axis=0).astype(BF16)
        q = jnp.concatenate(
            [z_ref[n * t:(n + 1) * t, (kh * GROUP + g) * HEAD_DIM:(kh * GROUP + g + 1) * HEAD_DIM].astype(F32)
             for g in range(GROUP)], axis=0).astype(BF16)
        return _band_scores(q, k)

    def finish(i, s):
        n, kh = divmod(i, N_KV_HEADS)
        lo, hi = kh * HEAD_DIM, (kh + 1) * HEAD_DIM
        v = jnp.concatenate([cv_ref[n, :, lo:hi], vnew_ref[n, :, lo:hi]], axis=0).astype(BF16)
        sink = jnp.concatenate(
            [jnp.full((t, 1), sink_ref[kh * GROUP + g], F32) for g in range(GROUP)], axis=0)
        o = _band_values(_band_probs(s, sink, upper, upper), v, jnp.ones((2 * WINDOW, HEAD_DIM), BF16))
        for g in range(GROUP):
            h = kh * GROUP + g
            mixf_ref[n * t:(n + 1) * t, h * HEAD_DIM:(h + 1) * HEAD_DIM] = o[g * t:(g + 1) * t, :]

    _run_ahead(SAMPLE_GROUP * N_KV_HEADS, SAMPLE_LOOKAHEAD, scores, finish)
    _sample_cross_attention(z_ref, mk_ref, mv_ref, mixf_ref, qm_off, t)
    mix_ref[...] = mixf_ref[...].astype(BF16)


def _swa_sample(z, cache_k, cache_v, mem_k, mem_v, sinks, w_out, *, layer, dec_seq):
    rows, zc = z.shape
    dec_batch = cache_k.shape[0]
    g = SAMPLE_GROUP
    gr = g * dec_seq
    n_steps = dec_batch // g
    wk, wn = w_out.shape[1:]
    slab = wk // n_steps
    assert slab * n_steps == wk and slab % (2 * V7X_SUBLANES) == 0
    win = jax.ShapeDtypeStruct((dec_batch, WINDOW, KV_WIDTH), F32)
    blocks = (_nbytes((gr, zc), BF16) + 4 * _nbytes((g, WINDOW, KV_WIDTH), F32)
              + 2 * _nbytes((g, N_MEM, V7X_SUBLANES, MEM_HEAD_DIM), F32) + _nbytes((gr, D_MODEL), BF16)
              + _nbytes((slab, wn), F32) + _nbytes((slab, wn), BF16))
    scratch = 2 * _nbytes((g, WINDOW, KV_WIDTH), F32) + _nbytes((gr, D_MODEL), F32)
    return pl.pallas_call(
        functools.partial(_swa_sample_kernel, dec_seq=dec_seq),
        grid=(n_steps,),
        in_specs=[
            pl.BlockSpec(memory_space=pltpu.SMEM),
            pl.BlockSpec((gr, zc), lambda i: (i, 0)),
            pl.BlockSpec((g, WINDOW, KV_WIDTH), lambda i: (i, 0, 0)),
            pl.BlockSpec((g, WINDOW, KV_WIDTH), lambda i: (i, 0, 0)),
            pl.BlockSpec((None, g, N_MEM * MEM_HEADS, MEM_HEAD_DIM), lambda i: (layer, i, 0, 0)),
            pl.BlockSpec((None, g, N_MEM * MEM_HEADS, MEM_HEAD_DIM), lambda i: (layer, i, 0, 0)),
            pl.BlockSpec((None, slab, wn), lambda i: (0, i, 0)),
        ],
        out_specs=[
            pl.BlockSpec((gr, D_MODEL), lambda i: (i, 0)),
            pl.BlockSpec((g, WINDOW, KV_WIDTH), lambda i: (i, 0, 0)),
            pl.BlockSpec((g, WINDOW, KV_WIDTH), lambda i: (i, 0, 0)),
            pl.BlockSpec((slab, wn), lambda i: (i, 0)),
        ],
        out_shape=[jax.ShapeDtypeStruct((rows, D_MODEL), BF16), win, win,
                   jax.ShapeDtypeStruct((wk, wn), BF16)],
        scratch_shapes=[pltpu.VMEM((g, WINDOW, KV_WIDTH), F32), pltpu.VMEM((g, WINDOW, KV_WIDTH), F32),
                        pltpu.VMEM((gr, D_MODEL), F32)],
        compiler_params=pltpu.CompilerParams(
            dimension_semantics=("arbitrary",),
            vmem_limit_bytes=_vmem_limit(blocks, scratch)),
        name="swa_sample",
    )(sinks, z, cache_k, cache_v, mem_k, mem_v, w_out)


def kernel(x_prompt, x_sample, mem_prompt, state_conv, cache_win_k, cache_win_v, cache_mem_k, cache_mem_v,
           norm_mix, norm_mem, w_mem_kv, norm_ffn, w_gate, w_up, w_down,
           conv_w_in, conv_w, conv_w_out, attn_w_in, attn_sinks, attn_w_out, norm_final):
    batch, seq, d = x_prompt.shape
    dec_batch, dec_seq, _ = x_sample.shape
    depth = norm_mix.shape[0]
    d_ff = w_gate.shape[2]
    prompt_rows = batch * seq
    sample_rows = dec_batch * dec_seq
    assert d == D_MODEL and depth == 2 and seq % CONV_ROWS == 0 and seq % SWA_ROWS == 0 and SWA_ROWS % WINDOW == 0
    assert prompt_rows % PROMPT_TILE == 0 and dec_batch % SAMPLE_GROUP == 0
    assert dec_seq == V7X_SUBLANES and d_ff % FFN_COL_TILE == 0 and d % COL_TILE == 0
    assert d_ff % DOWN_K_TILE == 0 and d % DOWN_X_CHUNK == 0
    assert prompt_rows % FFN_ROW_TILE == 0 and prompt_rows % sample_rows == 0

    xp = x_prompt.reshape(prompt_rows, d)
    xs = x_sample.reshape(sample_rows, d)
    mem = mem_prompt.reshape(batch * N_MEM, d)
    mem_k_s = cache_mem_k.reshape(depth, dec_batch, N_MEM * MEM_HEADS, MEM_HEAD_DIM)
    mem_v_s = cache_mem_v.reshape(depth, dec_batch, N_MEM * MEM_HEADS, MEM_HEAD_DIM)
    g_mix = norm_mix.reshape(depth, 1, d)
    g_ffn = norm_ffn.reshape(depth, 1, d)

    mk, mv = _mem_kv(mem, norm_mem.reshape(depth, 1, d), w_mem_kv)

    hp, hs = _norm_rows(xp, xs, g_mix, layer=0)
    zp, zs, wd_bf = _matmul(hp, hs, conv_w_in, w_layer=0, side=w_down, side_layer=0)
    mix_s, conv_s, wo_bf = _conv_sample(zs, state_conv, mem_k_s, mem_v_s, conv_w, conv_w_out,
                                        layer=0, dec_seq=dec_seq)
    mix_p, conv_p = _conv_prompt(zp, mk, mv, conv_w, layer=0, batch=batch, seq=seq)
    xp, xs, hp, hs = _out_proj(mix_p, mix_s, wo_bf, xp, xs, g_ffn, layer=0)
    ap, as_ = _ffn_up(hp, hs, w_gate, w_up, layer=0)
    xp, xs, hp, hs = _ffn_down(ap, as_, wd_bf[None], xp, xs, g_mix, w_layer=0, g_layer=1, final=False)

    zp, zs = _matmul(hp, hs, attn_w_in, w_layer=0)
    sinks = attn_sinks[0]
    mix_s, win_k_s, win_v_s, wo_bf = _swa_sample(
        zs, cache_win_k[0].reshape(dec_batch, WINDOW, KV_WIDTH), cache_win_v[0].reshape(dec_batch, WINDOW, KV_WIDTH),
        mem_k_s, mem_v_s, sinks, attn_w_out, layer=1, dec_seq=dec_seq)
    mix_p, wd_bf = _swa_prompt(zp, mk, mv, sinks, [w_down], layer=1, batch=batch, seq=seq)
    xp, xs, hp, hs = _out_proj(mix_p, mix_s, wo_bf, xp, xs, g_ffn, layer=1)
    ap, as_ = _ffn_up(hp, hs, w_gate, w_up, layer=1)
    y_prompt, y_sample = _ffn_down(ap, as_, wd_bf[None], xp, xs, norm_final.reshape(1, 1, d),
                                   w_layer=0, g_layer=0, final=True)

    win_p = zp.reshape(batch, seq, -1)[:, seq - WINDOW:, TOKEN_WIDTH:TOKEN_WIDTH + 2 * KV_WIDTH].astype(F32)
    kv_shape = (1, -1, WINDOW, N_KV_HEADS, HEAD_DIM)
    mem_shape = (depth, batch, N_MEM, MEM_HEADS, MEM_HEAD_DIM)
    return (y_prompt.reshape(batch, seq, d),
            y_sample.reshape(dec_batch, dec_seq, d),
            conv_p[None],
            conv_s[None],
            win_p[..., :KV_WIDTH].reshape(kv_shape),
            win_p[..., KV_WIDTH:].reshape(kv_shape),
            win_k_s.reshape(kv_shape),
            win_v_s.reshape(kv_shape),
            mk.reshape(mem_shape),
            mv.reshape(mem_shape))
```

```python
import functools

import jax
import jax.numpy as jnp
from jax import lax
from jax.experimental import pallas as pl
from jax.experimental.pallas import tpu as pltpu

F32 = jnp.float32
BF16 = jnp.bfloat16

D_MODEL = 2048
N_MEM = 256
MEM_HEADS = 4
MEM_WIDTH = D_MODEL // 4
MEM_HEAD_DIM = MEM_WIDTH // MEM_HEADS
TOKEN_WIDTH = D_MODEL - MEM_WIDTH
CONV_WIDTH = 3
WINDOW = 128
HEAD_DIM = 64
N_HEADS = TOKEN_WIDTH // HEAD_DIM
N_KV_HEADS = 4
GROUP = N_HEADS // N_KV_HEADS
KV_WIDTH = N_KV_HEADS * HEAD_DIM
EPS = 1e-6

V7X_VMEM_BYTES = 64 * 1024 * 1024
V7X_SUBLANES = 8
V7X_MXU_DEPTH = 256

PROMPT_TILE = 1024
COL_TILE = 512
DOWN_K_TILE = 1408
DOWN_X_CHUNK = 512
FFN_ROW_TILE = 2048
FFN_COL_TILE = 512
MXU_ROWS = 1024
OUT_ROWS = 512
CONV_ROWS = 512
CONV_CHUNK = 16
SWA_ROWS = 512
SCORE_LOOKAHEAD = 96
SAMPLE_LOOKAHEAD = 32
SAMPLE_GROUP = 8
SIDE_CAST_STEPS = 32
CARRY_ROWS = V7X_SUBLANES
MIN_TEMP_BYTES = 8 << 20
VMEM_RESERVE_BYTES = 6 << 20


def _nbytes(shape, dtype):
    n = 1
    for s in shape:
        n *= s
    return n * jnp.dtype(dtype).itemsize


def _vmem_limit(block_bytes, scratch_bytes):
    need = 2 * block_bytes + scratch_bytes
    temporaries = max(need // 4, MIN_TEMP_BYTES)
    return int(min(need + temporaries, V7X_VMEM_BYTES - VMEM_RESERVE_BYTES))


def _rmsnorm(x, g):
    r = lax.rsqrt(jnp.mean(x * x, axis=-1, keepdims=True) + EPS)
    return (x * r) * g


def _dot(a, b):
    return jnp.dot(a, b, preferred_element_type=F32)


def _dot_nt(a, b):
    return lax.dot_general(a, b, (((1,), (1,)), ((), ())), preferred_element_type=F32)


def _prompt_rows_map(i, j):
    return (i, 0)


def _prompt_tile_map(i, j):
    return (i, j)


def _sample_tile_map(n_pt):
    return lambda i, j: (0, jnp.where(i == n_pt - 1, j, 0))


def _const_map(i, j):
    return (0, 0)


def _norm_rows_kernel(xp_ref, xs_ref, g_ref, hp_ref, hs_ref, *, n_pt):
    hp_ref[...] = _rmsnorm(xp_ref[...], g_ref[...]).astype(BF16)

    @pl.when(pl.program_id(0) == n_pt - 1)
    def _():
        hs_ref[...] = _rmsnorm(xs_ref[...], g_ref[...]).astype(BF16)


def _norm_rows(xp, xs, g, *, layer):
    rp, d = xp.shape
    rs = xs.shape[0]
    tm = OUT_ROWS
    n_pt = rp // tm
    blocks = _nbytes((tm, d), F32) + _nbytes((tm, d), BF16) + _nbytes((rs, d), F32) + _nbytes((rs, d), BF16)
    return pl.pallas_call(
        functools.partial(_norm_rows_kernel, n_pt=n_pt),
        grid=(n_pt,),
        in_specs=[
            pl.BlockSpec((tm, d), lambda s: (s, 0)),
            pl.BlockSpec((rs, d), lambda s: (0, 0)),
            pl.BlockSpec((None, 1, d), lambda s: (layer, 0, 0)),
        ],
        out_specs=[
            pl.BlockSpec((tm, d), lambda s: (s, 0)),
            pl.BlockSpec((rs, d), lambda s: (0, 0)),
        ],
        out_shape=[jax.ShapeDtypeStruct((rp, d), BF16), jax.ShapeDtypeStruct((rs, d), BF16)],
        compiler_params=pltpu.CompilerParams(
            dimension_semantics=("arbitrary",),
            vmem_limit_bytes=_vmem_limit(blocks, 0)),
        name="norm_rows",
    )(xp, xs, g)


def _swiglu(h, wg, wu):
    gate = _dot(h, wg)
    up = _dot(h, wu)
    return (gate * jax.nn.sigmoid(gate) * up).astype(BF16)


def _ffn_up_kernel(hp_ref, hs_ref, wg_ref, wu_ref, ap_ref, as_ref, *, n_pt):
    wg, wu = wg_ref[...].astype(BF16), wu_ref[...].astype(BF16)
    for r in range(0, hp_ref.shape[0], MXU_ROWS):
        ap_ref[r:r + MXU_ROWS, :] = _swiglu(hp_ref[r:r + MXU_ROWS, :], wg, wu)

    @pl.when(pl.program_id(0) == n_pt - 1)
    def _():
        as_ref[...] = _swiglu(hs_ref[...], wg_ref[...].astype(BF16), wu_ref[...].astype(BF16))


def _ffn_up(hp, hs, wg, wu, *, layer):
    rp, d = hp.shape
    rs = hs.shape[0]
    n = wg.shape[2]
    tm, tn = FFN_ROW_TILE, FFN_COL_TILE
    n_pt, n_j = rp // tm, n // tn
    blocks = (_nbytes((tm, d), BF16) + _nbytes((rs, d), BF16) + 2 * _nbytes((d, tn), wg.dtype)
              + _nbytes((tm, tn), BF16) + _nbytes((rs, tn), BF16))
    temps = 2 * _nbytes((d, tn), BF16) + 3 * _nbytes((tm, tn), F32)
    w_spec = pl.BlockSpec((None, d, tn), lambda i, j: (layer, 0, j))
    return pl.pallas_call(
        functools.partial(_ffn_up_kernel, n_pt=n_pt),
        grid=(n_pt, n_j),
        in_specs=[
            pl.BlockSpec((tm, d), _prompt_rows_map),
            pl.BlockSpec((rs, d), _const_map),
            w_spec,
            w_spec,
        ],
        out_specs=[
            pl.BlockSpec((tm, tn), _prompt_tile_map),
            pl.BlockSpec((rs, tn), _sample_tile_map(n_pt)),
        ],
        out_shape=[jax.ShapeDtypeStruct((rp, n), BF16), jax.ShapeDtypeStruct((rs, n), BF16)],
        compiler_params=pltpu.CompilerParams(
            dimension_semantics=("arbitrary", "arbitrary"),
            vmem_limit_bytes=_vmem_limit(blocks, temps)),
        name="ffn_up",
    )(hp, hs, wg, wu)


def _out_proj_kernel(ap_ref, as_ref, w_ref, xp_ref, xs_ref, g_ref, op_ref, os_ref, hp_ref, hs_ref, *, n_pt):
    x = xp_ref[...] + _dot(ap_ref[...], w_ref[...])
    op_ref[...] = x
    hp_ref[...] = _rmsnorm(x, g_ref[...]).astype(BF16)

    @pl.when(pl.program_id(0) == n_pt - 1)
    def _():
        x = xs_ref[...] + _dot(as_ref[...], w_ref[...])
        os_ref[...] = x
        hs_ref[...] = _rmsnorm(x, g_ref[...]).astype(BF16)


def _out_proj(ap, as_, w, xp, xs, g, *, layer):
    rp, k = ap.shape
    rs = as_.shape[0]
    n = w.shape[1]
    tm = OUT_ROWS
    n_pt = rp // tm
    rows_map = lambda s: (s, 0)
    const = lambda s: (0, 0)
    once = dict(pipeline_mode=pl.Buffered(1))
    blocks = _nbytes((tm, k), BF16) + 2 * _nbytes((tm, n), F32) + _nbytes((tm, n), BF16)
    resident = (_nbytes((k, n), BF16) + _nbytes((rs, k), BF16) + _nbytes((rs, n), F32)
                + 2 * (_nbytes((rs, n), F32) + _nbytes((rs, n), BF16)))
    return pl.pallas_call(
        functools.partial(_out_proj_kernel, n_pt=n_pt),
        grid=(n_pt,),
        in_specs=[
            pl.BlockSpec((tm, k), rows_map),
            pl.BlockSpec((rs, k), const, **once),
            pl.BlockSpec((k, n), const, **once),
            pl.BlockSpec((tm, n), rows_map),
            pl.BlockSpec((rs, n), const, **once),
            pl.BlockSpec((None, 1, n), lambda s: (layer, 0, 0)),
        ],
        out_specs=[
            pl.BlockSpec((tm, n), rows_map),
            pl.BlockSpec((rs, n), const),
            pl.BlockSpec((tm, n), rows_map),
            pl.BlockSpec((rs, n), const),
        ],
        out_shape=[jax.ShapeDtypeStruct((rp, n), F32), jax.ShapeDtypeStruct((rs, n), F32),
                   jax.ShapeDtypeStruct((rp, n), BF16), jax.ShapeDtypeStruct((rs, n), BF16)],
        compiler_params=pltpu.CompilerParams(
            dimension_semantics=("arbitrary",),
            vmem_limit_bytes=_vmem_limit(blocks, resident + 2 * _nbytes((tm, n), F32))),
        name="out_proj",
    )(ap, as_, w, xp, xs, g)


def _matmul_kernel(hp_ref, hs_ref, w_ref, *refs, n_pt, n_j, n_cast):
    if n_cast:
        side_ref, zp_ref, zs_ref, side_bf_ref = refs

        @pl.when(pl.program_id(0) * n_j + pl.program_id(1) < n_cast)
        def _():
            side_bf_ref[...] = side_ref[...].astype(BF16)
    else:
        zp_ref, zs_ref = refs

    w = w_ref[...].astype(BF16)
    for r in range(0, hp_ref.shape[0], MXU_ROWS):
        zp_ref[r:r + MXU_ROWS, :] = _dot(hp_ref[r:r + MXU_ROWS, :], w).astype(zp_ref.dtype)

    @pl.when(pl.program_id(0) == n_pt - 1)
    def _():
        zs_ref[...] = _dot(hs_ref[...], w_ref[...].astype(BF16)).astype(zs_ref.dtype)


def _matmul(hp, hs, w, *, w_layer, side=None, side_layer=0):
    rp, d = hp.shape
    rs = hs.shape[0]
    n = w.shape[2]
    tm, tn = FFN_ROW_TILE, COL_TILE
    n_pt, n_j = rp // tm, n // tn
    blocks = (_nbytes((tm, d), BF16) + _nbytes((rs, d), BF16) + _nbytes((d, tn), F32)
              + _nbytes((tm, tn), BF16) + _nbytes((rs, tn), BF16))
    temps = _nbytes((d, tn), BF16) + _nbytes((tm, tn), F32)
    in_specs = [
        pl.BlockSpec((tm, d), _prompt_rows_map),
        pl.BlockSpec((rs, d), _const_map),
        pl.BlockSpec((None, d, tn), lambda i, j: (w_layer, 0, j)),
    ]
    out_specs = [
        pl.BlockSpec((tm, tn), _prompt_tile_map),
        pl.BlockSpec((rs, tn), _sample_tile_map(n_pt)),
    ]
    out_shape = [jax.ShapeDtypeStruct((rp, n), BF16), jax.ShapeDtypeStruct((rs, n), BF16)]
    operands = [hp, hs, w]
    n_cast = 0
    if side is not None:
        r, c = side.shape[1:]
        n_cast = SIDE_CAST_STEPS
        slab = r // n_cast
        assert n_cast <= n_pt * n_j and slab * n_cast == r and slab % (2 * V7X_SUBLANES) == 0
        slab_idx = lambda i, j: jnp.minimum(i * n_j + j, n_cast - 1)
        in_specs.append(pl.BlockSpec((None, slab, c), lambda i, j: (side_layer, slab_idx(i, j), 0)))
        out_specs.append(pl.BlockSpec((slab, c), lambda i, j: (slab_idx(i, j), 0)))
        out_shape.append(jax.ShapeDtypeStruct((r, c), BF16))
        operands.append(side)
        blocks += _nbytes((slab, c), F32) + _nbytes((slab, c), BF16)
    return pl.pallas_call(
        functools.partial(_matmul_kernel, n_pt=n_pt, n_j=n_j, n_cast=n_cast),
        grid=(n_pt, n_j),
        in_specs=in_specs,
        out_specs=out_specs,
        out_shape=out_shape,
        compiler_params=pltpu.CompilerParams(
            dimension_semantics=("arbitrary", "arbitrary"),
            vmem_limit_bytes=_vmem_limit(blocks, temps)),
        name="matmul",
    )(*operands)


def _ffn_down_kernel(ap_ref, as_ref, w_ref, xp_ref, xs_ref, g_ref, *refs, n_pt, n_k, n_xc, final):
    i, k = pl.program_id(0), pl.program_id(1)
    if final:
        op_ref, os_ref, apl_ref, asl_ref, wl_ref = refs
    else:
        op_ref, os_ref, hp_ref, hs_ref, apl_ref, asl_ref, wl_ref = refs
    xc = xp_ref.shape[1]
    kf = (ap_ref.shape[1] // V7X_MXU_DEPTH) * V7X_MXU_DEPTH
    even = k % 2 == 0

    def whole_passes(a_ref):
        return _dot(a_ref[:, :kf], w_ref[:kf, :])

    def with_stash(a_ref, al_ref):
        return _dot(jnp.concatenate([al_ref[...], a_ref[...]], axis=1),
                    jnp.concatenate([wl_ref[...], w_ref[...]], axis=0))

    for kk in range(n_k):
        @pl.when(k == kk)
        def _():
            cols = slice(kk * xc, (kk + 1) * xc)
            if 0 < kk < n_xc:
                op_ref[:, cols] = op_ref[:, cols] + xp_ref[...]
            contrib = whole_passes(ap_ref) if kk % 2 == 0 else with_stash(ap_ref, apl_ref)
            if kk == 0:
                op_ref[...] = contrib
                op_ref[:, cols] = op_ref[:, cols] + xp_ref[...]
            else:
                op_ref[...] = op_ref[...] + contrib
            if kk == n_k - 1:
                if final:
                    op_ref[...] = _rmsnorm(op_ref[...], g_ref[...])
                else:
                    hp_ref[...] = _rmsnorm(op_ref[...], g_ref[...]).astype(BF16)

    @pl.when(i == n_pt - 1)
    def _():
        @pl.when(k == 0)
        def _():
            os_ref[...] = xs_ref[...] + whole_passes(as_ref)

        @pl.when(jnp.logical_and(k > 0, even))
        def _():
            os_ref[...] = os_ref[...] + whole_passes(as_ref)

        @pl.when(jnp.logical_not(even))
        def _():
            os_ref[...] = os_ref[...] + with_stash(as_ref, asl_ref)

        @pl.when(even)
        def _():
            asl_ref[...] = as_ref[:, kf:]

        @pl.when(k == n_k - 1)
        def _():
            if final:
                os_ref[...] = _rmsnorm(os_ref[...], g_ref[...])
            else:
                hs_ref[...] = _rmsnorm(os_ref[...], g_ref[...]).astype(BF16)

    @pl.when(even)
    def _():
        apl_ref[...] = ap_ref[:, kf:]
        wl_ref[...] = w_ref[kf:, :]


def _ffn_down(ap, as_, w, xp, xs, g, *, w_layer, g_layer, final):
    rp, kdim = ap.shape
    rs = as_.shape[0]
    n = w.shape[2]
    tm, tk, xc = PROMPT_TILE, DOWN_K_TILE, DOWN_X_CHUNK
    n_pt, n_k, n_xc = rp // tm, kdim // tk, n // xc
    k_left = tk % V7X_MXU_DEPTH
    assert n_xc <= n_k and n_k % 2 == 0 and 2 * k_left == V7X_MXU_DEPTH and w.dtype == BF16
    rows = lambda i, k: (i, 0)
    blocks = (_nbytes((tm, tk), BF16) + _nbytes((rs, tk), BF16) + _nbytes((tk, n), w.dtype)
              + _nbytes((tm, xc), F32) + _nbytes((rs, n), F32)
              + _nbytes((tm, n), F32) + _nbytes((rs, n), F32))
    out_specs = [pl.BlockSpec((tm, n), rows), pl.BlockSpec((rs, n), _const_map)]
    out_shape = [jax.ShapeDtypeStruct((rp, n), F32), jax.ShapeDtypeStruct((rs, n), F32)]
    if not final:
        blocks += _nbytes((tm, n), BF16) + _nbytes((rs, n), BF16)
        out_specs += [pl.BlockSpec((tm, n), rows), pl.BlockSpec((rs, n), _const_map)]
        out_shape += [jax.ShapeDtypeStruct((rp, n), BF16), jax.ShapeDtypeStruct((rs, n), BF16)]
    return pl.pallas_call(
        functools.partial(_ffn_down_kernel, n_pt=n_pt, n_k=n_k, n_xc=n_xc, final=final),
        grid=(n_pt, n_k),
        in_specs=[
            pl.BlockSpec((tm, tk), lambda i, k: (i, k)),
            pl.BlockSpec((rs, tk), lambda i, k: (0, jnp.where(i == n_pt - 1, k, 0))),
            pl.BlockSpec((None, tk, n), lambda i, k: (w_layer, k, 0)),
            pl.BlockSpec((tm, xc), lambda i, k: (i, jnp.minimum(k, n_xc - 1))),
            pl.BlockSpec((rs, n), _const_map),
            pl.BlockSpec((None, 1, n), lambda i, k: (g_layer, 0, 0)),
        ],
        out_specs=out_specs,
        out_shape=out_shape,
        scratch_shapes=[pltpu.VMEM((tm, k_left), BF16), pltpu.VMEM((rs, k_left), BF16),
                        pltpu.VMEM((k_left, n), BF16)],
        compiler_params=pltpu.CompilerParams(
            dimension_semantics=("arbitrary", "arbitrary"),
            vmem_limit_bytes=_vmem_limit(blocks, _nbytes((tk, n), BF16))),
        name="ffn_down",
    )(ap, as_, w, xp, xs, g)


def _mem_kv_kernel(x_ref, g_ref, w_ref, k_ref, v_ref):
    h = _rmsnorm(x_ref[...], g_ref[...]).astype(BF16)
    kv = _dot(h, w_ref[...].astype(BF16))
    tm = x_ref.shape[0]
    for hd in range(MEM_HEADS):
        head_rows = pl.ds(hd, tm, stride=MEM_HEADS)
        k_ref[head_rows, :] = kv[:, hd * MEM_HEAD_DIM:(hd + 1) * MEM_HEAD_DIM]
        v_ref[head_rows, :] = kv[:, MEM_WIDTH + hd * MEM_HEAD_DIM:MEM_WIDTH + (hd + 1) * MEM_HEAD_DIM]


def _mem_kv(mem, g, w):
    rows, d = mem.shape
    depth = w.shape[0]
    tm = 512
    out = jax.ShapeDtypeStruct((depth, rows * MEM_HEADS, MEM_HEAD_DIM), F32)
    blocks = (_nbytes((tm, d), F32) + _nbytes((d, 2 * MEM_WIDTH), F32) + 2 * _nbytes((tm, MEM_WIDTH), F32))
    return pl.pallas_call(
        _mem_kv_kernel,
        grid=(depth, rows // tm),
        in_specs=[
            pl.BlockSpec((tm, d), lambda l, i: (i, 0)),
            pl.BlockSpec((None, 1, d), lambda l, i: (l, 0, 0)),
            pl.BlockSpec((None, d, 2 * MEM_WIDTH), lambda l, i: (l, 0, 0)),
        ],
        out_specs=[
            pl.BlockSpec((None, tm * MEM_HEADS, MEM_HEAD_DIM), lambda l, i: (l, i, 0)),
            pl.BlockSpec((None, tm * MEM_HEADS, MEM_HEAD_DIM), lambda l, i: (l, i, 0)),
        ],
        out_shape=[out, out],
        compiler_params=pltpu.CompilerParams(
            dimension_semantics=("arbitrary", "arbitrary"),
            vmem_limit_bytes=_vmem_limit(blocks, _nbytes((d, 2 * MEM_WIDTH), BF16))),
        name="mem_kv",
    )(mem, g, w)


def _cross_scores(q, k):
    return _dot_nt(q, k) * (MEM_HEAD_DIM ** -0.5)


def _cross_values(s, v):
    e = jnp.exp(s - jnp.max(s, axis=-1, keepdims=True)).astype(BF16)
    return _dot(e, v) / _dot(e, jnp.ones(v.shape, BF16))


def _run_ahead(n_items, lookahead, first, second):
    pending = {}
    for i in range(n_items + lookahead):
        if i < n_items:
            pending[i] = first(i)
        if i >= lookahead:
            second(i - lookahead, pending.pop(i - lookahead))


def _conv_prompt_kernel(z_ref, mk_ref, mv_ref, cw_ref, mix_ref, st_ref, ext_ref, *, tiles_per_seq):
    s = pl.program_id(0)
    tq = z_ref.shape[0]

    @pl.when(s % tiles_per_seq == 0)
    def _():
        ext_ref[0:CARRY_ROWS, :] = jnp.zeros((CARRY_ROWS, TOKEN_WIDTH), F32)

    def scores(h):
        lo, hi = h * MEM_HEAD_DIM, (h + 1) * MEM_HEAD_DIM
        return _cross_scores(z_ref[:, 3 * TOKEN_WIDTH + lo:3 * TOKEN_WIDTH + hi],
                             mk_ref[pl.ds(h, N_MEM, stride=MEM_HEADS), :].astype(BF16))

    def finish(h, s):
        lo, hi = h * MEM_HEAD_DIM, (h + 1) * MEM_HEAD_DIM
        mix_ref[:, TOKEN_WIDTH + lo:TOKEN_WIDTH + hi] = _cross_values(
            s, mv_ref[pl.ds(h, N_MEM, stride=MEM_HEADS), :].astype(BF16)).astype(BF16)

    pending = [scores(h) for h in range(MEM_HEADS)]

    for r in range(0, tq, CONV_CHUNK):
        rows = slice(r, r + CONV_CHUNK)
        c = z_ref[rows, TOKEN_WIDTH:2 * TOKEN_WIDTH].astype(F32)
        u = z_ref[rows, 2 * TOKEN_WIDTH:3 * TOKEN_WIDTH].astype(F32)
        cu = c * u
        ext_ref[CARRY_ROWS + r:CARRY_ROWS + r + CONV_CHUNK, :] = cu
        conv = (cw_ref[0:1, :] * ext_ref[CARRY_ROWS - 2 + r:CARRY_ROWS - 2 + r + CONV_CHUNK, :]
                + cw_ref[1:2, :] * ext_ref[CARRY_ROWS - 1 + r:CARRY_ROWS - 1 + r + CONV_CHUNK, :]
                + cw_ref[2:3, :] * cu)
        b = z_ref[rows, 0:TOKEN_WIDTH].astype(F32)
        mix_ref[rows, 0:TOKEN_WIDTH] = (b * conv).astype(BF16)
    st_ref[...] = ext_ref[CARRY_ROWS + tq - 2:CARRY_ROWS + tq, :]
    ext_ref[0:CARRY_ROWS, :] = ext_ref[tq:tq + CARRY_ROWS, :]
    for h in range(MEM_HEADS):
        finish(h, pending[h])


def _conv_prompt(z, mk, mv, conv_w, *, layer, batch, seq):
    rows, zc = z.shape
    tq = CONV_ROWS
    tiles_per_seq = seq // tq
    n_steps = batch * tiles_per_seq
    blocks = _nbytes((tq, zc), BF16) + 2 * _nbytes((N_MEM, MEM_WIDTH), F32) + _nbytes((tq, D_MODEL), BF16)
    scratch = _nbytes((tq + CARRY_ROWS, TOKEN_WIDTH), F32)
    return pl.pallas_call(
        functools.partial(_conv_prompt_kernel, tiles_per_seq=tiles_per_seq),
        grid=(n_steps,),
        in_specs=[
            pl.BlockSpec((tq, zc), lambda s: (s, 0)),
            pl.BlockSpec((None, N_MEM * MEM_HEADS, MEM_HEAD_DIM), lambda s: (layer, s // tiles_per_seq, 0)),
            pl.BlockSpec((None, N_MEM * MEM_HEADS, MEM_HEAD_DIM), lambda s: (layer, s // tiles_per_seq, 0)),
            pl.BlockSpec((None, CONV_WIDTH, TOKEN_WIDTH), lambda s: (0, 0, 0)),
        ],
        out_specs=[
            pl.BlockSpec((tq, D_MODEL), lambda s: (s, 0)),
            pl.BlockSpec((None, CONV_WIDTH - 1, TOKEN_WIDTH), lambda s: (s // tiles_per_seq, 0, 0)),
        ],
        out_shape=[
            jax.ShapeDtypeStruct((rows, D_MODEL), BF16),
            jax.ShapeDtypeStruct((batch, CONV_WIDTH - 1, TOKEN_WIDTH), F32),
        ],
        scratch_shapes=[pltpu.VMEM((tq + CARRY_ROWS, TOKEN_WIDTH), F32)],
        compiler_params=pltpu.CompilerParams(
            dimension_semantics=("arbitrary",),
            vmem_limit_bytes=_vmem_limit(blocks, scratch + 6 * _nbytes((tq, TOKEN_WIDTH), F32))),
        name="conv_prompt",
    )(z, mk, mv, conv_w)


def _conv_sample_kernel(z_ref, st_ref, mk_ref, mv_ref, cw_ref, wo_ref, mix_ref, nst_ref, wob_ref,
                        ext_ref, mixf_ref, *, dec_seq):
    t = dec_seq
    wob_ref[...] = wo_ref[...].astype(BF16)
    for n in range(SAMPLE_GROUP):
        r0, r1 = n * t, (n + 1) * t
        c = z_ref[r0:r1, TOKEN_WIDTH:2 * TOKEN_WIDTH].astype(F32)
        u = z_ref[r0:r1, 2 * TOKEN_WIDTH:3 * TOKEN_WIDTH].astype(F32)
        cu = c * u
        ext_ref[CARRY_ROWS - 2:CARRY_ROWS, :] = st_ref[n]
        ext_ref[CARRY_ROWS:CARRY_ROWS + t, :] = cu
        conv = (cw_ref[0:1, :] * ext_ref[CARRY_ROWS - 2:CARRY_ROWS - 2 + t, :]
                + cw_ref[1:2, :] * ext_ref[CARRY_ROWS - 1:CARRY_ROWS - 1 + t, :]
                + cw_ref[2:3, :] * cu)
        b = z_ref[r0:r1, 0:TOKEN_WIDTH].astype(F32)
        mixf_ref[r0:r1, 0:TOKEN_WIDTH] = b * conv
        nst_ref[n] = ext_ref[CARRY_ROWS + t - 2:CARRY_ROWS + t, :]
    _sample_cross_attention(z_ref, mk_ref, mv_ref, mixf_ref, 3 * TOKEN_WIDTH, t)
    mix_ref[...] = mixf_ref[...].astype(BF16)


def _sample_cross_attention(z_ref, mk_ref, mv_ref, mixf_ref, qm_off, t):
    def scores(i):
        n, h = divmod(i, MEM_HEADS)
        q = z_ref[n * t:(n + 1) * t, qm_off + h * MEM_HEAD_DIM:qm_off + (h + 1) * MEM_HEAD_DIM]
        return _cross_scores(q, mk_ref[n, pl.ds(h, N_MEM, stride=MEM_HEADS), :].astype(BF16))

    def finish(i, s):
        n, h = divmod(i, MEM_HEADS)
        o = _cross_values(s, mv_ref[n, pl.ds(h, N_MEM, stride=MEM_HEADS), :].astype(BF16))
        mixf_ref[n * t:(n + 1) * t, TOKEN_WIDTH + h * MEM_HEAD_DIM:TOKEN_WIDTH + (h + 1) * MEM_HEAD_DIM] = o

    _run_ahead(SAMPLE_GROUP * MEM_HEADS, SAMPLE_LOOKAHEAD, scores, finish)


def _conv_sample(z, state, mem_k, mem_v, conv_w, w_out, *, layer, dec_seq):
    rows, zc = z.shape
    dec_batch = state.shape[1]
    g = SAMPLE_GROUP
    gr = g * dec_seq
    n_steps = dec_batch // g
    wk, wn = w_out.shape[1:]
    slab = wk // n_steps
    assert slab * n_steps == wk and slab % (2 * V7X_SUBLANES) == 0
    blocks = (_nbytes((gr, zc), BF16) + 2 * _nbytes((g, CONV_WIDTH - 1, TOKEN_WIDTH), F32)
              + 2 * _nbytes((g, N_MEM, V7X_SUBLANES, MEM_HEAD_DIM), F32) + _nbytes((gr, D_MODEL), BF16)
              + _nbytes((slab, wn), F32) + _nbytes((slab, wn), BF16))
    scratch = _nbytes((2 * CARRY_ROWS, TOKEN_WIDTH), F32) + _nbytes((gr, D_MODEL), F32)
    return pl.pallas_call(
        functools.partial(_conv_sample_kernel, dec_seq=dec_seq),
        grid=(n_steps,),
        in_specs=[
            pl.BlockSpec((gr, zc), lambda i: (i, 0)),
            pl.BlockSpec((None, g, CONV_WIDTH - 1, TOKEN_WIDTH), lambda i: (0, i, 0, 0)),
            pl.BlockSpec((None, g, N_MEM * MEM_HEADS, MEM_HEAD_DIM), lambda i: (layer, i, 0, 0)),
            pl.BlockSpec((None, g, N_MEM * MEM_HEADS, MEM_HEAD_DIM), lambda i: (layer, i, 0, 0)),
            pl.BlockSpec((None, CONV_WIDTH, TOKEN_WIDTH), lambda i: (0, 0, 0)),
            pl.BlockSpec((None, slab, wn), lambda i: (0, i, 0)),
        ],
        out_specs=[
            pl.BlockSpec((gr, D_MODEL), lambda i: (i, 0)),
            pl.BlockSpec((g, CONV_WIDTH - 1, TOKEN_WIDTH), lambda i: (i, 0, 0)),
            pl.BlockSpec((slab, wn), lambda i: (i, 0)),
        ],
        out_shape=[
            jax.ShapeDtypeStruct((rows, D_MODEL), BF16),
            jax.ShapeDtypeStruct((dec_batch, CONV_WIDTH - 1, TOKEN_WIDTH), F32),
            jax.ShapeDtypeStruct((wk, wn), BF16),
        ],
        scratch_shapes=[pltpu.VMEM((2 * CARRY_ROWS, TOKEN_WIDTH), F32), pltpu.VMEM((gr, D_MODEL), F32)],
        compiler_params=pltpu.CompilerParams(
            dimension_semantics=("arbitrary",),
            vmem_limit_bytes=_vmem_limit(blocks, scratch)),
        name="conv_sample",
    )(z, state, mem_k, mem_v, conv_w, w_out)


def _band_scores(q, k):
    return _dot_nt(q * (HEAD_DIM ** -0.5), k)


def _band_probs(s, sink, upper, upper_visible):
    s = jnp.where(upper_visible, s[:, :WINDOW], jnp.where(upper, -jnp.inf, s[:, WINDOW:]))
    m = jnp.maximum(jnp.max(s, axis=-1, keepdims=True), sink)
    e = jnp.exp(s - m)
    e = jnp.concatenate([jnp.where(upper, e, 0.0), jnp.where(upper, 0.0, e)], axis=1).astype(BF16)
    return e, jnp.exp(sink - m)


def _band_values(probs, v, ones):
    e, sink_term = probs
    return _dot(e, v) / (_dot(e, ones) + sink_term)


def _swa_prompt_kernel(sink_ref, zq_ref, zp_ref, mk_ref, mv_ref, *refs, blocks_per_seq, cast_steps):
    n_w = (len(refs) - 1) // 2
    w_refs, mix_ref, wb_refs = refs[:n_w], refs[n_w], refs[n_w + 1:]
    s = pl.program_id(0)

    for w_ref, wb_ref, n_cast in zip(w_refs, wb_refs, cast_steps):
        @pl.when(s < n_cast)
        def _():
            wb_ref[...] = w_ref[...].astype(BF16)

    k_off = TOKEN_WIDTH
    v_off = TOKEN_WIDTH + KV_WIDTH
    qm_off = TOKEN_WIDTH + 2 * KV_WIDTH
    n_blocks = zq_ref.shape[0] // WINDOW
    first_has_prev = (s % (blocks_per_seq // n_blocks)) > 0
    row = lax.broadcasted_iota(jnp.int32, (WINDOW, WINDOW), 0)
    col = lax.broadcasted_iota(jnp.int32, (WINDOW, WINDOW), 1)
    upper = col > row
    upper_first = jnp.logical_and(upper, first_has_prev)
    ones = jnp.ones((2 * WINDOW, HEAD_DIM), BF16)

    def block_rows(b):
        return slice(b * WINDOW, (b + 1) * WINDOW)

    @functools.lru_cache(maxsize=None)
    def window(b, kh, col_prev, col_cur):
        lo, hi = kh * HEAD_DIM, (kh + 1) * HEAD_DIM
        prev = (zp_ref[:, col_prev + lo:col_prev + hi] if b == 0
                else zq_ref[block_rows(b - 1), col_cur + lo:col_cur + hi])
        return jnp.concatenate([prev, zq_ref[block_rows(b), col_cur + lo:col_cur + hi]], axis=0)

    def band_scores(i):
        b, h = divmod(i, N_HEADS)
        return _band_scores(zq_ref[block_rows(b), h * HEAD_DIM:(h + 1) * HEAD_DIM], window(b, h // GROUP, 0, k_off))

    def band_finish(i, s):
        b, h = divmod(i, N_HEADS)
        p = _band_probs(s, sink_ref[h], upper, upper_first if b == 0 else upper)
        o = _band_values(p, window(b, h // GROUP, KV_WIDTH, v_off), ones)
        mix_ref[block_rows(b), h * HEAD_DIM:(h + 1) * HEAD_DIM] = o.astype(BF16)

    def cross_scores(h):
        lo, hi = h * MEM_HEAD_DIM, (h + 1) * MEM_HEAD_DIM
        return _cross_scores(zq_ref[:, qm_off + lo:qm_off + hi], mk_ref[pl.ds(h, N_MEM, stride=MEM_HEADS), :].astype(BF16))

    def cross_finish(h, s):
        lo, hi = h * MEM_HEAD_DIM, (h + 1) * MEM_HEAD_DIM
        mix_ref[:, TOKEN_WIDTH + lo:TOKEN_WIDTH + hi] = _cross_values(
            s, mv_ref[pl.ds(h, N_MEM, stride=MEM_HEADS), :].astype(BF16)).astype(BF16)

    _run_ahead(n_blocks * N_HEADS, SCORE_LOOKAHEAD, band_scores, band_finish)
    _run_ahead(MEM_HEADS, SCORE_LOOKAHEAD, cross_scores, cross_finish)


def _swa_prompt(z, mk, mv, sinks, weights, *, layer, batch, seq):
    rows, zc = z.shape
    tq = SWA_ROWS
    blocks_per_seq = seq // WINDOW
    steps_per_seq = seq // tq
    blocks_per_step = tq // WINDOW
    n_steps = batch * steps_per_seq
    kv_col_block = TOKEN_WIDTH // (2 * KV_WIDTH)
    bf16_rows = 2 * V7X_SUBLANES
    blocks = (_nbytes((tq, zc), BF16) + _nbytes((WINDOW, 2 * KV_WIDTH), BF16)
              + 2 * _nbytes((N_MEM, MEM_WIDTH), F32) + _nbytes((tq, D_MODEL), BF16))
    w_in_specs, w_out_specs, w_out_shapes, cast_steps = [], [], [], []
    for w in weights:
        r, c = w.shape[1:]
        n_cast = n_steps if r % (n_steps * bf16_rows) == 0 else n_steps // 2
        slab = r // n_cast
        assert slab * n_cast == r and slab % bf16_rows == 0
        blocks += _nbytes((slab, c), F32) + _nbytes((slab, c), BF16)
        w_in_specs.append(pl.BlockSpec((None, slab, c), lambda s, n=n_cast: (layer, jnp.minimum(s, n - 1), 0)))
        w_out_specs.append(pl.BlockSpec((slab, c), lambda s, n=n_cast: (jnp.minimum(s, n - 1), 0)))
        w_out_shapes.append(jax.ShapeDtypeStruct((r, c), BF16))
        cast_steps.append(n_cast)
    return pl.pallas_call(
        functools.partial(_swa_prompt_kernel, blocks_per_seq=blocks_per_seq, cast_steps=tuple(cast_steps)),
        grid=(n_steps,),
        in_specs=[
            pl.BlockSpec(memory_space=pltpu.SMEM),
            pl.BlockSpec((tq, zc), lambda s: (s, 0)),
            pl.BlockSpec((WINDOW, 2 * KV_WIDTH), lambda s: (jnp.maximum(s * blocks_per_step - 1, 0), kv_col_block)),
            pl.BlockSpec((None, N_MEM * MEM_HEADS, MEM_HEAD_DIM), lambda s: (layer, s // steps_per_seq, 0)),
            pl.BlockSpec((None, N_MEM * MEM_HEADS, MEM_HEAD_DIM), lambda s: (layer, s // steps_per_seq, 0)),
        ] + w_in_specs,
        out_specs=[pl.BlockSpec((tq, D_MODEL), lambda s: (s, 0))] + w_out_specs,
        out_shape=[jax.ShapeDtypeStruct((rows, D_MODEL), BF16)] + w_out_shapes,
        compiler_params=pltpu.CompilerParams(
            dimension_semantics=("arbitrary",),
            vmem_limit_bytes=_vmem_limit(blocks, SCORE_LOOKAHEAD * _nbytes((WINDOW, 2 * WINDOW), F32))),
        name="swa_prompt",
    )(sinks, z, z, mk, mv, *weights)


def _swa_sample_kernel(sink_ref, z_ref, ck_ref, cv_ref, mk_ref, mv_ref, wo_ref,
                       mix_ref, nk_ref, nv_ref, wob_ref, knew_ref, vnew_ref, mixf_ref, *, dec_seq):
    t = dec_seq
    wob_ref[...] = wo_ref[...].astype(BF16)
    k_off = TOKEN_WIDTH
    v_off = TOKEN_WIDTH + KV_WIDTH
    qm_off = TOKEN_WIDTH + 2 * KV_WIDTH
    rows = GROUP * t
    qi = lax.broadcasted_iota(jnp.int32, (rows, WINDOW), 0) % t
    col = lax.broadcasted_iota(jnp.int32, (rows, WINDOW), 1)
    upper = col > qi
    knew_ref[...] = jnp.zeros(knew_ref.shape, F32)
    vnew_ref[...] = jnp.zeros(vnew_ref.shape, F32)
    for n in range(SAMPLE_GROUP):
        r0, r1 = n * t, (n + 1) * t
        k_new = z_ref[r0:r1, k_off:k_off + KV_WIDTH].astype(F32)
        v_new = z_ref[r0:r1, v_off:v_off + KV_WIDTH].astype(F32)
        knew_ref[n, 0:t, :] = k_new
        vnew_ref[n, 0:t, :] = v_new
        nk_ref[n, 0:WINDOW - t, :] = ck_ref[n, t:WINDOW, :]
        nv_ref[n, 0:WINDOW - t, :] = cv_ref[n, t:WINDOW, :]
        nk_ref[n, WINDOW - t:WINDOW, :] = k_new
        nv_ref[n, WINDOW - t:WINDOW, :] = v_new

    def scores(i):
        n, kh = divmod(i, N_KV_HEADS)
        lo, hi = kh * HEAD_DIM, (kh + 1) * HEAD_DIM
        k = jnp.concatenate([ck_ref[n, :, lo:hi], knew_ref[n, :, lo:hi]], axis=0).astype(BF16)
        q = jnp.concatenate(
            [z_ref[n * t:(n + 1) * t, (kh * GROUP + g) * HEAD_DIM:(kh * GROUP + g + 1) * HEAD_DIM].astype(F32)
             for g in range(GROUP)], axis=0).astype(BF16)
        return _band_scores(q, k)

    def finish(i, s):
        n, kh = divmod(i, N_KV_HEADS)
        lo, hi = kh * HEAD_DIM, (kh + 1) * HEAD_DIM
        v = jnp.concatenate([cv_ref[n, :, lo:hi], vnew_ref[n, :, lo:hi]], axis=0).astype(BF16)
        sink = jnp.concatenate(
            [jnp.full((t, 1), sink_ref[kh * GROUP + g], F32) for g in range(GROUP)], axis=0)
        o = _band_values(_band_probs(s, sink, upper, upper), v, jnp.ones((2 * WINDOW, HEAD_DIM), BF16))
        for g in range(GROUP):
            h = kh * GROUP + g
            mixf_ref[n * t:(n + 1) * t, h * HEAD_DIM:(h + 1) * HEAD_DIM] = o[g * t:(g + 1) * t, :]

    _run_ahead(SAMPLE_GROUP * N_KV_HEADS, SAMPLE_LOOKAHEAD, scores, finish)
    _sample_cross_attention(z_ref, mk_ref, mv_ref, mixf_ref, qm_off, t)
    mix_ref[...] = mixf_ref[...].astype(BF16)


def _swa_sample(z, cache_k, cache_v, mem_k, mem_v, sinks, w_out, *, layer, dec_seq):
    rows, zc = z.shape
    dec_batch = cache_k.shape[0]
    g = SAMPLE_GROUP
    gr = g * dec_seq
    n_steps = dec_batch // g
    wk, wn = w_out.shape[1:]
    slab = wk // n_steps
    assert slab * n_steps == wk and slab % (2 * V7X_SUBLANES) == 0
    win = jax.ShapeDtypeStruct((dec_batch, WINDOW, KV_WIDTH), F32)
    blocks = (_nbytes((gr, zc), BF16) + 4 * _nbytes((g, WINDOW, KV_WIDTH), F32)
              + 2 * _nbytes((g, N_MEM, V7X_SUBLANES, MEM_HEAD_DIM), F32) + _nbytes((gr, D_MODEL), BF16)
              + _nbytes((slab, wn), F32) + _nbytes((slab, wn), BF16))
    scratch = 2 * _nbytes((g, WINDOW, KV_WIDTH), F32) + _nbytes((gr, D_MODEL), F32)
    return pl.pallas_call(
        functools.partial(_swa_sample_kernel, dec_seq=dec_seq),
        grid=(n_steps,),
        in_specs=[
            pl.BlockSpec(memory_space=pltpu.SMEM),
            pl.BlockSpec((gr, zc), lambda i: (i, 0)),
            pl.BlockSpec((g, WINDOW, KV_WIDTH), lambda i: (i, 0, 0)),
            pl.BlockSpec((g, WINDOW, KV_WIDTH), lambda i: (i, 0, 0)),
            pl.BlockSpec((None, g, N_MEM * MEM_HEADS, MEM_HEAD_DIM), lambda i: (layer, i, 0, 0)),
            pl.BlockSpec((None, g, N_MEM * MEM_HEADS, MEM_HEAD_DIM), lambda i: (layer, i, 0, 0)),
            pl.BlockSpec((None, slab, wn), lambda i: (0, i, 0)),
        ],
        out_specs=[
            pl.BlockSpec((gr, D_MODEL), lambda i: (i, 0)),
            pl.BlockSpec((g, WINDOW, KV_WIDTH), lambda i: (i, 0, 0)),
            pl.BlockSpec((g, WINDOW, KV_WIDTH), lambda i: (i, 0, 0)),
            pl.BlockSpec((slab, wn), lambda i: (i, 0)),
        ],
        out_shape=[jax.ShapeDtypeStruct((rows, D_MODEL), BF16), win, win,
                   jax.ShapeDtypeStruct((wk, wn), BF16)],
        scratch_shapes=[pltpu.VMEM((g, WINDOW, KV_WIDTH), F32), pltpu.VMEM((g, WINDOW, KV_WIDTH), F32),
                        pltpu.VMEM((gr, D_MODEL), F32)],
        compiler_params=pltpu.CompilerParams(
            dimension_semantics=("arbitrary",),
            vmem_limit_bytes=_vmem_limit(blocks, scratch)),
        name="swa_sample",
    )(sinks, z, cache_k, cache_v, mem_k, mem_v, w_out)


def kernel(x_prompt, x_sample, mem_prompt, state_conv, cache_win_k, cache_win_v, cache_mem_k, cache_mem_v,
           norm_mix, norm_mem, w_mem_kv, norm_ffn, w_gate, w_up, w_down,
           conv_w_in, conv_w, conv_w_out, attn_w_in, attn_sinks, attn_w_out, norm_final):
    batch, seq, d = x_prompt.shape
    dec_batch, dec_seq, _ = x_sample.shape
    depth = norm_mix.shape[0]
    d_ff = w_gate.shape[2]
    prompt_rows = batch * seq
    sample_rows = dec_batch * dec_seq
    assert d == D_MODEL and depth == 2 and seq % CONV_ROWS == 0 and seq % SWA_ROWS == 0 and SWA_ROWS % WINDOW == 0
    assert prompt_rows % PROMPT_TILE == 0 and dec_batch % SAMPLE_GROUP == 0
    assert dec_seq == V7X_SUBLANES and d_ff % FFN_COL_TILE == 0 and d % COL_TILE == 0
    assert d_ff % DOWN_K_TILE == 0 and d % DOWN_X_CHUNK == 0
    assert prompt_rows % FFN_ROW_TILE == 0 and prompt_rows % sample_rows == 0

    xp = x_prompt.reshape(prompt_rows, d)
    xs = x_sample.reshape(sample_rows, d)
    mem = mem_prompt.reshape(batch * N_MEM, d)
    mem_k_s = cache_mem_k.reshape(depth, dec_batch, N_MEM * MEM_HEADS, MEM_HEAD_DIM)
    mem_v_s = cache_mem_v.reshape(depth, dec_batch, N_MEM * MEM_HEADS, MEM_HEAD_DIM)
    g_mix = norm_mix.reshape(depth, 1, d)
    g_ffn = norm_ffn.reshape(depth, 1, d)

    mk, mv = _mem_kv(mem, norm_mem.reshape(depth, 1, d), w_mem_kv)

    hp, hs = _norm_rows(xp, xs, g_mix, layer=0)
    zp, zs, wd_bf = _matmul(hp, hs, conv_w_in, w_layer=0, side=w_down, side_layer=0)
    mix_s, conv_s, wo_bf = _conv_sample(zs, state_conv, mem_k_s, mem_v_s, conv_w, conv_w_out,
                                        layer=0, dec_seq=dec_seq)
    mix_p, conv_p = _conv_prompt(zp, mk, mv, conv_w, layer=0, batch=batch, seq=seq)
    xp, xs, hp, hs = _out_proj(mix_p, mix_s, wo_bf, xp, xs, g_ffn, layer=0)
    ap, as_ = _ffn_up(hp, hs, w_gate, w_up, layer=0)
    xp, xs, hp, hs = _ffn_down(ap, as_, wd_bf[None], xp, xs, g_mix, w_layer=0, g_layer=1, final=False)

    zp, zs = _matmul(hp, hs, attn_w_in, w_layer=0)
    sinks = attn_sinks[0]
    mix_s, win_k_s, win_v_s, wo_bf = _swa_sample(
        zs, cache_win_k[0].reshape(dec_batch, WINDOW, KV_WIDTH), cache_win_v[0].reshape(dec_batch, WINDOW, KV_WIDTH),
        mem_k_s, mem_v_s, sinks, attn_w_out, layer=1, dec_seq=dec_seq)
    mix_p, wd_bf = _swa_prompt(zp, mk, mv, sinks, [w_down], layer=1, batch=batch, seq=seq)
    xp, xs, hp, hs = _out_proj(mix_p, mix_s, wo_bf, xp, xs, g_ffn, layer=1)
    ap, as_ = _ffn_up(hp, hs, w_gate, w_up, layer=1)
    y_prompt, y_sample = _ffn_down(ap, as_, wd_bf[None], xp, xs, norm_final.reshape(1, 1, d),
                                   w_layer=0, g_layer=0, final=True)

    win_p = zp.reshape(batch, seq, -1)[:, seq - WINDOW:, TOKEN_WIDTH:TOKEN_WIDTH + 2 * KV_WIDTH].astype(F32)
    kv_shape = (1, -1, WINDOW, N_KV_HEADS, HEAD_DIM)
    mem_shape = (depth, batch, N_MEM, MEM_HEADS, MEM_HEAD_DIM)
    return (y_prompt.reshape(batch, seq, d),
            y_sample.reshape(dec_batch, dec_seq, d),
            conv_p[None],
            conv_s[None],
            win_p[..., :KV_WIDTH].reshape(kv_shape),
            win_p[..., KV_WIDTH:].reshape(kv_shape),
            win_k_s.reshape(kv_shape),
            win_v_s.reshape(kv_shape),
            mk.reshape(mem_shape),
            mv.reshape(mem_shape))
```
